```python
import jax, jax.numpy as jnp
from jax import lax
import numpy as np

D_MODEL = 2048
BATCH = 8
SEQ = 4096
DEPTH = 1

PLE_DIM = 256
CONV_WIDTH = D_MODEL // 2
CONV_K = 3
HG_DK = 128
HG_DV = 128
HG_HEADS = (D_MODEL // 2) // HG_DV
CHUNK = 32
D_FF = ((8 * D_MODEL // 3 + 127) // 128) * 128
LN_EPS = 1e-5
RMS_EPS = 1e-6
ALPHA = (2.0 * DEPTH) ** 0.25
BETA = (8.0 * DEPTH) ** -0.25
MIX_SIZES = (CONV_WIDTH,) * 3 + (HG_HEADS * HG_DK,) * 2 + (HG_HEADS * HG_DV,) * 2 + (D_MODEL,) * 2
MIX_COLS = sum(MIX_SIZES)

kernel_name = "hybrid_conv_hgrn2_macaron_deepnorm"


def _split_points():
    return [int(s) for s in np.cumsum(MIX_SIZES)[:-1]]


def layer_norm(x, g, b):
    xf = x.astype(jnp.float32)
    mu = xf.mean(-1, keepdims=True)
    var = jnp.square(xf - mu).mean(-1, keepdims=True)
    y = (xf - mu) * lax.rsqrt(var + LN_EPS) * g.astype(jnp.float32) + b.astype(jnp.float32)
    return y.astype(x.dtype)


def swiglu(x, w_in, w_out):
    a, u = jnp.split(x @ w_in, 2, axis=-1)
    return (jax.nn.silu(a) * u) @ w_out


def causal_dwconv(u, w):
    return lax.conv_general_dilated(
        u, w[:, None, :].astype(u.dtype), window_strides=(1,), padding=[(CONV_K - 1, 0)],
        dimension_numbers=("NWC", "WIO", "NWC"), feature_group_count=u.shape[-1])


def short_conv_mixer(b_gate, c_gate, h, w_conv):
    return b_gate * causal_dwconv(c_gate * h, w_conv)


def chunked_gla(q, k, v, logf):
    bsz, s, h, dk = q.shape
    dv = v.shape[-1]
    n = s // CHUNK

    def to_chunks(t):
        return t.reshape(bsz, n, CHUNK, h, t.shape[-1]).transpose(1, 0, 3, 2, 4)

    qc, kc, vc, gc = to_chunks(q), to_chunks(k), to_chunks(v), to_chunks(logf)
    causal = jnp.tril(jnp.ones((CHUNK, CHUNK), dtype=bool))[:, :, None]

    def step(state, inp):
        qb, kb, vb, gb = inp
        cum = jnp.cumsum(gb, axis=2)
        o_inter = jnp.einsum('bhck,bhkv->bhcv', qb * jnp.exp(cum), state)
        diff = cum[:, :, :, None, :] - cum[:, :, None, :, :]
        decay = jnp.exp(jnp.where(causal, diff, -jnp.inf))
        scores = jnp.einsum('bhtk,bhsk,bhtsk->bhts', qb, kb, decay)
        o_intra = jnp.einsum('bhts,bhsv->bhtv', scores, vb)
        last = cum[:, :, -1:, :]
        new_state = (jnp.exp(last[:, :, 0, :])[..., None] * state
                     + jnp.einsum('bhsk,bhsv->bhkv', kb * jnp.exp(last - cum), vb))
        return new_state, o_inter + o_intra

    s0 = jnp.zeros((bsz, h, dk, dv), jnp.float32)
    _, o = lax.scan(step, s0, (qc, kc, vc, gc))
    return o.transpose(1, 0, 3, 2, 4).reshape(bsz, s, h, dv)


def hgrn2_mixer(q_raw, f_raw, i_in, g_raw, lower_bound, norm_w):
    bsz, s, _ = q_raw.shape
    f32 = jnp.float32
    q = jax.nn.silu(q_raw.astype(f32)).reshape(bsz, s, HG_HEADS, HG_DK)
    lb = lower_bound.reshape(HG_HEADS, HG_DK)
    f = lb + (1.0 - lb) * jax.nn.sigmoid(f_raw.astype(f32)).reshape(bsz, s, HG_HEADS, HG_DK)
    k = 1.0 - f
    v = i_in.astype(f32).reshape(bsz, s, HG_HEADS, HG_DV)
    o = chunked_gla(q, k, v, jnp.log(f))
    o = o * lax.rsqrt(jnp.mean(jnp.square(o), -1, keepdims=True) + RMS_EPS) * norm_w.astype(f32)
    o = o * jax.nn.silu(g_raw.astype(f32).reshape(bsz, s, HG_HEADS, HG_DV))
    return o.reshape(bsz, s, HG_HEADS * HG_DV).astype(q_raw.dtype)


def _fwd_setup_inputs(seed: int = 0) -> dict:
    key = jax.random.key(seed)
    ks = jax.random.split(key, 18)
    nrm = lambda k, shape, scale: jax.random.normal(k, shape, jnp.float32) * scale
    L = DEPTH
    return {
        "x": nrm(ks[0], (BATCH, SEQ, D_MODEL), 1.0),
        "p": nrm(ks[1], (L, BATCH, SEQ, PLE_DIM), 1.0),
        "ln_g": 1.0 + nrm(ks[2], (L, 4, D_MODEL), 0.02),
        "ln_b": nrm(ks[3], (L, 4, D_MODEL), 0.02),
        "ffn1_w_in": nrm(ks[4], (L, D_MODEL, 2 * D_FF), D_MODEL ** -0.5),
        "ffn1_w_out": nrm(ks[5], (L, D_FF, D_MODEL), BETA * D_FF ** -0.5),
        "mix_w_in": nrm(ks[6], (L, D_MODEL, MIX_COLS), D_MODEL ** -0.5),
        "conv_w": nrm(ks[7], (L, CONV_K, CONV_WIDTH), CONV_K ** -0.5),
        "hg_lower_bound": nrm(ks[8], (L + 1, HG_HEADS * HG_DK), 0.1),
        "hg_norm_w": 1.0 + nrm(ks[9], (L, HG_DV), 0.02),
        "branch_w_conv": nrm(ks[10], (L, CONV_WIDTH, D_MODEL), BETA * CONV_WIDTH ** -0.5),
        "branch_w_hgrn": nrm(ks[11], (L, HG_HEADS * HG_DV, D_MODEL), BETA * (HG_HEADS * HG_DV) ** -0.5),
        "mix_w_out": nrm(ks[12], (L, D_MODEL, D_MODEL), BETA * D_MODEL ** -0.5),
        "ffn2_w_in": nrm(ks[13], (L, D_MODEL, 2 * D_FF), D_MODEL ** -0.5),
        "ffn2_w_out": nrm(ks[14], (L, D_FF, D_MODEL), BETA * D_FF ** -0.5),
        "ple_w_gate": nrm(ks[15], (L, D_MODEL, D_MODEL), D_MODEL ** -0.5),
        "ple_w_proj": nrm(ks[16], (L, PLE_DIM, D_MODEL), BETA * PLE_DIM ** -0.5),
    }


def _fwd_reference(x, p, ln_g, ln_b, ffn1_w_in, ffn1_w_out, mix_w_in, conv_w, hg_lower_bound,
              hg_norm_w, branch_w_conv, branch_w_hgrn, mix_w_out, ffn2_w_in, ffn2_w_out,
              ple_w_gate, ple_w_proj):
    lower_bounds = jnp.cumsum(jax.nn.softmax(hg_lower_bound.astype(jnp.float32), axis=0), axis=0)
    splits = _split_points()
    for i in range(DEPTH):
        x = layer_norm(ALPHA * x + 0.5 * swiglu(x, ffn1_w_in[i], ffn1_w_out[i]), ln_g[i, 0], ln_b[i, 0])
        z = x @ mix_w_in[i]
        b_gate, c_gate, h_conv, q_raw, f_raw, i_in, g_raw, gate_conv, gate_hgrn = jnp.split(z, splits, axis=-1)
        y_conv = short_conv_mixer(b_gate, c_gate, h_conv, conv_w[i])
        y_hgrn = hgrn2_mixer(q_raw, f_raw, i_in, g_raw, lower_bounds[i], hg_norm_w[i])
        merged = (jax.nn.sigmoid(gate_conv) * (y_conv @ branch_w_conv[i])
                  + jax.nn.sigmoid(gate_hgrn) * (y_hgrn @ branch_w_hgrn[i]))
        x = layer_norm(ALPHA * x + merged @ mix_w_out[i], ln_g[i, 1], ln_b[i, 1])
        x = layer_norm(ALPHA * x + 0.5 * swiglu(x, ffn2_w_in[i], ffn2_w_out[i]), ln_g[i, 2], ln_b[i, 2])
        ple = jax.nn.sigmoid(x @ ple_w_gate[i]) * (p[i] @ ple_w_proj[i])
        x = layer_norm(ALPHA * x + ple, ln_g[i, 3], ln_b[i, 3])
    return x


import jax as _jax
import jax.numpy as _jnp

TWIN_FORMAT = 'train_step'
FWD_PARAMS = ['x', 'p', 'ln_g', 'ln_b', 'ffn1_w_in', 'ffn1_w_out', 'mix_w_in', 'conv_w', 'hg_lower_bound', 'hg_norm_w', 'branch_w_conv', 'branch_w_hgrn', 'mix_w_out', 'ffn2_w_in', 'ffn2_w_out', 'ple_w_gate', 'ple_w_proj']
TWIN_WEIGHTS = ['ln_g', 'ln_b', 'ffn1_w_in', 'ffn1_w_out', 'mix_w_in', 'conv_w', 'hg_lower_bound', 'hg_norm_w', 'branch_w_conv', 'branch_w_hgrn', 'mix_w_out', 'ffn2_w_in', 'ffn2_w_out', 'ple_w_gate', 'ple_w_proj']
TWIN_DIFF_INPUT = 'x'
TWIN_INPUTS = ['x', 'p', 'ln_g', 'ln_b', 'ffn1_w_in', 'ffn1_w_out', 'mix_w_in', 'conv_w', 'hg_lower_bound', 'hg_norm_w', 'branch_w_conv', 'branch_w_hgrn', 'mix_w_out', 'ffn2_w_in', 'ffn2_w_out', 'ple_w_gate', 'ple_w_proj', 'loss_target', 'm_ln_g', 'm_ln_b', 'm_ffn1_w_in', 'm_ffn1_w_out', 'm_mix_w_in', 'm_conv_w', 'm_hg_lower_bound', 'm_hg_norm_w', 'm_branch_w_conv', 'm_branch_w_hgrn', 'm_mix_w_out', 'm_ffn2_w_in', 'm_ffn2_w_out', 'm_ple_w_gate', 'm_ple_w_proj', 'v_ln_g', 'v_ln_b', 'v_ffn1_w_in', 'v_ffn1_w_out', 'v_mix_w_in', 'v_conv_w', 'v_hg_lower_bound', 'v_hg_norm_w', 'v_branch_w_conv', 'v_branch_w_hgrn', 'v_mix_w_out', 'v_ffn2_w_in', 'v_ffn2_w_out', 'v_ple_w_gate', 'v_ple_w_proj']
TWIN_OUTPUTS = ['loss', 'grad_x', 'grad_ln_g', 'grad_ln_b', 'grad_ffn1_w_in', 'grad_ffn1_w_out', 'grad_mix_w_in', 'grad_conv_w', 'grad_hg_lower_bound', 'grad_hg_norm_w', 'grad_branch_w_conv', 'grad_branch_w_hgrn', 'grad_mix_w_out', 'grad_ffn2_w_in', 'grad_ffn2_w_out', 'grad_ple_w_gate', 'grad_ple_w_proj', 'delta_ln_g', 'delta_ln_b', 'delta_ffn1_w_in', 'delta_ffn1_w_out', 'delta_mix_w_in', 'delta_conv_w', 'delta_hg_lower_bound', 'delta_hg_norm_w', 'delta_branch_w_conv', 'delta_branch_w_hgrn', 'delta_mix_w_out', 'delta_ffn2_w_in', 'delta_ffn2_w_out', 'delta_ple_w_gate', 'delta_ple_w_proj', 'new_m_ln_g', 'new_m_ln_b', 'new_m_ffn1_w_in', 'new_m_ffn1_w_out', 'new_m_mix_w_in', 'new_m_conv_w', 'new_m_hg_lower_bound', 'new_m_hg_norm_w', 'new_m_branch_w_conv', 'new_m_branch_w_hgrn', 'new_m_mix_w_out', 'new_m_ffn2_w_in', 'new_m_ffn2_w_out', 'new_m_ple_w_gate', 'new_m_ple_w_proj', 'new_v_ln_g', 'new_v_ln_b', 'new_v_ffn1_w_in', 'new_v_ffn1_w_out', 'new_v_mix_w_in', 'new_v_conv_w', 'new_v_hg_lower_bound', 'new_v_hg_norm_w', 'new_v_branch_w_conv', 'new_v_branch_w_hgrn', 'new_v_mix_w_out', 'new_v_ffn2_w_in', 'new_v_ffn2_w_out', 'new_v_ple_w_gate', 'new_v_ple_w_proj']
TWIN_LEAF_KINDS = {'loss': 'loss', 'grad_x': 'grad_x', 'grad_ln_g': 'grad_w', 'grad_ln_b': 'grad_w', 'grad_ffn1_w_in': 'grad_w', 'grad_ffn1_w_out': 'grad_w', 'grad_mix_w_in': 'grad_w', 'grad_conv_w': 'grad_w', 'grad_hg_lower_bound': 'grad_w', 'grad_hg_norm_w': 'grad_w', 'grad_branch_w_conv': 'grad_w', 'grad_branch_w_hgrn': 'grad_w', 'grad_mix_w_out': 'grad_w', 'grad_ffn2_w_in': 'grad_w', 'grad_ffn2_w_out': 'grad_w', 'grad_ple_w_gate': 'grad_w', 'grad_ple_w_proj': 'grad_w', 'delta_ln_g': 'delta_w', 'delta_ln_b': 'delta_w', 'delta_ffn1_w_in': 'delta_w', 'delta_ffn1_w_out': 'delta_w', 'delta_mix_w_in': 'delta_w', 'delta_conv_w': 'delta_w', 'delta_hg_lower_bound': 'delta_w', 'delta_hg_norm_w': 'delta_w', 'delta_branch_w_conv': 'delta_w', 'delta_branch_w_hgrn': 'delta_w', 'delta_mix_w_out': 'delta_w', 'delta_ffn2_w_in': 'delta_w', 'delta_ffn2_w_out': 'delta_w', 'delta_ple_w_gate': 'delta_w', 'delta_ple_w_proj': 'delta_w', 'new_m_ln_g': 'new_m', 'new_m_ln_b': 'new_m', 'new_m_ffn1_w_in': 'new_m', 'new_m_ffn1_w_out': 'new_m', 'new_m_mix_w_in': 'new_m', 'new_m_conv_w': 'new_m', 'new_m_hg_lower_bound': 'new_m', 'new_m_hg_norm_w': 'new_m', 'new_m_branch_w_conv': 'new_m', 'new_m_branch_w_hgrn': 'new_m', 'new_m_mix_w_out': 'new_m', 'new_m_ffn2_w_in': 'new_m', 'new_m_ffn2_w_out': 'new_m', 'new_m_ple_w_gate': 'new_m', 'new_m_ple_w_proj': 'new_m', 'new_v_ln_g': 'new_v', 'new_v_ln_b': 'new_v', 'new_v_ffn1_w_in': 'new_v', 'new_v_ffn1_w_out': 'new_v', 'new_v_mix_w_in': 'new_v', 'new_v_conv_w': 'new_v', 'new_v_hg_lower_bound': 'new_v', 'new_v_hg_norm_w': 'new_v', 'new_v_branch_w_conv': 'new_v', 'new_v_branch_w_hgrn': 'new_v', 'new_v_mix_w_out': 'new_v', 'new_v_ffn2_w_in': 'new_v', 'new_v_ffn2_w_out': 'new_v', 'new_v_ple_w_gate': 'new_v', 'new_v_ple_w_proj': 'new_v'}


def _forward(args):
    return _fwd_reference(*[args[k] for k in FWD_PARAMS])


def _output_shape():
    def fwd():
        inp = _fwd_setup_inputs(0)
        return _fwd_reference(*[inp[k] for k in FWD_PARAMS])
    out = _jax.eval_shape(fwd)
    return out.shape, out.dtype

N_MICROBATCH = 1
ADAM_LR = 0.001
ADAM_B1 = 0.9
ADAM_B2 = 0.999
ADAM_EPS = 1e-08
ADAM_WD = 0.01
ADAM_STEP = 10
PER_EXAMPLE_BATCH_AXIS = {'x': 0, 'p': 1, 'loss_target': 0}
SHARED_INPUTS = []
_WEIGHT_DTYPES = {'ln_g': _jnp.float32, 'ln_b': _jnp.float32, 'ffn1_w_in': _jnp.float32, 'ffn1_w_out': _jnp.float32, 'mix_w_in': _jnp.float32, 'conv_w': _jnp.float32, 'hg_lower_bound': _jnp.float32, 'hg_norm_w': _jnp.float32, 'branch_w_conv': _jnp.float32, 'branch_w_hgrn': _jnp.float32, 'mix_w_out': _jnp.float32, 'ffn2_w_in': _jnp.float32, 'ffn2_w_out': _jnp.float32, 'ple_w_gate': _jnp.float32, 'ple_w_proj': _jnp.float32}
MOMENT_SCALE = {'ln_g': 8.018894e+00, 'ln_b': 3.549560e-01, 'ffn1_w_in': 8.194428e-03, 'ffn1_w_out': 2.223165e-02, 'mix_w_in': 1.163492e-02, 'conv_w': 2.013349e-02, 'hg_lower_bound': 1.078040e-03, 'hg_norm_w': 3.376940e-02, 'branch_w_conv': 2.307642e-02, 'branch_w_hgrn': 1.384419e-02, 'mix_w_out': 2.692717e-02, 'ffn2_w_in': 7.866352e-03, 'ffn2_w_out': 2.136035e-02, 'ple_w_gate': 9.050069e-03, 'ple_w_proj': 3.910846e-02}


def _to_microbatches(a, axis):
    t = _jnp.moveaxis(a, axis, 0)
    t = t.reshape((N_MICROBATCH, t.shape[0] // N_MICROBATCH) + t.shape[1:])
    return _jnp.moveaxis(t, 1, axis + 1)


def setup_inputs(seed: int = 0) -> dict:
    inp = _fwd_setup_inputs(seed)
    key = _jax.random.fold_in(_jax.random.key(seed), 7919)
    shape, _ = _output_shape()
    out = dict(inp)
    out["loss_target"] = _jax.random.normal(_jax.random.fold_in(key, 0), shape, _jnp.float32)
    for i, name in enumerate(TWIN_WEIGHTS):
        w = inp[name].astype(_jnp.float32)
        if MOMENT_SCALE is None:
            s = _jnp.sqrt(_jnp.mean(_jnp.square(w)) + 1e-30)
        else:
            s = MOMENT_SCALE[name]
        km, kv = _jax.random.split(_jax.random.fold_in(key, i + 1))
        out[name] = w
        out["m_" + name] = s * _jax.random.normal(km, w.shape, _jnp.float32)
        out["v_" + name] = (s * s) * _jax.random.uniform(kv, w.shape, _jnp.float32, 0.5, 1.5)
    if N_MICROBATCH > 1:
        for name, axis in PER_EXAMPLE_BATCH_AXIS.items():
            out[name] = _to_microbatches(out[name], axis)
    return {'x': out['x'], 'p': out['p'], 'ln_g': out['ln_g'], 'ln_b': out['ln_b'], 'ffn1_w_in': out['ffn1_w_in'], 'ffn1_w_out': out['ffn1_w_out'], 'mix_w_in': out['mix_w_in'], 'conv_w': out['conv_w'], 'hg_lower_bound': out['hg_lower_bound'], 'hg_norm_w': out['hg_norm_w'], 'branch_w_conv': out['branch_w_conv'], 'branch_w_hgrn': out['branch_w_hgrn'], 'mix_w_out': out['mix_w_out'], 'ffn2_w_in': out['ffn2_w_in'], 'ffn2_w_out': out['ffn2_w_out'], 'ple_w_gate': out['ple_w_gate'], 'ple_w_proj': out['ple_w_proj'], 'loss_target': out['loss_target'], 'm_ln_g': out['m_ln_g'], 'm_ln_b': out['m_ln_b'], 'm_ffn1_w_in': out['m_ffn1_w_in'], 'm_ffn1_w_out': out['m_ffn1_w_out'], 'm_mix_w_in': out['m_mix_w_in'], 'm_conv_w': out['m_conv_w'], 'm_hg_lower_bound': out['m_hg_lower_bound'], 'm_hg_norm_w': out['m_hg_norm_w'], 'm_branch_w_conv': out['m_branch_w_conv'], 'm_branch_w_hgrn': out['m_branch_w_hgrn'], 'm_mix_w_out': out['m_mix_w_out'], 'm_ffn2_w_in': out['m_ffn2_w_in'], 'm_ffn2_w_out': out['m_ffn2_w_out'], 'm_ple_w_gate': out['m_ple_w_gate'], 'm_ple_w_proj': out['m_ple_w_proj'], 'v_ln_g': out['v_ln_g'], 'v_ln_b': out['v_ln_b'], 'v_ffn1_w_in': out['v_ffn1_w_in'], 'v_ffn1_w_out': out['v_ffn1_w_out'], 'v_mix_w_in': out['v_mix_w_in'], 'v_conv_w': out['v_conv_w'], 'v_hg_lower_bound': out['v_hg_lower_bound'], 'v_hg_norm_w': out['v_hg_norm_w'], 'v_branch_w_conv': out['v_branch_w_conv'], 'v_branch_w_hgrn': out['v_branch_w_hgrn'], 'v_mix_w_out': out['v_mix_w_out'], 'v_ffn2_w_in': out['v_ffn2_w_in'], 'v_ffn2_w_out': out['v_ffn2_w_out'], 'v_ple_w_gate': out['v_ple_w_gate'], 'v_ple_w_proj': out['v_ple_w_proj']}


def _loss(weights, diff, rest, loss_target):
    with _jax.named_scope("forward"):
        args = {**rest, TWIN_DIFF_INPUT: diff, **{k: w.astype(_WEIGHT_DTYPES[k]) for k, w in weights.items()}}
        y = _forward(args)
    with _jax.named_scope("loss_head"):
        err = _jnp.square(y.astype(_jnp.float32) - loss_target)
        return 0.5 * _jnp.sum(_jnp.mean(err, axis=-1)) if err.ndim else 0.5 * err


def _adamw(w, g, m, v):
    m = ADAM_B1 * m + (1.0 - ADAM_B1) * g
    v = ADAM_B2 * v + (1.0 - ADAM_B2) * _jnp.square(g)
    m_hat = m / (1.0 - ADAM_B1 ** ADAM_STEP)
    v_hat = v / (1.0 - ADAM_B2 ** ADAM_STEP)
    delta = -ADAM_LR * (m_hat / (_jnp.sqrt(v_hat) + ADAM_EPS) + ADAM_WD * w)
    return delta, m, v


def reference(x, p, ln_g, ln_b, ffn1_w_in, ffn1_w_out, mix_w_in, conv_w, hg_lower_bound, hg_norm_w, branch_w_conv, branch_w_hgrn, mix_w_out, ffn2_w_in, ffn2_w_out, ple_w_gate, ple_w_proj, loss_target, m_ln_g, m_ln_b, m_ffn1_w_in, m_ffn1_w_out, m_mix_w_in, m_conv_w, m_hg_lower_bound, m_hg_norm_w, m_branch_w_conv, m_branch_w_hgrn, m_mix_w_out, m_ffn2_w_in, m_ffn2_w_out, m_ple_w_gate, m_ple_w_proj, v_ln_g, v_ln_b, v_ffn1_w_in, v_ffn1_w_out, v_mix_w_in, v_conv_w, v_hg_lower_bound, v_hg_norm_w, v_branch_w_conv, v_branch_w_hgrn, v_mix_w_out, v_ffn2_w_in, v_ffn2_w_out, v_ple_w_gate, v_ple_w_proj):
    given = dict(x=x, p=p, ln_g=ln_g, ln_b=ln_b, ffn1_w_in=ffn1_w_in, ffn1_w_out=ffn1_w_out, mix_w_in=mix_w_in, conv_w=conv_w, hg_lower_bound=hg_lower_bound, hg_norm_w=hg_norm_w, branch_w_conv=branch_w_conv, branch_w_hgrn=branch_w_hgrn, mix_w_out=mix_w_out, ffn2_w_in=ffn2_w_in, ffn2_w_out=ffn2_w_out, ple_w_gate=ple_w_gate, ple_w_proj=ple_w_proj, loss_target=loss_target, m_ln_g=m_ln_g, m_ln_b=m_ln_b, m_ffn1_w_in=m_ffn1_w_in, m_ffn1_w_out=m_ffn1_w_out, m_mix_w_in=m_mix_w_in, m_conv_w=m_conv_w, m_hg_lower_bound=m_hg_lower_bound, m_hg_norm_w=m_hg_norm_w, m_branch_w_conv=m_branch_w_conv, m_branch_w_hgrn=m_branch_w_hgrn, m_mix_w_out=m_mix_w_out, m_ffn2_w_in=m_ffn2_w_in, m_ffn2_w_out=m_ffn2_w_out, m_ple_w_gate=m_ple_w_gate, m_ple_w_proj=m_ple_w_proj, v_ln_g=v_ln_g, v_ln_b=v_ln_b, v_ffn1_w_in=v_ffn1_w_in, v_ffn1_w_out=v_ffn1_w_out, v_mix_w_in=v_mix_w_in, v_conv_w=v_conv_w, v_hg_lower_bound=v_hg_lower_bound, v_hg_norm_w=v_hg_norm_w, v_branch_w_conv=v_branch_w_conv, v_branch_w_hgrn=v_branch_w_hgrn, v_mix_w_out=v_mix_w_out, v_ffn2_w_in=v_ffn2_w_in, v_ffn2_w_out=v_ffn2_w_out, v_ple_w_gate=v_ple_w_gate, v_ple_w_proj=v_ple_w_proj)
    weights = {n: given[n] for n in TWIN_WEIGHTS}
    shared = {n: given[n] for n in SHARED_INPUTS}
    per_example = {n: given[n] for n in ['x', 'p']}
    grad_fn = _jax.value_and_grad(_loss, argnums=(0, 1))

    def one_microbatch(ex, loss_target):
        ex = dict(ex)
        diff = ex.pop(TWIN_DIFF_INPUT)
        return grad_fn(weights, diff, {**shared, **ex}, loss_target)

    if N_MICROBATCH == 1:
        loss, (grad_w, grad_x) = one_microbatch(per_example, given["loss_target"])
    else:
        def body(carry, xs):
            loss_sum, grad_sum = carry
            l_k, (gw_k, gx_k) = one_microbatch(xs[0], xs[1])
            with _jax.named_scope("update"):
                return (loss_sum + l_k, _jax.tree.map(_jnp.add, grad_sum, gw_k)), gx_k

        init = (_jnp.zeros((), _jnp.float32), _jax.tree.map(_jnp.zeros_like, weights))
        (loss, grad_w), grad_x = _jax.lax.scan(body, init, (per_example, given["loss_target"]))
    with _jax.named_scope("update"):
        delta_w, new_m, new_v = {}, {}, {}
        for n in TWIN_WEIGHTS:
            delta_w[n], new_m[n], new_v[n] = _adamw(weights[n], grad_w[n], given["m_" + n], given["v_" + n])
    return (loss, grad_x, *[grad_w[n] for n in TWIN_WEIGHTS], *[delta_w[n] for n in TWIN_WEIGHTS],
            *[new_m[n] for n in TWIN_WEIGHTS], *[new_v[n] for n in TWIN_WEIGHTS])
```

```python
import functools

import jax
import jax.numpy as jnp
from jax import lax
from jax.experimental import pallas as pl
from jax.experimental.pallas import tpu as pltpu

F32 = jnp.float32
BF16 = jnp.bfloat16
MESH = pl.DeviceIdType.MESH
ANY = pl.BlockSpec(memory_space=pl.ANY)
VMEM_SPEC = pl.BlockSpec(memory_space=pltpu.VMEM)
SDS = jax.ShapeDtypeStruct

DEPTH = 1
ALPHA = (2.0 * DEPTH) ** 0.25
LN_EPS = 1e-5
RMS_EPS = 1e-6
CHUNK = 32
HEAD = 128
ADAM_LR, ADAM_B1, ADAM_B2, ADAM_EPS, ADAM_WD, ADAM_STEP = 0.001, 0.9, 0.999, 1e-08, 0.01, 10

LANES = 128
N_CHIPS = 4
N_DEV = 8
VMEM_LIMIT = 52 * 1024 * 1024


def _cparams(*sem):
    if sem:
        return pltpu.CompilerParams(dimension_semantics=sem, vmem_limit_bytes=VMEM_LIMIT)
    return pltpu.CompilerParams(vmem_limit_bytes=VMEM_LIMIT)


def _tile(n, target, mult):
    best = None
    for t in range(mult, min(n, target) + 1, mult):
        if n % t == 0:
            best = t
    return best if best is not None else n


def _sigmoid(x):
    return 1.0 / (1.0 + jnp.exp(-x))


def _mm(a, b, *, name, ta=False, tb=False, b_blocked=False, out_blocked=0, out_dtype=F32,
        tm=512, tn=1408, tk=2048):
    if ta:
        kd, m = a.shape
    else:
        m, kd = a.shape
    if b_blocked and not tb:
        g, kb, nb = b.shape
        assert kb == kd
        n = g * nb
        tn = _tile(nb, tn, LANES)
        tk = _tile(kd, tk, LANES)
        per_n = nb // tn
        b_spec = pl.BlockSpec((None, tk, tn), lambda i, j, k: (j // per_n, k, j % per_n))
    elif b_blocked and tb:
        g, n, kb = b.shape
        assert g * kb == kd
        tn = _tile(n, tn, LANES)
        tk = _tile(kb, tk, LANES)
        per_k = kb // tk
        b_spec = pl.BlockSpec((None, tn, tk), lambda i, j, k: (k // per_k, j, k % per_k))
    elif tb:
        n, kb = b.shape
        assert kb == kd
        tn = _tile(n, tn, LANES)
        tk = _tile(kd, tk, LANES)
        b_spec = pl.BlockSpec((tn, tk), lambda i, j, k: (j, k))
    else:
        kb, n = b.shape
        assert kb == kd
        tn = _tile(n // out_blocked if out_blocked else n, tn, LANES)
        per_o = (n // out_blocked) // tn if out_blocked else None
        tk = _tile(kd, tk, LANES)
        b_spec = pl.BlockSpec((tk, tn), lambda i, j, k: (k, j))
    tm = _tile(m, tm, LANES if ta else 8)
    if ta:
        a_spec = pl.BlockSpec((tk, tm), lambda i, j, k: (k, i))
    else:
        a_spec = pl.BlockSpec((tm, tk), lambda i, j, k: (i, k))
    if out_blocked:
        assert not b_blocked and not tb
        o_spec = pl.BlockSpec((None, tm, tn), lambda i, j, k: (j // per_o, i, j % per_o))
        o_shape = SDS((out_blocked, m, n // out_blocked), out_dtype)
    else:
        o_spec = pl.BlockSpec((tm, tn), lambda i, j, k: (i, j))
        o_shape = SDS((m, n), out_dtype)
    nk = kd // tk
    dn = (((0 if ta else 1,), (1 if tb else 0,)), ((), ()))

    def body(a_ref, b_ref, o_ref, acc_ref):
        part = lax.dot_general(a_ref[...].astype(BF16), b_ref[...].astype(BF16), dn, preferred_element_type=F32)
        if nk == 1:
            o_ref[...] = part.astype(o_ref.dtype)
        else:
            k = pl.program_id(2)

            @pl.when(k == 0)
            def _():
                acc_ref[...] = part

            @pl.when(k > 0)
            def _():
                acc_ref[...] += part

            @pl.when(k == nk - 1)
            def _():
                o_ref[...] = acc_ref[...].astype(o_ref.dtype)

    return pl.pallas_call(
        body, name=name, grid=(m // tm, n // tn, nk), in_specs=[a_spec, b_spec], out_specs=o_spec, out_shape=o_shape,
        scratch_shapes=[pltpu.VMEM((tm, tn), F32)],
        compiler_params=_cparams("parallel", "parallel", "arbitrary"),
    )(a, b)


def _swiglu_fwd(z, name):
    t, n = z.shape
    n2 = n // 2
    tr = _tile(t, 128, 16)

    def body(a_ref, u_ref, o_ref):
        a = a_ref[...].astype(F32)
        o_ref[...] = (a * _sigmoid(a) * u_ref[...].astype(F32)).astype(o_ref.dtype)

    return pl.pallas_call(
        body, name=name, grid=(t // tr,),
        in_specs=[pl.BlockSpec((tr, n2), lambda i: (i, 0)), pl.BlockSpec((tr, n2), lambda i: (i, 1))],
        out_specs=pl.BlockSpec((tr, n2), lambda i: (i, 0)), out_shape=SDS((t, n2), BF16),
        compiler_params=_cparams("parallel"),
    )(z, z)


def _swiglu_bwd(dh, z, name):
    t, n = z.shape
    n2 = n // 2
    tr = _tile(t, 128, 16)

    def body(dh_ref, a_ref, u_ref, o_ref):
        j = pl.program_id(1)
        a = a_ref[...].astype(F32)
        dh_ = dh_ref[...].astype(F32)
        s = _sigmoid(a)

        @pl.when(j == 0)
        def _():
            o_ref[...] = (dh_ * u_ref[...].astype(F32) * (s * (1.0 + a * (1.0 - s)))).astype(o_ref.dtype)

        @pl.when(j == 1)
        def _():
            o_ref[...] = (dh_ * a * s).astype(o_ref.dtype)

    return pl.pallas_call(
        body, name=name, grid=(t // tr, 2),
        in_specs=[pl.BlockSpec((tr, n2), lambda i, j: (i, 0)), pl.BlockSpec((tr, n2), lambda i, j: (i, 0)),
                  pl.BlockSpec((tr, n2), lambda i, j: (i, 1))],
        out_specs=pl.BlockSpec((tr, n2), lambda i, j: (i, j)), out_shape=SDS((t, n), BF16),
        compiler_params=_cparams("parallel", "arbitrary"),
    )(dh, z, z)


def _ln_stats(r):
    mu = jnp.mean(r, axis=-1, keepdims=True)
    xc = r - mu
    var = jnp.mean(xc * xc, axis=-1, keepdims=True)
    return xc * lax.rsqrt(var + LN_EPS)


def _ln_fwd(xp, y, g, b, scale, name):
    t, d = xp.shape
    tr = _tile(t, 256, 16)

    def body(xp_ref, y_ref, g_ref, b_ref, r_ref, x_ref, xb_ref):
        r = ALPHA * xp_ref[...] + scale * y_ref[...]
        x = _ln_stats(r) * g_ref[...] + b_ref[...]
        r_ref[...] = r
        x_ref[...] = x
        xb_ref[...] = x.astype(BF16)

    row = pl.BlockSpec((tr, d), lambda i: (i, 0))
    vec = pl.BlockSpec((1, d), lambda i: (0, 0))
    return pl.pallas_call(
        body, name=name, grid=(t // tr,), in_specs=[row, row, vec, vec], out_specs=[row, row, row],
        out_shape=[SDS((t, d), F32), SDS((t, d), F32), SDS((t, d), BF16)], compiler_params=_cparams("parallel"),
    )(xp, y, g, b)


def _ln_bwd(dra, dxm, r, g, scale, name):
    t, d = r.shape
    tr = _tile(t, 256, 16)

    def body(dra_ref, dxm_ref, r_ref, g_ref, dr_ref, dyb_ref, dg_ref, db_ref):
        i = pl.program_id(0)
        dx = ALPHA * dra_ref[...] + dxm_ref[...]
        rr = r_ref[...]
        mu = jnp.mean(rr, axis=-1, keepdims=True)
        xc = rr - mu
        rstd = lax.rsqrt(jnp.mean(xc * xc, axis=-1, keepdims=True) + LN_EPS)
        xh = xc * rstd
        dxh = dx * g_ref[...]
        dr = rstd * (dxh - jnp.mean(dxh, axis=-1, keepdims=True) - xh * jnp.mean(dxh * xh, axis=-1, keepdims=True))
        dr_ref[...] = dr
        dyb_ref[...] = (scale * dr).astype(BF16)
        dg = jnp.sum(dx * xh, axis=0, keepdims=True)
        db = jnp.sum(dx, axis=0, keepdims=True)

        @pl.when(i == 0)
        def _():
            dg_ref[...] = dg
            db_ref[...] = db

        @pl.when(i > 0)
        def _():
            dg_ref[...] += dg
            db_ref[...] += db

    row = pl.BlockSpec((tr, d), lambda i: (i, 0))
    vec = pl.BlockSpec((1, d), lambda i: (0, 0))
    return pl.pallas_call(
        body, name=name, grid=(t // tr,), in_specs=[row, row, row, vec], out_specs=[row, row, vec, vec],
        out_shape=[SDS((t, d), F32), SDS((t, d), BF16), SDS((1, d), F32), SDS((1, d), F32)],
        compiler_params=_cparams("arbitrary"),
    )(dra, dxm, r, g)


def _tail(x3, gp, pp, g, b, target, name):
    t, d = x3.shape
    tr = _tile(t, 256, 16)

    def body(x3_ref, gp_ref, pp_ref, g_ref, b_ref, tg_ref, dr_ref, dgp_ref, dpp_ref, dg_ref, db_ref, sq_ref):
        i = pl.program_id(0)
        gate = _sigmoid(gp_ref[...])
        pp_ = pp_ref[...]
        r = ALPHA * x3_ref[...] + gate * pp_
        mu = jnp.mean(r, axis=-1, keepdims=True)
        xc = r - mu
        rstd = lax.rsqrt(jnp.mean(xc * xc, axis=-1, keepdims=True) + LN_EPS)
        xh = xc * rstd
        err = xh * g_ref[...] + b_ref[...] - tg_ref[...]
        dx = err * (1.0 / d)
        dxh = dx * g_ref[...]
        dr = rstd * (dxh - jnp.mean(dxh, axis=-1, keepdims=True) - xh * jnp.mean(dxh * xh, axis=-1, keepdims=True))
        dr_ref[...] = dr
        dgp_ref[...] = (dr * pp_ * gate * (1.0 - gate)).astype(BF16)
        dpp_ref[...] = (dr * gate).astype(BF16)
        dg = jnp.sum(dx * xh, axis=0, keepdims=True)
        db = jnp.sum(dx, axis=0, keepdims=True)
        sq = jnp.sum(err * err, axis=0, keepdims=True)

        @pl.when(i == 0)
        def _():
            dg_ref[...] = dg
            db_ref[...] = db
            sq_ref[...] = sq

        @pl.when(i > 0)
        def _():
            dg_ref[...] += dg
            db_ref[...] += db
            sq_ref[...] += sq

    row = pl.BlockSpec((tr, d), lambda i: (i, 0))
    vec = pl.BlockSpec((1, d), lambda i: (0, 0))
    return pl.pallas_call(
        body, name=name, grid=(t // tr,), in_specs=[row, row, row, vec, vec, row],
        out_specs=[row, row, row, vec, vec, vec],
        out_shape=[SDS((t, d), F32), SDS((t, d), BF16), SDS((t, d), BF16), SDS((1, d), F32), SDS((1, d), F32),
                   SDS((1, d), F32)],
        compiler_params=_cparams("arbitrary"),
    )(x3, gp, pp, g, b, target)


def _residual_out(dra, dxm, name):
    t, d = dra.shape
    tr = _tile(t, 256, 8)

    def body(a_ref, b_ref, o_ref):
        o_ref[...] = ALPHA * a_ref[...] + b_ref[...]

    row = pl.BlockSpec((tr, d), lambda i: (i, 0))
    return pl.pallas_call(body, name=name, grid=(t // tr,), in_specs=[row, row], out_specs=row,
                          out_shape=SDS((t, d), F32), compiler_params=_cparams("parallel"))(dra, dxm)


def _merge_fwd(z, ma, mb, w, name):
    t = z.shape[0]
    tr = _tile(t, 256, 16)

    def body(gc_ref, gh_ref, ma_ref, mb_ref, o_ref):
        o_ref[...] = (_sigmoid(gc_ref[...]) * ma_ref[...] + _sigmoid(gh_ref[...]) * mb_ref[...]).astype(BF16)

    half = pl.BlockSpec((tr, w), lambda i, j: (i, j))
    return pl.pallas_call(
        body, name=name, grid=(t // tr, 2),
        in_specs=[pl.BlockSpec((tr, w), lambda i, j: (i, 7 + j)), pl.BlockSpec((tr, w), lambda i, j: (i, 9 + j)), half, half],
        out_specs=half, out_shape=SDS((t, 2 * w), BF16), compiler_params=_cparams("parallel", "parallel"),
    )(z, z, ma, mb)


def _merge_bwd(dmer, z, ma, mb, w, name):
    t = z.shape[0]
    tr = _tile(t, 256, 16)

    def body(d_ref, gc_ref, gh_ref, ma_ref, mb_ref, dma_ref, dmb_ref, dgc_ref, dgh_ref):
        dm = d_ref[...]
        sc = _sigmoid(gc_ref[...])
        sh = _sigmoid(gh_ref[...])
        dma_ref[...] = (dm * sc).astype(BF16)
        dmb_ref[...] = (dm * sh).astype(BF16)
        dgc_ref[...] = (dm * ma_ref[...] * sc * (1.0 - sc)).astype(BF16)
        dgh_ref[...] = (dm * mb_ref[...] * sh * (1.0 - sh)).astype(BF16)

    half = pl.BlockSpec((tr, w), lambda i, j: (i, j))
    return pl.pallas_call(
        body, name=name, grid=(t // tr, 2),
        in_specs=[half, pl.BlockSpec((tr, w), lambda i, j: (i, 7 + j)), pl.BlockSpec((tr, w), lambda i, j: (i, 9 + j)), half, half],
        out_specs=[half] * 4, out_shape=[SDS((t, 2 * w), BF16)] * 4, compiler_params=_cparams("parallel", "parallel"),
    )(dmer, z, z, ma, mb)


def _shift_down(x, s, row):
    return jnp.where(row >= s, pltpu.roll(x, s, axis=0), 0.0)


def _shift_up(x, s, row, t):
    return jnp.where(row < t - s, pltpu.roll(x, t - s, axis=0), 0.0)


def _conv_fwd(z, cw, w, name):
    t = z.shape[0]
    tc = LANES
    nb = w // tc

    def body(b_ref, c_ref, h_ref, w_ref, o_ref):
        u = c_ref[...] * h_ref[...]
        row = lax.broadcasted_iota(jnp.int32, u.shape, 0)
        cw_ = w_ref[...]
        conv = cw_[2:3, :] * u + cw_[1:2, :] * _shift_down(u, 1, row) + cw_[0:1, :] * _shift_down(u, 2, row)
        o_ref[...] = (b_ref[...] * conv).astype(BF16)

    col = lambda off: pl.BlockSpec((t, tc), lambda j: (0, off * nb + j))
    return pl.pallas_call(
        body, name=name, grid=(nb,), in_specs=[col(0), col(1), col(2), pl.BlockSpec((3, tc), lambda j: (0, j))],
        out_specs=pl.BlockSpec((t, tc), lambda j: (0, j)), out_shape=SDS((t, w), BF16), compiler_params=_cparams("parallel"),
    )(z, z, z, cw)


def _conv_bwd(dy, z, cw, w, name):
    t = z.shape[0]
    tc = LANES
    nb = w // tc

    def body(dy_ref, b_ref, c_ref, h_ref, w_ref, db_ref, dc_ref, dh_ref, dw_ref):
        c_, h_ = c_ref[...], h_ref[...]
        u = c_ * h_
        row = lax.broadcasted_iota(jnp.int32, u.shape, 0)
        cw_ = w_ref[...]
        u1 = _shift_down(u, 1, row)
        u2 = _shift_down(u, 2, row)
        dy_ = dy_ref[...]
        db_ref[...] = (dy_ * (cw_[2:3, :] * u + cw_[1:2, :] * u1 + cw_[0:1, :] * u2)).astype(BF16)
        dconv = dy_ * b_ref[...]
        du = cw_[2:3, :] * dconv + cw_[1:2, :] * _shift_up(dconv, 1, row, t) + cw_[0:1, :] * _shift_up(dconv, 2, row, t)
        dc_ref[...] = (du * h_).astype(BF16)
        dh_ref[...] = (du * c_).astype(BF16)
        dw_ref[0:1, :] = jnp.sum(dconv * u2, axis=0, keepdims=True)
        dw_ref[1:2, :] = jnp.sum(dconv * u1, axis=0, keepdims=True)
        dw_ref[2:3, :] = jnp.sum(dconv * u, axis=0, keepdims=True)

    col = lambda off: pl.BlockSpec((t, tc), lambda j: (0, off * nb + j))
    own = pl.BlockSpec((t, tc), lambda j: (0, j))
    wsp = pl.BlockSpec((3, tc), lambda j: (0, j))
    return pl.pallas_call(
        body, name=name, grid=(nb,), in_specs=[own, col(0), col(1), col(2), wsp], out_specs=[own, own, own, wsp],
        out_shape=[SDS((t, w), BF16)] * 3 + [SDS((3, w), F32)], compiler_params=_cparams("parallel"),
    )(dy, z, z, z, cw)


def _lower_bound(hg):
    mx = jnp.max(hg, axis=0, keepdims=True)
    e = jnp.exp(hg - mx)
    inv = 1.0 / jnp.sum(e, axis=0, keepdims=True)
    return e[0:1, :] * inv, e[1:2, :] * inv


def _chunk_cumsum(x, row):
    s = 1
    while s < CHUNK:
        x = x + jnp.where(row % CHUNK >= s, pltpu.roll(x, s, axis=0), 0.0)
        s *= 2
    return x


def _dot_nt(a, b):
    return lax.dot_general(a.astype(BF16), b.astype(BF16), (((1,), (1,)), ((), ())), preferred_element_type=F32)


def _dot_tn(a, b):
    return lax.dot_general(a.astype(BF16), b.astype(BF16), (((0,), (0,)), ((), ())), preferred_element_type=F32)


def _dot_nn(a, b):
    return jnp.dot(a.astype(BF16), b.astype(BF16), preferred_element_type=F32)


def _tril(x):
    r = lax.broadcasted_iota(jnp.int32, x.shape, 0)
    c = lax.broadcasted_iota(jnp.int32, x.shape, 1)
    return jnp.where(r >= c, x, 0.0)


def _hgrn_chunk_inputs(q_ref, f_ref, cum_ref, lb, rows):
    qr = q_ref[rows, :]
    q = qr * _sigmoid(qr)
    f = lb + (1.0 - lb) * _sigmoid(f_ref[rows, :])
    return q, 1.0 - f, cum_ref[rows, :]


def _hgrn_fwd(z, hg, nw, w, name):
    t = z.shape[0]
    nh = w // HEAD
    nc = t // CHUNK

    def body(q_ref, f_ref, i_ref, g_ref, hg_ref, nw_ref, y_ref, o_ref, st_ref, cum_ref, s_ref):
        lb, _ = _lower_bound(hg_ref[...])
        row = lax.broadcasted_iota(jnp.int32, (t, HEAD), 0)
        cum_ref[...] = _chunk_cumsum(jnp.log(lb + (1.0 - lb) * _sigmoid(f_ref[...])), row)
        s_ref[...] = jnp.zeros_like(s_ref)

        def step(c, carry):
            rows = pl.ds(pl.multiple_of(c * CHUNK, CHUNK), CHUNK)
            q, k, cum = _hgrn_chunk_inputs(q_ref, f_ref, cum_ref, lb, rows)
            v = i_ref[rows, :]
            last = cum[CHUNK - 1:CHUNK, :]
            qe = q * jnp.exp(cum)
            st = s_ref[...]
            st_ref[c] = st.astype(BF16)
            o = _dot_nt(qe, st) + _dot_nn(_tril(_dot_nt(qe, k * jnp.exp(-cum))), v)
            o_ref[rows, :] = o
            s_ref[...] = st * jnp.exp(last) + _dot_tn(v, k * jnp.exp(last - cum))
            return carry

        lax.fori_loop(0, nc, step, 0)
        o = o_ref[...]
        n = o * lax.rsqrt(jnp.mean(o * o, axis=-1, keepdims=True) + RMS_EPS)
        gr = g_ref[...]
        y_ref[...] = (n * nw_ref[...] * gr * _sigmoid(gr)).astype(BF16)

    col = lambda off: pl.BlockSpec((t, HEAD), lambda h: (0, off * nh + h))
    own = pl.BlockSpec((t, HEAD), lambda h: (0, h))
    return pl.pallas_call(
        body, name=name, grid=(nh,),
        in_specs=[col(3), col(4), col(5), col(6), pl.BlockSpec((2, HEAD), lambda h: (0, h)),
                  pl.BlockSpec((1, HEAD), lambda h: (0, 0))],
        out_specs=[own, own, pl.BlockSpec((None, nc, HEAD, HEAD), lambda h: (h, 0, 0, 0))],
        out_shape=[SDS((t, w), BF16), SDS((t, w), F32), SDS((nh, nc, HEAD, HEAD), BF16)],
        scratch_shapes=[pltpu.VMEM((t, HEAD), F32), pltpu.VMEM((HEAD, HEAD), F32)],
        compiler_params=_cparams("parallel"),
    )(z, z, z, z, hg, nw)


def _hgrn_bwd(dy, z, o, states, hg, nw, w, name):
    t = z.shape[0]
    nh = w // HEAD
    nc = t // CHUNK

    def body(dy_ref, q_ref, f_ref, i_ref, g_ref, o_ref, st_ref, hg_ref, nw_ref,
             dq_ref, df_ref, di_ref, dg_ref, dhg_ref, dnw_ref, cum_ref, do_ref, ds_ref):
        lb, s1 = _lower_bound(hg_ref[...])
        row = lax.broadcasted_iota(jnp.int32, (t, HEAD), 0)
        crow = lax.broadcasted_iota(jnp.int32, (CHUNK, HEAD), 0)
        cum_ref[...] = _chunk_cumsum(jnp.log(lb + (1.0 - lb) * _sigmoid(f_ref[...])), row)

        o_ = o_ref[...]
        rstd = lax.rsqrt(jnp.mean(o_ * o_, axis=-1, keepdims=True) + RMS_EPS)
        n = o_ * rstd
        gr = g_ref[...]
        sg = _sigmoid(gr)
        dy_ = dy_ref[...]
        dg_ref[...] = (dy_ * n * nw_ref[...] * (sg * (1.0 + gr * (1.0 - sg)))).astype(BF16)
        dsil = dy_ * gr * sg
        dnw_ref[...] = jnp.sum(dsil * n, axis=0, keepdims=True)
        dn = dsil * nw_ref[...]
        do_ref[...] = rstd * (dn - n * jnp.mean(dn * n, axis=-1, keepdims=True))

        ds_ref[...] = jnp.zeros_like(ds_ref)

        def step(cc, dlb):
            c = nc - 1 - cc
            rows = pl.ds(pl.multiple_of(c * CHUNK, CHUNK), CHUNK)
            qr = q_ref[rows, :]
            sq = _sigmoid(qr)
            q = qr * sq
            sf = _sigmoid(f_ref[rows, :])
            f = lb + (1.0 - lb) * sf
            k = 1.0 - f
            cum = cum_ref[rows, :]
            v = i_ref[rows, :]
            do = do_ref[rows, :]
            last = cum[CHUNK - 1:CHUNK, :]
            eg = jnp.exp(cum)
            eng = jnp.exp(-cum)
            elc = jnp.exp(last - cum)
            qe, ke, kl = q * eg, k * eng, k * elc
            ds = ds_ref[...]
            a = _tril(_dot_nt(qe, ke))
            da = _tril(_dot_nt(do, v))
            di_ref[rows, :] = (_dot_tn(a, do) + _dot_nt(kl, ds)).astype(BF16)
            st = st_ref[c]
            dkl = _dot_nn(v, ds)
            dq = (_dot_nn(do, st) + _dot_nn(da, ke)) * eg
            dk = _dot_tn(da, qe) * eng + dkl * elc
            el = jnp.exp(last)
            ds_ref[...] = ds * el + _dot_tn(do, qe)
            dlast = jnp.sum(kl * dkl, axis=0, keepdims=True) + el * jnp.sum(ds * st.astype(F32), axis=0, keepdims=True)
            x = q * dq - k * dk + jnp.where(crow == CHUNK - 1, dlast, 0.0)
            s = 1
            while s < CHUNK:
                x = x + _shift_up(x, s, crow, CHUNK)
                s *= 2
            df = x / f - dk
            dq_ref[rows, :] = (dq * (sq * (1.0 + qr * (1.0 - sq)))).astype(BF16)
            df_ref[rows, :] = (df * (1.0 - lb) * sf * (1.0 - sf)).astype(BF16)
            return dlb + jnp.sum(df * (1.0 - sf), axis=0, keepdims=True)

        dlb = lax.fori_loop(0, nc, step, jnp.zeros((1, HEAD), F32))
        dlb = dlb * lb * s1
        dhg_ref[0:1, :] = dlb
        dhg_ref[1:2, :] = -dlb

    col = lambda off: pl.BlockSpec((t, HEAD), lambda h: (0, off * nh + h))
    own = pl.BlockSpec((t, HEAD), lambda h: (0, h))
    hsp = pl.BlockSpec((2, HEAD), lambda h: (0, h))
    return pl.pallas_call(
        body, name=name, grid=(nh,),
        in_specs=[own, col(3), col(4), col(5), col(6), own, pl.BlockSpec((None, nc, HEAD, HEAD), lambda h: (h, 0, 0, 0)),
                  hsp, pl.BlockSpec((1, HEAD), lambda h: (0, 0))],
        out_specs=[own, own, own, own, hsp, pl.BlockSpec((None, 1, HEAD), lambda h: (h, 0, 0))],
        out_shape=[SDS((t, w), BF16)] * 4 + [SDS((2, w), F32), SDS((nh, 1, HEAD), F32)],
        scratch_shapes=[pltpu.VMEM((t, HEAD), F32)] * 2 + [pltpu.VMEM((HEAD, HEAD), F32)],
        compiler_params=_cparams("parallel"),
    )(dy, z, z, z, z, o, states, hg, nw)


def _cast_pad(wt, n_pad, name):
    r, n = wt.shape
    tr = _tile(r, 256, 16)

    def body(w_ref, o_ref):
        if n_pad != n:
            o_ref[...] = jnp.zeros(o_ref.shape, o_ref.dtype)
        o_ref[:, 0:n] = w_ref[...].astype(BF16)

    return pl.pallas_call(
        body, name=name, grid=(r // tr,), in_specs=[pl.BlockSpec((tr, n), lambda i: (i, 0))],
        out_specs=pl.BlockSpec((tr, n_pad), lambda i: (i, 0)), out_shape=SDS((r, n_pad), BF16),
        compiler_params=_cparams("parallel"),
    )(wt)


def _adamw(wt, g, m, v, name):
    r, n = wt.shape
    ng = g.shape[1]
    tr = _tile(r, max(8, (1 << 18) // max(ng, 1) // 8 * 8), 8)
    c1 = 1.0 / (1.0 - ADAM_B1 ** ADAM_STEP)
    c2 = 1.0 / (1.0 - ADAM_B2 ** ADAM_STEP)

    def body(w_ref, g_ref, m_ref, v_ref, go_ref, d_ref, mo_ref, vo_ref):
        g_ = g_ref[:, 0:n]
        m2 = ADAM_B1 * m_ref[...] + (1.0 - ADAM_B1) * g_
        v2 = ADAM_B2 * v_ref[...] + (1.0 - ADAM_B2) * (g_ * g_)
        go_ref[...] = g_
        mo_ref[...] = m2
        vo_ref[...] = v2
        d_ref[...] = -ADAM_LR * ((m2 * c1) / (jnp.sqrt(v2 * c2) + ADAM_EPS) + ADAM_WD * w_ref[...])

    blk = pl.BlockSpec((tr, n), lambda i: (i, 0))
    return pl.pallas_call(
        body, name=name, grid=(r // tr,), in_specs=[blk, pl.BlockSpec((tr, ng), lambda i: (i, 0)), blk, blk],
        out_specs=[blk] * 4, out_shape=[SDS((r, n), F32)] * 4, compiler_params=_cparams("parallel"),
    )(wt, g, m, v)


def _place():
    x, y, c = lax.axis_index("x"), lax.axis_index("y"), lax.axis_index("c")
    return x, y, c, 2 * x + y


def _chip_dev(k, c):
    return (k // 2, k % 2, c)


def _half(ref, j, h, rows, per):
    return ref.at[j // per, pl.ds((j % per) * rows + h * (rows // 2), rows // 2)]


def _all_gather_weights(shards, metas, zero_pad, name):
    nw = len(shards)

    def body(*refs):
        src = refs[:nw]
        zp = refs[nw]
        dst = refs[nw + 1:2 * nw + 1]
        loc, pads, send, recv, fsend, frecv = refs[2 * nw + 1:]
        x, y, c, me = _place()
        sib = (x, y, 1 - c)
        copies = []
        npad = 0
        for i in range(nw):
            rows = src[i].shape[0]
            g, p, per = metas[i]
            cp = pltpu.make_async_copy(src[i], dst[i].at[me // per, pl.ds((me % per) * rows, rows)], loc.at[i])
            cp.start()
            copies.append(cp)
            if p > per * rows:
                for gi in range(g):
                    cp = pltpu.make_async_copy(zp.at[pl.ds(0, p - per * rows)], dst[i].at[gi, pl.ds(per * rows, p - per * rows)],
                                               pads.at[npad])
                    cp.start()
                    copies.append(cp)
                    npad += 1

        def ici(i, r, frm):
            rows = src[i].shape[0]
            per = metas[i][2]
            return pltpu.make_async_remote_copy(
                src_ref=src[i].at[pl.ds(c * (rows // 2), rows // 2)], dst_ref=_half(dst[i], frm, c, rows, per),
                send_sem=send.at[i, r - 1], recv_sem=recv.at[i, r - 1],
                device_id=_chip_dev((me + r) % N_CHIPS, c), device_id_type=MESH)

        def d2d(i, r, frm, h):
            rows = src[i].shape[0]
            per = metas[i][2]
            blk = _half(dst[i], frm, h, rows, per)
            return pltpu.make_async_remote_copy(src_ref=blk, dst_ref=blk, send_sem=fsend.at[i, r - 1],
                                                recv_sem=frecv.at[i, r - 1], device_id=sib, device_id_type=MESH)

        sends = []
        for i in range(nw):
            for r in range(1, N_CHIPS):
                cp = ici(i, r, me)
                cp.start()
                sends.append(cp)
        for i in range(nw):
            for r in range(1, N_CHIPS):
                frm = (me - r) % N_CHIPS
                ici(i, r, frm).wait_recv()
                cp = d2d(i, r, frm, c)
                cp.start()
                sends.append(cp)
        for i in range(nw):
            for r in range(1, N_CHIPS):
                d2d(i, r, (me - r) % N_CHIPS, 1 - c).wait_recv()
        for cp in sends:
            cp.wait_send()
        for cp in copies:
            cp.wait()

    n_pads = sum(m[0] for s, m in zip(shards, metas) if m[1] > m[2] * s.shape[0])
    out_shape = [SDS((m[0], m[1], s.shape[1]), BF16) for s, m in zip(shards, metas)]
    return pl.pallas_call(
        body, name=name, in_specs=[ANY] * (nw + 1), out_specs=[ANY] * nw, out_shape=out_shape,
        scratch_shapes=[pltpu.SemaphoreType.DMA((nw,)), pltpu.SemaphoreType.DMA((max(n_pads, 1),))]
        + [pltpu.SemaphoreType.DMA((nw, N_CHIPS - 1))] * 4,
    )(*shards, zero_pad)


def _gather_small(packed, name):
    r, n = packed.shape

    def body(src, dst, send, recv):
        x, y, c, me = _place()
        dst[me] = src[...]
        cps = []
        for d in range(1, N_CHIPS):
            cp = pltpu.make_async_remote_copy(src_ref=src, dst_ref=dst.at[me], send_sem=send.at[d - 1], recv_sem=recv.at[d - 1],
                                              device_id=_chip_dev((me + d) % N_CHIPS, c), device_id_type=MESH)
            cp.start()
            cps.append(cp)
        for d in range(1, N_CHIPS):
            pltpu.make_async_remote_copy(src_ref=src, dst_ref=dst.at[(me - d) % N_CHIPS], send_sem=send.at[d - 1],
                                         recv_sem=recv.at[d - 1], device_id=_chip_dev((me + d) % N_CHIPS, c),
                                         device_id_type=MESH).wait_recv()
        for cp in cps:
            cp.wait_send()

    return pl.pallas_call(
        body, name=name, in_specs=[VMEM_SPEC], out_specs=VMEM_SPEC, out_shape=SDS((N_CHIPS, r, n), F32),
        scratch_shapes=[pltpu.SemaphoreType.DMA((N_CHIPS - 1,))] * 2,
    )(packed)


def _all_reduce_small(packed, name):
    r, n = packed.shape

    def body(src, out, slots, send, recv):
        x, y, c, me = _place()
        idx = 2 * me + c
        slots[idx] = src[...]
        cps = []

        def peer(d):
            p = (idx + d) % N_DEV
            return (p // 4, (p // 2) % 2, p % 2)

        for d in range(1, N_DEV):
            cp = pltpu.make_async_remote_copy(src_ref=src, dst_ref=slots.at[idx], send_sem=send.at[d - 1], recv_sem=recv.at[d - 1],
                                              device_id=peer(d), device_id_type=MESH)
            cp.start()
            cps.append(cp)
        for d in range(1, N_DEV):
            pltpu.make_async_remote_copy(src_ref=src, dst_ref=slots.at[(idx - d) % N_DEV], send_sem=send.at[d - 1],
                                         recv_sem=recv.at[d - 1], device_id=peer(d), device_id_type=MESH).wait_recv()
        for cp in cps:
            cp.wait_send()
        acc = slots[0]
        for k in range(1, N_DEV):
            acc = acc + slots[k]
        out[...] = acc

    return pl.pallas_call(
        body, name=name, in_specs=[VMEM_SPEC], out_specs=VMEM_SPEC, out_shape=SDS((r, n), F32),
        scratch_shapes=[pltpu.VMEM((N_DEV, r, n), F32)] + [pltpu.SemaphoreType.DMA((N_DEV - 1,))] * 2,
    )(packed)


def _rs_pair_exchange(grads, metas, rows_of, name):
    nw = len(grads)

    def body(*refs):
        src = refs[:nw]
        dst = refs[nw:2 * nw]
        send, recv = refs[2 * nw:]
        x, y, c, me = _place()
        cps = []
        for i in range(nw):
            for j in range(N_CHIPS):
                cp = pltpu.make_async_remote_copy(
                    src_ref=_half(src[i], j, 1 - c, rows_of[i], metas[i][2]), dst_ref=dst[i].at[j],
                    send_sem=send.at[i, j], recv_sem=recv.at[i, j], device_id=(x, y, 1 - c), device_id_type=MESH)
                cp.start()
                cps.append(cp)
        for cp in cps:
            cp.wait()

    out_shape = [SDS((N_CHIPS, rows_of[i] // 2, g.shape[2]), F32) for i, g in enumerate(grads)]
    return pl.pallas_call(
        body, name=name, in_specs=[ANY] * nw, out_specs=[ANY] * nw, out_shape=out_shape,
        scratch_shapes=[pltpu.SemaphoreType.DMA((nw, N_CHIPS))] * 2,
    )(*grads)


def _rs_pair_add(g, got, meta, rows, sp, name):
    per = meta[2]
    n = g.shape[2]
    hr = rows // 2
    tr = _tile(hr, max(16, (1 << 18) // n // 16 * 16), 16)

    def body(sp_ref, g_ref, got_ref, snd_ref, own_ref):
        j = pl.program_id(1)
        s = g_ref[...] + got_ref[...]
        snd_ref[...] = s.astype(BF16)

        @pl.when(j == sp_ref[1])
        def _():
            own_ref[...] = s

    grid_spec = pltpu.PrefetchScalarGridSpec(
        num_scalar_prefetch=1, grid=(hr // tr, N_CHIPS),
        in_specs=[pl.BlockSpec((None, tr, n), lambda i, j, sp: (j // per, ((j % per) * rows + sp[0] * hr) // tr + i, 0)),
                  pl.BlockSpec((None, tr, n), lambda i, j, sp: (j, i, 0))],
        out_specs=[pl.BlockSpec((None, tr, n), lambda i, j, sp: (j, i, 0)), pl.BlockSpec((tr, n), lambda i, j, sp: (i, 0))])
    return pl.pallas_call(
        body, name=name, grid_spec=grid_spec, out_shape=[SDS((N_CHIPS, hr, n), BF16), SDS((hr, n), F32)],
        compiler_params=_cparams("parallel", "arbitrary"),
    )(sp, g, got)


def _rs_chip_exchange(sends, name):
    nw = len(sends)

    def body(*refs):
        src = refs[:nw]
        dst = refs[nw:2 * nw]
        send, recv = refs[2 * nw:]
        x, y, c, me = _place()
        cps = []
        for i in range(nw):
            for r in range(1, N_CHIPS):
                k = (me + r) % N_CHIPS
                cp = pltpu.make_async_remote_copy(src_ref=src[i].at[k], dst_ref=dst[i].at[r - 1], send_sem=send.at[i, r - 1],
                                                  recv_sem=recv.at[i, r - 1], device_id=_chip_dev(k, c), device_id_type=MESH)
                cp.start()
                cps.append(cp)
        for cp in cps:
            cp.wait()

    out_shape = [SDS((N_CHIPS - 1,) + s.shape[1:], BF16) for s in sends]
    return pl.pallas_call(
        body, name=name, in_specs=[ANY] * nw, out_specs=[ANY] * nw, out_shape=out_shape,
        scratch_shapes=[pltpu.SemaphoreType.DMA((nw, N_CHIPS - 1))] * 2,
    )(*sends)


def _rs_chip_add(own, got, name):
    hr, n = own.shape
    tr = _tile(hr, max(16, (1 << 18) // n // 16 * 16), 16)

    def body(own_ref, got_ref, o_ref):
        acc = own_ref[...]
        for r in range(N_CHIPS - 1):
            acc = acc + got_ref[r].astype(F32)
        o_ref[...] = acc

    return pl.pallas_call(
        body, name=name, grid=(hr // tr,),
        in_specs=[pl.BlockSpec((tr, n), lambda i: (i, 0)), pl.BlockSpec((N_CHIPS - 1, tr, n), lambda i: (0, i, 0))],
        out_specs=pl.BlockSpec((tr, n), lambda i: (i, 0)), out_shape=SDS((hr, n), F32), compiler_params=_cparams("parallel"),
    )(own, got)


def _rs_pair_share(halves, name):
    nw = len(halves)

    def body(*refs):
        src = refs[:nw]
        dst = refs[nw:2 * nw]
        loc, send, recv = refs[2 * nw:]
        x, y, c, me = _place()
        cps = []
        for i in range(nw):
            hr = src[i].shape[0]
            mine = dst[i].at[pl.ds(c * hr, hr)]
            cp = pltpu.make_async_copy(src[i], mine, loc.at[i])
            cp.start()
            cps.append(cp)
            cp = pltpu.make_async_remote_copy(src_ref=src[i], dst_ref=mine, send_sem=send.at[i], recv_sem=recv.at[i],
                                              device_id=(x, y, 1 - c), device_id_type=MESH)
            cp.start()
            cps.append(cp)
        for cp in cps:
            cp.wait()

    out_shape = [SDS((2 * h.shape[0], h.shape[1]), F32) for h in halves]
    return pl.pallas_call(
        body, name=name, in_specs=[ANY] * nw, out_specs=[ANY] * nw, out_shape=out_shape,
        scratch_shapes=[pltpu.SemaphoreType.DMA((nw,))] * 3,
    )(*halves)


def kernel(x, p, ln_g, ln_b, ffn1_w_in, ffn1_w_out, mix_w_in, conv_w, hg_lower_bound, hg_norm_w, branch_w_conv, branch_w_hgrn, mix_w_out, ffn2_w_in, ffn2_w_out, ple_w_gate, ple_w_proj, loss_target, m_ln_g, m_ln_b, m_ffn1_w_in, m_ffn1_w_out, m_mix_w_in, m_conv_w, m_hg_lower_bound, m_hg_norm_w, m_branch_w_conv, m_branch_w_hgrn, m_mix_w_out, m_ffn2_w_in, m_ffn2_w_out, m_ple_w_gate, m_ple_w_proj, v_ln_g, v_ln_b, v_ffn1_w_in, v_ffn1_w_out, v_mix_w_in, v_conv_w, v_hg_lower_bound, v_hg_norm_w, v_branch_w_conv, v_branch_w_hgrn, v_mix_w_out, v_ffn2_w_in, v_ffn2_w_out, v_ple_w_gate, v_ple_w_proj):
    assert ln_g.shape[0] == DEPTH and x.shape[0] == 1 and p.shape[:2] == (1, 1)
    t, d = x.shape[1], x.shape[2]
    w = d // 2
    x0 = x.reshape(t, d)
    pe = p.reshape(t, p.shape[-1])
    target = loss_target.reshape(t, d)
    cx, cy, cc = lax.axis_index("x"), lax.axis_index("y"), lax.axis_index("c")
    chip = 2 * cx + cy
    sp = jnp.stack([cc, chip]).astype(jnp.int32)

    big = dict(ffn1_w_in=ffn1_w_in[0], ffn1_w_out=ffn1_w_out[0], mix_w_in=mix_w_in[0], branch_w_conv=branch_w_conv[0],
               branch_w_hgrn=branch_w_hgrn[0], mix_w_out=mix_w_out[0], ffn2_w_in=ffn2_w_in[0], ffn2_w_out=ffn2_w_out[0],
               ple_w_gate=ple_w_gate[0], ple_w_proj=ple_w_proj[0])
    moments = dict(ffn1_w_in=(m_ffn1_w_in, v_ffn1_w_in), ffn1_w_out=(m_ffn1_w_out, v_ffn1_w_out), mix_w_in=(m_mix_w_in, v_mix_w_in),
                   branch_w_conv=(m_branch_w_conv, v_branch_w_conv), branch_w_hgrn=(m_branch_w_hgrn, v_branch_w_hgrn),
                   mix_w_out=(m_mix_w_out, v_mix_w_out), ffn2_w_in=(m_ffn2_w_in, v_ffn2_w_in), ffn2_w_out=(m_ffn2_w_out, v_ffn2_w_out),
                   ple_w_gate=(m_ple_w_gate, v_ple_w_gate), ple_w_proj=(m_ple_w_proj, v_ple_w_proj))
    names = list(big)

    n_loc = ffn1_w_in.shape[-1]
    n_pad = -(-n_loc // LANES) * LANES
    assert mix_w_in.shape[-1] % LANES == 0 and ffn1_w_out.shape[1] * 2 == n_loc
    pad_cols = dict(ffn1_w_in=n_pad, ffn2_w_in=n_pad)
    meta = {k: (N_CHIPS, big[k].shape[0], 1) for k in names}
    meta["ffn1_w_out"] = meta["ffn2_w_out"] = (2, n_pad, 2)
    shards = [_cast_pad(big[k], pad_cols.get(k, big[k].shape[1]), "cast_" + k) for k in names]
    zero_pad = jnp.zeros((max(n_pad - n_loc, 16), d), BF16)
    gathered = dict(zip(names, _all_gather_weights(shards, [meta[k] for k in names], zero_pad, "gather_weights")))
    wg = {k: (v.reshape(v.shape[0] * v.shape[1], v.shape[2]) if k in ("ffn1_w_out", "ffn2_w_out", "mix_w_out", "ple_w_gate") else v)
          for k, v in gathered.items()}

    dq, wq = d // N_CHIPS, w // N_CHIPS
    small = jnp.concatenate([ln_g[0], ln_b[0], jnp.pad(conv_w[0], ((0, 5), (0, dq - wq)))], axis=0)
    small = _gather_small(small, "gather_small")
    lng = small[:, 0:4, :].transpose(1, 0, 2).reshape(4, 1, d)
    lnb = small[:, 4:8, :].transpose(1, 0, 2).reshape(4, 1, d)
    cw = small[:, 8:11, :wq].transpose(1, 0, 2).reshape(3, w)
    hg = hg_lower_bound
    nw_ = hg_norm_w

    z1 = _mm(x0, wg["ffn1_w_in"], name="ffn1_in", b_blocked=True, out_dtype=BF16)
    h1 = _swiglu_fwd(z1, "ffn1_act")
    y1 = _mm(h1, wg["ffn1_w_out"], name="ffn1_out", tk=512, tn=1024)
    r1, x1, x1b = _ln_fwd(x0, y1, lng[0], lnb[0], 0.5, "ln0")
    z = _mm(x1b, wg["mix_w_in"], name="mix_in", b_blocked=True)
    ya = _conv_fwd(z, cw, w, "conv_fwd")
    yb, o_h, states = _hgrn_fwd(z, hg, nw_, w, "hgrn_fwd")
    ma = _mm(ya, wg["branch_w_conv"], name="branch_conv", b_blocked=True, tn=512)
    mb = _mm(yb, wg["branch_w_hgrn"], name="branch_hgrn", b_blocked=True, tn=512)
    merged = _merge_fwd(z, ma, mb, w, "merge_fwd")
    y2 = _mm(merged, wg["mix_w_out"], name="mix_out", tn=1024)
    r2, x2, x2b = _ln_fwd(x1, y2, lng[1], lnb[1], 1.0, "ln1")
    z3 = _mm(x2b, wg["ffn2_w_in"], name="ffn2_in", b_blocked=True, out_dtype=BF16)
    h3 = _swiglu_fwd(z3, "ffn2_act")
    y3 = _mm(h3, wg["ffn2_w_out"], name="ffn2_out", tk=512, tn=1024)
    r3, x3, x3b = _ln_fwd(x2, y3, lng[2], lnb[2], 0.5, "ln2")
    gp = _mm(x3b, wg["ple_w_gate"], name="ple_gate", tn=1024)
    pp = _mm(pe, wg["ple_w_proj"], name="ple_proj", b_blocked=True, tn=512)
    dr4, dgp, dpp, dg3, db3, sq = _tail(x3, gp, pp, lng[3], lnb[3], target, "tail")

    grads = {}
    dx3m = _mm(dgp, wg["ple_w_gate"], name="d_ple_gate_x", tb=True, tn=1024, tk=1024)
    grads["ple_w_gate"] = _mm(x3b, dgp, name="d_ple_gate_w", ta=True, tk=512, tn=1024).reshape(N_CHIPS, -1, d)
    grads["ple_w_proj"] = _mm(pe, dpp, name="d_ple_proj_w", ta=True, out_blocked=N_CHIPS, tk=512, tn=512)
    dr3, dy3b, dg2, db2 = _ln_bwd(dr4, dx3m, r3, lng[2], 0.5, "ln2_bwd")
    dh3 = _mm(dy3b, wg["ffn2_w_out"], name="d_ffn2_out_x", tb=True, out_dtype=BF16, tn=1408, tk=1024)
    grads["ffn2_w_out"] = _mm(h3, dy3b, name="d_ffn2_out_w", ta=True, tk=512, tn=1024).reshape(2, n_pad, d)
    dz3 = _swiglu_bwd(dh3, z3, "ffn2_act_bwd")
    dx2m = _mm(dz3, wg["ffn2_w_in"], name="d_ffn2_in_x", tb=True, b_blocked=True, tn=1024, tk=1408)
    grads["ffn2_w_in"] = _mm(x2b, dz3, name="d_ffn2_in_w", ta=True, out_blocked=N_CHIPS, tk=512)
    dr2, dy2b, dg1, db1 = _ln_bwd(dr3, dx2m, r2, lng[1], 1.0, "ln1_bwd")
    dmer = _mm(dy2b, wg["mix_w_out"], name="d_mix_out_x", tb=True, tn=1024, tk=1024)
    grads["mix_w_out"] = _mm(merged, dy2b, name="d_mix_out_w", ta=True, tk=512, tn=1024).reshape(N_CHIPS, -1, d)
    dma, dmb, dgc, dgh = _merge_bwd(dmer, z, ma, mb, w, "merge_bwd")
    dya = _mm(dma, wg["branch_w_conv"], name="d_branch_conv_x", tb=True, b_blocked=True, tn=1024, tk=512)
    dyb = _mm(dmb, wg["branch_w_hgrn"], name="d_branch_hgrn_x", tb=True, b_blocked=True, tn=1024, tk=512)
    grads["branch_w_conv"] = _mm(ya, dma, name="d_branch_conv_w", ta=True, out_blocked=N_CHIPS, tk=512, tn=512)
    grads["branch_w_hgrn"] = _mm(yb, dmb, name="d_branch_hgrn_w", ta=True, out_blocked=N_CHIPS, tk=512, tn=512)
    dbg, dcg, dhc, dcw = _conv_bwd(dya, z, cw, w, "conv_bwd")
    dq_, df_, di_, dgr_, dhg, dnw = _hgrn_bwd(dyb, z, o_h, states, hg, nw_, w, "hgrn_bwd")
    dz = jnp.concatenate([dbg, dcg, dhc, dq_, df_, di_, dgr_, dgc, dgh], axis=1)
    dx1m = _mm(dz, wg["mix_w_in"], name="d_mix_in_x", tb=True, b_blocked=True, tn=1024, tk=1408)
    grads["mix_w_in"] = _mm(x1b, dz, name="d_mix_in_w", ta=True, out_blocked=N_CHIPS, tk=512)
    dr1, dy1b, dg0, db0 = _ln_bwd(dr2, dx1m, r1, lng[0], 0.5, "ln0_bwd")
    dh1 = _mm(dy1b, wg["ffn1_w_out"], name="d_ffn1_out_x", tb=True, out_dtype=BF16, tn=1408, tk=1024)
    grads["ffn1_w_out"] = _mm(h1, dy1b, name="d_ffn1_out_w", ta=True, tk=512, tn=1024).reshape(2, n_pad, d)
    dz1 = _swiglu_bwd(dh1, z1, "ffn1_act_bwd")
    dx0m = _mm(dz1, wg["ffn1_w_in"], name="d_ffn1_in_x", tb=True, b_blocked=True, tn=1024, tk=1408)
    grads["ffn1_w_in"] = _mm(x0, dz1, name="d_ffn1_in_w", ta=True, out_blocked=N_CHIPS, tk=512)
    grad_x = _residual_out(dr1, dx0m, "grad_x").reshape(x.shape)

    pack = jnp.concatenate([
        dg0, dg1, dg2, dg3, db0, db1, db2, db3,
        jnp.pad(dcw, ((0, 0), (0, d - w))), jnp.pad(dhg, ((0, 0), (0, d - w))),
        jnp.pad(jnp.sum(dnw, axis=0), ((0, 0), (0, d - HEAD))), sq], axis=0)
    pack = _all_reduce_small(jnp.pad(pack, ((0, 1), (0, 0))), "reduce_small")
    loss = (0.5 / d) * jnp.sum(pack[14])
    g_ln_g = lax.dynamic_slice_in_dim(pack[0:4], chip * dq, dq, axis=1)
    g_ln_b = lax.dynamic_slice_in_dim(pack[4:8], chip * dq, dq, axis=1)
    g_conv = lax.dynamic_slice_in_dim(pack[8:11, :w], chip * wq, wq, axis=1)
    g_hg = pack[11:13, :w]
    g_nw = pack[13:14, :HEAD]

    metas = [meta[k] for k in names]
    rows_of = [big[k].shape[0] for k in names]
    gl = [grads[k] for k in names]
    got1 = _rs_pair_exchange(gl, metas, rows_of, "rs_pair_exchange")
    sends, owns = [], []
    for k, g_, got, m_, r_ in zip(names, gl, got1, metas, rows_of):
        s_, o_ = _rs_pair_add(g_, got, m_, r_, sp, "rs_pair_add_" + k)
        sends.append(s_)
        owns.append(o_)
    got2 = _rs_chip_exchange(sends, "rs_chip_exchange")
    halves = [_rs_chip_add(o_, g2, "rs_chip_add_" + k) for k, o_, g2 in zip(names, owns, got2)]
    full = dict(zip(names, _rs_pair_share(halves, "rs_pair_share")))

    outs = {}
    for k in names:
        m_, v_ = moments[k]
        outs[k] = [a.reshape(m_.shape) for a in _adamw(big[k], full[k], m_[0], v_[0], "adamw_" + k)]
    small_w = dict(ln_g=(ln_g, g_ln_g, m_ln_g, v_ln_g), ln_b=(ln_b, g_ln_b, m_ln_b, v_ln_b),
                   conv_w=(conv_w, g_conv, m_conv_w, v_conv_w), hg_lower_bound=(hg_lower_bound, g_hg, m_hg_lower_bound, v_hg_lower_bound),
                   hg_norm_w=(hg_norm_w, g_nw, m_hg_norm_w, v_hg_norm_w))
    for k, (w_, g_, m_, v_) in small_w.items():
        s2 = (-1, w_.shape[-1])
        outs[k] = [a.reshape(w_.shape) for a in _adamw(w_.reshape(s2), g_.reshape(s2), m_.reshape(s2), v_.reshape(s2), "adamw_" + k)]

    order = ["ln_g", "ln_b", "ffn1_w_in", "ffn1_w_out", "mix_w_in", "conv_w", "hg_lower_bound", "hg_norm_w", "branch_w_conv",
             "branch_w_hgrn", "mix_w_out", "ffn2_w_in", "ffn2_w_out", "ple_w_gate", "ple_w_proj"]
    return (loss, grad_x, *[outs[k][0] for k in order], *[outs[k][1] for k in order], *[outs[k][2] for k in order],
            *[outs[k][3] for k in order])
```

```python
import functools

import jax
import jax.numpy as jnp
from jax import lax
from jax.experimental import pallas as pl
from jax.experimental.pallas import tpu as pltpu

F32 = jnp.float32
BF16 = jnp.bfloat16
MESH = pl.DeviceIdType.MESH
ANY = pl.BlockSpec(memory_space=pl.ANY)
VMEM_SPEC = pl.BlockSpec(memory_space=pltpu.VMEM)
SDS = jax.ShapeDtypeStruct

DEPTH = 1
ALPHA = (2.0 * DEPTH) ** 0.25
LN_EPS = 1e-5
RMS_EPS = 1e-6
CHUNK = 32
HEAD = 128
ADAM_LR, ADAM_B1, ADAM_B2, ADAM_EPS, ADAM_WD, ADAM_STEP = 0.001, 0.9, 0.999, 1e-08, 0.01, 10

LANES = 128
N_CHIPS = 4
N_DEV = 8
VMEM_LIMIT = 52 * 1024 * 1024


def _cparams(*sem):
    if sem:
        return pltpu.CompilerParams(dimension_semantics=sem, vmem_limit_bytes=VMEM_LIMIT)
    return pltpu.CompilerParams(vmem_limit_bytes=VMEM_LIMIT)


def _tile(n, target, mult):
    best = None
    for t in range(mult, min(n, target) + 1, mult):
        if n % t == 0:
            best = t
    return best if best is not None else n


def _sigmoid(x):
    return 1.0 / (1.0 + jnp.exp(-x))


def _mm(a, b, *, name, ta=False, tb=False, b_blocked=False, out_blocked=0, out_dtype=F32,
        tm=512, tn=1408, tk=2048):
    if ta:
        kd, m = a.shape
    else:
        m, kd = a.shape
    if b_blocked and not tb:
        g, kb, nb = b.shape
        assert kb == kd
        n = g * nb
        tn = _tile(nb, tn, LANES)
        tk = _tile(kd, tk, LANES)
        per_n = nb // tn
        b_spec = pl.BlockSpec((None, tk, tn), lambda i, j, k: (j // per_n, k, j % per_n))
    elif b_blocked and tb:
        g, n, kb = b.shape
        assert g * kb == kd
        tn = _tile(n, tn, LANES)
        tk = _tile(kb, tk, LANES)
        per_k = kb // tk
        b_spec = pl.BlockSpec((None, tn, tk), lambda i, j, k: (k // per_k, j, k % per_k))
    elif tb:
        n, kb = b.shape
        assert kb == kd
        tn = _tile(n, tn, LANES)
        tk = _tile(kd, tk, LANES)
        b_spec = pl.BlockSpec((tn, tk), lambda i, j, k: (j, k))
    else:
        kb, n = b.shape
        assert kb == kd
        tn = _tile(n // out_blocked if out_blocked else n, tn, LANES)
        per_o = (n // out_blocked) // tn if out_blocked else None
        tk = _tile(kd, tk, LANES)
        b_spec = pl.BlockSpec((tk, tn), lambda i, j, k: (k, j))
    tm = _tile(m, tm, LANES if ta else 8)
    if ta:
        a_spec = pl.BlockSpec((tk, tm), lambda i, j, k: (k, i))
    else:
        a_spec = pl.BlockSpec((tm, tk), lambda i, j, k: (i, k))
    if out_blocked:
        assert not b_blocked and not tb
        o_spec = pl.BlockSpec((None, tm, tn), lambda i, j, k: (j // per_o, i, j % per_o))
        o_shape = SDS((out_blocked, m, n // out_blocked), out_dtype)
    else:
        o_spec = pl.BlockSpec((tm, tn), lambda i, j, k: (i, j))
        o_shape = SDS((m, n), out_dtype)
    nk = kd // tk
    dn = (((0 if ta else 1,), (1 if tb else 0,)), ((), ()))

    def body(a_ref, b_ref, o_ref, acc_ref):
        part = lax.dot_general(a_ref[...].astype(BF16), b_ref[...].astype(BF16), dn, preferred_element_type=F32)
        if nk == 1:
            o_ref[...] = part.astype(o_ref.dtype)
        else:
            k = pl.program_id(2)

            @pl.when(k == 0)
            def _():
                acc_ref[...] = part

            @pl.when(k > 0)
            def _():
                acc_ref[...] += part

            @pl.when(k == nk - 1)
            def _():
                o_ref[...] = acc_ref[...].astype(o_ref.dtype)

    return pl.pallas_call(
        body, name=name, grid=(m // tm, n // tn, nk), in_specs=[a_spec, b_spec], out_specs=o_spec, out_shape=o_shape,
        scratch_shapes=[pltpu.VMEM((tm, tn), F32)],
        compiler_params=_cparams("parallel", "parallel", "arbitrary"),
    )(a, b)


def _swiglu_fwd(z, name):
    t, n = z.shape
    n2 = n // 2
    tr = _tile(t, 128, 16)

    def body(a_ref, u_ref, o_ref):
        a = a_ref[...].astype(F32)
        o_ref[...] = (a * _sigmoid(a) * u_ref[...].astype(F32)).astype(o_ref.dtype)

    return pl.pallas_call(
        body, name=name, grid=(t // tr,),
        in_specs=[pl.BlockSpec((tr, n2), lambda i: (i, 0)), pl.BlockSpec((tr, n2), lambda i: (i, 1))],
        out_specs=pl.BlockSpec((tr, n2), lambda i: (i, 0)), out_shape=SDS((t, n2), BF16),
        compiler_params=_cparams("parallel"),
    )(z, z)


def _swiglu_bwd(dh, z, name):
    t, n = z.shape
    n2 = n // 2
    tr = _tile(t, 128, 16)

    def body(dh_ref, a_ref, u_ref, o_ref):
        a = a_ref[...].astype(F32)
        dh_ = dh_ref[...].astype(F32)
        s = _sigmoid(a)
        o_ref[:, 0:n2] = (dh_ * u_ref[...].astype(F32) * (s * (1.0 + a * (1.0 - s)))).astype(o_ref.dtype)
        o_ref[:, n2:n] = (dh_ * a * s).astype(o_ref.dtype)

    return pl.pallas_call(
        body, name=name, grid=(t // tr,),
        in_specs=[pl.BlockSpec((tr, n2), lambda i: (i, 0)), pl.BlockSpec((tr, n2), lambda i: (i, 0)),
                  pl.BlockSpec((tr, n2), lambda i: (i, 1))],
        out_specs=pl.BlockSpec((tr, n), lambda i: (i, 0)), out_shape=SDS((t, n), BF16),
        compiler_params=_cparams("parallel"),
    )(dh, z, z)


def _ln_stats(r):
    mu = jnp.mean(r, axis=-1, keepdims=True)
    xc = r - mu
    var = jnp.mean(xc * xc, axis=-1, keepdims=True)
    return xc * lax.rsqrt(var + LN_EPS)


def _ln_fwd(xp, y, g, b, scale, name):
    t, d = xp.shape
    tr = _tile(t, 256, 16)

    def body(xp_ref, y_ref, g_ref, b_ref, r_ref, x_ref, xb_ref):
        r = ALPHA * xp_ref[...] + scale * y_ref[...]
        x = _ln_stats(r) * g_ref[...] + b_ref[...]
        r_ref[...] = r
        x_ref[...] = x
        xb_ref[...] = x.astype(BF16)

    row = pl.BlockSpec((tr, d), lambda i: (i, 0))
    vec = pl.BlockSpec((1, d), lambda i: (0, 0))
    return pl.pallas_call(
        body, name=name, grid=(t // tr,), in_specs=[row, row, vec, vec], out_specs=[row, row, row],
        out_shape=[SDS((t, d), F32), SDS((t, d), F32), SDS((t, d), BF16)], compiler_params=_cparams("parallel"),
    )(xp, y, g, b)


def _ln_bwd(dra, dxm, r, g, scale, name):
    t, d = r.shape
    tr = _tile(t, 256, 16)

    def body(dra_ref, dxm_ref, r_ref, g_ref, dr_ref, dyb_ref, dg_ref, db_ref):
        i = pl.program_id(0)
        dx = ALPHA * dra_ref[...] + dxm_ref[...]
        rr = r_ref[...]
        mu = jnp.mean(rr, axis=-1, keepdims=True)
        xc = rr - mu
        rstd = lax.rsqrt(jnp.mean(xc * xc, axis=-1, keepdims=True) + LN_EPS)
        xh = xc * rstd
        dxh = dx * g_ref[...]
        dr = rstd * (dxh - jnp.mean(dxh, axis=-1, keepdims=True) - xh * jnp.mean(dxh * xh, axis=-1, keepdims=True))
        dr_ref[...] = dr
        dyb_ref[...] = (scale * dr).astype(BF16)
        dg = jnp.sum(dx * xh, axis=0, keepdims=True)
        db = jnp.sum(dx, axis=0, keepdims=True)

        @pl.when(i == 0)
        def _():
            dg_ref[...] = dg
            db_ref[...] = db

        @pl.when(i > 0)
        def _():
            dg_ref[...] += dg
            db_ref[...] += db

    row = pl.BlockSpec((tr, d), lambda i: (i, 0))
    vec = pl.BlockSpec((1, d), lambda i: (0, 0))
    return pl.pallas_call(
        body, name=name, grid=(t // tr,), in_specs=[row, row, row, vec], out_specs=[row, row, vec, vec],
        out_shape=[SDS((t, d), F32), SDS((t, d), BF16), SDS((1, d), F32), SDS((1, d), F32)],
        compiler_params=_cparams("arbitrary"),
    )(dra, dxm, r, g)


def _tail(x3, gp, pp, g, b, target, name):
    t, d = x3.shape
    tr = _tile(t, 256, 16)

    def body(x3_ref, gp_ref, pp_ref, g_ref, b_ref, tg_ref, dr_ref, dgp_ref, dpp_ref, dg_ref, db_ref, sq_ref):
        i = pl.program_id(0)
        gate = _sigmoid(gp_ref[...])
        pp_ = pp_ref[...]
        r = ALPHA * x3_ref[...] + gate * pp_
        mu = jnp.mean(r, axis=-1, keepdims=True)
        xc = r - mu
        rstd = lax.rsqrt(jnp.mean(xc * xc, axis=-1, keepdims=True) + LN_EPS)
        xh = xc * rstd
        err = xh * g_ref[...] + b_ref[...] - tg_ref[...]
        dx = err * (1.0 / d)
        dxh = dx * g_ref[...]
        dr = rstd * (dxh - jnp.mean(dxh, axis=-1, keepdims=True) - xh * jnp.mean(dxh * xh, axis=-1, keepdims=True))
        dr_ref[...] = dr
        dgp_ref[...] = (dr * pp_ * gate * (1.0 - gate)).astype(BF16)
        dpp_ref[...] = (dr * gate).astype(BF16)
        dg = jnp.sum(dx * xh, axis=0, keepdims=True)
        db = jnp.sum(dx, axis=0, keepdims=True)
        sq = jnp.sum(err * err, axis=0, keepdims=True)

        @pl.when(i == 0)
        def _():
            dg_ref[...] = dg
            db_ref[...] = db
            sq_ref[...] = sq

        @pl.when(i > 0)
        def _():
            dg_ref[...] += dg
            db_ref[...] += db
            sq_ref[...] += sq

    row = pl.BlockSpec((tr, d), lambda i: (i, 0))
    vec = pl.BlockSpec((1, d), lambda i: (0, 0))
    return pl.pallas_call(
        body, name=name, grid=(t // tr,), in_specs=[row, row, row, vec, vec, row],
        out_specs=[row, row, row, vec, vec, vec],
        out_shape=[SDS((t, d), F32), SDS((t, d), BF16), SDS((t, d), BF16), SDS((1, d), F32), SDS((1, d), F32),
                   SDS((1, d), F32)],
        compiler_params=_cparams("arbitrary"),
    )(x3, gp, pp, g, b, target)


def _residual_out(dra, dxm, name):
    t, d = dra.shape
    tr = _tile(t, 256, 8)

    def body(a_ref, b_ref, o_ref):
        o_ref[...] = ALPHA * a_ref[...] + b_ref[...]

    row = pl.BlockSpec((tr, d), lambda i: (i, 0))
    return pl.pallas_call(body, name=name, grid=(t // tr,), in_specs=[row, row], out_specs=row,
                          out_shape=SDS((t, d), F32), compiler_params=_cparams("parallel"))(dra, dxm)


def _merge_fwd(z, ma, mb, w, name):
    t = z.shape[0]
    tr = _tile(t, 256, 16)

    def body(gc_ref, gh_ref, ma_ref, mb_ref, o_ref):
        o_ref[...] = (_sigmoid(gc_ref[...]) * ma_ref[...] + _sigmoid(gh_ref[...]) * mb_ref[...]).astype(BF16)

    half = pl.BlockSpec((tr, w), lambda i, j: (i, j))
    return pl.pallas_call(
        body, name=name, grid=(t // tr, 2),
        in_specs=[pl.BlockSpec((tr, w), lambda i, j: (i, 7 + j)), pl.BlockSpec((tr, w), lambda i, j: (i, 9 + j)), half, half],
        out_specs=half, out_shape=SDS((t, 2 * w), BF16), compiler_params=_cparams("parallel", "parallel"),
    )(z, z, ma, mb)


def _merge_bwd(dmer, z, ma, mb, w, name):
    t = z.shape[0]
    tr = _tile(t, 256, 16)

    def body(d_ref, gc_ref, gh_ref, ma_ref, mb_ref, dma_ref, dmb_ref, dgc_ref, dgh_ref):
        dm = d_ref[...]
        sc = _sigmoid(gc_ref[...])
        sh = _sigmoid(gh_ref[...])
        dma_ref[...] = (dm * sc).astype(BF16)
        dmb_ref[...] = (dm * sh).astype(BF16)
        dgc_ref[...] = (dm * ma_ref[...] * sc * (1.0 - sc)).astype(BF16)
        dgh_ref[...] = (dm * mb_ref[...] * sh * (1.0 - sh)).astype(BF16)

    half = pl.BlockSpec((tr, w), lambda i, j: (i, j))
    return pl.pallas_call(
        body, name=name, grid=(t // tr, 2),
        in_specs=[half, pl.BlockSpec((tr, w), lambda i, j: (i, 7 + j)), pl.BlockSpec((tr, w), lambda i, j: (i, 9 + j)), half, half],
        out_specs=[half] * 4, out_shape=[SDS((t, 2 * w), BF16)] * 4, compiler_params=_cparams("parallel", "parallel"),
    )(dmer, z, z, ma, mb)


def _shift_down(x, s, row):
    return jnp.where(row >= s, pltpu.roll(x, s, axis=0), 0.0)


def _shift_up(x, s, row, t):
    return jnp.where(row < t - s, pltpu.roll(x, t - s, axis=0), 0.0)


def _conv_fwd(z, cw, w, name):
    t = z.shape[0]
    tc = LANES
    nb = w // tc

    def body(b_ref, c_ref, h_ref, w_ref, o_ref):
        u = c_ref[...] * h_ref[...]
        row = lax.broadcasted_iota(jnp.int32, u.shape, 0)
        cw_ = w_ref[...]
        conv = cw_[2:3, :] * u + cw_[1:2, :] * _shift_down(u, 1, row) + cw_[0:1, :] * _shift_down(u, 2, row)
        o_ref[...] = (b_ref[...] * conv).astype(BF16)

    col = lambda off: pl.BlockSpec((t, tc), lambda j: (0, off * nb + j))
    return pl.pallas_call(
        body, name=name, grid=(nb,), in_specs=[col(0), col(1), col(2), pl.BlockSpec((3, tc), lambda j: (0, j))],
        out_specs=pl.BlockSpec((t, tc), lambda j: (0, j)), out_shape=SDS((t, w), BF16), compiler_params=_cparams("parallel"),
    )(z, z, z, cw)


def _conv_bwd(dy, z, cw, w, name):
    t = z.shape[0]
    tc = LANES
    nb = w // tc

    def body(dy_ref, b_ref, c_ref, h_ref, w_ref, db_ref, dc_ref, dh_ref, dw_ref):
        c_, h_ = c_ref[...], h_ref[...]
        u = c_ * h_
        row = lax.broadcasted_iota(jnp.int32, u.shape, 0)
        cw_ = w_ref[...]
        u1 = _shift_down(u, 1, row)
        u2 = _shift_down(u, 2, row)
        dy_ = dy_ref[...]
        db_ref[...] = (dy_ * (cw_[2:3, :] * u + cw_[1:2, :] * u1 + cw_[0:1, :] * u2)).astype(BF16)
        dconv = dy_ * b_ref[...]
        du = cw_[2:3, :] * dconv + cw_[1:2, :] * _shift_up(dconv, 1, row, t) + cw_[0:1, :] * _shift_up(dconv, 2, row, t)
        dc_ref[...] = (du * h_).astype(BF16)
        dh_ref[...] = (du * c_).astype(BF16)
        dw_ref[0:1, :] = jnp.sum(dconv * u2, axis=0, keepdims=True)
        dw_ref[1:2, :] = jnp.sum(dconv * u1, axis=0, keepdims=True)
        dw_ref[2:3, :] = jnp.sum(dconv * u, axis=0, keepdims=True)

    col = lambda off: pl.BlockSpec((t, tc), lambda j: (0, off * nb + j))
    own = pl.BlockSpec((t, tc), lambda j: (0, j))
    wsp = pl.BlockSpec((3, tc), lambda j: (0, j))
    return pl.pallas_call(
        body, name=name, grid=(nb,), in_specs=[own, col(0), col(1), col(2), wsp], out_specs=[own, own, own, wsp],
        out_shape=[SDS((t, w), BF16)] * 3 + [SDS((3, w), F32)], compiler_params=_cparams("parallel"),
    )(dy, z, z, z, cw)


def _lower_bound(hg):
    mx = jnp.max(hg, axis=0, keepdims=True)
    e = jnp.exp(hg - mx)
    inv = 1.0 / jnp.sum(e, axis=0, keepdims=True)
    return e[0:1, :] * inv, e[1:2, :] * inv


def _chunk_cumsum(x, row):
    s = 1
    while s < CHUNK:
        x = x + jnp.where(row % CHUNK >= s, pltpu.roll(x, s, axis=0), 0.0)
        s *= 2
    return x


def _dot_nt(a, b):
    return lax.dot_general(a.astype(BF16), b.astype(BF16), (((1,), (1,)), ((), ())), preferred_element_type=F32)


def _dot_tn(a, b):
    return lax.dot_general(a.astype(BF16), b.astype(BF16), (((0,), (0,)), ((), ())), preferred_element_type=F32)


def _dot_nn(a, b):
    return jnp.dot(a.astype(BF16), b.astype(BF16), preferred_element_type=F32)


def _tril(x):
    r = lax.broadcasted_iota(jnp.int32, x.shape, 0)
    c = lax.broadcasted_iota(jnp.int32, x.shape, 1)
    return jnp.where(r >= c, x, 0.0)


def _hgrn_chunk_inputs(q_ref, f_ref, cum_ref, lb, rows):
    qr = q_ref[rows, :]
    q = qr * _sigmoid(qr)
    f = lb + (1.0 - lb) * _sigmoid(f_ref[rows, :])
    return q, 1.0 - f, cum_ref[rows, :]


def _hgrn_fwd(z, hg, nw, w, name):
    t = z.shape[0]
    nh = w // HEAD
    nc = t // CHUNK

    def body(q_ref, f_ref, i_ref, g_ref, hg_ref, nw_ref, y_ref, o_ref, st_ref, cum_ref, s_ref):
        lb, _ = _lower_bound(hg_ref[...])
        row = lax.broadcasted_iota(jnp.int32, (t, HEAD), 0)
        cum_ref[...] = _chunk_cumsum(jnp.log(lb + (1.0 - lb) * _sigmoid(f_ref[...])), row)
        s_ref[...] = jnp.zeros_like(s_ref)

        def step(c, carry):
            rows = pl.ds(pl.multiple_of(c * CHUNK, CHUNK), CHUNK)
            q, k, cum = _hgrn_chunk_inputs(q_ref, f_ref, cum_ref, lb, rows)
            v = i_ref[rows, :]
            last = cum[CHUNK - 1:CHUNK, :]
            qe = q * jnp.exp(cum)
            st = s_ref[...]
            st_ref[c] = st.astype(BF16)
            o = _dot_nt(qe, st) + _dot_nn(_tril(_dot_nt(qe, k * jnp.exp(-cum))), v)
            o_ref[rows, :] = o
            s_ref[...] = st * jnp.exp(last) + _dot_tn(v, k * jnp.exp(last - cum))
            return carry

        lax.fori_loop(0, nc, step, 0)
        o = o_ref[...]
        n = o * lax.rsqrt(jnp.mean(o * o, axis=-1, keepdims=True) + RMS_EPS)
        gr = g_ref[...]
        y_ref[...] = (n * nw_ref[...] * gr * _sigmoid(gr)).astype(BF16)

    col = lambda off: pl.BlockSpec((t, HEAD), lambda h: (0, off * nh + h))
    own = pl.BlockSpec((t, HEAD), lambda h: (0, h))
    return pl.pallas_call(
        body, name=name, grid=(nh,),
        in_specs=[col(3), col(4), col(5), col(6), pl.BlockSpec((2, HEAD), lambda h: (0, h)),
                  pl.BlockSpec((1, HEAD), lambda h: (0, 0))],
        out_specs=[own, own, pl.BlockSpec((None, nc, HEAD, HEAD), lambda h: (h, 0, 0, 0))],
        out_shape=[SDS((t, w), BF16), SDS((t, w), F32), SDS((nh, nc, HEAD, HEAD), BF16)],
        scratch_shapes=[pltpu.VMEM((t, HEAD), F32), pltpu.VMEM((HEAD, HEAD), F32)],
        compiler_params=_cparams("parallel"),
    )(z, z, z, z, hg, nw)


def _hgrn_bwd(dy, z, o, states, hg, nw, w, name):
    t = z.shape[0]
    nh = w // HEAD
    nc = t // CHUNK

    def body(dy_ref, q_ref, f_ref, i_ref, g_ref, o_ref, st_ref, hg_ref, nw_ref,
             dq_ref, df_ref, di_ref, dg_ref, dhg_ref, dnw_ref, cum_ref, do_ref, ds_ref):
        lb, s1 = _lower_bound(hg_ref[...])
        row = lax.broadcasted_iota(jnp.int32, (t, HEAD), 0)
        crow = lax.broadcasted_iota(jnp.int32, (CHUNK, HEAD), 0)
        cum_ref[...] = _chunk_cumsum(jnp.log(lb + (1.0 - lb) * _sigmoid(f_ref[...])), row)

        o_ = o_ref[...]
        rstd = lax.rsqrt(jnp.mean(o_ * o_, axis=-1, keepdims=True) + RMS_EPS)
        n = o_ * rstd
        gr = g_ref[...]
        sg = _sigmoid(gr)
        dy_ = dy_ref[...]
        dg_ref[...] = (dy_ * n * nw_ref[...] * (sg * (1.0 + gr * (1.0 - sg)))).astype(BF16)
        dsil = dy_ * gr * sg
        dnw_ref[...] = jnp.sum(dsil * n, axis=0, keepdims=True)
        dn = dsil * nw_ref[...]
        do_ref[...] = rstd * (dn - n * jnp.mean(dn * n, axis=-1, keepdims=True))

        ds_ref[...] = jnp.zeros_like(ds_ref)

        def step(cc, dlb):
            c = nc - 1 - cc
            rows = pl.ds(pl.multiple_of(c * CHUNK, CHUNK), CHUNK)
            qr = q_ref[rows, :]
            sq = _sigmoid(qr)
            q = qr * sq
            sf = _sigmoid(f_ref[rows, :])
            f = lb + (1.0 - lb) * sf
            k = 1.0 - f
            cum = cum_ref[rows, :]
            v = i_ref[rows, :]
            do = do_ref[rows, :]
            last = cum[CHUNK - 1:CHUNK, :]
            eg = jnp.exp(cum)
            eng = jnp.exp(-cum)
            elc = jnp.exp(last - cum)
            qe, ke, kl = q * eg, k * eng, k * elc
            ds = ds_ref[...]
            a = _tril(_dot_nt(qe, ke))
            da = _tril(_dot_nt(do, v))
            di_ref[rows, :] = (_dot_tn(a, do) + _dot_nt(kl, ds)).astype(BF16)
            st = st_ref[c]
            dkl = _dot_nn(v, ds)
            dq = (_dot_nn(do, st) + _dot_nn(da, ke)) * eg
            dk = _dot_tn(da, qe) * eng + dkl * elc
            el = jnp.exp(last)
            ds_ref[...] = ds * el + _dot_tn(do, qe)
            dlast = jnp.sum(kl * dkl, axis=0, keepdims=True) + el * jnp.sum(ds * st.astype(F32), axis=0, keepdims=True)
            x = q * dq - k * dk + jnp.where(crow == CHUNK - 1, dlast, 0.0)
            s = 1
            while s < CHUNK:
                x = x + _shift_up(x, s, crow, CHUNK)
                s *= 2
            df = x / f - dk
            dq_ref[rows, :] = (dq * (sq * (1.0 + qr * (1.0 - sq)))).astype(BF16)
            df_ref[rows, :] = (df * (1.0 - lb) * sf * (1.0 - sf)).astype(BF16)
            return dlb + jnp.sum(df * (1.0 - sf), axis=0, keepdims=True)

        dlb = lax.fori_loop(0, nc, step, jnp.zeros((1, HEAD), F32))
        dlb = dlb * lb * s1
        dhg_ref[0:1, :] = dlb
        dhg_ref[1:2, :] = -dlb

    col = lambda off: pl.BlockSpec((t, HEAD), lambda h: (0, off * nh + h))
    own = pl.BlockSpec((t, HEAD), lambda h: (0, h))
    hsp = pl.BlockSpec((2, HEAD), lambda h: (0, h))
    return pl.pallas_call(
        body, name=name, grid=(nh,),
        in_specs=[own, col(3), col(4), col(5), col(6), own, pl.BlockSpec((None, nc, HEAD, HEAD), lambda h: (h, 0, 0, 0)),
                  hsp, pl.BlockSpec((1, HEAD), lambda h: (0, 0))],
        out_specs=[own, own, own, own, hsp, pl.BlockSpec((None, 1, HEAD), lambda h: (h, 0, 0))],
        out_shape=[SDS((t, w), BF16)] * 4 + [SDS((2, w), F32), SDS((nh, 1, HEAD), F32)],
        scratch_shapes=[pltpu.VMEM((t, HEAD), F32)] * 2 + [pltpu.VMEM((HEAD, HEAD), F32)],
        compiler_params=_cparams("parallel"),
    )(dy, z, z, z, z, o, states, hg, nw)


def _cast_pad(wt, n_pad, meta, sp, name):
    r, n = wt.shape
    g, p, per = meta
    tr = _tile(r, 256, 16)

    def body(sp_ref, w_ref, o_ref):
        if n_pad != n:
            o_ref[...] = jnp.zeros(o_ref.shape, o_ref.dtype)
        o_ref[:, 0:n] = w_ref[...].astype(BF16)

    grid_spec = pltpu.PrefetchScalarGridSpec(
        num_scalar_prefetch=1, grid=(r // tr,), in_specs=[pl.BlockSpec((tr, n), lambda i, sp: (i, 0))],
        out_specs=pl.BlockSpec((None, tr, n_pad), lambda i, sp: (sp[1] // per, ((sp[1] % per) * r) // tr + i, 0)))
    return pl.pallas_call(body, name=name, grid_spec=grid_spec, out_shape=SDS((g, p, n_pad), BF16),
                          compiler_params=_cparams("parallel"))(sp, wt)


def _adamw(wt, g, m, v, name):
    r, n = wt.shape
    ng = g.shape[1]
    tr = _tile(r, max(8, (1 << 18) // max(ng, 1) // 8 * 8), 8)
    c1 = 1.0 / (1.0 - ADAM_B1 ** ADAM_STEP)
    c2 = 1.0 / (1.0 - ADAM_B2 ** ADAM_STEP)

    def body(w_ref, g_ref, m_ref, v_ref, go_ref, d_ref, mo_ref, vo_ref):
        g_ = g_ref[:, 0:n]
        m2 = ADAM_B1 * m_ref[...] + (1.0 - ADAM_B1) * g_
        v2 = ADAM_B2 * v_ref[...] + (1.0 - ADAM_B2) * (g_ * g_)
        go_ref[...] = g_
        mo_ref[...] = m2
        vo_ref[...] = v2
        d_ref[...] = -ADAM_LR * ((m2 * c1) / (jnp.sqrt(v2 * c2) + ADAM_EPS) + ADAM_WD * w_ref[...])

    blk = pl.BlockSpec((tr, n), lambda i: (i, 0))
    return pl.pallas_call(
        body, name=name, grid=(r // tr,), in_specs=[blk, pl.BlockSpec((tr, ng), lambda i: (i, 0)), blk, blk],
        out_specs=[blk] * 4, out_shape=[SDS((r, n), F32)] * 4, compiler_params=_cparams("parallel"),
    )(wt, g, m, v)


def _place():
    x, y, c = lax.axis_index("x"), lax.axis_index("y"), lax.axis_index("c")
    return x, y, c, 2 * x + y


def _chip_dev(k, c):
    return (k // 2, k % 2, c)


def _half(ref, j, h, rows, per):
    return ref.at[j // per, pl.ds((j % per) * rows + h * (rows // 2), rows // 2)]


def _all_gather_weights(bufs, metas, rows_of, zero_pad, name):
    nw = len(bufs)

    def body(*refs):
        src = refs[:nw]
        zp = refs[nw]
        dst = refs[nw + 1:2 * nw + 1]
        pads, send, recv, fsend, frecv = refs[2 * nw + 1:]
        x, y, c, me = _place()
        sib = (x, y, 1 - c)
        copies = []
        npad = 0
        for i in range(nw):
            rows = rows_of[i]
            g, p, per = metas[i]
            if p > per * rows:
                for gi in range(g):
                    cp = pltpu.make_async_copy(zp.at[pl.ds(0, p - per * rows)], dst[i].at[gi, pl.ds(per * rows, p - per * rows)],
                                               pads.at[npad])
                    cp.start()
                    copies.append(cp)
                    npad += 1

        def ici(i, r, frm):
            per = metas[i][2]
            return pltpu.make_async_remote_copy(
                src_ref=_half(src[i], me, c, rows_of[i], per), dst_ref=_half(dst[i], frm, c, rows_of[i], per),
                send_sem=send.at[i, r - 1], recv_sem=recv.at[i, r - 1],
                device_id=_chip_dev((me + r) % N_CHIPS, c), device_id_type=MESH)

        def d2d(i, r, frm, h):
            blk = _half(dst[i], frm, h, rows_of[i], metas[i][2])
            return pltpu.make_async_remote_copy(src_ref=blk, dst_ref=blk, send_sem=fsend.at[i, r - 1],
                                                recv_sem=frecv.at[i, r - 1], device_id=sib, device_id_type=MESH)

        sends = []
        for i in range(nw):
            for r in range(1, N_CHIPS):
                cp = ici(i, r, me)
                cp.start()
                sends.append(cp)
        for i in range(nw):
            for r in range(1, N_CHIPS):
                frm = (me - r) % N_CHIPS
                ici(i, r, frm).wait_recv()
                cp = d2d(i, r, frm, c)
                cp.start()
                sends.append(cp)
        for i in range(nw):
            for r in range(1, N_CHIPS):
                d2d(i, r, (me - r) % N_CHIPS, 1 - c).wait_recv()
        for cp in sends:
            cp.wait_send()
        for cp in copies:
            cp.wait()

    n_pads = sum(m[0] for r, m in zip(rows_of, metas) if m[1] > m[2] * r)
    return pl.pallas_call(
        body, name=name, in_specs=[ANY] * (nw + 1), out_specs=[ANY] * nw, out_shape=[SDS(b.shape, b.dtype) for b in bufs],
        input_output_aliases={i: i for i in range(nw)},
        scratch_shapes=[pltpu.SemaphoreType.DMA((max(n_pads, 1),))] + [pltpu.SemaphoreType.DMA((nw, N_CHIPS - 1))] * 4,
    )(*bufs, zero_pad)


def _gather_small(packed, name):
    r, n = packed.shape

    def body(src, dst, send, recv):
        x, y, c, me = _place()
        dst[me] = src[...]
        cps = []
        for d in range(1, N_CHIPS):
            cp = pltpu.make_async_remote_copy(src_ref=src, dst_ref=dst.at[me], send_sem=send.at[d - 1], recv_sem=recv.at[d - 1],
                                              device_id=_chip_dev((me + d) % N_CHIPS, c), device_id_type=MESH)
            cp.start()
            cps.append(cp)
        for d in range(1, N_CHIPS):
            pltpu.make_async_remote_copy(src_ref=src, dst_ref=dst.at[(me - d) % N_CHIPS], send_sem=send.at[d - 1],
                                         recv_sem=recv.at[d - 1], device_id=_chip_dev((me + d) % N_CHIPS, c),
                                         device_id_type=MESH).wait_recv()
        for cp in cps:
            cp.wait_send()

    return pl.pallas_call(
        body, name=name, in_specs=[VMEM_SPEC], out_specs=VMEM_SPEC, out_shape=SDS((N_CHIPS, r, n), F32),
        scratch_shapes=[pltpu.SemaphoreType.DMA((N_CHIPS - 1,))] * 2,
    )(packed)


def _all_reduce_small(packed, name):
    r, n = packed.shape

    def body(src, out, slots, send, recv):
        x, y, c, me = _place()
        idx = 2 * me + c
        slots[idx] = src[...]
        cps = []

        def peer(d):
            p = (idx + d) % N_DEV
            return (p // 4, (p // 2) % 2, p % 2)

        for d in range(1, N_DEV):
            cp = pltpu.make_async_remote_copy(src_ref=src, dst_ref=slots.at[idx], send_sem=send.at[d - 1], recv_sem=recv.at[d - 1],
                                              device_id=peer(d), device_id_type=MESH)
            cp.start()
            cps.append(cp)
        for d in range(1, N_DEV):
            pltpu.make_async_remote_copy(src_ref=src, dst_ref=slots.at[(idx - d) % N_DEV], send_sem=send.at[d - 1],
                                         recv_sem=recv.at[d - 1], device_id=peer(d), device_id_type=MESH).wait_recv()
        for cp in cps:
            cp.wait_send()
        acc = slots[0]
        for k in range(1, N_DEV):
            acc = acc + slots[k]
        out[...] = acc

    return pl.pallas_call(
        body, name=name, in_specs=[VMEM_SPEC], out_specs=VMEM_SPEC, out_shape=SDS((r, n), F32),
        scratch_shapes=[pltpu.VMEM((N_DEV, r, n), F32)] + [pltpu.SemaphoreType.DMA((N_DEV - 1,))] * 2,
    )(packed)


def _rs_pair_exchange(grads, metas, rows_of, name):
    nw = len(grads)

    def body(*refs):
        src = refs[:nw]
        dst = refs[nw:2 * nw]
        send, recv = refs[2 * nw:]
        x, y, c, me = _place()
        cps = []
        for i in range(nw):
            for j in range(N_CHIPS):
                cp = pltpu.make_async_remote_copy(
                    src_ref=_half(src[i], j, 1 - c, rows_of[i], metas[i][2]), dst_ref=dst[i].at[j],
                    send_sem=send.at[i, j], recv_sem=recv.at[i, j], device_id=(x, y, 1 - c), device_id_type=MESH)
                cp.start()
                cps.append(cp)
        for cp in cps:
            cp.wait()

    out_shape = [SDS((N_CHIPS, rows_of[i] // 2, g.shape[2]), F32) for i, g in enumerate(grads)]
    return pl.pallas_call(
        body, name=name, in_specs=[ANY] * nw, out_specs=[ANY] * nw, out_shape=out_shape,
        scratch_shapes=[pltpu.SemaphoreType.DMA((nw, N_CHIPS))] * 2,
    )(*grads)


def _rs_pair_add(g, got, meta, rows, sp, name):
    per = meta[2]
    n = g.shape[2]
    hr = rows // 2
    tr = _tile(hr, max(16, (1 << 18) // n // 16 * 16), 16)

    def body(sp_ref, g_ref, got_ref, snd_ref, own_ref):
        j = pl.program_id(1)
        s = g_ref[...] + got_ref[...]
        snd_ref[...] = s.astype(BF16)

        @pl.when(j == sp_ref[1])
        def _():
            own_ref[...] = s

    grid_spec = pltpu.PrefetchScalarGridSpec(
        num_scalar_prefetch=1, grid=(hr // tr, N_CHIPS),
        in_specs=[pl.BlockSpec((None, tr, n), lambda i, j, sp: (j // per, ((j % per) * rows + sp[0] * hr) // tr + i, 0)),
                  pl.BlockSpec((None, tr, n), lambda i, j, sp: (j, i, 0))],
        out_specs=[pl.BlockSpec((None, tr, n), lambda i, j, sp: (j, i, 0)), pl.BlockSpec((tr, n), lambda i, j, sp: (i, 0))])
    return pl.pallas_call(
        body, name=name, grid_spec=grid_spec, out_shape=[SDS((N_CHIPS, hr, n), BF16), SDS((hr, n), F32)],
        compiler_params=_cparams("parallel", "arbitrary"),
    )(sp, g, got)


def _rs_chip_exchange(sends, name):
    nw = len(sends)

    def body(*refs):
        src = refs[:nw]
        dst = refs[nw:2 * nw]
        send, recv = refs[2 * nw:]
        x, y, c, me = _place()
        cps = []
        for i in range(nw):
            for r in range(1, N_CHIPS):
                k = (me + r) % N_CHIPS
                cp = pltpu.make_async_remote_copy(src_ref=src[i].at[k], dst_ref=dst[i].at[r - 1], send_sem=send.at[i, r - 1],
                                                  recv_sem=recv.at[i, r - 1], device_id=_chip_dev(k, c), device_id_type=MESH)
                cp.start()
                cps.append(cp)
        for cp in cps:
            cp.wait()

    out_shape = [SDS((N_CHIPS - 1,) + s.shape[1:], BF16) for s in sends]
    return pl.pallas_call(
        body, name=name, in_specs=[ANY] * nw, out_specs=[ANY] * nw, out_shape=out_shape,
        scratch_shapes=[pltpu.SemaphoreType.DMA((nw, N_CHIPS - 1))] * 2,
    )(*sends)


def _rs_chip_add(own, got, sp, name):
    hr, n = own.shape
    tr = _tile(hr, max(16, (1 << 18) // n // 16 * 16), 16)

    def body(sp_ref, own_ref, got_ref, o_ref):
        acc = own_ref[...]
        for r in range(N_CHIPS - 1):
            acc = acc + got_ref[r].astype(F32)
        o_ref[...] = acc

    grid_spec = pltpu.PrefetchScalarGridSpec(
        num_scalar_prefetch=1, grid=(hr // tr,),
        in_specs=[pl.BlockSpec((tr, n), lambda i, sp: (i, 0)), pl.BlockSpec((N_CHIPS - 1, tr, n), lambda i, sp: (0, i, 0))],
        out_specs=pl.BlockSpec((tr, n), lambda i, sp: (sp[0] * (hr // tr) + i, 0)))
    return pl.pallas_call(body, name=name, grid_spec=grid_spec, out_shape=SDS((2 * hr, n), F32),
                          compiler_params=_cparams("parallel"))(sp, own, got)


def _rs_pair_share(blocks, name):
    nw = len(blocks)

    def body(*refs):
        src = refs[:nw]
        dst = refs[nw:2 * nw]
        send, recv = refs[2 * nw:]
        x, y, c, me = _place()
        cps = []
        for i in range(nw):
            hr = src[i].shape[0] // 2
            cp = pltpu.make_async_remote_copy(src_ref=src[i].at[pl.ds(c * hr, hr)], dst_ref=dst[i].at[pl.ds(c * hr, hr)],
                                              send_sem=send.at[i], recv_sem=recv.at[i], device_id=(x, y, 1 - c), device_id_type=MESH)
            cp.start()
            cps.append(cp)
        for cp in cps:
            cp.wait()

    return pl.pallas_call(
        body, name=name, in_specs=[ANY] * nw, out_specs=[ANY] * nw, out_shape=[SDS(b.shape, b.dtype) for b in blocks],
        input_output_aliases={i: i for i in range(nw)}, scratch_shapes=[pltpu.SemaphoreType.DMA((nw,))] * 2,
    )(*blocks)


def kernel(x, p, ln_g, ln_b, ffn1_w_in, ffn1_w_out, mix_w_in, conv_w, hg_lower_bound, hg_norm_w, branch_w_conv, branch_w_hgrn, mix_w_out, ffn2_w_in, ffn2_w_out, ple_w_gate, ple_w_proj, loss_target, m_ln_g, m_ln_b, m_ffn1_w_in, m_ffn1_w_out, m_mix_w_in, m_conv_w, m_hg_lower_bound, m_hg_norm_w, m_branch_w_conv, m_branch_w_hgrn, m_mix_w_out, m_ffn2_w_in, m_ffn2_w_out, m_ple_w_gate, m_ple_w_proj, v_ln_g, v_ln_b, v_ffn1_w_in, v_ffn1_w_out, v_mix_w_in, v_conv_w, v_hg_lower_bound, v_hg_norm_w, v_branch_w_conv, v_branch_w_hgrn, v_mix_w_out, v_ffn2_w_in, v_ffn2_w_out, v_ple_w_gate, v_ple_w_proj):
    assert ln_g.shape[0] == DEPTH and x.shape[0] == 1 and p.shape[:2] == (1, 1)
    t, d = x.shape[1], x.shape[2]
    w = d // 2
    x0 = x.reshape(t, d)
    pe = p.reshape(t, p.shape[-1])
    target = loss_target.reshape(t, d)
    cx, cy, cc = lax.axis_index("x"), lax.axis_index("y"), lax.axis_index("c")
    chip = 2 * cx + cy
    sp = jnp.stack([cc, chip]).astype(jnp.int32)

    big = dict(ffn1_w_in=ffn1_w_in[0], ffn1_w_out=ffn1_w_out[0], mix_w_in=mix_w_in[0], branch_w_conv=branch_w_conv[0],
               branch_w_hgrn=branch_w_hgrn[0], mix_w_out=mix_w_out[0], ffn2_w_in=ffn2_w_in[0], ffn2_w_out=ffn2_w_out[0],
               ple_w_gate=ple_w_gate[0], ple_w_proj=ple_w_proj[0])
    moments = dict(ffn1_w_in=(m_ffn1_w_in, v_ffn1_w_in), ffn1_w_out=(m_ffn1_w_out, v_ffn1_w_out), mix_w_in=(m_mix_w_in, v_mix_w_in),
                   branch_w_conv=(m_branch_w_conv, v_branch_w_conv), branch_w_hgrn=(m_branch_w_hgrn, v_branch_w_hgrn),
                   mix_w_out=(m_mix_w_out, v_mix_w_out), ffn2_w_in=(m_ffn2_w_in, v_ffn2_w_in), ffn2_w_out=(m_ffn2_w_out, v_ffn2_w_out),
                   ple_w_gate=(m_ple_w_gate, v_ple_w_gate), ple_w_proj=(m_ple_w_proj, v_ple_w_proj))
    names = list(big)

    n_loc = ffn1_w_in.shape[-1]
    n_pad = -(-n_loc // LANES) * LANES
    assert mix_w_in.shape[-1] % LANES == 0 and ffn1_w_out.shape[1] * 2 == n_loc
    pad_cols = dict(ffn1_w_in=n_pad, ffn2_w_in=n_pad)
    meta = {k: (N_CHIPS, big[k].shape[0], 1) for k in names}
    meta["ffn1_w_out"] = meta["ffn2_w_out"] = (2, n_pad, 2)
    metas = [meta[k] for k in names]
    rows_of = [big[k].shape[0] for k in names]
    bufs = [_cast_pad(big[k], pad_cols.get(k, big[k].shape[1]), meta[k], sp, "cast_" + k) for k in names]
    zero_pad = jnp.zeros((max(n_pad - n_loc, 16), d), BF16)
    gathered = dict(zip(names, _all_gather_weights(bufs, metas, rows_of, zero_pad, "gather_weights")))
    wg = {k: (v.reshape(v.shape[0] * v.shape[1], v.shape[2]) if k in ("ffn1_w_out", "ffn2_w_out", "mix_w_out", "ple_w_gate") else v)
          for k, v in gathered.items()}

    dq, wq = d // N_CHIPS, w // N_CHIPS
    small = jnp.concatenate([ln_g[0], ln_b[0], jnp.pad(conv_w[0], ((0, 5), (0, dq - wq)))], axis=0)
    small = _gather_small(small, "gather_small")
    lng = small[:, 0:4, :].transpose(1, 0, 2).reshape(4, 1, d)
    lnb = small[:, 4:8, :].transpose(1, 0, 2).reshape(4, 1, d)
    cw = small[:, 8:11, :wq].transpose(1, 0, 2).reshape(3, w)
    hg = hg_lower_bound
    nw_ = hg_norm_w

    z1 = _mm(x0, wg["ffn1_w_in"], name="ffn1_in", b_blocked=True, out_dtype=BF16)
    h1 = _swiglu_fwd(z1, "ffn1_act")
    y1 = _mm(h1, wg["ffn1_w_out"], name="ffn1_out", tm=1024, tn=1024, tk=2816)
    r1, x1, x1b = _ln_fwd(x0, y1, lng[0], lnb[0], 0.5, "ln0")
    z = _mm(x1b, wg["mix_w_in"], name="mix_in", b_blocked=True)
    ya = _conv_fwd(z, cw, w, "conv_fwd")
    yb, o_h, states = _hgrn_fwd(z, hg, nw_, w, "hgrn_fwd")
    ma = _mm(ya, wg["branch_w_conv"], name="branch_conv", b_blocked=True, tn=512)
    mb = _mm(yb, wg["branch_w_hgrn"], name="branch_hgrn", b_blocked=True, tn=512)
    merged = _merge_fwd(z, ma, mb, w, "merge_fwd")
    y2 = _mm(merged, wg["mix_w_out"], name="mix_out", tn=1024)
    r2, x2, x2b = _ln_fwd(x1, y2, lng[1], lnb[1], 1.0, "ln1")
    z3 = _mm(x2b, wg["ffn2_w_in"], name="ffn2_in", b_blocked=True, out_dtype=BF16)
    h3 = _swiglu_fwd(z3, "ffn2_act")
    y3 = _mm(h3, wg["ffn2_w_out"], name="ffn2_out", tm=1024, tn=1024, tk=2816)
    r3, x3, x3b = _ln_fwd(x2, y3, lng[2], lnb[2], 0.5, "ln2")
    gp = _mm(x3b, wg["ple_w_gate"], name="ple_gate", tn=1024)
    pp = _mm(pe, wg["ple_w_proj"], name="ple_proj", b_blocked=True, tn=512)
    dr4, dgp, dpp, dg3, db3, sq = _tail(x3, gp, pp, lng[3], lnb[3], target, "tail")

    grads = {}
    dx3m = _mm(dgp, wg["ple_w_gate"], name="d_ple_gate_x", tb=True, tn=1024, tk=2048)
    grads["ple_w_gate"] = _mm(x3b, dgp, name="d_ple_gate_w", ta=True, tm=1024, tk=2048, tn=1024).reshape(N_CHIPS, -1, d)
    grads["ple_w_proj"] = _mm(pe, dpp, name="d_ple_proj_w", ta=True, out_blocked=N_CHIPS, tk=2048, tn=512)
    dr3, dy3b, dg2, db2 = _ln_bwd(dr4, dx3m, r3, lng[2], 0.5, "ln2_bwd")
    dh3 = _mm(dy3b, wg["ffn2_w_out"], name="d_ffn2_out_x", tb=True, out_dtype=BF16, tn=1408, tk=2048)
    grads["ffn2_w_out"] = _mm(h3, dy3b, name="d_ffn2_out_w", ta=True, tm=1408, tk=2048, tn=1024).reshape(2, n_pad, d)
    dz3 = _swiglu_bwd(dh3, z3, "ffn2_act_bwd")
    dx2m = _mm(dz3, wg["ffn2_w_in"], name="d_ffn2_in_x", tb=True, b_blocked=True, tm=1024, tn=1024, tk=2816)
    grads["ffn2_w_in"] = _mm(x2b, dz3, name="d_ffn2_in_w", ta=True, out_blocked=N_CHIPS, tk=2048)
    dr2, dy2b, dg1, db1 = _ln_bwd(dr3, dx2m, r2, lng[1], 1.0, "ln1_bwd")
    dmer = _mm(dy2b, wg["mix_w_out"], name="d_mix_out_x", tb=True, tn=1024, tk=2048)
    grads["mix_w_out"] = _mm(merged, dy2b, name="d_mix_out_w", ta=True, tm=1024, tk=2048, tn=1024).reshape(N_CHIPS, -1, d)
    dma, dmb, dgc, dgh = _merge_bwd(dmer, z, ma, mb, w, "merge_bwd")
    dya = _mm(dma, wg["branch_w_conv"], name="d_branch_conv_x", tb=True, b_blocked=True, tn=1024, tk=512)
    dyb = _mm(dmb, wg["branch_w_hgrn"], name="d_branch_hgrn_x", tb=True, b_blocked=True, tn=1024, tk=512)
    grads["branch_w_conv"] = _mm(ya, dma, name="d_branch_conv_w", ta=True, out_blocked=N_CHIPS, tm=1024, tk=2048, tn=512)
    grads["branch_w_hgrn"] = _mm(yb, dmb, name="d_branch_hgrn_w", ta=True, out_blocked=N_CHIPS, tm=1024, tk=2048, tn=512)
    dbg, dcg, dhc, dcw = _conv_bwd(dya, z, cw, w, "conv_bwd")
    dq_, df_, di_, dgr_, dhg, dnw = _hgrn_bwd(dyb, z, o_h, states, hg, nw_, w, "hgrn_bwd")
    dz = jnp.concatenate([dbg, dcg, dhc, dq_, df_, di_, dgr_, dgc, dgh], axis=1)
    dx1m = _mm(dz, wg["mix_w_in"], name="d_mix_in_x", tb=True, b_blocked=True, tm=1024, tn=1024, tk=2816)
    grads["mix_w_in"] = _mm(x1b, dz, name="d_mix_in_w", ta=True, out_blocked=N_CHIPS, tk=2048)
    dr1, dy1b, dg0, db0 = _ln_bwd(dr2, dx1m, r1, lng[0], 0.5, "ln0_bwd")
    dh1 = _mm(dy1b, wg["ffn1_w_out"], name="d_ffn1_out_x", tb=True, out_dtype=BF16, tn=1408, tk=2048)
    grads["ffn1_w_out"] = _mm(h1, dy1b, name="d_ffn1_out_w", ta=True, tm=1408, tk=2048, tn=1024).reshape(2, n_pad, d)
    dz1 = _swiglu_bwd(dh1, z1, "ffn1_act_bwd")
    dx0m = _mm(dz1, wg["ffn1_w_in"], name="d_ffn1_in_x", tb=True, b_blocked=True, tm=1024, tn=1024, tk=2816)
    grads["ffn1_w_in"] = _mm(x0, dz1, name="d_ffn1_in_w", ta=True, out_blocked=N_CHIPS, tk=2048)
    grad_x = _residual_out(dr1, dx0m, "grad_x").reshape(x.shape)

    pack = jnp.concatenate([
        dg0, dg1, dg2, dg3, db0, db1, db2, db3,
        jnp.pad(dcw, ((0, 0), (0, d - w))), jnp.pad(dhg, ((0, 0), (0, d - w))),
        jnp.pad(jnp.sum(dnw, axis=0), ((0, 0), (0, d - HEAD))), sq], axis=0)
    pack = _all_reduce_small(jnp.pad(pack, ((0, 1), (0, 0))), "reduce_small")
    loss = (0.5 / d) * jnp.sum(pack[14])
    g_ln_g = lax.dynamic_slice_in_dim(pack[0:4], chip * dq, dq, axis=1)
    g_ln_b = lax.dynamic_slice_in_dim(pack[4:8], chip * dq, dq, axis=1)
    g_conv = lax.dynamic_slice_in_dim(pack[8:11, :w], chip * wq, wq, axis=1)
    g_hg = pack[11:13, :w]
    g_nw = pack[13:14, :HEAD]

    gl = [grads[k] for k in names]
    got1 = _rs_pair_exchange(gl, metas, rows_of, "rs_pair_exchange")
    sends, owns = [], []
    for k, g_, got, m_, r_ in zip(names, gl, got1, metas, rows_of):
        s_, o_ = _rs_pair_add(g_, got, m_, r_, sp, "rs_pair_add_" + k)
        sends.append(s_)
        owns.append(o_)
    got2 = _rs_chip_exchange(sends, "rs_chip_exchange")
    halves = [_rs_chip_add(o_, g2, sp, "rs_chip_add_" + k) for k, o_, g2 in zip(names, owns, got2)]
    full = dict(zip(names, _rs_pair_share(halves, "rs_pair_share")))

    outs = {}
    for k in names:
        m_, v_ = moments[k]
        outs[k] = [a.reshape(m_.shape) for a in _adamw(big[k], full[k], m_[0], v_[0], "adamw_" + k)]
    small_w = dict(ln_g=(ln_g, g_ln_g, m_ln_g, v_ln_g), ln_b=(ln_b, g_ln_b, m_ln_b, v_ln_b),
                   conv_w=(conv_w, g_conv, m_conv_w, v_conv_w), hg_lower_bound=(hg_lower_bound, g_hg, m_hg_lower_bound, v_hg_lower_bound),
                   hg_norm_w=(hg_norm_w, g_nw, m_hg_norm_w, v_hg_norm_w))
    for k, (w_, g_, m_, v_) in small_w.items():
        s2 = (-1, w_.shape[-1])
        outs[k] = [a.reshape(w_.shape) for a in _adamw(w_.reshape(s2), g_.reshape(s2), m_.reshape(s2), v_.reshape(s2), "adamw_" + k)]

    order = ["ln_g", "ln_b", "ffn1_w_in", "ffn1_w_out", "mix_w_in", "conv_w", "hg_lower_bound", "hg_norm_w", "branch_w_conv",
             "branch_w_hgrn", "mix_w_out", "ffn2_w_in", "ffn2_w_out", "ple_w_gate", "ple_w_proj"]
    return (loss, grad_x, *[outs[k][0] for k in order], *[outs[k][1] for k in order], *[outs[k][2] for k in order],
            *[outs[k][3] for k in order])
```

```python
import collections
import functools

import jax
import jax.numpy as jnp
from jax import lax
from jax.experimental import pallas as pl
from jax.experimental.pallas import tpu as pltpu

F32 = jnp.float32
BF16 = jnp.bfloat16
MESH = pl.DeviceIdType.MESH
ANY = pl.BlockSpec(memory_space=pl.ANY)
VMEM_SPEC = pl.BlockSpec(memory_space=pltpu.VMEM)
SDS = jax.ShapeDtypeStruct

DEPTH = 1
ALPHA = (2.0 * DEPTH) ** 0.25
LN_EPS = 1e-5
RMS_EPS = 1e-6
CHUNK = 32
HEAD = 128
ADAM_LR, ADAM_B1, ADAM_B2, ADAM_EPS, ADAM_WD, ADAM_STEP = 0.001, 0.9, 0.999, 1e-08, 0.01, 10

LANES = 128
N_CHIPS = 4
N_DEV = 8
VMEM_LIMIT = 52 * 1024 * 1024


def _cparams(*sem):
    if sem:
        return pltpu.CompilerParams(dimension_semantics=sem, vmem_limit_bytes=VMEM_LIMIT)
    return pltpu.CompilerParams(vmem_limit_bytes=VMEM_LIMIT)


def _tile(n, target, mult):
    best = None
    for t in range(mult, min(n, target) + 1, mult):
        if n % t == 0:
            best = t
    return best if best is not None else n


def _sigmoid(x):
    return 1.0 / (1.0 + jnp.exp(-x))


_Stage = collections.namedtuple("_Stage", "ins out_shapes aliases sems start finish")


def _hosted_call(compute, stages, *, name, grid, in_specs, out_specs, out_shape, scratch_shapes, operands, parallel):
    n_in, n_out, n_scr = len(in_specs), len(out_specs), len(scratch_shapes)
    c_in = [len(s.ins) for s in stages]
    c_out = [len(s.out_shapes) for s in stages]
    c_sem = [len(s.sems) for s in stages]
    aliases = {}
    for si, s in enumerate(stages):
        for a_in, a_out in s.aliases.items():
            aliases[n_in + sum(c_in[:si]) + a_in] = n_out + sum(c_out[:si]) + a_out

    def body(*refs):
        ins = refs[:n_in]
        cins = refs[n_in:n_in + sum(c_in)]
        outs = refs[n_in + sum(c_in):n_in + sum(c_in) + n_out]
        couts = refs[n_in + sum(c_in) + n_out:n_in + sum(c_in) + n_out + sum(c_out)]
        scr = refs[n_in + sum(c_in) + n_out + sum(c_out):][:n_scr]
        sems = refs[n_in + sum(c_in) + n_out + sum(c_out) + n_scr:]

        def stage_refs(si):
            return (cins[sum(c_in[:si]):sum(c_in[:si + 1])], couts[sum(c_out[:si]):sum(c_out[:si + 1])],
                    sems[sum(c_sem[:si]):sum(c_sem[:si + 1])])

        if stages:
            first = functools.reduce(jnp.logical_and, [pl.program_id(ax) == 0 for ax in range(len(grid))])
            last = functools.reduce(jnp.logical_and, [pl.program_id(ax) == grid[ax] - 1 for ax in range(len(grid))])

            @pl.when(first)
            def _():
                for si, s in enumerate(stages):
                    s.start(*stage_refs(si))

        compute(*ins, *outs, *scr)
        if stages:
            @pl.when(last)
            def _():
                for si, s in enumerate(stages):
                    s.finish(*stage_refs(si))

    sem = ("arbitrary",) * len(grid) if stages else ("parallel",) * parallel + ("arbitrary",) * (len(grid) - parallel)
    res = pl.pallas_call(
        body, name=name, grid=grid, in_specs=list(in_specs) + [ANY] * sum(c_in), out_specs=list(out_specs) + [ANY] * sum(c_out),
        out_shape=list(out_shape) + [o for s in stages for o in s.out_shapes], input_output_aliases=aliases,
        scratch_shapes=list(scratch_shapes) + [q for s in stages for q in s.sems], compiler_params=_cparams(*sem),
    )(*operands, *[a for s in stages for a in s.ins])
    main = res[0] if n_out == 1 else list(res[:n_out])
    if not stages:
        return main
    rest = res[n_out:]
    return (main, *[list(rest[sum(c_out[:si]):sum(c_out[:si + 1])]) for si in range(len(stages))])


def _run_stages(stages, name):
    def body(*refs):
        n_i = sum(len(s.ins) for s in stages)
        n_o = sum(len(s.out_shapes) for s in stages)
        cins, couts, sems = refs[:n_i], refs[n_i:n_i + n_o], refs[n_i + n_o:]
        pos = [0, 0, 0]
        parts = []
        for s in stages:
            parts.append((cins[pos[0]:pos[0] + len(s.ins)], couts[pos[1]:pos[1] + len(s.out_shapes)], sems[pos[2]:pos[2] + len(s.sems)]))
            pos = [pos[0] + len(s.ins), pos[1] + len(s.out_shapes), pos[2] + len(s.sems)]
        for s, p_ in zip(stages, parts):
            s.start(*p_)
        for s, p_ in zip(stages, parts):
            s.finish(*p_)

    aliases, ni, no = {}, 0, 0
    for s in stages:
        for a_in, a_out in s.aliases.items():
            aliases[ni + a_in] = no + a_out
        ni, no = ni + len(s.ins), no + len(s.out_shapes)
    res = pl.pallas_call(
        body, name=name, in_specs=[ANY] * ni, out_specs=[ANY] * no, out_shape=[o for s in stages for o in s.out_shapes],
        input_output_aliases=aliases, scratch_shapes=[q for s in stages for q in s.sems],
    )(*[a for s in stages for a in s.ins])
    out, pos = [], 0
    for s in stages:
        out.append(list(res[pos:pos + len(s.out_shapes)]))
        pos += len(s.out_shapes)
    return out


def _mm(a, b, *, name, ta=False, tb=False, b_blocked=False, out_blocked=0, out_dtype=F32,
        tm=512, tn=1408, tk=2048, comm=()):
    if ta:
        kd, m = a.shape
    else:
        m, kd = a.shape
    if b_blocked and not tb:
        g, kb, nb = b.shape
        assert kb == kd
        n = g * nb
        tn = _tile(nb, tn, LANES)
        tk = _tile(kd, tk, LANES)
        per_n = nb // tn
        b_spec = pl.BlockSpec((None, tk, tn), lambda i, j, k: (j // per_n, k, j % per_n))
    elif b_blocked and tb:
        g, n, kb = b.shape
        assert g * kb == kd
        tn = _tile(n, tn, LANES)
        tk = _tile(kb, tk, LANES)
        per_k = kb // tk
        b_spec = pl.BlockSpec((None, tn, tk), lambda i, j, k: (k // per_k, j, k % per_k))
    elif tb:
        n, kb = b.shape
        assert kb == kd
        tn = _tile(n, tn, LANES)
        tk = _tile(kd, tk, LANES)
        b_spec = pl.BlockSpec((tn, tk), lambda i, j, k: (j, k))
    else:
        kb, n = b.shape
        assert kb == kd
        tn = _tile(n // out_blocked if out_blocked else n, tn, LANES)
        per_o = (n // out_blocked) // tn if out_blocked else None
        tk = _tile(kd, tk, LANES)
        b_spec = pl.BlockSpec((tk, tn), lambda i, j, k: (k, j))
    tm = _tile(m, tm, LANES if ta else 8)
    if ta:
        a_spec = pl.BlockSpec((tk, tm), lambda i, j, k: (k, i))
    else:
        a_spec = pl.BlockSpec((tm, tk), lambda i, j, k: (i, k))
    if out_blocked:
        assert not b_blocked and not tb
        o_spec = pl.BlockSpec((None, tm, tn), lambda i, j, k: (j // per_o, i, j % per_o))
        o_shape = SDS((out_blocked, m, n // out_blocked), out_dtype)
    else:
        o_spec = pl.BlockSpec((tm, tn), lambda i, j, k: (i, j))
        o_shape = SDS((m, n), out_dtype)
    nk = kd // tk
    dn = (((0 if ta else 1,), (1 if tb else 0,)), ((), ()))
    grid = (m // tm, n // tn, nk)

    def compute(a_ref, b_ref, o_ref, acc_ref):
        part = lax.dot_general(a_ref[...].astype(BF16), b_ref[...].astype(BF16), dn, preferred_element_type=F32)
        if nk == 1:
            o_ref[...] = part.astype(o_ref.dtype)
        else:
            k = pl.program_id(2)

            @pl.when(k == 0)
            def _():
                acc_ref[...] = part

            @pl.when(k > 0)
            def _():
                acc_ref[...] += part

            @pl.when(k == nk - 1)
            def _():
                o_ref[...] = acc_ref[...].astype(o_ref.dtype)

    return _hosted_call(compute, comm, name=name, grid=grid, in_specs=[a_spec, b_spec], out_specs=[o_spec], out_shape=[o_shape],
                        scratch_shapes=[pltpu.VMEM((tm, tn), F32)], operands=(a, b), parallel=2)


def _swiglu_fwd(z, name):
    t, n = z.shape
    n2 = n // 2
    tr = _tile(t, 128, 16)

    def body(a_ref, u_ref, o_ref):
        a = a_ref[...].astype(F32)
        o_ref[...] = (a * _sigmoid(a) * u_ref[...].astype(F32)).astype(o_ref.dtype)

    return pl.pallas_call(
        body, name=name, grid=(t // tr,),
        in_specs=[pl.BlockSpec((tr, n2), lambda i: (i, 0)), pl.BlockSpec((tr, n2), lambda i: (i, 1))],
        out_specs=pl.BlockSpec((tr, n2), lambda i: (i, 0)), out_shape=SDS((t, n2), BF16),
        compiler_params=_cparams("parallel"),
    )(z, z)


def _swiglu_bwd(dh, z, name):
    t, n = z.shape
    n2 = n // 2
    tr = _tile(t, 128, 16)

    def body(dh_ref, a_ref, u_ref, o_ref):
        a = a_ref[...].astype(F32)
        dh_ = dh_ref[...].astype(F32)
        s = _sigmoid(a)
        o_ref[:, 0:n2] = (dh_ * u_ref[...].astype(F32) * (s * (1.0 + a * (1.0 - s)))).astype(o_ref.dtype)
        o_ref[:, n2:n] = (dh_ * a * s).astype(o_ref.dtype)

    return pl.pallas_call(
        body, name=name, grid=(t // tr,),
        in_specs=[pl.BlockSpec((tr, n2), lambda i: (i, 0)), pl.BlockSpec((tr, n2), lambda i: (i, 0)),
                  pl.BlockSpec((tr, n2), lambda i: (i, 1))],
        out_specs=pl.BlockSpec((tr, n), lambda i: (i, 0)), out_shape=SDS((t, n), BF16),
        compiler_params=_cparams("parallel"),
    )(dh, z, z)


def _ln_stats(r):
    mu = jnp.mean(r, axis=-1, keepdims=True)
    xc = r - mu
    var = jnp.mean(xc * xc, axis=-1, keepdims=True)
    return xc * lax.rsqrt(var + LN_EPS)


def _ln_fwd(xp, y, g, b, scale, name):
    t, d = xp.shape
    tr = _tile(t, 256, 16)

    def body(xp_ref, y_ref, g_ref, b_ref, r_ref, x_ref, xb_ref):
        r = ALPHA * xp_ref[...] + scale * y_ref[...]
        x = _ln_stats(r) * g_ref[...] + b_ref[...]
        r_ref[...] = r
        x_ref[...] = x
        xb_ref[...] = x.astype(BF16)

    row = pl.BlockSpec((tr, d), lambda i: (i, 0))
    vec = pl.BlockSpec((1, d), lambda i: (0, 0))
    return pl.pallas_call(
        body, name=name, grid=(t // tr,), in_specs=[row, row, vec, vec], out_specs=[row, row, row],
        out_shape=[SDS((t, d), F32), SDS((t, d), F32), SDS((t, d), BF16)], compiler_params=_cparams("parallel"),
    )(xp, y, g, b)


def _ln_bwd(dra, dxm, r, g, scale, name):
    t, d = r.shape
    tr = _tile(t, 256, 16)

    def body(dra_ref, dxm_ref, r_ref, g_ref, dr_ref, dyb_ref, dg_ref, db_ref):
        i = pl.program_id(0)
        dx = ALPHA * dra_ref[...] + dxm_ref[...]
        rr = r_ref[...]
        mu = jnp.mean(rr, axis=-1, keepdims=True)
        xc = rr - mu
        rstd = lax.rsqrt(jnp.mean(xc * xc, axis=-1, keepdims=True) + LN_EPS)
        xh = xc * rstd
        dxh = dx * g_ref[...]
        dr = rstd * (dxh - jnp.mean(dxh, axis=-1, keepdims=True) - xh * jnp.mean(dxh * xh, axis=-1, keepdims=True))
        dr_ref[...] = dr
        dyb_ref[...] = (scale * dr).astype(BF16)
        dg = jnp.sum(dx * xh, axis=0, keepdims=True)
        db = jnp.sum(dx, axis=0, keepdims=True)

        @pl.when(i == 0)
        def _():
            dg_ref[...] = dg
            db_ref[...] = db

        @pl.when(i > 0)
        def _():
            dg_ref[...] += dg
            db_ref[...] += db

    row = pl.BlockSpec((tr, d), lambda i: (i, 0))
    vec = pl.BlockSpec((1, d), lambda i: (0, 0))
    return pl.pallas_call(
        body, name=name, grid=(t // tr,), in_specs=[row, row, row, vec], out_specs=[row, row, vec, vec],
        out_shape=[SDS((t, d), F32), SDS((t, d), BF16), SDS((1, d), F32), SDS((1, d), F32)],
        compiler_params=_cparams("arbitrary"),
    )(dra, dxm, r, g)


def _tail(x3, gp, pp, g, b, target, name):
    t, d = x3.shape
    tr = _tile(t, 256, 16)

    def body(x3_ref, gp_ref, pp_ref, g_ref, b_ref, tg_ref, dr_ref, dgp_ref, dpp_ref, dg_ref, db_ref, sq_ref):
        i = pl.program_id(0)
        gate = _sigmoid(gp_ref[...])
        pp_ = pp_ref[...]
        r = ALPHA * x3_ref[...] + gate * pp_
        mu = jnp.mean(r, axis=-1, keepdims=True)
        xc = r - mu
        rstd = lax.rsqrt(jnp.mean(xc * xc, axis=-1, keepdims=True) + LN_EPS)
        xh = xc * rstd
        err = xh * g_ref[...] + b_ref[...] - tg_ref[...]
        dx = err * (1.0 / d)
        dxh = dx * g_ref[...]
        dr = rstd * (dxh - jnp.mean(dxh, axis=-1, keepdims=True) - xh * jnp.mean(dxh * xh, axis=-1, keepdims=True))
        dr_ref[...] = dr
        dgp_ref[...] = (dr * pp_ * gate * (1.0 - gate)).astype(BF16)
        dpp_ref[...] = (dr * gate).astype(BF16)
        dg = jnp.sum(dx * xh, axis=0, keepdims=True)
        db = jnp.sum(dx, axis=0, keepdims=True)
        sq = jnp.sum(err * err, axis=0, keepdims=True)

        @pl.when(i == 0)
        def _():
            dg_ref[...] = dg
            db_ref[...] = db
            sq_ref[...] = sq

        @pl.when(i > 0)
        def _():
            dg_ref[...] += dg
            db_ref[...] += db
            sq_ref[...] += sq

    row = pl.BlockSpec((tr, d), lambda i: (i, 0))
    vec = pl.BlockSpec((1, d), lambda i: (0, 0))
    return pl.pallas_call(
        body, name=name, grid=(t // tr,), in_specs=[row, row, row, vec, vec, row],
        out_specs=[row, row, row, vec, vec, vec],
        out_shape=[SDS((t, d), F32), SDS((t, d), BF16), SDS((t, d), BF16), SDS((1, d), F32), SDS((1, d), F32),
                   SDS((1, d), F32)],
        compiler_params=_cparams("arbitrary"),
    )(x3, gp, pp, g, b, target)


def _residual_out(dra, dxm, name):
    t, d = dra.shape
    tr = _tile(t, 256, 8)

    def body(a_ref, b_ref, o_ref):
        o_ref[...] = ALPHA * a_ref[...] + b_ref[...]

    row = pl.BlockSpec((tr, d), lambda i: (i, 0))
    return pl.pallas_call(body, name=name, grid=(t // tr,), in_specs=[row, row], out_specs=row,
                          out_shape=SDS((t, d), F32), compiler_params=_cparams("parallel"))(dra, dxm)


def _merge_fwd(z, ma, mb, w, name):
    t = z.shape[0]
    tr = _tile(t, 256, 16)

    def body(gc_ref, gh_ref, ma_ref, mb_ref, o_ref):
        o_ref[...] = (_sigmoid(gc_ref[...]) * ma_ref[...] + _sigmoid(gh_ref[...]) * mb_ref[...]).astype(BF16)

    half = pl.BlockSpec((tr, w), lambda i, j: (i, j))
    return pl.pallas_call(
        body, name=name, grid=(t // tr, 2),
        in_specs=[pl.BlockSpec((tr, w), lambda i, j: (i, 7 + j)), pl.BlockSpec((tr, w), lambda i, j: (i, 9 + j)), half, half],
        out_specs=half, out_shape=SDS((t, 2 * w), BF16), compiler_params=_cparams("parallel", "parallel"),
    )(z, z, ma, mb)


def _merge_bwd(dmer, z, ma, mb, w, name):
    t = z.shape[0]
    tr = _tile(t, 256, 16)

    def body(d_ref, gc_ref, gh_ref, ma_ref, mb_ref, dma_ref, dmb_ref, dgc_ref, dgh_ref):
        dm = d_ref[...]
        sc = _sigmoid(gc_ref[...])
        sh = _sigmoid(gh_ref[...])
        dma_ref[...] = (dm * sc).astype(BF16)
        dmb_ref[...] = (dm * sh).astype(BF16)
        dgc_ref[...] = (dm * ma_ref[...] * sc * (1.0 - sc)).astype(BF16)
        dgh_ref[...] = (dm * mb_ref[...] * sh * (1.0 - sh)).astype(BF16)

    half = pl.BlockSpec((tr, w), lambda i, j: (i, j))
    return pl.pallas_call(
        body, name=name, grid=(t // tr, 2),
        in_specs=[half, pl.BlockSpec((tr, w), lambda i, j: (i, 7 + j)), pl.BlockSpec((tr, w), lambda i, j: (i, 9 + j)), half, half],
        out_specs=[half] * 4, out_shape=[SDS((t, 2 * w), BF16)] * 4, compiler_params=_cparams("parallel", "parallel"),
    )(dmer, z, z, ma, mb)


def _shift_down(x, s, row):
    return jnp.where(row >= s, pltpu.roll(x, s, axis=0), 0.0)


def _shift_up(x, s, row, t):
    return jnp.where(row < t - s, pltpu.roll(x, t - s, axis=0), 0.0)


def _conv_fwd(z, cw, w, name):
    t = z.shape[0]
    tc = LANES
    nb = w // tc

    def body(b_ref, c_ref, h_ref, w_ref, o_ref):
        u = c_ref[...] * h_ref[...]
        row = lax.broadcasted_iota(jnp.int32, u.shape, 0)
        cw_ = w_ref[...]
        conv = cw_[2:3, :] * u + cw_[1:2, :] * _shift_down(u, 1, row) + cw_[0:1, :] * _shift_down(u, 2, row)
        o_ref[...] = (b_ref[...] * conv).astype(BF16)

    col = lambda off: pl.BlockSpec((t, tc), lambda j: (0, off * nb + j))
    return pl.pallas_call(
        body, name=name, grid=(nb,), in_specs=[col(0), col(1), col(2), pl.BlockSpec((3, tc), lambda j: (0, j))],
        out_specs=pl.BlockSpec((t, tc), lambda j: (0, j)), out_shape=SDS((t, w), BF16), compiler_params=_cparams("parallel"),
    )(z, z, z, cw)


def _conv_bwd(dy, z, cw, w, name):
    t = z.shape[0]
    tc = LANES
    nb = w // tc

    def body(dy_ref, b_ref, c_ref, h_ref, w_ref, db_ref, dc_ref, dh_ref, dw_ref):
        c_, h_ = c_ref[...], h_ref[...]
        u = c_ * h_
        row = lax.broadcasted_iota(jnp.int32, u.shape, 0)
        cw_ = w_ref[...]
        u1 = _shift_down(u, 1, row)
        u2 = _shift_down(u, 2, row)
        dy_ = dy_ref[...]
        db_ref[...] = (dy_ * (cw_[2:3, :] * u + cw_[1:2, :] * u1 + cw_[0:1, :] * u2)).astype(BF16)
        dconv = dy_ * b_ref[...]
        du = cw_[2:3, :] * dconv + cw_[1:2, :] * _shift_up(dconv, 1, row, t) + cw_[0:1, :] * _shift_up(dconv, 2, row, t)
        dc_ref[...] = (du * h_).astype(BF16)
        dh_ref[...] = (du * c_).astype(BF16)
        dw_ref[0:1, :] = jnp.sum(dconv * u2, axis=0, keepdims=True)
        dw_ref[1:2, :] = jnp.sum(dconv * u1, axis=0, keepdims=True)
        dw_ref[2:3, :] = jnp.sum(dconv * u, axis=0, keepdims=True)

    col = lambda off: pl.BlockSpec((t, tc), lambda j: (0, off * nb + j))
    own = pl.BlockSpec((t, tc), lambda j: (0, j))
    wsp = pl.BlockSpec((3, tc), lambda j: (0, j))
    return pl.pallas_call(
        body, name=name, grid=(nb,), in_specs=[own, col(0), col(1), col(2), wsp], out_specs=[own, own, own, wsp],
        out_shape=[SDS((t, w), BF16)] * 3 + [SDS((3, w), F32)], compiler_params=_cparams("parallel"),
    )(dy, z, z, z, cw)


def _lower_bound(hg):
    mx = jnp.max(hg, axis=0, keepdims=True)
    e = jnp.exp(hg - mx)
    inv = 1.0 / jnp.sum(e, axis=0, keepdims=True)
    return e[0:1, :] * inv, e[1:2, :] * inv


def _chunk_cumsum(x, row):
    s = 1
    while s < CHUNK:
        x = x + jnp.where(row % CHUNK >= s, pltpu.roll(x, s, axis=0), 0.0)
        s *= 2
    return x


def _dot_nt(a, b):
    return lax.dot_general(a.astype(BF16), b.astype(BF16), (((1,), (1,)), ((), ())), preferred_element_type=F32)


def _dot_tn(a, b):
    return lax.dot_general(a.astype(BF16), b.astype(BF16), (((0,), (0,)), ((), ())), preferred_element_type=F32)


def _dot_nn(a, b):
    return jnp.dot(a.astype(BF16), b.astype(BF16), preferred_element_type=F32)


def _tril(x):
    r = lax.broadcasted_iota(jnp.int32, x.shape, 0)
    c = lax.broadcasted_iota(jnp.int32, x.shape, 1)
    return jnp.where(r >= c, x, 0.0)


def _hgrn_chunk_inputs(q_ref, f_ref, cum_ref, lb, rows):
    qr = q_ref[rows, :]
    q = qr * _sigmoid(qr)
    f = lb + (1.0 - lb) * _sigmoid(f_ref[rows, :])
    return q, 1.0 - f, cum_ref[rows, :]


def _hgrn_fwd(z, hg, nw, w, name, comm=()):
    t = z.shape[0]
    nh = w // HEAD
    nc = t // CHUNK

    def body(q_ref, f_ref, i_ref, g_ref, hg_ref, nw_ref, y_ref, o_ref, st_ref, cum_ref, s_ref):
        lb, _ = _lower_bound(hg_ref[...])
        row = lax.broadcasted_iota(jnp.int32, (t, HEAD), 0)
        cum_ref[...] = _chunk_cumsum(jnp.log(lb + (1.0 - lb) * _sigmoid(f_ref[...])), row)
        s_ref[...] = jnp.zeros_like(s_ref)

        def step(c, carry):
            rows = pl.ds(pl.multiple_of(c * CHUNK, CHUNK), CHUNK)
            q, k, cum = _hgrn_chunk_inputs(q_ref, f_ref, cum_ref, lb, rows)
            v = i_ref[rows, :]
            last = cum[CHUNK - 1:CHUNK, :]
            qe = q * jnp.exp(cum)
            st = s_ref[...]
            st_ref[c] = st.astype(BF16)
            o = _dot_nt(qe, st) + _dot_nn(_tril(_dot_nt(qe, k * jnp.exp(-cum))), v)
            o_ref[rows, :] = o
            s_ref[...] = st * jnp.exp(last) + _dot_tn(v, k * jnp.exp(last - cum))
            return carry

        lax.fori_loop(0, nc, step, 0)
        o = o_ref[...]
        n = o * lax.rsqrt(jnp.mean(o * o, axis=-1, keepdims=True) + RMS_EPS)
        gr = g_ref[...]
        y_ref[...] = (n * nw_ref[...] * gr * _sigmoid(gr)).astype(BF16)

    col = lambda off: pl.BlockSpec((t, HEAD), lambda h: (0, off * nh + h))
    own = pl.BlockSpec((t, HEAD), lambda h: (0, h))
    return _hosted_call(
        body, comm, name=name, grid=(nh,),
        in_specs=[col(3), col(4), col(5), col(6), pl.BlockSpec((2, HEAD), lambda h: (0, h)),
                  pl.BlockSpec((1, HEAD), lambda h: (0, 0))],
        out_specs=[own, own, pl.BlockSpec((None, nc, HEAD, HEAD), lambda h: (h, 0, 0, 0))],
        out_shape=[SDS((t, w), BF16), SDS((t, w), F32), SDS((nh, nc, HEAD, HEAD), BF16)],
        scratch_shapes=[pltpu.VMEM((t, HEAD), F32), pltpu.VMEM((HEAD, HEAD), F32)],
        operands=(z, z, z, z, hg, nw), parallel=1)


def _hgrn_bwd(dy, z, o, states, hg, nw, w, name, comm=()):
    t = z.shape[0]
    nh = w // HEAD
    nc = t // CHUNK

    def body(dy_ref, q_ref, f_ref, i_ref, g_ref, o_ref, st_ref, hg_ref, nw_ref,
             dq_ref, df_ref, di_ref, dg_ref, dhg_ref, dnw_ref, cum_ref, do_ref, ds_ref):
        lb, s1 = _lower_bound(hg_ref[...])
        row = lax.broadcasted_iota(jnp.int32, (t, HEAD), 0)
        crow = lax.broadcasted_iota(jnp.int32, (CHUNK, HEAD), 0)
        cum_ref[...] = _chunk_cumsum(jnp.log(lb + (1.0 - lb) * _sigmoid(f_ref[...])), row)

        o_ = o_ref[...]
        rstd = lax.rsqrt(jnp.mean(o_ * o_, axis=-1, keepdims=True) + RMS_EPS)
        n = o_ * rstd
        gr = g_ref[...]
        sg = _sigmoid(gr)
        dy_ = dy_ref[...]
        dg_ref[...] = (dy_ * n * nw_ref[...] * (sg * (1.0 + gr * (1.0 - sg)))).astype(BF16)
        dsil = dy_ * gr * sg
        dnw_ref[...] = jnp.sum(dsil * n, axis=0, keepdims=True)
        dn = dsil * nw_ref[...]
        do_ref[...] = rstd * (dn - n * jnp.mean(dn * n, axis=-1, keepdims=True))

        ds_ref[...] = jnp.zeros_like(ds_ref)

        def step(cc, dlb):
            c = nc - 1 - cc
            rows = pl.ds(pl.multiple_of(c * CHUNK, CHUNK), CHUNK)
            qr = q_ref[rows, :]
            sq = _sigmoid(qr)
            q = qr * sq
            sf = _sigmoid(f_ref[rows, :])
            f = lb + (1.0 - lb) * sf
            k = 1.0 - f
            cum = cum_ref[rows, :]
            v = i_ref[rows, :]
            do = do_ref[rows, :]
            last = cum[CHUNK - 1:CHUNK, :]
            eg = jnp.exp(cum)
            eng = jnp.exp(-cum)
            elc = jnp.exp(last - cum)
            qe, ke, kl = q * eg, k * eng, k * elc
            ds = ds_ref[...]
            a = _tril(_dot_nt(qe, ke))
            da = _tril(_dot_nt(do, v))
            di_ref[rows, :] = (_dot_tn(a, do) + _dot_nt(kl, ds)).astype(BF16)
            st = st_ref[c]
            dkl = _dot_nn(v, ds)
            dq = (_dot_nn(do, st) + _dot_nn(da, ke)) * eg
            dk = _dot_tn(da, qe) * eng + dkl * elc
            el = jnp.exp(last)
            ds_ref[...] = ds * el + _dot_tn(do, qe)
            dlast = jnp.sum(kl * dkl, axis=0, keepdims=True) + el * jnp.sum(ds * st.astype(F32), axis=0, keepdims=True)
            x = q * dq - k * dk + jnp.where(crow == CHUNK - 1, dlast, 0.0)
            s = 1
            while s < CHUNK:
                x = x + _shift_up(x, s, crow, CHUNK)
                s *= 2
            df = x / f - dk
            dq_ref[rows, :] = (dq * (sq * (1.0 + qr * (1.0 - sq)))).astype(BF16)
            df_ref[rows, :] = (df * (1.0 - lb) * sf * (1.0 - sf)).astype(BF16)
            return dlb + jnp.sum(df * (1.0 - sf), axis=0, keepdims=True)

        dlb = lax.fori_loop(0, nc, step, jnp.zeros((1, HEAD), F32))
        dlb = dlb * lb * s1
        dhg_ref[0:1, :] = dlb
        dhg_ref[1:2, :] = -dlb

    col = lambda off: pl.BlockSpec((t, HEAD), lambda h: (0, off * nh + h))
    own = pl.BlockSpec((t, HEAD), lambda h: (0, h))
    hsp = pl.BlockSpec((2, HEAD), lambda h: (0, h))
    return _hosted_call(
        body, comm, name=name, grid=(nh,),
        in_specs=[own, col(3), col(4), col(5), col(6), own, pl.BlockSpec((None, nc, HEAD, HEAD), lambda h: (h, 0, 0, 0)),
                  hsp, pl.BlockSpec((1, HEAD), lambda h: (0, 0))],
        out_specs=[own, own, own, own, hsp, pl.BlockSpec((None, 1, HEAD), lambda h: (h, 0, 0))],
        out_shape=[SDS((t, w), BF16)] * 4 + [SDS((2, w), F32), SDS((nh, 1, HEAD), F32)],
        scratch_shapes=[pltpu.VMEM((t, HEAD), F32)] * 2 + [pltpu.VMEM((HEAD, HEAD), F32)],
        operands=(dy, z, z, z, z, o, states, hg, nw), parallel=1)


def _cast_pad(wt, n_pad, meta, sp, name):
    r, n = wt.shape
    g, p, per = meta
    tr = _tile(r, 256, 16)

    def body(sp_ref, w_ref, o_ref):
        if n_pad != n:
            o_ref[...] = jnp.zeros(o_ref.shape, o_ref.dtype)
        o_ref[:, 0:n] = w_ref[...].astype(BF16)

    grid_spec = pltpu.PrefetchScalarGridSpec(
        num_scalar_prefetch=1, grid=(r // tr,), in_specs=[pl.BlockSpec((tr, n), lambda i, sp: (i, 0))],
        out_specs=pl.BlockSpec((None, tr, n_pad), lambda i, sp: (sp[1] // per, ((sp[1] % per) * r) // tr + i, 0)))
    return pl.pallas_call(body, name=name, grid_spec=grid_spec, out_shape=SDS((g, p, n_pad), BF16),
                          compiler_params=_cparams("parallel"))(sp, wt)


def _adamw(wt, g, m, v, name):
    r, n = wt.shape
    ng = g.shape[1]
    tr = _tile(r, max(8, (1 << 18) // max(ng, 1) // 8 * 8), 8)
    c1 = 1.0 / (1.0 - ADAM_B1 ** ADAM_STEP)
    c2 = 1.0 / (1.0 - ADAM_B2 ** ADAM_STEP)

    def body(w_ref, g_ref, m_ref, v_ref, go_ref, d_ref, mo_ref, vo_ref):
        g_ = g_ref[:, 0:n]
        m2 = ADAM_B1 * m_ref[...] + (1.0 - ADAM_B1) * g_
        v2 = ADAM_B2 * v_ref[...] + (1.0 - ADAM_B2) * (g_ * g_)
        go_ref[...] = g_
        mo_ref[...] = m2
        vo_ref[...] = v2
        d_ref[...] = -ADAM_LR * ((m2 * c1) / (jnp.sqrt(v2 * c2) + ADAM_EPS) + ADAM_WD * w_ref[...])

    blk = pl.BlockSpec((tr, n), lambda i: (i, 0))
    return pl.pallas_call(
        body, name=name, grid=(r // tr,), in_specs=[blk, pl.BlockSpec((tr, ng), lambda i: (i, 0)), blk, blk],
        out_specs=[blk] * 4, out_shape=[SDS((r, n), F32)] * 4, compiler_params=_cparams("parallel"),
    )(wt, g, m, v)


def _place():
    x, y, c = lax.axis_index("x"), lax.axis_index("y"), lax.axis_index("c")
    return x, y, c, 2 * x + y


def _chip_dev(k, c):
    return (k // 2, k % 2, c)


def _half(ref, j, h, rows, per):
    return ref.at[j // per, pl.ds((j % per) * rows + h * (rows // 2), rows // 2)]


def _gather_stage(bufs, metas, rows_of, parts, zero_pad):
    nw = len(bufs)
    pad_jobs = [(i, gi) for i in range(nw) if parts[i][0] == 0 and metas[i][1] > metas[i][2] * rows_of[i]
                for gi in range(metas[i][0])]

    def part_of(ref, i, j, h):
        per = metas[i][2]
        p, np_ = parts[i]
        pr = rows_of[i] // 2 // np_
        return ref.at[j // per, pl.ds((j % per) * rows_of[i] + h * (rows_of[i] // 2) + p * pr, pr)]

    def descriptors(ins, outs, sems):
        src, zp, dst = ins[:nw], ins[nw], outs
        pads, send, recv, fsend, frecv = sems
        x, y, c, me = _place()

        def pad(n):
            i, gi = pad_jobs[n]
            extra = metas[i][1] - metas[i][2] * rows_of[i]
            return pltpu.make_async_copy(zp.at[pl.ds(0, extra)], dst[i].at[gi, pl.ds(metas[i][2] * rows_of[i], extra)], pads.at[n])

        def ici(i, r, frm):
            return pltpu.make_async_remote_copy(
                src_ref=part_of(src[i], i, me, c), dst_ref=part_of(dst[i], i, frm, c), send_sem=send.at[i, r - 1],
                recv_sem=recv.at[i, r - 1], device_id=_chip_dev((me + r) % N_CHIPS, c), device_id_type=MESH)

        def d2d(i, r, frm, h):
            blk = part_of(dst[i], i, frm, h)
            return pltpu.make_async_remote_copy(src_ref=blk, dst_ref=blk, send_sem=fsend.at[i, r - 1],
                                                recv_sem=frecv.at[i, r - 1], device_id=(x, y, 1 - c), device_id_type=MESH)

        return pad, ici, d2d, c, me

    def start(ins, outs, sems):
        pad, ici, d2d, c, me = descriptors(ins, outs, sems)
        for n in range(len(pad_jobs)):
            pad(n).start()
        for i in range(nw):
            for r in range(1, N_CHIPS):
                ici(i, r, me).start()

    def finish(ins, outs, sems):
        pad, ici, d2d, c, me = descriptors(ins, outs, sems)
        for i in range(nw):
            for r in range(1, N_CHIPS):
                frm = (me - r) % N_CHIPS
                ici(i, r, frm).wait_recv()
                d2d(i, r, frm, c).start()
        for i in range(nw):
            for r in range(1, N_CHIPS):
                d2d(i, r, (me - r) % N_CHIPS, 1 - c).wait_recv()
        for i in range(nw):
            for r in range(1, N_CHIPS):
                ici(i, r, me).wait_send()
                d2d(i, r, (me - r) % N_CHIPS, c).wait_send()
        for n in range(len(pad_jobs)):
            pad(n).wait()

    return _Stage(ins=list(bufs) + [zero_pad], out_shapes=[SDS(b.shape, b.dtype) for b in bufs],
                  aliases={i: i for i in range(nw)},
                  sems=[pltpu.SemaphoreType.DMA((max(len(pad_jobs), 1),))] + [pltpu.SemaphoreType.DMA((nw, N_CHIPS - 1))] * 4,
                  start=start, finish=finish)


def _gather_small(packed, name):
    r, n = packed.shape

    def body(src, dst, send, recv):
        x, y, c, me = _place()
        dst[me] = src[...]
        cps = []
        for d in range(1, N_CHIPS):
            cp = pltpu.make_async_remote_copy(src_ref=src, dst_ref=dst.at[me], send_sem=send.at[d - 1], recv_sem=recv.at[d - 1],
                                              device_id=_chip_dev((me + d) % N_CHIPS, c), device_id_type=MESH)
            cp.start()
            cps.append(cp)
        for d in range(1, N_CHIPS):
            pltpu.make_async_remote_copy(src_ref=src, dst_ref=dst.at[(me - d) % N_CHIPS], send_sem=send.at[d - 1],
                                         recv_sem=recv.at[d - 1], device_id=_chip_dev((me + d) % N_CHIPS, c),
                                         device_id_type=MESH).wait_recv()
        for cp in cps:
            cp.wait_send()

    return pl.pallas_call(
        body, name=name, in_specs=[VMEM_SPEC], out_specs=VMEM_SPEC, out_shape=SDS((N_CHIPS, r, n), F32),
        scratch_shapes=[pltpu.SemaphoreType.DMA((N_CHIPS - 1,))] * 2,
    )(packed)


def _all_reduce_small(packed, name):
    r, n = packed.shape

    def body(src, out, slots, send, recv):
        x, y, c, me = _place()
        idx = 2 * me + c
        slots[idx] = src[...]
        cps = []

        def peer(d):
            p = (idx + d) % N_DEV
            return (p // 4, (p // 2) % 2, p % 2)

        for d in range(1, N_DEV):
            cp = pltpu.make_async_remote_copy(src_ref=src, dst_ref=slots.at[idx], send_sem=send.at[d - 1], recv_sem=recv.at[d - 1],
                                              device_id=peer(d), device_id_type=MESH)
            cp.start()
            cps.append(cp)
        for d in range(1, N_DEV):
            pltpu.make_async_remote_copy(src_ref=src, dst_ref=slots.at[(idx - d) % N_DEV], send_sem=send.at[d - 1],
                                         recv_sem=recv.at[d - 1], device_id=peer(d), device_id_type=MESH).wait_recv()
        for cp in cps:
            cp.wait_send()
        acc = slots[0]
        for k in range(1, N_DEV):
            acc = acc + slots[k]
        out[...] = acc

    return pl.pallas_call(
        body, name=name, in_specs=[VMEM_SPEC], out_specs=VMEM_SPEC, out_shape=SDS((r, n), F32),
        scratch_shapes=[pltpu.VMEM((N_DEV, r, n), F32)] + [pltpu.SemaphoreType.DMA((N_DEV - 1,))] * 2,
    )(packed)


def _simple_stage(ins, out_shapes, aliases, n_copies, copies):
    def start(ins_, outs, sems):
        for cp in copies(ins_, outs, *sems):
            cp.start()

    def finish(ins_, outs, sems):
        for cp in copies(ins_, outs, *sems):
            cp.wait()

    return _Stage(ins=list(ins), out_shapes=list(out_shapes), aliases=aliases,
                  sems=[pltpu.SemaphoreType.DMA((n_copies,))] * 2, start=start, finish=finish)


def _rs_pair_exchange(grads, metas, rows_of):
    nw = len(grads)

    def copies(src, dst, send, recv):
        x, y, c, me = _place()
        return [pltpu.make_async_remote_copy(
            src_ref=_half(src[i], j, 1 - c, rows_of[i], metas[i][2]), dst_ref=dst[i].at[j], send_sem=send.at[i * N_CHIPS + j],
            recv_sem=recv.at[i * N_CHIPS + j], device_id=(x, y, 1 - c), device_id_type=MESH)
            for i in range(nw) for j in range(N_CHIPS)]

    out_shapes = [SDS((N_CHIPS, rows_of[i] // 2, g.shape[2]), F32) for i, g in enumerate(grads)]
    return _simple_stage(grads, out_shapes, {}, nw * N_CHIPS, copies)


def _rs_pair_add(g, got, meta, rows, sp, name):
    per = meta[2]
    n = g.shape[2]
    hr = rows // 2
    tr = _tile(hr, max(16, (3 << 19) // n // 16 * 16), 16)

    def body(sp_ref, g_ref, got_ref, snd_ref, own_ref):
        j = pl.program_id(1)
        s = g_ref[...] + got_ref[...]
        snd_ref[...] = s.astype(BF16)

        @pl.when(j == sp_ref[1])
        def _():
            own_ref[...] = s

    grid_spec = pltpu.PrefetchScalarGridSpec(
        num_scalar_prefetch=1, grid=(hr // tr, N_CHIPS),
        in_specs=[pl.BlockSpec((None, tr, n), lambda i, j, sp: (j // per, ((j % per) * rows + sp[0] * hr) // tr + i, 0)),
                  pl.BlockSpec((None, tr, n), lambda i, j, sp: (j, i, 0))],
        out_specs=[pl.BlockSpec((None, tr, n), lambda i, j, sp: (j, i, 0)), pl.BlockSpec((tr, n), lambda i, j, sp: (i, 0))])
    return pl.pallas_call(
        body, name=name, grid_spec=grid_spec, out_shape=[SDS((N_CHIPS, hr, n), BF16), SDS((hr, n), F32)],
        compiler_params=_cparams("parallel", "arbitrary"),
    )(sp, g, got)


def _rs_chip_exchange(sends):
    nw = len(sends)

    def copies(src, dst, send, recv):
        x, y, c, me = _place()
        return [pltpu.make_async_remote_copy(
            src_ref=src[i].at[(me + r) % N_CHIPS], dst_ref=dst[i].at[r - 1], send_sem=send.at[i * (N_CHIPS - 1) + r - 1],
            recv_sem=recv.at[i * (N_CHIPS - 1) + r - 1], device_id=_chip_dev((me + r) % N_CHIPS, c), device_id_type=MESH)
            for i in range(nw) for r in range(1, N_CHIPS)]

    out_shapes = [SDS((N_CHIPS - 1,) + s.shape[1:], BF16) for s in sends]
    return _simple_stage(sends, out_shapes, {}, nw * (N_CHIPS - 1), copies)


def _rs_chip_add(own, got, sp, name):
    hr, n = own.shape
    tr = _tile(hr, max(16, (3 << 19) // n // 16 * 16), 16)

    def body(sp_ref, own_ref, got_ref, o_ref):
        acc = own_ref[...]
        for r in range(N_CHIPS - 1):
            acc = acc + got_ref[r].astype(F32)
        o_ref[...] = acc

    grid_spec = pltpu.PrefetchScalarGridSpec(
        num_scalar_prefetch=1, grid=(hr // tr,),
        in_specs=[pl.BlockSpec((tr, n), lambda i, sp: (i, 0)), pl.BlockSpec((N_CHIPS - 1, tr, n), lambda i, sp: (0, i, 0))],
        out_specs=pl.BlockSpec((tr, n), lambda i, sp: (sp[0] * (hr // tr) + i, 0)))
    return pl.pallas_call(body, name=name, grid_spec=grid_spec, out_shape=SDS((2 * hr, n), F32),
                          compiler_params=_cparams("parallel"))(sp, own, got)


def _rs_pair_share(blocks):
    nw = len(blocks)

    def copies(src, dst, send, recv):
        x, y, c, me = _place()
        cps = []
        for i in range(nw):
            hr = src[i].shape[0] // 2
            cps.append(pltpu.make_async_remote_copy(
                src_ref=src[i].at[pl.ds(c * hr, hr)], dst_ref=dst[i].at[pl.ds(c * hr, hr)], send_sem=send.at[i],
                recv_sem=recv.at[i], device_id=(x, y, 1 - c), device_id_type=MESH))
        return cps

    return _simple_stage(blocks, [SDS(b.shape, b.dtype) for b in blocks], {i: i for i in range(nw)}, nw, copies)


def kernel(x, p, ln_g, ln_b, ffn1_w_in, ffn1_w_out, mix_w_in, conv_w, hg_lower_bound, hg_norm_w, branch_w_conv, branch_w_hgrn, mix_w_out, ffn2_w_in, ffn2_w_out, ple_w_gate, ple_w_proj, loss_target, m_ln_g, m_ln_b, m_ffn1_w_in, m_ffn1_w_out, m_mix_w_in, m_conv_w, m_hg_lower_bound, m_hg_norm_w, m_branch_w_conv, m_branch_w_hgrn, m_mix_w_out, m_ffn2_w_in, m_ffn2_w_out, m_ple_w_gate, m_ple_w_proj, v_ln_g, v_ln_b, v_ffn1_w_in, v_ffn1_w_out, v_mix_w_in, v_conv_w, v_hg_lower_bound, v_hg_norm_w, v_branch_w_conv, v_branch_w_hgrn, v_mix_w_out, v_ffn2_w_in, v_ffn2_w_out, v_ple_w_gate, v_ple_w_proj):
    assert ln_g.shape[0] == DEPTH and x.shape[0] == 1 and p.shape[:2] == (1, 1)
    t, d = x.shape[1], x.shape[2]
    w = d // 2
    x0 = x.reshape(t, d)
    pe = p.reshape(t, p.shape[-1])
    target = loss_target.reshape(t, d)
    cx, cy, cc = lax.axis_index("x"), lax.axis_index("y"), lax.axis_index("c")
    chip = 2 * cx + cy
    sp = jnp.stack([cc, chip]).astype(jnp.int32)

    big = dict(ffn1_w_in=ffn1_w_in[0], ffn1_w_out=ffn1_w_out[0], mix_w_in=mix_w_in[0], branch_w_conv=branch_w_conv[0],
               branch_w_hgrn=branch_w_hgrn[0], mix_w_out=mix_w_out[0], ffn2_w_in=ffn2_w_in[0], ffn2_w_out=ffn2_w_out[0],
               ple_w_gate=ple_w_gate[0], ple_w_proj=ple_w_proj[0])
    moments = dict(ffn1_w_in=(m_ffn1_w_in, v_ffn1_w_in), ffn1_w_out=(m_ffn1_w_out, v_ffn1_w_out), mix_w_in=(m_mix_w_in, v_mix_w_in),
                   branch_w_conv=(m_branch_w_conv, v_branch_w_conv), branch_w_hgrn=(m_branch_w_hgrn, v_branch_w_hgrn),
                   mix_w_out=(m_mix_w_out, v_mix_w_out), ffn2_w_in=(m_ffn2_w_in, v_ffn2_w_in), ffn2_w_out=(m_ffn2_w_out, v_ffn2_w_out),
                   ple_w_gate=(m_ple_w_gate, v_ple_w_gate), ple_w_proj=(m_ple_w_proj, v_ple_w_proj))
    names = list(big)

    n_loc = ffn1_w_in.shape[-1]
    n_pad = -(-n_loc // LANES) * LANES
    assert mix_w_in.shape[-1] % LANES == 0 and ffn1_w_out.shape[1] * 2 == n_loc
    pad_cols = dict(ffn1_w_in=n_pad, ffn2_w_in=n_pad)
    meta = {k: (N_CHIPS, big[k].shape[0], 1) for k in names}
    meta["ffn1_w_out"] = meta["ffn2_w_out"] = (2, n_pad, 2)
    rows = {k: big[k].shape[0] for k in names}
    wbuf = {k: _cast_pad(big[k], pad_cols.get(k, big[k].shape[1]), meta[k], sp, "cast_" + k) for k in names}
    zero_pad = jnp.zeros((max(n_pad - n_loc, 16), d), BF16)

    def gather(*items):
        ks = [k for k, _, _ in items]
        return _gather_stage([wbuf[k] for k in ks], [meta[k] for k in ks], [rows[k] for k in ks], [(p_, n_) for _, p_, n_ in items],
                             zero_pad), ks

    def gathered(ks, outs):
        wbuf.update(zip(ks, outs))

    def w3(k):
        return wbuf[k]

    def w2(k):
        return wbuf[k].reshape(-1, wbuf[k].shape[2])

    dq, wq = d // N_CHIPS, w // N_CHIPS
    small = jnp.concatenate([ln_g[0], ln_b[0], jnp.pad(conv_w[0], ((0, 5), (0, dq - wq)))], axis=0)
    small = _gather_small(small, "gather_small")
    lng = small[:, 0:4, :].transpose(1, 0, 2).reshape(4, 1, d)
    lnb = small[:, 4:8, :].transpose(1, 0, 2).reshape(4, 1, d)
    cw = small[:, 8:11, :wq].transpose(1, 0, 2).reshape(3, w)
    hg = hg_lower_bound
    nw_ = hg_norm_w

    st, ks = gather(("ffn1_w_in", 0, 1))
    gathered(ks, _run_stages([st], "gather_first")[0])
    st, ks = gather(("ffn1_w_out", 0, 1), ("mix_w_in", 0, 2))
    z1, got = _mm(x0, w3("ffn1_w_in"), name="ffn1_in", b_blocked=True, out_dtype=BF16, comm=[st])
    gathered(ks, got)
    h1 = _swiglu_fwd(z1, "ffn1_act")
    st, ks = gather(("mix_w_in", 1, 2))
    y1, got = _mm(h1, w2("ffn1_w_out"), name="ffn1_out", tm=1024, tn=1024, tk=2816, comm=[st])
    gathered(ks, got)
    r1, x1, x1b = _ln_fwd(x0, y1, lng[0], lnb[0], 0.5, "ln0")
    st, ks = gather(("branch_w_conv", 0, 1), ("branch_w_hgrn", 0, 1), ("mix_w_out", 0, 1), ("ffn2_w_in", 0, 2))
    z, got = _mm(x1b, w3("mix_w_in"), name="mix_in", b_blocked=True, comm=[st])
    gathered(ks, got)
    ya = _conv_fwd(z, cw, w, "conv_fwd")
    st, ks = gather(("ffn2_w_in", 1, 2), ("ffn2_w_out", 0, 1), ("ple_w_gate", 0, 1), ("ple_w_proj", 0, 1))
    (yb, o_h, states), got = _hgrn_fwd(z, hg, nw_, w, "hgrn_fwd", comm=[st])
    gathered(ks, got)
    ma = _mm(ya, w3("branch_w_conv"), name="branch_conv", b_blocked=True, tn=512)
    mb = _mm(yb, w3("branch_w_hgrn"), name="branch_hgrn", b_blocked=True, tn=512)
    merged = _merge_fwd(z, ma, mb, w, "merge_fwd")
    y2 = _mm(merged, w2("mix_w_out"), name="mix_out", tn=1024)
    r2, x2, x2b = _ln_fwd(x1, y2, lng[1], lnb[1], 1.0, "ln1")
    z3 = _mm(x2b, w3("ffn2_w_in"), name="ffn2_in", b_blocked=True, out_dtype=BF16)
    h3 = _swiglu_fwd(z3, "ffn2_act")
    y3 = _mm(h3, w2("ffn2_w_out"), name="ffn2_out", tm=1024, tn=1024, tk=2816)
    r3, x3, x3b = _ln_fwd(x2, y3, lng[2], lnb[2], 0.5, "ln2")
    gp = _mm(x3b, w2("ple_w_gate"), name="ple_gate", tn=1024)
    pp = _mm(pe, w3("ple_w_proj"), name="ple_proj", b_blocked=True, tn=512)
    dr4, dgp, dpp, dg3, db3, sq = _tail(x3, gp, pp, lng[3], lnb[3], target, "tail")

    grads, sends, owns, blocks, outs = {}, {}, {}, {}, {}

    def pair_exchange(*ks):
        return _rs_pair_exchange([grads[k] for k in ks], [meta[k] for k in ks], [rows[k] for k in ks])

    def pair_add(ks, got):
        for k, g_ in zip(ks, got):
            sends[k], owns[k] = _rs_pair_add(grads[k], g_, meta[k], rows[k], sp, "rs_pair_add_" + k)

    def chip_exchange(*ks):
        return _rs_chip_exchange([sends[k] for k in ks])

    def chip_add(ks, got):
        for k, g_ in zip(ks, got):
            blocks[k] = _rs_chip_add(owns[k], g_, sp, "rs_chip_add_" + k)

    def pair_share(*ks):
        return _rs_pair_share([blocks[k] for k in ks])

    def update(ks, full):
        for k, g_ in zip(ks, full):
            m_, v_ = moments[k]
            outs[k] = [a.reshape(m_.shape) for a in _adamw(big[k], g_, m_[0], v_[0], "adamw_" + k)]

    ple = ("ple_w_gate", "ple_w_proj")
    mixo = ("mix_w_out", "branch_w_conv", "branch_w_hgrn")
    dx3m = _mm(dgp, w2("ple_w_gate"), name="d_ple_gate_x", tb=True, tn=1024, tk=2048)
    grads["ple_w_gate"] = _mm(x3b, dgp, name="d_ple_gate_w", ta=True, tm=1024, tk=2048, tn=1024).reshape(N_CHIPS, -1, d)
    grads["ple_w_proj"] = _mm(pe, dpp, name="d_ple_proj_w", ta=True, out_blocked=N_CHIPS, tk=2048, tn=512)
    dr3, dy3b, dg2, db2 = _ln_bwd(dr4, dx3m, r3, lng[2], 0.5, "ln2_bwd")
    dh3, got = _mm(dy3b, w2("ffn2_w_out"), name="d_ffn2_out_x", tb=True, out_dtype=BF16, tn=1408, tk=2048,
                   comm=[pair_exchange(*ple)])
    pair_add(ple, got)
    g_, got = _mm(h3, dy3b, name="d_ffn2_out_w", ta=True, tm=1408, tk=2048, tn=1024, comm=[chip_exchange(*ple)])
    grads["ffn2_w_out"] = g_.reshape(2, n_pad, d)
    chip_add(ple, got)
    dz3 = _swiglu_bwd(dh3, z3, "ffn2_act_bwd")
    dx2m, got, full = _mm(dz3, w3("ffn2_w_in"), name="d_ffn2_in_x", tb=True, b_blocked=True, tm=1024, tn=1024, tk=2816,
                          comm=[pair_exchange("ffn2_w_out"), pair_share(*ple)])
    pair_add(["ffn2_w_out"], got)
    update(ple, full)
    grads["ffn2_w_in"], got = _mm(x2b, dz3, name="d_ffn2_in_w", ta=True, out_blocked=N_CHIPS, tk=2048, comm=[chip_exchange("ffn2_w_out")])
    chip_add(["ffn2_w_out"], got)
    dr2, dy2b, dg1, db1 = _ln_bwd(dr3, dx2m, r2, lng[1], 1.0, "ln1_bwd")
    dmer, got = _mm(dy2b, w2("mix_w_out"), name="d_mix_out_x", tb=True, tn=1024, tk=2048, comm=[pair_exchange("ffn2_w_in")])
    pair_add(["ffn2_w_in"], got)
    g_, full = _mm(merged, dy2b, name="d_mix_out_w", ta=True, tm=1024, tk=2048, tn=1024, comm=[pair_share("ffn2_w_out")])
    grads["mix_w_out"] = g_.reshape(N_CHIPS, -1, d)
    update(["ffn2_w_out"], full)
    dma, dmb, dgc, dgh = _merge_bwd(dmer, z, ma, mb, w, "merge_bwd")
    dya = _mm(dma, w3("branch_w_conv"), name="d_branch_conv_x", tb=True, b_blocked=True, tn=1024, tk=512)
    dyb = _mm(dmb, w3("branch_w_hgrn"), name="d_branch_hgrn_x", tb=True, b_blocked=True, tn=1024, tk=512)
    grads["branch_w_conv"] = _mm(ya, dma, name="d_branch_conv_w", ta=True, out_blocked=N_CHIPS, tm=1024, tk=2048, tn=512)
    grads["branch_w_hgrn"] = _mm(yb, dmb, name="d_branch_hgrn_w", ta=True, out_blocked=N_CHIPS, tm=1024, tk=2048, tn=512)
    dbg, dcg, dhc, dcw = _conv_bwd(dya, z, cw, w, "conv_bwd")
    (dq_, df_, di_, dgr_, dhg, dnw), got2, got = _hgrn_bwd(dyb, z, o_h, states, hg, nw_, w, "hgrn_bwd",
                                                            comm=[chip_exchange("ffn2_w_in"), pair_exchange(*mixo)])
    chip_add(["ffn2_w_in"], got2)
    pair_add(mixo, got)
    dz = jnp.concatenate([dbg, dcg, dhc, dq_, df_, di_, dgr_, dgc, dgh], axis=1)
    dx1m, full, got = _mm(dz, w3("mix_w_in"), name="d_mix_in_x", tb=True, b_blocked=True, tm=1024, tn=1024, tk=2816,
                          comm=[pair_share("ffn2_w_in"), chip_exchange(*mixo)])
    update(["ffn2_w_in"], full)
    chip_add(mixo, got)
    grads["mix_w_in"], full = _mm(x1b, dz, name="d_mix_in_w", ta=True, out_blocked=N_CHIPS, tk=2048, comm=[pair_share(*mixo)])
    update(mixo, full)
    dr1, dy1b, dg0, db0 = _ln_bwd(dr2, dx1m, r1, lng[0], 0.5, "ln0_bwd")
    dh1, got = _mm(dy1b, w2("ffn1_w_out"), name="d_ffn1_out_x", tb=True, out_dtype=BF16, tn=1408, tk=2048,
                   comm=[pair_exchange("mix_w_in")])
    pair_add(["mix_w_in"], got)
    grads["ffn1_w_out"] = _mm(h1, dy1b, name="d_ffn1_out_w", ta=True, tm=1408, tk=2048, tn=1024).reshape(2, n_pad, d)
    dz1 = _swiglu_bwd(dh1, z1, "ffn1_act_bwd")
    grads["ffn1_w_in"], got2, got = _mm(x0, dz1, name="d_ffn1_in_w", ta=True, out_blocked=N_CHIPS, tk=2048,
                                        comm=[chip_exchange("mix_w_in"), pair_exchange("ffn1_w_out")])
    chip_add(["mix_w_in"], got2)
    pair_add(["ffn1_w_out"], got)
    dx0m, full, got2, got = _mm(dz1, w3("ffn1_w_in"), name="d_ffn1_in_x", tb=True, b_blocked=True, tm=1024, tn=1024, tk=2816,
                                comm=[pair_share("mix_w_in"), chip_exchange("ffn1_w_out"), pair_exchange("ffn1_w_in")])
    update(["mix_w_in"], full)
    chip_add(["ffn1_w_out"], got2)
    pair_add(["ffn1_w_in"], got)
    grad_x = _residual_out(dr1, dx0m, "grad_x").reshape(x.shape)
    got2, full = _run_stages([chip_exchange("ffn1_w_in"), pair_share("ffn1_w_out")], "rs_tail_chip")
    chip_add(["ffn1_w_in"], got2)
    update(["ffn1_w_out"], full)
    update(["ffn1_w_in"], _run_stages([pair_share("ffn1_w_in")], "rs_tail_pair")[0])

    pack = jnp.concatenate([
        dg0, dg1, dg2, dg3, db0, db1, db2, db3,
        jnp.pad(dcw, ((0, 0), (0, d - w))), jnp.pad(dhg, ((0, 0), (0, d - w))),
        jnp.pad(jnp.sum(dnw, axis=0), ((0, 0), (0, d - HEAD))), sq], axis=0)
    pack = _all_reduce_small(jnp.pad(pack, ((0, 1), (0, 0))), "reduce_small")
    loss = (0.5 / d) * jnp.sum(pack[14])
    g_ln_g = lax.dynamic_slice_in_dim(pack[0:4], chip * dq, dq, axis=1)
    g_ln_b = lax.dynamic_slice_in_dim(pack[4:8], chip * dq, dq, axis=1)
    g_conv = lax.dynamic_slice_in_dim(pack[8:11, :w], chip * wq, wq, axis=1)
    g_hg = pack[11:13, :w]
    g_nw = pack[13:14, :HEAD]

    small_w = dict(ln_g=(ln_g, g_ln_g, m_ln_g, v_ln_g), ln_b=(ln_b, g_ln_b, m_ln_b, v_ln_b),
                   conv_w=(conv_w, g_conv, m_conv_w, v_conv_w), hg_lower_bound=(hg_lower_bound, g_hg, m_hg_lower_bound, v_hg_lower_bound),
                   hg_norm_w=(hg_norm_w, g_nw, m_hg_norm_w, v_hg_norm_w))
    for k, (w_, g_, m_, v_) in small_w.items():
        s2 = (-1, w_.shape[-1])
        outs[k] = [a.reshape(w_.shape) for a in _adamw(w_.reshape(s2), g_.reshape(s2), m_.reshape(s2), v_.reshape(s2), "adamw_" + k)]

    order = ["ln_g", "ln_b", "ffn1_w_in", "ffn1_w_out", "mix_w_in", "conv_w", "hg_lower_bound", "hg_norm_w", "branch_w_conv",
             "branch_w_hgrn", "mix_w_out", "ffn2_w_in", "ffn2_w_out", "ple_w_gate", "ple_w_proj"]
    return (loss, grad_x, *[outs[k][0] for k in order], *[outs[k][1] for k in order], *[outs[k][2] for k in order],
            *[outs[k][3] for k in order])
```

```python
import collections
import functools

import jax
import jax.numpy as jnp
from jax import lax
from jax.experimental import pallas as pl
from jax.experimental.pallas import tpu as pltpu

F32 = jnp.float32
BF16 = jnp.bfloat16
MESH = pl.DeviceIdType.MESH
ANY = pl.BlockSpec(memory_space=pl.ANY)
VMEM_SPEC = pl.BlockSpec(memory_space=pltpu.VMEM)
SDS = jax.ShapeDtypeStruct

DEPTH = 1
ALPHA = (2.0 * DEPTH) ** 0.25
LN_EPS = 1e-5
RMS_EPS = 1e-6
CHUNK = 32
HEAD = 128
ADAM_LR, ADAM_B1, ADAM_B2, ADAM_EPS, ADAM_WD, ADAM_STEP = 0.001, 0.9, 0.999, 1e-08, 0.01, 10

LANES = 128
N_CHIPS = 4
N_DEV = 8
VMEM_LIMIT = 52 * 1024 * 1024


def _cparams(*sem):
    if sem:
        return pltpu.CompilerParams(dimension_semantics=sem, vmem_limit_bytes=VMEM_LIMIT)
    return pltpu.CompilerParams(vmem_limit_bytes=VMEM_LIMIT)


def _tile(n, target, mult):
    best = None
    for t in range(mult, min(n, target) + 1, mult):
        if n % t == 0:
            best = t
    return best if best is not None else n


def _sigmoid(x):
    return 1.0 / (1.0 + jnp.exp(-x))


_Stage = collections.namedtuple("_Stage", "ins out_shapes aliases sems start finish")


def _hosted_call(compute, stages, *, name, grid, in_specs, out_specs, out_shape, scratch_shapes, operands, parallel,
                 prefetch=None):
    n_cmp, n_out, n_scr = len(in_specs), len(out_specs), len(scratch_shapes)
    n_in = n_cmp
    n_pre = int(prefetch is not None)
    c_in = [len(s.ins) for s in stages]
    c_out = [len(s.out_shapes) for s in stages]
    c_sem = [len(s.sems) for s in stages]
    aliases = {}
    for si, s in enumerate(stages):
        for a_in, a_out in s.aliases.items():
            aliases[n_pre + n_in + sum(c_in[:si]) + a_in] = n_out + sum(c_out[:si]) + a_out

    def body(*refs):
        refs = refs[n_pre:]
        ins = refs[:n_cmp]
        cins = refs[n_in:n_in + sum(c_in)]
        outs = refs[n_in + sum(c_in):n_in + sum(c_in) + n_out]
        couts = refs[n_in + sum(c_in) + n_out:n_in + sum(c_in) + n_out + sum(c_out)]
        scr = refs[n_in + sum(c_in) + n_out + sum(c_out):][:n_scr]
        sems = refs[n_in + sum(c_in) + n_out + sum(c_out) + n_scr:]

        def stage_refs(si):
            return (cins[sum(c_in[:si]):sum(c_in[:si + 1])], couts[sum(c_out[:si]):sum(c_out[:si + 1])],
                    sems[sum(c_sem[:si]):sum(c_sem[:si + 1])])

        if stages:
            first = functools.reduce(jnp.logical_and, [pl.program_id(ax) == 0 for ax in range(len(grid))])
            last = functools.reduce(jnp.logical_and, [pl.program_id(ax) == grid[ax] - 1 for ax in range(len(grid))])

            @pl.when(first)
            def _():
                for si, s in enumerate(stages):
                    s.start(*stage_refs(si))

        compute(*ins, *outs, *scr)
        if stages:
            @pl.when(last)
            def _():
                for si, s in enumerate(stages):
                    s.finish(*stage_refs(si))

    sem = ("arbitrary",) * len(grid) if stages else ("parallel",) * parallel + ("arbitrary",) * (len(grid) - parallel)
    all_in = list(in_specs) + [ANY] * (n_in - n_cmp + sum(c_in))
    all_out = list(out_specs) + [ANY] * sum(c_out)
    all_scr = list(scratch_shapes) + [q for s in stages for q in s.sems]
    all_shape = list(out_shape) + [o for s in stages for o in s.out_shapes]
    args = list(operands) + [a for s in stages for a in s.ins]
    if prefetch is None:
        res = pl.pallas_call(body, name=name, grid=grid, in_specs=all_in, out_specs=all_out, out_shape=all_shape,
                             input_output_aliases=aliases, scratch_shapes=all_scr, compiler_params=_cparams(*sem))(*args)
    else:
        grid_spec = pltpu.PrefetchScalarGridSpec(num_scalar_prefetch=1, grid=grid, in_specs=all_in, out_specs=all_out,
                                                 scratch_shapes=all_scr)
        res = pl.pallas_call(body, name=name, grid_spec=grid_spec, out_shape=all_shape, input_output_aliases=aliases,
                             compiler_params=_cparams(*sem))(prefetch, *args)
    main = res[0] if n_out == 1 else list(res[:n_out])
    if not stages:
        return main
    rest = res[n_out:]
    return (main, *[list(rest[sum(c_out[:si]):sum(c_out[:si + 1])]) for si in range(len(stages))])


def _run_stages(stages, name):
    def body(*refs):
        n_i = sum(len(s.ins) for s in stages)
        n_o = sum(len(s.out_shapes) for s in stages)
        cins, couts, sems = refs[:n_i], refs[n_i:n_i + n_o], refs[n_i + n_o:]
        pos = [0, 0, 0]
        parts = []
        for s in stages:
            parts.append((cins[pos[0]:pos[0] + len(s.ins)], couts[pos[1]:pos[1] + len(s.out_shapes)], sems[pos[2]:pos[2] + len(s.sems)]))
            pos = [pos[0] + len(s.ins), pos[1] + len(s.out_shapes), pos[2] + len(s.sems)]
        for s, p_ in zip(stages, parts):
            s.start(*p_)
        for s, p_ in zip(stages, parts):
            s.finish(*p_)

    aliases, ni, no = {}, 0, 0
    for s in stages:
        for a_in, a_out in s.aliases.items():
            aliases[ni + a_in] = no + a_out
        ni, no = ni + len(s.ins), no + len(s.out_shapes)
    res = pl.pallas_call(
        body, name=name, in_specs=[ANY] * ni, out_specs=[ANY] * no, out_shape=[o for s in stages for o in s.out_shapes],
        input_output_aliases=aliases, scratch_shapes=[q for s in stages for q in s.sems],
    )(*[a for s in stages for a in s.ins])
    out, pos = [], 0
    for s in stages:
        out.append(list(res[pos:pos + len(s.out_shapes)]))
        pos += len(s.out_shapes)
    return out


def _mm(a, b, *, name, ta=False, tb=False, b_blocked=False, out_blocked=0, out_dtype=F32,
        tm=512, tn=1408, tk=2048, comm=(), half=None):
    if ta:
        kd, m = a.shape
    else:
        m, kd = a.shape
    if b_blocked and not tb:
        g, kb, nb = b.shape
        assert kb == kd
        n = g * nb
        tn = _tile(nb, tn, LANES)
        tk = _tile(kd, tk, LANES)
        per_n = nb // tn
        b_spec = pl.BlockSpec((None, tk, tn), lambda i, j, k, *s: (j // per_n, k, j % per_n))
    elif b_blocked and tb:
        g, n, kb = b.shape
        assert g * kb == kd
        tn = _tile(n, tn, LANES)
        tk = _tile(kb, tk, LANES)
        per_k = kb // tk
        b_spec = pl.BlockSpec((None, tn, tk), lambda i, j, k, *s: (k // per_k, j, k % per_k))
    elif tb:
        n, kb = b.shape
        assert kb == kd
        tn = _tile(n, tn, LANES)
        tk = _tile(kd, tk, LANES)
        b_spec = pl.BlockSpec((tn, tk), lambda i, j, k, *s: (j, k))
    else:
        kb, n = b.shape
        assert kb == kd
        tn = _tile(n // out_blocked if out_blocked else n, tn, LANES)
        per_o = (n // out_blocked) // tn if out_blocked else None
        tk = _tile(kd, tk, LANES)
        b_spec = pl.BlockSpec((tk, tn), lambda i, j, k, *s: (k, j))
    m_run = m // 2 if half else m
    tm = _tile(m_run, tm, LANES if ta else 8)

    def row(i, s):
        if not half:
            return i
        h = 1 - s[0][0] if half[1] else s[0][0]
        return h * (m_run // tm) + i

    if ta:
        a_spec = pl.BlockSpec((tk, tm), lambda i, j, k, *s: (k, row(i, s)))
    else:
        a_spec = pl.BlockSpec((tm, tk), lambda i, j, k, *s: (row(i, s), k))
    if out_blocked:
        assert not b_blocked and not tb
        o_spec = pl.BlockSpec((None, tm, tn), lambda i, j, k, *s: (j // per_o, row(i, s), j % per_o))
        o_shape = SDS((out_blocked, m, n // out_blocked), out_dtype)
    else:
        o_spec = pl.BlockSpec((tm, tn), lambda i, j, k, *s: (row(i, s), j))
        o_shape = SDS((m, n), out_dtype)
    nk = kd // tk
    dn = (((0 if ta else 1,), (1 if tb else 0,)), ((), ()))
    grid = (m_run // tm, n // tn, nk)

    def compute(a_ref, b_ref, o_ref, acc_ref):
        part = lax.dot_general(a_ref[...].astype(BF16), b_ref[...].astype(BF16), dn, preferred_element_type=F32)
        if nk == 1:
            o_ref[...] = part.astype(o_ref.dtype)
        else:
            k = pl.program_id(2)

            @pl.when(k == 0)
            def _():
                acc_ref[...] = part

            @pl.when(k > 0)
            def _():
                acc_ref[...] += part

            @pl.when(k == nk - 1)
            def _():
                o_ref[...] = acc_ref[...].astype(o_ref.dtype)

    return _hosted_call(compute, comm, name=name, grid=grid, in_specs=[a_spec, b_spec], out_specs=[o_spec], out_shape=[o_shape],
                        scratch_shapes=[pltpu.VMEM((tm, tn), F32)], operands=(a, b), parallel=2,
                        prefetch=half[0] if half else None)


def _swiglu_fwd(z, name):
    t, n = z.shape
    n2 = n // 2
    tr = _tile(t, 128, 16)

    def body(a_ref, u_ref, o_ref):
        a = a_ref[...].astype(F32)
        o_ref[...] = (a * _sigmoid(a) * u_ref[...].astype(F32)).astype(o_ref.dtype)

    return pl.pallas_call(
        body, name=name, grid=(t // tr,),
        in_specs=[pl.BlockSpec((tr, n2), lambda i: (i, 0)), pl.BlockSpec((tr, n2), lambda i: (i, 1))],
        out_specs=pl.BlockSpec((tr, n2), lambda i: (i, 0)), out_shape=SDS((t, n2), BF16),
        compiler_params=_cparams("parallel"),
    )(z, z)


def _swiglu_bwd(dh, z, name):
    t, n = z.shape
    n2 = n // 2
    tr = _tile(t, 128, 16)

    def body(dh_ref, a_ref, u_ref, o_ref):
        a = a_ref[...].astype(F32)
        dh_ = dh_ref[...].astype(F32)
        s = _sigmoid(a)
        o_ref[:, 0:n2] = (dh_ * u_ref[...].astype(F32) * (s * (1.0 + a * (1.0 - s)))).astype(o_ref.dtype)
        o_ref[:, n2:n] = (dh_ * a * s).astype(o_ref.dtype)

    return pl.pallas_call(
        body, name=name, grid=(t // tr,),
        in_specs=[pl.BlockSpec((tr, n2), lambda i: (i, 0)), pl.BlockSpec((tr, n2), lambda i: (i, 0)),
                  pl.BlockSpec((tr, n2), lambda i: (i, 1))],
        out_specs=pl.BlockSpec((tr, n), lambda i: (i, 0)), out_shape=SDS((t, n), BF16),
        compiler_params=_cparams("parallel"),
    )(dh, z, z)


def _ln_stats(r):
    mu = jnp.mean(r, axis=-1, keepdims=True)
    xc = r - mu
    var = jnp.mean(xc * xc, axis=-1, keepdims=True)
    return xc * lax.rsqrt(var + LN_EPS)


def _ln_fwd(xp, y, g, b, scale, name):
    t, d = xp.shape
    tr = _tile(t, 256, 16)

    def body(xp_ref, y_ref, g_ref, b_ref, r_ref, x_ref, xb_ref):
        r = ALPHA * xp_ref[...] + scale * y_ref[...]
        x = _ln_stats(r) * g_ref[...] + b_ref[...]
        r_ref[...] = r
        x_ref[...] = x
        xb_ref[...] = x.astype(BF16)

    row = pl.BlockSpec((tr, d), lambda i: (i, 0))
    vec = pl.BlockSpec((1, d), lambda i: (0, 0))
    return pl.pallas_call(
        body, name=name, grid=(t // tr,), in_specs=[row, row, vec, vec], out_specs=[row, row, row],
        out_shape=[SDS((t, d), F32), SDS((t, d), F32), SDS((t, d), BF16)], compiler_params=_cparams("parallel"),
    )(xp, y, g, b)


def _ln_bwd(dra, dxm, r, g, scale, name):
    t, d = r.shape
    tr = _tile(t, 256, 16)

    def body(dra_ref, dxm_ref, r_ref, g_ref, dr_ref, dyb_ref, dg_ref, db_ref):
        i = pl.program_id(0)
        dx = ALPHA * dra_ref[...] + dxm_ref[...]
        rr = r_ref[...]
        mu = jnp.mean(rr, axis=-1, keepdims=True)
        xc = rr - mu
        rstd = lax.rsqrt(jnp.mean(xc * xc, axis=-1, keepdims=True) + LN_EPS)
        xh = xc * rstd
        dxh = dx * g_ref[...]
        dr = rstd * (dxh - jnp.mean(dxh, axis=-1, keepdims=True) - xh * jnp.mean(dxh * xh, axis=-1, keepdims=True))
        dr_ref[...] = dr
        dyb_ref[...] = (scale * dr).astype(BF16)
        dg = jnp.sum(dx * xh, axis=0, keepdims=True)
        db = jnp.sum(dx, axis=0, keepdims=True)

        @pl.when(i == 0)
        def _():
            dg_ref[...] = dg
            db_ref[...] = db

        @pl.when(i > 0)
        def _():
            dg_ref[...] += dg
            db_ref[...] += db

    row = pl.BlockSpec((tr, d), lambda i: (i, 0))
    vec = pl.BlockSpec((1, d), lambda i: (0, 0))
    return pl.pallas_call(
        body, name=name, grid=(t // tr,), in_specs=[row, row, row, vec], out_specs=[row, row, vec, vec],
        out_shape=[SDS((t, d), F32), SDS((t, d), BF16), SDS((1, d), F32), SDS((1, d), F32)],
        compiler_params=_cparams("arbitrary"),
    )(dra, dxm, r, g)


def _tail(x3, gp, pp, g, b, target, name):
    t, d = x3.shape
    tr = _tile(t, 256, 16)

    def body(x3_ref, gp_ref, pp_ref, g_ref, b_ref, tg_ref, dr_ref, dgp_ref, dpp_ref, dg_ref, db_ref, sq_ref):
        i = pl.program_id(0)
        gate = _sigmoid(gp_ref[...])
        pp_ = pp_ref[...]
        r = ALPHA * x3_ref[...] + gate * pp_
        mu = jnp.mean(r, axis=-1, keepdims=True)
        xc = r - mu
        rstd = lax.rsqrt(jnp.mean(xc * xc, axis=-1, keepdims=True) + LN_EPS)
        xh = xc * rstd
        err = xh * g_ref[...] + b_ref[...] - tg_ref[...]
        dx = err * (1.0 / d)
        dxh = dx * g_ref[...]
        dr = rstd * (dxh - jnp.mean(dxh, axis=-1, keepdims=True) - xh * jnp.mean(dxh * xh, axis=-1, keepdims=True))
        dr_ref[...] = dr
        dgp_ref[...] = (dr * pp_ * gate * (1.0 - gate)).astype(BF16)
        dpp_ref[...] = (dr * gate).astype(BF16)
        dg = jnp.sum(dx * xh, axis=0, keepdims=True)
        db = jnp.sum(dx, axis=0, keepdims=True)
        sq = jnp.sum(err * err, axis=0, keepdims=True)

        @pl.when(i == 0)
        def _():
            dg_ref[...] = dg
            db_ref[...] = db
            sq_ref[...] = sq

        @pl.when(i > 0)
        def _():
            dg_ref[...] += dg
            db_ref[...] += db
            sq_ref[...] += sq

    row = pl.BlockSpec((tr, d), lambda i: (i, 0))
    vec = pl.BlockSpec((1, d), lambda i: (0, 0))
    return pl.pallas_call(
        body, name=name, grid=(t // tr,), in_specs=[row, row, row, vec, vec, row],
        out_specs=[row, row, row, vec, vec, vec],
        out_shape=[SDS((t, d), F32), SDS((t, d), BF16), SDS((t, d), BF16), SDS((1, d), F32), SDS((1, d), F32),
                   SDS((1, d), F32)],
        compiler_params=_cparams("arbitrary"),
    )(x3, gp, pp, g, b, target)


def _residual_out(dra, dxm, name):
    t, d = dra.shape
    tr = _tile(t, 256, 8)

    def body(a_ref, b_ref, o_ref):
        o_ref[...] = ALPHA * a_ref[...] + b_ref[...]

    row = pl.BlockSpec((tr, d), lambda i: (i, 0))
    return pl.pallas_call(body, name=name, grid=(t // tr,), in_specs=[row, row], out_specs=row,
                          out_shape=SDS((t, d), F32), compiler_params=_cparams("parallel"))(dra, dxm)


def _merge_fwd(z, ma, mb, w, name):
    t = z.shape[0]
    tr = _tile(t, 256, 16)

    def body(gc_ref, gh_ref, ma_ref, mb_ref, o_ref):
        o_ref[...] = (_sigmoid(gc_ref[...]) * ma_ref[...] + _sigmoid(gh_ref[...]) * mb_ref[...]).astype(BF16)

    half = pl.BlockSpec((tr, w), lambda i, j: (i, j))
    return pl.pallas_call(
        body, name=name, grid=(t // tr, 2),
        in_specs=[pl.BlockSpec((tr, w), lambda i, j: (i, 7 + j)), pl.BlockSpec((tr, w), lambda i, j: (i, 9 + j)), half, half],
        out_specs=half, out_shape=SDS((t, 2 * w), BF16), compiler_params=_cparams("parallel", "parallel"),
    )(z, z, ma, mb)


def _merge_bwd(dmer, z, ma, mb, w, name):
    t = z.shape[0]
    tr = _tile(t, 256, 16)

    def body(d_ref, gc_ref, gh_ref, ma_ref, mb_ref, dma_ref, dmb_ref, dgc_ref, dgh_ref):
        dm = d_ref[...]
        sc = _sigmoid(gc_ref[...])
        sh = _sigmoid(gh_ref[...])
        dma_ref[...] = (dm * sc).astype(BF16)
        dmb_ref[...] = (dm * sh).astype(BF16)
        dgc_ref[...] = (dm * ma_ref[...] * sc * (1.0 - sc)).astype(BF16)
        dgh_ref[...] = (dm * mb_ref[...] * sh * (1.0 - sh)).astype(BF16)

    half = pl.BlockSpec((tr, w), lambda i, j: (i, j))
    return pl.pallas_call(
        body, name=name, grid=(t // tr, 2),
        in_specs=[half, pl.BlockSpec((tr, w), lambda i, j: (i, 7 + j)), pl.BlockSpec((tr, w), lambda i, j: (i, 9 + j)), half, half],
        out_specs=[half] * 4, out_shape=[SDS((t, 2 * w), BF16)] * 4, compiler_params=_cparams("parallel", "parallel"),
    )(dmer, z, z, ma, mb)


def _shift_down(x, s, row):
    return jnp.where(row >= s, pltpu.roll(x, s, axis=0), 0.0)


def _shift_up(x, s, row, t):
    return jnp.where(row < t - s, pltpu.roll(x, t - s, axis=0), 0.0)


def _conv_fwd(z, cw, w, name):
    t = z.shape[0]
    tc = LANES
    nb = w // tc

    def body(b_ref, c_ref, h_ref, w_ref, o_ref):
        u = c_ref[...] * h_ref[...]
        row = lax.broadcasted_iota(jnp.int32, u.shape, 0)
        cw_ = w_ref[...]
        conv = cw_[2:3, :] * u + cw_[1:2, :] * _shift_down(u, 1, row) + cw_[0:1, :] * _shift_down(u, 2, row)
        o_ref[...] = (b_ref[...] * conv).astype(BF16)

    col = lambda off: pl.BlockSpec((t, tc), lambda j: (0, off * nb + j))
    return pl.pallas_call(
        body, name=name, grid=(nb,), in_specs=[col(0), col(1), col(2), pl.BlockSpec((3, tc), lambda j: (0, j))],
        out_specs=pl.BlockSpec((t, tc), lambda j: (0, j)), out_shape=SDS((t, w), BF16), compiler_params=_cparams("parallel"),
    )(z, z, z, cw)


def _conv_bwd(dy, z, cw, w, name):
    t = z.shape[0]
    tc = LANES
    nb = w // tc

    def body(dy_ref, b_ref, c_ref, h_ref, w_ref, db_ref, dc_ref, dh_ref, dw_ref):
        c_, h_ = c_ref[...], h_ref[...]
        u = c_ * h_
        row = lax.broadcasted_iota(jnp.int32, u.shape, 0)
        cw_ = w_ref[...]
        u1 = _shift_down(u, 1, row)
        u2 = _shift_down(u, 2, row)
        dy_ = dy_ref[...]
        db_ref[...] = (dy_ * (cw_[2:3, :] * u + cw_[1:2, :] * u1 + cw_[0:1, :] * u2)).astype(BF16)
        dconv = dy_ * b_ref[...]
        du = cw_[2:3, :] * dconv + cw_[1:2, :] * _shift_up(dconv, 1, row, t) + cw_[0:1, :] * _shift_up(dconv, 2, row, t)
        dc_ref[...] = (du * h_).astype(BF16)
        dh_ref[...] = (du * c_).astype(BF16)
        dw_ref[0:1, :] = jnp.sum(dconv * u2, axis=0, keepdims=True)
        dw_ref[1:2, :] = jnp.sum(dconv * u1, axis=0, keepdims=True)
        dw_ref[2:3, :] = jnp.sum(dconv * u, axis=0, keepdims=True)

    col = lambda off: pl.BlockSpec((t, tc), lambda j: (0, off * nb + j))
    own = pl.BlockSpec((t, tc), lambda j: (0, j))
    wsp = pl.BlockSpec((3, tc), lambda j: (0, j))
    return pl.pallas_call(
        body, name=name, grid=(nb,), in_specs=[own, col(0), col(1), col(2), wsp], out_specs=[own, own, own, wsp],
        out_shape=[SDS((t, w), BF16)] * 3 + [SDS((3, w), F32)], compiler_params=_cparams("parallel"),
    )(dy, z, z, z, cw)


def _lower_bound(hg):
    mx = jnp.max(hg, axis=0, keepdims=True)
    e = jnp.exp(hg - mx)
    inv = 1.0 / jnp.sum(e, axis=0, keepdims=True)
    return e[0:1, :] * inv, e[1:2, :] * inv


def _chunk_cumsum(x, row):
    s = 1
    while s < CHUNK:
        x = x + jnp.where(row % CHUNK >= s, pltpu.roll(x, s, axis=0), 0.0)
        s *= 2
    return x


def _dot_nt(a, b):
    return lax.dot_general(a.astype(BF16), b.astype(BF16), (((1,), (1,)), ((), ())), preferred_element_type=F32)


def _dot_tn(a, b):
    return lax.dot_general(a.astype(BF16), b.astype(BF16), (((0,), (0,)), ((), ())), preferred_element_type=F32)


def _dot_nn(a, b):
    return jnp.dot(a.astype(BF16), b.astype(BF16), preferred_element_type=F32)


def _tril(x):
    r = lax.broadcasted_iota(jnp.int32, x.shape, 0)
    c = lax.broadcasted_iota(jnp.int32, x.shape, 1)
    return jnp.where(r >= c, x, 0.0)


HGRN_GROUP = 4
HGRN_ROWS = 512


def _hgrn_chunk_inputs(q_ref, f_ref, cum_ref, lb, rows, ln):
    qr = q_ref[rows, ln]
    q = qr * _sigmoid(qr)
    f = lb + (1.0 - lb) * _sigmoid(f_ref[rows, ln])
    return q, 1.0 - f, cum_ref[rows, ln]


def _hgrn_fwd(z, hg, nw, w, name, comm=()):
    t = z.shape[0]
    nh = w // HEAD
    gh = _tile(nh, HGRN_GROUP, 1)
    gw = gh * HEAD
    ngrp = nh // gh
    tb = _tile(t, HGRN_ROWS, CHUNK)
    ncb = tb // CHUNK

    def body(q_ref, f_ref, i_ref, g_ref, hg_ref, nw_ref, y_ref, o_ref, st_ref, cum_ref, *s_refs):
        lb_all, _ = _lower_bound(hg_ref[...])
        row = lax.broadcasted_iota(jnp.int32, (tb, gw), 0)
        cum_ref[...] = _chunk_cumsum(jnp.log(lb_all + (1.0 - lb_all) * _sigmoid(f_ref[...])), row)

        @pl.when(pl.program_id(1) == 0)
        def _():
            for s_ref in s_refs:
                s_ref[...] = jnp.zeros_like(s_ref)

        def step(c, carry):
            rows = pl.ds(pl.multiple_of(c * CHUNK, CHUNK), CHUNK)
            for g in range(gh):
                ln = slice(g * HEAD, (g + 1) * HEAD)
                lb = lb_all[:, ln]
                q, k, cum = _hgrn_chunk_inputs(q_ref, f_ref, cum_ref, lb, rows, ln)
                v = i_ref[rows, ln]
                last = cum[CHUNK - 1:CHUNK, :]
                qe = q * jnp.exp(cum)
                st = s_refs[g][...]
                st_ref[g, c] = st.astype(BF16)
                o_ref[rows, ln] = _dot_nt(qe, st) + _dot_nn(_tril(_dot_nt(qe, k * jnp.exp(-cum))), v)
                s_refs[g][...] = st * jnp.exp(last) + _dot_tn(v, k * jnp.exp(last - cum))
            return carry

        lax.fori_loop(0, ncb, step, 0)
        for g in range(gh):
            ln = slice(g * HEAD, (g + 1) * HEAD)
            o = o_ref[:, ln]
            n = o * lax.rsqrt(jnp.mean(o * o, axis=-1, keepdims=True) + RMS_EPS)
            gr = g_ref[:, ln]
            y_ref[:, ln] = (n * nw_ref[...] * gr * _sigmoid(gr)).astype(BF16)

    col = lambda off: pl.BlockSpec((tb, gw), lambda h, j: (j, off * ngrp + h))
    own = pl.BlockSpec((tb, gw), lambda h, j: (j, h))
    return _hosted_call(
        body, comm, name=name, grid=(ngrp, t // tb),
        in_specs=[col(3), col(4), col(5), col(6), pl.BlockSpec((2, gw), lambda h, j: (0, h)),
                  pl.BlockSpec((1, HEAD), lambda h, j: (0, 0))],
        out_specs=[own, own, pl.BlockSpec((gh, ncb, HEAD, HEAD), lambda h, j: (h, j, 0, 0))],
        out_shape=[SDS((t, w), BF16), SDS((t, w), F32), SDS((nh, t // CHUNK, HEAD, HEAD), BF16)],
        scratch_shapes=[pltpu.VMEM((tb, gw), F32)] + [pltpu.VMEM((HEAD, HEAD), F32)] * gh,
        operands=(z, z, z, z, hg, nw), parallel=1)


def _hgrn_bwd(dy, z, o, states, hg, nw, w, name, comm=()):
    t = z.shape[0]
    nh = w // HEAD
    gh = _tile(nh, HGRN_GROUP, 1)
    gw = gh * HEAD
    ngrp = nh // gh
    tb = _tile(t, HGRN_ROWS, CHUNK)
    ncb = tb // CHUNK
    nt = t // tb

    def body(dy_ref, q_ref, f_ref, i_ref, g_ref, o_ref, st_ref, hg_ref, nw_ref,
             dq_ref, df_ref, di_ref, dg_ref, dhg_ref, dnw_ref, cum_ref, do_ref, *ds_refs):
        lb_all, s1_all = _lower_bound(hg_ref[...])
        row = lax.broadcasted_iota(jnp.int32, (tb, gw), 0)
        crow = lax.broadcasted_iota(jnp.int32, (CHUNK, HEAD), 0)
        cum_ref[...] = _chunk_cumsum(jnp.log(lb_all + (1.0 - lb_all) * _sigmoid(f_ref[...])), row)

        @pl.when(pl.program_id(1) == 0)
        def _():
            for ds_ref in ds_refs:
                ds_ref[...] = jnp.zeros_like(ds_ref)
            dhg_ref[...] = jnp.zeros_like(dhg_ref)
            dnw_ref[...] = jnp.zeros_like(dnw_ref)

        for g in range(gh):
            ln = slice(g * HEAD, (g + 1) * HEAD)
            o_ = o_ref[:, ln]
            rstd = lax.rsqrt(jnp.mean(o_ * o_, axis=-1, keepdims=True) + RMS_EPS)
            n = o_ * rstd
            gr = g_ref[:, ln]
            sg = _sigmoid(gr)
            dy_ = dy_ref[:, ln]
            dg_ref[:, ln] = (dy_ * n * nw_ref[...] * (sg * (1.0 + gr * (1.0 - sg)))).astype(BF16)
            dsil = dy_ * gr * sg
            dnw_ref[:, ln] += jnp.sum(dsil * n, axis=0, keepdims=True)
            dn = dsil * nw_ref[...]
            do_ref[:, ln] = rstd * (dn - n * jnp.mean(dn * n, axis=-1, keepdims=True))

        def step(cc, dlbs):
            c = ncb - 1 - cc
            rows = pl.ds(pl.multiple_of(c * CHUNK, CHUNK), CHUNK)
            new = []
            for g in range(gh):
                ln = slice(g * HEAD, (g + 1) * HEAD)
                lb = lb_all[:, ln]
                qr = q_ref[rows, ln]
                sq = _sigmoid(qr)
                q = qr * sq
                sf = _sigmoid(f_ref[rows, ln])
                f = lb + (1.0 - lb) * sf
                k = 1.0 - f
                cum = cum_ref[rows, ln]
                v = i_ref[rows, ln]
                do = do_ref[rows, ln]
                last = cum[CHUNK - 1:CHUNK, :]
                eg = jnp.exp(cum)
                eng = jnp.exp(-cum)
                elc = jnp.exp(last - cum)
                qe, ke, kl = q * eg, k * eng, k * elc
                ds = ds_refs[g][...]
                a = _tril(_dot_nt(qe, ke))
                da = _tril(_dot_nt(do, v))
                di_ref[rows, ln] = (_dot_tn(a, do) + _dot_nt(kl, ds)).astype(BF16)
                st = st_ref[g, c]
                dkl = _dot_nn(v, ds)
                dq = (_dot_nn(do, st) + _dot_nn(da, ke)) * eg
                dk = _dot_tn(da, qe) * eng + dkl * elc
                el = jnp.exp(last)
                ds_refs[g][...] = ds * el + _dot_tn(do, qe)
                dlast = jnp.sum(kl * dkl, axis=0, keepdims=True) + el * jnp.sum(ds * st.astype(F32), axis=0, keepdims=True)
                x = q * dq - k * dk + jnp.where(crow == CHUNK - 1, dlast, 0.0)
                s = 1
                while s < CHUNK:
                    x = x + _shift_up(x, s, crow, CHUNK)
                    s *= 2
                df = x / f - dk
                dq_ref[rows, ln] = (dq * (sq * (1.0 + qr * (1.0 - sq)))).astype(BF16)
                df_ref[rows, ln] = (df * (1.0 - lb) * sf * (1.0 - sf)).astype(BF16)
                new.append(dlbs[g] + jnp.sum(df * (1.0 - sf), axis=0, keepdims=True))
            return tuple(new)

        dlbs = lax.fori_loop(0, ncb, step, tuple(jnp.zeros((1, HEAD), F32) for _ in range(gh)))
        for g in range(gh):
            ln = slice(g * HEAD, (g + 1) * HEAD)
            dlb = dlbs[g] * lb_all[:, ln] * s1_all[:, ln]
            dhg_ref[0:1, ln] += dlb
            dhg_ref[1:2, ln] -= dlb

    col = lambda off: pl.BlockSpec((tb, gw), lambda h, j: (nt - 1 - j, off * ngrp + h))
    own = pl.BlockSpec((tb, gw), lambda h, j: (nt - 1 - j, h))
    hsp = pl.BlockSpec((2, gw), lambda h, j: (0, h))
    return _hosted_call(
        body, comm, name=name, grid=(ngrp, nt),
        in_specs=[own, col(3), col(4), col(5), col(6), own,
                  pl.BlockSpec((gh, ncb, HEAD, HEAD), lambda h, j: (h, nt - 1 - j, 0, 0)),
                  hsp, pl.BlockSpec((1, HEAD), lambda h, j: (0, 0))],
        out_specs=[own, own, own, own, hsp, pl.BlockSpec((1, gw), lambda h, j: (0, h))],
        out_shape=[SDS((t, w), BF16)] * 4 + [SDS((2, w), F32), SDS((1, w), F32)],
        scratch_shapes=[pltpu.VMEM((tb, gw), F32)] * 2 + [pltpu.VMEM((HEAD, HEAD), F32)] * gh,
        operands=(dy, z, z, z, z, o, states, hg, nw), parallel=1)


def _cast_pad(wt, n_pad, meta, sp, name):
    r, n = wt.shape
    g, p, per = meta
    tr = _tile(r, max(16, (3 << 19) // n_pad // 16 * 16), 16)

    def body(sp_ref, w_ref, o_ref):
        if n_pad != n:
            o_ref[...] = jnp.zeros(o_ref.shape, o_ref.dtype)
        o_ref[:, 0:n] = w_ref[...].astype(BF16)

    grid_spec = pltpu.PrefetchScalarGridSpec(
        num_scalar_prefetch=1, grid=(r // tr,), in_specs=[pl.BlockSpec((tr, n), lambda i, sp: (i, 0))],
        out_specs=pl.BlockSpec((None, tr, n_pad), lambda i, sp: (sp[1] // per, ((sp[1] % per) * r) // tr + i, 0)))
    return pl.pallas_call(body, name=name, grid_spec=grid_spec, out_shape=SDS((g, p, n_pad), BF16),
                          compiler_params=_cparams("parallel"))(sp, wt)


def _adamw(wt, g, m, v, name):
    r, n = wt.shape
    ng = g.shape[1]
    tr = _tile(r, max(8, (3 << 17) // ng // 8 * 8), 8)
    c1 = 1.0 / (1.0 - ADAM_B1 ** ADAM_STEP)
    c2 = 1.0 / (1.0 - ADAM_B2 ** ADAM_STEP)

    def body(w_ref, g_ref, m_ref, v_ref, go_ref, d_ref, mo_ref, vo_ref):
        g_ = g_ref[:, 0:n]
        m2 = ADAM_B1 * m_ref[...] + (1.0 - ADAM_B1) * g_
        v2 = ADAM_B2 * v_ref[...] + (1.0 - ADAM_B2) * (g_ * g_)
        go_ref[...] = g_
        mo_ref[...] = m2
        vo_ref[...] = v2
        d_ref[...] = -ADAM_LR * ((m2 * c1) / (jnp.sqrt(v2 * c2) + ADAM_EPS) + ADAM_WD * w_ref[...])

    blk = pl.BlockSpec((tr, n), lambda i: (i, 0))
    return pl.pallas_call(
        body, name=name, grid=(r // tr,), in_specs=[blk, pl.BlockSpec((tr, ng), lambda i: (i, 0)), blk, blk],
        out_specs=[blk] * 4, out_shape=[SDS((r, n), F32)] * 4, compiler_params=_cparams("parallel"),
    )(wt, g, m, v)


def _place():
    x, y, c = lax.axis_index("x"), lax.axis_index("y"), lax.axis_index("c")
    return x, y, c, 2 * x + y


def _chip_dev(k, c):
    return (k // 2, k % 2, c)


def _half(ref, j, h, rows, per):
    return ref.at[j // per, pl.ds((j % per) * rows + h * (rows // 2), rows // 2)]


def _gather_stage(bufs, metas, rows_of, parts, zero_pad):
    nw = len(bufs)
    pad_jobs = [(i, gi) for i in range(nw) if parts[i][0] == 0 and metas[i][1] > metas[i][2] * rows_of[i]
                for gi in range(metas[i][0])]

    def part_of(ref, i, j, h):
        per = metas[i][2]
        p, np_ = parts[i]
        pr = rows_of[i] // 2 // np_
        return ref.at[j // per, pl.ds((j % per) * rows_of[i] + h * (rows_of[i] // 2) + p * pr, pr)]

    def descriptors(ins, outs, sems):
        src, zp, dst = ins[:nw], ins[nw], outs
        pads, send, recv, fsend, frecv = sems
        x, y, c, me = _place()

        def pad(n):
            i, gi = pad_jobs[n]
            extra = metas[i][1] - metas[i][2] * rows_of[i]
            return pltpu.make_async_copy(zp.at[pl.ds(0, extra)], dst[i].at[gi, pl.ds(metas[i][2] * rows_of[i], extra)], pads.at[n])

        def ici(i, r, frm):
            return pltpu.make_async_remote_copy(
                src_ref=part_of(src[i], i, me, c), dst_ref=part_of(dst[i], i, frm, c), send_sem=send.at[i, r - 1],
                recv_sem=recv.at[i, r - 1], device_id=_chip_dev((me + r) % N_CHIPS, c), device_id_type=MESH)

        def d2d(i, r, frm, h):
            blk = part_of(dst[i], i, frm, h)
            return pltpu.make_async_remote_copy(src_ref=blk, dst_ref=blk, send_sem=fsend.at[i, r - 1],
                                                recv_sem=frecv.at[i, r - 1], device_id=(x, y, 1 - c), device_id_type=MESH)

        return pad, ici, d2d, c, me

    def start(ins, outs, sems):
        pad, ici, d2d, c, me = descriptors(ins, outs, sems)
        for n in range(len(pad_jobs)):
            pad(n).start()
        for i in range(nw):
            for r in range(1, N_CHIPS):
                ici(i, r, me).start()

    def finish(ins, outs, sems):
        pad, ici, d2d, c, me = descriptors(ins, outs, sems)
        for i in range(nw):
            for r in range(1, N_CHIPS):
                frm = (me - r) % N_CHIPS
                ici(i, r, frm).wait_recv()
                d2d(i, r, frm, c).start()
        for i in range(nw):
            for r in range(1, N_CHIPS):
                d2d(i, r, (me - r) % N_CHIPS, 1 - c).wait_recv()
        for i in range(nw):
            for r in range(1, N_CHIPS):
                ici(i, r, me).wait_send()
                d2d(i, r, (me - r) % N_CHIPS, c).wait_send()
        for n in range(len(pad_jobs)):
            pad(n).wait()

    return _Stage(ins=list(bufs) + [zero_pad], out_shapes=[SDS(b.shape, b.dtype) for b in bufs],
                  aliases={i: i for i in range(nw)},
                  sems=[pltpu.SemaphoreType.DMA((max(len(pad_jobs), 1),))] + [pltpu.SemaphoreType.DMA((nw, N_CHIPS - 1))] * 4,
                  start=start, finish=finish)


def _gather_small(packed, name):
    r, n = packed.shape

    def body(src, dst, send, recv):
        x, y, c, me = _place()
        dst[me] = src[...]
        cps = []
        for d in range(1, N_CHIPS):
            cp = pltpu.make_async_remote_copy(src_ref=src, dst_ref=dst.at[me], send_sem=send.at[d - 1], recv_sem=recv.at[d - 1],
                                              device_id=_chip_dev((me + d) % N_CHIPS, c), device_id_type=MESH)
            cp.start()
            cps.append(cp)
        for d in range(1, N_CHIPS):
            pltpu.make_async_remote_copy(src_ref=src, dst_ref=dst.at[(me - d) % N_CHIPS], send_sem=send.at[d - 1],
                                         recv_sem=recv.at[d - 1], device_id=_chip_dev((me + d) % N_CHIPS, c),
                                         device_id_type=MESH).wait_recv()
        for cp in cps:
            cp.wait_send()

    return pl.pallas_call(
        body, name=name, in_specs=[VMEM_SPEC], out_specs=VMEM_SPEC, out_shape=SDS((N_CHIPS, r, n), F32),
        scratch_shapes=[pltpu.SemaphoreType.DMA((N_CHIPS - 1,))] * 2,
    )(packed)


def _all_reduce_small(packed, name):
    r, n = packed.shape

    def body(src, out, slots, send, recv):
        x, y, c, me = _place()
        idx = 2 * me + c
        slots[idx] = src[...]
        cps = []

        def peer(d):
            p = (idx + d) % N_DEV
            return (p // 4, (p // 2) % 2, p % 2)

        for d in range(1, N_DEV):
            cp = pltpu.make_async_remote_copy(src_ref=src, dst_ref=slots.at[idx], send_sem=send.at[d - 1], recv_sem=recv.at[d - 1],
                                              device_id=peer(d), device_id_type=MESH)
            cp.start()
            cps.append(cp)
        for d in range(1, N_DEV):
            pltpu.make_async_remote_copy(src_ref=src, dst_ref=slots.at[(idx - d) % N_DEV], send_sem=send.at[d - 1],
                                         recv_sem=recv.at[d - 1], device_id=peer(d), device_id_type=MESH).wait_recv()
        for cp in cps:
            cp.wait_send()
        acc = slots[0]
        for k in range(1, N_DEV):
            acc = acc + slots[k]
        out[...] = acc

    return pl.pallas_call(
        body, name=name, in_specs=[VMEM_SPEC], out_specs=VMEM_SPEC, out_shape=SDS((r, n), F32),
        scratch_shapes=[pltpu.VMEM((N_DEV, r, n), F32)] + [pltpu.SemaphoreType.DMA((N_DEV - 1,))] * 2,
    )(packed)


def _simple_stage(ins, out_shapes, aliases, n_copies, copies):
    def start(ins_, outs, sems):
        for cp in copies(ins_, outs, *sems):
            cp.start()

    def finish(ins_, outs, sems):
        for cp in copies(ins_, outs, *sems):
            cp.wait()

    return _Stage(ins=list(ins), out_shapes=list(out_shapes), aliases=aliases,
                  sems=[pltpu.SemaphoreType.DMA((n_copies,))] * 2, start=start, finish=finish)


def _rs_pair_exchange(grads, metas, rows_of):
    nw = len(grads)

    def copies(src, dst, send, recv):
        x, y, c, me = _place()
        return [pltpu.make_async_remote_copy(
            src_ref=_half(src[i], j, 1 - c, rows_of[i], metas[i][2]), dst_ref=dst[i].at[j], send_sem=send.at[i * N_CHIPS + j],
            recv_sem=recv.at[i * N_CHIPS + j], device_id=(x, y, 1 - c), device_id_type=MESH)
            for i in range(nw) for j in range(N_CHIPS)]

    out_shapes = [SDS((N_CHIPS, rows_of[i] // 2, g.shape[2]), F32) for i, g in enumerate(grads)]
    return _simple_stage(grads, out_shapes, {}, nw * N_CHIPS, copies)


def _rs_pair_add(g, got, meta, rows, sp, name):
    per = meta[2]
    n = g.shape[2]
    hr = rows // 2
    tr = _tile(hr, max(16, (3 << 19) // n // 16 * 16), 16)

    def body(sp_ref, g_ref, got_ref, snd_ref, own_ref):
        j = pl.program_id(1)
        s = g_ref[...] + got_ref[...]
        snd_ref[...] = s.astype(BF16)

        @pl.when(j == sp_ref[1])
        def _():
            own_ref[...] = s

    grid_spec = pltpu.PrefetchScalarGridSpec(
        num_scalar_prefetch=1, grid=(hr // tr, N_CHIPS),
        in_specs=[pl.BlockSpec((None, tr, n), lambda i, j, sp: (j // per, ((j % per) * rows + sp[0] * hr) // tr + i, 0)),
                  pl.BlockSpec((None, tr, n), lambda i, j, sp: (j, i, 0))],
        out_specs=[pl.BlockSpec((None, tr, n), lambda i, j, sp: (j, i, 0)), pl.BlockSpec((tr, n), lambda i, j, sp: (i, 0))])
    return pl.pallas_call(
        body, name=name, grid_spec=grid_spec, out_shape=[SDS((N_CHIPS, hr, n), BF16), SDS((hr, n), F32)],
        compiler_params=_cparams("parallel", "arbitrary"),
    )(sp, g, got)


def _rs_chip_exchange(sends, part=(0, 1), prev=None):
    nw = len(sends)
    p, np_ = part

    def copies(src, dst, send, recv):
        x, y, c, me = _place()
        cps = []
        for i in range(nw):
            pr = sends[i].shape[1] // np_
            for r in range(1, N_CHIPS):
                cps.append(pltpu.make_async_remote_copy(
                    src_ref=src[i].at[(me + r) % N_CHIPS, pl.ds(p * pr, pr)], dst_ref=dst[i].at[r - 1, pl.ds(p * pr, pr)],
                    send_sem=send.at[i * (N_CHIPS - 1) + r - 1], recv_sem=recv.at[i * (N_CHIPS - 1) + r - 1],
                    device_id=_chip_dev((me + r) % N_CHIPS, c), device_id_type=MESH))
        return cps

    out_shapes = [SDS((N_CHIPS - 1,) + s.shape[1:], BF16) for s in sends]
    if prev is None:
        return _simple_stage(sends, out_shapes, {}, nw * (N_CHIPS - 1), copies)
    return _simple_stage(list(sends) + list(prev), out_shapes, {nw + i: i for i in range(nw)}, nw * (N_CHIPS - 1), copies)


def _rs_chip_add(own, got, sp, name):
    hr, n = own.shape
    tr = _tile(hr, max(16, (3 << 19) // n // 16 * 16), 16)

    def body(sp_ref, own_ref, got_ref, o_ref):
        acc = own_ref[...]
        for r in range(N_CHIPS - 1):
            acc = acc + got_ref[r].astype(F32)
        o_ref[...] = acc

    grid_spec = pltpu.PrefetchScalarGridSpec(
        num_scalar_prefetch=1, grid=(hr // tr,),
        in_specs=[pl.BlockSpec((tr, n), lambda i, sp: (i, 0)), pl.BlockSpec((N_CHIPS - 1, tr, n), lambda i, sp: (0, i, 0))],
        out_specs=pl.BlockSpec((tr, n), lambda i, sp: (sp[0] * (hr // tr) + i, 0)))
    return pl.pallas_call(body, name=name, grid_spec=grid_spec, out_shape=SDS((2 * hr, n), F32),
                          compiler_params=_cparams("parallel"))(sp, own, got)


def _rs_pair_share(blocks):
    nw = len(blocks)

    def copies(src, dst, send, recv):
        x, y, c, me = _place()
        cps = []
        for i in range(nw):
            hr = src[i].shape[0] // 2
            cps.append(pltpu.make_async_remote_copy(
                src_ref=src[i].at[pl.ds(c * hr, hr)], dst_ref=dst[i].at[pl.ds(c * hr, hr)], send_sem=send.at[i],
                recv_sem=recv.at[i], device_id=(x, y, 1 - c), device_id_type=MESH))
        return cps

    return _simple_stage(blocks, [SDS(b.shape, b.dtype) for b in blocks], {i: i for i in range(nw)}, nw, copies)


def kernel(x, p, ln_g, ln_b, ffn1_w_in, ffn1_w_out, mix_w_in, conv_w, hg_lower_bound, hg_norm_w, branch_w_conv, branch_w_hgrn, mix_w_out, ffn2_w_in, ffn2_w_out, ple_w_gate, ple_w_proj, loss_target, m_ln_g, m_ln_b, m_ffn1_w_in, m_ffn1_w_out, m_mix_w_in, m_conv_w, m_hg_lower_bound, m_hg_norm_w, m_branch_w_conv, m_branch_w_hgrn, m_mix_w_out, m_ffn2_w_in, m_ffn2_w_out, m_ple_w_gate, m_ple_w_proj, v_ln_g, v_ln_b, v_ffn1_w_in, v_ffn1_w_out, v_mix_w_in, v_conv_w, v_hg_lower_bound, v_hg_norm_w, v_branch_w_conv, v_branch_w_hgrn, v_mix_w_out, v_ffn2_w_in, v_ffn2_w_out, v_ple_w_gate, v_ple_w_proj):
    assert ln_g.shape[0] == DEPTH and x.shape[0] == 1 and p.shape[:2] == (1, 1)
    t, d = x.shape[1], x.shape[2]
    w = d // 2
    x0 = x.reshape(t, d)
    pe = p.reshape(t, p.shape[-1])
    target = loss_target.reshape(t, d)
    cx, cy, cc = lax.axis_index("x"), lax.axis_index("y"), lax.axis_index("c")
    chip = 2 * cx + cy
    sp = jnp.stack([cc, chip]).astype(jnp.int32)

    big = dict(ffn1_w_in=ffn1_w_in[0], ffn1_w_out=ffn1_w_out[0], mix_w_in=mix_w_in[0], branch_w_conv=branch_w_conv[0],
               branch_w_hgrn=branch_w_hgrn[0], mix_w_out=mix_w_out[0], ffn2_w_in=ffn2_w_in[0], ffn2_w_out=ffn2_w_out[0],
               ple_w_gate=ple_w_gate[0], ple_w_proj=ple_w_proj[0])
    moments = dict(ffn1_w_in=(m_ffn1_w_in, v_ffn1_w_in), ffn1_w_out=(m_ffn1_w_out, v_ffn1_w_out), mix_w_in=(m_mix_w_in, v_mix_w_in),
                   branch_w_conv=(m_branch_w_conv, v_branch_w_conv), branch_w_hgrn=(m_branch_w_hgrn, v_branch_w_hgrn),
                   mix_w_out=(m_mix_w_out, v_mix_w_out), ffn2_w_in=(m_ffn2_w_in, v_ffn2_w_in), ffn2_w_out=(m_ffn2_w_out, v_ffn2_w_out),
                   ple_w_gate=(m_ple_w_gate, v_ple_w_gate), ple_w_proj=(m_ple_w_proj, v_ple_w_proj))
    names = list(big)

    n_loc = ffn1_w_in.shape[-1]
    n_pad = -(-n_loc // LANES) * LANES
    assert mix_w_in.shape[-1] % LANES == 0 and ffn1_w_out.shape[1] * 2 == n_loc
    pad_cols = dict(ffn1_w_in=n_pad, ffn2_w_in=n_pad)
    meta = {k: (N_CHIPS, big[k].shape[0], 1) for k in names}
    meta["ffn1_w_out"] = meta["ffn2_w_out"] = (2, n_pad, 2)
    rows = {k: big[k].shape[0] for k in names}
    wbuf = {k: _cast_pad(big[k], pad_cols.get(k, big[k].shape[1]), meta[k], sp, "cast_" + k) for k in names}
    zero_pad = jnp.zeros((max(n_pad - n_loc, 16), d), BF16)

    def gather(*items):
        ks = [k for k, _, _ in items]
        return _gather_stage([wbuf[k] for k in ks], [meta[k] for k in ks], [rows[k] for k in ks], [(p_, n_) for _, p_, n_ in items],
                             zero_pad), ks

    def gathered(ks, outs):
        wbuf.update(zip(ks, outs))

    def w3(k):
        return wbuf[k]

    def w2(k):
        return wbuf[k].reshape(-1, wbuf[k].shape[2])

    dq, wq = d // N_CHIPS, w // N_CHIPS
    small = jnp.concatenate([ln_g[0], ln_b[0], jnp.pad(conv_w[0], ((0, 5), (0, dq - wq)))], axis=0)
    small = _gather_small(small, "gather_small")
    lng = small[:, 0:4, :].transpose(1, 0, 2).reshape(4, 1, d)
    lnb = small[:, 4:8, :].transpose(1, 0, 2).reshape(4, 1, d)
    cw = small[:, 8:11, :wq].transpose(1, 0, 2).reshape(3, w)
    hg = hg_lower_bound
    nw_ = hg_norm_w

    st, ks = gather(("ffn1_w_in", 0, 1))
    gathered(ks, _run_stages([st], "gather_first")[0])
    st, ks = gather(("ffn1_w_out", 0, 1), ("mix_w_in", 0, 2))
    z1, got = _mm(x0, w3("ffn1_w_in"), name="ffn1_in", b_blocked=True, out_dtype=BF16, comm=[st])
    gathered(ks, got)
    h1 = _swiglu_fwd(z1, "ffn1_act")
    st, ks = gather(("mix_w_in", 1, 2))
    y1, got = _mm(h1, w2("ffn1_w_out"), name="ffn1_out", tm=1024, tn=1024, tk=2816, comm=[st])
    gathered(ks, got)
    r1, x1, x1b = _ln_fwd(x0, y1, lng[0], lnb[0], 0.5, "ln0")
    st, ks = gather(("branch_w_conv", 0, 1), ("branch_w_hgrn", 0, 1), ("mix_w_out", 0, 1), ("ffn2_w_in", 0, 2))
    z, got = _mm(x1b, w3("mix_w_in"), name="mix_in", b_blocked=True, comm=[st])
    gathered(ks, got)
    ya = _conv_fwd(z, cw, w, "conv_fwd")
    st, ks = gather(("ffn2_w_in", 1, 2))
    (yb, o_h, states), got = _hgrn_fwd(z, hg, nw_, w, "hgrn_fwd", comm=[st])
    gathered(ks, got)
    ma = _mm(ya, w3("branch_w_conv"), name="branch_conv", b_blocked=True, tn=512)
    mb = _mm(yb, w3("branch_w_hgrn"), name="branch_hgrn", b_blocked=True, tn=512)
    merged = _merge_fwd(z, ma, mb, w, "merge_fwd")
    y2 = _mm(merged, w2("mix_w_out"), name="mix_out", tn=1024)
    r2, x2, x2b = _ln_fwd(x1, y2, lng[1], lnb[1], 1.0, "ln1")
    st, ks = gather(("ffn2_w_out", 0, 1), ("ple_w_gate", 0, 1), ("ple_w_proj", 0, 1))
    z3, got = _mm(x2b, w3("ffn2_w_in"), name="ffn2_in", b_blocked=True, out_dtype=BF16, comm=[st])
    gathered(ks, got)
    h3 = _swiglu_fwd(z3, "ffn2_act")
    y3 = _mm(h3, w2("ffn2_w_out"), name="ffn2_out", tm=1024, tn=1024, tk=2816)
    r3, x3, x3b = _ln_fwd(x2, y3, lng[2], lnb[2], 0.5, "ln2")
    gp = _mm(x3b, w2("ple_w_gate"), name="ple_gate", tn=1024)
    pp = _mm(pe, w3("ple_w_proj"), name="ple_proj", b_blocked=True, tn=512)
    dr4, dgp, dpp, dg3, db3, sq = _tail(x3, gp, pp, lng[3], lnb[3], target, "tail")

    grads, sends, owns, blocks, outs = {}, {}, {}, {}, {}

    def pair_exchange(*ks):
        return _rs_pair_exchange([grads[k] for k in ks], [meta[k] for k in ks], [rows[k] for k in ks])

    def pair_add(ks, got):
        for k, g_ in zip(ks, got):
            sends[k], owns[k] = _rs_pair_add(grads[k], g_, meta[k], rows[k], sp, "rs_pair_add_" + k)

    def chip_exchange(*ks):
        return _rs_chip_exchange([sends[k] for k in ks])

    def chip_add(ks, got):
        for k, g_ in zip(ks, got):
            blocks[k] = _rs_chip_add(owns[k], g_, sp, "rs_chip_add_" + k)

    def pair_share(*ks):
        return _rs_pair_share([blocks[k] for k in ks])

    def update(ks, full):
        for k, g_ in zip(ks, full):
            m_, v_ = moments[k]
            s2 = (-1, big[k].shape[1] // 2) if g_.shape == big[k].shape else big[k].shape
            res = _adamw(big[k].reshape(s2), g_.reshape(s2) if g_.shape == big[k].shape else g_, m_[0].reshape(s2),
                         v_[0].reshape(s2), "adamw_" + k)
            outs[k] = [a.reshape(m_.shape) for a in res]

    ple = ("ple_w_gate", "ple_w_proj")
    mixo = ("mix_w_out", "branch_w_conv", "branch_w_hgrn")
    dx3m = _mm(dgp, w2("ple_w_gate"), name="d_ple_gate_x", tb=True, tn=1024, tk=2048)
    grads["ple_w_gate"] = _mm(x3b, dgp, name="d_ple_gate_w", ta=True, tm=1024, tk=2048, tn=1024).reshape(N_CHIPS, -1, d)
    grads["ple_w_proj"] = _mm(pe, dpp, name="d_ple_proj_w", ta=True, out_blocked=N_CHIPS, tk=2048, tn=512)
    dr3, dy3b, dg2, db2 = _ln_bwd(dr4, dx3m, r3, lng[2], 0.5, "ln2_bwd")
    dh3, got = _mm(dy3b, w2("ffn2_w_out"), name="d_ffn2_out_x", tb=True, out_dtype=BF16, tn=1408, tk=2048,
                   comm=[pair_exchange(*ple)])
    pair_add(ple, got)
    g_, got = _mm(h3, dy3b, name="d_ffn2_out_w", ta=True, tm=1408, tk=2048, tn=1024, comm=[chip_exchange(*ple)])
    grads["ffn2_w_out"] = g_.reshape(2, n_pad, d)
    chip_add(ple, got)
    dz3 = _swiglu_bwd(dh3, z3, "ffn2_act_bwd")
    dx2m, got, full = _mm(dz3, w3("ffn2_w_in"), name="d_ffn2_in_x", tb=True, b_blocked=True, tm=1024, tn=1024, tk=2816,
                          comm=[pair_exchange("ffn2_w_out"), pair_share(*ple)])
    pair_add(["ffn2_w_out"], got)
    update(ple, full)
    grads["ffn2_w_in"], got = _mm(x2b, dz3, name="d_ffn2_in_w", ta=True, out_blocked=N_CHIPS, tk=2048, comm=[chip_exchange("ffn2_w_out")])
    chip_add(["ffn2_w_out"], got)
    dr2, dy2b, dg1, db1 = _ln_bwd(dr3, dx2m, r2, lng[1], 1.0, "ln1_bwd")
    dmer, got = _mm(dy2b, w2("mix_w_out"), name="d_mix_out_x", tb=True, tn=1024, tk=2048, comm=[pair_exchange("ffn2_w_in")])
    pair_add(["ffn2_w_in"], got)
    g_, full = _mm(merged, dy2b, name="d_mix_out_w", ta=True, tm=1024, tk=2048, tn=1024, comm=[pair_share("ffn2_w_out")])
    grads["mix_w_out"] = g_.reshape(N_CHIPS, -1, d)
    update(["ffn2_w_out"], full)
    dma, dmb, dgc, dgh = _merge_bwd(dmer, z, ma, mb, w, "merge_bwd")
    dya = _mm(dma, w3("branch_w_conv"), name="d_branch_conv_x", tb=True, b_blocked=True, tn=1024, tk=512)
    dyb = _mm(dmb, w3("branch_w_hgrn"), name="d_branch_hgrn_x", tb=True, b_blocked=True, tn=1024, tk=512)
    grads["branch_w_conv"] = _mm(ya, dma, name="d_branch_conv_w", ta=True, out_blocked=N_CHIPS, tm=1024, tk=2048, tn=512)
    grads["branch_w_hgrn"] = _mm(yb, dmb, name="d_branch_hgrn_w", ta=True, out_blocked=N_CHIPS, tm=1024, tk=2048, tn=512)
    dbg, dcg, dhc, dcw = _conv_bwd(dya, z, cw, w, "conv_bwd")
    (dq_, df_, di_, dgr_, dhg, dnw), got2, got = _hgrn_bwd(dyb, z, o_h, states, hg, nw_, w, "hgrn_bwd",
                                                            comm=[chip_exchange("ffn2_w_in"), pair_exchange(*mixo)])
    chip_add(["ffn2_w_in"], got2)
    pair_add(mixo, got)
    dz = jnp.concatenate([dbg, dcg, dhc, dq_, df_, di_, dgr_, dgc, dgh], axis=1)
    dx1m, full, got = _mm(dz, w3("mix_w_in"), name="d_mix_in_x", tb=True, b_blocked=True, tm=1024, tn=1024, tk=2816,
                          comm=[pair_share("ffn2_w_in"), chip_exchange(*mixo)])
    update(["ffn2_w_in"], full)
    chip_add(mixo, got)
    grads["mix_w_in"], full = _mm(x1b, dz, name="d_mix_in_w", ta=True, out_blocked=N_CHIPS, tk=2048, comm=[pair_share(*mixo)])
    update(mixo, full)
    dr1, dy1b, dg0, db0 = _ln_bwd(dr2, dx1m, r1, lng[0], 0.5, "ln0_bwd")
    dh1, got = _mm(dy1b, w2("ffn1_w_out"), name="d_ffn1_out_x", tb=True, out_dtype=BF16, tn=1408, tk=2048,
                   comm=[pair_exchange("mix_w_in")])
    pair_add(["mix_w_in"], got)
    mix_sends = [sends["mix_w_in"]]
    g_, got_a = _mm(h1, dy1b, name="d_ffn1_out_w", ta=True, tm=1408, tk=2048, tn=1024, comm=[_rs_chip_exchange(mix_sends, (0, 2))])
    grads["ffn1_w_out"] = g_.reshape(2, n_pad, d)
    dz1 = _swiglu_bwd(dh1, z1, "ffn1_act_bwd")
    g_other, got2, got = _mm(x0, dz1, name="d_ffn1_in_w_other", ta=True, out_blocked=N_CHIPS, tk=2048, half=(sp, True),
                             comm=[_rs_chip_exchange(mix_sends, (1, 2), got_a), pair_exchange("ffn1_w_out")])
    chip_add(["mix_w_in"], got2)
    pair_add(["ffn1_w_out"], got)
    grads["ffn1_w_in"], full, got2, got = _mm(
        x0, dz1, name="d_ffn1_in_w_own", ta=True, out_blocked=N_CHIPS, tk=2048, half=(sp, False),
        comm=[pair_share("mix_w_in"), chip_exchange("ffn1_w_out"),
              _rs_pair_exchange([g_other], [meta["ffn1_w_in"]], [rows["ffn1_w_in"]])])
    update(["mix_w_in"], full)
    chip_add(["ffn1_w_out"], got2)
    pair_add(["ffn1_w_in"], got)
    dx0m, got2, full = _mm(dz1, w3("ffn1_w_in"), name="d_ffn1_in_x", tb=True, b_blocked=True, tm=1024, tn=1024, tk=2816,
                           comm=[chip_exchange("ffn1_w_in"), pair_share("ffn1_w_out")])
    chip_add(["ffn1_w_in"], got2)
    update(["ffn1_w_out"], full)
    grad_x = _residual_out(dr1, dx0m, "grad_x").reshape(x.shape)
    update(["ffn1_w_in"], _run_stages([pair_share("ffn1_w_in")], "rs_tail_pair")[0])

    pack = jnp.concatenate([
        dg0, dg1, dg2, dg3, db0, db1, db2, db3,
        jnp.pad(dcw, ((0, 0), (0, d - w))), jnp.pad(dhg, ((0, 0), (0, d - w))),
        jnp.pad(jnp.sum(dnw.reshape(-1, HEAD), axis=0, keepdims=True), ((0, 0), (0, d - HEAD))), sq], axis=0)
    pack = _all_reduce_small(jnp.pad(pack, ((0, 1), (0, 0))), "reduce_small")
    loss = (0.5 / d) * jnp.sum(pack[14])
    g_ln_g = lax.dynamic_slice_in_dim(pack[0:4], chip * dq, dq, axis=1)
    g_ln_b = lax.dynamic_slice_in_dim(pack[4:8], chip * dq, dq, axis=1)
    g_conv = lax.dynamic_slice_in_dim(pack[8:11, :w], chip * wq, wq, axis=1)
    g_hg = pack[11:13, :w]
    g_nw = pack[13:14, :HEAD]

    small_w = dict(ln_g=(ln_g, g_ln_g, m_ln_g, v_ln_g), ln_b=(ln_b, g_ln_b, m_ln_b, v_ln_b),
                   conv_w=(conv_w, g_conv, m_conv_w, v_conv_w), hg_lower_bound=(hg_lower_bound, g_hg, m_hg_lower_bound, v_hg_lower_bound),
                   hg_norm_w=(hg_norm_w, g_nw, m_hg_norm_w, v_hg_norm_w))
    for k, (w_, g_, m_, v_) in small_w.items():
        s2 = (-1, w_.shape[-1])
        outs[k] = [a.reshape(w_.shape) for a in _adamw(w_.reshape(s2), g_.reshape(s2), m_.reshape(s2), v_.reshape(s2), "adamw_" + k)]

    order = ["ln_g", "ln_b", "ffn1_w_in", "ffn1_w_out", "mix_w_in", "conv_w", "hg_lower_bound", "hg_norm_w", "branch_w_conv",
             "branch_w_hgrn", "mix_w_out", "ffn2_w_in", "ffn2_w_out", "ple_w_gate", "ple_w_proj"]
    return (loss, grad_x, *[outs[k][0] for k in order], *[outs[k][1] for k in order], *[outs[k][2] for k in order],
            *[outs[k][3] for k in order])
```

```python
import collections
import functools

import jax
import jax.numpy as jnp
from jax import lax
from jax.experimental import pallas as pl
from jax.experimental.pallas import tpu as pltpu

F32 = jnp.float32
BF16 = jnp.bfloat16
MESH = pl.DeviceIdType.MESH
ANY = pl.BlockSpec(memory_space=pl.ANY)
VMEM_SPEC = pl.BlockSpec(memory_space=pltpu.VMEM)
SDS = jax.ShapeDtypeStruct

DEPTH = 1
ALPHA = (2.0 * DEPTH) ** 0.25
LN_EPS = 1e-5
RMS_EPS = 1e-6
CHUNK = 32
HEAD = 128
ADAM_LR, ADAM_B1, ADAM_B2, ADAM_EPS, ADAM_WD, ADAM_STEP = 0.001, 0.9, 0.999, 1e-08, 0.01, 10

LANES = 128
N_CHIPS = 4
N_DEV = 8
VMEM_LIMIT = 52 * 1024 * 1024


def _cparams(*sem):
    if sem:
        return pltpu.CompilerParams(dimension_semantics=sem, vmem_limit_bytes=VMEM_LIMIT)
    return pltpu.CompilerParams(vmem_limit_bytes=VMEM_LIMIT)


def _tile(n, target, mult):
    best = None
    for t in range(mult, min(n, target) + 1, mult):
        if n % t == 0:
            best = t
    return best if best is not None else n


def _sigmoid(x):
    return 1.0 / (1.0 + jnp.exp(-x))


_Stage = collections.namedtuple("_Stage", "ins out_shapes aliases sems start finish")


def _hosted_call(compute, stages, *, name, grid, in_specs, out_specs, out_shape, scratch_shapes, operands, parallel,
                 prefetch=None):
    n_cmp, n_out, n_scr = len(in_specs), len(out_specs), len(scratch_shapes)
    n_in = n_cmp
    n_pre = int(prefetch is not None)
    c_in = [len(s.ins) for s in stages]
    c_out = [len(s.out_shapes) for s in stages]
    c_sem = [len(s.sems) for s in stages]
    aliases = {}
    for si, s in enumerate(stages):
        for a_in, a_out in s.aliases.items():
            aliases[n_pre + n_in + sum(c_in[:si]) + a_in] = n_out + sum(c_out[:si]) + a_out

    def body(*refs):
        refs = refs[n_pre:]
        ins = refs[:n_cmp]
        cins = refs[n_in:n_in + sum(c_in)]
        outs = refs[n_in + sum(c_in):n_in + sum(c_in) + n_out]
        couts = refs[n_in + sum(c_in) + n_out:n_in + sum(c_in) + n_out + sum(c_out)]
        scr = refs[n_in + sum(c_in) + n_out + sum(c_out):][:n_scr]
        sems = refs[n_in + sum(c_in) + n_out + sum(c_out) + n_scr:]

        def stage_refs(si):
            return (cins[sum(c_in[:si]):sum(c_in[:si + 1])], couts[sum(c_out[:si]):sum(c_out[:si + 1])],
                    sems[sum(c_sem[:si]):sum(c_sem[:si + 1])])

        if stages:
            first = functools.reduce(jnp.logical_and, [pl.program_id(ax) == 0 for ax in range(len(grid))])
            last = functools.reduce(jnp.logical_and, [pl.program_id(ax) == grid[ax] - 1 for ax in range(len(grid))])

            @pl.when(first)
            def _():
                for si, s in enumerate(stages):
                    s.start(*stage_refs(si))

        compute(*ins, *outs, *scr)
        if stages:
            @pl.when(last)
            def _():
                for si, s in enumerate(stages):
                    s.finish(*stage_refs(si))

    sem = ("arbitrary",) * len(grid) if stages else ("parallel",) * parallel + ("arbitrary",) * (len(grid) - parallel)
    all_in = list(in_specs) + [ANY] * (n_in - n_cmp + sum(c_in))
    all_out = list(out_specs) + [ANY] * sum(c_out)
    all_scr = list(scratch_shapes) + [q for s in stages for q in s.sems]
    all_shape = list(out_shape) + [o for s in stages for o in s.out_shapes]
    args = list(operands) + [a for s in stages for a in s.ins]
    if prefetch is None:
        res = pl.pallas_call(body, name=name, grid=grid, in_specs=all_in, out_specs=all_out, out_shape=all_shape,
                             input_output_aliases=aliases, scratch_shapes=all_scr, compiler_params=_cparams(*sem))(*args)
    else:
        grid_spec = pltpu.PrefetchScalarGridSpec(num_scalar_prefetch=1, grid=grid, in_specs=all_in, out_specs=all_out,
                                                 scratch_shapes=all_scr)
        res = pl.pallas_call(body, name=name, grid_spec=grid_spec, out_shape=all_shape, input_output_aliases=aliases,
                             compiler_params=_cparams(*sem))(prefetch, *args)
    main = res[0] if n_out == 1 else list(res[:n_out])
    if not stages:
        return main
    rest = res[n_out:]
    return (main, *[list(rest[sum(c_out[:si]):sum(c_out[:si + 1])]) for si in range(len(stages))])


def _run_stages(stages, name):
    def body(*refs):
        n_i = sum(len(s.ins) for s in stages)
        n_o = sum(len(s.out_shapes) for s in stages)
        cins, couts, sems = refs[:n_i], refs[n_i:n_i + n_o], refs[n_i + n_o:]
        pos = [0, 0, 0]
        parts = []
        for s in stages:
            parts.append((cins[pos[0]:pos[0] + len(s.ins)], couts[pos[1]:pos[1] + len(s.out_shapes)], sems[pos[2]:pos[2] + len(s.sems)]))
            pos = [pos[0] + len(s.ins), pos[1] + len(s.out_shapes), pos[2] + len(s.sems)]
        for s, p_ in zip(stages, parts):
            s.start(*p_)
        for s, p_ in zip(stages, parts):
            s.finish(*p_)

    aliases, ni, no = {}, 0, 0
    for s in stages:
        for a_in, a_out in s.aliases.items():
            aliases[ni + a_in] = no + a_out
        ni, no = ni + len(s.ins), no + len(s.out_shapes)
    res = pl.pallas_call(
        body, name=name, in_specs=[ANY] * ni, out_specs=[ANY] * no, out_shape=[o for s in stages for o in s.out_shapes],
        input_output_aliases=aliases, scratch_shapes=[q for s in stages for q in s.sems],
    )(*[a for s in stages for a in s.ins])
    out, pos = [], 0
    for s in stages:
        out.append(list(res[pos:pos + len(s.out_shapes)]))
        pos += len(s.out_shapes)
    return out


def _mm(a, b, *, name, ta=False, tb=False, b_blocked=False, out_blocked=0, out_dtype=F32,
        tm=512, tn=1408, tk=2048, comm=(), half=None):
    if ta:
        kd, m = a.shape
    else:
        m, kd = a.shape
    if b_blocked and not tb:
        g, kb, nb = b.shape
        assert kb == kd
        n = g * nb
        tn = _tile(nb, tn, LANES)
        tk = _tile(kd, tk, LANES)
        per_n = nb // tn
        b_spec = pl.BlockSpec((None, tk, tn), lambda i, j, k, *s: (j // per_n, k, j % per_n))
    elif b_blocked and tb:
        g, n, kb = b.shape
        assert g * kb == kd
        tn = _tile(n, tn, LANES)
        tk = _tile(kb, tk, LANES)
        per_k = kb // tk
        b_spec = pl.BlockSpec((None, tn, tk), lambda i, j, k, *s: (k // per_k, j, k % per_k))
    elif tb:
        n, kb = b.shape
        assert kb == kd
        tn = _tile(n, tn, LANES)
        tk = _tile(kd, tk, LANES)
        b_spec = pl.BlockSpec((tn, tk), lambda i, j, k, *s: (j, k))
    else:
        kb, n = b.shape
        assert kb == kd
        tn = _tile(n // out_blocked if out_blocked else n, tn, LANES)
        per_o = (n // out_blocked) // tn if out_blocked else None
        tk = _tile(kd, tk, LANES)
        b_spec = pl.BlockSpec((tk, tn), lambda i, j, k, *s: (k, j))
    m_run = m // 2 if half else m
    tm = _tile(m_run, tm, LANES if ta else 8)

    def row(i, s):
        if not half:
            return i
        h = 1 - s[0][0] if half[1] else s[0][0]
        return h * (m_run // tm) + i

    if ta:
        a_spec = pl.BlockSpec((tk, tm), lambda i, j, k, *s: (k, row(i, s)))
    else:
        a_spec = pl.BlockSpec((tm, tk), lambda i, j, k, *s: (row(i, s), k))
    if out_blocked:
        assert not b_blocked and not tb
        o_spec = pl.BlockSpec((None, tm, tn), lambda i, j, k, *s: (j // per_o, row(i, s), j % per_o))
        o_shape = SDS((out_blocked, m, n // out_blocked), out_dtype)
    else:
        o_spec = pl.BlockSpec((tm, tn), lambda i, j, k, *s: (row(i, s), j))
        o_shape = SDS((m, n), out_dtype)
    nk = kd // tk
    dn = (((0 if ta else 1,), (1 if tb else 0,)), ((), ()))
    grid = (m_run // tm, n // tn, nk)

    def compute(a_ref, b_ref, o_ref, acc_ref):
        part = lax.dot_general(a_ref[...].astype(BF16), b_ref[...].astype(BF16), dn, preferred_element_type=F32)
        if nk == 1:
            o_ref[...] = part.astype(o_ref.dtype)
        else:
            k = pl.program_id(2)

            @pl.when(k == 0)
            def _():
                acc_ref[...] = part

            @pl.when(k > 0)
            def _():
                acc_ref[...] += part

            @pl.when(k == nk - 1)
            def _():
                o_ref[...] = acc_ref[...].astype(o_ref.dtype)

    return _hosted_call(compute, comm, name=name, grid=grid, in_specs=[a_spec, b_spec], out_specs=[o_spec], out_shape=[o_shape],
                        scratch_shapes=[pltpu.VMEM((tm, tn), F32)], operands=(a, b), parallel=2,
                        prefetch=half[0] if half else None)


def _swiglu_fwd(z, name):
    t, n = z.shape
    n2 = n // 2
    tr = _tile(t, 128, 16)

    def body(a_ref, u_ref, o_ref):
        a = a_ref[...].astype(F32)
        o_ref[...] = (a * _sigmoid(a) * u_ref[...].astype(F32)).astype(o_ref.dtype)

    return pl.pallas_call(
        body, name=name, grid=(t // tr,),
        in_specs=[pl.BlockSpec((tr, n2), lambda i: (i, 0)), pl.BlockSpec((tr, n2), lambda i: (i, 1))],
        out_specs=pl.BlockSpec((tr, n2), lambda i: (i, 0)), out_shape=SDS((t, n2), BF16),
        compiler_params=_cparams("parallel"),
    )(z, z)


def _swiglu_bwd(dh, z, name):
    t, n = z.shape
    n2 = n // 2
    tr = _tile(t, 128, 16)

    def body(dh_ref, a_ref, u_ref, o_ref):
        a = a_ref[...].astype(F32)
        dh_ = dh_ref[...].astype(F32)
        s = _sigmoid(a)
        o_ref[:, 0:n2] = (dh_ * u_ref[...].astype(F32) * (s * (1.0 + a * (1.0 - s)))).astype(o_ref.dtype)
        o_ref[:, n2:n] = (dh_ * a * s).astype(o_ref.dtype)

    return pl.pallas_call(
        body, name=name, grid=(t // tr,),
        in_specs=[pl.BlockSpec((tr, n2), lambda i: (i, 0)), pl.BlockSpec((tr, n2), lambda i: (i, 0)),
                  pl.BlockSpec((tr, n2), lambda i: (i, 1))],
        out_specs=pl.BlockSpec((tr, n), lambda i: (i, 0)), out_shape=SDS((t, n), BF16),
        compiler_params=_cparams("parallel"),
    )(dh, z, z)


def _ln_stats(r):
    mu = jnp.mean(r, axis=-1, keepdims=True)
    xc = r - mu
    var = jnp.mean(xc * xc, axis=-1, keepdims=True)
    return xc * lax.rsqrt(var + LN_EPS)


def _ln_fwd(xp, y, g, b, scale, name):
    t, d = xp.shape
    tr = _tile(t, 256, 16)

    def body(xp_ref, y_ref, g_ref, b_ref, r_ref, x_ref, xb_ref):
        r = ALPHA * xp_ref[...] + scale * y_ref[...]
        x = _ln_stats(r) * g_ref[...] + b_ref[...]
        r_ref[...] = r
        x_ref[...] = x
        xb_ref[...] = x.astype(BF16)

    row = pl.BlockSpec((tr, d), lambda i: (i, 0))
    vec = pl.BlockSpec((1, d), lambda i: (0, 0))
    return pl.pallas_call(
        body, name=name, grid=(t // tr,), in_specs=[row, row, vec, vec], out_specs=[row, row, row],
        out_shape=[SDS((t, d), F32), SDS((t, d), F32), SDS((t, d), BF16)], compiler_params=_cparams("parallel"),
    )(xp, y, g, b)


def _ln_bwd(dra, dxm, r, g, scale, name):
    t, d = r.shape
    tr = _tile(t, 256, 16)

    def body(dra_ref, dxm_ref, r_ref, g_ref, dr_ref, dyb_ref, dg_ref, db_ref):
        i = pl.program_id(0)
        dx = ALPHA * dra_ref[...] + dxm_ref[...]
        rr = r_ref[...]
        mu = jnp.mean(rr, axis=-1, keepdims=True)
        xc = rr - mu
        rstd = lax.rsqrt(jnp.mean(xc * xc, axis=-1, keepdims=True) + LN_EPS)
        xh = xc * rstd
        dxh = dx * g_ref[...]
        dr = rstd * (dxh - jnp.mean(dxh, axis=-1, keepdims=True) - xh * jnp.mean(dxh * xh, axis=-1, keepdims=True))
        dr_ref[...] = dr
        dyb_ref[...] = (scale * dr).astype(BF16)
        dg = jnp.sum(dx * xh, axis=0, keepdims=True)
        db = jnp.sum(dx, axis=0, keepdims=True)

        @pl.when(i == 0)
        def _():
            dg_ref[...] = dg
            db_ref[...] = db

        @pl.when(i > 0)
        def _():
            dg_ref[...] += dg
            db_ref[...] += db

    row = pl.BlockSpec((tr, d), lambda i: (i, 0))
    vec = pl.BlockSpec((1, d), lambda i: (0, 0))
    return pl.pallas_call(
        body, name=name, grid=(t // tr,), in_specs=[row, row, row, vec], out_specs=[row, row, vec, vec],
        out_shape=[SDS((t, d), F32), SDS((t, d), BF16), SDS((1, d), F32), SDS((1, d), F32)],
        compiler_params=_cparams("arbitrary"),
    )(dra, dxm, r, g)


def _tail(x3, gp, pp, g, b, target, name):
    t, d = x3.shape
    tr = _tile(t, 256, 16)

    def body(x3_ref, gp_ref, pp_ref, g_ref, b_ref, tg_ref, dr_ref, dgp_ref, dpp_ref, dg_ref, db_ref, sq_ref):
        i = pl.program_id(0)
        gate = _sigmoid(gp_ref[...])
        pp_ = pp_ref[...]
        r = ALPHA * x3_ref[...] + gate * pp_
        mu = jnp.mean(r, axis=-1, keepdims=True)
        xc = r - mu
        rstd = lax.rsqrt(jnp.mean(xc * xc, axis=-1, keepdims=True) + LN_EPS)
        xh = xc * rstd
        err = xh * g_ref[...] + b_ref[...] - tg_ref[...]
        dx = err * (1.0 / d)
        dxh = dx * g_ref[...]
        dr = rstd * (dxh - jnp.mean(dxh, axis=-1, keepdims=True) - xh * jnp.mean(dxh * xh, axis=-1, keepdims=True))
        dr_ref[...] = dr
        dgp_ref[...] = (dr * pp_ * gate * (1.0 - gate)).astype(BF16)
        dpp_ref[...] = (dr * gate).astype(BF16)
        dg = jnp.sum(dx * xh, axis=0, keepdims=True)
        db = jnp.sum(dx, axis=0, keepdims=True)
        sq = jnp.sum(err * err, axis=0, keepdims=True)

        @pl.when(i == 0)
        def _():
            dg_ref[...] = dg
            db_ref[...] = db
            sq_ref[...] = sq

        @pl.when(i > 0)
        def _():
            dg_ref[...] += dg
            db_ref[...] += db
            sq_ref[...] += sq

    row = pl.BlockSpec((tr, d), lambda i: (i, 0))
    vec = pl.BlockSpec((1, d), lambda i: (0, 0))
    return pl.pallas_call(
        body, name=name, grid=(t // tr,), in_specs=[row, row, row, vec, vec, row],
        out_specs=[row, row, row, vec, vec, vec],
        out_shape=[SDS((t, d), F32), SDS((t, d), BF16), SDS((t, d), BF16), SDS((1, d), F32), SDS((1, d), F32),
                   SDS((1, d), F32)],
        compiler_params=_cparams("arbitrary"),
    )(x3, gp, pp, g, b, target)


def _to_bf16(x, name):
    t, d = x.shape
    tr = _tile(t, 512, 16)
    row = pl.BlockSpec((tr, d), lambda i: (i, 0))

    def body(x_ref, o_ref):
        o_ref[...] = x_ref[...].astype(BF16)

    return pl.pallas_call(body, name=name, grid=(t // tr,), in_specs=[row], out_specs=row, out_shape=SDS((t, d), BF16),
                          compiler_params=_cparams("parallel"))(x)


def _concat_cols(parts, name):
    t = parts[0].shape[0]
    widths = [p_.shape[1] for p_ in parts]
    tr = _tile(t, 256, 16)

    def body(*refs):
        o_ref = refs[-1]
        at = 0
        for ref, wd in zip(refs[:-1], widths):
            o_ref[:, at:at + wd] = ref[...]
            at += wd

    return pl.pallas_call(
        body, name=name, grid=(t // tr,), in_specs=[pl.BlockSpec((tr, wd), lambda i: (i, 0)) for wd in widths],
        out_specs=pl.BlockSpec((tr, sum(widths)), lambda i: (i, 0)), out_shape=SDS((t, sum(widths)), parts[0].dtype),
        compiler_params=_cparams("parallel"),
    )(*parts)


def _residual_out(dra, dxm, name):
    t, d = dra.shape
    tr = _tile(t, 256, 8)

    def body(a_ref, b_ref, o_ref):
        o_ref[...] = ALPHA * a_ref[...] + b_ref[...]

    row = pl.BlockSpec((tr, d), lambda i: (i, 0))
    return pl.pallas_call(body, name=name, grid=(t // tr,), in_specs=[row, row], out_specs=row,
                          out_shape=SDS((t, d), F32), compiler_params=_cparams("parallel"))(dra, dxm)


def _merge_fwd(z, ma, mb, w, name):
    t = z.shape[0]
    tr = _tile(t, 256, 16)

    def body(gc_ref, gh_ref, ma_ref, mb_ref, o_ref):
        o_ref[...] = (_sigmoid(gc_ref[...]) * ma_ref[...] + _sigmoid(gh_ref[...]) * mb_ref[...]).astype(BF16)

    half = pl.BlockSpec((tr, w), lambda i, j: (i, j))
    return pl.pallas_call(
        body, name=name, grid=(t // tr, 2),
        in_specs=[pl.BlockSpec((tr, w), lambda i, j: (i, 7 + j)), pl.BlockSpec((tr, w), lambda i, j: (i, 9 + j)), half, half],
        out_specs=half, out_shape=SDS((t, 2 * w), BF16), compiler_params=_cparams("parallel", "parallel"),
    )(z, z, ma, mb)


def _merge_bwd(dmer, z, ma, mb, w, name):
    t = z.shape[0]
    tr = _tile(t, 256, 16)

    def body(d_ref, gc_ref, gh_ref, ma_ref, mb_ref, dma_ref, dmb_ref, dgc_ref, dgh_ref):
        dm = d_ref[...]
        sc = _sigmoid(gc_ref[...])
        sh = _sigmoid(gh_ref[...])
        dma_ref[...] = (dm * sc).astype(BF16)
        dmb_ref[...] = (dm * sh).astype(BF16)
        dgc_ref[...] = (dm * ma_ref[...] * sc * (1.0 - sc)).astype(BF16)
        dgh_ref[...] = (dm * mb_ref[...] * sh * (1.0 - sh)).astype(BF16)

    half = pl.BlockSpec((tr, w), lambda i, j: (i, j))
    return pl.pallas_call(
        body, name=name, grid=(t // tr, 2),
        in_specs=[half, pl.BlockSpec((tr, w), lambda i, j: (i, 7 + j)), pl.BlockSpec((tr, w), lambda i, j: (i, 9 + j)), half, half],
        out_specs=[half] * 4, out_shape=[SDS((t, 2 * w), BF16)] * 4, compiler_params=_cparams("parallel", "parallel"),
    )(dmer, z, z, ma, mb)


def _shift_down(x, s, row):
    return jnp.where(row >= s, pltpu.roll(x, s, axis=0), 0.0)


def _shift_up(x, s, row, t):
    return jnp.where(row < t - s, pltpu.roll(x, t - s, axis=0), 0.0)


def _conv_fwd(z, cw, w, name):
    t = z.shape[0]
    tc = LANES
    nb = w // tc

    def body(b_ref, c_ref, h_ref, w_ref, o_ref):
        u = c_ref[...] * h_ref[...]
        row = lax.broadcasted_iota(jnp.int32, u.shape, 0)
        cw_ = w_ref[...]
        conv = cw_[2:3, :] * u + cw_[1:2, :] * _shift_down(u, 1, row) + cw_[0:1, :] * _shift_down(u, 2, row)
        o_ref[...] = (b_ref[...] * conv).astype(BF16)

    col = lambda off: pl.BlockSpec((t, tc), lambda j: (0, off * nb + j))
    return pl.pallas_call(
        body, name=name, grid=(nb,), in_specs=[col(0), col(1), col(2), pl.BlockSpec((3, tc), lambda j: (0, j))],
        out_specs=pl.BlockSpec((t, tc), lambda j: (0, j)), out_shape=SDS((t, w), BF16), compiler_params=_cparams("parallel"),
    )(z, z, z, cw)


def _conv_bwd(dy, z, cw, w, name):
    t = z.shape[0]
    tc = LANES
    nb = w // tc

    def body(dy_ref, b_ref, c_ref, h_ref, w_ref, db_ref, dc_ref, dh_ref, dw_ref):
        c_, h_ = c_ref[...], h_ref[...]
        u = c_ * h_
        row = lax.broadcasted_iota(jnp.int32, u.shape, 0)
        cw_ = w_ref[...]
        u1 = _shift_down(u, 1, row)
        u2 = _shift_down(u, 2, row)
        dy_ = dy_ref[...]
        db_ref[...] = (dy_ * (cw_[2:3, :] * u + cw_[1:2, :] * u1 + cw_[0:1, :] * u2)).astype(BF16)
        dconv = dy_ * b_ref[...]
        du = cw_[2:3, :] * dconv + cw_[1:2, :] * _shift_up(dconv, 1, row, t) + cw_[0:1, :] * _shift_up(dconv, 2, row, t)
        dc_ref[...] = (du * h_).astype(BF16)
        dh_ref[...] = (du * c_).astype(BF16)
        dw_ref[0:1, :] = jnp.sum(dconv * u2, axis=0, keepdims=True)
        dw_ref[1:2, :] = jnp.sum(dconv * u1, axis=0, keepdims=True)
        dw_ref[2:3, :] = jnp.sum(dconv * u, axis=0, keepdims=True)

    col = lambda off: pl.BlockSpec((t, tc), lambda j: (0, off * nb + j))
    own = pl.BlockSpec((t, tc), lambda j: (0, j))
    wsp = pl.BlockSpec((3, tc), lambda j: (0, j))
    return pl.pallas_call(
        body, name=name, grid=(nb,), in_specs=[own, col(0), col(1), col(2), wsp], out_specs=[own, own, own, wsp],
        out_shape=[SDS((t, w), BF16)] * 3 + [SDS((3, w), F32)], compiler_params=_cparams("parallel"),
    )(dy, z, z, z, cw)


def _lower_bound(hg):
    mx = jnp.max(hg, axis=0, keepdims=True)
    e = jnp.exp(hg - mx)
    inv = 1.0 / jnp.sum(e, axis=0, keepdims=True)
    return e[0:1, :] * inv, e[1:2, :] * inv


def _chunk_cumsum(x, row):
    s = 1
    while s < CHUNK:
        x = x + jnp.where(row % CHUNK >= s, pltpu.roll(x, s, axis=0), 0.0)
        s *= 2
    return x


def _dot_nt(a, b):
    return lax.dot_general(a.astype(BF16), b.astype(BF16), (((1,), (1,)), ((), ())), preferred_element_type=F32)


def _dot_tn(a, b):
    return lax.dot_general(a.astype(BF16), b.astype(BF16), (((0,), (0,)), ((), ())), preferred_element_type=F32)


def _dot_nn(a, b):
    return jnp.dot(a.astype(BF16), b.astype(BF16), preferred_element_type=F32)


def _tril(x):
    r = lax.broadcasted_iota(jnp.int32, x.shape, 0)
    c = lax.broadcasted_iota(jnp.int32, x.shape, 1)
    return jnp.where(r >= c, x, 0.0)


HGRN_GROUP = 4
HGRN_ROWS = 512


def _hgrn_chunk_inputs(q_ref, f_ref, cum_ref, lb, rows, ln):
    qr = q_ref[rows, ln]
    q = qr * _sigmoid(qr)
    f = lb + (1.0 - lb) * _sigmoid(f_ref[rows, ln])
    return q, 1.0 - f, cum_ref[rows, ln]


def _hgrn_fwd(z, hg, nw, w, name, comm=()):
    t = z.shape[0]
    nh = w // HEAD
    gh = _tile(nh, HGRN_GROUP, 1)
    gw = gh * HEAD
    ngrp = nh // gh
    tb = _tile(t, HGRN_ROWS, CHUNK)
    ncb = tb // CHUNK

    def body(q_ref, f_ref, i_ref, g_ref, hg_ref, nw_ref, y_ref, o_ref, st_ref, cum_ref, *s_refs):
        lb_all, _ = _lower_bound(hg_ref[...])
        row = lax.broadcasted_iota(jnp.int32, (tb, gw), 0)
        cum_ref[...] = _chunk_cumsum(jnp.log(lb_all + (1.0 - lb_all) * _sigmoid(f_ref[...])), row)

        @pl.when(pl.program_id(1) == 0)
        def _():
            for s_ref in s_refs:
                s_ref[...] = jnp.zeros_like(s_ref)

        def step(c, carry):
            rows = pl.ds(pl.multiple_of(c * CHUNK, CHUNK), CHUNK)
            for g in range(gh):
                ln = slice(g * HEAD, (g + 1) * HEAD)
                lb = lb_all[:, ln]
                q, k, cum = _hgrn_chunk_inputs(q_ref, f_ref, cum_ref, lb, rows, ln)
                v = i_ref[rows, ln]
                last = cum[CHUNK - 1:CHUNK, :]
                qe = q * jnp.exp(cum)
                st = s_refs[g][...]
                st_ref[g, c] = st.astype(BF16)
                o_ref[rows, ln] = _dot_nt(qe, st) + _dot_nn(_tril(_dot_nt(qe, k * jnp.exp(-cum))), v)
                s_refs[g][...] = st * jnp.exp(last) + _dot_tn(v, k * jnp.exp(last - cum))
            return carry

        lax.fori_loop(0, ncb, step, 0)
        for g in range(gh):
            ln = slice(g * HEAD, (g + 1) * HEAD)
            o = o_ref[:, ln]
            n = o * lax.rsqrt(jnp.mean(o * o, axis=-1, keepdims=True) + RMS_EPS)
            gr = g_ref[:, ln]
            y_ref[:, ln] = (n * nw_ref[...] * gr * _sigmoid(gr)).astype(BF16)

    col = lambda off: pl.BlockSpec((tb, gw), lambda h, j: (j, off * ngrp + h))
    own = pl.BlockSpec((tb, gw), lambda h, j: (j, h))
    return _hosted_call(
        body, comm, name=name, grid=(ngrp, t // tb),
        in_specs=[col(3), col(4), col(5), col(6), pl.BlockSpec((2, gw), lambda h, j: (0, h)),
                  pl.BlockSpec((1, HEAD), lambda h, j: (0, 0))],
        out_specs=[own, own, pl.BlockSpec((gh, ncb, HEAD, HEAD), lambda h, j: (h, j, 0, 0))],
        out_shape=[SDS((t, w), BF16), SDS((t, w), F32), SDS((nh, t // CHUNK, HEAD, HEAD), BF16)],
        scratch_shapes=[pltpu.VMEM((tb, gw), F32)] + [pltpu.VMEM((HEAD, HEAD), F32)] * gh,
        operands=(z, z, z, z, hg, nw), parallel=1)


def _hgrn_bwd(dy, z, o, states, hg, nw, w, name, comm=()):
    t = z.shape[0]
    nh = w // HEAD
    gh = _tile(nh, HGRN_GROUP, 1)
    gw = gh * HEAD
    ngrp = nh // gh
    tb = _tile(t, HGRN_ROWS, CHUNK)
    ncb = tb // CHUNK
    nt = t // tb

    def body(dy_ref, q_ref, f_ref, i_ref, g_ref, o_ref, st_ref, hg_ref, nw_ref,
             dq_ref, df_ref, di_ref, dg_ref, dhg_ref, dnw_ref, cum_ref, do_ref, *ds_refs):
        lb_all, s1_all = _lower_bound(hg_ref[...])
        row = lax.broadcasted_iota(jnp.int32, (tb, gw), 0)
        crow = lax.broadcasted_iota(jnp.int32, (CHUNK, HEAD), 0)
        cum_ref[...] = _chunk_cumsum(jnp.log(lb_all + (1.0 - lb_all) * _sigmoid(f_ref[...])), row)

        @pl.when(pl.program_id(1) == 0)
        def _():
            for ds_ref in ds_refs:
                ds_ref[...] = jnp.zeros_like(ds_ref)
            dhg_ref[...] = jnp.zeros_like(dhg_ref)
            dnw_ref[...] = jnp.zeros_like(dnw_ref)

        for g in range(gh):
            ln = slice(g * HEAD, (g + 1) * HEAD)
            o_ = o_ref[:, ln]
            rstd = lax.rsqrt(jnp.mean(o_ * o_, axis=-1, keepdims=True) + RMS_EPS)
            n = o_ * rstd
            gr = g_ref[:, ln]
            sg = _sigmoid(gr)
            dy_ = dy_ref[:, ln]
            dg_ref[:, ln] = (dy_ * n * nw_ref[...] * (sg * (1.0 + gr * (1.0 - sg)))).astype(BF16)
            dsil = dy_ * gr * sg
            dnw_ref[:, ln] += jnp.sum(dsil * n, axis=0, keepdims=True)
            dn = dsil * nw_ref[...]
            do_ref[:, ln] = rstd * (dn - n * jnp.mean(dn * n, axis=-1, keepdims=True))

        def step(cc, dlbs):
            c = ncb - 1 - cc
            rows = pl.ds(pl.multiple_of(c * CHUNK, CHUNK), CHUNK)
            new = []
            for g in range(gh):
                ln = slice(g * HEAD, (g + 1) * HEAD)
                lb = lb_all[:, ln]
                qr = q_ref[rows, ln]
                sq = _sigmoid(qr)
                q = qr * sq
                sf = _sigmoid(f_ref[rows, ln])
                f = lb + (1.0 - lb) * sf
                k = 1.0 - f
                cum = cum_ref[rows, ln]
                v = i_ref[rows, ln]
                do = do_ref[rows, ln]
                last = cum[CHUNK - 1:CHUNK, :]
                eg = jnp.exp(cum)
                eng = jnp.exp(-cum)
                elc = jnp.exp(last - cum)
                qe, ke, kl = q * eg, k * eng, k * elc
                ds = ds_refs[g][...]
                a = _tril(_dot_nt(qe, ke))
                da = _tril(_dot_nt(do, v))
                di_ref[rows, ln] = (_dot_tn(a, do) + _dot_nt(kl, ds)).astype(BF16)
                st = st_ref[g, c]
                dkl = _dot_nn(v, ds)
                dq = (_dot_nn(do, st) + _dot_nn(da, ke)) * eg
                dk = _dot_tn(da, qe) * eng + dkl * elc
                el = jnp.exp(last)
                ds_refs[g][...] = ds * el + _dot_tn(do, qe)
                dlast = jnp.sum(kl * dkl, axis=0, keepdims=True) + el * jnp.sum(ds * st.astype(F32), axis=0, keepdims=True)
                x = q * dq - k * dk + jnp.where(crow == CHUNK - 1, dlast, 0.0)
                s = 1
                while s < CHUNK:
                    x = x + _shift_up(x, s, crow, CHUNK)
                    s *= 2
                df = x / f - dk
                dq_ref[rows, ln] = (dq * (sq * (1.0 + qr * (1.0 - sq)))).astype(BF16)
                df_ref[rows, ln] = (df * (1.0 - lb) * sf * (1.0 - sf)).astype(BF16)
                new.append(dlbs[g] + jnp.sum(df * (1.0 - sf), axis=0, keepdims=True))
            return tuple(new)

        dlbs = lax.fori_loop(0, ncb, step, tuple(jnp.zeros((1, HEAD), F32) for _ in range(gh)))
        for g in range(gh):
            ln = slice(g * HEAD, (g + 1) * HEAD)
            dlb = dlbs[g] * lb_all[:, ln] * s1_all[:, ln]
            dhg_ref[0:1, ln] += dlb
            dhg_ref[1:2, ln] -= dlb

    col = lambda off: pl.BlockSpec((tb, gw), lambda h, j: (nt - 1 - j, off * ngrp + h))
    own = pl.BlockSpec((tb, gw), lambda h, j: (nt - 1 - j, h))
    hsp = pl.BlockSpec((2, gw), lambda h, j: (0, h))
    return _hosted_call(
        body, comm, name=name, grid=(ngrp, nt),
        in_specs=[own, col(3), col(4), col(5), col(6), own,
                  pl.BlockSpec((gh, ncb, HEAD, HEAD), lambda h, j: (h, nt - 1 - j, 0, 0)),
                  hsp, pl.BlockSpec((1, HEAD), lambda h, j: (0, 0))],
        out_specs=[own, own, own, own, hsp, pl.BlockSpec((1, gw), lambda h, j: (0, h))],
        out_shape=[SDS((t, w), BF16)] * 4 + [SDS((2, w), F32), SDS((1, w), F32)],
        scratch_shapes=[pltpu.VMEM((tb, gw), F32)] * 2 + [pltpu.VMEM((HEAD, HEAD), F32)] * gh,
        operands=(dy, z, z, z, z, o, states, hg, nw), parallel=1)


def _cast_pad(wt, n_pad, meta, sp, name):
    r, n = wt.shape
    g, p, per = meta
    tr = _tile(r, max(16, (3 << 19) // n_pad // 16 * 16), 16)

    def body(sp_ref, w_ref, o_ref):
        if n_pad != n:
            o_ref[...] = jnp.zeros(o_ref.shape, o_ref.dtype)
        o_ref[:, 0:n] = w_ref[...].astype(BF16)

    grid_spec = pltpu.PrefetchScalarGridSpec(
        num_scalar_prefetch=1, grid=(r // tr,), in_specs=[pl.BlockSpec((tr, n), lambda i, sp: (i, 0))],
        out_specs=pl.BlockSpec((None, tr, n_pad), lambda i, sp: (sp[1] // per, ((sp[1] % per) * r) // tr + i, 0)))
    return pl.pallas_call(body, name=name, grid_spec=grid_spec, out_shape=SDS((g, p, n_pad), BF16),
                          compiler_params=_cparams("parallel"))(sp, wt)


def _adamw(wt, g, m, v, name):
    r, n = wt.shape
    ng = g.shape[1]
    nct = 2 if ng == n and n % (2 * LANES) == 0 else 1
    tc, tg = n // nct, ng // nct
    tr = _tile(r, max(8, (3 << 17) // tg // 8 * 8), 8)
    c1 = 1.0 / (1.0 - ADAM_B1 ** ADAM_STEP)
    c2 = 1.0 / (1.0 - ADAM_B2 ** ADAM_STEP)

    def body(w_ref, g_ref, m_ref, v_ref, go_ref, d_ref, mo_ref, vo_ref):
        g_ = g_ref[:, 0:tc]
        m2 = ADAM_B1 * m_ref[...] + (1.0 - ADAM_B1) * g_
        v2 = ADAM_B2 * v_ref[...] + (1.0 - ADAM_B2) * (g_ * g_)
        go_ref[...] = g_
        mo_ref[...] = m2
        vo_ref[...] = v2
        d_ref[...] = -ADAM_LR * ((m2 * c1) / (jnp.sqrt(v2 * c2) + ADAM_EPS) + ADAM_WD * w_ref[...])

    blk = pl.BlockSpec((tr, tc), lambda i, j: (i, j))
    return pl.pallas_call(
        body, name=name, grid=(r // tr, nct), in_specs=[blk, pl.BlockSpec((tr, tg), lambda i, j: (i, j)), blk, blk],
        out_specs=[blk] * 4, out_shape=[SDS((r, n), F32)] * 4, compiler_params=_cparams("parallel", "parallel"),
    )(wt, g, m, v)


def _place():
    x, y, c = lax.axis_index("x"), lax.axis_index("y"), lax.axis_index("c")
    return x, y, c, 2 * x + y


def _chip_dev(k, c):
    return (k // 2, k % 2, c)


def _half(ref, j, h, rows, per):
    return ref.at[j // per, pl.ds((j % per) * rows + h * (rows // 2), rows // 2)]


def _gather_stage(bufs, metas, rows_of, parts, zero_pad):
    nw = len(bufs)
    pad_jobs = [(i, gi) for i in range(nw) if parts[i][0] == 0 and metas[i][1] > metas[i][2] * rows_of[i]
                for gi in range(metas[i][0])]

    def part_of(ref, i, j, h):
        per = metas[i][2]
        p, np_ = parts[i]
        pr = rows_of[i] // 2 // np_
        return ref.at[j // per, pl.ds((j % per) * rows_of[i] + h * (rows_of[i] // 2) + p * pr, pr)]

    def descriptors(ins, outs, sems):
        src, zp, dst = ins[:nw], ins[nw], outs
        pads, send, recv, fsend, frecv = sems
        x, y, c, me = _place()

        def pad(n):
            i, gi = pad_jobs[n]
            extra = metas[i][1] - metas[i][2] * rows_of[i]
            return pltpu.make_async_copy(zp.at[pl.ds(0, extra)], dst[i].at[gi, pl.ds(metas[i][2] * rows_of[i], extra)], pads.at[n])

        def ici(i, r, frm):
            return pltpu.make_async_remote_copy(
                src_ref=part_of(src[i], i, me, c), dst_ref=part_of(dst[i], i, frm, c), send_sem=send.at[i, r - 1],
                recv_sem=recv.at[i, r - 1], device_id=_chip_dev((me + r) % N_CHIPS, c), device_id_type=MESH)

        def d2d(i, r, frm, h):
            blk = part_of(dst[i], i, frm, h)
            return pltpu.make_async_remote_copy(src_ref=blk, dst_ref=blk, send_sem=fsend.at[i, r - 1],
                                                recv_sem=frecv.at[i, r - 1], device_id=(x, y, 1 - c), device_id_type=MESH)

        return pad, ici, d2d, c, me

    def start(ins, outs, sems):
        pad, ici, d2d, c, me = descriptors(ins, outs, sems)
        for n in range(len(pad_jobs)):
            pad(n).start()
        for i in range(nw):
            for r in range(1, N_CHIPS):
                ici(i, r, me).start()

    def finish(ins, outs, sems):
        pad, ici, d2d, c, me = descriptors(ins, outs, sems)
        for i in range(nw):
            for r in range(1, N_CHIPS):
                frm = (me - r) % N_CHIPS
                ici(i, r, frm).wait_recv()
                d2d(i, r, frm, c).start()
        for i in range(nw):
            for r in range(1, N_CHIPS):
                d2d(i, r, (me - r) % N_CHIPS, 1 - c).wait_recv()
        for i in range(nw):
            for r in range(1, N_CHIPS):
                ici(i, r, me).wait_send()
                d2d(i, r, (me - r) % N_CHIPS, c).wait_send()
        for n in range(len(pad_jobs)):
            pad(n).wait()

    return _Stage(ins=list(bufs) + [zero_pad], out_shapes=[SDS(b.shape, b.dtype) for b in bufs],
                  aliases={i: i for i in range(nw)},
                  sems=[pltpu.SemaphoreType.DMA((max(len(pad_jobs), 1),))] + [pltpu.SemaphoreType.DMA((nw, N_CHIPS - 1))] * 4,
                  start=start, finish=finish)


def _gather_small(packed, name):
    r, n = packed.shape

    def body(src, dst, send, recv):
        x, y, c, me = _place()
        dst[me] = src[...]
        cps = []
        for d in range(1, N_CHIPS):
            cp = pltpu.make_async_remote_copy(src_ref=src, dst_ref=dst.at[me], send_sem=send.at[d - 1], recv_sem=recv.at[d - 1],
                                              device_id=_chip_dev((me + d) % N_CHIPS, c), device_id_type=MESH)
            cp.start()
            cps.append(cp)
        for d in range(1, N_CHIPS):
            pltpu.make_async_remote_copy(src_ref=src, dst_ref=dst.at[(me - d) % N_CHIPS], send_sem=send.at[d - 1],
                                         recv_sem=recv.at[d - 1], device_id=_chip_dev((me + d) % N_CHIPS, c),
                                         device_id_type=MESH).wait_recv()
        for cp in cps:
            cp.wait_send()

    return pl.pallas_call(
        body, name=name, in_specs=[VMEM_SPEC], out_specs=VMEM_SPEC, out_shape=SDS((N_CHIPS, r, n), F32),
        scratch_shapes=[pltpu.SemaphoreType.DMA((N_CHIPS - 1,))] * 2,
    )(packed)


def _all_reduce_small(packed, name):
    r, n = packed.shape

    def body(src, out, slots, send, recv):
        x, y, c, me = _place()
        idx = 2 * me + c
        slots[idx] = src[...]
        cps = []

        def peer(d):
            p = (idx + d) % N_DEV
            return (p // 4, (p // 2) % 2, p % 2)

        for d in range(1, N_DEV):
            cp = pltpu.make_async_remote_copy(src_ref=src, dst_ref=slots.at[idx], send_sem=send.at[d - 1], recv_sem=recv.at[d - 1],
                                              device_id=peer(d), device_id_type=MESH)
            cp.start()
            cps.append(cp)
        for d in range(1, N_DEV):
            pltpu.make_async_remote_copy(src_ref=src, dst_ref=slots.at[(idx - d) % N_DEV], send_sem=send.at[d - 1],
                                         recv_sem=recv.at[d - 1], device_id=peer(d), device_id_type=MESH).wait_recv()
        for cp in cps:
            cp.wait_send()
        acc = slots[0]
        for k in range(1, N_DEV):
            acc = acc + slots[k]
        out[...] = acc

    return pl.pallas_call(
        body, name=name, in_specs=[VMEM_SPEC], out_specs=VMEM_SPEC, out_shape=SDS((r, n), F32),
        scratch_shapes=[pltpu.VMEM((N_DEV, r, n), F32)] + [pltpu.SemaphoreType.DMA((N_DEV - 1,))] * 2,
    )(packed)


def _simple_stage(ins, out_shapes, aliases, n_copies, copies):
    def start(ins_, outs, sems):
        for cp in copies(ins_, outs, *sems):
            cp.start()

    def finish(ins_, outs, sems):
        for cp in copies(ins_, outs, *sems):
            cp.wait()

    return _Stage(ins=list(ins), out_shapes=list(out_shapes), aliases=aliases,
                  sems=[pltpu.SemaphoreType.DMA((n_copies,))] * 2, start=start, finish=finish)


def _rs_pair_exchange(grads, metas, rows_of):
    nw = len(grads)

    def copies(src, dst, send, recv):
        x, y, c, me = _place()
        return [pltpu.make_async_remote_copy(
            src_ref=_half(src[i], j, 1 - c, rows_of[i], metas[i][2]), dst_ref=dst[i].at[j], send_sem=send.at[i * N_CHIPS + j],
            recv_sem=recv.at[i * N_CHIPS + j], device_id=(x, y, 1 - c), device_id_type=MESH)
            for i in range(nw) for j in range(N_CHIPS)]

    out_shapes = [SDS((N_CHIPS, rows_of[i] // 2, g.shape[2]), F32) for i, g in enumerate(grads)]
    return _simple_stage(grads, out_shapes, {}, nw * N_CHIPS, copies)


def _rs_pair_add(g, got, meta, rows, sp, name):
    per = meta[2]
    n = g.shape[2]
    hr = rows // 2
    tr = _tile(hr, max(16, (3 << 19) // n // 16 * 16), 16)

    def body(sp_ref, g_ref, got_ref, snd_ref, own_ref):
        j = pl.program_id(1)
        s = g_ref[...] + got_ref[...]
        snd_ref[...] = s.astype(BF16)

        @pl.when(j == sp_ref[1])
        def _():
            own_ref[...] = s

    grid_spec = pltpu.PrefetchScalarGridSpec(
        num_scalar_prefetch=1, grid=(hr // tr, N_CHIPS),
        in_specs=[pl.BlockSpec((None, tr, n), lambda i, j, sp: (j // per, ((j % per) * rows + sp[0] * hr) // tr + i, 0)),
                  pl.BlockSpec((None, tr, n), lambda i, j, sp: (j, i, 0))],
        out_specs=[pl.BlockSpec((None, tr, n), lambda i, j, sp: (j, i, 0)), pl.BlockSpec((tr, n), lambda i, j, sp: (i, 0))])
    return pl.pallas_call(
        body, name=name, grid_spec=grid_spec, out_shape=[SDS((N_CHIPS, hr, n), BF16), SDS((hr, n), F32)],
        compiler_params=_cparams("parallel", "arbitrary"),
    )(sp, g, got)


def _rs_chip_exchange(sends, part=(0, 1), prev=None):
    nw = len(sends)
    p, np_ = part

    def copies(src, dst, send, recv):
        x, y, c, me = _place()
        cps = []
        for i in range(nw):
            pr = sends[i].shape[1] // np_
            for r in range(1, N_CHIPS):
                cps.append(pltpu.make_async_remote_copy(
                    src_ref=src[i].at[(me + r) % N_CHIPS, pl.ds(p * pr, pr)], dst_ref=dst[i].at[r - 1, pl.ds(p * pr, pr)],
                    send_sem=send.at[i * (N_CHIPS - 1) + r - 1], recv_sem=recv.at[i * (N_CHIPS - 1) + r - 1],
                    device_id=_chip_dev((me + r) % N_CHIPS, c), device_id_type=MESH))
        return cps

    out_shapes = [SDS((N_CHIPS - 1,) + s.shape[1:], BF16) for s in sends]
    if prev is None:
        return _simple_stage(sends, out_shapes, {}, nw * (N_CHIPS - 1), copies)
    return _simple_stage(list(sends) + list(prev), out_shapes, {nw + i: i for i in range(nw)}, nw * (N_CHIPS - 1), copies)


def _rs_chip_add(own, got, sp, name):
    hr, n = own.shape
    tr = _tile(hr, max(16, (3 << 19) // n // 16 * 16), 16)

    def body(sp_ref, own_ref, got_ref, o_ref):
        acc = own_ref[...]
        for r in range(N_CHIPS - 1):
            acc = acc + got_ref[r].astype(F32)
        o_ref[...] = acc

    grid_spec = pltpu.PrefetchScalarGridSpec(
        num_scalar_prefetch=1, grid=(hr // tr,),
        in_specs=[pl.BlockSpec((tr, n), lambda i, sp: (i, 0)), pl.BlockSpec((N_CHIPS - 1, tr, n), lambda i, sp: (0, i, 0))],
        out_specs=pl.BlockSpec((tr, n), lambda i, sp: (sp[0] * (hr // tr) + i, 0)))
    return pl.pallas_call(body, name=name, grid_spec=grid_spec, out_shape=SDS((2 * hr, n), F32),
                          compiler_params=_cparams("parallel"))(sp, own, got)


def _rs_pair_share(blocks):
    nw = len(blocks)

    def copies(src, dst, send, recv):
        x, y, c, me = _place()
        cps = []
        for i in range(nw):
            hr = src[i].shape[0] // 2
            cps.append(pltpu.make_async_remote_copy(
                src_ref=src[i].at[pl.ds(c * hr, hr)], dst_ref=dst[i].at[pl.ds(c * hr, hr)], send_sem=send.at[i],
                recv_sem=recv.at[i], device_id=(x, y, 1 - c), device_id_type=MESH))
        return cps

    return _simple_stage(blocks, [SDS(b.shape, b.dtype) for b in blocks], {i: i for i in range(nw)}, nw, copies)


def kernel(x, p, ln_g, ln_b, ffn1_w_in, ffn1_w_out, mix_w_in, conv_w, hg_lower_bound, hg_norm_w, branch_w_conv, branch_w_hgrn, mix_w_out, ffn2_w_in, ffn2_w_out, ple_w_gate, ple_w_proj, loss_target, m_ln_g, m_ln_b, m_ffn1_w_in, m_ffn1_w_out, m_mix_w_in, m_conv_w, m_hg_lower_bound, m_hg_norm_w, m_branch_w_conv, m_branch_w_hgrn, m_mix_w_out, m_ffn2_w_in, m_ffn2_w_out, m_ple_w_gate, m_ple_w_proj, v_ln_g, v_ln_b, v_ffn1_w_in, v_ffn1_w_out, v_mix_w_in, v_conv_w, v_hg_lower_bound, v_hg_norm_w, v_branch_w_conv, v_branch_w_hgrn, v_mix_w_out, v_ffn2_w_in, v_ffn2_w_out, v_ple_w_gate, v_ple_w_proj):
    assert ln_g.shape[0] == DEPTH and x.shape[0] == 1 and p.shape[:2] == (1, 1)
    t, d = x.shape[1], x.shape[2]
    w = d // 2
    x0 = x.reshape(t, d)
    x0b = _to_bf16(x0, "x_bf16")
    pe = p.reshape(t, p.shape[-1])
    target = loss_target.reshape(t, d)
    cx, cy, cc = lax.axis_index("x"), lax.axis_index("y"), lax.axis_index("c")
    chip = 2 * cx + cy
    sp = jnp.stack([cc, chip]).astype(jnp.int32)

    big = dict(ffn1_w_in=ffn1_w_in[0], ffn1_w_out=ffn1_w_out[0], mix_w_in=mix_w_in[0], branch_w_conv=branch_w_conv[0],
               branch_w_hgrn=branch_w_hgrn[0], mix_w_out=mix_w_out[0], ffn2_w_in=ffn2_w_in[0], ffn2_w_out=ffn2_w_out[0],
               ple_w_gate=ple_w_gate[0], ple_w_proj=ple_w_proj[0])
    moments = dict(ffn1_w_in=(m_ffn1_w_in, v_ffn1_w_in), ffn1_w_out=(m_ffn1_w_out, v_ffn1_w_out), mix_w_in=(m_mix_w_in, v_mix_w_in),
                   branch_w_conv=(m_branch_w_conv, v_branch_w_conv), branch_w_hgrn=(m_branch_w_hgrn, v_branch_w_hgrn),
                   mix_w_out=(m_mix_w_out, v_mix_w_out), ffn2_w_in=(m_ffn2_w_in, v_ffn2_w_in), ffn2_w_out=(m_ffn2_w_out, v_ffn2_w_out),
                   ple_w_gate=(m_ple_w_gate, v_ple_w_gate), ple_w_proj=(m_ple_w_proj, v_ple_w_proj))
    names = list(big)

    n_loc = ffn1_w_in.shape[-1]
    n_pad = -(-n_loc // LANES) * LANES
    assert mix_w_in.shape[-1] % LANES == 0 and ffn1_w_out.shape[1] * 2 == n_loc
    pad_cols = dict(ffn1_w_in=n_pad, ffn2_w_in=n_pad)
    meta = {k: (N_CHIPS, big[k].shape[0], 1) for k in names}
    meta["ffn1_w_out"] = meta["ffn2_w_out"] = (2, n_pad, 2)
    rows = {k: big[k].shape[0] for k in names}
    wbuf = {k: _cast_pad(big[k], pad_cols.get(k, big[k].shape[1]), meta[k], sp, "cast_" + k) for k in names}
    zero_pad = jnp.zeros((max(n_pad - n_loc, 16), d), BF16)

    def gather(*items):
        ks = [k for k, _, _ in items]
        return _gather_stage([wbuf[k] for k in ks], [meta[k] for k in ks], [rows[k] for k in ks], [(p_, n_) for _, p_, n_ in items],
                             zero_pad), ks

    def gathered(ks, outs):
        wbuf.update(zip(ks, outs))

    def w3(k):
        return wbuf[k]

    def w2(k):
        return wbuf[k].reshape(-1, wbuf[k].shape[2])

    dq, wq = d // N_CHIPS, w // N_CHIPS
    small = jnp.concatenate([ln_g[0], ln_b[0], jnp.pad(conv_w[0], ((0, 5), (0, dq - wq)))], axis=0)
    small = _gather_small(small, "gather_small")
    lng = small[:, 0:4, :].transpose(1, 0, 2).reshape(4, 1, d)
    lnb = small[:, 4:8, :].transpose(1, 0, 2).reshape(4, 1, d)
    cw = small[:, 8:11, :wq].transpose(1, 0, 2).reshape(3, w)
    hg = hg_lower_bound
    nw_ = hg_norm_w

    st, ks = gather(("ffn1_w_in", 0, 1))
    gathered(ks, _run_stages([st], "gather_first")[0])
    st, ks = gather(("ffn1_w_out", 0, 1), ("mix_w_in", 0, 2))
    z1, got = _mm(x0b, w3("ffn1_w_in"), name="ffn1_in", b_blocked=True, out_dtype=BF16, comm=[st])
    gathered(ks, got)
    h1 = _swiglu_fwd(z1, "ffn1_act")
    st, ks = gather(("mix_w_in", 1, 2))
    y1, got = _mm(h1, w2("ffn1_w_out"), name="ffn1_out", tm=1024, tn=1024, tk=2816, comm=[st])
    gathered(ks, got)
    r1, x1, x1b = _ln_fwd(x0, y1, lng[0], lnb[0], 0.5, "ln0")
    st, ks = gather(("branch_w_conv", 0, 1), ("branch_w_hgrn", 0, 1), ("mix_w_out", 0, 1), ("ffn2_w_in", 0, 2))
    z, got = _mm(x1b, w3("mix_w_in"), name="mix_in", b_blocked=True, comm=[st])
    gathered(ks, got)
    ya = _conv_fwd(z, cw, w, "conv_fwd")
    st, ks = gather(("ffn2_w_in", 1, 2))
    (yb, o_h, states), got = _hgrn_fwd(z, hg, nw_, w, "hgrn_fwd", comm=[st])
    gathered(ks, got)
    ma = _mm(ya, w3("branch_w_conv"), name="branch_conv", b_blocked=True, tn=512)
    mb = _mm(yb, w3("branch_w_hgrn"), name="branch_hgrn", b_blocked=True, tn=512)
    merged = _merge_fwd(z, ma, mb, w, "merge_fwd")
    y2 = _mm(merged, w2("mix_w_out"), name="mix_out", tn=1024)
    r2, x2, x2b = _ln_fwd(x1, y2, lng[1], lnb[1], 1.0, "ln1")
    st, ks = gather(("ffn2_w_out", 0, 1), ("ple_w_gate", 0, 1), ("ple_w_proj", 0, 1))
    z3, got = _mm(x2b, w3("ffn2_w_in"), name="ffn2_in", b_blocked=True, out_dtype=BF16, comm=[st])
    gathered(ks, got)
    h3 = _swiglu_fwd(z3, "ffn2_act")
    y3 = _mm(h3, w2("ffn2_w_out"), name="ffn2_out", tm=1024, tn=1024, tk=2816)
    r3, x3, x3b = _ln_fwd(x2, y3, lng[2], lnb[2], 0.5, "ln2")
    gp = _mm(x3b, w2("ple_w_gate"), name="ple_gate", tn=1024)
    pp = _mm(pe, w3("ple_w_proj"), name="ple_proj", b_blocked=True, tn=512)
    dr4, dgp, dpp, dg3, db3, sq = _tail(x3, gp, pp, lng[3], lnb[3], target, "tail")

    grads, sends, owns, blocks, outs = {}, {}, {}, {}, {}

    def pair_exchange(*ks):
        return _rs_pair_exchange([grads[k] for k in ks], [meta[k] for k in ks], [rows[k] for k in ks])

    def pair_add(ks, got):
        for k, g_ in zip(ks, got):
            sends[k], owns[k] = _rs_pair_add(grads[k], g_, meta[k], rows[k], sp, "rs_pair_add_" + k)

    def chip_exchange(*ks):
        return _rs_chip_exchange([sends[k] for k in ks])

    def chip_add(ks, got):
        for k, g_ in zip(ks, got):
            blocks[k] = _rs_chip_add(owns[k], g_, sp, "rs_chip_add_" + k)

    def pair_share(*ks):
        return _rs_pair_share([blocks[k] for k in ks])

    def update(ks, full):
        for k, g_ in zip(ks, full):
            m_, v_ = moments[k]
            outs[k] = [a.reshape(m_.shape) for a in _adamw(big[k], g_, m_[0], v_[0], "adamw_" + k)]

    ple = ("ple_w_gate", "ple_w_proj")
    mixo = ("mix_w_out", "branch_w_conv", "branch_w_hgrn")
    dx3m = _mm(dgp, w2("ple_w_gate"), name="d_ple_gate_x", tb=True, tn=1024, tk=2048)
    grads["ple_w_gate"] = _mm(x3b, dgp, name="d_ple_gate_w", ta=True, tm=1024, tk=2048, tn=1024).reshape(N_CHIPS, -1, d)
    grads["ple_w_proj"] = _mm(pe, dpp, name="d_ple_proj_w", ta=True, out_blocked=N_CHIPS, tk=2048, tn=512)
    dr3, dy3b, dg2, db2 = _ln_bwd(dr4, dx3m, r3, lng[2], 0.5, "ln2_bwd")
    dh3, got = _mm(dy3b, w2("ffn2_w_out"), name="d_ffn2_out_x", tb=True, out_dtype=BF16, tn=1408, tk=2048,
                   comm=[pair_exchange(*ple)])
    pair_add(ple, got)
    g_, got = _mm(h3, dy3b, name="d_ffn2_out_w", ta=True, tm=1408, tk=2048, tn=1024, comm=[chip_exchange(*ple)])
    grads["ffn2_w_out"] = g_.reshape(2, n_pad, d)
    chip_add(ple, got)
    dz3 = _swiglu_bwd(dh3, z3, "ffn2_act_bwd")
    dx2m, got, full = _mm(dz3, w3("ffn2_w_in"), name="d_ffn2_in_x", tb=True, b_blocked=True, tm=1024, tn=1024, tk=2816,
                          comm=[pair_exchange("ffn2_w_out"), pair_share(*ple)])
    pair_add(["ffn2_w_out"], got)
    update(ple, full)
    grads["ffn2_w_in"], got = _mm(x2b, dz3, name="d_ffn2_in_w", ta=True, out_blocked=N_CHIPS, tk=4096, comm=[chip_exchange("ffn2_w_out")])
    chip_add(["ffn2_w_out"], got)
    dr2, dy2b, dg1, db1 = _ln_bwd(dr3, dx2m, r2, lng[1], 1.0, "ln1_bwd")
    dmer, got = _mm(dy2b, w2("mix_w_out"), name="d_mix_out_x", tb=True, tn=1024, tk=2048, comm=[pair_exchange("ffn2_w_in")])
    pair_add(["ffn2_w_in"], got)
    g_, full = _mm(merged, dy2b, name="d_mix_out_w", ta=True, tm=1024, tk=2048, tn=1024, comm=[pair_share("ffn2_w_out")])
    grads["mix_w_out"] = g_.reshape(N_CHIPS, -1, d)
    update(["ffn2_w_out"], full)
    dma, dmb, dgc, dgh = _merge_bwd(dmer, z, ma, mb, w, "merge_bwd")
    dya = _mm(dma, w3("branch_w_conv"), name="d_branch_conv_x", tb=True, b_blocked=True, tn=1024, tk=512)
    dyb = _mm(dmb, w3("branch_w_hgrn"), name="d_branch_hgrn_x", tb=True, b_blocked=True, tn=1024, tk=512)
    grads["branch_w_conv"] = _mm(ya, dma, name="d_branch_conv_w", ta=True, out_blocked=N_CHIPS, tm=1024, tk=2048, tn=512)
    grads["branch_w_hgrn"] = _mm(yb, dmb, name="d_branch_hgrn_w", ta=True, out_blocked=N_CHIPS, tm=1024, tk=2048, tn=512)
    dbg, dcg, dhc, dcw = _conv_bwd(dya, z, cw, w, "conv_bwd")
    (dq_, df_, di_, dgr_, dhg, dnw), got2, got = _hgrn_bwd(dyb, z, o_h, states, hg, nw_, w, "hgrn_bwd",
                                                            comm=[chip_exchange("ffn2_w_in"), pair_exchange(*mixo)])
    chip_add(["ffn2_w_in"], got2)
    pair_add(mixo, got)
    dz = _concat_cols([dbg, dcg, dhc, dq_, df_, di_, dgr_, dgc, dgh], "dz_concat")
    dx1m, full, got = _mm(dz, w3("mix_w_in"), name="d_mix_in_x", tb=True, b_blocked=True, tm=1024, tn=1024, tk=2816,
                          comm=[pair_share("ffn2_w_in"), chip_exchange(*mixo)])
    update(["ffn2_w_in"], full)
    chip_add(mixo, got)
    grads["mix_w_in"], full = _mm(x1b, dz, name="d_mix_in_w", ta=True, out_blocked=N_CHIPS, tk=4096, comm=[pair_share(*mixo)])
    update(mixo, full)
    dr1, dy1b, dg0, db0 = _ln_bwd(dr2, dx1m, r1, lng[0], 0.5, "ln0_bwd")
    dh1, got = _mm(dy1b, w2("ffn1_w_out"), name="d_ffn1_out_x", tb=True, out_dtype=BF16, tn=1408, tk=2048,
                   comm=[pair_exchange("mix_w_in")])
    pair_add(["mix_w_in"], got)
    mix_sends = [sends["mix_w_in"]]
    g_, got_a = _mm(h1, dy1b, name="d_ffn1_out_w", ta=True, tm=1408, tk=2048, tn=1024, comm=[_rs_chip_exchange(mix_sends, (0, 2))])
    grads["ffn1_w_out"] = g_.reshape(2, n_pad, d)
    dz1 = _swiglu_bwd(dh1, z1, "ffn1_act_bwd")
    g_other, got2, got = _mm(x0b, dz1, name="d_ffn1_in_w_other", ta=True, out_blocked=N_CHIPS, tk=4096, half=(sp, True),
                             comm=[_rs_chip_exchange(mix_sends, (1, 2), got_a), pair_exchange("ffn1_w_out")])
    chip_add(["mix_w_in"], got2)
    pair_add(["ffn1_w_out"], got)
    grads["ffn1_w_in"], full, got2, got = _mm(
        x0b, dz1, name="d_ffn1_in_w_own", ta=True, out_blocked=N_CHIPS, tk=4096, half=(sp, False),
        comm=[pair_share("mix_w_in"), chip_exchange("ffn1_w_out"),
              _rs_pair_exchange([g_other], [meta["ffn1_w_in"]], [rows["ffn1_w_in"]])])
    update(["mix_w_in"], full)
    chip_add(["ffn1_w_out"], got2)
    pair_add(["ffn1_w_in"], got)
    dx0m, got2, full = _mm(dz1, w3("ffn1_w_in"), name="d_ffn1_in_x", tb=True, b_blocked=True, tm=1024, tn=1024, tk=2816,
                           comm=[chip_exchange("ffn1_w_in"), pair_share("ffn1_w_out")])
    chip_add(["ffn1_w_in"], got2)
    update(["ffn1_w_out"], full)
    grad_x = _residual_out(dr1, dx0m, "grad_x").reshape(x.shape)
    update(["ffn1_w_in"], _run_stages([pair_share("ffn1_w_in")], "rs_tail_pair")[0])

    pack = jnp.concatenate([
        dg0, dg1, dg2, dg3, db0, db1, db2, db3,
        jnp.pad(dcw, ((0, 0), (0, d - w))), jnp.pad(dhg, ((0, 0), (0, d - w))),
        jnp.pad(jnp.sum(dnw.reshape(-1, HEAD), axis=0, keepdims=True), ((0, 0), (0, d - HEAD))), sq], axis=0)
    pack = _all_reduce_small(jnp.pad(pack, ((0, 1), (0, 0))), "reduce_small")
    loss = (0.5 / d) * jnp.sum(pack[14])
    g_ln_g = lax.dynamic_slice_in_dim(pack[0:4], chip * dq, dq, axis=1)
    g_ln_b = lax.dynamic_slice_in_dim(pack[4:8], chip * dq, dq, axis=1)
    g_conv = lax.dynamic_slice_in_dim(pack[8:11, :w], chip * wq, wq, axis=1)
    g_hg = pack[11:13, :w]
    g_nw = pack[13:14, :HEAD]

    small_w = dict(ln_g=(ln_g, g_ln_g, m_ln_g, v_ln_g), ln_b=(ln_b, g_ln_b, m_ln_b, v_ln_b),
                   conv_w=(conv_w, g_conv, m_conv_w, v_conv_w), hg_lower_bound=(hg_lower_bound, g_hg, m_hg_lower_bound, v_hg_lower_bound),
                   hg_norm_w=(hg_norm_w, g_nw, m_hg_norm_w, v_hg_norm_w))
    for k, (w_, g_, m_, v_) in small_w.items():
        s2 = (-1, w_.shape[-1])
        outs[k] = [a.reshape(w_.shape) for a in _adamw(w_.reshape(s2), g_.reshape(s2), m_.reshape(s2), v_.reshape(s2), "adamw_" + k)]

    order = ["ln_g", "ln_b", "ffn1_w_in", "ffn1_w_out", "mix_w_in", "conv_w", "hg_lower_bound", "hg_norm_w", "branch_w_conv",
             "branch_w_hgrn", "mix_w_out", "ffn2_w_in", "ffn2_w_out", "ple_w_gate", "ple_w_proj"]
    return (loss, grad_x, *[outs[k][0] for k in order], *[outs[k][1] for k in order], *[outs[k][2] for k in order],
            *[outs[k][3] for k in order])
```

```python
import collections
import functools

import jax
import jax.numpy as jnp
from jax import lax
from jax.experimental import pallas as pl
from jax.experimental.pallas import tpu as pltpu

F32 = jnp.float32
BF16 = jnp.bfloat16
MESH = pl.DeviceIdType.MESH
ANY = pl.BlockSpec(memory_space=pl.ANY)
VMEM_SPEC = pl.BlockSpec(memory_space=pltpu.VMEM)
SDS = jax.ShapeDtypeStruct

DEPTH = 1
ALPHA = (2.0 * DEPTH) ** 0.25
LN_EPS = 1e-5
RMS_EPS = 1e-6
CHUNK = 32
HEAD = 128
ADAM_LR, ADAM_B1, ADAM_B2, ADAM_EPS, ADAM_WD, ADAM_STEP = 0.001, 0.9, 0.999, 1e-08, 0.01, 10

LANES = 128
N_CHIPS = 4
N_DEV = 8
VMEM_LIMIT = 52 * 1024 * 1024


def _cparams(*sem):
    if sem:
        return pltpu.CompilerParams(dimension_semantics=sem, vmem_limit_bytes=VMEM_LIMIT)
    return pltpu.CompilerParams(vmem_limit_bytes=VMEM_LIMIT)


def _tile(n, target, mult):
    best = None
    for t in range(mult, min(n, target) + 1, mult):
        if n % t == 0:
            best = t
    return best if best is not None else n


def _sigmoid(x):
    return 1.0 / (1.0 + jnp.exp(-x))


_Stage = collections.namedtuple("_Stage", "ins out_shapes aliases sems start finish")


def _hosted_call(compute, stages, *, name, grid, in_specs, out_specs, out_shape, scratch_shapes, operands, parallel,
                 prefetch=None):
    n_cmp, n_out, n_scr = len(in_specs), len(out_specs), len(scratch_shapes)
    n_in = n_cmp
    n_pre = int(prefetch is not None)
    c_in = [len(s.ins) for s in stages]
    c_out = [len(s.out_shapes) for s in stages]
    c_sem = [len(s.sems) for s in stages]
    aliases = {}
    for si, s in enumerate(stages):
        for a_in, a_out in s.aliases.items():
            aliases[n_pre + n_in + sum(c_in[:si]) + a_in] = n_out + sum(c_out[:si]) + a_out

    def body(*refs):
        refs = refs[n_pre:]
        ins = refs[:n_cmp]
        cins = refs[n_in:n_in + sum(c_in)]
        outs = refs[n_in + sum(c_in):n_in + sum(c_in) + n_out]
        couts = refs[n_in + sum(c_in) + n_out:n_in + sum(c_in) + n_out + sum(c_out)]
        scr = refs[n_in + sum(c_in) + n_out + sum(c_out):][:n_scr]
        sems = refs[n_in + sum(c_in) + n_out + sum(c_out) + n_scr:]

        def stage_refs(si):
            return (cins[sum(c_in[:si]):sum(c_in[:si + 1])], couts[sum(c_out[:si]):sum(c_out[:si + 1])],
                    sems[sum(c_sem[:si]):sum(c_sem[:si + 1])])

        if stages:
            first = functools.reduce(jnp.logical_and, [pl.program_id(ax) == 0 for ax in range(len(grid))])
            last = functools.reduce(jnp.logical_and, [pl.program_id(ax) == grid[ax] - 1 for ax in range(len(grid))])

            @pl.when(first)
            def _():
                for si, s in enumerate(stages):
                    s.start(*stage_refs(si))

        compute(*ins, *outs, *scr)
        if stages:
            @pl.when(last)
            def _():
                for si, s in enumerate(stages):
                    s.finish(*stage_refs(si))

    sem = ("arbitrary",) * len(grid) if stages else ("parallel",) * parallel + ("arbitrary",) * (len(grid) - parallel)
    all_in = list(in_specs) + [ANY] * (n_in - n_cmp + sum(c_in))
    all_out = list(out_specs) + [ANY] * sum(c_out)
    all_scr = list(scratch_shapes) + [q for s in stages for q in s.sems]
    all_shape = list(out_shape) + [o for s in stages for o in s.out_shapes]
    args = list(operands) + [a for s in stages for a in s.ins]
    if prefetch is None:
        res = pl.pallas_call(body, name=name, grid=grid, in_specs=all_in, out_specs=all_out, out_shape=all_shape,
                             input_output_aliases=aliases, scratch_shapes=all_scr, compiler_params=_cparams(*sem))(*args)
    else:
        grid_spec = pltpu.PrefetchScalarGridSpec(num_scalar_prefetch=1, grid=grid, in_specs=all_in, out_specs=all_out,
                                                 scratch_shapes=all_scr)
        res = pl.pallas_call(body, name=name, grid_spec=grid_spec, out_shape=all_shape, input_output_aliases=aliases,
                             compiler_params=_cparams(*sem))(prefetch, *args)
    main = res[0] if n_out == 1 else list(res[:n_out])
    if not stages:
        return main
    rest = res[n_out:]
    return (main, *[list(rest[sum(c_out[:si]):sum(c_out[:si + 1])]) for si in range(len(stages))])


def _run_stages(stages, name):
    def body(*refs):
        n_i = sum(len(s.ins) for s in stages)
        n_o = sum(len(s.out_shapes) for s in stages)
        cins, couts, sems = refs[:n_i], refs[n_i:n_i + n_o], refs[n_i + n_o:]
        pos = [0, 0, 0]
        parts = []
        for s in stages:
            parts.append((cins[pos[0]:pos[0] + len(s.ins)], couts[pos[1]:pos[1] + len(s.out_shapes)], sems[pos[2]:pos[2] + len(s.sems)]))
            pos = [pos[0] + len(s.ins), pos[1] + len(s.out_shapes), pos[2] + len(s.sems)]
        for s, p_ in zip(stages, parts):
            s.start(*p_)
        for s, p_ in zip(stages, parts):
            s.finish(*p_)

    aliases, ni, no = {}, 0, 0
    for s in stages:
        for a_in, a_out in s.aliases.items():
            aliases[ni + a_in] = no + a_out
        ni, no = ni + len(s.ins), no + len(s.out_shapes)
    res = pl.pallas_call(
        body, name=name, in_specs=[ANY] * ni, out_specs=[ANY] * no, out_shape=[o for s in stages for o in s.out_shapes],
        input_output_aliases=aliases, scratch_shapes=[q for s in stages for q in s.sems],
    )(*[a for s in stages for a in s.ins])
    out, pos = [], 0
    for s in stages:
        out.append(list(res[pos:pos + len(s.out_shapes)]))
        pos += len(s.out_shapes)
    return out


def _mm(a, b, *, name, ta=False, tb=False, b_blocked=False, out_blocked=0, out_dtype=F32,
        tm=512, tn=1408, tk=2048, comm=(), half=None):
    if ta:
        kd, m = a.shape
    else:
        m, kd = a.shape
    if b_blocked and not tb:
        g, kb, nb = b.shape
        assert kb == kd
        n = g * nb
        tn = _tile(nb, tn, LANES)
        tk = _tile(kd, tk, LANES)
        per_n = nb // tn
        b_spec = pl.BlockSpec((None, tk, tn), lambda i, j, k, *s: (j // per_n, k, j % per_n))
    elif b_blocked and tb:
        g, n, kb = b.shape
        assert g * kb == kd
        tn = _tile(n, tn, LANES)
        tk = _tile(kb, tk, LANES)
        per_k = kb // tk
        b_spec = pl.BlockSpec((None, tn, tk), lambda i, j, k, *s: (k // per_k, j, k % per_k))
    elif tb:
        n, kb = b.shape
        assert kb == kd
        tn = _tile(n, tn, LANES)
        tk = _tile(kd, tk, LANES)
        b_spec = pl.BlockSpec((tn, tk), lambda i, j, k, *s: (j, k))
    else:
        kb, n = b.shape
        assert kb == kd
        tn = _tile(n // out_blocked if out_blocked else n, tn, LANES)
        per_o = (n // out_blocked) // tn if out_blocked else None
        tk = _tile(kd, tk, LANES)
        b_spec = pl.BlockSpec((tk, tn), lambda i, j, k, *s: (k, j))
    m_run = m // 2 if half else m
    tm = _tile(m_run, tm, LANES if ta else 8)

    def row(i, s):
        if not half:
            return i
        h = 1 - s[0][0] if half[1] else s[0][0]
        return h * (m_run // tm) + i

    if ta:
        a_spec = pl.BlockSpec((tk, tm), lambda i, j, k, *s: (k, row(i, s)))
    else:
        a_spec = pl.BlockSpec((tm, tk), lambda i, j, k, *s: (row(i, s), k))
    if out_blocked:
        assert not b_blocked and not tb
        o_spec = pl.BlockSpec((None, tm, tn), lambda i, j, k, *s: (j // per_o, row(i, s), j % per_o))
        o_shape = SDS((out_blocked, m, n // out_blocked), out_dtype)
    else:
        o_spec = pl.BlockSpec((tm, tn), lambda i, j, k, *s: (row(i, s), j))
        o_shape = SDS((m, n), out_dtype)
    nk = kd // tk
    dn = (((0 if ta else 1,), (1 if tb else 0,)), ((), ()))
    grid = (m_run // tm, n // tn, nk)

    def compute(a_ref, b_ref, o_ref, acc_ref):
        part = lax.dot_general(a_ref[...].astype(BF16), b_ref[...].astype(BF16), dn, preferred_element_type=F32)
        if nk == 1:
            o_ref[...] = part.astype(o_ref.dtype)
        else:
            k = pl.program_id(2)

            @pl.when(k == 0)
            def _():
                acc_ref[...] = part

            @pl.when(k > 0)
            def _():
                acc_ref[...] += part

            @pl.when(k == nk - 1)
            def _():
                o_ref[...] = acc_ref[...].astype(o_ref.dtype)

    return _hosted_call(compute, comm, name=name, grid=grid, in_specs=[a_spec, b_spec], out_specs=[o_spec], out_shape=[o_shape],
                        scratch_shapes=[pltpu.VMEM((tm, tn), F32)], operands=(a, b), parallel=2,
                        prefetch=half[0] if half else None)


def _swiglu_fwd(z, name):
    t, n = z.shape
    n2 = n // 2
    tr = _tile(t, 128, 16)

    def body(a_ref, u_ref, o_ref):
        a = a_ref[...].astype(F32)
        o_ref[...] = (a * _sigmoid(a) * u_ref[...].astype(F32)).astype(o_ref.dtype)

    return pl.pallas_call(
        body, name=name, grid=(t // tr,),
        in_specs=[pl.BlockSpec((tr, n2), lambda i: (i, 0)), pl.BlockSpec((tr, n2), lambda i: (i, 1))],
        out_specs=pl.BlockSpec((tr, n2), lambda i: (i, 0)), out_shape=SDS((t, n2), BF16),
        compiler_params=_cparams("parallel"),
    )(z, z)


def _swiglu_bwd(dh, z, name):
    t, n = z.shape
    n2 = n // 2
    tr = _tile(t, 128, 16)

    def body(dh_ref, a_ref, u_ref, o_ref):
        a = a_ref[...].astype(F32)
        dh_ = dh_ref[...].astype(F32)
        s = _sigmoid(a)
        o_ref[:, 0:n2] = (dh_ * u_ref[...].astype(F32) * (s * (1.0 + a * (1.0 - s)))).astype(o_ref.dtype)
        o_ref[:, n2:n] = (dh_ * a * s).astype(o_ref.dtype)

    return pl.pallas_call(
        body, name=name, grid=(t // tr,),
        in_specs=[pl.BlockSpec((tr, n2), lambda i: (i, 0)), pl.BlockSpec((tr, n2), lambda i: (i, 0)),
                  pl.BlockSpec((tr, n2), lambda i: (i, 1))],
        out_specs=pl.BlockSpec((tr, n), lambda i: (i, 0)), out_shape=SDS((t, n), BF16),
        compiler_params=_cparams("parallel"),
    )(dh, z, z)


def _ln_stats(r):
    mu = jnp.mean(r, axis=-1, keepdims=True)
    xc = r - mu
    var = jnp.mean(xc * xc, axis=-1, keepdims=True)
    return xc * lax.rsqrt(var + LN_EPS)


def _ln_fwd(xp, y, g, b, scale, name):
    t, d = xp.shape
    tr = _tile(t, 256, 16)

    def body(xp_ref, y_ref, g_ref, b_ref, r_ref, x_ref, xb_ref):
        r = ALPHA * xp_ref[...] + scale * y_ref[...]
        x = _ln_stats(r) * g_ref[...] + b_ref[...]
        r_ref[...] = r
        x_ref[...] = x
        xb_ref[...] = x.astype(BF16)

    row = pl.BlockSpec((tr, d), lambda i: (i, 0))
    vec = pl.BlockSpec((1, d), lambda i: (0, 0))
    return pl.pallas_call(
        body, name=name, grid=(t // tr,), in_specs=[row, row, vec, vec], out_specs=[row, row, row],
        out_shape=[SDS((t, d), F32), SDS((t, d), F32), SDS((t, d), BF16)], compiler_params=_cparams("parallel"),
    )(xp, y, g, b)


def _ln_bwd(dra, dxm, r, g, scale, name):
    t, d = r.shape
    tr = _tile(t, 256, 16)

    def body(dra_ref, dxm_ref, r_ref, g_ref, dr_ref, dyb_ref, dg_ref, db_ref):
        i = pl.program_id(0)
        dx = ALPHA * dra_ref[...] + dxm_ref[...]
        rr = r_ref[...]
        mu = jnp.mean(rr, axis=-1, keepdims=True)
        xc = rr - mu
        rstd = lax.rsqrt(jnp.mean(xc * xc, axis=-1, keepdims=True) + LN_EPS)
        xh = xc * rstd
        dxh = dx * g_ref[...]
        dr = rstd * (dxh - jnp.mean(dxh, axis=-1, keepdims=True) - xh * jnp.mean(dxh * xh, axis=-1, keepdims=True))
        dr_ref[...] = dr
        dyb_ref[...] = (scale * dr).astype(BF16)
        dg = jnp.sum(dx * xh, axis=0, keepdims=True)
        db = jnp.sum(dx, axis=0, keepdims=True)

        @pl.when(i == 0)
        def _():
            dg_ref[...] = dg
            db_ref[...] = db

        @pl.when(i > 0)
        def _():
            dg_ref[...] += dg
            db_ref[...] += db

    row = pl.BlockSpec((tr, d), lambda i: (i, 0))
    vec = pl.BlockSpec((1, d), lambda i: (0, 0))
    return pl.pallas_call(
        body, name=name, grid=(t // tr,), in_specs=[row, row, row, vec], out_specs=[row, row, vec, vec],
        out_shape=[SDS((t, d), F32), SDS((t, d), BF16), SDS((1, d), F32), SDS((1, d), F32)],
        compiler_params=_cparams("arbitrary"),
    )(dra, dxm, r, g)


def _tail(x3, gp, pp, g, b, target, name):
    t, d = x3.shape
    tr = _tile(t, 256, 16)

    def body(x3_ref, gp_ref, pp_ref, g_ref, b_ref, tg_ref, dr_ref, dgp_ref, dpp_ref, dg_ref, db_ref, sq_ref):
        i = pl.program_id(0)
        gate = _sigmoid(gp_ref[...])
        pp_ = pp_ref[...]
        r = ALPHA * x3_ref[...] + gate * pp_
        mu = jnp.mean(r, axis=-1, keepdims=True)
        xc = r - mu
        rstd = lax.rsqrt(jnp.mean(xc * xc, axis=-1, keepdims=True) + LN_EPS)
        xh = xc * rstd
        err = xh * g_ref[...] + b_ref[...] - tg_ref[...]
        dx = err * (1.0 / d)
        dxh = dx * g_ref[...]
        dr = rstd * (dxh - jnp.mean(dxh, axis=-1, keepdims=True) - xh * jnp.mean(dxh * xh, axis=-1, keepdims=True))
        dr_ref[...] = dr
        dgp_ref[...] = (dr * pp_ * gate * (1.0 - gate)).astype(BF16)
        dpp_ref[...] = (dr * gate).astype(BF16)
        dg = jnp.sum(dx * xh, axis=0, keepdims=True)
        db = jnp.sum(dx, axis=0, keepdims=True)
        sq = jnp.sum(err * err, axis=0, keepdims=True)

        @pl.when(i == 0)
        def _():
            dg_ref[...] = dg
            db_ref[...] = db
            sq_ref[...] = sq

        @pl.when(i > 0)
        def _():
            dg_ref[...] += dg
            db_ref[...] += db
            sq_ref[...] += sq

    row = pl.BlockSpec((tr, d), lambda i: (i, 0))
    vec = pl.BlockSpec((1, d), lambda i: (0, 0))
    return pl.pallas_call(
        body, name=name, grid=(t // tr,), in_specs=[row, row, row, vec, vec, row],
        out_specs=[row, row, row, vec, vec, vec],
        out_shape=[SDS((t, d), F32), SDS((t, d), BF16), SDS((t, d), BF16), SDS((1, d), F32), SDS((1, d), F32),
                   SDS((1, d), F32)],
        compiler_params=_cparams("arbitrary"),
    )(x3, gp, pp, g, b, target)


def _to_bf16(x, name):
    t, d = x.shape
    tr = _tile(t, 512, 16)
    row = pl.BlockSpec((tr, d), lambda i: (i, 0))

    def body(x_ref, o_ref):
        o_ref[...] = x_ref[...].astype(BF16)

    return pl.pallas_call(body, name=name, grid=(t // tr,), in_specs=[row], out_specs=row, out_shape=SDS((t, d), BF16),
                          compiler_params=_cparams("parallel"))(x)


def _concat_cols(parts, name):
    t = parts[0].shape[0]
    widths = [p_.shape[1] for p_ in parts]
    tr = _tile(t, 256, 16)

    def body(*refs):
        o_ref = refs[-1]
        at = 0
        for ref, wd in zip(refs[:-1], widths):
            o_ref[:, at:at + wd] = ref[...]
            at += wd

    return pl.pallas_call(
        body, name=name, grid=(t // tr,), in_specs=[pl.BlockSpec((tr, wd), lambda i: (i, 0)) for wd in widths],
        out_specs=pl.BlockSpec((tr, sum(widths)), lambda i: (i, 0)), out_shape=SDS((t, sum(widths)), parts[0].dtype),
        compiler_params=_cparams("parallel"),
    )(*parts)


def _residual_out(dra, dxm, name):
    t, d = dra.shape
    tr = _tile(t, 256, 8)

    def body(a_ref, b_ref, o_ref):
        o_ref[...] = ALPHA * a_ref[...] + b_ref[...]

    row = pl.BlockSpec((tr, d), lambda i: (i, 0))
    return pl.pallas_call(body, name=name, grid=(t // tr,), in_specs=[row, row], out_specs=row,
                          out_shape=SDS((t, d), F32), compiler_params=_cparams("parallel"))(dra, dxm)


def _merge_fwd(z, ma, mb, w, name):
    t = z.shape[0]
    tr = _tile(t, 256, 16)

    def body(gc_ref, gh_ref, ma_ref, mb_ref, o_ref):
        o_ref[...] = (_sigmoid(gc_ref[...]) * ma_ref[...] + _sigmoid(gh_ref[...]) * mb_ref[...]).astype(BF16)

    half = pl.BlockSpec((tr, w), lambda i, j: (i, j))
    return pl.pallas_call(
        body, name=name, grid=(t // tr, 2),
        in_specs=[pl.BlockSpec((tr, w), lambda i, j: (i, 7 + j)), pl.BlockSpec((tr, w), lambda i, j: (i, 9 + j)), half, half],
        out_specs=half, out_shape=SDS((t, 2 * w), BF16), compiler_params=_cparams("parallel", "parallel"),
    )(z, z, ma, mb)


def _merge_bwd(dmer, z, ma, mb, w, name):
    t = z.shape[0]
    tr = _tile(t, 256, 16)

    def body(d_ref, gc_ref, gh_ref, ma_ref, mb_ref, dma_ref, dmb_ref, dgc_ref, dgh_ref):
        dm = d_ref[...]
        sc = _sigmoid(gc_ref[...])
        sh = _sigmoid(gh_ref[...])
        dma_ref[...] = (dm * sc).astype(BF16)
        dmb_ref[...] = (dm * sh).astype(BF16)
        dgc_ref[...] = (dm * ma_ref[...] * sc * (1.0 - sc)).astype(BF16)
        dgh_ref[...] = (dm * mb_ref[...] * sh * (1.0 - sh)).astype(BF16)

    half = pl.BlockSpec((tr, w), lambda i, j: (i, j))
    return pl.pallas_call(
        body, name=name, grid=(t // tr, 2),
        in_specs=[half, pl.BlockSpec((tr, w), lambda i, j: (i, 7 + j)), pl.BlockSpec((tr, w), lambda i, j: (i, 9 + j)), half, half],
        out_specs=[half] * 4, out_shape=[SDS((t, 2 * w), BF16)] * 4, compiler_params=_cparams("parallel", "parallel"),
    )(dmer, z, z, ma, mb)


def _shift_down(x, s, row):
    return jnp.where(row >= s, pltpu.roll(x, s, axis=0), 0.0)


def _shift_up(x, s, row, t):
    return jnp.where(row < t - s, pltpu.roll(x, t - s, axis=0), 0.0)


def _conv_fwd(z, cw, w, name):
    t = z.shape[0]
    tc = LANES
    nb = w // tc

    def body(b_ref, c_ref, h_ref, w_ref, o_ref):
        u = c_ref[...] * h_ref[...]
        row = lax.broadcasted_iota(jnp.int32, u.shape, 0)
        cw_ = w_ref[...]
        conv = cw_[2:3, :] * u + cw_[1:2, :] * _shift_down(u, 1, row) + cw_[0:1, :] * _shift_down(u, 2, row)
        o_ref[...] = (b_ref[...] * conv).astype(BF16)

    col = lambda off: pl.BlockSpec((t, tc), lambda j: (0, off * nb + j))
    return pl.pallas_call(
        body, name=name, grid=(nb,), in_specs=[col(0), col(1), col(2), pl.BlockSpec((3, tc), lambda j: (0, j))],
        out_specs=pl.BlockSpec((t, tc), lambda j: (0, j)), out_shape=SDS((t, w), BF16), compiler_params=_cparams("parallel"),
    )(z, z, z, cw)


def _conv_bwd(dy, z, cw, w, name):
    t = z.shape[0]
    tc = LANES
    nb = w // tc

    def body(dy_ref, b_ref, c_ref, h_ref, w_ref, db_ref, dc_ref, dh_ref, dw_ref):
        c_, h_ = c_ref[...], h_ref[...]
        u = c_ * h_
        row = lax.broadcasted_iota(jnp.int32, u.shape, 0)
        cw_ = w_ref[...]
        u1 = _shift_down(u, 1, row)
        u2 = _shift_down(u, 2, row)
        dy_ = dy_ref[...]
        db_ref[...] = (dy_ * (cw_[2:3, :] * u + cw_[1:2, :] * u1 + cw_[0:1, :] * u2)).astype(BF16)
        dconv = dy_ * b_ref[...]
        du = cw_[2:3, :] * dconv + cw_[1:2, :] * _shift_up(dconv, 1, row, t) + cw_[0:1, :] * _shift_up(dconv, 2, row, t)
        dc_ref[...] = (du * h_).astype(BF16)
        dh_ref[...] = (du * c_).astype(BF16)
        dw_ref[0:1, :] = jnp.sum(dconv * u2, axis=0, keepdims=True)
        dw_ref[1:2, :] = jnp.sum(dconv * u1, axis=0, keepdims=True)
        dw_ref[2:3, :] = jnp.sum(dconv * u, axis=0, keepdims=True)

    col = lambda off: pl.BlockSpec((t, tc), lambda j: (0, off * nb + j))
    own = pl.BlockSpec((t, tc), lambda j: (0, j))
    wsp = pl.BlockSpec((3, tc), lambda j: (0, j))
    return pl.pallas_call(
        body, name=name, grid=(nb,), in_specs=[own, col(0), col(1), col(2), wsp], out_specs=[own, own, own, wsp],
        out_shape=[SDS((t, w), BF16)] * 3 + [SDS((3, w), F32)], compiler_params=_cparams("parallel"),
    )(dy, z, z, z, cw)


def _lower_bound(hg):
    mx = jnp.max(hg, axis=0, keepdims=True)
    e = jnp.exp(hg - mx)
    inv = 1.0 / jnp.sum(e, axis=0, keepdims=True)
    return e[0:1, :] * inv, e[1:2, :] * inv


def _chunk_cumsum(x, row):
    s = 1
    while s < CHUNK:
        x = x + jnp.where(row % CHUNK >= s, pltpu.roll(x, s, axis=0), 0.0)
        s *= 2
    return x


def _dot_nt(a, b):
    return lax.dot_general(a.astype(BF16), b.astype(BF16), (((1,), (1,)), ((), ())), preferred_element_type=F32)


def _dot_tn(a, b):
    return lax.dot_general(a.astype(BF16), b.astype(BF16), (((0,), (0,)), ((), ())), preferred_element_type=F32)


def _dot_nn(a, b):
    return jnp.dot(a.astype(BF16), b.astype(BF16), preferred_element_type=F32)


def _tril(x):
    r = lax.broadcasted_iota(jnp.int32, x.shape, 0)
    c = lax.broadcasted_iota(jnp.int32, x.shape, 1)
    return jnp.where(r >= c, x, 0.0)


HGRN_GROUP = 4
HGRN_ROWS = 512
HGRN_UNROLL = 2


def _unrolled_loop(n, step, init):
    assert n % HGRN_UNROLL == 0

    def trip(i, carry):
        for u in range(HGRN_UNROLL):
            carry = step(i * HGRN_UNROLL + u, carry)
        return carry

    return lax.fori_loop(0, n // HGRN_UNROLL, trip, init)


def _hgrn_chunk_inputs(q_ref, f_ref, cum_ref, lb, rows, ln):
    qr = q_ref[rows, ln]
    q = qr * _sigmoid(qr)
    f = lb + (1.0 - lb) * _sigmoid(f_ref[rows, ln])
    return q, 1.0 - f, cum_ref[rows, ln]


def _hgrn_fwd(z, hg, nw, w, name, comm=()):
    t = z.shape[0]
    nh = w // HEAD
    gh = _tile(nh, HGRN_GROUP, 1)
    gw = gh * HEAD
    ngrp = nh // gh
    tb = _tile(t, HGRN_ROWS, CHUNK)
    ncb = tb // CHUNK

    def body(q_ref, f_ref, i_ref, g_ref, hg_ref, nw_ref, y_ref, o_ref, st_ref, cum_ref, *s_refs):
        lb_all, _ = _lower_bound(hg_ref[...])
        row = lax.broadcasted_iota(jnp.int32, (tb, gw), 0)
        cum_ref[...] = _chunk_cumsum(jnp.log(lb_all + (1.0 - lb_all) * _sigmoid(f_ref[...])), row)

        @pl.when(pl.program_id(1) == 0)
        def _():
            for s_ref in s_refs:
                s_ref[...] = jnp.zeros_like(s_ref)

        def step(c, carry):
            rows = pl.ds(pl.multiple_of(c * CHUNK, CHUNK), CHUNK)
            for g in range(gh):
                ln = slice(g * HEAD, (g + 1) * HEAD)
                lb = lb_all[:, ln]
                q, k, cum = _hgrn_chunk_inputs(q_ref, f_ref, cum_ref, lb, rows, ln)
                v = i_ref[rows, ln]
                last = cum[CHUNK - 1:CHUNK, :]
                qe = q * jnp.exp(cum)
                st = s_refs[g][...]
                st_ref[g, c] = st.astype(BF16)
                o_ref[rows, ln] = _dot_nt(qe, st) + _dot_nn(_tril(_dot_nt(qe, k * jnp.exp(-cum))), v)
                s_refs[g][...] = st * jnp.exp(last) + _dot_tn(v, k * jnp.exp(last - cum))
            return carry

        _unrolled_loop(ncb, step, 0)
        for g in range(gh):
            ln = slice(g * HEAD, (g + 1) * HEAD)
            o = o_ref[:, ln]
            n = o * lax.rsqrt(jnp.mean(o * o, axis=-1, keepdims=True) + RMS_EPS)
            gr = g_ref[:, ln]
            y_ref[:, ln] = (n * nw_ref[...] * gr * _sigmoid(gr)).astype(BF16)

    col = lambda off: pl.BlockSpec((tb, gw), lambda h, j: (j, off * ngrp + h))
    own = pl.BlockSpec((tb, gw), lambda h, j: (j, h))
    return _hosted_call(
        body, comm, name=name, grid=(ngrp, t // tb),
        in_specs=[col(3), col(4), col(5), col(6), pl.BlockSpec((2, gw), lambda h, j: (0, h)),
                  pl.BlockSpec((1, HEAD), lambda h, j: (0, 0))],
        out_specs=[own, own, pl.BlockSpec((gh, ncb, HEAD, HEAD), lambda h, j: (h, j, 0, 0))],
        out_shape=[SDS((t, w), BF16), SDS((t, w), F32), SDS((nh, t // CHUNK, HEAD, HEAD), BF16)],
        scratch_shapes=[pltpu.VMEM((tb, gw), F32)] + [pltpu.VMEM((HEAD, HEAD), F32)] * gh,
        operands=(z, z, z, z, hg, nw), parallel=1)


def _hgrn_bwd(dy, z, o, states, hg, nw, w, name, comm=()):
    t = z.shape[0]
    nh = w // HEAD
    gh = _tile(nh, HGRN_GROUP, 1)
    gw = gh * HEAD
    ngrp = nh // gh
    tb = _tile(t, HGRN_ROWS, CHUNK)
    ncb = tb // CHUNK
    nt = t // tb

    def body(dy_ref, q_ref, f_ref, i_ref, g_ref, o_ref, st_ref, hg_ref, nw_ref,
             dq_ref, df_ref, di_ref, dg_ref, dhg_ref, dnw_ref, cum_ref, do_ref, *ds_refs):
        lb_all, s1_all = _lower_bound(hg_ref[...])
        row = lax.broadcasted_iota(jnp.int32, (tb, gw), 0)
        crow = lax.broadcasted_iota(jnp.int32, (CHUNK, HEAD), 0)
        cum_ref[...] = _chunk_cumsum(jnp.log(lb_all + (1.0 - lb_all) * _sigmoid(f_ref[...])), row)

        @pl.when(pl.program_id(1) == 0)
        def _():
            for ds_ref in ds_refs:
                ds_ref[...] = jnp.zeros_like(ds_ref)
            dhg_ref[...] = jnp.zeros_like(dhg_ref)
            dnw_ref[...] = jnp.zeros_like(dnw_ref)

        for g in range(gh):
            ln = slice(g * HEAD, (g + 1) * HEAD)
            o_ = o_ref[:, ln]
            rstd = lax.rsqrt(jnp.mean(o_ * o_, axis=-1, keepdims=True) + RMS_EPS)
            n = o_ * rstd
            gr = g_ref[:, ln]
            sg = _sigmoid(gr)
            dy_ = dy_ref[:, ln]
            dg_ref[:, ln] = (dy_ * n * nw_ref[...] * (sg * (1.0 + gr * (1.0 - sg)))).astype(BF16)
            dsil = dy_ * gr * sg
            dnw_ref[:, ln] += jnp.sum(dsil * n, axis=0, keepdims=True)
            dn = dsil * nw_ref[...]
            do_ref[:, ln] = rstd * (dn - n * jnp.mean(dn * n, axis=-1, keepdims=True))

        def step(cc, dlbs):
            c = ncb - 1 - cc
            rows = pl.ds(pl.multiple_of(c * CHUNK, CHUNK), CHUNK)
            new = []
            for g in range(gh):
                ln = slice(g * HEAD, (g + 1) * HEAD)
                lb = lb_all[:, ln]
                qr = q_ref[rows, ln]
                sq = _sigmoid(qr)
                q = qr * sq
                sf = _sigmoid(f_ref[rows, ln])
                f = lb + (1.0 - lb) * sf
                k = 1.0 - f
                cum = cum_ref[rows, ln]
                v = i_ref[rows, ln]
                do = do_ref[rows, ln]
                last = cum[CHUNK - 1:CHUNK, :]
                eg = jnp.exp(cum)
                eng = jnp.exp(-cum)
                elc = jnp.exp(last - cum)
                qe, ke, kl = q * eg, k * eng, k * elc
                ds = ds_refs[g][...]
                a = _tril(_dot_nt(qe, ke))
                da = _tril(_dot_nt(do, v))
                di_ref[rows, ln] = (_dot_tn(a, do) + _dot_nt(kl, ds)).astype(BF16)
                st = st_ref[g, c]
                dkl = _dot_nn(v, ds)
                dq = (_dot_nn(do, st) + _dot_nn(da, ke)) * eg
                dk = _dot_tn(da, qe) * eng + dkl * elc
                el = jnp.exp(last)
                ds_refs[g][...] = ds * el + _dot_tn(do, qe)
                dlast = jnp.sum(kl * dkl, axis=0, keepdims=True) + el * jnp.sum(ds * st.astype(F32), axis=0, keepdims=True)
                x = q * dq - k * dk + jnp.where(crow == CHUNK - 1, dlast, 0.0)
                s = 1
                while s < CHUNK:
                    x = x + _shift_up(x, s, crow, CHUNK)
                    s *= 2
                df = x / f - dk
                dq_ref[rows, ln] = (dq * (sq * (1.0 + qr * (1.0 - sq)))).astype(BF16)
                df_ref[rows, ln] = (df * (1.0 - lb) * sf * (1.0 - sf)).astype(BF16)
                new.append(dlbs[g] + jnp.sum(df * (1.0 - sf), axis=0, keepdims=True))
            return tuple(new)

        dlbs = _unrolled_loop(ncb, step, tuple(jnp.zeros((1, HEAD), F32) for _ in range(gh)))
        for g in range(gh):
            ln = slice(g * HEAD, (g + 1) * HEAD)
            dlb = dlbs[g] * lb_all[:, ln] * s1_all[:, ln]
            dhg_ref[0:1, ln] += dlb
            dhg_ref[1:2, ln] -= dlb

    col = lambda off: pl.BlockSpec((tb, gw), lambda h, j: (nt - 1 - j, off * ngrp + h))
    own = pl.BlockSpec((tb, gw), lambda h, j: (nt - 1 - j, h))
    hsp = pl.BlockSpec((2, gw), lambda h, j: (0, h))
    return _hosted_call(
        body, comm, name=name, grid=(ngrp, nt),
        in_specs=[own, col(3), col(4), col(5), col(6), own,
                  pl.BlockSpec((gh, ncb, HEAD, HEAD), lambda h, j: (h, nt - 1 - j, 0, 0)),
                  hsp, pl.BlockSpec((1, HEAD), lambda h, j: (0, 0))],
        out_specs=[own, own, own, own, hsp, pl.BlockSpec((1, gw), lambda h, j: (0, h))],
        out_shape=[SDS((t, w), BF16)] * 4 + [SDS((2, w), F32), SDS((1, w), F32)],
        scratch_shapes=[pltpu.VMEM((tb, gw), F32)] * 2 + [pltpu.VMEM((HEAD, HEAD), F32)] * gh,
        operands=(dy, z, z, z, z, o, states, hg, nw), parallel=1)


def _cast_pad(wt, n_pad, meta, sp, name):
    _, r, n = wt.shape
    g, p, per = meta
    tr = _tile(r, max(16, (3 << 19) // n_pad // 16 * 16), 16)

    def body(sp_ref, w_ref, o_ref):
        if n_pad != n:
            o_ref[...] = jnp.zeros(o_ref.shape, o_ref.dtype)
        o_ref[:, 0:n] = w_ref[...].astype(BF16)

    grid_spec = pltpu.PrefetchScalarGridSpec(
        num_scalar_prefetch=1, grid=(r // tr,), in_specs=[pl.BlockSpec((None, tr, n), lambda i, sp: (0, i, 0))],
        out_specs=pl.BlockSpec((None, tr, n_pad), lambda i, sp: (sp[1] // per, ((sp[1] % per) * r) // tr + i, 0)))
    return pl.pallas_call(body, name=name, grid_spec=grid_spec, out_shape=SDS((g, p, n_pad), BF16),
                          compiler_params=_cparams("parallel"))(sp, wt)


def _adamw(wt, g, m, v, name):
    lead = (None,) * (wt.ndim - 2)
    zero = (0,) * (wt.ndim - 2)
    r, n = wt.shape[-2:]
    ng = g.shape[1]
    nct = 2 if ng == n and n % (2 * LANES) == 0 else 1
    tc, tg = n // nct, ng // nct
    tr = _tile(r, max(8, (3 << 17) // tg // 8 * 8), 8)
    c1 = 1.0 / (1.0 - ADAM_B1 ** ADAM_STEP)
    c2 = 1.0 / (1.0 - ADAM_B2 ** ADAM_STEP)

    def body(w_ref, g_ref, m_ref, v_ref, go_ref, d_ref, mo_ref, vo_ref):
        g_ = g_ref[:, 0:tc]
        m2 = ADAM_B1 * m_ref[...] + (1.0 - ADAM_B1) * g_
        v2 = ADAM_B2 * v_ref[...] + (1.0 - ADAM_B2) * (g_ * g_)
        go_ref[...] = g_
        mo_ref[...] = m2
        vo_ref[...] = v2
        d_ref[...] = -ADAM_LR * ((m2 * c1) / (jnp.sqrt(v2 * c2) + ADAM_EPS) + ADAM_WD * w_ref[...])

    blk = pl.BlockSpec(lead + (tr, tc), lambda i, j: zero + (i, j))
    return pl.pallas_call(
        body, name=name, grid=(r // tr, nct), in_specs=[blk, pl.BlockSpec((tr, tg), lambda i, j: (i, j)), blk, blk],
        out_specs=[blk] * 4, out_shape=[SDS(wt.shape, F32)] * 4, compiler_params=_cparams("parallel", "parallel"),
    )(wt, g, m, v)


def _place():
    x, y, c = lax.axis_index("x"), lax.axis_index("y"), lax.axis_index("c")
    return x, y, c, 2 * x + y


def _chip_dev(k, c):
    return (k // 2, k % 2, c)


def _half(ref, j, h, rows, per):
    return ref.at[j // per, pl.ds((j % per) * rows + h * (rows // 2), rows // 2)]


def _gather_stage(bufs, metas, rows_of, parts, zero_pad):
    nw = len(bufs)
    pad_jobs = [(i, gi) for i in range(nw) if parts[i][0] == 0 and metas[i][1] > metas[i][2] * rows_of[i]
                for gi in range(metas[i][0])]

    def part_of(ref, i, j, h):
        per = metas[i][2]
        p, np_ = parts[i]
        pr = rows_of[i] // 2 // np_
        return ref.at[j // per, pl.ds((j % per) * rows_of[i] + h * (rows_of[i] // 2) + p * pr, pr)]

    def descriptors(ins, outs, sems):
        src, zp, dst = ins[:nw], ins[nw], outs
        pads, send, recv, fsend, frecv = sems
        x, y, c, me = _place()

        def pad(n):
            i, gi = pad_jobs[n]
            extra = metas[i][1] - metas[i][2] * rows_of[i]
            return pltpu.make_async_copy(zp.at[pl.ds(0, extra)], dst[i].at[gi, pl.ds(metas[i][2] * rows_of[i], extra)], pads.at[n])

        def ici(i, r, frm):
            return pltpu.make_async_remote_copy(
                src_ref=part_of(src[i], i, me, c), dst_ref=part_of(dst[i], i, frm, c), send_sem=send.at[i, r - 1],
                recv_sem=recv.at[i, r - 1], device_id=_chip_dev((me + r) % N_CHIPS, c), device_id_type=MESH)

        def d2d(i, r, frm, h):
            blk = part_of(dst[i], i, frm, h)
            return pltpu.make_async_remote_copy(src_ref=blk, dst_ref=blk, send_sem=fsend.at[i, r - 1],
                                                recv_sem=frecv.at[i, r - 1], device_id=(x, y, 1 - c), device_id_type=MESH)

        return pad, ici, d2d, c, me

    def start(ins, outs, sems):
        pad, ici, d2d, c, me = descriptors(ins, outs, sems)
        for n in range(len(pad_jobs)):
            pad(n).start()
        for i in range(nw):
            for r in range(1, N_CHIPS):
                ici(i, r, me).start()

    def finish(ins, outs, sems):
        pad, ici, d2d, c, me = descriptors(ins, outs, sems)
        for i in range(nw):
            for r in range(1, N_CHIPS):
                frm = (me - r) % N_CHIPS
                ici(i, r, frm).wait_recv()
                d2d(i, r, frm, c).start()
        for i in range(nw):
            for r in range(1, N_CHIPS):
                d2d(i, r, (me - r) % N_CHIPS, 1 - c).wait_recv()
        for i in range(nw):
            for r in range(1, N_CHIPS):
                ici(i, r, me).wait_send()
                d2d(i, r, (me - r) % N_CHIPS, c).wait_send()
        for n in range(len(pad_jobs)):
            pad(n).wait()

    return _Stage(ins=list(bufs) + [zero_pad], out_shapes=[SDS(b.shape, b.dtype) for b in bufs],
                  aliases={i: i for i in range(nw)},
                  sems=[pltpu.SemaphoreType.DMA((max(len(pad_jobs), 1),))] + [pltpu.SemaphoreType.DMA((nw, N_CHIPS - 1))] * 4,
                  start=start, finish=finish)


def _gather_small(packed, name):
    r, n = packed.shape

    def body(src, dst, send, recv):
        x, y, c, me = _place()
        dst[me] = src[...]
        cps = []
        for d in range(1, N_CHIPS):
            cp = pltpu.make_async_remote_copy(src_ref=src, dst_ref=dst.at[me], send_sem=send.at[d - 1], recv_sem=recv.at[d - 1],
                                              device_id=_chip_dev((me + d) % N_CHIPS, c), device_id_type=MESH)
            cp.start()
            cps.append(cp)
        for d in range(1, N_CHIPS):
            pltpu.make_async_remote_copy(src_ref=src, dst_ref=dst.at[(me - d) % N_CHIPS], send_sem=send.at[d - 1],
                                         recv_sem=recv.at[d - 1], device_id=_chip_dev((me + d) % N_CHIPS, c),
                                         device_id_type=MESH).wait_recv()
        for cp in cps:
            cp.wait_send()

    return pl.pallas_call(
        body, name=name, in_specs=[VMEM_SPEC], out_specs=VMEM_SPEC, out_shape=SDS((N_CHIPS, r, n), F32),
        scratch_shapes=[pltpu.SemaphoreType.DMA((N_CHIPS - 1,))] * 2,
    )(packed)


def _all_reduce_small(packed, name):
    r, n = packed.shape

    def body(src, out, slots, send, recv):
        x, y, c, me = _place()
        idx = 2 * me + c
        slots[idx] = src[...]
        cps = []

        def peer(d):
            p = (idx + d) % N_DEV
            return (p // 4, (p // 2) % 2, p % 2)

        for d in range(1, N_DEV):
            cp = pltpu.make_async_remote_copy(src_ref=src, dst_ref=slots.at[idx], send_sem=send.at[d - 1], recv_sem=recv.at[d - 1],
                                              device_id=peer(d), device_id_type=MESH)
            cp.start()
            cps.append(cp)
        for d in range(1, N_DEV):
            pltpu.make_async_remote_copy(src_ref=src, dst_ref=slots.at[(idx - d) % N_DEV], send_sem=send.at[d - 1],
                                         recv_sem=recv.at[d - 1], device_id=peer(d), device_id_type=MESH).wait_recv()
        for cp in cps:
            cp.wait_send()
        acc = slots[0]
        for k in range(1, N_DEV):
            acc = acc + slots[k]
        out[...] = acc

    return pl.pallas_call(
        body, name=name, in_specs=[VMEM_SPEC], out_specs=VMEM_SPEC, out_shape=SDS((r, n), F32),
        scratch_shapes=[pltpu.VMEM((N_DEV, r, n), F32)] + [pltpu.SemaphoreType.DMA((N_DEV - 1,))] * 2,
    )(packed)


def _simple_stage(ins, out_shapes, aliases, n_copies, copies):
    def start(ins_, outs, sems):
        for cp in copies(ins_, outs, *sems):
            cp.start()

    def finish(ins_, outs, sems):
        for cp in copies(ins_, outs, *sems):
            cp.wait()

    return _Stage(ins=list(ins), out_shapes=list(out_shapes), aliases=aliases,
                  sems=[pltpu.SemaphoreType.DMA((n_copies,))] * 2, start=start, finish=finish)


def _rs_pair_exchange(grads, metas, rows_of):
    nw = len(grads)

    def copies(src, dst, send, recv):
        x, y, c, me = _place()
        return [pltpu.make_async_remote_copy(
            src_ref=_half(src[i], j, 1 - c, rows_of[i], metas[i][2]), dst_ref=dst[i].at[j], send_sem=send.at[i * N_CHIPS + j],
            recv_sem=recv.at[i * N_CHIPS + j], device_id=(x, y, 1 - c), device_id_type=MESH)
            for i in range(nw) for j in range(N_CHIPS)]

    out_shapes = [SDS((N_CHIPS, rows_of[i] // 2, g.shape[2]), F32) for i, g in enumerate(grads)]
    return _simple_stage(grads, out_shapes, {}, nw * N_CHIPS, copies)


def _rs_pair_add(g, got, meta, rows, sp, name):
    per = meta[2]
    n = g.shape[2]
    hr = rows // 2
    tr = _tile(hr, max(16, (3 << 19) // n // 16 * 16), 16)

    def body(sp_ref, g_ref, got_ref, snd_ref, own_ref):
        j = pl.program_id(1)
        s = g_ref[...] + got_ref[...]
        snd_ref[...] = s.astype(BF16)

        @pl.when(j == sp_ref[1])
        def _():
            own_ref[...] = s

    grid_spec = pltpu.PrefetchScalarGridSpec(
        num_scalar_prefetch=1, grid=(hr // tr, N_CHIPS),
        in_specs=[pl.BlockSpec((None, tr, n), lambda i, j, sp: (j // per, ((j % per) * rows + sp[0] * hr) // tr + i, 0)),
                  pl.BlockSpec((None, tr, n), lambda i, j, sp: (j, i, 0))],
        out_specs=[pl.BlockSpec((None, tr, n), lambda i, j, sp: (j, i, 0)), pl.BlockSpec((tr, n), lambda i, j, sp: (i, 0))])
    return pl.pallas_call(
        body, name=name, grid_spec=grid_spec, out_shape=[SDS((N_CHIPS, hr, n), BF16), SDS((hr, n), F32)],
        compiler_params=_cparams("parallel", "arbitrary"),
    )(sp, g, got)


def _rs_chip_exchange(sends, part=(0, 1), prev=None):
    nw = len(sends)
    p, np_ = part

    def copies(src, dst, send, recv):
        x, y, c, me = _place()
        cps = []
        for i in range(nw):
            pr = sends[i].shape[1] // np_
            for r in range(1, N_CHIPS):
                cps.append(pltpu.make_async_remote_copy(
                    src_ref=src[i].at[(me + r) % N_CHIPS, pl.ds(p * pr, pr)], dst_ref=dst[i].at[r - 1, pl.ds(p * pr, pr)],
                    send_sem=send.at[i * (N_CHIPS - 1) + r - 1], recv_sem=recv.at[i * (N_CHIPS - 1) + r - 1],
                    device_id=_chip_dev((me + r) % N_CHIPS, c), device_id_type=MESH))
        return cps

    out_shapes = [SDS((N_CHIPS - 1,) + s.shape[1:], BF16) for s in sends]
    if prev is None:
        return _simple_stage(sends, out_shapes, {}, nw * (N_CHIPS - 1), copies)
    return _simple_stage(list(sends) + list(prev), out_shapes, {nw + i: i for i in range(nw)}, nw * (N_CHIPS - 1), copies)


def _rs_chip_add(own, got, sp, name):
    hr, n = own.shape
    tr = _tile(hr, max(16, (3 << 19) // n // 16 * 16), 16)

    def body(sp_ref, own_ref, got_ref, o_ref):
        acc = own_ref[...]
        for r in range(N_CHIPS - 1):
            acc = acc + got_ref[r].astype(F32)
        o_ref[...] = acc

    grid_spec = pltpu.PrefetchScalarGridSpec(
        num_scalar_prefetch=1, grid=(hr // tr,),
        in_specs=[pl.BlockSpec((tr, n), lambda i, sp: (i, 0)), pl.BlockSpec((N_CHIPS - 1, tr, n), lambda i, sp: (0, i, 0))],
        out_specs=pl.BlockSpec((tr, n), lambda i, sp: (sp[0] * (hr // tr) + i, 0)))
    return pl.pallas_call(body, name=name, grid_spec=grid_spec, out_shape=SDS((2 * hr, n), F32),
                          compiler_params=_cparams("parallel"))(sp, own, got)


def _rs_pair_share(blocks):
    nw = len(blocks)

    def copies(src, dst, send, recv):
        x, y, c, me = _place()
        cps = []
        for i in range(nw):
            hr = src[i].shape[0] // 2
            cps.append(pltpu.make_async_remote_copy(
                src_ref=src[i].at[pl.ds(c * hr, hr)], dst_ref=dst[i].at[pl.ds(c * hr, hr)], send_sem=send.at[i],
                recv_sem=recv.at[i], device_id=(x, y, 1 - c), device_id_type=MESH))
        return cps

    return _simple_stage(blocks, [SDS(b.shape, b.dtype) for b in blocks], {i: i for i in range(nw)}, nw, copies)


def kernel(x, p, ln_g, ln_b, ffn1_w_in, ffn1_w_out, mix_w_in, conv_w, hg_lower_bound, hg_norm_w, branch_w_conv, branch_w_hgrn, mix_w_out, ffn2_w_in, ffn2_w_out, ple_w_gate, ple_w_proj, loss_target, m_ln_g, m_ln_b, m_ffn1_w_in, m_ffn1_w_out, m_mix_w_in, m_conv_w, m_hg_lower_bound, m_hg_norm_w, m_branch_w_conv, m_branch_w_hgrn, m_mix_w_out, m_ffn2_w_in, m_ffn2_w_out, m_ple_w_gate, m_ple_w_proj, v_ln_g, v_ln_b, v_ffn1_w_in, v_ffn1_w_out, v_mix_w_in, v_conv_w, v_hg_lower_bound, v_hg_norm_w, v_branch_w_conv, v_branch_w_hgrn, v_mix_w_out, v_ffn2_w_in, v_ffn2_w_out, v_ple_w_gate, v_ple_w_proj):
    assert ln_g.shape[0] == DEPTH and x.shape[0] == 1 and p.shape[:2] == (1, 1)
    t, d = x.shape[1], x.shape[2]
    w = d // 2
    x0 = x.reshape(t, d)
    x0b = _to_bf16(x0, "x_bf16")
    pe = p.reshape(t, p.shape[-1])
    target = loss_target.reshape(t, d)
    cx, cy, cc = lax.axis_index("x"), lax.axis_index("y"), lax.axis_index("c")
    chip = 2 * cx + cy
    sp = jnp.stack([cc, chip]).astype(jnp.int32)

    big = dict(ffn1_w_in=ffn1_w_in, ffn1_w_out=ffn1_w_out, mix_w_in=mix_w_in, branch_w_conv=branch_w_conv,
               branch_w_hgrn=branch_w_hgrn, mix_w_out=mix_w_out, ffn2_w_in=ffn2_w_in, ffn2_w_out=ffn2_w_out,
               ple_w_gate=ple_w_gate, ple_w_proj=ple_w_proj)
    moments = dict(ffn1_w_in=(m_ffn1_w_in, v_ffn1_w_in), ffn1_w_out=(m_ffn1_w_out, v_ffn1_w_out), mix_w_in=(m_mix_w_in, v_mix_w_in),
                   branch_w_conv=(m_branch_w_conv, v_branch_w_conv), branch_w_hgrn=(m_branch_w_hgrn, v_branch_w_hgrn),
                   mix_w_out=(m_mix_w_out, v_mix_w_out), ffn2_w_in=(m_ffn2_w_in, v_ffn2_w_in), ffn2_w_out=(m_ffn2_w_out, v_ffn2_w_out),
                   ple_w_gate=(m_ple_w_gate, v_ple_w_gate), ple_w_proj=(m_ple_w_proj, v_ple_w_proj))
    names = list(big)

    n_loc = ffn1_w_in.shape[-1]
    n_pad = -(-n_loc // LANES) * LANES
    assert mix_w_in.shape[-1] % LANES == 0 and ffn1_w_out.shape[1] * 2 == n_loc
    pad_cols = dict(ffn1_w_in=n_pad, ffn2_w_in=n_pad)
    meta = {k: (N_CHIPS, big[k].shape[1], 1) for k in names}
    meta["ffn1_w_out"] = meta["ffn2_w_out"] = (2, n_pad, 2)
    rows = {k: big[k].shape[1] for k in names}
    wbuf = {k: _cast_pad(big[k], pad_cols.get(k, big[k].shape[2]), meta[k], sp, "cast_" + k) for k in names}
    zero_pad = jnp.zeros((max(n_pad - n_loc, 16), d), BF16)

    def gather(*items):
        ks = [k for k, _, _ in items]
        return _gather_stage([wbuf[k] for k in ks], [meta[k] for k in ks], [rows[k] for k in ks], [(p_, n_) for _, p_, n_ in items],
                             zero_pad), ks

    def gathered(ks, outs):
        wbuf.update(zip(ks, outs))

    def w3(k):
        return wbuf[k]

    def w2(k):
        return wbuf[k].reshape(-1, wbuf[k].shape[2])

    dq, wq = d // N_CHIPS, w // N_CHIPS
    small = jnp.concatenate([ln_g[0], ln_b[0], jnp.pad(conv_w[0], ((0, 5), (0, dq - wq)))], axis=0)
    small = _gather_small(small, "gather_small")
    lng = small[:, 0:4, :].transpose(1, 0, 2).reshape(4, 1, d)
    lnb = small[:, 4:8, :].transpose(1, 0, 2).reshape(4, 1, d)
    cw = small[:, 8:11, :wq].transpose(1, 0, 2).reshape(3, w)
    hg = hg_lower_bound
    nw_ = hg_norm_w

    st, ks = gather(("ffn1_w_in", 0, 1))
    gathered(ks, _run_stages([st], "gather_first")[0])
    st, ks = gather(("ffn1_w_out", 0, 1), ("mix_w_in", 0, 2))
    z1, got = _mm(x0b, w3("ffn1_w_in"), name="ffn1_in", b_blocked=True, out_dtype=BF16, comm=[st])
    gathered(ks, got)
    h1 = _swiglu_fwd(z1, "ffn1_act")
    st, ks = gather(("mix_w_in", 1, 2))
    y1, got = _mm(h1, w2("ffn1_w_out"), name="ffn1_out", tm=1024, tn=1024, tk=2816, comm=[st])
    gathered(ks, got)
    r1, x1, x1b = _ln_fwd(x0, y1, lng[0], lnb[0], 0.5, "ln0")
    st, ks = gather(("branch_w_conv", 0, 1), ("branch_w_hgrn", 0, 1), ("mix_w_out", 0, 1), ("ffn2_w_in", 0, 2))
    z, got = _mm(x1b, w3("mix_w_in"), name="mix_in", b_blocked=True, comm=[st])
    gathered(ks, got)
    ya = _conv_fwd(z, cw, w, "conv_fwd")
    st, ks = gather(("ffn2_w_in", 1, 2))
    (yb, o_h, states), got = _hgrn_fwd(z, hg, nw_, w, "hgrn_fwd", comm=[st])
    gathered(ks, got)
    ma = _mm(ya, w3("branch_w_conv"), name="branch_conv", b_blocked=True, tn=512)
    mb = _mm(yb, w3("branch_w_hgrn"), name="branch_hgrn", b_blocked=True, tn=512)
    merged = _merge_fwd(z, ma, mb, w, "merge_fwd")
    y2 = _mm(merged, w2("mix_w_out"), name="mix_out", tn=1024)
    r2, x2, x2b = _ln_fwd(x1, y2, lng[1], lnb[1], 1.0, "ln1")
    st, ks = gather(("ffn2_w_out", 0, 1), ("ple_w_gate", 0, 1), ("ple_w_proj", 0, 1))
    z3, got = _mm(x2b, w3("ffn2_w_in"), name="ffn2_in", b_blocked=True, out_dtype=BF16, comm=[st])
    gathered(ks, got)
    h3 = _swiglu_fwd(z3, "ffn2_act")
    y3 = _mm(h3, w2("ffn2_w_out"), name="ffn2_out", tm=1024, tn=1024, tk=2816)
    r3, x3, x3b = _ln_fwd(x2, y3, lng[2], lnb[2], 0.5, "ln2")
    gp = _mm(x3b, w2("ple_w_gate"), name="ple_gate", tn=1024)
    pp = _mm(pe, w3("ple_w_proj"), name="ple_proj", b_blocked=True, tn=512)
    dr4, dgp, dpp, dg3, db3, sq = _tail(x3, gp, pp, lng[3], lnb[3], target, "tail")

    grads, sends, owns, blocks, outs = {}, {}, {}, {}, {}

    def pair_exchange(*ks):
        return _rs_pair_exchange([grads[k] for k in ks], [meta[k] for k in ks], [rows[k] for k in ks])

    def pair_add(ks, got):
        for k, g_ in zip(ks, got):
            sends[k], owns[k] = _rs_pair_add(grads[k], g_, meta[k], rows[k], sp, "rs_pair_add_" + k)

    def chip_exchange(*ks):
        return _rs_chip_exchange([sends[k] for k in ks])

    def chip_add(ks, got):
        for k, g_ in zip(ks, got):
            blocks[k] = _rs_chip_add(owns[k], g_, sp, "rs_chip_add_" + k)

    def pair_share(*ks):
        return _rs_pair_share([blocks[k] for k in ks])

    def update(ks, full):
        for k, g_ in zip(ks, full):
            m_, v_ = moments[k]
            outs[k] = _adamw(big[k], g_, m_, v_, "adamw_" + k)

    ple = ("ple_w_gate", "ple_w_proj")
    mixo = ("mix_w_out", "branch_w_conv", "branch_w_hgrn")
    dx3m = _mm(dgp, w2("ple_w_gate"), name="d_ple_gate_x", tb=True, tn=1024, tk=2048)
    grads["ple_w_gate"] = _mm(x3b, dgp, name="d_ple_gate_w", ta=True, tm=1024, tk=2048, tn=1024).reshape(N_CHIPS, -1, d)
    grads["ple_w_proj"] = _mm(pe, dpp, name="d_ple_proj_w", ta=True, out_blocked=N_CHIPS, tk=2048, tn=512)
    dr3, dy3b, dg2, db2 = _ln_bwd(dr4, dx3m, r3, lng[2], 0.5, "ln2_bwd")
    dh3, got = _mm(dy3b, w2("ffn2_w_out"), name="d_ffn2_out_x", tb=True, out_dtype=BF16, tn=1408, tk=2048,
                   comm=[pair_exchange(*ple)])
    pair_add(ple, got)
    g_, got = _mm(h3, dy3b, name="d_ffn2_out_w", ta=True, tm=1408, tk=2048, tn=1024, comm=[chip_exchange(*ple)])
    grads["ffn2_w_out"] = g_.reshape(2, n_pad, d)
    chip_add(ple, got)
    dz3 = _swiglu_bwd(dh3, z3, "ffn2_act_bwd")
    dx2m, got, full = _mm(dz3, w3("ffn2_w_in"), name="d_ffn2_in_x", tb=True, b_blocked=True, tm=1024, tn=1024, tk=2816,
                          comm=[pair_exchange("ffn2_w_out"), pair_share(*ple)])
    pair_add(["ffn2_w_out"], got)
    update(ple, full)
    grads["ffn2_w_in"], got = _mm(x2b, dz3, name="d_ffn2_in_w", ta=True, out_blocked=N_CHIPS, tk=4096, comm=[chip_exchange("ffn2_w_out")])
    chip_add(["ffn2_w_out"], got)
    dr2, dy2b, dg1, db1 = _ln_bwd(dr3, dx2m, r2, lng[1], 1.0, "ln1_bwd")
    dmer, got = _mm(dy2b, w2("mix_w_out"), name="d_mix_out_x", tb=True, tn=1024, tk=2048, comm=[pair_exchange("ffn2_w_in")])
    pair_add(["ffn2_w_in"], got)
    g_, full = _mm(merged, dy2b, name="d_mix_out_w", ta=True, tm=1024, tk=2048, tn=1024, comm=[pair_share("ffn2_w_out")])
    grads["mix_w_out"] = g_.reshape(N_CHIPS, -1, d)
    update(["ffn2_w_out"], full)
    dma, dmb, dgc, dgh = _merge_bwd(dmer, z, ma, mb, w, "merge_bwd")
    dya = _mm(dma, w3("branch_w_conv"), name="d_branch_conv_x", tb=True, b_blocked=True, tn=1024, tk=512)
    dyb = _mm(dmb, w3("branch_w_hgrn"), name="d_branch_hgrn_x", tb=True, b_blocked=True, tn=1024, tk=512)
    grads["branch_w_conv"] = _mm(ya, dma, name="d_branch_conv_w", ta=True, out_blocked=N_CHIPS, tm=1024, tk=2048, tn=512)
    grads["branch_w_hgrn"] = _mm(yb, dmb, name="d_branch_hgrn_w", ta=True, out_blocked=N_CHIPS, tm=1024, tk=2048, tn=512)
    dbg, dcg, dhc, dcw = _conv_bwd(dya, z, cw, w, "conv_bwd")
    (dq_, df_, di_, dgr_, dhg, dnw), got2, got = _hgrn_bwd(dyb, z, o_h, states, hg, nw_, w, "hgrn_bwd",
                                                            comm=[chip_exchange("ffn2_w_in"), pair_exchange(*mixo)])
    chip_add(["ffn2_w_in"], got2)
    pair_add(mixo, got)
    dz = _concat_cols([dbg, dcg, dhc, dq_, df_, di_, dgr_, dgc, dgh], "dz_concat")
    dx1m, full, got = _mm(dz, w3("mix_w_in"), name="d_mix_in_x", tb=True, b_blocked=True, tm=1024, tn=1024, tk=2816,
                          comm=[pair_share("ffn2_w_in"), chip_exchange(*mixo)])
    update(["ffn2_w_in"], full)
    chip_add(mixo, got)
    grads["mix_w_in"], full = _mm(x1b, dz, name="d_mix_in_w", ta=True, out_blocked=N_CHIPS, tk=4096, comm=[pair_share(*mixo)])
    update(mixo, full)
    dr1, dy1b, dg0, db0 = _ln_bwd(dr2, dx1m, r1, lng[0], 0.5, "ln0_bwd")
    dh1, got = _mm(dy1b, w2("ffn1_w_out"), name="d_ffn1_out_x", tb=True, out_dtype=BF16, tn=1408, tk=2048,
                   comm=[pair_exchange("mix_w_in")])
    pair_add(["mix_w_in"], got)
    mix_sends = [sends["mix_w_in"]]
    g_, got_a = _mm(h1, dy1b, name="d_ffn1_out_w", ta=True, tm=1408, tk=2048, tn=1024, comm=[_rs_chip_exchange(mix_sends, (0, 2))])
    grads["ffn1_w_out"] = g_.reshape(2, n_pad, d)
    dz1 = _swiglu_bwd(dh1, z1, "ffn1_act_bwd")
    g_other, got2, got = _mm(x0b, dz1, name="d_ffn1_in_w_other", ta=True, out_blocked=N_CHIPS, tk=4096, half=(sp, True),
                             comm=[_rs_chip_exchange(mix_sends, (1, 2), got_a), pair_exchange("ffn1_w_out")])
    chip_add(["mix_w_in"], got2)
    pair_add(["ffn1_w_out"], got)
    grads["ffn1_w_in"], full, got2, got = _mm(
        x0b, dz1, name="d_ffn1_in_w_own", ta=True, out_blocked=N_CHIPS, tk=4096, half=(sp, False),
        comm=[pair_share("mix_w_in"), chip_exchange("ffn1_w_out"),
              _rs_pair_exchange([g_other], [meta["ffn1_w_in"]], [rows["ffn1_w_in"]])])
    update(["mix_w_in"], full)
    chip_add(["ffn1_w_out"], got2)
    pair_add(["ffn1_w_in"], got)
    dx0m, got2, full = _mm(dz1, w3("ffn1_w_in"), name="d_ffn1_in_x", tb=True, b_blocked=True, tm=1024, tn=1024, tk=2816,
                           comm=[chip_exchange("ffn1_w_in"), pair_share("ffn1_w_out")])
    chip_add(["ffn1_w_in"], got2)
    update(["ffn1_w_out"], full)
    grad_x = _residual_out(dr1, dx0m, "grad_x").reshape(x.shape)
    update(["ffn1_w_in"], _run_stages([pair_share("ffn1_w_in")], "rs_tail_pair")[0])

    pack = jnp.concatenate([
        dg0, dg1, dg2, dg3, db0, db1, db2, db3,
        jnp.pad(dcw, ((0, 0), (0, d - w))), jnp.pad(dhg, ((0, 0), (0, d - w))),
        jnp.pad(jnp.sum(dnw.reshape(-1, HEAD), axis=0, keepdims=True), ((0, 0), (0, d - HEAD))), sq], axis=0)
    pack = _all_reduce_small(jnp.pad(pack, ((0, 1), (0, 0))), "reduce_small")
    loss = (0.5 / d) * jnp.sum(pack[14])
    g_ln_g = lax.dynamic_slice_in_dim(pack[0:4], chip * dq, dq, axis=1)
    g_ln_b = lax.dynamic_slice_in_dim(pack[4:8], chip * dq, dq, axis=1)
    g_conv = lax.dynamic_slice_in_dim(pack[8:11, :w], chip * wq, wq, axis=1)
    g_hg = pack[11:13, :w]
    g_nw = pack[13:14, :HEAD]

    small_w = dict(ln_g=(ln_g, g_ln_g, m_ln_g, v_ln_g), ln_b=(ln_b, g_ln_b, m_ln_b, v_ln_b),
                   conv_w=(conv_w, g_conv, m_conv_w, v_conv_w), hg_lower_bound=(hg_lower_bound, g_hg, m_hg_lower_bound, v_hg_lower_bound),
                   hg_norm_w=(hg_norm_w, g_nw, m_hg_norm_w, v_hg_norm_w))
    for k, (w_, g_, m_, v_) in small_w.items():
        outs[k] = _adamw(w_, g_.reshape(-1, w_.shape[-1]), m_, v_, "adamw_" + k)

    order = ["ln_g", "ln_b", "ffn1_w_in", "ffn1_w_out", "mix_w_in", "conv_w", "hg_lower_bound", "hg_norm_w", "branch_w_conv",
             "branch_w_hgrn", "mix_w_out", "ffn2_w_in", "ffn2_w_out", "ple_w_gate", "ple_w_proj"]
    return (loss, grad_x, *[outs[k][0] for k in order], *[outs[k][1] for k in order], *[outs[k][2] for k in order],
            *[outs[k][3] for k in order])
```

```python
import collections
import functools

import jax
import jax.numpy as jnp
from jax import lax
from jax.experimental import pallas as pl
from jax.experimental.pallas import tpu as pltpu

F32 = jnp.float32
BF16 = jnp.bfloat16
MESH = pl.DeviceIdType.MESH
ANY = pl.BlockSpec(memory_space=pl.ANY)
VMEM_SPEC = pl.BlockSpec(memory_space=pltpu.VMEM)
SDS = jax.ShapeDtypeStruct

DEPTH = 1
ALPHA = (2.0 * DEPTH) ** 0.25
LN_EPS = 1e-5
RMS_EPS = 1e-6
CHUNK = 32
HEAD = 128
ADAM_LR, ADAM_B1, ADAM_B2, ADAM_EPS, ADAM_WD, ADAM_STEP = 0.001, 0.9, 0.999, 1e-08, 0.01, 10

LANES = 128
N_CHIPS = 4
N_DEV = 8
VMEM_LIMIT = 52 * 1024 * 1024


def _cparams(*sem):
    if sem:
        return pltpu.CompilerParams(dimension_semantics=sem, vmem_limit_bytes=VMEM_LIMIT)
    return pltpu.CompilerParams(vmem_limit_bytes=VMEM_LIMIT)


def _tile(n, target, mult):
    best = None
    for t in range(mult, min(n, target) + 1, mult):
        if n % t == 0:
            best = t
    return best if best is not None else n


def _sigmoid(x):
    return 1.0 / (1.0 + jnp.exp(-x))


_Stage = collections.namedtuple("_Stage", "ins out_shapes aliases sems start finish")


def _hosted_call(compute, stages, *, name, grid, in_specs, out_specs, out_shape, scratch_shapes, operands, parallel,
                 prefetch=None):
    n_cmp, n_out, n_scr = len(in_specs), len(out_specs), len(scratch_shapes)
    n_in = n_cmp
    n_pre = int(prefetch is not None)
    c_in = [len(s.ins) for s in stages]
    c_out = [len(s.out_shapes) for s in stages]
    c_sem = [len(s.sems) for s in stages]
    aliases = {}
    for si, s in enumerate(stages):
        for a_in, a_out in s.aliases.items():
            aliases[n_pre + n_in + sum(c_in[:si]) + a_in] = n_out + sum(c_out[:si]) + a_out

    def body(*refs):
        refs = refs[n_pre:]
        ins = refs[:n_cmp]
        cins = refs[n_in:n_in + sum(c_in)]
        outs = refs[n_in + sum(c_in):n_in + sum(c_in) + n_out]
        couts = refs[n_in + sum(c_in) + n_out:n_in + sum(c_in) + n_out + sum(c_out)]
        scr = refs[n_in + sum(c_in) + n_out + sum(c_out):][:n_scr]
        sems = refs[n_in + sum(c_in) + n_out + sum(c_out) + n_scr:]

        def stage_refs(si):
            return (cins[sum(c_in[:si]):sum(c_in[:si + 1])], couts[sum(c_out[:si]):sum(c_out[:si + 1])],
                    sems[sum(c_sem[:si]):sum(c_sem[:si + 1])])

        if stages:
            first = functools.reduce(jnp.logical_and, [pl.program_id(ax) == 0 for ax in range(len(grid))])
            last = functools.reduce(jnp.logical_and, [pl.program_id(ax) == grid[ax] - 1 for ax in range(len(grid))])

            @pl.when(first)
            def _():
                for si, s in enumerate(stages):
                    s.start(*stage_refs(si))

        compute(*ins, *outs, *scr)
        if stages:
            @pl.when(last)
            def _():
                for si, s in enumerate(stages):
                    s.finish(*stage_refs(si))

    sem = ("arbitrary",) * len(grid) if stages else ("parallel",) * parallel + ("arbitrary",) * (len(grid) - parallel)
    all_in = list(in_specs) + [ANY] * (n_in - n_cmp + sum(c_in))
    all_out = list(out_specs) + [ANY] * sum(c_out)
    all_scr = list(scratch_shapes) + [q for s in stages for q in s.sems]
    all_shape = list(out_shape) + [o for s in stages for o in s.out_shapes]
    args = list(operands) + [a for s in stages for a in s.ins]
    if prefetch is None:
        res = pl.pallas_call(body, name=name, grid=grid, in_specs=all_in, out_specs=all_out, out_shape=all_shape,
                             input_output_aliases=aliases, scratch_shapes=all_scr, compiler_params=_cparams(*sem))(*args)
    else:
        grid_spec = pltpu.PrefetchScalarGridSpec(num_scalar_prefetch=1, grid=grid, in_specs=all_in, out_specs=all_out,
                                                 scratch_shapes=all_scr)
        res = pl.pallas_call(body, name=name, grid_spec=grid_spec, out_shape=all_shape, input_output_aliases=aliases,
                             compiler_params=_cparams(*sem))(prefetch, *args)
    main = res[0] if n_out == 1 else list(res[:n_out])
    if not stages:
        return main
    rest = res[n_out:]
    return (main, *[list(rest[sum(c_out[:si]):sum(c_out[:si + 1])]) for si in range(len(stages))])


def _run_stages(stages, name):
    def body(*refs):
        n_i = sum(len(s.ins) for s in stages)
        n_o = sum(len(s.out_shapes) for s in stages)
        cins, couts, sems = refs[:n_i], refs[n_i:n_i + n_o], refs[n_i + n_o:]
        pos = [0, 0, 0]
        parts = []
        for s in stages:
            parts.append((cins[pos[0]:pos[0] + len(s.ins)], couts[pos[1]:pos[1] + len(s.out_shapes)], sems[pos[2]:pos[2] + len(s.sems)]))
            pos = [pos[0] + len(s.ins), pos[1] + len(s.out_shapes), pos[2] + len(s.sems)]
        for s, p_ in zip(stages, parts):
            s.start(*p_)
        for s, p_ in zip(stages, parts):
            s.finish(*p_)

    aliases, ni, no = {}, 0, 0
    for s in stages:
        for a_in, a_out in s.aliases.items():
            aliases[ni + a_in] = no + a_out
        ni, no = ni + len(s.ins), no + len(s.out_shapes)
    res = pl.pallas_call(
        body, name=name, in_specs=[ANY] * ni, out_specs=[ANY] * no, out_shape=[o for s in stages for o in s.out_shapes],
        input_output_aliases=aliases, scratch_shapes=[q for s in stages for q in s.sems],
    )(*[a for s in stages for a in s.ins])
    out, pos = [], 0
    for s in stages:
        out.append(list(res[pos:pos + len(s.out_shapes)]))
        pos += len(s.out_shapes)
    return out


def _mm(a, b, *, name, ta=False, tb=False, b_blocked=False, out_blocked=0, out_dtype=F32,
        tm=512, tn=1408, tk=2048, comm=(), half=None):
    if ta:
        kd, m = a.shape
    else:
        m, kd = a.shape
    if b_blocked and not tb:
        g, kb, nb = b.shape
        assert kb == kd
        n = g * nb
        tn = _tile(nb, tn, LANES)
        tk = _tile(kd, tk, LANES)
        per_n = nb // tn
        b_spec = pl.BlockSpec((None, tk, tn), lambda i, j, k, *s: (j // per_n, k, j % per_n))
    elif b_blocked and tb:
        g, n, kb = b.shape
        assert g * kb == kd
        tn = _tile(n, tn, LANES)
        tk = _tile(kb, tk, LANES)
        per_k = kb // tk
        b_spec = pl.BlockSpec((None, tn, tk), lambda i, j, k, *s: (k // per_k, j, k % per_k))
    elif tb:
        n, kb = b.shape
        assert kb == kd
        tn = _tile(n, tn, LANES)
        tk = _tile(kd, tk, LANES)
        b_spec = pl.BlockSpec((tn, tk), lambda i, j, k, *s: (j, k))
    else:
        kb, n = b.shape
        assert kb == kd
        tn = _tile(n // out_blocked if out_blocked else n, tn, LANES)
        per_o = (n // out_blocked) // tn if out_blocked else None
        tk = _tile(kd, tk, LANES)
        b_spec = pl.BlockSpec((tk, tn), lambda i, j, k, *s: (k, j))
    m_run = m // 2 if half else m
    tm = _tile(m_run, tm, LANES if ta else 8)

    def row(i, s):
        if not half:
            return i
        h = 1 - s[0][0] if half[1] else s[0][0]
        return h * (m_run // tm) + i

    if ta:
        a_spec = pl.BlockSpec((tk, tm), lambda i, j, k, *s: (k, row(i, s)))
    else:
        a_spec = pl.BlockSpec((tm, tk), lambda i, j, k, *s: (row(i, s), k))
    if out_blocked:
        assert not b_blocked and not tb
        o_spec = pl.BlockSpec((None, tm, tn), lambda i, j, k, *s: (j // per_o, row(i, s), j % per_o))
        o_shape = SDS((out_blocked, m, n // out_blocked), out_dtype)
    else:
        o_spec = pl.BlockSpec((tm, tn), lambda i, j, k, *s: (row(i, s), j))
        o_shape = SDS((m, n), out_dtype)
    nk = kd // tk
    dn = (((0 if ta else 1,), (1 if tb else 0,)), ((), ()))
    grid = (m_run // tm, n // tn, nk)

    def compute(a_ref, b_ref, o_ref, acc_ref):
        part = lax.dot_general(a_ref[...].astype(BF16), b_ref[...].astype(BF16), dn, preferred_element_type=F32)
        if nk == 1:
            o_ref[...] = part.astype(o_ref.dtype)
        else:
            k = pl.program_id(2)

            @pl.when(k == 0)
            def _():
                acc_ref[...] = part

            @pl.when(k > 0)
            def _():
                acc_ref[...] += part

            @pl.when(k == nk - 1)
            def _():
                o_ref[...] = acc_ref[...].astype(o_ref.dtype)

    return _hosted_call(compute, comm, name=name, grid=grid, in_specs=[a_spec, b_spec], out_specs=[o_spec], out_shape=[o_shape],
                        scratch_shapes=[pltpu.VMEM((tm, tn), F32)], operands=(a, b), parallel=2,
                        prefetch=half[0] if half else None)


def _swiglu_fwd(z, name):
    t, n = z.shape
    n2 = n // 2
    tr = _tile(t, 128, 16)

    def body(a_ref, u_ref, o_ref):
        a = a_ref[...].astype(F32)
        o_ref[...] = (a * _sigmoid(a) * u_ref[...].astype(F32)).astype(o_ref.dtype)

    return pl.pallas_call(
        body, name=name, grid=(t // tr,),
        in_specs=[pl.BlockSpec((tr, n2), lambda i: (i, 0)), pl.BlockSpec((tr, n2), lambda i: (i, 1))],
        out_specs=pl.BlockSpec((tr, n2), lambda i: (i, 0)), out_shape=SDS((t, n2), BF16),
        compiler_params=_cparams("parallel"),
    )(z, z)


def _swiglu_bwd(dh, z, name):
    t, n = z.shape
    n2 = n // 2
    tr = _tile(t, 128, 16)

    def body(dh_ref, a_ref, u_ref, o_ref):
        a = a_ref[...].astype(F32)
        dh_ = dh_ref[...].astype(F32)
        s = _sigmoid(a)
        o_ref[:, 0:n2] = (dh_ * u_ref[...].astype(F32) * (s * (1.0 + a * (1.0 - s)))).astype(o_ref.dtype)
        o_ref[:, n2:n] = (dh_ * a * s).astype(o_ref.dtype)

    return pl.pallas_call(
        body, name=name, grid=(t // tr,),
        in_specs=[pl.BlockSpec((tr, n2), lambda i: (i, 0)), pl.BlockSpec((tr, n2), lambda i: (i, 0)),
                  pl.BlockSpec((tr, n2), lambda i: (i, 1))],
        out_specs=pl.BlockSpec((tr, n), lambda i: (i, 0)), out_shape=SDS((t, n), BF16),
        compiler_params=_cparams("parallel"),
    )(dh, z, z)


def _ln_stats(r):
    mu = jnp.mean(r, axis=-1, keepdims=True)
    xc = r - mu
    var = jnp.mean(xc * xc, axis=-1, keepdims=True)
    return xc * lax.rsqrt(var + LN_EPS)


def _ln_fwd(xp, y, g, b, scale, name):
    t, d = xp.shape
    tr = _tile(t, 256, 16)

    def body(xp_ref, y_ref, g_ref, b_ref, r_ref, x_ref, xb_ref):
        r = ALPHA * xp_ref[...] + scale * y_ref[...]
        x = _ln_stats(r) * g_ref[...] + b_ref[...]
        r_ref[...] = r
        x_ref[...] = x
        xb_ref[...] = x.astype(BF16)

    row = pl.BlockSpec((tr, d), lambda i: (i, 0))
    vec = pl.BlockSpec((1, d), lambda i: (0, 0))
    return pl.pallas_call(
        body, name=name, grid=(t // tr,), in_specs=[row, row, vec, vec], out_specs=[row, row, row],
        out_shape=[SDS((t, d), F32), SDS((t, d), F32), SDS((t, d), BF16)], compiler_params=_cparams("parallel"),
    )(xp, y, g, b)


def _ln_bwd(dra, dxm, r, g, scale, name):
    t, d = r.shape
    tr = _tile(t, 256, 16)

    def body(dra_ref, dxm_ref, r_ref, g_ref, dr_ref, dyb_ref, dg_ref, db_ref):
        i = pl.program_id(0)
        dx = ALPHA * dra_ref[...] + dxm_ref[...]
        rr = r_ref[...]
        mu = jnp.mean(rr, axis=-1, keepdims=True)
        xc = rr - mu
        rstd = lax.rsqrt(jnp.mean(xc * xc, axis=-1, keepdims=True) + LN_EPS)
        xh = xc * rstd
        dxh = dx * g_ref[...]
        dr = rstd * (dxh - jnp.mean(dxh, axis=-1, keepdims=True) - xh * jnp.mean(dxh * xh, axis=-1, keepdims=True))
        dr_ref[...] = dr
        dyb_ref[...] = (scale * dr).astype(BF16)
        dg = jnp.sum(dx * xh, axis=0, keepdims=True)
        db = jnp.sum(dx, axis=0, keepdims=True)

        @pl.when(i == 0)
        def _():
            dg_ref[...] = dg
            db_ref[...] = db

        @pl.when(i > 0)
        def _():
            dg_ref[...] += dg
            db_ref[...] += db

    row = pl.BlockSpec((tr, d), lambda i: (i, 0))
    vec = pl.BlockSpec((1, d), lambda i: (0, 0))
    return pl.pallas_call(
        body, name=name, grid=(t // tr,), in_specs=[row, row, row, vec], out_specs=[row, row, vec, vec],
        out_shape=[SDS((t, d), F32), SDS((t, d), BF16), SDS((1, d), F32), SDS((1, d), F32)],
        compiler_params=_cparams("arbitrary"),
    )(dra, dxm, r, g)


def _tail(x3, gp, pp, g, b, target, name):
    t, d = x3.shape
    tr = _tile(t, 256, 16)

    def body(x3_ref, gp_ref, pp_ref, g_ref, b_ref, tg_ref, dr_ref, dgp_ref, dpp_ref, dg_ref, db_ref, sq_ref):
        i = pl.program_id(0)
        gate = _sigmoid(gp_ref[...])
        pp_ = pp_ref[...]
        r = ALPHA * x3_ref[...] + gate * pp_
        mu = jnp.mean(r, axis=-1, keepdims=True)
        xc = r - mu
        rstd = lax.rsqrt(jnp.mean(xc * xc, axis=-1, keepdims=True) + LN_EPS)
        xh = xc * rstd
        err = xh * g_ref[...] + b_ref[...] - tg_ref[...]
        dx = err * (1.0 / d)
        dxh = dx * g_ref[...]
        dr = rstd * (dxh - jnp.mean(dxh, axis=-1, keepdims=True) - xh * jnp.mean(dxh * xh, axis=-1, keepdims=True))
        dr_ref[...] = dr
        dgp_ref[...] = (dr * pp_ * gate * (1.0 - gate)).astype(BF16)
        dpp_ref[...] = (dr * gate).astype(BF16)
        dg = jnp.sum(dx * xh, axis=0, keepdims=True)
        db = jnp.sum(dx, axis=0, keepdims=True)
        sq = jnp.sum(err * err, axis=0, keepdims=True)

        @pl.when(i == 0)
        def _():
            dg_ref[...] = dg
            db_ref[...] = db
            sq_ref[...] = sq

        @pl.when(i > 0)
        def _():
            dg_ref[...] += dg
            db_ref[...] += db
            sq_ref[...] += sq

    row = pl.BlockSpec((tr, d), lambda i: (i, 0))
    vec = pl.BlockSpec((1, d), lambda i: (0, 0))
    return pl.pallas_call(
        body, name=name, grid=(t // tr,), in_specs=[row, row, row, vec, vec, row],
        out_specs=[row, row, row, vec, vec, vec],
        out_shape=[SDS((t, d), F32), SDS((t, d), BF16), SDS((t, d), BF16), SDS((1, d), F32), SDS((1, d), F32),
                   SDS((1, d), F32)],
        compiler_params=_cparams("arbitrary"),
    )(x3, gp, pp, g, b, target)


def _to_bf16(x, name):
    t, d = x.shape
    tr = _tile(t, 512, 16)
    row = pl.BlockSpec((tr, d), lambda i: (i, 0))

    def body(x_ref, o_ref):
        o_ref[...] = x_ref[...].astype(BF16)

    return pl.pallas_call(body, name=name, grid=(t // tr,), in_specs=[row], out_specs=row, out_shape=SDS((t, d), BF16),
                          compiler_params=_cparams("parallel"))(x)


def _concat_cols(parts, name):
    t = parts[0].shape[0]
    widths = [p_.shape[1] for p_ in parts]
    tr = _tile(t, 256, 16)

    def body(*refs):
        o_ref = refs[-1]
        at = 0
        for ref, wd in zip(refs[:-1], widths):
            o_ref[:, at:at + wd] = ref[...]
            at += wd

    return pl.pallas_call(
        body, name=name, grid=(t // tr,), in_specs=[pl.BlockSpec((tr, wd), lambda i: (i, 0)) for wd in widths],
        out_specs=pl.BlockSpec((tr, sum(widths)), lambda i: (i, 0)), out_shape=SDS((t, sum(widths)), parts[0].dtype),
        compiler_params=_cparams("parallel"),
    )(*parts)


def _residual_out(dra, dxm, name):
    t, d = dra.shape
    tr = _tile(t, 256, 8)

    def body(a_ref, b_ref, o_ref):
        o_ref[...] = ALPHA * a_ref[...] + b_ref[...]

    row = pl.BlockSpec((tr, d), lambda i: (i, 0))
    return pl.pallas_call(body, name=name, grid=(t // tr,), in_specs=[row, row], out_specs=row,
                          out_shape=SDS((t, d), F32), compiler_params=_cparams("parallel"))(dra, dxm)


def _merge_fwd(z, ma, mb, w, name):
    t = z.shape[0]
    tr = _tile(t, 256, 16)

    def body(gc_ref, gh_ref, ma_ref, mb_ref, o_ref):
        o_ref[...] = (_sigmoid(gc_ref[...]) * ma_ref[...] + _sigmoid(gh_ref[...]) * mb_ref[...]).astype(BF16)

    half = pl.BlockSpec((tr, w), lambda i, j: (i, j))
    return pl.pallas_call(
        body, name=name, grid=(t // tr, 2),
        in_specs=[pl.BlockSpec((tr, w), lambda i, j: (i, 7 + j)), pl.BlockSpec((tr, w), lambda i, j: (i, 9 + j)), half, half],
        out_specs=half, out_shape=SDS((t, 2 * w), BF16), compiler_params=_cparams("parallel", "parallel"),
    )(z, z, ma, mb)


def _merge_bwd(dmer, z, ma, mb, w, name):
    t = z.shape[0]
    tr = _tile(t, 256, 16)

    def body(d_ref, gc_ref, gh_ref, ma_ref, mb_ref, dma_ref, dmb_ref, dgc_ref, dgh_ref):
        dm = d_ref[...]
        sc = _sigmoid(gc_ref[...])
        sh = _sigmoid(gh_ref[...])
        dma_ref[...] = (dm * sc).astype(BF16)
        dmb_ref[...] = (dm * sh).astype(BF16)
        dgc_ref[...] = (dm * ma_ref[...] * sc * (1.0 - sc)).astype(BF16)
        dgh_ref[...] = (dm * mb_ref[...] * sh * (1.0 - sh)).astype(BF16)

    half = pl.BlockSpec((tr, w), lambda i, j: (i, j))
    return pl.pallas_call(
        body, name=name, grid=(t // tr, 2),
        in_specs=[half, pl.BlockSpec((tr, w), lambda i, j: (i, 7 + j)), pl.BlockSpec((tr, w), lambda i, j: (i, 9 + j)), half, half],
        out_specs=[half] * 4, out_shape=[SDS((t, 2 * w), BF16)] * 4, compiler_params=_cparams("parallel", "parallel"),
    )(dmer, z, z, ma, mb)


def _shift_down(x, s, row):
    return jnp.where(row >= s, pltpu.roll(x, s, axis=0), 0.0)


def _shift_up(x, s, row, t):
    return jnp.where(row < t - s, pltpu.roll(x, t - s, axis=0), 0.0)


def _conv_fwd(z, cw, w, name):
    t = z.shape[0]
    tc = LANES
    nb = w // tc

    def body(b_ref, c_ref, h_ref, w_ref, o_ref):
        u = c_ref[...] * h_ref[...]
        row = lax.broadcasted_iota(jnp.int32, u.shape, 0)
        cw_ = w_ref[...]
        conv = cw_[2:3, :] * u + cw_[1:2, :] * _shift_down(u, 1, row) + cw_[0:1, :] * _shift_down(u, 2, row)
        o_ref[...] = (b_ref[...] * conv).astype(BF16)

    col = lambda off: pl.BlockSpec((t, tc), lambda j: (0, off * nb + j))
    return pl.pallas_call(
        body, name=name, grid=(nb,), in_specs=[col(0), col(1), col(2), pl.BlockSpec((3, tc), lambda j: (0, j))],
        out_specs=pl.BlockSpec((t, tc), lambda j: (0, j)), out_shape=SDS((t, w), BF16), compiler_params=_cparams("parallel"),
    )(z, z, z, cw)


def _conv_bwd(dy, z, cw, w, name):
    t = z.shape[0]
    tc = LANES
    nb = w // tc

    def body(dy_ref, b_ref, c_ref, h_ref, w_ref, db_ref, dc_ref, dh_ref, dw_ref):
        c_, h_ = c_ref[...], h_ref[...]
        u = c_ * h_
        row = lax.broadcasted_iota(jnp.int32, u.shape, 0)
        cw_ = w_ref[...]
        u1 = _shift_down(u, 1, row)
        u2 = _shift_down(u, 2, row)
        dy_ = dy_ref[...]
        db_ref[...] = (dy_ * (cw_[2:3, :] * u + cw_[1:2, :] * u1 + cw_[0:1, :] * u2)).astype(BF16)
        dconv = dy_ * b_ref[...]
        du = cw_[2:3, :] * dconv + cw_[1:2, :] * _shift_up(dconv, 1, row, t) + cw_[0:1, :] * _shift_up(dconv, 2, row, t)
        dc_ref[...] = (du * h_).astype(BF16)
        dh_ref[...] = (du * c_).astype(BF16)
        dw_ref[0:1, :] = jnp.sum(dconv * u2, axis=0, keepdims=True)
        dw_ref[1:2, :] = jnp.sum(dconv * u1, axis=0, keepdims=True)
        dw_ref[2:3, :] = jnp.sum(dconv * u, axis=0, keepdims=True)

    col = lambda off: pl.BlockSpec((t, tc), lambda j: (0, off * nb + j))
    own = pl.BlockSpec((t, tc), lambda j: (0, j))
    wsp = pl.BlockSpec((3, tc), lambda j: (0, j))
    return pl.pallas_call(
        body, name=name, grid=(nb,), in_specs=[own, col(0), col(1), col(2), wsp], out_specs=[own, own, own, wsp],
        out_shape=[SDS((t, w), BF16)] * 3 + [SDS((3, w), F32)], compiler_params=_cparams("parallel"),
    )(dy, z, z, z, cw)


def _lower_bound(hg):
    mx = jnp.max(hg, axis=0, keepdims=True)
    e = jnp.exp(hg - mx)
    inv = 1.0 / jnp.sum(e, axis=0, keepdims=True)
    return e[0:1, :] * inv, e[1:2, :] * inv


def _chunk_cumsum(x, row):
    s = 1
    while s < CHUNK:
        x = x + jnp.where(row % CHUNK >= s, pltpu.roll(x, s, axis=0), 0.0)
        s *= 2
    return x


def _dot_nt(a, b):
    return lax.dot_general(a.astype(BF16), b.astype(BF16), (((1,), (1,)), ((), ())), preferred_element_type=F32)


def _dot_tn(a, b):
    return lax.dot_general(a.astype(BF16), b.astype(BF16), (((0,), (0,)), ((), ())), preferred_element_type=F32)


def _dot_nn(a, b):
    return jnp.dot(a.astype(BF16), b.astype(BF16), preferred_element_type=F32)


def _tril(x):
    r = lax.broadcasted_iota(jnp.int32, x.shape, 0)
    c = lax.broadcasted_iota(jnp.int32, x.shape, 1)
    return jnp.where(r >= c, x, 0.0)


HGRN_GROUP = 4
HGRN_ROWS = 512
HGRN_UNROLL = 2


def _unrolled_loop(n, step, init):
    assert n % HGRN_UNROLL == 0

    def trip(i, carry):
        for u in range(HGRN_UNROLL):
            carry = step(i * HGRN_UNROLL + u, carry)
        return carry

    return lax.fori_loop(0, n // HGRN_UNROLL, trip, init)


def _hgrn_chunk_inputs(q_ref, f_ref, cum_ref, lb, rows, ln):
    qr = q_ref[rows, ln]
    q = qr * _sigmoid(qr)
    f = lb + (1.0 - lb) * _sigmoid(f_ref[rows, ln])
    return q, 1.0 - f, cum_ref[rows, ln]


def _hgrn_fwd(z, hg, nw, w, name, comm=()):
    t = z.shape[0]
    nh = w // HEAD
    gh = _tile(nh, HGRN_GROUP, 1)
    gw = gh * HEAD
    ngrp = nh // gh
    tb = _tile(t, HGRN_ROWS, CHUNK)
    ncb = tb // CHUNK

    def body(q_ref, f_ref, i_ref, g_ref, hg_ref, nw_ref, y_ref, o_ref, st_ref, cum_ref, *s_refs):
        lb_all, _ = _lower_bound(hg_ref[...])
        row = lax.broadcasted_iota(jnp.int32, (tb, gw), 0)
        cum_ref[...] = _chunk_cumsum(jnp.log(lb_all + (1.0 - lb_all) * _sigmoid(f_ref[...])), row)

        @pl.when(pl.program_id(1) == 0)
        def _():
            for s_ref in s_refs:
                s_ref[...] = jnp.zeros_like(s_ref)

        def step(c, carry):
            rows = pl.ds(pl.multiple_of(c * CHUNK, CHUNK), CHUNK)
            for g in range(gh):
                ln = slice(g * HEAD, (g + 1) * HEAD)
                lb = lb_all[:, ln]
                q, k, cum = _hgrn_chunk_inputs(q_ref, f_ref, cum_ref, lb, rows, ln)
                v = i_ref[rows, ln]
                last = cum[CHUNK - 1:CHUNK, :]
                qe = q * jnp.exp(cum)
                st = s_refs[g][...]
                st_ref[g, c] = st.astype(BF16)
                o_ref[rows, ln] = _dot_nt(qe, st) + _dot_nn(_tril(_dot_nt(qe, k * jnp.exp(-cum))), v)
                s_refs[g][...] = st * jnp.exp(last) + _dot_tn(v, k * jnp.exp(last - cum))
            return carry

        _unrolled_loop(ncb, step, 0)
        for g in range(gh):
            ln = slice(g * HEAD, (g + 1) * HEAD)
            o = o_ref[:, ln]
            n = o * lax.rsqrt(jnp.mean(o * o, axis=-1, keepdims=True) + RMS_EPS)
            gr = g_ref[:, ln]
            y_ref[:, ln] = (n * nw_ref[...] * gr * _sigmoid(gr)).astype(BF16)

    col = lambda off: pl.BlockSpec((tb, gw), lambda h, j: (j, off * ngrp + h))
    own = pl.BlockSpec((tb, gw), lambda h, j: (j, h))
    return _hosted_call(
        body, comm, name=name, grid=(ngrp, t // tb),
        in_specs=[col(3), col(4), col(5), col(6), pl.BlockSpec((2, gw), lambda h, j: (0, h)),
                  pl.BlockSpec((1, HEAD), lambda h, j: (0, 0))],
        out_specs=[own, own, pl.BlockSpec((gh, ncb, HEAD, HEAD), lambda h, j: (h, j, 0, 0))],
        out_shape=[SDS((t, w), BF16), SDS((t, w), F32), SDS((nh, t // CHUNK, HEAD, HEAD), BF16)],
        scratch_shapes=[pltpu.VMEM((tb, gw), F32)] + [pltpu.VMEM((HEAD, HEAD), F32)] * gh,
        operands=(z, z, z, z, hg, nw), parallel=1)


def _hgrn_bwd(dy, z, o, states, hg, nw, w, name, comm=()):
    t = z.shape[0]
    nh = w // HEAD
    gh = _tile(nh, HGRN_GROUP, 1)
    gw = gh * HEAD
    ngrp = nh // gh
    tb = _tile(t, HGRN_ROWS, CHUNK)
    ncb = tb // CHUNK
    nt = t // tb

    def body(dy_ref, q_ref, f_ref, i_ref, g_ref, o_ref, st_ref, hg_ref, nw_ref,
             dq_ref, df_ref, di_ref, dg_ref, dhg_ref, dnw_ref, cum_ref, do_ref, *ds_refs):
        lb_all, s1_all = _lower_bound(hg_ref[...])
        row = lax.broadcasted_iota(jnp.int32, (tb, gw), 0)
        crow = lax.broadcasted_iota(jnp.int32, (CHUNK, HEAD), 0)
        cum_ref[...] = _chunk_cumsum(jnp.log(lb_all + (1.0 - lb_all) * _sigmoid(f_ref[...])), row)

        @pl.when(pl.program_id(1) == 0)
        def _():
            for ds_ref in ds_refs:
                ds_ref[...] = jnp.zeros_like(ds_ref)
            dhg_ref[...] = jnp.zeros_like(dhg_ref)
            dnw_ref[...] = jnp.zeros_like(dnw_ref)

        for g in range(gh):
            ln = slice(g * HEAD, (g + 1) * HEAD)
            o_ = o_ref[:, ln]
            rstd = lax.rsqrt(jnp.mean(o_ * o_, axis=-1, keepdims=True) + RMS_EPS)
            n = o_ * rstd
            gr = g_ref[:, ln]
            sg = _sigmoid(gr)
            dy_ = dy_ref[:, ln]
            dg_ref[:, ln] = (dy_ * n * nw_ref[...] * (sg * (1.0 + gr * (1.0 - sg)))).astype(BF16)
            dsil = dy_ * gr * sg
            dnw_ref[:, ln] += jnp.sum(dsil * n, axis=0, keepdims=True)
            dn = dsil * nw_ref[...]
            do_ref[:, ln] = rstd * (dn - n * jnp.mean(dn * n, axis=-1, keepdims=True))

        def step(cc, dlbs):
            c = ncb - 1 - cc
            rows = pl.ds(pl.multiple_of(c * CHUNK, CHUNK), CHUNK)
            new = []
            for g in range(gh):
                ln = slice(g * HEAD, (g + 1) * HEAD)
                lb = lb_all[:, ln]
                qr = q_ref[rows, ln]
                sq = _sigmoid(qr)
                q = qr * sq
                sf = _sigmoid(f_ref[rows, ln])
                f = lb + (1.0 - lb) * sf
                k = 1.0 - f
                cum = cum_ref[rows, ln]
                v = i_ref[rows, ln]
                do = do_ref[rows, ln]
                last = cum[CHUNK - 1:CHUNK, :]
                eg = jnp.exp(cum)
                eng = jnp.exp(-cum)
                elc = jnp.exp(last - cum)
                qe, ke, kl = q * eg, k * eng, k * elc
                ds = ds_refs[g][...]
                a = _tril(_dot_nt(qe, ke))
                da = _tril(_dot_nt(do, v))
                di_ref[rows, ln] = (_dot_tn(a, do) + _dot_nt(kl, ds)).astype(BF16)
                st = st_ref[g, c]
                dkl = _dot_nn(v, ds)
                dq = (_dot_nn(do, st) + _dot_nn(da, ke)) * eg
                dk = _dot_tn(da, qe) * eng + dkl * elc
                el = jnp.exp(last)
                ds_refs[g][...] = ds * el + _dot_tn(do, qe)
                dlast = jnp.sum(kl * dkl, axis=0, keepdims=True) + el * jnp.sum(ds * st.astype(F32), axis=0, keepdims=True)
                x = q * dq - k * dk + jnp.where(crow == CHUNK - 1, dlast, 0.0)
                s = 1
                while s < CHUNK:
                    x = x + _shift_up(x, s, crow, CHUNK)
                    s *= 2
                df = x / f - dk
                dq_ref[rows, ln] = (dq * (sq * (1.0 + qr * (1.0 - sq)))).astype(BF16)
                df_ref[rows, ln] = (df * (1.0 - lb) * sf * (1.0 - sf)).astype(BF16)
                new.append(dlbs[g] + jnp.sum(df * (1.0 - sf), axis=0, keepdims=True))
            return tuple(new)

        dlbs = _unrolled_loop(ncb, step, tuple(jnp.zeros((1, HEAD), F32) for _ in range(gh)))
        for g in range(gh):
            ln = slice(g * HEAD, (g + 1) * HEAD)
            dlb = dlbs[g] * lb_all[:, ln] * s1_all[:, ln]
            dhg_ref[0:1, ln] += dlb
            dhg_ref[1:2, ln] -= dlb

    col = lambda off: pl.BlockSpec((tb, gw), lambda h, j: (nt - 1 - j, off * ngrp + h))
    own = pl.BlockSpec((tb, gw), lambda h, j: (nt - 1 - j, h))
    hsp = pl.BlockSpec((2, gw), lambda h, j: (0, h))
    return _hosted_call(
        body, comm, name=name, grid=(ngrp, nt),
        in_specs=[own, col(3), col(4), col(5), col(6), own,
                  pl.BlockSpec((gh, ncb, HEAD, HEAD), lambda h, j: (h, nt - 1 - j, 0, 0)),
                  hsp, pl.BlockSpec((1, HEAD), lambda h, j: (0, 0))],
        out_specs=[own, own, own, own, hsp, pl.BlockSpec((1, gw), lambda h, j: (0, h))],
        out_shape=[SDS((t, w), BF16)] * 4 + [SDS((2, w), F32), SDS((1, w), F32)],
        scratch_shapes=[pltpu.VMEM((tb, gw), F32)] * 2 + [pltpu.VMEM((HEAD, HEAD), F32)] * gh,
        operands=(dy, z, z, z, z, o, states, hg, nw), parallel=1)


def _cast_pad(wt, n_pad, meta, sp, name):
    _, r, n = wt.shape
    g, p, per = meta
    tr = _tile(r, max(16, (3 << 19) // n_pad // 16 * 16), 16)

    def body(sp_ref, w_ref, o_ref):
        if n_pad != n:
            o_ref[...] = jnp.zeros(o_ref.shape, o_ref.dtype)
        o_ref[:, 0:n] = w_ref[...].astype(BF16)

    grid_spec = pltpu.PrefetchScalarGridSpec(
        num_scalar_prefetch=1, grid=(r // tr,), in_specs=[pl.BlockSpec((None, tr, n), lambda i, sp: (0, i, 0))],
        out_specs=pl.BlockSpec((None, tr, n_pad), lambda i, sp: (sp[1] // per, ((sp[1] % per) * r) // tr + i, 0)))
    return pl.pallas_call(body, name=name, grid_spec=grid_spec, out_shape=SDS((g, p, n_pad), BF16),
                          compiler_params=_cparams("parallel"))(sp, wt)


def _cast_pad_t(wt_t, n_pad, meta, sp, name):
    _, n, r = wt_t.shape
    g, p, per = meta
    tc = _tile(r, 256, LANES)

    def body(sp_ref, w_ref, o_ref):
        for lo in range(0, n_pad, LANES):
            rows = min(LANES, n - lo)
            piece = w_ref[lo:lo + rows, :]
            if rows < LANES:
                piece = jnp.concatenate([piece, jnp.zeros((LANES - rows, tc), F32)], axis=0)
            o_ref[:, lo:lo + LANES] = piece.T.astype(BF16)

    grid_spec = pltpu.PrefetchScalarGridSpec(
        num_scalar_prefetch=1, grid=(r // tc,), in_specs=[pl.BlockSpec((None, n, tc), lambda i, sp: (0, 0, i))],
        out_specs=pl.BlockSpec((None, tc, n_pad), lambda i, sp: (sp[1] // per, ((sp[1] % per) * r) // tc + i, 0)))
    return pl.pallas_call(body, name=name, grid_spec=grid_spec, out_shape=SDS((g, p, n_pad), BF16),
                          compiler_params=_cparams("parallel"))(sp, wt_t)


def _adam_math(w, g, m, v):
    m2 = ADAM_B1 * m + (1.0 - ADAM_B1) * g
    v2 = ADAM_B2 * v + (1.0 - ADAM_B2) * (g * g)
    c1 = 1.0 / (1.0 - ADAM_B1 ** ADAM_STEP)
    c2 = 1.0 / (1.0 - ADAM_B2 ** ADAM_STEP)
    return -ADAM_LR * ((m2 * c1) / (jnp.sqrt(v2 * c2) + ADAM_EPS) + ADAM_WD * w), m2, v2


def _adamw_t(wt_t, g, m_t, v_t, name):
    _, n, r = wt_t.shape
    ng = g.shape[1]
    tc = LANES

    def body(w_ref, g_ref, m_ref, v_ref, go_ref, d_ref, mo_ref, vo_ref, gt_ref):
        for lo in range(0, ng, LANES):
            gt_ref[lo:lo + LANES, :] = g_ref[:, lo:lo + LANES].T
        g_ = gt_ref[0:n, :]
        delta, m2, v2 = _adam_math(w_ref[...], g_, m_ref[...], v_ref[...])
        go_ref[...] = g_
        d_ref[...] = delta
        mo_ref[...] = m2
        vo_ref[...] = v2

    blk = pl.BlockSpec((None, n, tc), lambda i: (0, 0, i))
    return pl.pallas_call(
        body, name=name, grid=(r // tc,), in_specs=[blk, pl.BlockSpec((tc, ng), lambda i: (i, 0)), blk, blk],
        out_specs=[blk] * 4, out_shape=[SDS(wt_t.shape, F32)] * 4, scratch_shapes=[pltpu.VMEM((ng, tc), F32)],
        compiler_params=_cparams("parallel"),
    )(wt_t, g, m_t, v_t)


def _adamw(wt, g, m, v, name):
    lead = (None,) * (wt.ndim - 2)
    zero = (0,) * (wt.ndim - 2)
    r, n = wt.shape[-2:]
    ng = g.shape[1]
    nct = 2 if ng == n and n % (2 * LANES) == 0 else 1
    tc, tg = n // nct, ng // nct
    tr = _tile(r, max(8, (3 << 17) // tg // 8 * 8), 8)

    def body(w_ref, g_ref, m_ref, v_ref, go_ref, d_ref, mo_ref, vo_ref):
        g_ = g_ref[:, 0:tc]
        delta, m2, v2 = _adam_math(w_ref[...], g_, m_ref[...], v_ref[...])
        go_ref[...] = g_
        d_ref[...] = delta
        mo_ref[...] = m2
        vo_ref[...] = v2

    blk = pl.BlockSpec(lead + (tr, tc), lambda i, j: zero + (i, j))
    return pl.pallas_call(
        body, name=name, grid=(r // tr, nct), in_specs=[blk, pl.BlockSpec((tr, tg), lambda i, j: (i, j)), blk, blk],
        out_specs=[blk] * 4, out_shape=[SDS(wt.shape, F32)] * 4, compiler_params=_cparams("parallel", "parallel"),
    )(wt, g, m, v)


def _place():
    x, y, c = lax.axis_index("x"), lax.axis_index("y"), lax.axis_index("c")
    return x, y, c, 2 * x + y


def _chip_dev(k, c):
    return (k // 2, k % 2, c)


def _half(ref, j, h, rows, per):
    return ref.at[j // per, pl.ds((j % per) * rows + h * (rows // 2), rows // 2)]


def _gather_stage(bufs, metas, rows_of, parts, zero_pad):
    nw = len(bufs)
    pad_jobs = [(i, gi) for i in range(nw) if parts[i][0] == 0 and metas[i][1] > metas[i][2] * rows_of[i]
                for gi in range(metas[i][0])]

    def part_of(ref, i, j, h):
        per = metas[i][2]
        p, np_ = parts[i]
        pr = rows_of[i] // 2 // np_
        return ref.at[j // per, pl.ds((j % per) * rows_of[i] + h * (rows_of[i] // 2) + p * pr, pr)]

    def descriptors(ins, outs, sems):
        src, zp, dst = ins[:nw], ins[nw], outs
        pads, send, recv, fsend, frecv = sems
        x, y, c, me = _place()

        def pad(n):
            i, gi = pad_jobs[n]
            extra = metas[i][1] - metas[i][2] * rows_of[i]
            return pltpu.make_async_copy(zp.at[pl.ds(0, extra)], dst[i].at[gi, pl.ds(metas[i][2] * rows_of[i], extra)], pads.at[n])

        def ici(i, r, frm):
            return pltpu.make_async_remote_copy(
                src_ref=part_of(src[i], i, me, c), dst_ref=part_of(dst[i], i, frm, c), send_sem=send.at[i, r - 1],
                recv_sem=recv.at[i, r - 1], device_id=_chip_dev((me + r) % N_CHIPS, c), device_id_type=MESH)

        def d2d(i, r, frm, h):
            blk = part_of(dst[i], i, frm, h)
            return pltpu.make_async_remote_copy(src_ref=blk, dst_ref=blk, send_sem=fsend.at[i, r - 1],
                                                recv_sem=frecv.at[i, r - 1], device_id=(x, y, 1 - c), device_id_type=MESH)

        return pad, ici, d2d, c, me

    def start(ins, outs, sems):
        pad, ici, d2d, c, me = descriptors(ins, outs, sems)
        for n in range(len(pad_jobs)):
            pad(n).start()
        for i in range(nw):
            for r in range(1, N_CHIPS):
                ici(i, r, me).start()

    def finish(ins, outs, sems):
        pad, ici, d2d, c, me = descriptors(ins, outs, sems)
        for i in range(nw):
            for r in range(1, N_CHIPS):
                frm = (me - r) % N_CHIPS
                ici(i, r, frm).wait_recv()
                d2d(i, r, frm, c).start()
        for i in range(nw):
            for r in range(1, N_CHIPS):
                d2d(i, r, (me - r) % N_CHIPS, 1 - c).wait_recv()
        for i in range(nw):
            for r in range(1, N_CHIPS):
                ici(i, r, me).wait_send()
                d2d(i, r, (me - r) % N_CHIPS, c).wait_send()
        for n in range(len(pad_jobs)):
            pad(n).wait()

    return _Stage(ins=list(bufs) + [zero_pad], out_shapes=[SDS(b.shape, b.dtype) for b in bufs],
                  aliases={i: i for i in range(nw)},
                  sems=[pltpu.SemaphoreType.DMA((max(len(pad_jobs), 1),))] + [pltpu.SemaphoreType.DMA((nw, N_CHIPS - 1))] * 4,
                  start=start, finish=finish)


def _gather_small(packed, name):
    r, n = packed.shape

    def body(src, dst, send, recv):
        x, y, c, me = _place()
        dst[me] = src[...]
        cps = []
        for d in range(1, N_CHIPS):
            cp = pltpu.make_async_remote_copy(src_ref=src, dst_ref=dst.at[me], send_sem=send.at[d - 1], recv_sem=recv.at[d - 1],
                                              device_id=_chip_dev((me + d) % N_CHIPS, c), device_id_type=MESH)
            cp.start()
            cps.append(cp)
        for d in range(1, N_CHIPS):
            pltpu.make_async_remote_copy(src_ref=src, dst_ref=dst.at[(me - d) % N_CHIPS], send_sem=send.at[d - 1],
                                         recv_sem=recv.at[d - 1], device_id=_chip_dev((me + d) % N_CHIPS, c),
                                         device_id_type=MESH).wait_recv()
        for cp in cps:
            cp.wait_send()

    return pl.pallas_call(
        body, name=name, in_specs=[VMEM_SPEC], out_specs=VMEM_SPEC, out_shape=SDS((N_CHIPS, r, n), F32),
        scratch_shapes=[pltpu.SemaphoreType.DMA((N_CHIPS - 1,))] * 2,
    )(packed)


def _all_reduce_small(packed, name):
    r, n = packed.shape

    def body(src, out, slots, send, recv):
        x, y, c, me = _place()
        idx = 2 * me + c
        slots[idx] = src[...]
        cps = []

        def peer(d):
            p = (idx + d) % N_DEV
            return (p // 4, (p // 2) % 2, p % 2)

        for d in range(1, N_DEV):
            cp = pltpu.make_async_remote_copy(src_ref=src, dst_ref=slots.at[idx], send_sem=send.at[d - 1], recv_sem=recv.at[d - 1],
                                              device_id=peer(d), device_id_type=MESH)
            cp.start()
            cps.append(cp)
        for d in range(1, N_DEV):
            pltpu.make_async_remote_copy(src_ref=src, dst_ref=slots.at[(idx - d) % N_DEV], send_sem=send.at[d - 1],
                                         recv_sem=recv.at[d - 1], device_id=peer(d), device_id_type=MESH).wait_recv()
        for cp in cps:
            cp.wait_send()
        acc = slots[0]
        for k in range(1, N_DEV):
            acc = acc + slots[k]
        out[...] = acc

    return pl.pallas_call(
        body, name=name, in_specs=[VMEM_SPEC], out_specs=VMEM_SPEC, out_shape=SDS((r, n), F32),
        scratch_shapes=[pltpu.VMEM((N_DEV, r, n), F32)] + [pltpu.SemaphoreType.DMA((N_DEV - 1,))] * 2,
    )(packed)


def _simple_stage(ins, out_shapes, aliases, n_copies, copies):
    def start(ins_, outs, sems):
        for cp in copies(ins_, outs, *sems):
            cp.start()

    def finish(ins_, outs, sems):
        for cp in copies(ins_, outs, *sems):
            cp.wait()

    return _Stage(ins=list(ins), out_shapes=list(out_shapes), aliases=aliases,
                  sems=[pltpu.SemaphoreType.DMA((n_copies,))] * 2, start=start, finish=finish)


def _rs_pair_exchange(grads, metas, rows_of):
    nw = len(grads)

    def copies(src, dst, send, recv):
        x, y, c, me = _place()
        return [pltpu.make_async_remote_copy(
            src_ref=_half(src[i], j, 1 - c, rows_of[i], metas[i][2]), dst_ref=dst[i].at[j], send_sem=send.at[i * N_CHIPS + j],
            recv_sem=recv.at[i * N_CHIPS + j], device_id=(x, y, 1 - c), device_id_type=MESH)
            for i in range(nw) for j in range(N_CHIPS)]

    out_shapes = [SDS((N_CHIPS, rows_of[i] // 2, g.shape[2]), F32) for i, g in enumerate(grads)]
    return _simple_stage(grads, out_shapes, {}, nw * N_CHIPS, copies)


def _rs_pair_add(g, got, meta, rows, sp, name):
    per = meta[2]
    n = g.shape[2]
    hr = rows // 2
    tr = _tile(hr, max(16, (3 << 19) // n // 16 * 16), 16)

    def body(sp_ref, g_ref, got_ref, snd_ref, own_ref):
        j = pl.program_id(1)
        s = g_ref[...] + got_ref[...]
        snd_ref[...] = s.astype(BF16)

        @pl.when(j == sp_ref[1])
        def _():
            own_ref[...] = s

    grid_spec = pltpu.PrefetchScalarGridSpec(
        num_scalar_prefetch=1, grid=(hr // tr, N_CHIPS),
        in_specs=[pl.BlockSpec((None, tr, n), lambda i, j, sp: (j // per, ((j % per) * rows + sp[0] * hr) // tr + i, 0)),
                  pl.BlockSpec((None, tr, n), lambda i, j, sp: (j, i, 0))],
        out_specs=[pl.BlockSpec((None, tr, n), lambda i, j, sp: (j, i, 0)), pl.BlockSpec((tr, n), lambda i, j, sp: (i, 0))])
    return pl.pallas_call(
        body, name=name, grid_spec=grid_spec, out_shape=[SDS((N_CHIPS, hr, n), BF16), SDS((hr, n), F32)],
        compiler_params=_cparams("parallel", "arbitrary"),
    )(sp, g, got)


def _rs_chip_exchange(sends, part=(0, 1), prev=None):
    nw = len(sends)
    p, np_ = part

    def copies(src, dst, send, recv):
        x, y, c, me = _place()
        cps = []
        for i in range(nw):
            pr = sends[i].shape[1] // np_
            for r in range(1, N_CHIPS):
                cps.append(pltpu.make_async_remote_copy(
                    src_ref=src[i].at[(me + r) % N_CHIPS, pl.ds(p * pr, pr)], dst_ref=dst[i].at[r - 1, pl.ds(p * pr, pr)],
                    send_sem=send.at[i * (N_CHIPS - 1) + r - 1], recv_sem=recv.at[i * (N_CHIPS - 1) + r - 1],
                    device_id=_chip_dev((me + r) % N_CHIPS, c), device_id_type=MESH))
        return cps

    out_shapes = [SDS((N_CHIPS - 1,) + s.shape[1:], BF16) for s in sends]
    if prev is None:
        return _simple_stage(sends, out_shapes, {}, nw * (N_CHIPS - 1), copies)
    return _simple_stage(list(sends) + list(prev), out_shapes, {nw + i: i for i in range(nw)}, nw * (N_CHIPS - 1), copies)


def _rs_chip_add(own, got, sp, name):
    hr, n = own.shape
    tr = _tile(hr, max(16, (3 << 19) // n // 16 * 16), 16)

    def body(sp_ref, own_ref, got_ref, o_ref):
        acc = own_ref[...]
        for r in range(N_CHIPS - 1):
            acc = acc + got_ref[r].astype(F32)
        o_ref[...] = acc

    grid_spec = pltpu.PrefetchScalarGridSpec(
        num_scalar_prefetch=1, grid=(hr // tr,),
        in_specs=[pl.BlockSpec((tr, n), lambda i, sp: (i, 0)), pl.BlockSpec((N_CHIPS - 1, tr, n), lambda i, sp: (0, i, 0))],
        out_specs=pl.BlockSpec((tr, n), lambda i, sp: (sp[0] * (hr // tr) + i, 0)))
    return pl.pallas_call(body, name=name, grid_spec=grid_spec, out_shape=SDS((2 * hr, n), F32),
                          compiler_params=_cparams("parallel"))(sp, own, got)


def _rs_pair_share(blocks):
    nw = len(blocks)

    def copies(src, dst, send, recv):
        x, y, c, me = _place()
        cps = []
        for i in range(nw):
            hr = src[i].shape[0] // 2
            cps.append(pltpu.make_async_remote_copy(
                src_ref=src[i].at[pl.ds(c * hr, hr)], dst_ref=dst[i].at[pl.ds(c * hr, hr)], send_sem=send.at[i],
                recv_sem=recv.at[i], device_id=(x, y, 1 - c), device_id_type=MESH))
        return cps

    return _simple_stage(blocks, [SDS(b.shape, b.dtype) for b in blocks], {i: i for i in range(nw)}, nw, copies)


def kernel(x, p, ln_g, ln_b, ffn1_w_in, ffn1_w_out, mix_w_in, conv_w, hg_lower_bound, hg_norm_w, branch_w_conv, branch_w_hgrn, mix_w_out, ffn2_w_in, ffn2_w_out, ple_w_gate, ple_w_proj, loss_target, m_ln_g, m_ln_b, m_ffn1_w_in, m_ffn1_w_out, m_mix_w_in, m_conv_w, m_hg_lower_bound, m_hg_norm_w, m_branch_w_conv, m_branch_w_hgrn, m_mix_w_out, m_ffn2_w_in, m_ffn2_w_out, m_ple_w_gate, m_ple_w_proj, v_ln_g, v_ln_b, v_ffn1_w_in, v_ffn1_w_out, v_mix_w_in, v_conv_w, v_hg_lower_bound, v_hg_norm_w, v_branch_w_conv, v_branch_w_hgrn, v_mix_w_out, v_ffn2_w_in, v_ffn2_w_out, v_ple_w_gate, v_ple_w_proj):
    assert ln_g.shape[0] == DEPTH and x.shape[0] == 1 and p.shape[:2] == (1, 1)
    t, d = x.shape[1], x.shape[2]
    w = d // 2
    x0 = x.reshape(t, d)
    x0b = _to_bf16(x0, "x_bf16")
    pe = p.reshape(t, p.shape[-1])
    target = loss_target.reshape(t, d)
    cx, cy, cc = lax.axis_index("x"), lax.axis_index("y"), lax.axis_index("c")
    chip = 2 * cx + cy
    sp = jnp.stack([cc, chip]).astype(jnp.int32)

    big = dict(ffn1_w_in=ffn1_w_in, ffn1_w_out=ffn1_w_out, mix_w_in=mix_w_in, branch_w_conv=branch_w_conv,
               branch_w_hgrn=branch_w_hgrn, mix_w_out=mix_w_out, ffn2_w_in=ffn2_w_in, ffn2_w_out=ffn2_w_out,
               ple_w_gate=ple_w_gate, ple_w_proj=ple_w_proj)
    moments = dict(ffn1_w_in=(m_ffn1_w_in, v_ffn1_w_in), ffn1_w_out=(m_ffn1_w_out, v_ffn1_w_out), mix_w_in=(m_mix_w_in, v_mix_w_in),
                   branch_w_conv=(m_branch_w_conv, v_branch_w_conv), branch_w_hgrn=(m_branch_w_hgrn, v_branch_w_hgrn),
                   mix_w_out=(m_mix_w_out, v_mix_w_out), ffn2_w_in=(m_ffn2_w_in, v_ffn2_w_in), ffn2_w_out=(m_ffn2_w_out, v_ffn2_w_out),
                   ple_w_gate=(m_ple_w_gate, v_ple_w_gate), ple_w_proj=(m_ple_w_proj, v_ple_w_proj))
    names = list(big)

    n_loc = ffn1_w_in.shape[-1]
    n_pad = -(-n_loc // LANES) * LANES
    assert mix_w_in.shape[-1] % LANES == 0 and ffn1_w_out.shape[1] * 2 == n_loc
    pad_cols = dict(ffn1_w_in=n_pad, ffn2_w_in=n_pad)
    meta = {k: (N_CHIPS, big[k].shape[1], 1) for k in names}
    meta["ffn1_w_out"] = meta["ffn2_w_out"] = (2, n_pad, 2)
    rows = {k: big[k].shape[1] for k in names}
    swap = lambda a: jnp.transpose(a, (0, 2, 1))
    wbuf = {k: (_cast_pad_t(swap(big[k]), pad_cols[k], meta[k], sp, "cast_" + k) if k in pad_cols else
                _cast_pad(big[k], big[k].shape[2], meta[k], sp, "cast_" + k)) for k in names}
    zero_pad = jnp.zeros((max(n_pad - n_loc, 16), d), BF16)

    def gather(*items):
        ks = [k for k, _, _ in items]
        return _gather_stage([wbuf[k] for k in ks], [meta[k] for k in ks], [rows[k] for k in ks], [(p_, n_) for _, p_, n_ in items],
                             zero_pad), ks

    def gathered(ks, outs):
        wbuf.update(zip(ks, outs))

    def w3(k):
        return wbuf[k]

    def w2(k):
        return wbuf[k].reshape(-1, wbuf[k].shape[2])

    dq, wq = d // N_CHIPS, w // N_CHIPS
    small = jnp.concatenate([ln_g[0], ln_b[0], jnp.pad(conv_w[0], ((0, 5), (0, dq - wq)))], axis=0)
    small = _gather_small(small, "gather_small")
    lng = small[:, 0:4, :].transpose(1, 0, 2).reshape(4, 1, d)
    lnb = small[:, 4:8, :].transpose(1, 0, 2).reshape(4, 1, d)
    cw = small[:, 8:11, :wq].transpose(1, 0, 2).reshape(3, w)
    hg = hg_lower_bound
    nw_ = hg_norm_w

    st, ks = gather(("ffn1_w_in", 0, 1))
    gathered(ks, _run_stages([st], "gather_first")[0])
    st, ks = gather(("ffn1_w_out", 0, 1), ("mix_w_in", 0, 2))
    z1, got = _mm(x0b, w3("ffn1_w_in"), name="ffn1_in", b_blocked=True, out_dtype=BF16, comm=[st])
    gathered(ks, got)
    h1 = _swiglu_fwd(z1, "ffn1_act")
    st, ks = gather(("mix_w_in", 1, 2))
    y1, got = _mm(h1, w2("ffn1_w_out"), name="ffn1_out", tm=1024, tn=1024, tk=2816, comm=[st])
    gathered(ks, got)
    r1, x1, x1b = _ln_fwd(x0, y1, lng[0], lnb[0], 0.5, "ln0")
    st, ks = gather(("branch_w_conv", 0, 1), ("branch_w_hgrn", 0, 1), ("mix_w_out", 0, 1), ("ffn2_w_in", 0, 2))
    z, got = _mm(x1b, w3("mix_w_in"), name="mix_in", b_blocked=True, comm=[st])
    gathered(ks, got)
    ya = _conv_fwd(z, cw, w, "conv_fwd")
    st, ks = gather(("ffn2_w_in", 1, 2))
    (yb, o_h, states), got = _hgrn_fwd(z, hg, nw_, w, "hgrn_fwd", comm=[st])
    gathered(ks, got)
    ma = _mm(ya, w3("branch_w_conv"), name="branch_conv", b_blocked=True, tn=512)
    mb = _mm(yb, w3("branch_w_hgrn"), name="branch_hgrn", b_blocked=True, tn=512)
    merged = _merge_fwd(z, ma, mb, w, "merge_fwd")
    y2 = _mm(merged, w2("mix_w_out"), name="mix_out", tn=1024)
    r2, x2, x2b = _ln_fwd(x1, y2, lng[1], lnb[1], 1.0, "ln1")
    st, ks = gather(("ffn2_w_out", 0, 1), ("ple_w_gate", 0, 1), ("ple_w_proj", 0, 1))
    z3, got = _mm(x2b, w3("ffn2_w_in"), name="ffn2_in", b_blocked=True, out_dtype=BF16, comm=[st])
    gathered(ks, got)
    h3 = _swiglu_fwd(z3, "ffn2_act")
    y3 = _mm(h3, w2("ffn2_w_out"), name="ffn2_out", tm=1024, tn=1024, tk=2816)
    r3, x3, x3b = _ln_fwd(x2, y3, lng[2], lnb[2], 0.5, "ln2")
    gp = _mm(x3b, w2("ple_w_gate"), name="ple_gate", tn=1024)
    pp = _mm(pe, w3("ple_w_proj"), name="ple_proj", b_blocked=True, tn=512)
    dr4, dgp, dpp, dg3, db3, sq = _tail(x3, gp, pp, lng[3], lnb[3], target, "tail")

    grads, sends, owns, blocks, outs = {}, {}, {}, {}, {}

    def pair_exchange(*ks):
        return _rs_pair_exchange([grads[k] for k in ks], [meta[k] for k in ks], [rows[k] for k in ks])

    def pair_add(ks, got):
        for k, g_ in zip(ks, got):
            sends[k], owns[k] = _rs_pair_add(grads[k], g_, meta[k], rows[k], sp, "rs_pair_add_" + k)

    def chip_exchange(*ks):
        return _rs_chip_exchange([sends[k] for k in ks])

    def chip_add(ks, got):
        for k, g_ in zip(ks, got):
            blocks[k] = _rs_chip_add(owns[k], g_, sp, "rs_chip_add_" + k)

    def pair_share(*ks):
        return _rs_pair_share([blocks[k] for k in ks])

    def update(ks, full):
        for k, g_ in zip(ks, full):
            m_, v_ = moments[k]
            if k in pad_cols:
                outs[k] = [swap(a) for a in _adamw_t(swap(big[k]), g_, swap(m_), swap(v_), "adamw_" + k)]
            else:
                outs[k] = _adamw(big[k], g_, m_, v_, "adamw_" + k)

    ple = ("ple_w_gate", "ple_w_proj")
    mixo = ("mix_w_out", "branch_w_conv", "branch_w_hgrn")
    dx3m = _mm(dgp, w2("ple_w_gate"), name="d_ple_gate_x", tb=True, tn=1024, tk=2048)
    grads["ple_w_gate"] = _mm(x3b, dgp, name="d_ple_gate_w", ta=True, tm=1024, tk=2048, tn=1024).reshape(N_CHIPS, -1, d)
    grads["ple_w_proj"] = _mm(pe, dpp, name="d_ple_proj_w", ta=True, out_blocked=N_CHIPS, tk=2048, tn=512)
    dr3, dy3b, dg2, db2 = _ln_bwd(dr4, dx3m, r3, lng[2], 0.5, "ln2_bwd")
    dh3, got = _mm(dy3b, w2("ffn2_w_out"), name="d_ffn2_out_x", tb=True, out_dtype=BF16, tn=1408, tk=2048,
                   comm=[pair_exchange(*ple)])
    pair_add(ple, got)
    g_, got = _mm(h3, dy3b, name="d_ffn2_out_w", ta=True, tm=1408, tk=2048, tn=1024, comm=[chip_exchange(*ple)])
    grads["ffn2_w_out"] = g_.reshape(2, n_pad, d)
    chip_add(ple, got)
    dz3 = _swiglu_bwd(dh3, z3, "ffn2_act_bwd")
    dx2m, got, full = _mm(dz3, w3("ffn2_w_in"), name="d_ffn2_in_x", tb=True, b_blocked=True, tm=1024, tn=1024, tk=2816,
                          comm=[pair_exchange("ffn2_w_out"), pair_share(*ple)])
    pair_add(["ffn2_w_out"], got)
    update(ple, full)
    grads["ffn2_w_in"], got = _mm(x2b, dz3, name="d_ffn2_in_w", ta=True, out_blocked=N_CHIPS, tk=4096, comm=[chip_exchange("ffn2_w_out")])
    chip_add(["ffn2_w_out"], got)
    dr2, dy2b, dg1, db1 = _ln_bwd(dr3, dx2m, r2, lng[1], 1.0, "ln1_bwd")
    dmer, got = _mm(dy2b, w2("mix_w_out"), name="d_mix_out_x", tb=True, tn=1024, tk=2048, comm=[pair_exchange("ffn2_w_in")])
    pair_add(["ffn2_w_in"], got)
    g_, full = _mm(merged, dy2b, name="d_mix_out_w", ta=True, tm=1024, tk=2048, tn=1024, comm=[pair_share("ffn2_w_out")])
    grads["mix_w_out"] = g_.reshape(N_CHIPS, -1, d)
    update(["ffn2_w_out"], full)
    dma, dmb, dgc, dgh = _merge_bwd(dmer, z, ma, mb, w, "merge_bwd")
    dya = _mm(dma, w3("branch_w_conv"), name="d_branch_conv_x", tb=True, b_blocked=True, tn=1024, tk=512)
    dyb = _mm(dmb, w3("branch_w_hgrn"), name="d_branch_hgrn_x", tb=True, b_blocked=True, tn=1024, tk=512)
    grads["branch_w_conv"] = _mm(ya, dma, name="d_branch_conv_w", ta=True, out_blocked=N_CHIPS, tm=1024, tk=2048, tn=512)
    grads["branch_w_hgrn"] = _mm(yb, dmb, name="d_branch_hgrn_w", ta=True, out_blocked=N_CHIPS, tm=1024, tk=2048, tn=512)
    dbg, dcg, dhc, dcw = _conv_bwd(dya, z, cw, w, "conv_bwd")
    (dq_, df_, di_, dgr_, dhg, dnw), got2, got = _hgrn_bwd(dyb, z, o_h, states, hg, nw_, w, "hgrn_bwd",
                                                            comm=[chip_exchange("ffn2_w_in"), pair_exchange(*mixo)])
    chip_add(["ffn2_w_in"], got2)
    pair_add(mixo, got)
    dz = _concat_cols([dbg, dcg, dhc, dq_, df_, di_, dgr_, dgc, dgh], "dz_concat")
    dx1m, full, got = _mm(dz, w3("mix_w_in"), name="d_mix_in_x", tb=True, b_blocked=True, tm=1024, tn=1024, tk=2816,
                          comm=[pair_share("ffn2_w_in"), chip_exchange(*mixo)])
    update(["ffn2_w_in"], full)
    chip_add(mixo, got)
    grads["mix_w_in"], full = _mm(x1b, dz, name="d_mix_in_w", ta=True, out_blocked=N_CHIPS, tk=4096, comm=[pair_share(*mixo)])
    update(mixo, full)
    dr1, dy1b, dg0, db0 = _ln_bwd(dr2, dx1m, r1, lng[0], 0.5, "ln0_bwd")
    dh1, got = _mm(dy1b, w2("ffn1_w_out"), name="d_ffn1_out_x", tb=True, out_dtype=BF16, tn=1408, tk=2048,
                   comm=[pair_exchange("mix_w_in")])
    pair_add(["mix_w_in"], got)
    mix_sends = [sends["mix_w_in"]]
    g_, got_a = _mm(h1, dy1b, name="d_ffn1_out_w", ta=True, tm=1408, tk=2048, tn=1024, comm=[_rs_chip_exchange(mix_sends, (0, 2))])
    grads["ffn1_w_out"] = g_.reshape(2, n_pad, d)
    dz1 = _swiglu_bwd(dh1, z1, "ffn1_act_bwd")
    g_other, got2, got = _mm(x0b, dz1, name="d_ffn1_in_w_other", ta=True, out_blocked=N_CHIPS, tk=4096, half=(sp, True),
                             comm=[_rs_chip_exchange(mix_sends, (1, 2), got_a), pair_exchange("ffn1_w_out")])
    chip_add(["mix_w_in"], got2)
    pair_add(["ffn1_w_out"], got)
    grads["ffn1_w_in"], full, got2, got = _mm(
        x0b, dz1, name="d_ffn1_in_w_own", ta=True, out_blocked=N_CHIPS, tk=4096, half=(sp, False),
        comm=[pair_share("mix_w_in"), chip_exchange("ffn1_w_out"),
              _rs_pair_exchange([g_other], [meta["ffn1_w_in"]], [rows["ffn1_w_in"]])])
    update(["mix_w_in"], full)
    chip_add(["ffn1_w_out"], got2)
    pair_add(["ffn1_w_in"], got)
    dx0m, got2, full = _mm(dz1, w3("ffn1_w_in"), name="d_ffn1_in_x", tb=True, b_blocked=True, tm=1024, tn=1024, tk=2816,
                           comm=[chip_exchange("ffn1_w_in"), pair_share("ffn1_w_out")])
    chip_add(["ffn1_w_in"], got2)
    update(["ffn1_w_out"], full)
    grad_x = _residual_out(dr1, dx0m, "grad_x").reshape(x.shape)
    update(["ffn1_w_in"], _run_stages([pair_share("ffn1_w_in")], "rs_tail_pair")[0])

    pack = jnp.concatenate([
        dg0, dg1, dg2, dg3, db0, db1, db2, db3,
        jnp.pad(dcw, ((0, 0), (0, d - w))), jnp.pad(dhg, ((0, 0), (0, d - w))),
        jnp.pad(jnp.sum(dnw.reshape(-1, HEAD), axis=0, keepdims=True), ((0, 0), (0, d - HEAD))), sq], axis=0)
    pack = _all_reduce_small(jnp.pad(pack, ((0, 1), (0, 0))), "reduce_small")
    loss = (0.5 / d) * jnp.sum(pack[14])
    g_ln_g = lax.dynamic_slice_in_dim(pack[0:4], chip * dq, dq, axis=1)
    g_ln_b = lax.dynamic_slice_in_dim(pack[4:8], chip * dq, dq, axis=1)
    g_conv = lax.dynamic_slice_in_dim(pack[8:11, :w], chip * wq, wq, axis=1)
    g_hg = pack[11:13, :w]
    g_nw = pack[13:14, :HEAD]

    small_w = dict(ln_g=(ln_g, g_ln_g, m_ln_g, v_ln_g), ln_b=(ln_b, g_ln_b, m_ln_b, v_ln_b),
                   conv_w=(conv_w, g_conv, m_conv_w, v_conv_w), hg_lower_bound=(hg_lower_bound, g_hg, m_hg_lower_bound, v_hg_lower_bound),
                   hg_norm_w=(hg_norm_w, g_nw, m_hg_norm_w, v_hg_norm_w))
    for k, (w_, g_, m_, v_) in small_w.items():
        outs[k] = _adamw(w_, g_.reshape(-1, w_.shape[-1]), m_, v_, "adamw_" + k)

    order = ["ln_g", "ln_b", "ffn1_w_in", "ffn1_w_out", "mix_w_in", "conv_w", "hg_lower_bound", "hg_norm_w", "branch_w_conv",
             "branch_w_hgrn", "mix_w_out", "ffn2_w_in", "ffn2_w_out", "ple_w_gate", "ple_w_proj"]
    return (loss, grad_x, *[outs[k][0] for k in order], *[outs[k][1] for k in order], *[outs[k][2] for k in order],
            *[outs[k][3] for k in order])
```

```python
import collections
import functools

import jax
import jax.numpy as jnp
from jax import lax
from jax.experimental import pallas as pl
from jax.experimental.pallas import tpu as pltpu

F32 = jnp.float32
BF16 = jnp.bfloat16
MESH = pl.DeviceIdType.MESH
ANY = pl.BlockSpec(memory_space=pl.ANY)
VMEM_SPEC = pl.BlockSpec(memory_space=pltpu.VMEM)
SDS = jax.ShapeDtypeStruct

DEPTH = 1
ALPHA = (2.0 * DEPTH) ** 0.25
LN_EPS = 1e-5
RMS_EPS = 1e-6
CHUNK = 32
HEAD = 128
ADAM_LR, ADAM_B1, ADAM_B2, ADAM_EPS, ADAM_WD, ADAM_STEP = 0.001, 0.9, 0.999, 1e-08, 0.01, 10

LANES = 128
N_CHIPS = 4
N_DEV = 8
FIRST_GATHER_PARTS = 8
VMEM_LIMIT = 52 * 1024 * 1024


def _cparams(*sem):
    if sem:
        return pltpu.CompilerParams(dimension_semantics=sem, vmem_limit_bytes=VMEM_LIMIT)
    return pltpu.CompilerParams(vmem_limit_bytes=VMEM_LIMIT)


def _tile(n, target, mult):
    best = None
    for t in range(mult, min(n, target) + 1, mult):
        if n % t == 0:
            best = t
    return best if best is not None else n


def _sigmoid(x):
    return 1.0 / (1.0 + jnp.exp(-x))


_Stage = collections.namedtuple("_Stage", "ins out_shapes aliases sems start finish")


def _hosted_call(compute, stages, *, name, grid, in_specs, out_specs, out_shape, scratch_shapes, operands, parallel,
                 prefetch=None):
    n_cmp, n_out, n_scr = len(in_specs), len(out_specs), len(scratch_shapes)
    n_in = n_cmp
    n_pre = int(prefetch is not None)
    c_in = [len(s.ins) for s in stages]
    c_out = [len(s.out_shapes) for s in stages]
    c_sem = [len(s.sems) for s in stages]
    aliases = {}
    for si, s in enumerate(stages):
        for a_in, a_out in s.aliases.items():
            aliases[n_pre + n_in + sum(c_in[:si]) + a_in] = n_out + sum(c_out[:si]) + a_out

    def body(*refs):
        refs = refs[n_pre:]
        ins = refs[:n_cmp]
        cins = refs[n_in:n_in + sum(c_in)]
        outs = refs[n_in + sum(c_in):n_in + sum(c_in) + n_out]
        couts = refs[n_in + sum(c_in) + n_out:n_in + sum(c_in) + n_out + sum(c_out)]
        scr = refs[n_in + sum(c_in) + n_out + sum(c_out):][:n_scr]
        sems = refs[n_in + sum(c_in) + n_out + sum(c_out) + n_scr:]

        def stage_refs(si):
            return (cins[sum(c_in[:si]):sum(c_in[:si + 1])], couts[sum(c_out[:si]):sum(c_out[:si + 1])],
                    sems[sum(c_sem[:si]):sum(c_sem[:si + 1])])

        if stages:
            first = functools.reduce(jnp.logical_and, [pl.program_id(ax) == 0 for ax in range(len(grid))])
            last = functools.reduce(jnp.logical_and, [pl.program_id(ax) == grid[ax] - 1 for ax in range(len(grid))])

            @pl.when(first)
            def _():
                for si, s in enumerate(stages):
                    s.start(*stage_refs(si))

        compute(*ins, *outs, *scr)
        if stages:
            @pl.when(last)
            def _():
                for si, s in enumerate(stages):
                    s.finish(*stage_refs(si))

    sem = ("arbitrary",) * len(grid) if stages else ("parallel",) * parallel + ("arbitrary",) * (len(grid) - parallel)
    all_in = list(in_specs) + [ANY] * (n_in - n_cmp + sum(c_in))
    all_out = list(out_specs) + [ANY] * sum(c_out)
    all_scr = list(scratch_shapes) + [q for s in stages for q in s.sems]
    all_shape = list(out_shape) + [o for s in stages for o in s.out_shapes]
    args = list(operands) + [a for s in stages for a in s.ins]
    if prefetch is None:
        res = pl.pallas_call(body, name=name, grid=grid, in_specs=all_in, out_specs=all_out, out_shape=all_shape,
                             input_output_aliases=aliases, scratch_shapes=all_scr, compiler_params=_cparams(*sem))(*args)
    else:
        grid_spec = pltpu.PrefetchScalarGridSpec(num_scalar_prefetch=1, grid=grid, in_specs=all_in, out_specs=all_out,
                                                 scratch_shapes=all_scr)
        res = pl.pallas_call(body, name=name, grid_spec=grid_spec, out_shape=all_shape, input_output_aliases=aliases,
                             compiler_params=_cparams(*sem))(prefetch, *args)
    main = res[0] if n_out == 1 else list(res[:n_out])
    if not stages:
        return main
    rest = res[n_out:]
    return (main, *[list(rest[sum(c_out[:si]):sum(c_out[:si + 1])]) for si in range(len(stages))])


def _run_stages(stages, name):
    def body(*refs):
        n_i = sum(len(s.ins) for s in stages)
        n_o = sum(len(s.out_shapes) for s in stages)
        cins, couts, sems = refs[:n_i], refs[n_i:n_i + n_o], refs[n_i + n_o:]
        pos = [0, 0, 0]
        parts = []
        for s in stages:
            parts.append((cins[pos[0]:pos[0] + len(s.ins)], couts[pos[1]:pos[1] + len(s.out_shapes)], sems[pos[2]:pos[2] + len(s.sems)]))
            pos = [pos[0] + len(s.ins), pos[1] + len(s.out_shapes), pos[2] + len(s.sems)]
        for s, p_ in zip(stages, parts):
            s.start(*p_)
        for s, p_ in zip(stages, parts):
            s.finish(*p_)

    aliases, ni, no = {}, 0, 0
    for s in stages:
        for a_in, a_out in s.aliases.items():
            aliases[ni + a_in] = no + a_out
        ni, no = ni + len(s.ins), no + len(s.out_shapes)
    res = pl.pallas_call(
        body, name=name, in_specs=[ANY] * ni, out_specs=[ANY] * no, out_shape=[o for s in stages for o in s.out_shapes],
        input_output_aliases=aliases, scratch_shapes=[q for s in stages for q in s.sems],
    )(*[a for s in stages for a in s.ins])
    out, pos = [], 0
    for s in stages:
        out.append(list(res[pos:pos + len(s.out_shapes)]))
        pos += len(s.out_shapes)
    return out


def _mm(a, b, *, name, ta=False, tb=False, b_blocked=False, out_blocked=0, out_dtype=F32,
        tm=512, tn=1408, tk=2048, comm=(), half=None, add=None):
    if ta:
        kd, m = a.shape
    else:
        m, kd = a.shape
    if b_blocked and not tb:
        g, kb, nb = b.shape
        assert kb == kd
        n = g * nb
        tn = _tile(nb, tn, LANES)
        tk = _tile(kd, tk, LANES)
        per_n = nb // tn
        b_spec = pl.BlockSpec((None, tk, tn), lambda i, j, k, *s: (j // per_n, k, j % per_n))
    elif b_blocked and tb:
        g, n, kb = b.shape
        assert g * kb == kd
        tn = _tile(n, tn, LANES)
        tk = _tile(kb, tk, LANES)
        per_k = kb // tk
        b_spec = pl.BlockSpec((None, tn, tk), lambda i, j, k, *s: (k // per_k, j, k % per_k))
    elif tb:
        n, kb = b.shape
        assert kb == kd
        tn = _tile(n, tn, LANES)
        tk = _tile(kd, tk, LANES)
        b_spec = pl.BlockSpec((tn, tk), lambda i, j, k, *s: (j, k))
    else:
        kb, n = b.shape
        assert kb == kd
        tn = _tile(n // out_blocked if out_blocked else n, tn, LANES)
        per_o = (n // out_blocked) // tn if out_blocked else None
        tk = _tile(kd, tk, LANES)
        b_spec = pl.BlockSpec((tk, tn), lambda i, j, k, *s: (k, j))
    m_run = m // 2 if half else m
    tm = _tile(m_run, tm, LANES if ta else 8)

    def row(i, s):
        if not half:
            return i
        h = 1 - s[0][0] if half[1] else s[0][0]
        return h * (m_run // tm) + i

    if ta:
        a_spec = pl.BlockSpec((tk, tm), lambda i, j, k, *s: (k, row(i, s)))
    else:
        a_spec = pl.BlockSpec((tm, tk), lambda i, j, k, *s: (row(i, s), k))
    if out_blocked:
        assert not b_blocked and not tb
        o_spec = pl.BlockSpec((None, tm, tn), lambda i, j, k, *s: (j // per_o, row(i, s), j % per_o))
        o_shape = SDS((out_blocked, m, n // out_blocked), out_dtype)
    else:
        o_spec = pl.BlockSpec((tm, tn), lambda i, j, k, *s: (row(i, s), j))
        o_shape = SDS((m, n), out_dtype)
    nk = kd // tk
    dn = (((0 if ta else 1,), (1 if tb else 0,)), ((), ()))
    grid = (m_run // tm, n // tn, nk)

    def compute(a_ref, b_ref, *rest):
        add_ref = rest[0] if add else None
        o_ref, acc_ref = rest[-2:]
        part = lax.dot_general(a_ref[...].astype(BF16), b_ref[...].astype(BF16), dn, preferred_element_type=F32)

        def store(acc):
            if add:
                acc = acc + add[1] * add_ref[...]
            o_ref[...] = acc.astype(o_ref.dtype)

        if nk == 1:
            store(part)
        else:
            k = pl.program_id(2)

            @pl.when(k == 0)
            def _():
                acc_ref[...] = part

            @pl.when(k > 0)
            def _():
                acc_ref[...] += part

            @pl.when(k == nk - 1)
            def _():
                store(acc_ref[...])

    extra = [(add[0], o_spec)] if add else []
    return _hosted_call(compute, comm, name=name, grid=grid, in_specs=[a_spec, b_spec] + [s_ for _, s_ in extra], out_specs=[o_spec],
                        out_shape=[o_shape], scratch_shapes=[pltpu.VMEM((tm, tn), F32)], operands=(a, b, *[a_ for a_, _ in extra]),
                        parallel=2, prefetch=half[0] if half else None)


def _swiglu_fwd(z, name):
    t, n = z.shape
    n2 = n // 2
    tr = _tile(t, 128, 16)

    def body(a_ref, u_ref, o_ref):
        a = a_ref[...].astype(F32)
        o_ref[...] = (a * _sigmoid(a) * u_ref[...].astype(F32)).astype(o_ref.dtype)

    return pl.pallas_call(
        body, name=name, grid=(t // tr,),
        in_specs=[pl.BlockSpec((tr, n2), lambda i: (i, 0)), pl.BlockSpec((tr, n2), lambda i: (i, 1))],
        out_specs=pl.BlockSpec((tr, n2), lambda i: (i, 0)), out_shape=SDS((t, n2), BF16),
        compiler_params=_cparams("parallel"),
    )(z, z)


def _swiglu_bwd(dh, z, name):
    t, n = z.shape
    n2 = n // 2
    tr = _tile(t, 128, 16)

    def body(dh_ref, a_ref, u_ref, o_ref):
        a = a_ref[...].astype(F32)
        dh_ = dh_ref[...].astype(F32)
        s = _sigmoid(a)
        o_ref[:, 0:n2] = (dh_ * u_ref[...].astype(F32) * (s * (1.0 + a * (1.0 - s)))).astype(o_ref.dtype)
        o_ref[:, n2:n] = (dh_ * a * s).astype(o_ref.dtype)

    return pl.pallas_call(
        body, name=name, grid=(t // tr,),
        in_specs=[pl.BlockSpec((tr, n2), lambda i: (i, 0)), pl.BlockSpec((tr, n2), lambda i: (i, 0)),
                  pl.BlockSpec((tr, n2), lambda i: (i, 1))],
        out_specs=pl.BlockSpec((tr, n), lambda i: (i, 0)), out_shape=SDS((t, n), BF16),
        compiler_params=_cparams("parallel"),
    )(dh, z, z)


def _ln_stats(r):
    mu = jnp.mean(r, axis=-1, keepdims=True)
    xc = r - mu
    var = jnp.mean(xc * xc, axis=-1, keepdims=True)
    return xc * lax.rsqrt(var + LN_EPS)


def _ln_fwd(xp, y, g, b, scale, name):
    t, d = xp.shape
    tr = _tile(t, 256, 16)

    def body(xp_ref, y_ref, g_ref, b_ref, r_ref, x_ref, xb_ref):
        r = ALPHA * xp_ref[...] + scale * y_ref[...]
        x = _ln_stats(r) * g_ref[...] + b_ref[...]
        r_ref[...] = r
        x_ref[...] = x
        xb_ref[...] = x.astype(BF16)

    row = pl.BlockSpec((tr, d), lambda i: (i, 0))
    vec = pl.BlockSpec((1, d), lambda i: (0, 0))
    return pl.pallas_call(
        body, name=name, grid=(t // tr,), in_specs=[row, row, vec, vec], out_specs=[row, row, row],
        out_shape=[SDS((t, d), F32), SDS((t, d), F32), SDS((t, d), BF16)], compiler_params=_cparams("parallel"),
    )(xp, y, g, b)


def _ln_bwd(dra, dxm, r, g, scale, name):
    t, d = r.shape
    tr = _tile(t, 256, 16)

    def body(dra_ref, dxm_ref, r_ref, g_ref, dr_ref, dyb_ref, dg_ref, db_ref):
        i = pl.program_id(0)
        dx = ALPHA * dra_ref[...] + dxm_ref[...]
        rr = r_ref[...]
        mu = jnp.mean(rr, axis=-1, keepdims=True)
        xc = rr - mu
        rstd = lax.rsqrt(jnp.mean(xc * xc, axis=-1, keepdims=True) + LN_EPS)
        xh = xc * rstd
        dxh = dx * g_ref[...]
        dr = rstd * (dxh - jnp.mean(dxh, axis=-1, keepdims=True) - xh * jnp.mean(dxh * xh, axis=-1, keepdims=True))
        dr_ref[...] = dr
        dyb_ref[...] = (scale * dr).astype(BF16)
        dg = jnp.sum(dx * xh, axis=0, keepdims=True)
        db = jnp.sum(dx, axis=0, keepdims=True)

        @pl.when(i == 0)
        def _():
            dg_ref[...] = dg
            db_ref[...] = db

        @pl.when(i > 0)
        def _():
            dg_ref[...] += dg
            db_ref[...] += db

    row = pl.BlockSpec((tr, d), lambda i: (i, 0))
    vec = pl.BlockSpec((1, d), lambda i: (0, 0))
    return pl.pallas_call(
        body, name=name, grid=(t // tr,), in_specs=[row, row, row, vec], out_specs=[row, row, vec, vec],
        out_shape=[SDS((t, d), F32), SDS((t, d), BF16), SDS((1, d), F32), SDS((1, d), F32)],
        compiler_params=_cparams("arbitrary"),
    )(dra, dxm, r, g)


def _tail(x3, gp, pp, g, b, target, name):
    t, d = x3.shape
    tr = _tile(t, 256, 16)

    def body(x3_ref, gp_ref, pp_ref, g_ref, b_ref, tg_ref, dr_ref, dgp_ref, dpp_ref, dg_ref, db_ref, sq_ref):
        i = pl.program_id(0)
        gate = _sigmoid(gp_ref[...])
        pp_ = pp_ref[...]
        r = ALPHA * x3_ref[...] + gate * pp_
        mu = jnp.mean(r, axis=-1, keepdims=True)
        xc = r - mu
        rstd = lax.rsqrt(jnp.mean(xc * xc, axis=-1, keepdims=True) + LN_EPS)
        xh = xc * rstd
        err = xh * g_ref[...] + b_ref[...] - tg_ref[...]
        dx = err * (1.0 / d)
        dxh = dx * g_ref[...]
        dr = rstd * (dxh - jnp.mean(dxh, axis=-1, keepdims=True) - xh * jnp.mean(dxh * xh, axis=-1, keepdims=True))
        dr_ref[...] = dr
        dgp_ref[...] = (dr * pp_ * gate * (1.0 - gate)).astype(BF16)
        dpp_ref[...] = (dr * gate).astype(BF16)
        dg = jnp.sum(dx * xh, axis=0, keepdims=True)
        db = jnp.sum(dx, axis=0, keepdims=True)
        sq = jnp.sum(err * err, axis=0, keepdims=True)

        @pl.when(i == 0)
        def _():
            dg_ref[...] = dg
            db_ref[...] = db
            sq_ref[...] = sq

        @pl.when(i > 0)
        def _():
            dg_ref[...] += dg
            db_ref[...] += db
            sq_ref[...] += sq

    row = pl.BlockSpec((tr, d), lambda i: (i, 0))
    vec = pl.BlockSpec((1, d), lambda i: (0, 0))
    return pl.pallas_call(
        body, name=name, grid=(t // tr,), in_specs=[row, row, row, vec, vec, row],
        out_specs=[row, row, row, vec, vec, vec],
        out_shape=[SDS((t, d), F32), SDS((t, d), BF16), SDS((t, d), BF16), SDS((1, d), F32), SDS((1, d), F32),
                   SDS((1, d), F32)],
        compiler_params=_cparams("arbitrary"),
    )(x3, gp, pp, g, b, target)


def _to_bf16(x, name):
    t, d = x.shape
    tr = _tile(t, 512, 16)
    row = pl.BlockSpec((tr, d), lambda i: (i, 0))

    def body(x_ref, o_ref):
        o_ref[...] = x_ref[...].astype(BF16)

    return pl.pallas_call(body, name=name, grid=(t // tr,), in_specs=[row], out_specs=row, out_shape=SDS((t, d), BF16),
                          compiler_params=_cparams("parallel"))(x)


def _concat_cols(parts, name):
    t = parts[0].shape[0]
    widths = [p_.shape[1] for p_ in parts]
    tr = _tile(t, 256, 16)

    def body(*refs):
        o_ref = refs[-1]
        at = 0
        for ref, wd in zip(refs[:-1], widths):
            o_ref[:, at:at + wd] = ref[...]
            at += wd

    return pl.pallas_call(
        body, name=name, grid=(t // tr,), in_specs=[pl.BlockSpec((tr, wd), lambda i: (i, 0)) for wd in widths],
        out_specs=pl.BlockSpec((tr, sum(widths)), lambda i: (i, 0)), out_shape=SDS((t, sum(widths)), parts[0].dtype),
        compiler_params=_cparams("parallel"),
    )(*parts)


def _merge_fwd(z, ma, mb, w, name):
    t = z.shape[0]
    tr = _tile(t, 256, 16)

    def body(gc_ref, gh_ref, ma_ref, mb_ref, o_ref):
        o_ref[...] = (_sigmoid(gc_ref[...]) * ma_ref[...] + _sigmoid(gh_ref[...]) * mb_ref[...]).astype(BF16)

    half = pl.BlockSpec((tr, w), lambda i, j: (i, j))
    return pl.pallas_call(
        body, name=name, grid=(t // tr, 2),
        in_specs=[pl.BlockSpec((tr, w), lambda i, j: (i, 7 + j)), pl.BlockSpec((tr, w), lambda i, j: (i, 9 + j)), half, half],
        out_specs=half, out_shape=SDS((t, 2 * w), BF16), compiler_params=_cparams("parallel", "parallel"),
    )(z, z, ma, mb)


def _merge_bwd(dmer, z, ma, mb, w, name):
    t = z.shape[0]
    tr = _tile(t, 256, 16)

    def body(d_ref, gc_ref, gh_ref, ma_ref, mb_ref, dma_ref, dmb_ref, dgc_ref, dgh_ref):
        dm = d_ref[...]
        sc = _sigmoid(gc_ref[...])
        sh = _sigmoid(gh_ref[...])
        dma_ref[...] = (dm * sc).astype(BF16)
        dmb_ref[...] = (dm * sh).astype(BF16)
        dgc_ref[...] = (dm * ma_ref[...] * sc * (1.0 - sc)).astype(BF16)
        dgh_ref[...] = (dm * mb_ref[...] * sh * (1.0 - sh)).astype(BF16)

    half = pl.BlockSpec((tr, w), lambda i, j: (i, j))
    return pl.pallas_call(
        body, name=name, grid=(t // tr, 2),
        in_specs=[half, pl.BlockSpec((tr, w), lambda i, j: (i, 7 + j)), pl.BlockSpec((tr, w), lambda i, j: (i, 9 + j)), half, half],
        out_specs=[half] * 4, out_shape=[SDS((t, 2 * w), BF16)] * 4, compiler_params=_cparams("parallel", "parallel"),
    )(dmer, z, z, ma, mb)


def _shift_down(x, s, row):
    return jnp.where(row >= s, pltpu.roll(x, s, axis=0), 0.0)


def _shift_up(x, s, row, t):
    return jnp.where(row < t - s, pltpu.roll(x, t - s, axis=0), 0.0)


def _conv_fwd(z, cw, w, name):
    t = z.shape[0]
    tc = LANES
    nb = w // tc

    def body(b_ref, c_ref, h_ref, w_ref, o_ref):
        u = c_ref[...] * h_ref[...]
        row = lax.broadcasted_iota(jnp.int32, u.shape, 0)
        cw_ = w_ref[...]
        conv = cw_[2:3, :] * u + cw_[1:2, :] * _shift_down(u, 1, row) + cw_[0:1, :] * _shift_down(u, 2, row)
        o_ref[...] = (b_ref[...] * conv).astype(BF16)

    col = lambda off: pl.BlockSpec((t, tc), lambda j: (0, off * nb + j))
    return pl.pallas_call(
        body, name=name, grid=(nb,), in_specs=[col(0), col(1), col(2), pl.BlockSpec((3, tc), lambda j: (0, j))],
        out_specs=pl.BlockSpec((t, tc), lambda j: (0, j)), out_shape=SDS((t, w), BF16), compiler_params=_cparams("parallel"),
    )(z, z, z, cw)


def _conv_bwd(dy, z, cw, w, name):
    t = z.shape[0]
    tc = LANES
    nb = w // tc

    def body(dy_ref, b_ref, c_ref, h_ref, w_ref, db_ref, dc_ref, dh_ref, dw_ref):
        c_, h_ = c_ref[...], h_ref[...]
        u = c_ * h_
        row = lax.broadcasted_iota(jnp.int32, u.shape, 0)
        cw_ = w_ref[...]
        u1 = _shift_down(u, 1, row)
        u2 = _shift_down(u, 2, row)
        dy_ = dy_ref[...]
        db_ref[...] = (dy_ * (cw_[2:3, :] * u + cw_[1:2, :] * u1 + cw_[0:1, :] * u2)).astype(BF16)
        dconv = dy_ * b_ref[...]
        du = cw_[2:3, :] * dconv + cw_[1:2, :] * _shift_up(dconv, 1, row, t) + cw_[0:1, :] * _shift_up(dconv, 2, row, t)
        dc_ref[...] = (du * h_).astype(BF16)
        dh_ref[...] = (du * c_).astype(BF16)
        dw_ref[0:1, :] = jnp.sum(dconv * u2, axis=0, keepdims=True)
        dw_ref[1:2, :] = jnp.sum(dconv * u1, axis=0, keepdims=True)
        dw_ref[2:3, :] = jnp.sum(dconv * u, axis=0, keepdims=True)

    col = lambda off: pl.BlockSpec((t, tc), lambda j: (0, off * nb + j))
    own = pl.BlockSpec((t, tc), lambda j: (0, j))
    wsp = pl.BlockSpec((3, tc), lambda j: (0, j))
    return pl.pallas_call(
        body, name=name, grid=(nb,), in_specs=[own, col(0), col(1), col(2), wsp], out_specs=[own, own, own, wsp],
        out_shape=[SDS((t, w), BF16)] * 3 + [SDS((3, w), F32)], compiler_params=_cparams("parallel"),
    )(dy, z, z, z, cw)


def _lower_bound(hg):
    mx = jnp.max(hg, axis=0, keepdims=True)
    e = jnp.exp(hg - mx)
    inv = 1.0 / jnp.sum(e, axis=0, keepdims=True)
    return e[0:1, :] * inv, e[1:2, :] * inv


def _chunk_cumsum(x, row):
    s = 1
    while s < CHUNK:
        x = x + jnp.where(row % CHUNK >= s, pltpu.roll(x, s, axis=0), 0.0)
        s *= 2
    return x


def _dot_nt(a, b):
    return lax.dot_general(a.astype(BF16), b.astype(BF16), (((1,), (1,)), ((), ())), preferred_element_type=F32)


def _dot_tn(a, b):
    return lax.dot_general(a.astype(BF16), b.astype(BF16), (((0,), (0,)), ((), ())), preferred_element_type=F32)


def _dot_nn(a, b):
    return jnp.dot(a.astype(BF16), b.astype(BF16), preferred_element_type=F32)


def _tril(x):
    r = lax.broadcasted_iota(jnp.int32, x.shape, 0)
    c = lax.broadcasted_iota(jnp.int32, x.shape, 1)
    return jnp.where(r >= c, x, 0.0)


HGRN_GROUP = 4
HGRN_ROWS = 512
HGRN_UNROLL = 2


def _unrolled_loop(n, step, init):
    assert n % HGRN_UNROLL == 0

    def trip(i, carry):
        for u in range(HGRN_UNROLL):
            carry = step(i * HGRN_UNROLL + u, carry)
        return carry

    return lax.fori_loop(0, n // HGRN_UNROLL, trip, init)


def _hgrn_chunk_inputs(q_ref, f_ref, cum_ref, lb, rows, ln):
    qr = q_ref[rows, ln]
    q = qr * _sigmoid(qr)
    f = lb + (1.0 - lb) * _sigmoid(f_ref[rows, ln])
    return q, 1.0 - f, cum_ref[rows, ln]


def _hgrn_fwd(z, hg, nw, w, name, comm=()):
    t = z.shape[0]
    nh = w // HEAD
    gh = _tile(nh, HGRN_GROUP, 1)
    gw = gh * HEAD
    ngrp = nh // gh
    tb = _tile(t, HGRN_ROWS, CHUNK)
    ncb = tb // CHUNK

    def body(q_ref, f_ref, i_ref, g_ref, hg_ref, nw_ref, y_ref, o_ref, st_ref, cum_ref, *s_refs):
        lb_all, _ = _lower_bound(hg_ref[...])
        row = lax.broadcasted_iota(jnp.int32, (tb, gw), 0)
        cum_ref[...] = _chunk_cumsum(jnp.log(lb_all + (1.0 - lb_all) * _sigmoid(f_ref[...])), row)

        @pl.when(pl.program_id(1) == 0)
        def _():
            for s_ref in s_refs:
                s_ref[...] = jnp.zeros_like(s_ref)

        def step(c, carry):
            rows = pl.ds(pl.multiple_of(c * CHUNK, CHUNK), CHUNK)
            for g in range(gh):
                ln = slice(g * HEAD, (g + 1) * HEAD)
                lb = lb_all[:, ln]
                q, k, cum = _hgrn_chunk_inputs(q_ref, f_ref, cum_ref, lb, rows, ln)
                v = i_ref[rows, ln]
                last = cum[CHUNK - 1:CHUNK, :]
                qe = q * jnp.exp(cum)
                st = s_refs[g][...]
                st_ref[g, c] = st.astype(BF16)
                o_ref[rows, ln] = _dot_nt(qe, st) + _dot_nn(_tril(_dot_nt(qe, k * jnp.exp(-cum))), v)
                s_refs[g][...] = st * jnp.exp(last) + _dot_tn(v, k * jnp.exp(last - cum))
            return carry

        _unrolled_loop(ncb, step, 0)
        for g in range(gh):
            ln = slice(g * HEAD, (g + 1) * HEAD)
            o = o_ref[:, ln]
            n = o * lax.rsqrt(jnp.mean(o * o, axis=-1, keepdims=True) + RMS_EPS)
            gr = g_ref[:, ln]
            y_ref[:, ln] = (n * nw_ref[...] * gr * _sigmoid(gr)).astype(BF16)

    col = lambda off: pl.BlockSpec((tb, gw), lambda h, j: (j, off * ngrp + h))
    own = pl.BlockSpec((tb, gw), lambda h, j: (j, h))
    return _hosted_call(
        body, comm, name=name, grid=(ngrp, t // tb),
        in_specs=[col(3), col(4), col(5), col(6), pl.BlockSpec((2, gw), lambda h, j: (0, h)),
                  pl.BlockSpec((1, HEAD), lambda h, j: (0, 0))],
        out_specs=[own, own, pl.BlockSpec((gh, ncb, HEAD, HEAD), lambda h, j: (h, j, 0, 0))],
        out_shape=[SDS((t, w), BF16), SDS((t, w), F32), SDS((nh, t // CHUNK, HEAD, HEAD), BF16)],
        scratch_shapes=[pltpu.VMEM((tb, gw), F32)] + [pltpu.VMEM((HEAD, HEAD), F32)] * gh,
        operands=(z, z, z, z, hg, nw), parallel=1)


def _hgrn_bwd(dy, z, o, states, hg, nw, w, name, comm=()):
    t = z.shape[0]
    nh = w // HEAD
    gh = _tile(nh, HGRN_GROUP, 1)
    gw = gh * HEAD
    ngrp = nh // gh
    tb = _tile(t, HGRN_ROWS, CHUNK)
    ncb = tb // CHUNK
    nt = t // tb

    def body(dy_ref, q_ref, f_ref, i_ref, g_ref, o_ref, st_ref, hg_ref, nw_ref,
             dq_ref, df_ref, di_ref, dg_ref, dhg_ref, dnw_ref, cum_ref, do_ref, *ds_refs):
        lb_all, s1_all = _lower_bound(hg_ref[...])
        row = lax.broadcasted_iota(jnp.int32, (tb, gw), 0)
        crow = lax.broadcasted_iota(jnp.int32, (CHUNK, HEAD), 0)
        cum_ref[...] = _chunk_cumsum(jnp.log(lb_all + (1.0 - lb_all) * _sigmoid(f_ref[...])), row)

        @pl.when(pl.program_id(1) == 0)
        def _():
            for ds_ref in ds_refs:
                ds_ref[...] = jnp.zeros_like(ds_ref)
            dhg_ref[...] = jnp.zeros_like(dhg_ref)
            dnw_ref[...] = jnp.zeros_like(dnw_ref)

        for g in range(gh):
            ln = slice(g * HEAD, (g + 1) * HEAD)
            o_ = o_ref[:, ln]
            rstd = lax.rsqrt(jnp.mean(o_ * o_, axis=-1, keepdims=True) + RMS_EPS)
            n = o_ * rstd
            gr = g_ref[:, ln]
            sg = _sigmoid(gr)
            dy_ = dy_ref[:, ln]
            dg_ref[:, ln] = (dy_ * n * nw_ref[...] * (sg * (1.0 + gr * (1.0 - sg)))).astype(BF16)
            dsil = dy_ * gr * sg
            dnw_ref[:, ln] += jnp.sum(dsil * n, axis=0, keepdims=True)
            dn = dsil * nw_ref[...]
            do_ref[:, ln] = rstd * (dn - n * jnp.mean(dn * n, axis=-1, keepdims=True))

        def step(cc, dlbs):
            c = ncb - 1 - cc
            rows = pl.ds(pl.multiple_of(c * CHUNK, CHUNK), CHUNK)
            new = []
            for g in range(gh):
                ln = slice(g * HEAD, (g + 1) * HEAD)
                lb = lb_all[:, ln]
                qr = q_ref[rows, ln]
                sq = _sigmoid(qr)
                q = qr * sq
                sf = _sigmoid(f_ref[rows, ln])
                f = lb + (1.0 - lb) * sf
                k = 1.0 - f
                cum = cum_ref[rows, ln]
                v = i_ref[rows, ln]
                do = do_ref[rows, ln]
                last = cum[CHUNK - 1:CHUNK, :]
                eg = jnp.exp(cum)
                eng = jnp.exp(-cum)
                elc = jnp.exp(last - cum)
                qe, ke, kl = q * eg, k * eng, k * elc
                ds = ds_refs[g][...]
                a = _tril(_dot_nt(qe, ke))
                da = _tril(_dot_nt(do, v))
                di_ref[rows, ln] = (_dot_tn(a, do) + _dot_nt(kl, ds)).astype(BF16)
                st = st_ref[g, c]
                dkl = _dot_nn(v, ds)
                dq = (_dot_nn(do, st) + _dot_nn(da, ke)) * eg
                dk = _dot_tn(da, qe) * eng + dkl * elc
                el = jnp.exp(last)
                ds_refs[g][...] = ds * el + _dot_tn(do, qe)
                dlast = jnp.sum(kl * dkl, axis=0, keepdims=True) + el * jnp.sum(ds * st.astype(F32), axis=0, keepdims=True)
                x = q * dq - k * dk + jnp.where(crow == CHUNK - 1, dlast, 0.0)
                s = 1
                while s < CHUNK:
                    x = x + _shift_up(x, s, crow, CHUNK)
                    s *= 2
                df = x / f - dk
                dq_ref[rows, ln] = (dq * (sq * (1.0 + qr * (1.0 - sq)))).astype(BF16)
                df_ref[rows, ln] = (df * (1.0 - lb) * sf * (1.0 - sf)).astype(BF16)
                new.append(dlbs[g] + jnp.sum(df * (1.0 - sf), axis=0, keepdims=True))
            return tuple(new)

        dlbs = _unrolled_loop(ncb, step, tuple(jnp.zeros((1, HEAD), F32) for _ in range(gh)))
        for g in range(gh):
            ln = slice(g * HEAD, (g + 1) * HEAD)
            dlb = dlbs[g] * lb_all[:, ln] * s1_all[:, ln]
            dhg_ref[0:1, ln] += dlb
            dhg_ref[1:2, ln] -= dlb

    col = lambda off: pl.BlockSpec((tb, gw), lambda h, j: (nt - 1 - j, off * ngrp + h))
    own = pl.BlockSpec((tb, gw), lambda h, j: (nt - 1 - j, h))
    hsp = pl.BlockSpec((2, gw), lambda h, j: (0, h))
    return _hosted_call(
        body, comm, name=name, grid=(ngrp, nt),
        in_specs=[own, col(3), col(4), col(5), col(6), own,
                  pl.BlockSpec((gh, ncb, HEAD, HEAD), lambda h, j: (h, nt - 1 - j, 0, 0)),
                  hsp, pl.BlockSpec((1, HEAD), lambda h, j: (0, 0))],
        out_specs=[own, own, own, own, hsp, pl.BlockSpec((1, gw), lambda h, j: (0, h))],
        out_shape=[SDS((t, w), BF16)] * 4 + [SDS((2, w), F32), SDS((1, w), F32)],
        scratch_shapes=[pltpu.VMEM((tb, gw), F32)] * 2 + [pltpu.VMEM((HEAD, HEAD), F32)] * gh,
        operands=(dy, z, z, z, z, o, states, hg, nw), parallel=1)


def _cast_pad(wt, n_pad, meta, sp, name, comm=()):
    _, r, n = wt.shape
    g, p, per = meta
    tr = _tile(r, max(16, (3 << 19) // n_pad // 16 * 16), 16)

    def body(w_ref, o_ref):
        if n_pad != n:
            o_ref[...] = jnp.zeros(o_ref.shape, o_ref.dtype)
        o_ref[:, 0:n] = w_ref[...].astype(BF16)

    return _hosted_call(
        body, comm, name=name, grid=(r // tr,), in_specs=[pl.BlockSpec((None, tr, n), lambda i, sp: (0, i, 0))],
        out_specs=[pl.BlockSpec((None, tr, n_pad), lambda i, sp: (sp[1] // per, ((sp[1] % per) * r) // tr + i, 0))],
        out_shape=[SDS((g, p, n_pad), BF16)], scratch_shapes=[], operands=(wt,), parallel=1, prefetch=sp)


def _cast_pad_t(wt_t, n_pad, meta, sp, name, comm=()):
    _, n, r = wt_t.shape
    g, p, per = meta
    tc = _tile(r, 256, LANES)

    def body(w_ref, o_ref):
        for lo in range(0, n_pad, LANES):
            rows = min(LANES, n - lo)
            piece = w_ref[lo:lo + rows, :]
            if rows < LANES:
                piece = jnp.concatenate([piece, jnp.zeros((LANES - rows, tc), F32)], axis=0)
            o_ref[:, lo:lo + LANES] = piece.T.astype(BF16)

    return _hosted_call(
        body, comm, name=name, grid=(r // tc,), in_specs=[pl.BlockSpec((None, n, tc), lambda i, sp: (0, 0, i))],
        out_specs=[pl.BlockSpec((None, tc, n_pad), lambda i, sp: (sp[1] // per, ((sp[1] % per) * r) // tc + i, 0))],
        out_shape=[SDS((g, p, n_pad), BF16)], scratch_shapes=[], operands=(wt_t,), parallel=1, prefetch=sp)


def _adam_math(w, g, m, v):
    m2 = ADAM_B1 * m + (1.0 - ADAM_B1) * g
    v2 = ADAM_B2 * v + (1.0 - ADAM_B2) * (g * g)
    c1 = 1.0 / (1.0 - ADAM_B1 ** ADAM_STEP)
    c2 = 1.0 / (1.0 - ADAM_B2 ** ADAM_STEP)
    return -ADAM_LR * ((m2 * c1) / (jnp.sqrt(v2 * c2) + ADAM_EPS) + ADAM_WD * w), m2, v2


def _adamw_t(wt_t, g, m_t, v_t, name):
    _, n, r = wt_t.shape
    ng = g.shape[1]
    tc = LANES

    def body(w_ref, g_ref, m_ref, v_ref, go_ref, d_ref, mo_ref, vo_ref, gt_ref):
        for lo in range(0, ng, LANES):
            gt_ref[lo:lo + LANES, :] = g_ref[:, lo:lo + LANES].T
        g_ = gt_ref[0:n, :]
        delta, m2, v2 = _adam_math(w_ref[...], g_, m_ref[...], v_ref[...])
        go_ref[...] = g_
        d_ref[...] = delta
        mo_ref[...] = m2
        vo_ref[...] = v2

    blk = pl.BlockSpec((None, n, tc), lambda i: (0, 0, i))
    return pl.pallas_call(
        body, name=name, grid=(r // tc,), in_specs=[blk, pl.BlockSpec((tc, ng), lambda i: (i, 0)), blk, blk],
        out_specs=[blk] * 4, out_shape=[SDS(wt_t.shape, F32)] * 4, scratch_shapes=[pltpu.VMEM((ng, tc), F32)],
        compiler_params=_cparams("parallel"),
    )(wt_t, g, m_t, v_t)


def _adamw(wt, g, m, v, name):
    lead = (None,) * (wt.ndim - 2)
    zero = (0,) * (wt.ndim - 2)
    r, n = wt.shape[-2:]
    ng = g.shape[1]
    nct = 2 if ng == n and n % (2 * LANES) == 0 else 1
    tc, tg = n // nct, ng // nct
    tr = _tile(r, max(8, (3 << 17) // tg // 8 * 8), 8)

    def body(w_ref, g_ref, m_ref, v_ref, go_ref, d_ref, mo_ref, vo_ref):
        g_ = g_ref[:, 0:tc]
        delta, m2, v2 = _adam_math(w_ref[...], g_, m_ref[...], v_ref[...])
        go_ref[...] = g_
        d_ref[...] = delta
        mo_ref[...] = m2
        vo_ref[...] = v2

    blk = pl.BlockSpec(lead + (tr, tc), lambda i, j: zero + (i, j))
    return pl.pallas_call(
        body, name=name, grid=(r // tr, nct), in_specs=[blk, pl.BlockSpec((tr, tg), lambda i, j: (i, j)), blk, blk],
        out_specs=[blk] * 4, out_shape=[SDS(wt.shape, F32)] * 4, compiler_params=_cparams("parallel", "parallel"),
    )(wt, g, m, v)


def _place():
    x, y, c = lax.axis_index("x"), lax.axis_index("y"), lax.axis_index("c")
    return x, y, c, 2 * x + y


def _chip_dev(k, c):
    return (k // 2, k % 2, c)


def _half(ref, j, h, rows, per):
    return ref.at[j // per, pl.ds((j % per) * rows + h * (rows // 2), rows // 2)]


def _gather_stage(bufs, metas, rows_of, parts, zero_pad):
    nw = len(bufs)
    pad_jobs = [(i, gi) for i in range(nw) if parts[i][0] == 0 and metas[i][1] > metas[i][2] * rows_of[i]
                for gi in range(metas[i][0])]

    def part_of(ref, i, j, h):
        per = metas[i][2]
        p, np_ = parts[i]
        pr = rows_of[i] // 2 // np_
        return ref.at[j // per, pl.ds((j % per) * rows_of[i] + h * (rows_of[i] // 2) + p * pr, pr)]

    def descriptors(ins, outs, sems):
        src, zp, dst = ins[:nw], ins[nw], outs
        pads, send, recv, fsend, frecv = sems
        x, y, c, me = _place()

        def pad(n):
            i, gi = pad_jobs[n]
            extra = metas[i][1] - metas[i][2] * rows_of[i]
            return pltpu.make_async_copy(zp.at[pl.ds(0, extra)], dst[i].at[gi, pl.ds(metas[i][2] * rows_of[i], extra)], pads.at[n])

        def ici(i, r, frm):
            return pltpu.make_async_remote_copy(
                src_ref=part_of(src[i], i, me, c), dst_ref=part_of(dst[i], i, frm, c), send_sem=send.at[i, r - 1],
                recv_sem=recv.at[i, r - 1], device_id=_chip_dev((me + r) % N_CHIPS, c), device_id_type=MESH)

        def d2d(i, r, frm, h):
            blk = part_of(dst[i], i, frm, h)
            return pltpu.make_async_remote_copy(src_ref=blk, dst_ref=blk, send_sem=fsend.at[i, r - 1],
                                                recv_sem=frecv.at[i, r - 1], device_id=(x, y, 1 - c), device_id_type=MESH)

        return pad, ici, d2d, c, me

    def start(ins, outs, sems):
        pad, ici, d2d, c, me = descriptors(ins, outs, sems)
        for n in range(len(pad_jobs)):
            pad(n).start()
        for i in range(nw):
            for r in range(1, N_CHIPS):
                ici(i, r, me).start()

    def finish(ins, outs, sems):
        pad, ici, d2d, c, me = descriptors(ins, outs, sems)
        for i in range(nw):
            for r in range(1, N_CHIPS):
                frm = (me - r) % N_CHIPS
                ici(i, r, frm).wait_recv()
                d2d(i, r, frm, c).start()
        for i in range(nw):
            for r in range(1, N_CHIPS):
                d2d(i, r, (me - r) % N_CHIPS, 1 - c).wait_recv()
        for i in range(nw):
            for r in range(1, N_CHIPS):
                ici(i, r, me).wait_send()
                d2d(i, r, (me - r) % N_CHIPS, c).wait_send()
        for n in range(len(pad_jobs)):
            pad(n).wait()

    return _Stage(ins=list(bufs) + [zero_pad], out_shapes=[SDS(b.shape, b.dtype) for b in bufs],
                  aliases={i: i for i in range(nw)},
                  sems=[pltpu.SemaphoreType.DMA((max(len(pad_jobs), 1),))] + [pltpu.SemaphoreType.DMA((nw, N_CHIPS - 1))] * 4,
                  start=start, finish=finish)


def _gather_small(packed, name):
    r, n = packed.shape

    def body(src, dst, send, recv):
        x, y, c, me = _place()
        dst[me] = src[...]
        cps = []
        for d in range(1, N_CHIPS):
            cp = pltpu.make_async_remote_copy(src_ref=src, dst_ref=dst.at[me], send_sem=send.at[d - 1], recv_sem=recv.at[d - 1],
                                              device_id=_chip_dev((me + d) % N_CHIPS, c), device_id_type=MESH)
            cp.start()
            cps.append(cp)
        for d in range(1, N_CHIPS):
            pltpu.make_async_remote_copy(src_ref=src, dst_ref=dst.at[(me - d) % N_CHIPS], send_sem=send.at[d - 1],
                                         recv_sem=recv.at[d - 1], device_id=_chip_dev((me + d) % N_CHIPS, c),
                                         device_id_type=MESH).wait_recv()
        for cp in cps:
            cp.wait_send()

    return pl.pallas_call(
        body, name=name, in_specs=[VMEM_SPEC], out_specs=VMEM_SPEC, out_shape=SDS((N_CHIPS, r, n), F32),
        scratch_shapes=[pltpu.SemaphoreType.DMA((N_CHIPS - 1,))] * 2,
    )(packed)


def _all_reduce_small(packed, name):
    r, n = packed.shape

    def body(src, out, slots, send, recv):
        x, y, c, me = _place()
        idx = 2 * me + c
        slots[idx] = src[...]
        cps = []

        def peer(d):
            p = (idx + d) % N_DEV
            return (p // 4, (p // 2) % 2, p % 2)

        for d in range(1, N_DEV):
            cp = pltpu.make_async_remote_copy(src_ref=src, dst_ref=slots.at[idx], send_sem=send.at[d - 1], recv_sem=recv.at[d - 1],
                                              device_id=peer(d), device_id_type=MESH)
            cp.start()
            cps.append(cp)
        for d in range(1, N_DEV):
            pltpu.make_async_remote_copy(src_ref=src, dst_ref=slots.at[(idx - d) % N_DEV], send_sem=send.at[d - 1],
                                         recv_sem=recv.at[d - 1], device_id=peer(d), device_id_type=MESH).wait_recv()
        for cp in cps:
            cp.wait_send()
        acc = slots[0]
        for k in range(1, N_DEV):
            acc = acc + slots[k]
        out[...] = acc

    return pl.pallas_call(
        body, name=name, in_specs=[VMEM_SPEC], out_specs=VMEM_SPEC, out_shape=SDS((r, n), F32),
        scratch_shapes=[pltpu.VMEM((N_DEV, r, n), F32)] + [pltpu.SemaphoreType.DMA((N_DEV - 1,))] * 2,
    )(packed)


def _simple_stage(ins, out_shapes, aliases, n_copies, copies):
    def start(ins_, outs, sems):
        for cp in copies(ins_, outs, *sems):
            cp.start()

    def finish(ins_, outs, sems):
        for cp in copies(ins_, outs, *sems):
            cp.wait()

    return _Stage(ins=list(ins), out_shapes=list(out_shapes), aliases=aliases,
                  sems=[pltpu.SemaphoreType.DMA((n_copies,))] * 2, start=start, finish=finish)


def _rs_pair_exchange(grads, metas, rows_of):
    nw = len(grads)

    def copies(src, dst, send, recv):
        x, y, c, me = _place()
        return [pltpu.make_async_remote_copy(
            src_ref=_half(src[i], j, 1 - c, rows_of[i], metas[i][2]), dst_ref=dst[i].at[j], send_sem=send.at[i * N_CHIPS + j],
            recv_sem=recv.at[i * N_CHIPS + j], device_id=(x, y, 1 - c), device_id_type=MESH)
            for i in range(nw) for j in range(N_CHIPS)]

    out_shapes = [SDS((N_CHIPS, rows_of[i] // 2, g.shape[2]), g.dtype) for i, g in enumerate(grads)]
    return _simple_stage(grads, out_shapes, {}, nw * N_CHIPS, copies)


def _rs_pair_add(g, got, meta, rows, sp, name):
    per = meta[2]
    n = g.shape[2]
    hr = rows // 2
    tr = _tile(hr, max(16, (3 << 19) // n // 16 * 16), 16)

    def body(sp_ref, g_ref, got_ref, snd_ref, own_ref):
        j = pl.program_id(1)
        s = g_ref[...].astype(F32) + got_ref[...].astype(F32)
        snd_ref[...] = s.astype(BF16)

        @pl.when(j == sp_ref[1])
        def _():
            own_ref[...] = s

    grid_spec = pltpu.PrefetchScalarGridSpec(
        num_scalar_prefetch=1, grid=(hr // tr, N_CHIPS),
        in_specs=[pl.BlockSpec((None, tr, n), lambda i, j, sp: (j // per, ((j % per) * rows + sp[0] * hr) // tr + i, 0)),
                  pl.BlockSpec((None, tr, n), lambda i, j, sp: (j, i, 0))],
        out_specs=[pl.BlockSpec((None, tr, n), lambda i, j, sp: (j, i, 0)), pl.BlockSpec((tr, n), lambda i, j, sp: (i, 0))])
    return pl.pallas_call(
        body, name=name, grid_spec=grid_spec, out_shape=[SDS((N_CHIPS, hr, n), BF16), SDS((hr, n), F32)],
        compiler_params=_cparams("parallel", "arbitrary"),
    )(sp, g, got)


def _rs_chip_exchange(sends, part=(0, 1), prev=None):
    nw = len(sends)
    p, np_ = part

    def copies(src, dst, send, recv):
        x, y, c, me = _place()
        cps = []
        for i in range(nw):
            pr = sends[i].shape[1] // np_
            for r in range(1, N_CHIPS):
                cps.append(pltpu.make_async_remote_copy(
                    src_ref=src[i].at[(me + r) % N_CHIPS, pl.ds(p * pr, pr)], dst_ref=dst[i].at[r - 1, pl.ds(p * pr, pr)],
                    send_sem=send.at[i * (N_CHIPS - 1) + r - 1], recv_sem=recv.at[i * (N_CHIPS - 1) + r - 1],
                    device_id=_chip_dev((me + r) % N_CHIPS, c), device_id_type=MESH))
        return cps

    out_shapes = [SDS((N_CHIPS - 1,) + s.shape[1:], BF16) for s in sends]
    if prev is None:
        return _simple_stage(sends, out_shapes, {}, nw * (N_CHIPS - 1), copies)
    return _simple_stage(list(sends) + list(prev), out_shapes, {nw + i: i for i in range(nw)}, nw * (N_CHIPS - 1), copies)


def _rs_chip_add(own, got, sp, name):
    hr, n = own.shape
    tr = _tile(hr, max(16, (3 << 19) // n // 16 * 16), 16)

    def body(sp_ref, own_ref, got_ref, o_ref):
        acc = own_ref[...]
        for r in range(N_CHIPS - 1):
            acc = acc + got_ref[r].astype(F32)
        o_ref[...] = acc

    grid_spec = pltpu.PrefetchScalarGridSpec(
        num_scalar_prefetch=1, grid=(hr // tr,),
        in_specs=[pl.BlockSpec((tr, n), lambda i, sp: (i, 0)), pl.BlockSpec((N_CHIPS - 1, tr, n), lambda i, sp: (0, i, 0))],
        out_specs=pl.BlockSpec((tr, n), lambda i, sp: (sp[0] * (hr // tr) + i, 0)))
    return pl.pallas_call(body, name=name, grid_spec=grid_spec, out_shape=SDS((2 * hr, n), F32),
                          compiler_params=_cparams("parallel"))(sp, own, got)


def _rs_pair_share(blocks):
    nw = len(blocks)

    def copies(src, dst, send, recv):
        x, y, c, me = _place()
        cps = []
        for i in range(nw):
            hr = src[i].shape[0] // 2
            cps.append(pltpu.make_async_remote_copy(
                src_ref=src[i].at[pl.ds(c * hr, hr)], dst_ref=dst[i].at[pl.ds(c * hr, hr)], send_sem=send.at[i],
                recv_sem=recv.at[i], device_id=(x, y, 1 - c), device_id_type=MESH))
        return cps

    return _simple_stage(blocks, [SDS(b.shape, b.dtype) for b in blocks], {i: i for i in range(nw)}, nw, copies)


def kernel(x, p, ln_g, ln_b, ffn1_w_in, ffn1_w_out, mix_w_in, conv_w, hg_lower_bound, hg_norm_w, branch_w_conv, branch_w_hgrn, mix_w_out, ffn2_w_in, ffn2_w_out, ple_w_gate, ple_w_proj, loss_target, m_ln_g, m_ln_b, m_ffn1_w_in, m_ffn1_w_out, m_mix_w_in, m_conv_w, m_hg_lower_bound, m_hg_norm_w, m_branch_w_conv, m_branch_w_hgrn, m_mix_w_out, m_ffn2_w_in, m_ffn2_w_out, m_ple_w_gate, m_ple_w_proj, v_ln_g, v_ln_b, v_ffn1_w_in, v_ffn1_w_out, v_mix_w_in, v_conv_w, v_hg_lower_bound, v_hg_norm_w, v_branch_w_conv, v_branch_w_hgrn, v_mix_w_out, v_ffn2_w_in, v_ffn2_w_out, v_ple_w_gate, v_ple_w_proj):
    assert ln_g.shape[0] == DEPTH and x.shape[0] == 1 and p.shape[:2] == (1, 1)
    t, d = x.shape[1], x.shape[2]
    w = d // 2
    x0 = x.reshape(t, d)
    x0b = _to_bf16(x0, "x_bf16")
    pe = p.reshape(t, p.shape[-1])
    target = loss_target.reshape(t, d)
    cx, cy, cc = lax.axis_index("x"), lax.axis_index("y"), lax.axis_index("c")
    chip = 2 * cx + cy
    sp = jnp.stack([cc, chip]).astype(jnp.int32)

    big = dict(ffn1_w_in=ffn1_w_in, ffn1_w_out=ffn1_w_out, mix_w_in=mix_w_in, branch_w_conv=branch_w_conv,
               branch_w_hgrn=branch_w_hgrn, mix_w_out=mix_w_out, ffn2_w_in=ffn2_w_in, ffn2_w_out=ffn2_w_out,
               ple_w_gate=ple_w_gate, ple_w_proj=ple_w_proj)
    moments = dict(ffn1_w_in=(m_ffn1_w_in, v_ffn1_w_in), ffn1_w_out=(m_ffn1_w_out, v_ffn1_w_out), mix_w_in=(m_mix_w_in, v_mix_w_in),
                   branch_w_conv=(m_branch_w_conv, v_branch_w_conv), branch_w_hgrn=(m_branch_w_hgrn, v_branch_w_hgrn),
                   mix_w_out=(m_mix_w_out, v_mix_w_out), ffn2_w_in=(m_ffn2_w_in, v_ffn2_w_in), ffn2_w_out=(m_ffn2_w_out, v_ffn2_w_out),
                   ple_w_gate=(m_ple_w_gate, v_ple_w_gate), ple_w_proj=(m_ple_w_proj, v_ple_w_proj))
    names = list(big)

    n_loc = ffn1_w_in.shape[-1]
    n_pad = -(-n_loc // LANES) * LANES
    assert mix_w_in.shape[-1] % LANES == 0 and ffn1_w_out.shape[1] * 2 == n_loc
    pad_cols = dict(ffn1_w_in=n_pad, ffn2_w_in=n_pad)
    meta = {k: (N_CHIPS, big[k].shape[1], 1) for k in names}
    meta["ffn1_w_out"] = meta["ffn2_w_out"] = (2, n_pad, 2)
    rows = {k: big[k].shape[1] for k in names}
    swap = lambda a: jnp.transpose(a, (0, 2, 1))
    wbuf = {}
    zero_pad = jnp.zeros((max(n_pad - n_loc, 16), d), BF16)

    def cast(k, comm=()):
        if k in pad_cols:
            return _cast_pad_t(swap(big[k]), pad_cols[k], meta[k], sp, "cast_" + k, comm=comm)
        return _cast_pad(big[k], big[k].shape[2], meta[k], sp, "cast_" + k, comm=comm)

    def gather(*items):
        ks = [k for k, _, _ in items]
        return _gather_stage([wbuf[k] for k in ks], [meta[k] for k in ks], [rows[k] for k in ks], [(p_, n_) for _, p_, n_ in items],
                             zero_pad), ks

    def gathered(ks, outs):
        wbuf.update(zip(ks, outs))

    def w3(k):
        return wbuf[k]

    def w2(k):
        return wbuf[k].reshape(-1, wbuf[k].shape[2])

    dq, wq = d // N_CHIPS, w // N_CHIPS
    small = jnp.concatenate([ln_g[0], ln_b[0], jnp.pad(conv_w[0], ((0, 5), (0, dq - wq)))], axis=0)
    small = _gather_small(small, "gather_small")
    lng = small[:, 0:4, :].transpose(1, 0, 2).reshape(4, 1, d)
    lnb = small[:, 4:8, :].transpose(1, 0, 2).reshape(4, 1, d)
    cw = small[:, 8:11, :wq].transpose(1, 0, 2).reshape(3, w)
    hg = hg_lower_bound
    nw_ = hg_norm_w

    wbuf["ffn1_w_in"] = cast("ffn1_w_in")
    others = [k for k in names if k != "ffn1_w_in"]
    for part, k in enumerate(others):
        if part < FIRST_GATHER_PARTS:
            st, ks = gather(("ffn1_w_in", part, FIRST_GATHER_PARTS))
            wbuf[k], got = cast(k, comm=[st])
            gathered(ks, got)
        else:
            wbuf[k] = cast(k)
    st, ks = gather(("ffn1_w_out", 0, 1), ("mix_w_in", 0, 2))
    z1, got = _mm(x0b, w3("ffn1_w_in"), name="ffn1_in", b_blocked=True, out_dtype=BF16, comm=[st])
    gathered(ks, got)
    h1 = _swiglu_fwd(z1, "ffn1_act")
    st, ks = gather(("mix_w_in", 1, 2))
    y1, got = _mm(h1, w2("ffn1_w_out"), name="ffn1_out", tm=1024, tn=1024, tk=2816, comm=[st])
    gathered(ks, got)
    r1, x1, x1b = _ln_fwd(x0, y1, lng[0], lnb[0], 0.5, "ln0")
    st, ks = gather(("branch_w_conv", 0, 1), ("branch_w_hgrn", 0, 1), ("mix_w_out", 0, 1), ("ffn2_w_in", 0, 2))
    z, got = _mm(x1b, w3("mix_w_in"), name="mix_in", b_blocked=True, comm=[st])
    gathered(ks, got)
    ya = _conv_fwd(z, cw, w, "conv_fwd")
    st, ks = gather(("ffn2_w_in", 1, 2))
    (yb, o_h, states), got = _hgrn_fwd(z, hg, nw_, w, "hgrn_fwd", comm=[st])
    gathered(ks, got)
    ma = _mm(ya, w3("branch_w_conv"), name="branch_conv", b_blocked=True, tn=512)
    mb = _mm(yb, w3("branch_w_hgrn"), name="branch_hgrn", b_blocked=True, tn=512)
    merged = _merge_fwd(z, ma, mb, w, "merge_fwd")
    y2 = _mm(merged, w2("mix_w_out"), name="mix_out", tn=1024)
    r2, x2, x2b = _ln_fwd(x1, y2, lng[1], lnb[1], 1.0, "ln1")
    st, ks = gather(("ffn2_w_out", 0, 1), ("ple_w_gate", 0, 1), ("ple_w_proj", 0, 1))
    z3, got = _mm(x2b, w3("ffn2_w_in"), name="ffn2_in", b_blocked=True, out_dtype=BF16, comm=[st])
    gathered(ks, got)
    h3 = _swiglu_fwd(z3, "ffn2_act")
    y3 = _mm(h3, w2("ffn2_w_out"), name="ffn2_out", tm=1024, tn=1024, tk=2816)
    r3, x3, x3b = _ln_fwd(x2, y3, lng[2], lnb[2], 0.5, "ln2")
    gp = _mm(x3b, w2("ple_w_gate"), name="ple_gate", tn=1024)
    pp = _mm(pe, w3("ple_w_proj"), name="ple_proj", b_blocked=True, tn=512)
    dr4, dgp, dpp, dg3, db3, sq = _tail(x3, gp, pp, lng[3], lnb[3], target, "tail")

    grads, sends, owns, blocks, outs = {}, {}, {}, {}, {}

    def pair_exchange(*ks):
        return _rs_pair_exchange([grads[k] for k in ks], [meta[k] for k in ks], [rows[k] for k in ks])

    def pair_add(ks, got):
        for k, g_ in zip(ks, got):
            sends[k], owns[k] = _rs_pair_add(grads[k], g_, meta[k], rows[k], sp, "rs_pair_add_" + k)

    def chip_exchange(*ks):
        return _rs_chip_exchange([sends[k] for k in ks])

    def chip_add(ks, got):
        for k, g_ in zip(ks, got):
            blocks[k] = _rs_chip_add(owns[k], g_, sp, "rs_chip_add_" + k)

    def pair_share(*ks):
        return _rs_pair_share([blocks[k] for k in ks])

    def update(ks, full):
        for k, g_ in zip(ks, full):
            m_, v_ = moments[k]
            if k in pad_cols:
                outs[k] = [swap(a) for a in _adamw_t(swap(big[k]), g_, swap(m_), swap(v_), "adamw_" + k)]
            else:
                outs[k] = _adamw(big[k], g_, m_, v_, "adamw_" + k)

    ple = ("ple_w_gate", "ple_w_proj")
    mixo = ("mix_w_out", "branch_w_conv", "branch_w_hgrn")
    dx3m = _mm(dgp, w2("ple_w_gate"), name="d_ple_gate_x", tb=True, tn=1024, tk=2048)
    grads["ple_w_gate"] = _mm(x3b, dgp, name="d_ple_gate_w", ta=True, out_dtype=BF16, tm=1024, tk=2048, tn=1024).reshape(N_CHIPS, -1, d)
    grads["ple_w_proj"] = _mm(pe, dpp, name="d_ple_proj_w", ta=True, out_dtype=BF16, out_blocked=N_CHIPS, tk=2048, tn=512)
    dr3, dy3b, dg2, db2 = _ln_bwd(dr4, dx3m, r3, lng[2], 0.5, "ln2_bwd")
    dh3, got = _mm(dy3b, w2("ffn2_w_out"), name="d_ffn2_out_x", tb=True, out_dtype=BF16, tn=1408, tk=2048,
                   comm=[pair_exchange(*ple)])
    pair_add(ple, got)
    g_, got = _mm(h3, dy3b, name="d_ffn2_out_w", ta=True, out_dtype=BF16, tm=1408, tk=2048, tn=1024, comm=[chip_exchange(*ple)])
    grads["ffn2_w_out"] = g_.reshape(2, n_pad, d)
    chip_add(ple, got)
    dz3 = _swiglu_bwd(dh3, z3, "ffn2_act_bwd")
    dx2m, got, full = _mm(dz3, w3("ffn2_w_in"), name="d_ffn2_in_x", tb=True, b_blocked=True, tm=1024, tn=1024, tk=2816,
                          comm=[pair_exchange("ffn2_w_out"), pair_share(*ple)])
    pair_add(["ffn2_w_out"], got)
    update(ple, full)
    grads["ffn2_w_in"], got = _mm(x2b, dz3, name="d_ffn2_in_w", ta=True, out_dtype=BF16, out_blocked=N_CHIPS, tk=4096, comm=[chip_exchange("ffn2_w_out")])
    chip_add(["ffn2_w_out"], got)
    dr2, dy2b, dg1, db1 = _ln_bwd(dr3, dx2m, r2, lng[1], 1.0, "ln1_bwd")
    dmer, got = _mm(dy2b, w2("mix_w_out"), name="d_mix_out_x", tb=True, tn=1024, tk=2048, comm=[pair_exchange("ffn2_w_in")])
    pair_add(["ffn2_w_in"], got)
    g_, full = _mm(merged, dy2b, name="d_mix_out_w", ta=True, out_dtype=BF16, tm=1024, tk=2048, tn=1024, comm=[pair_share("ffn2_w_out")])
    grads["mix_w_out"] = g_.reshape(N_CHIPS, -1, d)
    update(["ffn2_w_out"], full)
    dma, dmb, dgc, dgh = _merge_bwd(dmer, z, ma, mb, w, "merge_bwd")
    dya = _mm(dma, w3("branch_w_conv"), name="d_branch_conv_x", tb=True, b_blocked=True, tn=1024, tk=512)
    dyb = _mm(dmb, w3("branch_w_hgrn"), name="d_branch_hgrn_x", tb=True, b_blocked=True, tn=1024, tk=512)
    grads["branch_w_conv"] = _mm(ya, dma, name="d_branch_conv_w", ta=True, out_dtype=BF16, out_blocked=N_CHIPS, tm=1024, tk=2048, tn=512)
    grads["branch_w_hgrn"] = _mm(yb, dmb, name="d_branch_hgrn_w", ta=True, out_dtype=BF16, out_blocked=N_CHIPS, tm=1024, tk=2048, tn=512)
    dbg, dcg, dhc, dcw = _conv_bwd(dya, z, cw, w, "conv_bwd")
    (dq_, df_, di_, dgr_, dhg, dnw), got2, got = _hgrn_bwd(dyb, z, o_h, states, hg, nw_, w, "hgrn_bwd",
                                                            comm=[chip_exchange("ffn2_w_in"), pair_exchange(*mixo)])
    chip_add(["ffn2_w_in"], got2)
    pair_add(mixo, got)
    dz = _concat_cols([dbg, dcg, dhc, dq_, df_, di_, dgr_, dgc, dgh], "dz_concat")
    dx1m, full, got = _mm(dz, w3("mix_w_in"), name="d_mix_in_x", tb=True, b_blocked=True, tm=1024, tn=1024, tk=2816,
                          comm=[pair_share("ffn2_w_in"), chip_exchange(*mixo)])
    update(["ffn2_w_in"], full)
    chip_add(mixo, got)
    grads["mix_w_in"], full = _mm(x1b, dz, name="d_mix_in_w", ta=True, out_dtype=BF16, out_blocked=N_CHIPS, tk=4096, comm=[pair_share(*mixo)])
    update(mixo, full)
    dr1, dy1b, dg0, db0 = _ln_bwd(dr2, dx1m, r1, lng[0], 0.5, "ln0_bwd")
    dh1, got = _mm(dy1b, w2("ffn1_w_out"), name="d_ffn1_out_x", tb=True, out_dtype=BF16, tn=1408, tk=2048,
                   comm=[pair_exchange("mix_w_in")])
    pair_add(["mix_w_in"], got)
    mix_sends = [sends["mix_w_in"]]
    g_, got_a = _mm(h1, dy1b, name="d_ffn1_out_w", ta=True, out_dtype=BF16, tm=1408, tk=2048, tn=1024, comm=[_rs_chip_exchange(mix_sends, (0, 2))])
    grads["ffn1_w_out"] = g_.reshape(2, n_pad, d)
    dz1 = _swiglu_bwd(dh1, z1, "ffn1_act_bwd")
    g_other, got2, got = _mm(x0b, dz1, name="d_ffn1_in_w_other", ta=True, out_dtype=BF16, out_blocked=N_CHIPS, tk=4096, half=(sp, True),
                             comm=[_rs_chip_exchange(mix_sends, (1, 2), got_a), pair_exchange("ffn1_w_out")])
    chip_add(["mix_w_in"], got2)
    pair_add(["ffn1_w_out"], got)
    grads["ffn1_w_in"], full, got2, got = _mm(
        x0b, dz1, name="d_ffn1_in_w_own", ta=True, out_dtype=BF16, out_blocked=N_CHIPS, tk=4096, half=(sp, False),
        comm=[pair_share("mix_w_in"), chip_exchange("ffn1_w_out"),
              _rs_pair_exchange([g_other], [meta["ffn1_w_in"]], [rows["ffn1_w_in"]])])
    update(["mix_w_in"], full)
    chip_add(["ffn1_w_out"], got2)
    pair_add(["ffn1_w_in"], got)
    dx0, got2, full = _mm(dz1, w3("ffn1_w_in"), name="d_ffn1_in_x", tb=True, b_blocked=True, tm=1024, tn=1024, tk=2816,
                          add=(dr1, ALPHA), comm=[chip_exchange("ffn1_w_in"), pair_share("ffn1_w_out")])
    chip_add(["ffn1_w_in"], got2)
    update(["ffn1_w_out"], full)
    grad_x = dx0.reshape(x.shape)
    update(["ffn1_w_in"], _run_stages([pair_share("ffn1_w_in")], "rs_tail_pair")[0])

    pack = jnp.concatenate([
        dg0, dg1, dg2, dg3, db0, db1, db2, db3,
        jnp.pad(dcw, ((0, 0), (0, d - w))), jnp.pad(dhg, ((0, 0), (0, d - w))),
        jnp.pad(jnp.sum(dnw.reshape(-1, HEAD), axis=0, keepdims=True), ((0, 0), (0, d - HEAD))), sq], axis=0)
    pack = _all_reduce_small(jnp.pad(pack, ((0, 1), (0, 0))), "reduce_small")
    loss = (0.5 / d) * jnp.sum(pack[14])
    g_ln_g = lax.dynamic_slice_in_dim(pack[0:4], chip * dq, dq, axis=1)
    g_ln_b = lax.dynamic_slice_in_dim(pack[4:8], chip * dq, dq, axis=1)
    g_conv = lax.dynamic_slice_in_dim(pack[8:11, :w], chip * wq, wq, axis=1)
    g_hg = pack[11:13, :w]
    g_nw = pack[13:14, :HEAD]

    small_w = dict(ln_g=(ln_g, g_ln_g, m_ln_g, v_ln_g), ln_b=(ln_b, g_ln_b, m_ln_b, v_ln_b),
                   conv_w=(conv_w, g_conv, m_conv_w, v_conv_w), hg_lower_bound=(hg_lower_bound, g_hg, m_hg_lower_bound, v_hg_lower_bound),
                   hg_norm_w=(hg_norm_w, g_nw, m_hg_norm_w, v_hg_norm_w))
    for k, (w_, g_, m_, v_) in small_w.items():
        outs[k] = _adamw(w_, g_.reshape(-1, w_.shape[-1]), m_, v_, "adamw_" + k)

    order = ["ln_g", "ln_b", "ffn1_w_in", "ffn1_w_out", "mix_w_in", "conv_w", "hg_lower_bound", "hg_norm_w", "branch_w_conv",
             "branch_w_hgrn", "mix_w_out", "ffn2_w_in", "ffn2_w_out", "ple_w_gate", "ple_w_proj"]
    return (loss, grad_x, *[outs[k][0] for k in order], *[outs[k][1] for k in order], *[outs[k][2] for k in order],
            *[outs[k][3] for k in order])
```

```python
import collections
import functools

import jax
import jax.numpy as jnp
from jax import lax
from jax.experimental import pallas as pl
from jax.experimental.pallas import tpu as pltpu

F32 = jnp.float32
BF16 = jnp.bfloat16
MESH = pl.DeviceIdType.MESH
ANY = pl.BlockSpec(memory_space=pl.ANY)
VMEM_SPEC = pl.BlockSpec(memory_space=pltpu.VMEM)
SDS = jax.ShapeDtypeStruct

DEPTH = 1
ALPHA = (2.0 * DEPTH) ** 0.25
LN_EPS = 1e-5
RMS_EPS = 1e-6
CHUNK = 32
HEAD = 128
ADAM_LR, ADAM_B1, ADAM_B2, ADAM_EPS, ADAM_WD, ADAM_STEP = 0.001, 0.9, 0.999, 1e-08, 0.01, 10

LANES = 128
N_CHIPS = 4
N_DEV = 8
FIRST_GATHER_PARTS = 8
VMEM_LIMIT = 52 * 1024 * 1024


def _cparams(*sem):
    if sem:
        return pltpu.CompilerParams(dimension_semantics=sem, vmem_limit_bytes=VMEM_LIMIT)
    return pltpu.CompilerParams(vmem_limit_bytes=VMEM_LIMIT)


def _tile(n, target, mult):
    best = None
    for t in range(mult, min(n, target) + 1, mult):
        if n % t == 0:
            best = t
    return best if best is not None else n


def _sigmoid(x):
    return 1.0 / (1.0 + jnp.exp(-x))


_Stage = collections.namedtuple("_Stage", "ins out_shapes aliases sems start finish")


def _hosted_call(compute, stages, *, name, grid, in_specs, out_specs, out_shape, scratch_shapes, operands, parallel,
                 prefetch=None):
    n_cmp, n_out, n_scr = len(in_specs), len(out_specs), len(scratch_shapes)
    n_in = n_cmp
    n_pre = int(prefetch is not None)
    c_in = [len(s.ins) for s in stages]
    c_out = [len(s.out_shapes) for s in stages]
    c_sem = [len(s.sems) for s in stages]
    aliases = {}
    for si, s in enumerate(stages):
        for a_in, a_out in s.aliases.items():
            aliases[n_pre + n_in + sum(c_in[:si]) + a_in] = n_out + sum(c_out[:si]) + a_out

    def body(*refs):
        refs = refs[n_pre:]
        ins = refs[:n_cmp]
        cins = refs[n_in:n_in + sum(c_in)]
        outs = refs[n_in + sum(c_in):n_in + sum(c_in) + n_out]
        couts = refs[n_in + sum(c_in) + n_out:n_in + sum(c_in) + n_out + sum(c_out)]
        scr = refs[n_in + sum(c_in) + n_out + sum(c_out):][:n_scr]
        sems = refs[n_in + sum(c_in) + n_out + sum(c_out) + n_scr:]

        def stage_refs(si):
            return (cins[sum(c_in[:si]):sum(c_in[:si + 1])], couts[sum(c_out[:si]):sum(c_out[:si + 1])],
                    sems[sum(c_sem[:si]):sum(c_sem[:si + 1])])

        if stages:
            first = functools.reduce(jnp.logical_and, [pl.program_id(ax) == 0 for ax in range(len(grid))])
            last = functools.reduce(jnp.logical_and, [pl.program_id(ax) == grid[ax] - 1 for ax in range(len(grid))])

            @pl.when(first)
            def _():
                for si, s in enumerate(stages):
                    s.start(*stage_refs(si))

        compute(*ins, *outs, *scr)
        if stages:
            @pl.when(last)
            def _():
                for si, s in enumerate(stages):
                    s.finish(*stage_refs(si))

    sem = ("arbitrary",) * len(grid) if stages else ("parallel",) * parallel + ("arbitrary",) * (len(grid) - parallel)
    all_in = list(in_specs) + [ANY] * (n_in - n_cmp + sum(c_in))
    all_out = list(out_specs) + [ANY] * sum(c_out)
    all_scr = list(scratch_shapes) + [q for s in stages for q in s.sems]
    all_shape = list(out_shape) + [o for s in stages for o in s.out_shapes]
    args = list(operands) + [a for s in stages for a in s.ins]
    if prefetch is None:
        res = pl.pallas_call(body, name=name, grid=grid, in_specs=all_in, out_specs=all_out, out_shape=all_shape,
                             input_output_aliases=aliases, scratch_shapes=all_scr, compiler_params=_cparams(*sem))(*args)
    else:
        grid_spec = pltpu.PrefetchScalarGridSpec(num_scalar_prefetch=1, grid=grid, in_specs=all_in, out_specs=all_out,
                                                 scratch_shapes=all_scr)
        res = pl.pallas_call(body, name=name, grid_spec=grid_spec, out_shape=all_shape, input_output_aliases=aliases,
                             compiler_params=_cparams(*sem))(prefetch, *args)
    main = res[0] if n_out == 1 else list(res[:n_out])
    if not stages:
        return main
    rest = res[n_out:]
    return (main, *[list(rest[sum(c_out[:si]):sum(c_out[:si + 1])]) for si in range(len(stages))])


def _run_stages(stages, name):
    def body(*refs):
        n_i = sum(len(s.ins) for s in stages)
        n_o = sum(len(s.out_shapes) for s in stages)
        cins, couts, sems = refs[:n_i], refs[n_i:n_i + n_o], refs[n_i + n_o:]
        pos = [0, 0, 0]
        parts = []
        for s in stages:
            parts.append((cins[pos[0]:pos[0] + len(s.ins)], couts[pos[1]:pos[1] + len(s.out_shapes)], sems[pos[2]:pos[2] + len(s.sems)]))
            pos = [pos[0] + len(s.ins), pos[1] + len(s.out_shapes), pos[2] + len(s.sems)]
        for s, p_ in zip(stages, parts):
            s.start(*p_)
        for s, p_ in zip(stages, parts):
            s.finish(*p_)

    aliases, ni, no = {}, 0, 0
    for s in stages:
        for a_in, a_out in s.aliases.items():
            aliases[ni + a_in] = no + a_out
        ni, no = ni + len(s.ins), no + len(s.out_shapes)
    res = pl.pallas_call(
        body, name=name, in_specs=[ANY] * ni, out_specs=[ANY] * no, out_shape=[o for s in stages for o in s.out_shapes],
        input_output_aliases=aliases, scratch_shapes=[q for s in stages for q in s.sems],
    )(*[a for s in stages for a in s.ins])
    out, pos = [], 0
    for s in stages:
        out.append(list(res[pos:pos + len(s.out_shapes)]))
        pos += len(s.out_shapes)
    return out


def _mm(a, b, *, name, ta=False, tb=False, b_blocked=False, out_blocked=0, out_dtype=F32,
        tm=512, tn=1408, tk=2048, comm=(), half=None, add=None):
    if ta:
        kd, m = a.shape
    else:
        m, kd = a.shape
    if b_blocked and not tb:
        g, kb, nb = b.shape
        assert kb == kd
        n = g * nb
        tn = _tile(nb, tn, LANES)
        tk = _tile(kd, tk, LANES)
        per_n = nb // tn
        b_spec = pl.BlockSpec((None, tk, tn), lambda i, j, k, *s: (j // per_n, k, j % per_n))
    elif b_blocked and tb:
        g, n, kb = b.shape
        assert g * kb == kd
        tn = _tile(n, tn, LANES)
        tk = _tile(kb, tk, LANES)
        per_k = kb // tk
        b_spec = pl.BlockSpec((None, tn, tk), lambda i, j, k, *s: (k // per_k, j, k % per_k))
    elif tb:
        n, kb = b.shape
        assert kb == kd
        tn = _tile(n, tn, LANES)
        tk = _tile(kd, tk, LANES)
        b_spec = pl.BlockSpec((tn, tk), lambda i, j, k, *s: (j, k))
    else:
        kb, n = b.shape
        assert kb == kd
        tn = _tile(n // out_blocked if out_blocked else n, tn, LANES)
        per_o = (n // out_blocked) // tn if out_blocked else None
        tk = _tile(kd, tk, LANES)
        b_spec = pl.BlockSpec((tk, tn), lambda i, j, k, *s: (k, j))
    m_run = m // 2 if half else m
    tm = _tile(m_run, tm, LANES if ta else 8)

    def row(i, s):
        if not half:
            return i
        h = 1 - s[0][0] if half[1] else s[0][0]
        return h * (m_run // tm) + i

    if ta:
        a_spec = pl.BlockSpec((tk, tm), lambda i, j, k, *s: (k, row(i, s)))
    else:
        a_spec = pl.BlockSpec((tm, tk), lambda i, j, k, *s: (row(i, s), k))
    if out_blocked:
        assert not b_blocked and not tb
        o_spec = pl.BlockSpec((None, tm, tn), lambda i, j, k, *s: (j // per_o, row(i, s), j % per_o))
        o_shape = SDS((out_blocked, m, n // out_blocked), out_dtype)
    else:
        o_spec = pl.BlockSpec((tm, tn), lambda i, j, k, *s: (row(i, s), j))
        o_shape = SDS((m, n), out_dtype)
    nk = kd // tk
    dn = (((0 if ta else 1,), (1 if tb else 0,)), ((), ()))
    grid = (m_run // tm, n // tn, nk)

    def compute(a_ref, b_ref, *rest):
        add_ref = rest[0] if add else None
        o_ref, acc_ref = rest[-2:]
        part = lax.dot_general(a_ref[...].astype(BF16), b_ref[...].astype(BF16), dn, preferred_element_type=F32)

        def store(acc):
            if add:
                acc = acc + add[1] * add_ref[...]
            o_ref[...] = acc.astype(o_ref.dtype)

        if nk == 1:
            store(part)
        else:
            k = pl.program_id(2)

            @pl.when(k == 0)
            def _():
                acc_ref[...] = part

            @pl.when(k > 0)
            def _():
                acc_ref[...] += part

            @pl.when(k == nk - 1)
            def _():
                store(acc_ref[...])

    extra = [(add[0], o_spec)] if add else []
    return _hosted_call(compute, comm, name=name, grid=grid, in_specs=[a_spec, b_spec] + [s_ for _, s_ in extra], out_specs=[o_spec],
                        out_shape=[o_shape], scratch_shapes=[pltpu.VMEM((tm, tn), F32)], operands=(a, b, *[a_ for a_, _ in extra]),
                        parallel=2, prefetch=half[0] if half else None)


def _swiglu_fwd(z, name, comm=()):
    t, n = z.shape
    n2 = n // 2
    tr = _tile(t, 128, 16)

    def body(a_ref, u_ref, o_ref):
        a = a_ref[...].astype(F32)
        o_ref[...] = (a * _sigmoid(a) * u_ref[...].astype(F32)).astype(o_ref.dtype)

    return _hosted_call(
        body, comm, name=name, grid=(t // tr,),
        in_specs=[pl.BlockSpec((tr, n2), lambda i: (i, 0)), pl.BlockSpec((tr, n2), lambda i: (i, 1))],
        out_specs=[pl.BlockSpec((tr, n2), lambda i: (i, 0))], out_shape=[SDS((t, n2), BF16)], scratch_shapes=[],
        operands=(z, z), parallel=1)


def _swiglu_bwd(dh, z, name):
    t, n = z.shape
    n2 = n // 2
    tr = _tile(t, 128, 16)

    def body(dh_ref, a_ref, u_ref, o_ref):
        a = a_ref[...].astype(F32)
        dh_ = dh_ref[...].astype(F32)
        s = _sigmoid(a)
        o_ref[:, 0:n2] = (dh_ * u_ref[...].astype(F32) * (s * (1.0 + a * (1.0 - s)))).astype(o_ref.dtype)
        o_ref[:, n2:n] = (dh_ * a * s).astype(o_ref.dtype)

    return pl.pallas_call(
        body, name=name, grid=(t // tr,),
        in_specs=[pl.BlockSpec((tr, n2), lambda i: (i, 0)), pl.BlockSpec((tr, n2), lambda i: (i, 0)),
                  pl.BlockSpec((tr, n2), lambda i: (i, 1))],
        out_specs=pl.BlockSpec((tr, n), lambda i: (i, 0)), out_shape=SDS((t, n), BF16),
        compiler_params=_cparams("parallel"),
    )(dh, z, z)


def _ln_stats(r):
    mu = jnp.mean(r, axis=-1, keepdims=True)
    xc = r - mu
    var = jnp.mean(xc * xc, axis=-1, keepdims=True)
    return xc * lax.rsqrt(var + LN_EPS)


def _ln_fwd(xp, y, g, b, scale, name, comm=()):
    t, d = xp.shape
    tr = _tile(t, 256, 16)

    def body(xp_ref, y_ref, g_ref, b_ref, r_ref, x_ref, xb_ref):
        r = ALPHA * xp_ref[...] + scale * y_ref[...]
        x = _ln_stats(r) * g_ref[...] + b_ref[...]
        r_ref[...] = r
        x_ref[...] = x
        xb_ref[...] = x.astype(BF16)

    row = pl.BlockSpec((tr, d), lambda i: (i, 0))
    vec = pl.BlockSpec((1, d), lambda i: (0, 0))
    return _hosted_call(
        body, comm, name=name, grid=(t // tr,), in_specs=[row, row, vec, vec], out_specs=[row, row, row],
        out_shape=[SDS((t, d), F32), SDS((t, d), F32), SDS((t, d), BF16)], scratch_shapes=[], operands=(xp, y, g, b), parallel=1)


def _ln_bwd(dra, dxm, r, g, scale, name):
    t, d = r.shape
    tr = _tile(t, 256, 16)

    def body(dra_ref, dxm_ref, r_ref, g_ref, dr_ref, dyb_ref, dg_ref, db_ref):
        i = pl.program_id(0)
        dx = ALPHA * dra_ref[...] + dxm_ref[...]
        rr = r_ref[...]
        mu = jnp.mean(rr, axis=-1, keepdims=True)
        xc = rr - mu
        rstd = lax.rsqrt(jnp.mean(xc * xc, axis=-1, keepdims=True) + LN_EPS)
        xh = xc * rstd
        dxh = dx * g_ref[...]
        dr = rstd * (dxh - jnp.mean(dxh, axis=-1, keepdims=True) - xh * jnp.mean(dxh * xh, axis=-1, keepdims=True))
        dr_ref[...] = dr
        dyb_ref[...] = (scale * dr).astype(BF16)
        dg = jnp.sum(dx * xh, axis=0, keepdims=True)
        db = jnp.sum(dx, axis=0, keepdims=True)

        @pl.when(i == 0)
        def _():
            dg_ref[...] = dg
            db_ref[...] = db

        @pl.when(i > 0)
        def _():
            dg_ref[...] += dg
            db_ref[...] += db

    row = pl.BlockSpec((tr, d), lambda i: (i, 0))
    vec = pl.BlockSpec((1, d), lambda i: (0, 0))
    return pl.pallas_call(
        body, name=name, grid=(t // tr,), in_specs=[row, row, row, vec], out_specs=[row, row, vec, vec],
        out_shape=[SDS((t, d), F32), SDS((t, d), BF16), SDS((1, d), F32), SDS((1, d), F32)],
        compiler_params=_cparams("arbitrary"),
    )(dra, dxm, r, g)


def _tail(x3, gp, pp, g, b, target, name):
    t, d = x3.shape
    tr = _tile(t, 256, 16)

    def body(x3_ref, gp_ref, pp_ref, g_ref, b_ref, tg_ref, dr_ref, dgp_ref, dpp_ref, dg_ref, db_ref, sq_ref):
        i = pl.program_id(0)
        gate = _sigmoid(gp_ref[...])
        pp_ = pp_ref[...]
        r = ALPHA * x3_ref[...] + gate * pp_
        mu = jnp.mean(r, axis=-1, keepdims=True)
        xc = r - mu
        rstd = lax.rsqrt(jnp.mean(xc * xc, axis=-1, keepdims=True) + LN_EPS)
        xh = xc * rstd
        err = xh * g_ref[...] + b_ref[...] - tg_ref[...]
        dx = err * (1.0 / d)
        dxh = dx * g_ref[...]
        dr = rstd * (dxh - jnp.mean(dxh, axis=-1, keepdims=True) - xh * jnp.mean(dxh * xh, axis=-1, keepdims=True))
        dr_ref[...] = dr
        dgp_ref[...] = (dr * pp_ * gate * (1.0 - gate)).astype(BF16)
        dpp_ref[...] = (dr * gate).astype(BF16)
        dg = jnp.sum(dx * xh, axis=0, keepdims=True)
        db = jnp.sum(dx, axis=0, keepdims=True)
        sq = jnp.sum(err * err, axis=0, keepdims=True)

        @pl.when(i == 0)
        def _():
            dg_ref[...] = dg
            db_ref[...] = db
            sq_ref[...] = sq

        @pl.when(i > 0)
        def _():
            dg_ref[...] += dg
            db_ref[...] += db
            sq_ref[...] += sq

    row = pl.BlockSpec((tr, d), lambda i: (i, 0))
    vec = pl.BlockSpec((1, d), lambda i: (0, 0))
    return pl.pallas_call(
        body, name=name, grid=(t // tr,), in_specs=[row, row, row, vec, vec, row],
        out_specs=[row, row, row, vec, vec, vec],
        out_shape=[SDS((t, d), F32), SDS((t, d), BF16), SDS((t, d), BF16), SDS((1, d), F32), SDS((1, d), F32),
                   SDS((1, d), F32)],
        compiler_params=_cparams("arbitrary"),
    )(x3, gp, pp, g, b, target)


def _to_bf16(x, name):
    t, d = x.shape
    tr = _tile(t, 512, 16)
    row = pl.BlockSpec((tr, d), lambda i: (i, 0))

    def body(x_ref, o_ref):
        o_ref[...] = x_ref[...].astype(BF16)

    return pl.pallas_call(body, name=name, grid=(t // tr,), in_specs=[row], out_specs=row, out_shape=SDS((t, d), BF16),
                          compiler_params=_cparams("parallel"))(x)


def _concat_cols(parts, name):
    t = parts[0].shape[0]
    widths = [p_.shape[1] for p_ in parts]
    tr = _tile(t, 256, 16)

    def body(*refs):
        o_ref = refs[-1]
        at = 0
        for ref, wd in zip(refs[:-1], widths):
            o_ref[:, at:at + wd] = ref[...]
            at += wd

    return pl.pallas_call(
        body, name=name, grid=(t // tr,), in_specs=[pl.BlockSpec((tr, wd), lambda i: (i, 0)) for wd in widths],
        out_specs=pl.BlockSpec((tr, sum(widths)), lambda i: (i, 0)), out_shape=SDS((t, sum(widths)), parts[0].dtype),
        compiler_params=_cparams("parallel"),
    )(*parts)


def _merge_fwd(z, ma, mb, w, name):
    t = z.shape[0]
    tr = _tile(t, 256, 16)

    def body(gc_ref, gh_ref, ma_ref, mb_ref, o_ref):
        o_ref[...] = (_sigmoid(gc_ref[...]) * ma_ref[...] + _sigmoid(gh_ref[...]) * mb_ref[...]).astype(BF16)

    half = pl.BlockSpec((tr, w), lambda i, j: (i, j))
    return pl.pallas_call(
        body, name=name, grid=(t // tr, 2),
        in_specs=[pl.BlockSpec((tr, w), lambda i, j: (i, 7 + j)), pl.BlockSpec((tr, w), lambda i, j: (i, 9 + j)), half, half],
        out_specs=half, out_shape=SDS((t, 2 * w), BF16), compiler_params=_cparams("parallel", "parallel"),
    )(z, z, ma, mb)


def _merge_bwd(dmer, z, ma, mb, w, name):
    t = z.shape[0]
    tr = _tile(t, 256, 16)

    def body(d_ref, gc_ref, gh_ref, ma_ref, mb_ref, dma_ref, dmb_ref, dgc_ref, dgh_ref):
        dm = d_ref[...]
        sc = _sigmoid(gc_ref[...])
        sh = _sigmoid(gh_ref[...])
        dma_ref[...] = (dm * sc).astype(BF16)
        dmb_ref[...] = (dm * sh).astype(BF16)
        dgc_ref[...] = (dm * ma_ref[...] * sc * (1.0 - sc)).astype(BF16)
        dgh_ref[...] = (dm * mb_ref[...] * sh * (1.0 - sh)).astype(BF16)

    half = pl.BlockSpec((tr, w), lambda i, j: (i, j))
    return pl.pallas_call(
        body, name=name, grid=(t // tr, 2),
        in_specs=[half, pl.BlockSpec((tr, w), lambda i, j: (i, 7 + j)), pl.BlockSpec((tr, w), lambda i, j: (i, 9 + j)), half, half],
        out_specs=[half] * 4, out_shape=[SDS((t, 2 * w), BF16)] * 4, compiler_params=_cparams("parallel", "parallel"),
    )(dmer, z, z, ma, mb)


def _shift_down(x, s, row):
    return jnp.where(row >= s, pltpu.roll(x, s, axis=0), 0.0)


def _shift_up(x, s, row, t):
    return jnp.where(row < t - s, pltpu.roll(x, t - s, axis=0), 0.0)


def _conv_fwd(z, cw, w, name):
    t = z.shape[0]
    tc = LANES
    nb = w // tc

    def body(b_ref, c_ref, h_ref, w_ref, o_ref):
        u = c_ref[...] * h_ref[...]
        row = lax.broadcasted_iota(jnp.int32, u.shape, 0)
        cw_ = w_ref[...]
        conv = cw_[2:3, :] * u + cw_[1:2, :] * _shift_down(u, 1, row) + cw_[0:1, :] * _shift_down(u, 2, row)
        o_ref[...] = (b_ref[...] * conv).astype(BF16)

    col = lambda off: pl.BlockSpec((t, tc), lambda j: (0, off * nb + j))
    return pl.pallas_call(
        body, name=name, grid=(nb,), in_specs=[col(0), col(1), col(2), pl.BlockSpec((3, tc), lambda j: (0, j))],
        out_specs=pl.BlockSpec((t, tc), lambda j: (0, j)), out_shape=SDS((t, w), BF16), compiler_params=_cparams("parallel"),
    )(z, z, z, cw)


def _conv_bwd(dy, z, cw, w, name):
    t = z.shape[0]
    tc = LANES
    nb = w // tc

    def body(dy_ref, b_ref, c_ref, h_ref, w_ref, db_ref, dc_ref, dh_ref, dw_ref):
        c_, h_ = c_ref[...], h_ref[...]
        u = c_ * h_
        row = lax.broadcasted_iota(jnp.int32, u.shape, 0)
        cw_ = w_ref[...]
        u1 = _shift_down(u, 1, row)
        u2 = _shift_down(u, 2, row)
        dy_ = dy_ref[...]
        db_ref[...] = (dy_ * (cw_[2:3, :] * u + cw_[1:2, :] * u1 + cw_[0:1, :] * u2)).astype(BF16)
        dconv = dy_ * b_ref[...]
        du = cw_[2:3, :] * dconv + cw_[1:2, :] * _shift_up(dconv, 1, row, t) + cw_[0:1, :] * _shift_up(dconv, 2, row, t)
        dc_ref[...] = (du * h_).astype(BF16)
        dh_ref[...] = (du * c_).astype(BF16)
        dw_ref[0:1, :] = jnp.sum(dconv * u2, axis=0, keepdims=True)
        dw_ref[1:2, :] = jnp.sum(dconv * u1, axis=0, keepdims=True)
        dw_ref[2:3, :] = jnp.sum(dconv * u, axis=0, keepdims=True)

    col = lambda off: pl.BlockSpec((t, tc), lambda j: (0, off * nb + j))
    own = pl.BlockSpec((t, tc), lambda j: (0, j))
    wsp = pl.BlockSpec((3, tc), lambda j: (0, j))
    return pl.pallas_call(
        body, name=name, grid=(nb,), in_specs=[own, col(0), col(1), col(2), wsp], out_specs=[own, own, own, wsp],
        out_shape=[SDS((t, w), BF16)] * 3 + [SDS((3, w), F32)], compiler_params=_cparams("parallel"),
    )(dy, z, z, z, cw)


def _lower_bound(hg):
    mx = jnp.max(hg, axis=0, keepdims=True)
    e = jnp.exp(hg - mx)
    inv = 1.0 / jnp.sum(e, axis=0, keepdims=True)
    return e[0:1, :] * inv, e[1:2, :] * inv


def _chunk_cumsum(x, row):
    s = 1
    while s < CHUNK:
        x = x + jnp.where(row % CHUNK >= s, pltpu.roll(x, s, axis=0), 0.0)
        s *= 2
    return x


def _dot_nt(a, b):
    return lax.dot_general(a.astype(BF16), b.astype(BF16), (((1,), (1,)), ((), ())), preferred_element_type=F32)


def _dot_tn(a, b):
    return lax.dot_general(a.astype(BF16), b.astype(BF16), (((0,), (0,)), ((), ())), preferred_element_type=F32)


def _dot_nn(a, b):
    return jnp.dot(a.astype(BF16), b.astype(BF16), preferred_element_type=F32)


def _tril(x):
    r = lax.broadcasted_iota(jnp.int32, x.shape, 0)
    c = lax.broadcasted_iota(jnp.int32, x.shape, 1)
    return jnp.where(r >= c, x, 0.0)


HGRN_GROUP = 4
HGRN_ROWS = 512
HGRN_UNROLL = 2


def _unrolled_loop(n, step, init):
    assert n % HGRN_UNROLL == 0

    def trip(i, carry):
        for u in range(HGRN_UNROLL):
            carry = step(i * HGRN_UNROLL + u, carry)
        return carry

    return lax.fori_loop(0, n // HGRN_UNROLL, trip, init)


def _hgrn_chunk_inputs(q_ref, f_ref, cum_ref, lb, rows, ln):
    qr = q_ref[rows, ln]
    q = qr * _sigmoid(qr)
    f = lb + (1.0 - lb) * _sigmoid(f_ref[rows, ln])
    return q, 1.0 - f, cum_ref[rows, ln]


def _hgrn_fwd(z, hg, nw, w, name, comm=()):
    t = z.shape[0]
    nh = w // HEAD
    gh = _tile(nh, HGRN_GROUP, 1)
    gw = gh * HEAD
    ngrp = nh // gh
    tb = _tile(t, HGRN_ROWS, CHUNK)
    ncb = tb // CHUNK

    def body(q_ref, f_ref, i_ref, g_ref, hg_ref, nw_ref, y_ref, o_ref, st_ref, cum_ref, *s_refs):
        lb_all, _ = _lower_bound(hg_ref[...])
        row = lax.broadcasted_iota(jnp.int32, (tb, gw), 0)
        cum_ref[...] = _chunk_cumsum(jnp.log(lb_all + (1.0 - lb_all) * _sigmoid(f_ref[...])), row)

        @pl.when(pl.program_id(1) == 0)
        def _():
            for s_ref in s_refs:
                s_ref[...] = jnp.zeros_like(s_ref)

        def step(c, carry):
            rows = pl.ds(pl.multiple_of(c * CHUNK, CHUNK), CHUNK)
            for g in range(gh):
                ln = slice(g * HEAD, (g + 1) * HEAD)
                lb = lb_all[:, ln]
                q, k, cum = _hgrn_chunk_inputs(q_ref, f_ref, cum_ref, lb, rows, ln)
                v = i_ref[rows, ln]
                last = cum[CHUNK - 1:CHUNK, :]
                qe = q * jnp.exp(cum)
                st = s_refs[g][...]
                st_ref[g, c] = st.astype(BF16)
                o_ref[rows, ln] = _dot_nt(qe, st) + _dot_nn(_tril(_dot_nt(qe, k * jnp.exp(-cum))), v)
                s_refs[g][...] = st * jnp.exp(last) + _dot_tn(v, k * jnp.exp(last - cum))
            return carry

        _unrolled_loop(ncb, step, 0)
        for g in range(gh):
            ln = slice(g * HEAD, (g + 1) * HEAD)
            o = o_ref[:, ln]
            n = o * lax.rsqrt(jnp.mean(o * o, axis=-1, keepdims=True) + RMS_EPS)
            gr = g_ref[:, ln]
            y_ref[:, ln] = (n * nw_ref[...] * gr * _sigmoid(gr)).astype(BF16)

    col = lambda off: pl.BlockSpec((tb, gw), lambda h, j: (j, off * ngrp + h))
    own = pl.BlockSpec((tb, gw), lambda h, j: (j, h))
    return _hosted_call(
        body, comm, name=name, grid=(ngrp, t // tb),
        in_specs=[col(3), col(4), col(5), col(6), pl.BlockSpec((2, gw), lambda h, j: (0, h)),
                  pl.BlockSpec((1, HEAD), lambda h, j: (0, 0))],
        out_specs=[own, own, pl.BlockSpec((gh, ncb, HEAD, HEAD), lambda h, j: (h, j, 0, 0))],
        out_shape=[SDS((t, w), BF16), SDS((t, w), F32), SDS((nh, t // CHUNK, HEAD, HEAD), BF16)],
        scratch_shapes=[pltpu.VMEM((tb, gw), F32)] + [pltpu.VMEM((HEAD, HEAD), F32)] * gh,
        operands=(z, z, z, z, hg, nw), parallel=1)


def _hgrn_bwd(dy, z, o, states, hg, nw, w, name, comm=()):
    t = z.shape[0]
    nh = w // HEAD
    gh = _tile(nh, HGRN_GROUP, 1)
    gw = gh * HEAD
    ngrp = nh // gh
    tb = _tile(t, HGRN_ROWS, CHUNK)
    ncb = tb // CHUNK
    nt = t // tb

    def body(dy_ref, q_ref, f_ref, i_ref, g_ref, o_ref, st_ref, hg_ref, nw_ref,
             dq_ref, df_ref, di_ref, dg_ref, dhg_ref, dnw_ref, cum_ref, do_ref, *ds_refs):
        lb_all, s1_all = _lower_bound(hg_ref[...])
        row = lax.broadcasted_iota(jnp.int32, (tb, gw), 0)
        crow = lax.broadcasted_iota(jnp.int32, (CHUNK, HEAD), 0)
        cum_ref[...] = _chunk_cumsum(jnp.log(lb_all + (1.0 - lb_all) * _sigmoid(f_ref[...])), row)

        @pl.when(pl.program_id(1) == 0)
        def _():
            for ds_ref in ds_refs:
                ds_ref[...] = jnp.zeros_like(ds_ref)
            dhg_ref[...] = jnp.zeros_like(dhg_ref)
            dnw_ref[...] = jnp.zeros_like(dnw_ref)

        for g in range(gh):
            ln = slice(g * HEAD, (g + 1) * HEAD)
            o_ = o_ref[:, ln]
            rstd = lax.rsqrt(jnp.mean(o_ * o_, axis=-1, keepdims=True) + RMS_EPS)
            n = o_ * rstd
            gr = g_ref[:, ln]
            sg = _sigmoid(gr)
            dy_ = dy_ref[:, ln]
            dg_ref[:, ln] = (dy_ * n * nw_ref[...] * (sg * (1.0 + gr * (1.0 - sg)))).astype(BF16)
            dsil = dy_ * gr * sg
            dnw_ref[:, ln] += jnp.sum(dsil * n, axis=0, keepdims=True)
            dn = dsil * nw_ref[...]
            do_ref[:, ln] = rstd * (dn - n * jnp.mean(dn * n, axis=-1, keepdims=True))

        def step(cc, dlbs):
            c = ncb - 1 - cc
            rows = pl.ds(pl.multiple_of(c * CHUNK, CHUNK), CHUNK)
            new = []
            for g in range(gh):
                ln = slice(g * HEAD, (g + 1) * HEAD)
                lb = lb_all[:, ln]
                qr = q_ref[rows, ln]
                sq = _sigmoid(qr)
                q = qr * sq
                sf = _sigmoid(f_ref[rows, ln])
                f = lb + (1.0 - lb) * sf
                k = 1.0 - f
                cum = cum_ref[rows, ln]
                v = i_ref[rows, ln]
                do = do_ref[rows, ln]
                last = cum[CHUNK - 1:CHUNK, :]
                eg = jnp.exp(cum)
                eng = jnp.exp(-cum)
                elc = jnp.exp(last - cum)
                qe, ke, kl = q * eg, k * eng, k * elc
                ds = ds_refs[g][...]
                a = _tril(_dot_nt(qe, ke))
                da = _tril(_dot_nt(do, v))
                di_ref[rows, ln] = (_dot_tn(a, do) + _dot_nt(kl, ds)).astype(BF16)
                st = st_ref[g, c]
                dkl = _dot_nn(v, ds)
                dq = (_dot_nn(do, st) + _dot_nn(da, ke)) * eg
                dk = _dot_tn(da, qe) * eng + dkl * elc
                el = jnp.exp(last)
                ds_refs[g][...] = ds * el + _dot_tn(do, qe)
                dlast = jnp.sum(kl * dkl, axis=0, keepdims=True) + el * jnp.sum(ds * st.astype(F32), axis=0, keepdims=True)
                x = q * dq - k * dk + jnp.where(crow == CHUNK - 1, dlast, 0.0)
                s = 1
                while s < CHUNK:
                    x = x + _shift_up(x, s, crow, CHUNK)
                    s *= 2
                df = x / f - dk
                dq_ref[rows, ln] = (dq * (sq * (1.0 + qr * (1.0 - sq)))).astype(BF16)
                df_ref[rows, ln] = (df * (1.0 - lb) * sf * (1.0 - sf)).astype(BF16)
                new.append(dlbs[g] + jnp.sum(df * (1.0 - sf), axis=0, keepdims=True))
            return tuple(new)

        dlbs = _unrolled_loop(ncb, step, tuple(jnp.zeros((1, HEAD), F32) for _ in range(gh)))
        for g in range(gh):
            ln = slice(g * HEAD, (g + 1) * HEAD)
            dlb = dlbs[g] * lb_all[:, ln] * s1_all[:, ln]
            dhg_ref[0:1, ln] += dlb
            dhg_ref[1:2, ln] -= dlb

    col = lambda off: pl.BlockSpec((tb, gw), lambda h, j: (nt - 1 - j, off * ngrp + h))
    own = pl.BlockSpec((tb, gw), lambda h, j: (nt - 1 - j, h))
    hsp = pl.BlockSpec((2, gw), lambda h, j: (0, h))
    return _hosted_call(
        body, comm, name=name, grid=(ngrp, nt),
        in_specs=[own, col(3), col(4), col(5), col(6), own,
                  pl.BlockSpec((gh, ncb, HEAD, HEAD), lambda h, j: (h, nt - 1 - j, 0, 0)),
                  hsp, pl.BlockSpec((1, HEAD), lambda h, j: (0, 0))],
        out_specs=[own, own, own, own, hsp, pl.BlockSpec((1, gw), lambda h, j: (0, h))],
        out_shape=[SDS((t, w), BF16)] * 4 + [SDS((2, w), F32), SDS((1, w), F32)],
        scratch_shapes=[pltpu.VMEM((tb, gw), F32)] * 2 + [pltpu.VMEM((HEAD, HEAD), F32)] * gh,
        operands=(dy, z, z, z, z, o, states, hg, nw), parallel=1)


def _cast_pad(wt, n_pad, meta, sp, name, comm=()):
    _, r, n = wt.shape
    g, p, per = meta
    tr = _tile(r, max(16, (3 << 19) // n_pad // 16 * 16), 16)

    def body(w_ref, o_ref):
        if n_pad != n:
            o_ref[...] = jnp.zeros(o_ref.shape, o_ref.dtype)
        o_ref[:, 0:n] = w_ref[...].astype(BF16)

    return _hosted_call(
        body, comm, name=name, grid=(r // tr,), in_specs=[pl.BlockSpec((None, tr, n), lambda i, sp: (0, i, 0))],
        out_specs=[pl.BlockSpec((None, tr, n_pad), lambda i, sp: (sp[1] // per, ((sp[1] % per) * r) // tr + i, 0))],
        out_shape=[SDS((g, p, n_pad), BF16)], scratch_shapes=[], operands=(wt,), parallel=1, prefetch=sp)


def _cast_pad_t(wt_t, n_pad, meta, sp, name, comm=()):
    _, n, r = wt_t.shape
    g, p, per = meta
    tc = _tile(r, 256, LANES)

    def body(w_ref, o_ref):
        for lo in range(0, n_pad, LANES):
            rows = min(LANES, n - lo)
            piece = w_ref[lo:lo + rows, :]
            if rows < LANES:
                piece = jnp.concatenate([piece, jnp.zeros((LANES - rows, tc), F32)], axis=0)
            o_ref[:, lo:lo + LANES] = piece.T.astype(BF16)

    return _hosted_call(
        body, comm, name=name, grid=(r // tc,), in_specs=[pl.BlockSpec((None, n, tc), lambda i, sp: (0, 0, i))],
        out_specs=[pl.BlockSpec((None, tc, n_pad), lambda i, sp: (sp[1] // per, ((sp[1] % per) * r) // tc + i, 0))],
        out_shape=[SDS((g, p, n_pad), BF16)], scratch_shapes=[], operands=(wt_t,), parallel=1, prefetch=sp)


def _adam_math(w, g, m, v):
    m2 = ADAM_B1 * m + (1.0 - ADAM_B1) * g
    v2 = ADAM_B2 * v + (1.0 - ADAM_B2) * (g * g)
    c1 = 1.0 / (1.0 - ADAM_B1 ** ADAM_STEP)
    c2 = 1.0 / (1.0 - ADAM_B2 ** ADAM_STEP)
    return -ADAM_LR * ((m2 * c1) / (jnp.sqrt(v2 * c2) + ADAM_EPS) + ADAM_WD * w), m2, v2


def _adamw_t(wt_t, g, m_t, v_t, name):
    _, n, r = wt_t.shape
    ng = g.shape[1]
    tc = LANES

    def body(w_ref, g_ref, m_ref, v_ref, go_ref, d_ref, mo_ref, vo_ref, gt_ref):
        for lo in range(0, ng, LANES):
            gt_ref[lo:lo + LANES, :] = g_ref[:, lo:lo + LANES].T
        g_ = gt_ref[0:n, :]
        delta, m2, v2 = _adam_math(w_ref[...], g_, m_ref[...], v_ref[...])
        go_ref[...] = g_
        d_ref[...] = delta
        mo_ref[...] = m2
        vo_ref[...] = v2

    blk = pl.BlockSpec((None, n, tc), lambda i: (0, 0, i))
    return pl.pallas_call(
        body, name=name, grid=(r // tc,), in_specs=[blk, pl.BlockSpec((tc, ng), lambda i: (i, 0)), blk, blk],
        out_specs=[blk] * 4, out_shape=[SDS(wt_t.shape, F32)] * 4, scratch_shapes=[pltpu.VMEM((ng, tc), F32)],
        compiler_params=_cparams("parallel"),
    )(wt_t, g, m_t, v_t)


def _adamw(wt, g, m, v, name):
    lead = (None,) * (wt.ndim - 2)
    zero = (0,) * (wt.ndim - 2)
    r, n = wt.shape[-2:]
    ng = g.shape[1]
    nct = 2 if ng == n and n % (2 * LANES) == 0 else 1
    tc, tg = n // nct, ng // nct
    tr = _tile(r, max(8, (3 << 17) // tg // 8 * 8), 8)

    def body(w_ref, g_ref, m_ref, v_ref, go_ref, d_ref, mo_ref, vo_ref):
        g_ = g_ref[:, 0:tc]
        delta, m2, v2 = _adam_math(w_ref[...], g_, m_ref[...], v_ref[...])
        go_ref[...] = g_
        d_ref[...] = delta
        mo_ref[...] = m2
        vo_ref[...] = v2

    blk = pl.BlockSpec(lead + (tr, tc), lambda i, j: zero + (i, j))
    return pl.pallas_call(
        body, name=name, grid=(r // tr, nct), in_specs=[blk, pl.BlockSpec((tr, tg), lambda i, j: (i, j)), blk, blk],
        out_specs=[blk] * 4, out_shape=[SDS(wt.shape, F32)] * 4, compiler_params=_cparams("parallel", "parallel"),
    )(wt, g, m, v)


def _place():
    x, y, c = lax.axis_index("x"), lax.axis_index("y"), lax.axis_index("c")
    return x, y, c, 2 * x + y


def _chip_dev(k, c):
    return (k // 2, k % 2, c)


def _half(ref, j, h, rows, per):
    return ref.at[j // per, pl.ds((j % per) * rows + h * (rows // 2), rows // 2)]


def _gather_stage(bufs, metas, rows_of, ici_parts, fwd_parts, zero_pad):
    nw = len(bufs)
    ici_on = [i for i in range(nw) if ici_parts[i] is not None]
    fwd_on = [i for i in range(nw) if fwd_parts[i] is not None]
    pad_jobs = [(i, gi) for i in ici_on if ici_parts[i][0] == 0 and metas[i][1] > metas[i][2] * rows_of[i]
                for gi in range(metas[i][0])]

    def part_of(ref, i, j, h, part):
        per = metas[i][2]
        p, np_ = part
        pr = rows_of[i] // 2 // np_
        return ref.at[j // per, pl.ds((j % per) * rows_of[i] + h * (rows_of[i] // 2) + p * pr, pr)]

    def descriptors(ins, outs, sems):
        src, zp, dst = ins[:nw], ins[nw], outs
        pads, send, recv, fsend, frecv = sems
        x, y, c, me = _place()

        def pad(n):
            i, gi = pad_jobs[n]
            extra = metas[i][1] - metas[i][2] * rows_of[i]
            return pltpu.make_async_copy(zp.at[pl.ds(0, extra)], dst[i].at[gi, pl.ds(metas[i][2] * rows_of[i], extra)], pads.at[n])

        def ici(i, r, frm):
            return pltpu.make_async_remote_copy(
                src_ref=part_of(src[i], i, me, c, ici_parts[i]), dst_ref=part_of(dst[i], i, frm, c, ici_parts[i]),
                send_sem=send.at[i, r - 1], recv_sem=recv.at[i, r - 1], device_id=_chip_dev((me + r) % N_CHIPS, c),
                device_id_type=MESH)

        def d2d(i, r, frm, h):
            blk = part_of(dst[i], i, frm, h, fwd_parts[i])
            return pltpu.make_async_remote_copy(src_ref=blk, dst_ref=blk, send_sem=fsend.at[i, r - 1],
                                                recv_sem=frecv.at[i, r - 1], device_id=(x, y, 1 - c), device_id_type=MESH)

        return pad, ici, d2d, c, me

    def start(ins, outs, sems):
        pad, ici, d2d, c, me = descriptors(ins, outs, sems)
        for n in range(len(pad_jobs)):
            pad(n).start()
        for i in fwd_on:
            for r in range(1, N_CHIPS):
                d2d(i, r, (me - r) % N_CHIPS, c).start()
        for i in ici_on:
            for r in range(1, N_CHIPS):
                ici(i, r, me).start()

    def finish(ins, outs, sems):
        pad, ici, d2d, c, me = descriptors(ins, outs, sems)
        for i in fwd_on:
            for r in range(1, N_CHIPS):
                d2d(i, r, (me - r) % N_CHIPS, 1 - c).wait_recv()
                d2d(i, r, (me - r) % N_CHIPS, c).wait_send()
        for i in ici_on:
            for r in range(1, N_CHIPS):
                ici(i, r, (me - r) % N_CHIPS).wait_recv()
                ici(i, r, me).wait_send()
        for n in range(len(pad_jobs)):
            pad(n).wait()

    return _Stage(ins=list(bufs) + [zero_pad], out_shapes=[SDS(b.shape, b.dtype) for b in bufs],
                  aliases={i: i for i in range(nw)},
                  sems=[pltpu.SemaphoreType.DMA((max(len(pad_jobs), 1),))] + [pltpu.SemaphoreType.DMA((nw, N_CHIPS - 1))] * 4,
                  start=start, finish=finish)


def _gather_small(packed, name):
    r, n = packed.shape

    def body(src, dst, send, recv):
        x, y, c, me = _place()
        dst[me] = src[...]
        cps = []
        for d in range(1, N_CHIPS):
            cp = pltpu.make_async_remote_copy(src_ref=src, dst_ref=dst.at[me], send_sem=send.at[d - 1], recv_sem=recv.at[d - 1],
                                              device_id=_chip_dev((me + d) % N_CHIPS, c), device_id_type=MESH)
            cp.start()
            cps.append(cp)
        for d in range(1, N_CHIPS):
            pltpu.make_async_remote_copy(src_ref=src, dst_ref=dst.at[(me - d) % N_CHIPS], send_sem=send.at[d - 1],
                                         recv_sem=recv.at[d - 1], device_id=_chip_dev((me + d) % N_CHIPS, c),
                                         device_id_type=MESH).wait_recv()
        for cp in cps:
            cp.wait_send()

    return pl.pallas_call(
        body, name=name, in_specs=[VMEM_SPEC], out_specs=VMEM_SPEC, out_shape=SDS((N_CHIPS, r, n), F32),
        scratch_shapes=[pltpu.SemaphoreType.DMA((N_CHIPS - 1,))] * 2,
    )(packed)


def _all_reduce_small(packed, name):
    r, n = packed.shape

    def body(src, out, slots, send, recv):
        x, y, c, me = _place()
        idx = 2 * me + c
        slots[idx] = src[...]
        cps = []

        def peer(d):
            p = (idx + d) % N_DEV
            return (p // 4, (p // 2) % 2, p % 2)

        for d in range(1, N_DEV):
            cp = pltpu.make_async_remote_copy(src_ref=src, dst_ref=slots.at[idx], send_sem=send.at[d - 1], recv_sem=recv.at[d - 1],
                                              device_id=peer(d), device_id_type=MESH)
            cp.start()
            cps.append(cp)
        for d in range(1, N_DEV):
            pltpu.make_async_remote_copy(src_ref=src, dst_ref=slots.at[(idx - d) % N_DEV], send_sem=send.at[d - 1],
                                         recv_sem=recv.at[d - 1], device_id=peer(d), device_id_type=MESH).wait_recv()
        for cp in cps:
            cp.wait_send()
        acc = slots[0]
        for k in range(1, N_DEV):
            acc = acc + slots[k]
        out[...] = acc

    return pl.pallas_call(
        body, name=name, in_specs=[VMEM_SPEC], out_specs=VMEM_SPEC, out_shape=SDS((r, n), F32),
        scratch_shapes=[pltpu.VMEM((N_DEV, r, n), F32)] + [pltpu.SemaphoreType.DMA((N_DEV - 1,))] * 2,
    )(packed)


def _simple_stage(ins, out_shapes, aliases, n_copies, copies):
    def start(ins_, outs, sems):
        for cp in copies(ins_, outs, *sems):
            cp.start()

    def finish(ins_, outs, sems):
        for cp in copies(ins_, outs, *sems):
            cp.wait()

    return _Stage(ins=list(ins), out_shapes=list(out_shapes), aliases=aliases,
                  sems=[pltpu.SemaphoreType.DMA((n_copies,))] * 2, start=start, finish=finish)


def _rs_pair_exchange(grads, metas, rows_of):
    nw = len(grads)

    def copies(src, dst, send, recv):
        x, y, c, me = _place()
        return [pltpu.make_async_remote_copy(
            src_ref=_half(src[i], j, 1 - c, rows_of[i], metas[i][2]), dst_ref=dst[i].at[j], send_sem=send.at[i * N_CHIPS + j],
            recv_sem=recv.at[i * N_CHIPS + j], device_id=(x, y, 1 - c), device_id_type=MESH)
            for i in range(nw) for j in range(N_CHIPS)]

    out_shapes = [SDS((N_CHIPS, rows_of[i] // 2, g.shape[2]), g.dtype) for i, g in enumerate(grads)]
    return _simple_stage(grads, out_shapes, {}, nw * N_CHIPS, copies)


def _rs_pair_add(g, got, meta, rows, sp, name):
    per = meta[2]
    n = g.shape[2]
    hr = rows // 2
    tr = _tile(hr, max(16, (3 << 19) // n // 16 * 16), 16)

    def body(sp_ref, g_ref, got_ref, snd_ref, own_ref):
        j = pl.program_id(1)
        s = g_ref[...].astype(F32) + got_ref[...].astype(F32)
        snd_ref[...] = s.astype(BF16)

        @pl.when(j == sp_ref[1])
        def _():
            own_ref[...] = s

    grid_spec = pltpu.PrefetchScalarGridSpec(
        num_scalar_prefetch=1, grid=(hr // tr, N_CHIPS),
        in_specs=[pl.BlockSpec((None, tr, n), lambda i, j, sp: (j // per, ((j % per) * rows + sp[0] * hr) // tr + i, 0)),
                  pl.BlockSpec((None, tr, n), lambda i, j, sp: (j, i, 0))],
        out_specs=[pl.BlockSpec((None, tr, n), lambda i, j, sp: (j, i, 0)), pl.BlockSpec((tr, n), lambda i, j, sp: (i, 0))])
    return pl.pallas_call(
        body, name=name, grid_spec=grid_spec, out_shape=[SDS((N_CHIPS, hr, n), BF16), SDS((hr, n), F32)],
        compiler_params=_cparams("parallel", "arbitrary"),
    )(sp, g, got)


def _rs_chip_exchange(sends, part=(0, 1), prev=None):
    nw = len(sends)
    p, np_ = part

    def copies(src, dst, send, recv):
        x, y, c, me = _place()
        cps = []
        for i in range(nw):
            pr = sends[i].shape[1] // np_
            for r in range(1, N_CHIPS):
                cps.append(pltpu.make_async_remote_copy(
                    src_ref=src[i].at[(me + r) % N_CHIPS, pl.ds(p * pr, pr)], dst_ref=dst[i].at[r - 1, pl.ds(p * pr, pr)],
                    send_sem=send.at[i * (N_CHIPS - 1) + r - 1], recv_sem=recv.at[i * (N_CHIPS - 1) + r - 1],
                    device_id=_chip_dev((me + r) % N_CHIPS, c), device_id_type=MESH))
        return cps

    out_shapes = [SDS((N_CHIPS - 1,) + s.shape[1:], BF16) for s in sends]
    if prev is None:
        return _simple_stage(sends, out_shapes, {}, nw * (N_CHIPS - 1), copies)
    return _simple_stage(list(sends) + list(prev), out_shapes, {nw + i: i for i in range(nw)}, nw * (N_CHIPS - 1), copies)


def _rs_chip_add(own, got, sp, name):
    hr, n = own.shape
    tr = _tile(hr, max(16, (3 << 19) // n // 16 * 16), 16)

    def body(sp_ref, own_ref, got_ref, o_ref):
        acc = own_ref[...]
        for r in range(N_CHIPS - 1):
            acc = acc + got_ref[r].astype(F32)
        o_ref[...] = acc

    grid_spec = pltpu.PrefetchScalarGridSpec(
        num_scalar_prefetch=1, grid=(hr // tr,),
        in_specs=[pl.BlockSpec((tr, n), lambda i, sp: (i, 0)), pl.BlockSpec((N_CHIPS - 1, tr, n), lambda i, sp: (0, i, 0))],
        out_specs=pl.BlockSpec((tr, n), lambda i, sp: (sp[0] * (hr // tr) + i, 0)))
    return pl.pallas_call(body, name=name, grid_spec=grid_spec, out_shape=SDS((2 * hr, n), F32),
                          compiler_params=_cparams("parallel"))(sp, own, got)


def _rs_pair_share(blocks):
    nw = len(blocks)

    def copies(src, dst, send, recv):
        x, y, c, me = _place()
        cps = []
        for i in range(nw):
            hr = src[i].shape[0] // 2
            cps.append(pltpu.make_async_remote_copy(
                src_ref=src[i].at[pl.ds(c * hr, hr)], dst_ref=dst[i].at[pl.ds(c * hr, hr)], send_sem=send.at[i],
                recv_sem=recv.at[i], device_id=(x, y, 1 - c), device_id_type=MESH))
        return cps

    return _simple_stage(blocks, [SDS(b.shape, b.dtype) for b in blocks], {i: i for i in range(nw)}, nw, copies)


def kernel(x, p, ln_g, ln_b, ffn1_w_in, ffn1_w_out, mix_w_in, conv_w, hg_lower_bound, hg_norm_w, branch_w_conv, branch_w_hgrn, mix_w_out, ffn2_w_in, ffn2_w_out, ple_w_gate, ple_w_proj, loss_target, m_ln_g, m_ln_b, m_ffn1_w_in, m_ffn1_w_out, m_mix_w_in, m_conv_w, m_hg_lower_bound, m_hg_norm_w, m_branch_w_conv, m_branch_w_hgrn, m_mix_w_out, m_ffn2_w_in, m_ffn2_w_out, m_ple_w_gate, m_ple_w_proj, v_ln_g, v_ln_b, v_ffn1_w_in, v_ffn1_w_out, v_mix_w_in, v_conv_w, v_hg_lower_bound, v_hg_norm_w, v_branch_w_conv, v_branch_w_hgrn, v_mix_w_out, v_ffn2_w_in, v_ffn2_w_out, v_ple_w_gate, v_ple_w_proj):
    assert ln_g.shape[0] == DEPTH and x.shape[0] == 1 and p.shape[:2] == (1, 1)
    t, d = x.shape[1], x.shape[2]
    w = d // 2
    x0 = x.reshape(t, d)
    x0b = _to_bf16(x0, "x_bf16")
    pe = p.reshape(t, p.shape[-1])
    target = loss_target.reshape(t, d)
    cx, cy, cc = lax.axis_index("x"), lax.axis_index("y"), lax.axis_index("c")
    chip = 2 * cx + cy
    sp = jnp.stack([cc, chip]).astype(jnp.int32)

    big = dict(ffn1_w_in=ffn1_w_in, ffn1_w_out=ffn1_w_out, mix_w_in=mix_w_in, branch_w_conv=branch_w_conv,
               branch_w_hgrn=branch_w_hgrn, mix_w_out=mix_w_out, ffn2_w_in=ffn2_w_in, ffn2_w_out=ffn2_w_out,
               ple_w_gate=ple_w_gate, ple_w_proj=ple_w_proj)
    moments = dict(ffn1_w_in=(m_ffn1_w_in, v_ffn1_w_in), ffn1_w_out=(m_ffn1_w_out, v_ffn1_w_out), mix_w_in=(m_mix_w_in, v_mix_w_in),
                   branch_w_conv=(m_branch_w_conv, v_branch_w_conv), branch_w_hgrn=(m_branch_w_hgrn, v_branch_w_hgrn),
                   mix_w_out=(m_mix_w_out, v_mix_w_out), ffn2_w_in=(m_ffn2_w_in, v_ffn2_w_in), ffn2_w_out=(m_ffn2_w_out, v_ffn2_w_out),
                   ple_w_gate=(m_ple_w_gate, v_ple_w_gate), ple_w_proj=(m_ple_w_proj, v_ple_w_proj))
    names = list(big)

    n_loc = ffn1_w_in.shape[-1]
    n_pad = -(-n_loc // LANES) * LANES
    assert mix_w_in.shape[-1] % LANES == 0 and ffn1_w_out.shape[1] * 2 == n_loc
    pad_cols = dict(ffn1_w_in=n_pad, ffn2_w_in=n_pad)
    meta = {k: (N_CHIPS, big[k].shape[1], 1) for k in names}
    meta["ffn1_w_out"] = meta["ffn2_w_out"] = (2, n_pad, 2)
    rows = {k: big[k].shape[1] for k in names}
    swap = lambda a: jnp.transpose(a, (0, 2, 1))
    wbuf = {}
    zero_pad = jnp.zeros((max(n_pad - n_loc, 16), d), BF16)

    def cast(k, comm=()):
        if k in pad_cols:
            return _cast_pad_t(swap(big[k]), pad_cols[k], meta[k], sp, "cast_" + k, comm=comm)
        return _cast_pad(big[k], big[k].shape[2], meta[k], sp, "cast_" + k, comm=comm)

    def gather(ici=(), fwd=()):
        ks = list(dict.fromkeys([k for k, _, _ in ici] + [k for k, _, _ in fwd]))
        ip = {k: (p_, n_) for k, p_, n_ in ici}
        fp = {k: (p_, n_) for k, p_, n_ in fwd}
        return _gather_stage([wbuf[k] for k in ks], [meta[k] for k in ks], [rows[k] for k in ks], [ip.get(k) for k in ks],
                             [fp.get(k) for k in ks], zero_pad), ks

    def gathered(ks, outs):
        wbuf.update(zip(ks, outs))

    def w3(k):
        return wbuf[k]

    def w2(k):
        return wbuf[k].reshape(-1, wbuf[k].shape[2])

    dq, wq = d // N_CHIPS, w // N_CHIPS
    small = jnp.concatenate([ln_g[0], ln_b[0], jnp.pad(conv_w[0], ((0, 5), (0, dq - wq)))], axis=0)
    small = _gather_small(small, "gather_small")
    lng = small[:, 0:4, :].transpose(1, 0, 2).reshape(4, 1, d)
    lnb = small[:, 4:8, :].transpose(1, 0, 2).reshape(4, 1, d)
    cw = small[:, 8:11, :wq].transpose(1, 0, 2).reshape(3, w)
    hg = hg_lower_bound
    nw_ = hg_norm_w

    wbuf["ffn1_w_in"] = cast("ffn1_w_in")
    others = [k for k in names if k != "ffn1_w_in"]
    assert len(others) > FIRST_GATHER_PARTS
    for step, k in enumerate(others):
        ici = [("ffn1_w_in", step, FIRST_GATHER_PARTS)] if step < FIRST_GATHER_PARTS else []
        fwd = [("ffn1_w_in", step - 1, FIRST_GATHER_PARTS)] if 1 <= step <= FIRST_GATHER_PARTS else []
        if ici or fwd:
            st, ks = gather(ici, fwd)
            wbuf[k], got = cast(k, comm=[st])
            gathered(ks, got)
        else:
            wbuf[k] = cast(k)
    one = lambda *ks_: [(k, 0, 1) for k in ks_]
    st, ks = gather(ici=one("ffn1_w_out") + [("mix_w_in", 0, 2)])
    z1, got = _mm(x0b, w3("ffn1_w_in"), name="ffn1_in", b_blocked=True, out_dtype=BF16, tm=1024, comm=[st])
    gathered(ks, got)
    st, ks = gather(fwd=one("ffn1_w_out") + [("mix_w_in", 0, 2)])
    h1, got = _swiglu_fwd(z1, "ffn1_act", comm=[st])
    gathered(ks, got)
    st, ks = gather(ici=[("mix_w_in", 1, 2)])
    y1, got = _mm(h1, w2("ffn1_w_out"), name="ffn1_out", tm=1024, tn=1024, tk=2816, comm=[st])
    gathered(ks, got)
    st, ks = gather(fwd=[("mix_w_in", 1, 2)])
    (r1, x1, x1b), got = _ln_fwd(x0, y1, lng[0], lnb[0], 0.5, "ln0", comm=[st])
    gathered(ks, got)
    mixo_w = one("branch_w_conv", "branch_w_hgrn", "mix_w_out")
    st, ks = gather(ici=mixo_w + [("ffn2_w_in", 0, 2)])
    z, got = _mm(x1b, w3("mix_w_in"), name="mix_in", b_blocked=True, tm=1024, comm=[st])
    gathered(ks, got)
    ya = _conv_fwd(z, cw, w, "conv_fwd")
    st, ks = gather(ici=[("ffn2_w_in", 1, 2)], fwd=mixo_w + [("ffn2_w_in", 0, 2)])
    (yb, o_h, states), got = _hgrn_fwd(z, hg, nw_, w, "hgrn_fwd", comm=[st])
    gathered(ks, got)
    ma = _mm(ya, w3("branch_w_conv"), name="branch_conv", b_blocked=True, tn=512)
    mb = _mm(yb, w3("branch_w_hgrn"), name="branch_hgrn", b_blocked=True, tn=512)
    merged = _merge_fwd(z, ma, mb, w, "merge_fwd")
    st, ks = gather(fwd=[("ffn2_w_in", 1, 2)])
    y2, got = _mm(merged, w2("mix_w_out"), name="mix_out", tn=1024, comm=[st])
    gathered(ks, got)
    r2, x2, x2b = _ln_fwd(x1, y2, lng[1], lnb[1], 1.0, "ln1")
    late = one("ffn2_w_out", "ple_w_gate", "ple_w_proj")
    st, ks = gather(ici=late)
    z3, got = _mm(x2b, w3("ffn2_w_in"), name="ffn2_in", b_blocked=True, out_dtype=BF16, tm=1024, comm=[st])
    gathered(ks, got)
    st, ks = gather(fwd=late)
    h3, got = _swiglu_fwd(z3, "ffn2_act", comm=[st])
    gathered(ks, got)
    y3 = _mm(h3, w2("ffn2_w_out"), name="ffn2_out", tm=1024, tn=1024, tk=2816)
    r3, x3, x3b = _ln_fwd(x2, y3, lng[2], lnb[2], 0.5, "ln2")
    gp = _mm(x3b, w2("ple_w_gate"), name="ple_gate", tn=1024)
    pp = _mm(pe, w3("ple_w_proj"), name="ple_proj", b_blocked=True, tn=512)
    dr4, dgp, dpp, dg3, db3, sq = _tail(x3, gp, pp, lng[3], lnb[3], target, "tail")

    grads, sends, owns, blocks, outs = {}, {}, {}, {}, {}

    def pair_exchange(*ks):
        return _rs_pair_exchange([grads[k] for k in ks], [meta[k] for k in ks], [rows[k] for k in ks])

    def pair_add(ks, got):
        for k, g_ in zip(ks, got):
            sends[k], owns[k] = _rs_pair_add(grads[k], g_, meta[k], rows[k], sp, "rs_pair_add_" + k)

    def chip_exchange(*ks):
        return _rs_chip_exchange([sends[k] for k in ks])

    def chip_add(ks, got):
        for k, g_ in zip(ks, got):
            blocks[k] = _rs_chip_add(owns[k], g_, sp, "rs_chip_add_" + k)

    def pair_share(*ks):
        return _rs_pair_share([blocks[k] for k in ks])

    def update(ks, full):
        for k, g_ in zip(ks, full):
            m_, v_ = moments[k]
            if k in pad_cols:
                outs[k] = [swap(a) for a in _adamw_t(swap(big[k]), g_, swap(m_), swap(v_), "adamw_" + k)]
            else:
                outs[k] = _adamw(big[k], g_, m_, v_, "adamw_" + k)

    ple = ("ple_w_gate", "ple_w_proj")
    mixo = ("mix_w_out", "branch_w_conv", "branch_w_hgrn")
    dx3m = _mm(dgp, w2("ple_w_gate"), name="d_ple_gate_x", tb=True, tn=1024, tk=2048)
    grads["ple_w_gate"] = _mm(x3b, dgp, name="d_ple_gate_w", ta=True, out_dtype=BF16, tm=1024, tk=2048, tn=1024).reshape(N_CHIPS, -1, d)
    grads["ple_w_proj"] = _mm(pe, dpp, name="d_ple_proj_w", ta=True, out_dtype=BF16, out_blocked=N_CHIPS, tk=2048, tn=512)
    dr3, dy3b, dg2, db2 = _ln_bwd(dr4, dx3m, r3, lng[2], 0.5, "ln2_bwd")
    dh3, got = _mm(dy3b, w2("ffn2_w_out"), name="d_ffn2_out_x", tb=True, out_dtype=BF16, tn=1408, tk=2048,
                   comm=[pair_exchange(*ple)])
    pair_add(ple, got)
    g_, got = _mm(h3, dy3b, name="d_ffn2_out_w", ta=True, out_dtype=BF16, tm=1408, tk=2048, tn=1024, comm=[chip_exchange(*ple)])
    grads["ffn2_w_out"] = g_.reshape(2, n_pad, d)
    chip_add(ple, got)
    dz3 = _swiglu_bwd(dh3, z3, "ffn2_act_bwd")
    dx2m, got, full = _mm(dz3, w3("ffn2_w_in"), name="d_ffn2_in_x", tb=True, b_blocked=True, tm=1024, tn=1024, tk=2816,
                          comm=[pair_exchange("ffn2_w_out"), pair_share(*ple)])
    pair_add(["ffn2_w_out"], got)
    update(ple, full)
    grads["ffn2_w_in"], got = _mm(x2b, dz3, name="d_ffn2_in_w", ta=True, out_dtype=BF16, out_blocked=N_CHIPS, tk=4096, comm=[chip_exchange("ffn2_w_out")])
    chip_add(["ffn2_w_out"], got)
    dr2, dy2b, dg1, db1 = _ln_bwd(dr3, dx2m, r2, lng[1], 1.0, "ln1_bwd")
    dmer, got = _mm(dy2b, w2("mix_w_out"), name="d_mix_out_x", tb=True, tn=1024, tk=2048, comm=[pair_exchange("ffn2_w_in")])
    pair_add(["ffn2_w_in"], got)
    g_, full = _mm(merged, dy2b, name="d_mix_out_w", ta=True, out_dtype=BF16, tm=1024, tk=2048, tn=1024, comm=[pair_share("ffn2_w_out")])
    grads["mix_w_out"] = g_.reshape(N_CHIPS, -1, d)
    update(["ffn2_w_out"], full)
    dma, dmb, dgc, dgh = _merge_bwd(dmer, z, ma, mb, w, "merge_bwd")
    dya = _mm(dma, w3("branch_w_conv"), name="d_branch_conv_x", tb=True, b_blocked=True, tn=1024, tk=512)
    dyb = _mm(dmb, w3("branch_w_hgrn"), name="d_branch_hgrn_x", tb=True, b_blocked=True, tn=1024, tk=512)
    grads["branch_w_conv"] = _mm(ya, dma, name="d_branch_conv_w", ta=True, out_dtype=BF16, out_blocked=N_CHIPS, tm=1024, tk=2048, tn=512)
    grads["branch_w_hgrn"] = _mm(yb, dmb, name="d_branch_hgrn_w", ta=True, out_dtype=BF16, out_blocked=N_CHIPS, tm=1024, tk=2048, tn=512)
    dbg, dcg, dhc, dcw = _conv_bwd(dya, z, cw, w, "conv_bwd")
    (dq_, df_, di_, dgr_, dhg, dnw), got2, got = _hgrn_bwd(dyb, z, o_h, states, hg, nw_, w, "hgrn_bwd",
                                                            comm=[chip_exchange("ffn2_w_in"), pair_exchange(*mixo)])
    chip_add(["ffn2_w_in"], got2)
    pair_add(mixo, got)
    dz = _concat_cols([dbg, dcg, dhc, dq_, df_, di_, dgr_, dgc, dgh], "dz_concat")
    dx1m, full, got = _mm(dz, w3("mix_w_in"), name="d_mix_in_x", tb=True, b_blocked=True, tm=1024, tn=1024, tk=2816,
                          comm=[pair_share("ffn2_w_in"), chip_exchange(*mixo)])
    update(["ffn2_w_in"], full)
    chip_add(mixo, got)
    grads["mix_w_in"], full = _mm(x1b, dz, name="d_mix_in_w", ta=True, out_dtype=BF16, out_blocked=N_CHIPS, tk=4096, comm=[pair_share(*mixo)])
    update(mixo, full)
    dr1, dy1b, dg0, db0 = _ln_bwd(dr2, dx1m, r1, lng[0], 0.5, "ln0_bwd")
    dh1, got = _mm(dy1b, w2("ffn1_w_out"), name="d_ffn1_out_x", tb=True, out_dtype=BF16, tn=1408, tk=2048,
                   comm=[pair_exchange("mix_w_in")])
    pair_add(["mix_w_in"], got)
    mix_sends = [sends["mix_w_in"]]
    g_, got_a = _mm(h1, dy1b, name="d_ffn1_out_w", ta=True, out_dtype=BF16, tm=1408, tk=2048, tn=1024, comm=[_rs_chip_exchange(mix_sends, (0, 2))])
    grads["ffn1_w_out"] = g_.reshape(2, n_pad, d)
    dz1 = _swiglu_bwd(dh1, z1, "ffn1_act_bwd")
    g_other, got2, got = _mm(x0b, dz1, name="d_ffn1_in_w_other", ta=True, out_dtype=BF16, out_blocked=N_CHIPS, tk=4096, half=(sp, True),
                             comm=[_rs_chip_exchange(mix_sends, (1, 2), got_a), pair_exchange("ffn1_w_out")])
    chip_add(["mix_w_in"], got2)
    pair_add(["ffn1_w_out"], got)
    grads["ffn1_w_in"], full, got2, got = _mm(
        x0b, dz1, name="d_ffn1_in_w_own", ta=True, out_dtype=BF16, out_blocked=N_CHIPS, tk=4096, half=(sp, False),
        comm=[pair_share("mix_w_in"), chip_exchange("ffn1_w_out"),
              _rs_pair_exchange([g_other], [meta["ffn1_w_in"]], [rows["ffn1_w_in"]])])
    update(["mix_w_in"], full)
    chip_add(["ffn1_w_out"], got2)
    pair_add(["ffn1_w_in"], got)
    dx0, got2, full = _mm(dz1, w3("ffn1_w_in"), name="d_ffn1_in_x", tb=True, b_blocked=True, tm=1024, tn=1024, tk=2816,
                          add=(dr1, ALPHA), comm=[chip_exchange("ffn1_w_in"), pair_share("ffn1_w_out")])
    chip_add(["ffn1_w_in"], got2)
    update(["ffn1_w_out"], full)
    grad_x = dx0.reshape(x.shape)
    update(["ffn1_w_in"], _run_stages([pair_share("ffn1_w_in")], "rs_tail_pair")[0])

    pack = jnp.concatenate([
        dg0, dg1, dg2, dg3, db0, db1, db2, db3,
        jnp.pad(dcw, ((0, 0), (0, d - w))), jnp.pad(dhg, ((0, 0), (0, d - w))),
        jnp.pad(jnp.sum(dnw.reshape(-1, HEAD), axis=0, keepdims=True), ((0, 0), (0, d - HEAD))), sq], axis=0)
    pack = _all_reduce_small(jnp.pad(pack, ((0, 1), (0, 0))), "reduce_small")
    loss = (0.5 / d) * jnp.sum(pack[14])
    g_ln_g = lax.dynamic_slice_in_dim(pack[0:4], chip * dq, dq, axis=1)
    g_ln_b = lax.dynamic_slice_in_dim(pack[4:8], chip * dq, dq, axis=1)
    g_conv = lax.dynamic_slice_in_dim(pack[8:11, :w], chip * wq, wq, axis=1)
    g_hg = pack[11:13, :w]
    g_nw = pack[13:14, :HEAD]

    small_w = dict(ln_g=(ln_g, g_ln_g, m_ln_g, v_ln_g), ln_b=(ln_b, g_ln_b, m_ln_b, v_ln_b),
                   conv_w=(conv_w, g_conv, m_conv_w, v_conv_w), hg_lower_bound=(hg_lower_bound, g_hg, m_hg_lower_bound, v_hg_lower_bound),
                   hg_norm_w=(hg_norm_w, g_nw, m_hg_norm_w, v_hg_norm_w))
    for k, (w_, g_, m_, v_) in small_w.items():
        outs[k] = _adamw(w_, g_.reshape(-1, w_.shape[-1]), m_, v_, "adamw_" + k)

    order = ["ln_g", "ln_b", "ffn1_w_in", "ffn1_w_out", "mix_w_in", "conv_w", "hg_lower_bound", "hg_norm_w", "branch_w_conv",
             "branch_w_hgrn", "mix_w_out", "ffn2_w_in", "ffn2_w_out", "ple_w_gate", "ple_w_proj"]
    return (loss, grad_x, *[outs[k][0] for k in order], *[outs[k][1] for k in order], *[outs[k][2] for k in order],
            *[outs[k][3] for k in order])
```

```python
import collections
import functools

import jax
import jax.numpy as jnp
from jax import lax
from jax.experimental import pallas as pl
from jax.experimental.pallas import tpu as pltpu

F32 = jnp.float32
BF16 = jnp.bfloat16
MESH = pl.DeviceIdType.MESH
ANY = pl.BlockSpec(memory_space=pl.ANY)
VMEM_SPEC = pl.BlockSpec(memory_space=pltpu.VMEM)
SDS = jax.ShapeDtypeStruct

DEPTH = 1
ALPHA = (2.0 * DEPTH) ** 0.25
LN_EPS = 1e-5
RMS_EPS = 1e-6
CHUNK = 32
HEAD = 128
ADAM_LR, ADAM_B1, ADAM_B2, ADAM_EPS, ADAM_WD, ADAM_STEP = 0.001, 0.9, 0.999, 1e-08, 0.01, 10

LANES = 128
N_CHIPS = 4
N_DEV = 8
FIRST_GATHER_PARTS = 8
VMEM_LIMIT = 52 * 1024 * 1024
MM_PIECE = 512


def _cparams(*sem):
    if sem:
        return pltpu.CompilerParams(dimension_semantics=sem, vmem_limit_bytes=VMEM_LIMIT)
    return pltpu.CompilerParams(vmem_limit_bytes=VMEM_LIMIT)


def _tile(n, target, mult):
    best = None
    for t in range(mult, min(n, target) + 1, mult):
        if n % t == 0:
            best = t
    return best if best is not None else n


def _sigmoid(x):
    return 1.0 / (1.0 + jnp.exp(-x))


_Stage = collections.namedtuple("_Stage", "ins out_shapes aliases sems start finish")


def _hosted_call(compute, stages, *, name, grid, in_specs, out_specs, out_shape, scratch_shapes, operands, parallel,
                 prefetch=None):
    n_cmp, n_out, n_scr = len(in_specs), len(out_specs), len(scratch_shapes)
    n_in = n_cmp
    n_pre = int(prefetch is not None)
    c_in = [len(s.ins) for s in stages]
    c_out = [len(s.out_shapes) for s in stages]
    c_sem = [len(s.sems) for s in stages]
    aliases = {}
    for si, s in enumerate(stages):
        for a_in, a_out in s.aliases.items():
            aliases[n_pre + n_in + sum(c_in[:si]) + a_in] = n_out + sum(c_out[:si]) + a_out

    def body(*refs):
        refs = refs[n_pre:]
        ins = refs[:n_cmp]
        cins = refs[n_in:n_in + sum(c_in)]
        outs = refs[n_in + sum(c_in):n_in + sum(c_in) + n_out]
        couts = refs[n_in + sum(c_in) + n_out:n_in + sum(c_in) + n_out + sum(c_out)]
        scr = refs[n_in + sum(c_in) + n_out + sum(c_out):][:n_scr]
        sems = refs[n_in + sum(c_in) + n_out + sum(c_out) + n_scr:]

        def stage_refs(si):
            return (cins[sum(c_in[:si]):sum(c_in[:si + 1])], couts[sum(c_out[:si]):sum(c_out[:si + 1])],
                    sems[sum(c_sem[:si]):sum(c_sem[:si + 1])])

        if stages:
            first = functools.reduce(jnp.logical_and, [pl.program_id(ax) == 0 for ax in range(len(grid))])
            last = functools.reduce(jnp.logical_and, [pl.program_id(ax) == grid[ax] - 1 for ax in range(len(grid))])

            @pl.when(first)
            def _():
                for si, s in enumerate(stages):
                    s.start(*stage_refs(si))

        compute(*ins, *outs, *scr)
        if stages:
            @pl.when(last)
            def _():
                for si, s in enumerate(stages):
                    s.finish(*stage_refs(si))

    sem = ("arbitrary",) * len(grid) if stages else ("parallel",) * parallel + ("arbitrary",) * (len(grid) - parallel)
    all_in = list(in_specs) + [ANY] * (n_in - n_cmp + sum(c_in))
    all_out = list(out_specs) + [ANY] * sum(c_out)
    all_scr = list(scratch_shapes) + [q for s in stages for q in s.sems]
    all_shape = list(out_shape) + [o for s in stages for o in s.out_shapes]
    args = list(operands) + [a for s in stages for a in s.ins]
    if prefetch is None:
        res = pl.pallas_call(body, name=name, grid=grid, in_specs=all_in, out_specs=all_out, out_shape=all_shape,
                             input_output_aliases=aliases, scratch_shapes=all_scr, compiler_params=_cparams(*sem))(*args)
    else:
        grid_spec = pltpu.PrefetchScalarGridSpec(num_scalar_prefetch=1, grid=grid, in_specs=all_in, out_specs=all_out,
                                                 scratch_shapes=all_scr)
        res = pl.pallas_call(body, name=name, grid_spec=grid_spec, out_shape=all_shape, input_output_aliases=aliases,
                             compiler_params=_cparams(*sem))(prefetch, *args)
    main = res[0] if n_out == 1 else list(res[:n_out])
    if not stages:
        return main
    rest = res[n_out:]
    return (main, *[list(rest[sum(c_out[:si]):sum(c_out[:si + 1])]) for si in range(len(stages))])


def _run_stages(stages, name):
    def body(*refs):
        n_i = sum(len(s.ins) for s in stages)
        n_o = sum(len(s.out_shapes) for s in stages)
        cins, couts, sems = refs[:n_i], refs[n_i:n_i + n_o], refs[n_i + n_o:]
        pos = [0, 0, 0]
        parts = []
        for s in stages:
            parts.append((cins[pos[0]:pos[0] + len(s.ins)], couts[pos[1]:pos[1] + len(s.out_shapes)], sems[pos[2]:pos[2] + len(s.sems)]))
            pos = [pos[0] + len(s.ins), pos[1] + len(s.out_shapes), pos[2] + len(s.sems)]
        for s, p_ in zip(stages, parts):
            s.start(*p_)
        for s, p_ in zip(stages, parts):
            s.finish(*p_)

    aliases, ni, no = {}, 0, 0
    for s in stages:
        for a_in, a_out in s.aliases.items():
            aliases[ni + a_in] = no + a_out
        ni, no = ni + len(s.ins), no + len(s.out_shapes)
    res = pl.pallas_call(
        body, name=name, in_specs=[ANY] * ni, out_specs=[ANY] * no, out_shape=[o for s in stages for o in s.out_shapes],
        input_output_aliases=aliases, scratch_shapes=[q for s in stages for q in s.sems],
    )(*[a for s in stages for a in s.ins])
    out, pos = [], 0
    for s in stages:
        out.append(list(res[pos:pos + len(s.out_shapes)]))
        pos += len(s.out_shapes)
    return out


def _mm(a, b, *, name, ta=False, tb=False, b_blocked=False, out_blocked=0, out_dtype=F32,
        tm=512, tn=1408, tk=2048, comm=(), half=None, add=None):
    if ta:
        kd, m = a.shape
    else:
        m, kd = a.shape
    if b_blocked and not tb:
        g, kb, nb = b.shape
        assert kb == kd
        n = g * nb
        tn = _tile(nb, tn, LANES)
        tk = _tile(kd, tk, LANES)
        per_n = nb // tn
        b_spec = pl.BlockSpec((None, tk, tn), lambda i, j, k, *s: (j // per_n, k, j % per_n))
    elif b_blocked and tb:
        g, n, kb = b.shape
        assert g * kb == kd
        tn = _tile(n, tn, LANES)
        tk = _tile(kb, tk, LANES)
        per_k = kb // tk
        b_spec = pl.BlockSpec((None, tn, tk), lambda i, j, k, *s: (k // per_k, j, k % per_k))
    elif tb:
        n, kb = b.shape
        assert kb == kd
        tn = _tile(n, tn, LANES)
        tk = _tile(kd, tk, LANES)
        b_spec = pl.BlockSpec((tn, tk), lambda i, j, k, *s: (j, k))
    else:
        kb, n = b.shape
        assert kb == kd
        tn = _tile(n // out_blocked if out_blocked else n, tn, LANES)
        per_o = (n // out_blocked) // tn if out_blocked else None
        tk = _tile(kd, tk, LANES)
        b_spec = pl.BlockSpec((tk, tn), lambda i, j, k, *s: (k, j))
    m_run = m // 2 if half else m
    tm = _tile(m_run, tm, LANES if ta else 8)

    def row(i, s):
        if not half:
            return i
        h = 1 - s[0][0] if half[1] else s[0][0]
        return h * (m_run // tm) + i

    if ta:
        a_spec = pl.BlockSpec((tk, tm), lambda i, j, k, *s: (k, row(i, s)))
    else:
        a_spec = pl.BlockSpec((tm, tk), lambda i, j, k, *s: (row(i, s), k))
    if out_blocked:
        assert not b_blocked and not tb
        o_spec = pl.BlockSpec((None, tm, tn), lambda i, j, k, *s: (j // per_o, row(i, s), j % per_o))
        o_shape = SDS((out_blocked, m, n // out_blocked), out_dtype)
    else:
        o_spec = pl.BlockSpec((tm, tn), lambda i, j, k, *s: (row(i, s), j))
        o_shape = SDS((m, n), out_dtype)
    nk = kd // tk
    dn = (((0 if ta else 1,), (1 if tb else 0,)), ((), ()))
    grid = (m_run // tm, n // tn, nk)

    pieces = [(lo, min(MM_PIECE, tn - lo)) for lo in range(0, tn, MM_PIECE)]

    def compute(a_ref, b_ref, *rest):
        add_ref = rest[0] if add else None
        o_ref, acc_ref = rest[-2:]
        a_tile = a_ref[...].astype(BF16)
        k = pl.program_id(2)

        def result(acc, cols):
            if add:
                acc = acc + add[1] * add_ref[:, cols]
            return acc.astype(o_ref.dtype)

        if nk > 1:
            @pl.when(k == 0)
            def _():
                acc_ref[...] = jnp.zeros_like(acc_ref)

        for lo, wd in pieces:
            cols = slice(lo, lo + wd)
            b_tile = b_ref[cols, :] if tb else b_ref[:, cols]
            part = lax.dot_general(a_tile, b_tile.astype(BF16), dn, preferred_element_type=F32)
            if nk == 1:
                o_ref[:, cols] = result(part, cols)
            else:
                acc_ref[:, cols] += part

        if nk > 1:
            @pl.when(k == nk - 1)
            def _():
                o_ref[...] = result(acc_ref[...], slice(None))

    extra = [(add[0], o_spec)] if add else []
    return _hosted_call(compute, comm, name=name, grid=grid, in_specs=[a_spec, b_spec] + [s_ for _, s_ in extra], out_specs=[o_spec],
                        out_shape=[o_shape], scratch_shapes=[pltpu.VMEM((tm, tn), F32)], operands=(a, b, *[a_ for a_, _ in extra]),
                        parallel=2, prefetch=half[0] if half else None)


def _swiglu_fwd(z, name, comm=()):
    t, n = z.shape
    n2 = n // 2
    tr = _tile(t, 128, 16)

    def body(a_ref, u_ref, o_ref):
        a = a_ref[...].astype(F32)
        o_ref[...] = (a * _sigmoid(a) * u_ref[...].astype(F32)).astype(o_ref.dtype)

    return _hosted_call(
        body, comm, name=name, grid=(t // tr,),
        in_specs=[pl.BlockSpec((tr, n2), lambda i: (i, 0)), pl.BlockSpec((tr, n2), lambda i: (i, 1))],
        out_specs=[pl.BlockSpec((tr, n2), lambda i: (i, 0))], out_shape=[SDS((t, n2), BF16)], scratch_shapes=[],
        operands=(z, z), parallel=1)


def _swiglu_bwd(dh, z, name):
    t, n = z.shape
    n2 = n // 2
    tr = _tile(t, 128, 16)

    def body(dh_ref, a_ref, u_ref, o_ref):
        a = a_ref[...].astype(F32)
        dh_ = dh_ref[...].astype(F32)
        s = _sigmoid(a)
        o_ref[:, 0:n2] = (dh_ * u_ref[...].astype(F32) * (s * (1.0 + a * (1.0 - s)))).astype(o_ref.dtype)
        o_ref[:, n2:n] = (dh_ * a * s).astype(o_ref.dtype)

    return pl.pallas_call(
        body, name=name, grid=(t // tr,),
        in_specs=[pl.BlockSpec((tr, n2), lambda i: (i, 0)), pl.BlockSpec((tr, n2), lambda i: (i, 0)),
                  pl.BlockSpec((tr, n2), lambda i: (i, 1))],
        out_specs=pl.BlockSpec((tr, n), lambda i: (i, 0)), out_shape=SDS((t, n), BF16),
        compiler_params=_cparams("parallel"),
    )(dh, z, z)


def _ln_stats(r):
    mu = jnp.mean(r, axis=-1, keepdims=True)
    xc = r - mu
    var = jnp.mean(xc * xc, axis=-1, keepdims=True)
    return xc * lax.rsqrt(var + LN_EPS)


def _ln_fwd(xp, y, g, b, scale, name, comm=()):
    t, d = xp.shape
    tr = _tile(t, 256, 16)

    def body(xp_ref, y_ref, g_ref, b_ref, r_ref, x_ref, xb_ref):
        r = ALPHA * xp_ref[...] + scale * y_ref[...]
        x = _ln_stats(r) * g_ref[...] + b_ref[...]
        r_ref[...] = r
        x_ref[...] = x
        xb_ref[...] = x.astype(BF16)

    row = pl.BlockSpec((tr, d), lambda i: (i, 0))
    vec = pl.BlockSpec((1, d), lambda i: (0, 0))
    return _hosted_call(
        body, comm, name=name, grid=(t // tr,), in_specs=[row, row, vec, vec], out_specs=[row, row, row],
        out_shape=[SDS((t, d), F32), SDS((t, d), F32), SDS((t, d), BF16)], scratch_shapes=[], operands=(xp, y, g, b), parallel=1)


def _ln_bwd(dra, dxm, r, g, scale, name):
    t, d = r.shape
    tr = _tile(t, 256, 16)

    def body(dra_ref, dxm_ref, r_ref, g_ref, dr_ref, dyb_ref, dg_ref, db_ref):
        i = pl.program_id(0)
        dx = ALPHA * dra_ref[...] + dxm_ref[...]
        rr = r_ref[...]
        mu = jnp.mean(rr, axis=-1, keepdims=True)
        xc = rr - mu
        rstd = lax.rsqrt(jnp.mean(xc * xc, axis=-1, keepdims=True) + LN_EPS)
        xh = xc * rstd
        dxh = dx * g_ref[...]
        dr = rstd * (dxh - jnp.mean(dxh, axis=-1, keepdims=True) - xh * jnp.mean(dxh * xh, axis=-1, keepdims=True))
        dr_ref[...] = dr
        dyb_ref[...] = (scale * dr).astype(BF16)
        dg = jnp.sum(dx * xh, axis=0, keepdims=True)
        db = jnp.sum(dx, axis=0, keepdims=True)

        @pl.when(i == 0)
        def _():
            dg_ref[...] = dg
            db_ref[...] = db

        @pl.when(i > 0)
        def _():
            dg_ref[...] += dg
            db_ref[...] += db

    row = pl.BlockSpec((tr, d), lambda i: (i, 0))
    vec = pl.BlockSpec((1, d), lambda i: (0, 0))
    return pl.pallas_call(
        body, name=name, grid=(t // tr,), in_specs=[row, row, row, vec], out_specs=[row, row, vec, vec],
        out_shape=[SDS((t, d), F32), SDS((t, d), BF16), SDS((1, d), F32), SDS((1, d), F32)],
        compiler_params=_cparams("arbitrary"),
    )(dra, dxm, r, g)


def _tail(x3, gp, pp, g, b, target, name):
    t, d = x3.shape
    tr = _tile(t, 256, 16)

    def body(x3_ref, gp_ref, pp_ref, g_ref, b_ref, tg_ref, dr_ref, dgp_ref, dpp_ref, dg_ref, db_ref, sq_ref):
        i = pl.program_id(0)
        gate = _sigmoid(gp_ref[...])
        pp_ = pp_ref[...]
        r = ALPHA * x3_ref[...] + gate * pp_
        mu = jnp.mean(r, axis=-1, keepdims=True)
        xc = r - mu
        rstd = lax.rsqrt(jnp.mean(xc * xc, axis=-1, keepdims=True) + LN_EPS)
        xh = xc * rstd
        err = xh * g_ref[...] + b_ref[...] - tg_ref[...]
        dx = err * (1.0 / d)
        dxh = dx * g_ref[...]
        dr = rstd * (dxh - jnp.mean(dxh, axis=-1, keepdims=True) - xh * jnp.mean(dxh * xh, axis=-1, keepdims=True))
        dr_ref[...] = dr
        dgp_ref[...] = (dr * pp_ * gate * (1.0 - gate)).astype(BF16)
        dpp_ref[...] = (dr * gate).astype(BF16)
        dg = jnp.sum(dx * xh, axis=0, keepdims=True)
        db = jnp.sum(dx, axis=0, keepdims=True)
        sq = jnp.sum(err * err, axis=0, keepdims=True)

        @pl.when(i == 0)
        def _():
            dg_ref[...] = dg
            db_ref[...] = db
            sq_ref[...] = sq

        @pl.when(i > 0)
        def _():
            dg_ref[...] += dg
            db_ref[...] += db
            sq_ref[...] += sq

    row = pl.BlockSpec((tr, d), lambda i: (i, 0))
    vec = pl.BlockSpec((1, d), lambda i: (0, 0))
    return pl.pallas_call(
        body, name=name, grid=(t // tr,), in_specs=[row, row, row, vec, vec, row],
        out_specs=[row, row, row, vec, vec, vec],
        out_shape=[SDS((t, d), F32), SDS((t, d), BF16), SDS((t, d), BF16), SDS((1, d), F32), SDS((1, d), F32),
                   SDS((1, d), F32)],
        compiler_params=_cparams("arbitrary"),
    )(x3, gp, pp, g, b, target)


def _to_bf16(x, name):
    t, d = x.shape
    tr = _tile(t, 512, 16)
    row = pl.BlockSpec((tr, d), lambda i: (i, 0))

    def body(x_ref, o_ref):
        o_ref[...] = x_ref[...].astype(BF16)

    return pl.pallas_call(body, name=name, grid=(t // tr,), in_specs=[row], out_specs=row, out_shape=SDS((t, d), BF16),
                          compiler_params=_cparams("parallel"))(x)


def _concat_cols(parts, name):
    t = parts[0].shape[0]
    widths = [p_.shape[1] for p_ in parts]
    tr = _tile(t, 256, 16)

    def body(*refs):
        o_ref = refs[-1]
        at = 0
        for ref, wd in zip(refs[:-1], widths):
            o_ref[:, at:at + wd] = ref[...]
            at += wd

    return pl.pallas_call(
        body, name=name, grid=(t // tr,), in_specs=[pl.BlockSpec((tr, wd), lambda i: (i, 0)) for wd in widths],
        out_specs=pl.BlockSpec((tr, sum(widths)), lambda i: (i, 0)), out_shape=SDS((t, sum(widths)), parts[0].dtype),
        compiler_params=_cparams("parallel"),
    )(*parts)


def _merge_fwd(z, ma, mb, w, name):
    t = z.shape[0]
    tr = _tile(t, 256, 16)

    def body(gc_ref, gh_ref, ma_ref, mb_ref, o_ref):
        o_ref[...] = (_sigmoid(gc_ref[...]) * ma_ref[...] + _sigmoid(gh_ref[...]) * mb_ref[...]).astype(BF16)

    half = pl.BlockSpec((tr, w), lambda i, j: (i, j))
    return pl.pallas_call(
        body, name=name, grid=(t // tr, 2),
        in_specs=[pl.BlockSpec((tr, w), lambda i, j: (i, 7 + j)), pl.BlockSpec((tr, w), lambda i, j: (i, 9 + j)), half, half],
        out_specs=half, out_shape=SDS((t, 2 * w), BF16), compiler_params=_cparams("parallel", "parallel"),
    )(z, z, ma, mb)


def _merge_bwd(dmer, z, ma, mb, w, name):
    t = z.shape[0]
    tr = _tile(t, 256, 16)

    def body(d_ref, gc_ref, gh_ref, ma_ref, mb_ref, dma_ref, dmb_ref, dgc_ref, dgh_ref):
        dm = d_ref[...]
        sc = _sigmoid(gc_ref[...])
        sh = _sigmoid(gh_ref[...])
        dma_ref[...] = (dm * sc).astype(BF16)
        dmb_ref[...] = (dm * sh).astype(BF16)
        dgc_ref[...] = (dm * ma_ref[...] * sc * (1.0 - sc)).astype(BF16)
        dgh_ref[...] = (dm * mb_ref[...] * sh * (1.0 - sh)).astype(BF16)

    half = pl.BlockSpec((tr, w), lambda i, j: (i, j))
    return pl.pallas_call(
        body, name=name, grid=(t // tr, 2),
        in_specs=[half, pl.BlockSpec((tr, w), lambda i, j: (i, 7 + j)), pl.BlockSpec((tr, w), lambda i, j: (i, 9 + j)), half, half],
        out_specs=[half] * 4, out_shape=[SDS((t, 2 * w), BF16)] * 4, compiler_params=_cparams("parallel", "parallel"),
    )(dmer, z, z, ma, mb)


def _shift_down(x, s, row):
    return jnp.where(row >= s, pltpu.roll(x, s, axis=0), 0.0)


def _shift_up(x, s, row, t):
    return jnp.where(row < t - s, pltpu.roll(x, t - s, axis=0), 0.0)


def _conv_fwd(z, cw, w, name):
    t = z.shape[0]
    tc = LANES
    nb = w // tc

    def body(b_ref, c_ref, h_ref, w_ref, o_ref):
        u = c_ref[...] * h_ref[...]
        row = lax.broadcasted_iota(jnp.int32, u.shape, 0)
        cw_ = w_ref[...]
        conv = cw_[2:3, :] * u + cw_[1:2, :] * _shift_down(u, 1, row) + cw_[0:1, :] * _shift_down(u, 2, row)
        o_ref[...] = (b_ref[...] * conv).astype(BF16)

    col = lambda off: pl.BlockSpec((t, tc), lambda j: (0, off * nb + j))
    return pl.pallas_call(
        body, name=name, grid=(nb,), in_specs=[col(0), col(1), col(2), pl.BlockSpec((3, tc), lambda j: (0, j))],
        out_specs=pl.BlockSpec((t, tc), lambda j: (0, j)), out_shape=SDS((t, w), BF16), compiler_params=_cparams("parallel"),
    )(z, z, z, cw)


def _conv_bwd(dy, z, cw, w, name):
    t = z.shape[0]
    tc = LANES
    nb = w // tc

    def body(dy_ref, b_ref, c_ref, h_ref, w_ref, db_ref, dc_ref, dh_ref, dw_ref):
        c_, h_ = c_ref[...], h_ref[...]
        u = c_ * h_
        row = lax.broadcasted_iota(jnp.int32, u.shape, 0)
        cw_ = w_ref[...]
        u1 = _shift_down(u, 1, row)
        u2 = _shift_down(u, 2, row)
        dy_ = dy_ref[...]
        db_ref[...] = (dy_ * (cw_[2:3, :] * u + cw_[1:2, :] * u1 + cw_[0:1, :] * u2)).astype(BF16)
        dconv = dy_ * b_ref[...]
        du = cw_[2:3, :] * dconv + cw_[1:2, :] * _shift_up(dconv, 1, row, t) + cw_[0:1, :] * _shift_up(dconv, 2, row, t)
        dc_ref[...] = (du * h_).astype(BF16)
        dh_ref[...] = (du * c_).astype(BF16)
        dw_ref[0:1, :] = jnp.sum(dconv * u2, axis=0, keepdims=True)
        dw_ref[1:2, :] = jnp.sum(dconv * u1, axis=0, keepdims=True)
        dw_ref[2:3, :] = jnp.sum(dconv * u, axis=0, keepdims=True)

    col = lambda off: pl.BlockSpec((t, tc), lambda j: (0, off * nb + j))
    own = pl.BlockSpec((t, tc), lambda j: (0, j))
    wsp = pl.BlockSpec((3, tc), lambda j: (0, j))
    return pl.pallas_call(
        body, name=name, grid=(nb,), in_specs=[own, col(0), col(1), col(2), wsp], out_specs=[own, own, own, wsp],
        out_shape=[SDS((t, w), BF16)] * 3 + [SDS((3, w), F32)], compiler_params=_cparams("parallel"),
    )(dy, z, z, z, cw)


def _lower_bound(hg):
    mx = jnp.max(hg, axis=0, keepdims=True)
    e = jnp.exp(hg - mx)
    inv = 1.0 / jnp.sum(e, axis=0, keepdims=True)
    return e[0:1, :] * inv, e[1:2, :] * inv


def _chunk_cumsum(x, row):
    s = 1
    while s < CHUNK:
        x = x + jnp.where(row % CHUNK >= s, pltpu.roll(x, s, axis=0), 0.0)
        s *= 2
    return x


def _dot_nt(a, b):
    return lax.dot_general(a.astype(BF16), b.astype(BF16), (((1,), (1,)), ((), ())), preferred_element_type=F32)


def _dot_tn(a, b):
    return lax.dot_general(a.astype(BF16), b.astype(BF16), (((0,), (0,)), ((), ())), preferred_element_type=F32)


def _dot_nn(a, b):
    return jnp.dot(a.astype(BF16), b.astype(BF16), preferred_element_type=F32)


def _tril(x):
    r = lax.broadcasted_iota(jnp.int32, x.shape, 0)
    c = lax.broadcasted_iota(jnp.int32, x.shape, 1)
    return jnp.where(r >= c, x, 0.0)


HGRN_GROUP = 4
HGRN_ROWS = 512
HGRN_UNROLL = 2


def _unrolled_loop(n, step, init):
    assert n % HGRN_UNROLL == 0

    def trip(i, carry):
        for u in range(HGRN_UNROLL):
            carry = step(i * HGRN_UNROLL + u, carry)
        return carry

    return lax.fori_loop(0, n // HGRN_UNROLL, trip, init)


def _hgrn_chunk_inputs(q_ref, f_ref, cum_ref, lb, rows, ln):
    qr = q_ref[rows, ln]
    q = qr * _sigmoid(qr)
    f = lb + (1.0 - lb) * _sigmoid(f_ref[rows, ln])
    return q, 1.0 - f, cum_ref[rows, ln]


def _hgrn_fwd(z, hg, nw, w, name, comm=()):
    t = z.shape[0]
    nh = w // HEAD
    gh = _tile(nh, HGRN_GROUP, 1)
    gw = gh * HEAD
    ngrp = nh // gh
    tb = _tile(t, HGRN_ROWS, CHUNK)
    ncb = tb // CHUNK

    def body(q_ref, f_ref, i_ref, g_ref, hg_ref, nw_ref, y_ref, o_ref, st_ref, cum_ref, *s_refs):
        lb_all, _ = _lower_bound(hg_ref[...])
        row = lax.broadcasted_iota(jnp.int32, (tb, gw), 0)
        cum_ref[...] = _chunk_cumsum(jnp.log(lb_all + (1.0 - lb_all) * _sigmoid(f_ref[...])), row)

        @pl.when(pl.program_id(1) == 0)
        def _():
            for s_ref in s_refs:
                s_ref[...] = jnp.zeros_like(s_ref)

        def step(c, carry):
            rows = pl.ds(pl.multiple_of(c * CHUNK, CHUNK), CHUNK)
            for g in range(gh):
                ln = slice(g * HEAD, (g + 1) * HEAD)
                lb = lb_all[:, ln]
                q, k, cum = _hgrn_chunk_inputs(q_ref, f_ref, cum_ref, lb, rows, ln)
                v = i_ref[rows, ln]
                last = cum[CHUNK - 1:CHUNK, :]
                qe = q * jnp.exp(cum)
                st = s_refs[g][...]
                st_ref[g, c] = st.astype(BF16)
                o_ref[rows, ln] = _dot_nt(qe, st) + _dot_nn(_tril(_dot_nt(qe, k * jnp.exp(-cum))), v)
                s_refs[g][...] = st * jnp.exp(last) + _dot_tn(v, k * jnp.exp(last - cum))
            return carry

        _unrolled_loop(ncb, step, 0)
        for g in range(gh):
            ln = slice(g * HEAD, (g + 1) * HEAD)
            o = o_ref[:, ln]
            n = o * lax.rsqrt(jnp.mean(o * o, axis=-1, keepdims=True) + RMS_EPS)
            gr = g_ref[:, ln]
            y_ref[:, ln] = (n * nw_ref[...] * gr * _sigmoid(gr)).astype(BF16)

    col = lambda off: pl.BlockSpec((tb, gw), lambda h, j: (j, off * ngrp + h))
    own = pl.BlockSpec((tb, gw), lambda h, j: (j, h))
    return _hosted_call(
        body, comm, name=name, grid=(ngrp, t // tb),
        in_specs=[col(3), col(4), col(5), col(6), pl.BlockSpec((2, gw), lambda h, j: (0, h)),
                  pl.BlockSpec((1, HEAD), lambda h, j: (0, 0))],
        out_specs=[own, own, pl.BlockSpec((gh, ncb, HEAD, HEAD), lambda h, j: (h, j, 0, 0))],
        out_shape=[SDS((t, w), BF16), SDS((t, w), F32), SDS((nh, t // CHUNK, HEAD, HEAD), BF16)],
        scratch_shapes=[pltpu.VMEM((tb, gw), F32)] + [pltpu.VMEM((HEAD, HEAD), F32)] * gh,
        operands=(z, z, z, z, hg, nw), parallel=1)


def _hgrn_bwd(dy, z, o, states, hg, nw, w, name, comm=()):
    t = z.shape[0]
    nh = w // HEAD
    gh = _tile(nh, HGRN_GROUP, 1)
    gw = gh * HEAD
    ngrp = nh // gh
    tb = _tile(t, HGRN_ROWS, CHUNK)
    ncb = tb // CHUNK
    nt = t // tb

    def body(dy_ref, q_ref, f_ref, i_ref, g_ref, o_ref, st_ref, hg_ref, nw_ref,
             dq_ref, df_ref, di_ref, dg_ref, dhg_ref, dnw_ref, cum_ref, do_ref, *ds_refs):
        lb_all, s1_all = _lower_bound(hg_ref[...])
        row = lax.broadcasted_iota(jnp.int32, (tb, gw), 0)
        crow = lax.broadcasted_iota(jnp.int32, (CHUNK, HEAD), 0)
        cum_ref[...] = _chunk_cumsum(jnp.log(lb_all + (1.0 - lb_all) * _sigmoid(f_ref[...])), row)

        @pl.when(pl.program_id(1) == 0)
        def _():
            for ds_ref in ds_refs:
                ds_ref[...] = jnp.zeros_like(ds_ref)
            dhg_ref[...] = jnp.zeros_like(dhg_ref)
            dnw_ref[...] = jnp.zeros_like(dnw_ref)

        for g in range(gh):
            ln = slice(g * HEAD, (g + 1) * HEAD)
            o_ = o_ref[:, ln]
            rstd = lax.rsqrt(jnp.mean(o_ * o_, axis=-1, keepdims=True) + RMS_EPS)
            n = o_ * rstd
            gr = g_ref[:, ln]
            sg = _sigmoid(gr)
            dy_ = dy_ref[:, ln]
            dg_ref[:, ln] = (dy_ * n * nw_ref[...] * (sg * (1.0 + gr * (1.0 - sg)))).astype(BF16)
            dsil = dy_ * gr * sg
            dnw_ref[:, ln] += jnp.sum(dsil * n, axis=0, keepdims=True)
            dn = dsil * nw_ref[...]
            do_ref[:, ln] = rstd * (dn - n * jnp.mean(dn * n, axis=-1, keepdims=True))

        def step(cc, dlbs):
            c = ncb - 1 - cc
            rows = pl.ds(pl.multiple_of(c * CHUNK, CHUNK), CHUNK)
            new = []
            for g in range(gh):
                ln = slice(g * HEAD, (g + 1) * HEAD)
                lb = lb_all[:, ln]
                qr = q_ref[rows, ln]
                sq = _sigmoid(qr)
                q = qr * sq
                sf = _sigmoid(f_ref[rows, ln])
                f = lb + (1.0 - lb) * sf
                k = 1.0 - f
                cum = cum_ref[rows, ln]
                v = i_ref[rows, ln]
                do = do_ref[rows, ln]
                last = cum[CHUNK - 1:CHUNK, :]
                eg = jnp.exp(cum)
                eng = jnp.exp(-cum)
                elc = jnp.exp(last - cum)
                qe, ke, kl = q * eg, k * eng, k * elc
                ds = ds_refs[g][...]
                a = _tril(_dot_nt(qe, ke))
                da = _tril(_dot_nt(do, v))
                di_ref[rows, ln] = (_dot_tn(a, do) + _dot_nt(kl, ds)).astype(BF16)
                st = st_ref[g, c]
                dkl = _dot_nn(v, ds)
                dq = (_dot_nn(do, st) + _dot_nn(da, ke)) * eg
                dk = _dot_tn(da, qe) * eng + dkl * elc
                el = jnp.exp(last)
                ds_refs[g][...] = ds * el + _dot_tn(do, qe)
                dlast = jnp.sum(kl * dkl, axis=0, keepdims=True) + el * jnp.sum(ds * st.astype(F32), axis=0, keepdims=True)
                x = q * dq - k * dk + jnp.where(crow == CHUNK - 1, dlast, 0.0)
                s = 1
                while s < CHUNK:
                    x = x + _shift_up(x, s, crow, CHUNK)
                    s *= 2
                df = x / f - dk
                dq_ref[rows, ln] = (dq * (sq * (1.0 + qr * (1.0 - sq)))).astype(BF16)
                df_ref[rows, ln] = (df * (1.0 - lb) * sf * (1.0 - sf)).astype(BF16)
                new.append(dlbs[g] + jnp.sum(df * (1.0 - sf), axis=0, keepdims=True))
            return tuple(new)

        dlbs = _unrolled_loop(ncb, step, tuple(jnp.zeros((1, HEAD), F32) for _ in range(gh)))
        for g in range(gh):
            ln = slice(g * HEAD, (g + 1) * HEAD)
            dlb = dlbs[g] * lb_all[:, ln] * s1_all[:, ln]
            dhg_ref[0:1, ln] += dlb
            dhg_ref[1:2, ln] -= dlb

    col = lambda off: pl.BlockSpec((tb, gw), lambda h, j: (nt - 1 - j, off * ngrp + h))
    own = pl.BlockSpec((tb, gw), lambda h, j: (nt - 1 - j, h))
    hsp = pl.BlockSpec((2, gw), lambda h, j: (0, h))
    return _hosted_call(
        body, comm, name=name, grid=(ngrp, nt),
        in_specs=[own, col(3), col(4), col(5), col(6), own,
                  pl.BlockSpec((gh, ncb, HEAD, HEAD), lambda h, j: (h, nt - 1 - j, 0, 0)),
                  hsp, pl.BlockSpec((1, HEAD), lambda h, j: (0, 0))],
        out_specs=[own, own, own, own, hsp, pl.BlockSpec((1, gw), lambda h, j: (0, h))],
        out_shape=[SDS((t, w), BF16)] * 4 + [SDS((2, w), F32), SDS((1, w), F32)],
        scratch_shapes=[pltpu.VMEM((tb, gw), F32)] * 2 + [pltpu.VMEM((HEAD, HEAD), F32)] * gh,
        operands=(dy, z, z, z, z, o, states, hg, nw), parallel=1)


def _cast_pad(wt, n_pad, meta, sp, name, comm=()):
    _, r, n = wt.shape
    g, p, per = meta
    tr = _tile(r, max(16, (3 << 19) // n_pad // 16 * 16), 16)

    def body(w_ref, o_ref):
        if n_pad != n:
            o_ref[...] = jnp.zeros(o_ref.shape, o_ref.dtype)
        o_ref[:, 0:n] = w_ref[...].astype(BF16)

    return _hosted_call(
        body, comm, name=name, grid=(r // tr,), in_specs=[pl.BlockSpec((None, tr, n), lambda i, sp: (0, i, 0))],
        out_specs=[pl.BlockSpec((None, tr, n_pad), lambda i, sp: (sp[1] // per, ((sp[1] % per) * r) // tr + i, 0))],
        out_shape=[SDS((g, p, n_pad), BF16)], scratch_shapes=[], operands=(wt,), parallel=1, prefetch=sp)


def _cast_pad_t(wt_t, n_pad, meta, sp, name, comm=()):
    _, n, r = wt_t.shape
    g, p, per = meta
    tc = _tile(r, 256, LANES)

    def body(w_ref, o_ref):
        for lo in range(0, n_pad, LANES):
            rows = min(LANES, n - lo)
            piece = w_ref[lo:lo + rows, :]
            if rows < LANES:
                piece = jnp.concatenate([piece, jnp.zeros((LANES - rows, tc), F32)], axis=0)
            o_ref[:, lo:lo + LANES] = piece.T.astype(BF16)

    return _hosted_call(
        body, comm, name=name, grid=(r // tc,), in_specs=[pl.BlockSpec((None, n, tc), lambda i, sp: (0, 0, i))],
        out_specs=[pl.BlockSpec((None, tc, n_pad), lambda i, sp: (sp[1] // per, ((sp[1] % per) * r) // tc + i, 0))],
        out_shape=[SDS((g, p, n_pad), BF16)], scratch_shapes=[], operands=(wt_t,), parallel=1, prefetch=sp)


def _adam_math(w, g, m, v):
    m2 = ADAM_B1 * m + (1.0 - ADAM_B1) * g
    v2 = ADAM_B2 * v + (1.0 - ADAM_B2) * (g * g)
    c1 = 1.0 / (1.0 - ADAM_B1 ** ADAM_STEP)
    c2 = 1.0 / (1.0 - ADAM_B2 ** ADAM_STEP)
    return -ADAM_LR * ((m2 * c1) / (jnp.sqrt(v2 * c2) + ADAM_EPS) + ADAM_WD * w), m2, v2


def _adamw_t(wt_t, g, m_t, v_t, name):
    _, n, r = wt_t.shape
    ng = g.shape[1]
    tc = LANES

    def body(w_ref, g_ref, m_ref, v_ref, go_ref, d_ref, mo_ref, vo_ref, gt_ref):
        for lo in range(0, ng, LANES):
            gt_ref[lo:lo + LANES, :] = g_ref[:, lo:lo + LANES].T
        g_ = gt_ref[0:n, :]
        delta, m2, v2 = _adam_math(w_ref[...], g_, m_ref[...], v_ref[...])
        go_ref[...] = g_
        d_ref[...] = delta
        mo_ref[...] = m2
        vo_ref[...] = v2

    blk = pl.BlockSpec((None, n, tc), lambda i: (0, 0, i))
    return pl.pallas_call(
        body, name=name, grid=(r // tc,), in_specs=[blk, pl.BlockSpec((tc, ng), lambda i: (i, 0)), blk, blk],
        out_specs=[blk] * 4, out_shape=[SDS(wt_t.shape, F32)] * 4, scratch_shapes=[pltpu.VMEM((ng, tc), F32)],
        compiler_params=_cparams("parallel"),
    )(wt_t, g, m_t, v_t)


def _adamw(wt, g, m, v, name):
    lead = (None,) * (wt.ndim - 2)
    zero = (0,) * (wt.ndim - 2)
    r, n = wt.shape[-2:]
    ng = g.shape[1]
    nct = 2 if ng == n and n % (2 * LANES) == 0 else 1
    tc, tg = n // nct, ng // nct
    tr = _tile(r, max(8, (3 << 17) // tg // 8 * 8), 8)

    def body(w_ref, g_ref, m_ref, v_ref, go_ref, d_ref, mo_ref, vo_ref):
        g_ = g_ref[:, 0:tc]
        delta, m2, v2 = _adam_math(w_ref[...], g_, m_ref[...], v_ref[...])
        go_ref[...] = g_
        d_ref[...] = delta
        mo_ref[...] = m2
        vo_ref[...] = v2

    blk = pl.BlockSpec(lead + (tr, tc), lambda i, j: zero + (i, j))
    return pl.pallas_call(
        body, name=name, grid=(r // tr, nct), in_specs=[blk, pl.BlockSpec((tr, tg), lambda i, j: (i, j)), blk, blk],
        out_specs=[blk] * 4, out_shape=[SDS(wt.shape, F32)] * 4, compiler_params=_cparams("parallel", "parallel"),
    )(wt, g, m, v)


def _place():
    x, y, c = lax.axis_index("x"), lax.axis_index("y"), lax.axis_index("c")
    return x, y, c, 2 * x + y


def _chip_dev(k, c):
    return (k // 2, k % 2, c)


def _half(ref, j, h, rows, per):
    return ref.at[j // per, pl.ds((j % per) * rows + h * (rows // 2), rows // 2)]


def _gather_stage(bufs, metas, rows_of, ici_parts, fwd_parts, zero_pad):
    nw = len(bufs)
    ici_on = [i for i in range(nw) if ici_parts[i] is not None]
    fwd_on = [i for i in range(nw) if fwd_parts[i] is not None]
    pad_jobs = [(i, gi) for i in ici_on if ici_parts[i][0] == 0 and metas[i][1] > metas[i][2] * rows_of[i]
                for gi in range(metas[i][0])]

    def part_of(ref, i, j, h, part):
        per = metas[i][2]
        p, np_ = part
        pr = rows_of[i] // 2 // np_
        return ref.at[j // per, pl.ds((j % per) * rows_of[i] + h * (rows_of[i] // 2) + p * pr, pr)]

    def descriptors(ins, outs, sems):
        src, zp, dst = ins[:nw], ins[nw], outs
        pads, send, recv, fsend, frecv = sems
        x, y, c, me = _place()

        def pad(n):
            i, gi = pad_jobs[n]
            extra = metas[i][1] - metas[i][2] * rows_of[i]
            return pltpu.make_async_copy(zp.at[pl.ds(0, extra)], dst[i].at[gi, pl.ds(metas[i][2] * rows_of[i], extra)], pads.at[n])

        def ici(i, r, frm):
            return pltpu.make_async_remote_copy(
                src_ref=part_of(src[i], i, me, c, ici_parts[i]), dst_ref=part_of(dst[i], i, frm, c, ici_parts[i]),
                send_sem=send.at[i, r - 1], recv_sem=recv.at[i, r - 1], device_id=_chip_dev((me + r) % N_CHIPS, c),
                device_id_type=MESH)

        def d2d(i, r, frm, h):
            blk = part_of(dst[i], i, frm, h, fwd_parts[i])
            return pltpu.make_async_remote_copy(src_ref=blk, dst_ref=blk, send_sem=fsend.at[i, r - 1],
                                                recv_sem=frecv.at[i, r - 1], device_id=(x, y, 1 - c), device_id_type=MESH)

        return pad, ici, d2d, c, me

    def start(ins, outs, sems):
        pad, ici, d2d, c, me = descriptors(ins, outs, sems)
        for n in range(len(pad_jobs)):
            pad(n).start()
        for i in fwd_on:
            for r in range(1, N_CHIPS):
                d2d(i, r, (me - r) % N_CHIPS, c).start()
        for i in ici_on:
            for r in range(1, N_CHIPS):
                ici(i, r, me).start()

    def finish(ins, outs, sems):
        pad, ici, d2d, c, me = descriptors(ins, outs, sems)
        for i in fwd_on:
            for r in range(1, N_CHIPS):
                d2d(i, r, (me - r) % N_CHIPS, 1 - c).wait_recv()
                d2d(i, r, (me - r) % N_CHIPS, c).wait_send()
        for i in ici_on:
            for r in range(1, N_CHIPS):
                ici(i, r, (me - r) % N_CHIPS).wait_recv()
                ici(i, r, me).wait_send()
        for n in range(len(pad_jobs)):
            pad(n).wait()

    return _Stage(ins=list(bufs) + [zero_pad], out_shapes=[SDS(b.shape, b.dtype) for b in bufs],
                  aliases={i: i for i in range(nw)},
                  sems=[pltpu.SemaphoreType.DMA((max(len(pad_jobs), 1),))] + [pltpu.SemaphoreType.DMA((nw, N_CHIPS - 1))] * 4,
                  start=start, finish=finish)


def _gather_small(packed, name):
    r, n = packed.shape

    def body(src, dst, send, recv):
        x, y, c, me = _place()
        dst[me] = src[...]
        cps = []
        for d in range(1, N_CHIPS):
            cp = pltpu.make_async_remote_copy(src_ref=src, dst_ref=dst.at[me], send_sem=send.at[d - 1], recv_sem=recv.at[d - 1],
                                              device_id=_chip_dev((me + d) % N_CHIPS, c), device_id_type=MESH)
            cp.start()
            cps.append(cp)
        for d in range(1, N_CHIPS):
            pltpu.make_async_remote_copy(src_ref=src, dst_ref=dst.at[(me - d) % N_CHIPS], send_sem=send.at[d - 1],
                                         recv_sem=recv.at[d - 1], device_id=_chip_dev((me + d) % N_CHIPS, c),
                                         device_id_type=MESH).wait_recv()
        for cp in cps:
            cp.wait_send()

    return pl.pallas_call(
        body, name=name, in_specs=[VMEM_SPEC], out_specs=VMEM_SPEC, out_shape=SDS((N_CHIPS, r, n), F32),
        scratch_shapes=[pltpu.SemaphoreType.DMA((N_CHIPS - 1,))] * 2,
    )(packed)


def _all_reduce_small(packed, name):
    r, n = packed.shape

    def body(src, out, slots, send, recv):
        x, y, c, me = _place()
        idx = 2 * me + c
        slots[idx] = src[...]
        cps = []

        def peer(d):
            p = (idx + d) % N_DEV
            return (p // 4, (p // 2) % 2, p % 2)

        for d in range(1, N_DEV):
            cp = pltpu.make_async_remote_copy(src_ref=src, dst_ref=slots.at[idx], send_sem=send.at[d - 1], recv_sem=recv.at[d - 1],
                                              device_id=peer(d), device_id_type=MESH)
            cp.start()
            cps.append(cp)
        for d in range(1, N_DEV):
            pltpu.make_async_remote_copy(src_ref=src, dst_ref=slots.at[(idx - d) % N_DEV], send_sem=send.at[d - 1],
                                         recv_sem=recv.at[d - 1], device_id=peer(d), device_id_type=MESH).wait_recv()
        for cp in cps:
            cp.wait_send()
        acc = slots[0]
        for k in range(1, N_DEV):
            acc = acc + slots[k]
        out[...] = acc

    return pl.pallas_call(
        body, name=name, in_specs=[VMEM_SPEC], out_specs=VMEM_SPEC, out_shape=SDS((r, n), F32),
        scratch_shapes=[pltpu.VMEM((N_DEV, r, n), F32)] + [pltpu.SemaphoreType.DMA((N_DEV - 1,))] * 2,
    )(packed)


def _simple_stage(ins, out_shapes, aliases, n_copies, copies):
    def start(ins_, outs, sems):
        for cp in copies(ins_, outs, *sems):
            cp.start()

    def finish(ins_, outs, sems):
        for cp in copies(ins_, outs, *sems):
            cp.wait()

    return _Stage(ins=list(ins), out_shapes=list(out_shapes), aliases=aliases,
                  sems=[pltpu.SemaphoreType.DMA((n_copies,))] * 2, start=start, finish=finish)


def _rs_pair_exchange(grads, metas, rows_of):
    nw = len(grads)

    def copies(src, dst, send, recv):
        x, y, c, me = _place()
        return [pltpu.make_async_remote_copy(
            src_ref=_half(src[i], j, 1 - c, rows_of[i], metas[i][2]), dst_ref=dst[i].at[j], send_sem=send.at[i * N_CHIPS + j],
            recv_sem=recv.at[i * N_CHIPS + j], device_id=(x, y, 1 - c), device_id_type=MESH)
            for i in range(nw) for j in range(N_CHIPS)]

    out_shapes = [SDS((N_CHIPS, rows_of[i] // 2, g.shape[2]), g.dtype) for i, g in enumerate(grads)]
    return _simple_stage(grads, out_shapes, {}, nw * N_CHIPS, copies)


def _rs_pair_add(g, got, meta, rows, sp, name):
    per = meta[2]
    n = g.shape[2]
    hr = rows // 2
    tr = _tile(hr, max(16, (3 << 19) // n // 16 * 16), 16)

    def body(sp_ref, g_ref, got_ref, snd_ref, own_ref):
        j = pl.program_id(1)
        s = g_ref[...].astype(F32) + got_ref[...].astype(F32)
        snd_ref[...] = s.astype(BF16)

        @pl.when(j == sp_ref[1])
        def _():
            own_ref[...] = s

    grid_spec = pltpu.PrefetchScalarGridSpec(
        num_scalar_prefetch=1, grid=(hr // tr, N_CHIPS),
        in_specs=[pl.BlockSpec((None, tr, n), lambda i, j, sp: (j // per, ((j % per) * rows + sp[0] * hr) // tr + i, 0)),
                  pl.BlockSpec((None, tr, n), lambda i, j, sp: (j, i, 0))],
        out_specs=[pl.BlockSpec((None, tr, n), lambda i, j, sp: (j, i, 0)), pl.BlockSpec((tr, n), lambda i, j, sp: (i, 0))])
    return pl.pallas_call(
        body, name=name, grid_spec=grid_spec, out_shape=[SDS((N_CHIPS, hr, n), BF16), SDS((hr, n), F32)],
        compiler_params=_cparams("parallel", "arbitrary"),
    )(sp, g, got)


def _rs_chip_exchange(sends, part=(0, 1), prev=None):
    nw = len(sends)
    p, np_ = part

    def copies(src, dst, send, recv):
        x, y, c, me = _place()
        cps = []
        for i in range(nw):
            pr = sends[i].shape[1] // np_
            for r in range(1, N_CHIPS):
                cps.append(pltpu.make_async_remote_copy(
                    src_ref=src[i].at[(me + r) % N_CHIPS, pl.ds(p * pr, pr)], dst_ref=dst[i].at[r - 1, pl.ds(p * pr, pr)],
                    send_sem=send.at[i * (N_CHIPS - 1) + r - 1], recv_sem=recv.at[i * (N_CHIPS - 1) + r - 1],
                    device_id=_chip_dev((me + r) % N_CHIPS, c), device_id_type=MESH))
        return cps

    out_shapes = [SDS((N_CHIPS - 1,) + s.shape[1:], BF16) for s in sends]
    if prev is None:
        return _simple_stage(sends, out_shapes, {}, nw * (N_CHIPS - 1), copies)
    return _simple_stage(list(sends) + list(prev), out_shapes, {nw + i: i for i in range(nw)}, nw * (N_CHIPS - 1), copies)


def _rs_chip_add(own, got, sp, name):
    hr, n = own.shape
    tr = _tile(hr, max(16, (3 << 19) // n // 16 * 16), 16)

    def body(sp_ref, own_ref, got_ref, o_ref):
        acc = own_ref[...]
        for r in range(N_CHIPS - 1):
            acc = acc + got_ref[r].astype(F32)
        o_ref[...] = acc

    grid_spec = pltpu.PrefetchScalarGridSpec(
        num_scalar_prefetch=1, grid=(hr // tr,),
        in_specs=[pl.BlockSpec((tr, n), lambda i, sp: (i, 0)), pl.BlockSpec((N_CHIPS - 1, tr, n), lambda i, sp: (0, i, 0))],
        out_specs=pl.BlockSpec((tr, n), lambda i, sp: (sp[0] * (hr // tr) + i, 0)))
    return pl.pallas_call(body, name=name, grid_spec=grid_spec, out_shape=SDS((2 * hr, n), F32),
                          compiler_params=_cparams("parallel"))(sp, own, got)


def _rs_pair_share(blocks):
    nw = len(blocks)

    def copies(src, dst, send, recv):
        x, y, c, me = _place()
        cps = []
        for i in range(nw):
            hr = src[i].shape[0] // 2
            cps.append(pltpu.make_async_remote_copy(
                src_ref=src[i].at[pl.ds(c * hr, hr)], dst_ref=dst[i].at[pl.ds(c * hr, hr)], send_sem=send.at[i],
                recv_sem=recv.at[i], device_id=(x, y, 1 - c), device_id_type=MESH))
        return cps

    return _simple_stage(blocks, [SDS(b.shape, b.dtype) for b in blocks], {i: i for i in range(nw)}, nw, copies)


def kernel(x, p, ln_g, ln_b, ffn1_w_in, ffn1_w_out, mix_w_in, conv_w, hg_lower_bound, hg_norm_w, branch_w_conv, branch_w_hgrn, mix_w_out, ffn2_w_in, ffn2_w_out, ple_w_gate, ple_w_proj, loss_target, m_ln_g, m_ln_b, m_ffn1_w_in, m_ffn1_w_out, m_mix_w_in, m_conv_w, m_hg_lower_bound, m_hg_norm_w, m_branch_w_conv, m_branch_w_hgrn, m_mix_w_out, m_ffn2_w_in, m_ffn2_w_out, m_ple_w_gate, m_ple_w_proj, v_ln_g, v_ln_b, v_ffn1_w_in, v_ffn1_w_out, v_mix_w_in, v_conv_w, v_hg_lower_bound, v_hg_norm_w, v_branch_w_conv, v_branch_w_hgrn, v_mix_w_out, v_ffn2_w_in, v_ffn2_w_out, v_ple_w_gate, v_ple_w_proj):
    assert ln_g.shape[0] == DEPTH and x.shape[0] == 1 and p.shape[:2] == (1, 1)
    t, d = x.shape[1], x.shape[2]
    w = d // 2
    x0 = x.reshape(t, d)
    x0b = _to_bf16(x0, "x_bf16")
    pe = p.reshape(t, p.shape[-1])
    target = loss_target.reshape(t, d)
    cx, cy, cc = lax.axis_index("x"), lax.axis_index("y"), lax.axis_index("c")
    chip = 2 * cx + cy
    sp = jnp.stack([cc, chip]).astype(jnp.int32)

    big = dict(ffn1_w_in=ffn1_w_in, ffn1_w_out=ffn1_w_out, mix_w_in=mix_w_in, branch_w_conv=branch_w_conv,
               branch_w_hgrn=branch_w_hgrn, mix_w_out=mix_w_out, ffn2_w_in=ffn2_w_in, ffn2_w_out=ffn2_w_out,
               ple_w_gate=ple_w_gate, ple_w_proj=ple_w_proj)
    moments = dict(ffn1_w_in=(m_ffn1_w_in, v_ffn1_w_in), ffn1_w_out=(m_ffn1_w_out, v_ffn1_w_out), mix_w_in=(m_mix_w_in, v_mix_w_in),
                   branch_w_conv=(m_branch_w_conv, v_branch_w_conv), branch_w_hgrn=(m_branch_w_hgrn, v_branch_w_hgrn),
                   mix_w_out=(m_mix_w_out, v_mix_w_out), ffn2_w_in=(m_ffn2_w_in, v_ffn2_w_in), ffn2_w_out=(m_ffn2_w_out, v_ffn2_w_out),
                   ple_w_gate=(m_ple_w_gate, v_ple_w_gate), ple_w_proj=(m_ple_w_proj, v_ple_w_proj))
    names = list(big)

    n_loc = ffn1_w_in.shape[-1]
    n_pad = -(-n_loc // LANES) * LANES
    assert mix_w_in.shape[-1] % LANES == 0 and ffn1_w_out.shape[1] * 2 == n_loc
    pad_cols = dict(ffn1_w_in=n_pad, ffn2_w_in=n_pad)
    meta = {k: (N_CHIPS, big[k].shape[1], 1) for k in names}
    meta["ffn1_w_out"] = meta["ffn2_w_out"] = (2, n_pad, 2)
    rows = {k: big[k].shape[1] for k in names}
    swap = lambda a: jnp.transpose(a, (0, 2, 1))
    wbuf = {}
    zero_pad = jnp.zeros((max(n_pad - n_loc, 16), d), BF16)

    def cast(k, comm=()):
        if k in pad_cols:
            return _cast_pad_t(swap(big[k]), pad_cols[k], meta[k], sp, "cast_" + k, comm=comm)
        return _cast_pad(big[k], big[k].shape[2], meta[k], sp, "cast_" + k, comm=comm)

    def gather(ici=(), fwd=()):
        ks = list(dict.fromkeys([k for k, _, _ in ici] + [k for k, _, _ in fwd]))
        ip = {k: (p_, n_) for k, p_, n_ in ici}
        fp = {k: (p_, n_) for k, p_, n_ in fwd}
        return _gather_stage([wbuf[k] for k in ks], [meta[k] for k in ks], [rows[k] for k in ks], [ip.get(k) for k in ks],
                             [fp.get(k) for k in ks], zero_pad), ks

    def gathered(ks, outs):
        wbuf.update(zip(ks, outs))

    def w3(k):
        return wbuf[k]

    def w2(k):
        return wbuf[k].reshape(-1, wbuf[k].shape[2])

    dq, wq = d // N_CHIPS, w // N_CHIPS
    small = jnp.concatenate([ln_g[0], ln_b[0], jnp.pad(conv_w[0], ((0, 5), (0, dq - wq)))], axis=0)
    small = _gather_small(small, "gather_small")
    lng = small[:, 0:4, :].transpose(1, 0, 2).reshape(4, 1, d)
    lnb = small[:, 4:8, :].transpose(1, 0, 2).reshape(4, 1, d)
    cw = small[:, 8:11, :wq].transpose(1, 0, 2).reshape(3, w)
    hg = hg_lower_bound
    nw_ = hg_norm_w

    wbuf["ffn1_w_in"] = cast("ffn1_w_in")
    others = [k for k in names if k != "ffn1_w_in"]
    assert len(others) > FIRST_GATHER_PARTS
    for step, k in enumerate(others):
        ici = [("ffn1_w_in", step, FIRST_GATHER_PARTS)] if step < FIRST_GATHER_PARTS else []
        fwd = [("ffn1_w_in", step - 1, FIRST_GATHER_PARTS)] if 1 <= step <= FIRST_GATHER_PARTS else []
        if ici or fwd:
            st, ks = gather(ici, fwd)
            wbuf[k], got = cast(k, comm=[st])
            gathered(ks, got)
        else:
            wbuf[k] = cast(k)
    one = lambda *ks_: [(k, 0, 1) for k in ks_]
    st, ks = gather(ici=one("ffn1_w_out") + [("mix_w_in", 0, 2)])
    z1, got = _mm(x0b, w3("ffn1_w_in"), name="ffn1_in", b_blocked=True, out_dtype=BF16, tm=1024, comm=[st])
    gathered(ks, got)
    st, ks = gather(fwd=one("ffn1_w_out") + [("mix_w_in", 0, 2)])
    h1, got = _swiglu_fwd(z1, "ffn1_act", comm=[st])
    gathered(ks, got)
    st, ks = gather(ici=[("mix_w_in", 1, 2)])
    y1, got = _mm(h1, w2("ffn1_w_out"), name="ffn1_out", tm=1024, tn=1024, tk=2816, comm=[st])
    gathered(ks, got)
    st, ks = gather(fwd=[("mix_w_in", 1, 2)])
    (r1, x1, x1b), got = _ln_fwd(x0, y1, lng[0], lnb[0], 0.5, "ln0", comm=[st])
    gathered(ks, got)
    mixo_w = one("branch_w_conv", "branch_w_hgrn", "mix_w_out")
    st, ks = gather(ici=mixo_w + [("ffn2_w_in", 0, 2)])
    z, got = _mm(x1b, w3("mix_w_in"), name="mix_in", b_blocked=True, tm=1024, comm=[st])
    gathered(ks, got)
    ya = _conv_fwd(z, cw, w, "conv_fwd")
    st, ks = gather(ici=[("ffn2_w_in", 1, 2)], fwd=mixo_w + [("ffn2_w_in", 0, 2)])
    (yb, o_h, states), got = _hgrn_fwd(z, hg, nw_, w, "hgrn_fwd", comm=[st])
    gathered(ks, got)
    ma = _mm(ya, w3("branch_w_conv"), name="branch_conv", b_blocked=True, tn=512)
    mb = _mm(yb, w3("branch_w_hgrn"), name="branch_hgrn", b_blocked=True, tn=512)
    merged = _merge_fwd(z, ma, mb, w, "merge_fwd")
    st, ks = gather(fwd=[("ffn2_w_in", 1, 2)])
    y2, got = _mm(merged, w2("mix_w_out"), name="mix_out", tn=1024, comm=[st])
    gathered(ks, got)
    r2, x2, x2b = _ln_fwd(x1, y2, lng[1], lnb[1], 1.0, "ln1")
    late = one("ffn2_w_out", "ple_w_gate", "ple_w_proj")
    st, ks = gather(ici=late)
    z3, got = _mm(x2b, w3("ffn2_w_in"), name="ffn2_in", b_blocked=True, out_dtype=BF16, tm=1024, comm=[st])
    gathered(ks, got)
    st, ks = gather(fwd=late)
    h3, got = _swiglu_fwd(z3, "ffn2_act", comm=[st])
    gathered(ks, got)
    y3 = _mm(h3, w2("ffn2_w_out"), name="ffn2_out", tm=1024, tn=1024, tk=2816)
    r3, x3, x3b = _ln_fwd(x2, y3, lng[2], lnb[2], 0.5, "ln2")
    gp = _mm(x3b, w2("ple_w_gate"), name="ple_gate", tn=1024)
    pp = _mm(pe, w3("ple_w_proj"), name="ple_proj", b_blocked=True, tn=512)
    dr4, dgp, dpp, dg3, db3, sq = _tail(x3, gp, pp, lng[3], lnb[3], target, "tail")

    grads, sends, owns, blocks, outs = {}, {}, {}, {}, {}

    def pair_exchange(*ks):
        return _rs_pair_exchange([grads[k] for k in ks], [meta[k] for k in ks], [rows[k] for k in ks])

    def pair_add(ks, got):
        for k, g_ in zip(ks, got):
            sends[k], owns[k] = _rs_pair_add(grads[k], g_, meta[k], rows[k], sp, "rs_pair_add_" + k)

    def chip_exchange(*ks):
        return _rs_chip_exchange([sends[k] for k in ks])

    def chip_add(ks, got):
        for k, g_ in zip(ks, got):
            blocks[k] = _rs_chip_add(owns[k], g_, sp, "rs_chip_add_" + k)

    def pair_share(*ks):
        return _rs_pair_share([blocks[k] for k in ks])

    def update(ks, full):
        for k, g_ in zip(ks, full):
            m_, v_ = moments[k]
            if k in pad_cols:
                outs[k] = [swap(a) for a in _adamw_t(swap(big[k]), g_, swap(m_), swap(v_), "adamw_" + k)]
            else:
                outs[k] = _adamw(big[k], g_, m_, v_, "adamw_" + k)

    ple = ("ple_w_gate", "ple_w_proj")
    mixo = ("mix_w_out", "branch_w_conv", "branch_w_hgrn")
    dx3m = _mm(dgp, w2("ple_w_gate"), name="d_ple_gate_x", tb=True, tn=1024, tk=2048)
    grads["ple_w_gate"] = _mm(x3b, dgp, name="d_ple_gate_w", ta=True, out_dtype=BF16, tm=1024, tk=2048, tn=1024).reshape(N_CHIPS, -1, d)
    grads["ple_w_proj"] = _mm(pe, dpp, name="d_ple_proj_w", ta=True, out_dtype=BF16, out_blocked=N_CHIPS, tk=2048, tn=512)
    dr3, dy3b, dg2, db2 = _ln_bwd(dr4, dx3m, r3, lng[2], 0.5, "ln2_bwd")
    dh3, got = _mm(dy3b, w2("ffn2_w_out"), name="d_ffn2_out_x", tb=True, out_dtype=BF16, tn=1408, tk=2048,
                   comm=[pair_exchange(*ple)])
    pair_add(ple, got)
    g_, got = _mm(h3, dy3b, name="d_ffn2_out_w", ta=True, out_dtype=BF16, tm=1408, tk=2048, tn=1024, comm=[chip_exchange(*ple)])
    grads["ffn2_w_out"] = g_.reshape(2, n_pad, d)
    chip_add(ple, got)
    dz3 = _swiglu_bwd(dh3, z3, "ffn2_act_bwd")
    dx2m, got, full = _mm(dz3, w3("ffn2_w_in"), name="d_ffn2_in_x", tb=True, b_blocked=True, tm=1024, tn=1024, tk=2816,
                          comm=[pair_exchange("ffn2_w_out"), pair_share(*ple)])
    pair_add(["ffn2_w_out"], got)
    update(ple, full)
    grads["ffn2_w_in"], got = _mm(x2b, dz3, name="d_ffn2_in_w", ta=True, out_dtype=BF16, out_blocked=N_CHIPS, tk=4096, comm=[chip_exchange("ffn2_w_out")])
    chip_add(["ffn2_w_out"], got)
    dr2, dy2b, dg1, db1 = _ln_bwd(dr3, dx2m, r2, lng[1], 1.0, "ln1_bwd")
    dmer, got = _mm(dy2b, w2("mix_w_out"), name="d_mix_out_x", tb=True, tn=1024, tk=2048, comm=[pair_exchange("ffn2_w_in")])
    pair_add(["ffn2_w_in"], got)
    g_, full = _mm(merged, dy2b, name="d_mix_out_w", ta=True, out_dtype=BF16, tm=1024, tk=2048, tn=1024, comm=[pair_share("ffn2_w_out")])
    grads["mix_w_out"] = g_.reshape(N_CHIPS, -1, d)
    update(["ffn2_w_out"], full)
    dma, dmb, dgc, dgh = _merge_bwd(dmer, z, ma, mb, w, "merge_bwd")
    dya = _mm(dma, w3("branch_w_conv"), name="d_branch_conv_x", tb=True, b_blocked=True, tn=1024, tk=512)
    dyb = _mm(dmb, w3("branch_w_hgrn"), name="d_branch_hgrn_x", tb=True, b_blocked=True, tn=1024, tk=512)
    grads["branch_w_conv"] = _mm(ya, dma, name="d_branch_conv_w", ta=True, out_dtype=BF16, out_blocked=N_CHIPS, tm=1024, tk=2048, tn=512)
    grads["branch_w_hgrn"] = _mm(yb, dmb, name="d_branch_hgrn_w", ta=True, out_dtype=BF16, out_blocked=N_CHIPS, tm=1024, tk=2048, tn=512)
    dbg, dcg, dhc, dcw = _conv_bwd(dya, z, cw, w, "conv_bwd")
    (dq_, df_, di_, dgr_, dhg, dnw), got2, got = _hgrn_bwd(dyb, z, o_h, states, hg, nw_, w, "hgrn_bwd",
                                                            comm=[chip_exchange("ffn2_w_in"), pair_exchange(*mixo)])
    chip_add(["ffn2_w_in"], got2)
    pair_add(mixo, got)
    dz = _concat_cols([dbg, dcg, dhc, dq_, df_, di_, dgr_, dgc, dgh], "dz_concat")
    dx1m, full, got = _mm(dz, w3("mix_w_in"), name="d_mix_in_x", tb=True, b_blocked=True, tm=1024, tn=1024, tk=2816,
                          comm=[pair_share("ffn2_w_in"), chip_exchange(*mixo)])
    update(["ffn2_w_in"], full)
    chip_add(mixo, got)
    grads["mix_w_in"], full = _mm(x1b, dz, name="d_mix_in_w", ta=True, out_dtype=BF16, out_blocked=N_CHIPS, tk=4096, comm=[pair_share(*mixo)])
    update(mixo, full)
    dr1, dy1b, dg0, db0 = _ln_bwd(dr2, dx1m, r1, lng[0], 0.5, "ln0_bwd")
    dh1, got = _mm(dy1b, w2("ffn1_w_out"), name="d_ffn1_out_x", tb=True, out_dtype=BF16, tn=1408, tk=2048,
                   comm=[pair_exchange("mix_w_in")])
    pair_add(["mix_w_in"], got)
    mix_sends = [sends["mix_w_in"]]
    g_, got_a = _mm(h1, dy1b, name="d_ffn1_out_w", ta=True, out_dtype=BF16, tm=1408, tk=2048, tn=1024, comm=[_rs_chip_exchange(mix_sends, (0, 2))])
    grads["ffn1_w_out"] = g_.reshape(2, n_pad, d)
    dz1 = _swiglu_bwd(dh1, z1, "ffn1_act_bwd")
    g_other, got2, got = _mm(x0b, dz1, name="d_ffn1_in_w_other", ta=True, out_dtype=BF16, out_blocked=N_CHIPS, tk=4096, half=(sp, True),
                             comm=[_rs_chip_exchange(mix_sends, (1, 2), got_a), pair_exchange("ffn1_w_out")])
    chip_add(["mix_w_in"], got2)
    pair_add(["ffn1_w_out"], got)
    grads["ffn1_w_in"], full, got2, got = _mm(
        x0b, dz1, name="d_ffn1_in_w_own", ta=True, out_dtype=BF16, out_blocked=N_CHIPS, tk=4096, half=(sp, False),
        comm=[pair_share("mix_w_in"), chip_exchange("ffn1_w_out"),
              _rs_pair_exchange([g_other], [meta["ffn1_w_in"]], [rows["ffn1_w_in"]])])
    update(["mix_w_in"], full)
    chip_add(["ffn1_w_out"], got2)
    pair_add(["ffn1_w_in"], got)
    dx0, got2, full = _mm(dz1, w3("ffn1_w_in"), name="d_ffn1_in_x", tb=True, b_blocked=True, tm=1024, tn=1024, tk=2816,
                          add=(dr1, ALPHA), comm=[chip_exchange("ffn1_w_in"), pair_share("ffn1_w_out")])
    chip_add(["ffn1_w_in"], got2)
    update(["ffn1_w_out"], full)
    grad_x = dx0.reshape(x.shape)
    update(["ffn1_w_in"], _run_stages([pair_share("ffn1_w_in")], "rs_tail_pair")[0])

    pack = jnp.concatenate([
        dg0, dg1, dg2, dg3, db0, db1, db2, db3,
        jnp.pad(dcw, ((0, 0), (0, d - w))), jnp.pad(dhg, ((0, 0), (0, d - w))),
        jnp.pad(jnp.sum(dnw.reshape(-1, HEAD), axis=0, keepdims=True), ((0, 0), (0, d - HEAD))), sq], axis=0)
    pack = _all_reduce_small(jnp.pad(pack, ((0, 1), (0, 0))), "reduce_small")
    loss = (0.5 / d) * jnp.sum(pack[14])
    g_ln_g = lax.dynamic_slice_in_dim(pack[0:4], chip * dq, dq, axis=1)
    g_ln_b = lax.dynamic_slice_in_dim(pack[4:8], chip * dq, dq, axis=1)
    g_conv = lax.dynamic_slice_in_dim(pack[8:11, :w], chip * wq, wq, axis=1)
    g_hg = pack[11:13, :w]
    g_nw = pack[13:14, :HEAD]

    small_w = dict(ln_g=(ln_g, g_ln_g, m_ln_g, v_ln_g), ln_b=(ln_b, g_ln_b, m_ln_b, v_ln_b),
                   conv_w=(conv_w, g_conv, m_conv_w, v_conv_w), hg_lower_bound=(hg_lower_bound, g_hg, m_hg_lower_bound, v_hg_lower_bound),
                   hg_norm_w=(hg_norm_w, g_nw, m_hg_norm_w, v_hg_norm_w))
    for k, (w_, g_, m_, v_) in small_w.items():
        outs[k] = _adamw(w_, g_.reshape(-1, w_.shape[-1]), m_, v_, "adamw_" + k)

    order = ["ln_g", "ln_b", "ffn1_w_in", "ffn1_w_out", "mix_w_in", "conv_w", "hg_lower_bound", "hg_norm_w", "branch_w_conv",
             "branch_w_hgrn", "mix_w_out", "ffn2_w_in", "ffn2_w_out", "ple_w_gate", "ple_w_proj"]
    return (loss, grad_x, *[outs[k][0] for k in order], *[outs[k][1] for k in order], *[outs[k][2] for k in order],
            *[outs[k][3] for k in order])
```

```python
import collections
import functools

import jax
import jax.numpy as jnp
from jax import lax
from jax.experimental import pallas as pl
from jax.experimental.pallas import tpu as pltpu

F32 = jnp.float32
BF16 = jnp.bfloat16
MESH = pl.DeviceIdType.MESH
ANY = pl.BlockSpec(memory_space=pl.ANY)
VMEM_SPEC = pl.BlockSpec(memory_space=pltpu.VMEM)
SDS = jax.ShapeDtypeStruct

DEPTH = 1
ALPHA = (2.0 * DEPTH) ** 0.25
LN_EPS = 1e-5
RMS_EPS = 1e-6
CHUNK = 32
HEAD = 128
ADAM_LR, ADAM_B1, ADAM_B2, ADAM_EPS, ADAM_WD, ADAM_STEP = 0.001, 0.9, 0.999, 1e-08, 0.01, 10

LANES = 128
N_CHIPS = 4
N_DEV = 8
FIRST_GATHER_PARTS = 8
VMEM_LIMIT = 52 * 1024 * 1024
MM_PIECE = 512


def _cparams(*sem):
    if sem:
        return pltpu.CompilerParams(dimension_semantics=sem, vmem_limit_bytes=VMEM_LIMIT)
    return pltpu.CompilerParams(vmem_limit_bytes=VMEM_LIMIT)


def _tile(n, target, mult):
    best = None
    for t in range(mult, min(n, target) + 1, mult):
        if n % t == 0:
            best = t
    return best if best is not None else n


def _sigmoid(x):
    return 1.0 / (1.0 + jnp.exp(-x))


_Stage = collections.namedtuple("_Stage", "ins out_shapes aliases sems start finish")


def _hosted_call(compute, stages, *, name, grid, in_specs, out_specs, out_shape, scratch_shapes, operands, parallel,
                 prefetch=None):
    n_cmp, n_out, n_scr = len(in_specs), len(out_specs), len(scratch_shapes)
    n_in = n_cmp
    n_pre = int(prefetch is not None)
    c_in = [len(s.ins) for s in stages]
    c_out = [len(s.out_shapes) for s in stages]
    c_sem = [len(s.sems) for s in stages]
    aliases = {}
    for si, s in enumerate(stages):
        for a_in, a_out in s.aliases.items():
            aliases[n_pre + n_in + sum(c_in[:si]) + a_in] = n_out + sum(c_out[:si]) + a_out

    def body(*refs):
        refs = refs[n_pre:]
        ins = refs[:n_cmp]
        cins = refs[n_in:n_in + sum(c_in)]
        outs = refs[n_in + sum(c_in):n_in + sum(c_in) + n_out]
        couts = refs[n_in + sum(c_in) + n_out:n_in + sum(c_in) + n_out + sum(c_out)]
        scr = refs[n_in + sum(c_in) + n_out + sum(c_out):][:n_scr]
        sems = refs[n_in + sum(c_in) + n_out + sum(c_out) + n_scr:]

        def stage_refs(si):
            return (cins[sum(c_in[:si]):sum(c_in[:si + 1])], couts[sum(c_out[:si]):sum(c_out[:si + 1])],
                    sems[sum(c_sem[:si]):sum(c_sem[:si + 1])])

        if stages:
            first = functools.reduce(jnp.logical_and, [pl.program_id(ax) == 0 for ax in range(len(grid))])
            last = functools.reduce(jnp.logical_and, [pl.program_id(ax) == grid[ax] - 1 for ax in range(len(grid))])

            @pl.when(first)
            def _():
                for si, s in enumerate(stages):
                    s.start(*stage_refs(si))

        compute(*ins, *outs, *scr)
        if stages:
            @pl.when(last)
            def _():
                for si, s in enumerate(stages):
                    s.finish(*stage_refs(si))

    sem = ("arbitrary",) * len(grid) if stages else ("parallel",) * parallel + ("arbitrary",) * (len(grid) - parallel)
    all_in = list(in_specs) + [ANY] * (n_in - n_cmp + sum(c_in))
    all_out = list(out_specs) + [ANY] * sum(c_out)
    all_scr = list(scratch_shapes) + [q for s in stages for q in s.sems]
    all_shape = list(out_shape) + [o for s in stages for o in s.out_shapes]
    args = list(operands) + [a for s in stages for a in s.ins]
    if prefetch is None:
        res = pl.pallas_call(body, name=name, grid=grid, in_specs=all_in, out_specs=all_out, out_shape=all_shape,
                             input_output_aliases=aliases, scratch_shapes=all_scr, compiler_params=_cparams(*sem))(*args)
    else:
        grid_spec = pltpu.PrefetchScalarGridSpec(num_scalar_prefetch=1, grid=grid, in_specs=all_in, out_specs=all_out,
                                                 scratch_shapes=all_scr)
        res = pl.pallas_call(body, name=name, grid_spec=grid_spec, out_shape=all_shape, input_output_aliases=aliases,
                             compiler_params=_cparams(*sem))(prefetch, *args)
    main = res[0] if n_out == 1 else list(res[:n_out])
    if not stages:
        return main
    rest = res[n_out:]
    return (main, *[list(rest[sum(c_out[:si]):sum(c_out[:si + 1])]) for si in range(len(stages))])


def _run_stages(stages, name):
    def body(*refs):
        n_i = sum(len(s.ins) for s in stages)
        n_o = sum(len(s.out_shapes) for s in stages)
        cins, couts, sems = refs[:n_i], refs[n_i:n_i + n_o], refs[n_i + n_o:]
        pos = [0, 0, 0]
        parts = []
        for s in stages:
            parts.append((cins[pos[0]:pos[0] + len(s.ins)], couts[pos[1]:pos[1] + len(s.out_shapes)], sems[pos[2]:pos[2] + len(s.sems)]))
            pos = [pos[0] + len(s.ins), pos[1] + len(s.out_shapes), pos[2] + len(s.sems)]
        for s, p_ in zip(stages, parts):
            s.start(*p_)
        for s, p_ in zip(stages, parts):
            s.finish(*p_)

    aliases, ni, no = {}, 0, 0
    for s in stages:
        for a_in, a_out in s.aliases.items():
            aliases[ni + a_in] = no + a_out
        ni, no = ni + len(s.ins), no + len(s.out_shapes)
    res = pl.pallas_call(
        body, name=name, in_specs=[ANY] * ni, out_specs=[ANY] * no, out_shape=[o for s in stages for o in s.out_shapes],
        input_output_aliases=aliases, scratch_shapes=[q for s in stages for q in s.sems],
    )(*[a for s in stages for a in s.ins])
    out, pos = [], 0
    for s in stages:
        out.append(list(res[pos:pos + len(s.out_shapes)]))
        pos += len(s.out_shapes)
    return out


def _mm(a, b, *, name, ta=False, tb=False, b_blocked=False, out_blocked=0, out_dtype=F32,
        tm=512, tn=1408, tk=2048, comm=(), half=None, add=None):
    if ta:
        kd, m = a.shape
    else:
        m, kd = a.shape
    if b_blocked and not tb:
        g, kb, nb = b.shape
        assert kb == kd
        n = g * nb
        tn = _tile(nb, tn, LANES)
        tk = _tile(kd, tk, LANES)
        per_n = nb // tn
        b_spec = pl.BlockSpec((None, tk, tn), lambda i, j, k, *s: (j // per_n, k, j % per_n))
    elif b_blocked and tb:
        g, n, kb = b.shape
        assert g * kb == kd
        tn = _tile(n, tn, LANES)
        tk = _tile(kb, tk, LANES)
        per_k = kb // tk
        b_spec = pl.BlockSpec((None, tn, tk), lambda i, j, k, *s: (k // per_k, j, k % per_k))
    elif tb:
        n, kb = b.shape
        assert kb == kd
        tn = _tile(n, tn, LANES)
        tk = _tile(kd, tk, LANES)
        b_spec = pl.BlockSpec((tn, tk), lambda i, j, k, *s: (j, k))
    else:
        kb, n = b.shape
        assert kb == kd
        tn = _tile(n // out_blocked if out_blocked else n, tn, LANES)
        per_o = (n // out_blocked) // tn if out_blocked else None
        tk = _tile(kd, tk, LANES)
        b_spec = pl.BlockSpec((tk, tn), lambda i, j, k, *s: (k, j))
    m_run = m // 2 if half else m
    tm = _tile(m_run, tm, LANES if ta else 8)

    def row(i, s):
        if not half:
            return i
        h = 1 - s[0][0] if half[1] else s[0][0]
        return h * (m_run // tm) + i

    if ta:
        a_spec = pl.BlockSpec((tk, tm), lambda i, j, k, *s: (k, row(i, s)))
    else:
        a_spec = pl.BlockSpec((tm, tk), lambda i, j, k, *s: (row(i, s), k))
    if out_blocked:
        assert not b_blocked and not tb
        o_spec = pl.BlockSpec((None, tm, tn), lambda i, j, k, *s: (j // per_o, row(i, s), j % per_o))
        o_shape = SDS((out_blocked, m, n // out_blocked), out_dtype)
    else:
        o_spec = pl.BlockSpec((tm, tn), lambda i, j, k, *s: (row(i, s), j))
        o_shape = SDS((m, n), out_dtype)
    nk = kd // tk
    dn = (((0 if ta else 1,), (1 if tb else 0,)), ((), ()))
    grid = (m_run // tm, n // tn, nk)

    pieces = [(lo, min(MM_PIECE, tn - lo)) for lo in range(0, tn, MM_PIECE)]

    def compute(a_ref, b_ref, *rest):
        add_ref = rest[0] if add else None
        o_ref, acc_ref = rest[-2:]
        a_tile = a_ref[...].astype(BF16)
        k = pl.program_id(2)

        def result(acc, cols):
            if add:
                acc = acc + add[1] * add_ref[:, cols]
            return acc.astype(o_ref.dtype)

        if nk > 1:
            @pl.when(k == 0)
            def _():
                acc_ref[...] = jnp.zeros_like(acc_ref)

        for lo, wd in pieces:
            cols = slice(lo, lo + wd)
            b_tile = b_ref[cols, :] if tb else b_ref[:, cols]
            part = lax.dot_general(a_tile, b_tile.astype(BF16), dn, preferred_element_type=F32)
            if nk == 1:
                o_ref[:, cols] = result(part, cols)
            else:
                acc_ref[:, cols] += part

        if nk > 1:
            @pl.when(k == nk - 1)
            def _():
                o_ref[...] = result(acc_ref[...], slice(None))

    extra = [(add[0], o_spec)] if add else []
    return _hosted_call(compute, comm, name=name, grid=grid, in_specs=[a_spec, b_spec] + [s_ for _, s_ in extra], out_specs=[o_spec],
                        out_shape=[o_shape], scratch_shapes=[pltpu.VMEM((tm, tn), F32)], operands=(a, b, *[a_ for a_, _ in extra]),
                        parallel=2, prefetch=half[0] if half else None)


def _swiglu_fwd(z, name, comm=()):
    t, n = z.shape
    n2 = n // 2
    tr = _tile(t, 128, 16)

    def body(a_ref, u_ref, o_ref):
        a = a_ref[...].astype(F32)
        o_ref[...] = (a * _sigmoid(a) * u_ref[...].astype(F32)).astype(o_ref.dtype)

    return _hosted_call(
        body, comm, name=name, grid=(t // tr,),
        in_specs=[pl.BlockSpec((tr, n2), lambda i: (i, 0)), pl.BlockSpec((tr, n2), lambda i: (i, 1))],
        out_specs=[pl.BlockSpec((tr, n2), lambda i: (i, 0))], out_shape=[SDS((t, n2), BF16)], scratch_shapes=[],
        operands=(z, z), parallel=1)


def _swiglu_bwd(dh, z, name):
    t, n = z.shape
    n2 = n // 2
    tr = _tile(t, 128, 16)

    def body(dh_ref, a_ref, u_ref, o_ref):
        a = a_ref[...].astype(F32)
        dh_ = dh_ref[...].astype(F32)
        s = _sigmoid(a)
        o_ref[:, 0:n2] = (dh_ * u_ref[...].astype(F32) * (s * (1.0 + a * (1.0 - s)))).astype(o_ref.dtype)
        o_ref[:, n2:n] = (dh_ * a * s).astype(o_ref.dtype)

    return pl.pallas_call(
        body, name=name, grid=(t // tr,),
        in_specs=[pl.BlockSpec((tr, n2), lambda i: (i, 0)), pl.BlockSpec((tr, n2), lambda i: (i, 0)),
                  pl.BlockSpec((tr, n2), lambda i: (i, 1))],
        out_specs=pl.BlockSpec((tr, n), lambda i: (i, 0)), out_shape=SDS((t, n), BF16),
        compiler_params=_cparams("parallel"),
    )(dh, z, z)


def _ln_stats(r):
    mu = jnp.mean(r, axis=-1, keepdims=True)
    xc = r - mu
    var = jnp.mean(xc * xc, axis=-1, keepdims=True)
    return xc * lax.rsqrt(var + LN_EPS)


def _ln_fwd(xp, y, g, b, gp, bp, scale, name, comm=()):
    t, d = xp.shape
    tr = _tile(t, 256, 16)

    def body(xp_ref, y_ref, g_ref, b_ref, *rest):
        r_ref, xb_ref = rest[-2:]
        x_prev = xp_ref[...]
        if gp is not None:
            x_prev = _ln_stats(x_prev) * rest[0][...] + rest[1][...]
        r = ALPHA * x_prev + scale * y_ref[...]
        r_ref[...] = r
        xb_ref[...] = (_ln_stats(r) * g_ref[...] + b_ref[...]).astype(BF16)

    row = pl.BlockSpec((tr, d), lambda i: (i, 0))
    vec = pl.BlockSpec((1, d), lambda i: (0, 0))
    prev = [] if gp is None else [gp, bp]
    return _hosted_call(
        body, comm, name=name, grid=(t // tr,), in_specs=[row, row, vec, vec] + [vec] * len(prev), out_specs=[row, row],
        out_shape=[SDS((t, d), F32), SDS((t, d), BF16)], scratch_shapes=[], operands=(xp, y, g, b, *prev), parallel=1)


def _ln_bwd(dra, dxm, r, g, scale, name):
    t, d = r.shape
    tr = _tile(t, 256, 16)

    def body(dra_ref, dxm_ref, r_ref, g_ref, dr_ref, dyb_ref, dg_ref, db_ref):
        i = pl.program_id(0)
        dx = ALPHA * dra_ref[...] + dxm_ref[...]
        rr = r_ref[...]
        mu = jnp.mean(rr, axis=-1, keepdims=True)
        xc = rr - mu
        rstd = lax.rsqrt(jnp.mean(xc * xc, axis=-1, keepdims=True) + LN_EPS)
        xh = xc * rstd
        dxh = dx * g_ref[...]
        dr = rstd * (dxh - jnp.mean(dxh, axis=-1, keepdims=True) - xh * jnp.mean(dxh * xh, axis=-1, keepdims=True))
        dr_ref[...] = dr
        dyb_ref[...] = (scale * dr).astype(BF16)
        dg = jnp.sum(dx * xh, axis=0, keepdims=True)
        db = jnp.sum(dx, axis=0, keepdims=True)

        @pl.when(i == 0)
        def _():
            dg_ref[...] = dg
            db_ref[...] = db

        @pl.when(i > 0)
        def _():
            dg_ref[...] += dg
            db_ref[...] += db

    row = pl.BlockSpec((tr, d), lambda i: (i, 0))
    vec = pl.BlockSpec((1, d), lambda i: (0, 0))
    return pl.pallas_call(
        body, name=name, grid=(t // tr,), in_specs=[row, row, row, vec], out_specs=[row, row, vec, vec],
        out_shape=[SDS((t, d), F32), SDS((t, d), BF16), SDS((1, d), F32), SDS((1, d), F32)],
        compiler_params=_cparams("arbitrary"),
    )(dra, dxm, r, g)


def _tail(r3, g3, b3, gp, pp, g, b, target, name):
    t, d = r3.shape
    tr = _tile(t, 256, 16)

    def body(r3_ref, g3_ref, b3_ref, gp_ref, pp_ref, g_ref, b_ref, tg_ref, dr_ref, dgp_ref, dpp_ref, dg_ref, db_ref, sq_ref):
        i = pl.program_id(0)
        gate = _sigmoid(gp_ref[...])
        pp_ = pp_ref[...]
        r = ALPHA * (_ln_stats(r3_ref[...]) * g3_ref[...] + b3_ref[...]) + gate * pp_
        mu = jnp.mean(r, axis=-1, keepdims=True)
        xc = r - mu
        rstd = lax.rsqrt(jnp.mean(xc * xc, axis=-1, keepdims=True) + LN_EPS)
        xh = xc * rstd
        err = xh * g_ref[...] + b_ref[...] - tg_ref[...]
        dx = err * (1.0 / d)
        dxh = dx * g_ref[...]
        dr = rstd * (dxh - jnp.mean(dxh, axis=-1, keepdims=True) - xh * jnp.mean(dxh * xh, axis=-1, keepdims=True))
        dr_ref[...] = dr
        dgp_ref[...] = (dr * pp_ * gate * (1.0 - gate)).astype(BF16)
        dpp_ref[...] = (dr * gate).astype(BF16)
        dg = jnp.sum(dx * xh, axis=0, keepdims=True)
        db = jnp.sum(dx, axis=0, keepdims=True)
        sq = jnp.sum(err * err, axis=0, keepdims=True)

        @pl.when(i == 0)
        def _():
            dg_ref[...] = dg
            db_ref[...] = db
            sq_ref[...] = sq

        @pl.when(i > 0)
        def _():
            dg_ref[...] += dg
            db_ref[...] += db
            sq_ref[...] += sq

    row = pl.BlockSpec((tr, d), lambda i: (i, 0))
    vec = pl.BlockSpec((1, d), lambda i: (0, 0))
    return pl.pallas_call(
        body, name=name, grid=(t // tr,), in_specs=[row, vec, vec, row, row, vec, vec, row],
        out_specs=[row, row, row, vec, vec, vec],
        out_shape=[SDS((t, d), F32), SDS((t, d), BF16), SDS((t, d), BF16), SDS((1, d), F32), SDS((1, d), F32),
                   SDS((1, d), F32)],
        compiler_params=_cparams("arbitrary"),
    )(r3, g3, b3, gp, pp, g, b, target)


def _to_bf16(x, name):
    t, d = x.shape
    tr = _tile(t, 512, 16)
    row = pl.BlockSpec((tr, d), lambda i: (i, 0))

    def body(x_ref, o_ref):
        o_ref[...] = x_ref[...].astype(BF16)

    return pl.pallas_call(body, name=name, grid=(t // tr,), in_specs=[row], out_specs=row, out_shape=SDS((t, d), BF16),
                          compiler_params=_cparams("parallel"))(x)


def _concat_cols(parts, name):
    t = parts[0].shape[0]
    widths = [p_.shape[1] for p_ in parts]
    tr = _tile(t, 256, 16)

    def body(*refs):
        o_ref = refs[-1]
        at = 0
        for ref, wd in zip(refs[:-1], widths):
            o_ref[:, at:at + wd] = ref[...]
            at += wd

    return pl.pallas_call(
        body, name=name, grid=(t // tr,), in_specs=[pl.BlockSpec((tr, wd), lambda i: (i, 0)) for wd in widths],
        out_specs=pl.BlockSpec((tr, sum(widths)), lambda i: (i, 0)), out_shape=SDS((t, sum(widths)), parts[0].dtype),
        compiler_params=_cparams("parallel"),
    )(*parts)


def _merge_fwd(z, ma, mb, w, name):
    t = z.shape[0]
    tr = _tile(t, 256, 16)

    def body(gc_ref, gh_ref, ma_ref, mb_ref, o_ref):
        o_ref[...] = (_sigmoid(gc_ref[...]) * ma_ref[...] + _sigmoid(gh_ref[...]) * mb_ref[...]).astype(BF16)

    half = pl.BlockSpec((tr, w), lambda i, j: (i, j))
    return pl.pallas_call(
        body, name=name, grid=(t // tr, 2),
        in_specs=[pl.BlockSpec((tr, w), lambda i, j: (i, 7 + j)), pl.BlockSpec((tr, w), lambda i, j: (i, 9 + j)), half, half],
        out_specs=half, out_shape=SDS((t, 2 * w), BF16), compiler_params=_cparams("parallel", "parallel"),
    )(z, z, ma, mb)


def _merge_bwd(dmer, z, ma, mb, w, name):
    t = z.shape[0]
    tr = _tile(t, 256, 16)

    def body(d_ref, gc_ref, gh_ref, ma_ref, mb_ref, dma_ref, dmb_ref, dgc_ref, dgh_ref):
        dm = d_ref[...]
        sc = _sigmoid(gc_ref[...])
        sh = _sigmoid(gh_ref[...])
        dma_ref[...] = (dm * sc).astype(BF16)
        dmb_ref[...] = (dm * sh).astype(BF16)
        dgc_ref[...] = (dm * ma_ref[...] * sc * (1.0 - sc)).astype(BF16)
        dgh_ref[...] = (dm * mb_ref[...] * sh * (1.0 - sh)).astype(BF16)

    half = pl.BlockSpec((tr, w), lambda i, j: (i, j))
    return pl.pallas_call(
        body, name=name, grid=(t // tr, 2),
        in_specs=[half, pl.BlockSpec((tr, w), lambda i, j: (i, 7 + j)), pl.BlockSpec((tr, w), lambda i, j: (i, 9 + j)), half, half],
        out_specs=[half] * 4, out_shape=[SDS((t, 2 * w), BF16)] * 4, compiler_params=_cparams("parallel", "parallel"),
    )(dmer, z, z, ma, mb)


def _shift_down(x, s, row):
    return jnp.where(row >= s, pltpu.roll(x, s, axis=0), 0.0)


def _shift_up(x, s, row, t):
    return jnp.where(row < t - s, pltpu.roll(x, t - s, axis=0), 0.0)


def _conv_fwd(z, cw, w, name):
    t = z.shape[0]
    tc = LANES
    nb = w // tc

    def body(b_ref, c_ref, h_ref, w_ref, o_ref):
        u = c_ref[...] * h_ref[...]
        row = lax.broadcasted_iota(jnp.int32, u.shape, 0)
        cw_ = w_ref[...]
        conv = cw_[2:3, :] * u + cw_[1:2, :] * _shift_down(u, 1, row) + cw_[0:1, :] * _shift_down(u, 2, row)
        o_ref[...] = (b_ref[...] * conv).astype(BF16)

    col = lambda off: pl.BlockSpec((t, tc), lambda j: (0, off * nb + j))
    return pl.pallas_call(
        body, name=name, grid=(nb,), in_specs=[col(0), col(1), col(2), pl.BlockSpec((3, tc), lambda j: (0, j))],
        out_specs=pl.BlockSpec((t, tc), lambda j: (0, j)), out_shape=SDS((t, w), BF16), compiler_params=_cparams("parallel"),
    )(z, z, z, cw)


def _conv_bwd(dy, z, cw, w, name):
    t = z.shape[0]
    tc = LANES
    nb = w // tc

    def body(dy_ref, b_ref, c_ref, h_ref, w_ref, db_ref, dc_ref, dh_ref, dw_ref):
        c_, h_ = c_ref[...], h_ref[...]
        u = c_ * h_
        row = lax.broadcasted_iota(jnp.int32, u.shape, 0)
        cw_ = w_ref[...]
        u1 = _shift_down(u, 1, row)
        u2 = _shift_down(u, 2, row)
        dy_ = dy_ref[...]
        db_ref[...] = (dy_ * (cw_[2:3, :] * u + cw_[1:2, :] * u1 + cw_[0:1, :] * u2)).astype(BF16)
        dconv = dy_ * b_ref[...]
        du = cw_[2:3, :] * dconv + cw_[1:2, :] * _shift_up(dconv, 1, row, t) + cw_[0:1, :] * _shift_up(dconv, 2, row, t)
        dc_ref[...] = (du * h_).astype(BF16)
        dh_ref[...] = (du * c_).astype(BF16)
        dw_ref[0:1, :] = jnp.sum(dconv * u2, axis=0, keepdims=True)
        dw_ref[1:2, :] = jnp.sum(dconv * u1, axis=0, keepdims=True)
        dw_ref[2:3, :] = jnp.sum(dconv * u, axis=0, keepdims=True)

    col = lambda off: pl.BlockSpec((t, tc), lambda j: (0, off * nb + j))
    own = pl.BlockSpec((t, tc), lambda j: (0, j))
    wsp = pl.BlockSpec((3, tc), lambda j: (0, j))
    return pl.pallas_call(
        body, name=name, grid=(nb,), in_specs=[own, col(0), col(1), col(2), wsp], out_specs=[own, own, own, wsp],
        out_shape=[SDS((t, w), BF16)] * 3 + [SDS((3, w), F32)], compiler_params=_cparams("parallel"),
    )(dy, z, z, z, cw)


def _lower_bound(hg):
    mx = jnp.max(hg, axis=0, keepdims=True)
    e = jnp.exp(hg - mx)
    inv = 1.0 / jnp.sum(e, axis=0, keepdims=True)
    return e[0:1, :] * inv, e[1:2, :] * inv


def _chunk_cumsum(x, row):
    s = 1
    while s < CHUNK:
        x = x + jnp.where(row % CHUNK >= s, pltpu.roll(x, s, axis=0), 0.0)
        s *= 2
    return x


def _dot_nt(a, b):
    return lax.dot_general(a.astype(BF16), b.astype(BF16), (((1,), (1,)), ((), ())), preferred_element_type=F32)


def _dot_tn(a, b):
    return lax.dot_general(a.astype(BF16), b.astype(BF16), (((0,), (0,)), ((), ())), preferred_element_type=F32)


def _dot_nn(a, b):
    return jnp.dot(a.astype(BF16), b.astype(BF16), preferred_element_type=F32)


def _tril(x):
    r = lax.broadcasted_iota(jnp.int32, x.shape, 0)
    c = lax.broadcasted_iota(jnp.int32, x.shape, 1)
    return jnp.where(r >= c, x, 0.0)


HGRN_GROUP = 4
HGRN_ROWS = 512
HGRN_UNROLL = 2


def _unrolled_loop(n, step, init):
    assert n % HGRN_UNROLL == 0

    def trip(i, carry):
        for u in range(HGRN_UNROLL):
            carry = step(i * HGRN_UNROLL + u, carry)
        return carry

    return lax.fori_loop(0, n // HGRN_UNROLL, trip, init)


def _hgrn_chunk_inputs(q_ref, f_ref, cum_ref, lb, rows, ln):
    qr = q_ref[rows, ln]
    q = qr * _sigmoid(qr)
    f = lb + (1.0 - lb) * _sigmoid(f_ref[rows, ln])
    return q, 1.0 - f, cum_ref[rows, ln]


def _hgrn_fwd(z, hg, nw, w, name, comm=()):
    t = z.shape[0]
    nh = w // HEAD
    gh = _tile(nh, HGRN_GROUP, 1)
    gw = gh * HEAD
    ngrp = nh // gh
    tb = _tile(t, HGRN_ROWS, CHUNK)
    ncb = tb // CHUNK

    def body(q_ref, f_ref, i_ref, g_ref, hg_ref, nw_ref, y_ref, o_ref, st_ref, cum_ref, *s_refs):
        lb_all, _ = _lower_bound(hg_ref[...])
        row = lax.broadcasted_iota(jnp.int32, (tb, gw), 0)
        cum_ref[...] = _chunk_cumsum(jnp.log(lb_all + (1.0 - lb_all) * _sigmoid(f_ref[...])), row)

        @pl.when(pl.program_id(1) == 0)
        def _():
            for s_ref in s_refs:
                s_ref[...] = jnp.zeros_like(s_ref)

        def step(c, carry):
            rows = pl.ds(pl.multiple_of(c * CHUNK, CHUNK), CHUNK)
            for g in range(gh):
                ln = slice(g * HEAD, (g + 1) * HEAD)
                lb = lb_all[:, ln]
                q, k, cum = _hgrn_chunk_inputs(q_ref, f_ref, cum_ref, lb, rows, ln)
                v = i_ref[rows, ln]
                last = cum[CHUNK - 1:CHUNK, :]
                qe = q * jnp.exp(cum)
                st = s_refs[g][...]
                st_ref[g, c] = st.astype(BF16)
                o_ref[rows, ln] = _dot_nt(qe, st) + _dot_nn(_tril(_dot_nt(qe, k * jnp.exp(-cum))), v)
                s_refs[g][...] = st * jnp.exp(last) + _dot_tn(v, k * jnp.exp(last - cum))
            return carry

        _unrolled_loop(ncb, step, 0)
        for g in range(gh):
            ln = slice(g * HEAD, (g + 1) * HEAD)
            o = o_ref[:, ln]
            n = o * lax.rsqrt(jnp.mean(o * o, axis=-1, keepdims=True) + RMS_EPS)
            gr = g_ref[:, ln]
            y_ref[:, ln] = (n * nw_ref[...] * gr * _sigmoid(gr)).astype(BF16)

    col = lambda off: pl.BlockSpec((tb, gw), lambda h, j: (j, off * ngrp + h))
    own = pl.BlockSpec((tb, gw), lambda h, j: (j, h))
    return _hosted_call(
        body, comm, name=name, grid=(ngrp, t // tb),
        in_specs=[col(3), col(4), col(5), col(6), pl.BlockSpec((2, gw), lambda h, j: (0, h)),
                  pl.BlockSpec((1, HEAD), lambda h, j: (0, 0))],
        out_specs=[own, own, pl.BlockSpec((gh, ncb, HEAD, HEAD), lambda h, j: (h, j, 0, 0))],
        out_shape=[SDS((t, w), BF16), SDS((t, w), F32), SDS((nh, t // CHUNK, HEAD, HEAD), BF16)],
        scratch_shapes=[pltpu.VMEM((tb, gw), F32)] + [pltpu.VMEM((HEAD, HEAD), F32)] * gh,
        operands=(z, z, z, z, hg, nw), parallel=1)


def _hgrn_bwd(dy, z, o, states, hg, nw, w, name, comm=()):
    t = z.shape[0]
    nh = w // HEAD
    gh = _tile(nh, HGRN_GROUP, 1)
    gw = gh * HEAD
    ngrp = nh // gh
    tb = _tile(t, HGRN_ROWS, CHUNK)
    ncb = tb // CHUNK
    nt = t // tb

    def body(dy_ref, q_ref, f_ref, i_ref, g_ref, o_ref, st_ref, hg_ref, nw_ref,
             dq_ref, df_ref, di_ref, dg_ref, dhg_ref, dnw_ref, cum_ref, do_ref, *ds_refs):
        lb_all, s1_all = _lower_bound(hg_ref[...])
        row = lax.broadcasted_iota(jnp.int32, (tb, gw), 0)
        crow = lax.broadcasted_iota(jnp.int32, (CHUNK, HEAD), 0)
        cum_ref[...] = _chunk_cumsum(jnp.log(lb_all + (1.0 - lb_all) * _sigmoid(f_ref[...])), row)

        @pl.when(pl.program_id(1) == 0)
        def _():
            for ds_ref in ds_refs:
                ds_ref[...] = jnp.zeros_like(ds_ref)
            dhg_ref[...] = jnp.zeros_like(dhg_ref)
            dnw_ref[...] = jnp.zeros_like(dnw_ref)

        for g in range(gh):
            ln = slice(g * HEAD, (g + 1) * HEAD)
            o_ = o_ref[:, ln]
            rstd = lax.rsqrt(jnp.mean(o_ * o_, axis=-1, keepdims=True) + RMS_EPS)
            n = o_ * rstd
            gr = g_ref[:, ln]
            sg = _sigmoid(gr)
            dy_ = dy_ref[:, ln]
            dg_ref[:, ln] = (dy_ * n * nw_ref[...] * (sg * (1.0 + gr * (1.0 - sg)))).astype(BF16)
            dsil = dy_ * gr * sg
            dnw_ref[:, ln] += jnp.sum(dsil * n, axis=0, keepdims=True)
            dn = dsil * nw_ref[...]
            do_ref[:, ln] = rstd * (dn - n * jnp.mean(dn * n, axis=-1, keepdims=True))

        def step(cc, dlbs):
            c = ncb - 1 - cc
            rows = pl.ds(pl.multiple_of(c * CHUNK, CHUNK), CHUNK)
            new = []
            for g in range(gh):
                ln = slice(g * HEAD, (g + 1) * HEAD)
                lb = lb_all[:, ln]
                qr = q_ref[rows, ln]
                sq = _sigmoid(qr)
                q = qr * sq
                sf = _sigmoid(f_ref[rows, ln])
                f = lb + (1.0 - lb) * sf
                k = 1.0 - f
                cum = cum_ref[rows, ln]
                v = i_ref[rows, ln]
                do = do_ref[rows, ln]
                last = cum[CHUNK - 1:CHUNK, :]
                eg = jnp.exp(cum)
                eng = jnp.exp(-cum)
                elc = jnp.exp(last - cum)
                qe, ke, kl = q * eg, k * eng, k * elc
                ds = ds_refs[g][...]
                a = _tril(_dot_nt(qe, ke))
                da = _tril(_dot_nt(do, v))
                di_ref[rows, ln] = (_dot_tn(a, do) + _dot_nt(kl, ds)).astype(BF16)
                st = st_ref[g, c]
                dkl = _dot_nn(v, ds)
                dq = (_dot_nn(do, st) + _dot_nn(da, ke)) * eg
                dk = _dot_tn(da, qe) * eng + dkl * elc
                el = jnp.exp(last)
                ds_refs[g][...] = ds * el + _dot_tn(do, qe)
                dlast = jnp.sum(kl * dkl, axis=0, keepdims=True) + el * jnp.sum(ds * st.astype(F32), axis=0, keepdims=True)
                x = q * dq - k * dk + jnp.where(crow == CHUNK - 1, dlast, 0.0)
                s = 1
                while s < CHUNK:
                    x = x + _shift_up(x, s, crow, CHUNK)
                    s *= 2
                df = x / f - dk
                dq_ref[rows, ln] = (dq * (sq * (1.0 + qr * (1.0 - sq)))).astype(BF16)
                df_ref[rows, ln] = (df * (1.0 - lb) * sf * (1.0 - sf)).astype(BF16)
                new.append(dlbs[g] + jnp.sum(df * (1.0 - sf), axis=0, keepdims=True))
            return tuple(new)

        dlbs = _unrolled_loop(ncb, step, tuple(jnp.zeros((1, HEAD), F32) for _ in range(gh)))
        for g in range(gh):
            ln = slice(g * HEAD, (g + 1) * HEAD)
            dlb = dlbs[g] * lb_all[:, ln] * s1_all[:, ln]
            dhg_ref[0:1, ln] += dlb
            dhg_ref[1:2, ln] -= dlb

    col = lambda off: pl.BlockSpec((tb, gw), lambda h, j: (nt - 1 - j, off * ngrp + h))
    own = pl.BlockSpec((tb, gw), lambda h, j: (nt - 1 - j, h))
    hsp = pl.BlockSpec((2, gw), lambda h, j: (0, h))
    return _hosted_call(
        body, comm, name=name, grid=(ngrp, nt),
        in_specs=[own, col(3), col(4), col(5), col(6), own,
                  pl.BlockSpec((gh, ncb, HEAD, HEAD), lambda h, j: (h, nt - 1 - j, 0, 0)),
                  hsp, pl.BlockSpec((1, HEAD), lambda h, j: (0, 0))],
        out_specs=[own, own, own, own, hsp, pl.BlockSpec((1, gw), lambda h, j: (0, h))],
        out_shape=[SDS((t, w), BF16)] * 4 + [SDS((2, w), F32), SDS((1, w), F32)],
        scratch_shapes=[pltpu.VMEM((tb, gw), F32)] * 2 + [pltpu.VMEM((HEAD, HEAD), F32)] * gh,
        operands=(dy, z, z, z, z, o, states, hg, nw), parallel=1)


def _cast_pad(wt, n_pad, meta, sp, name, comm=()):
    _, r, n = wt.shape
    g, p, per = meta
    tr = _tile(r, max(16, (3 << 19) // n_pad // 16 * 16), 16)

    def body(w_ref, o_ref):
        if n_pad != n:
            o_ref[...] = jnp.zeros(o_ref.shape, o_ref.dtype)
        o_ref[:, 0:n] = w_ref[...].astype(BF16)

    return _hosted_call(
        body, comm, name=name, grid=(r // tr,), in_specs=[pl.BlockSpec((None, tr, n), lambda i, sp: (0, i, 0))],
        out_specs=[pl.BlockSpec((None, tr, n_pad), lambda i, sp: (sp[1] // per, ((sp[1] % per) * r) // tr + i, 0))],
        out_shape=[SDS((g, p, n_pad), BF16)], scratch_shapes=[], operands=(wt,), parallel=1, prefetch=sp)


def _cast_pad_t(wt_t, n_pad, meta, sp, name, comm=()):
    _, n, r = wt_t.shape
    g, p, per = meta
    tc = _tile(r, 256, LANES)

    def body(w_ref, o_ref):
        for lo in range(0, n_pad, LANES):
            rows = min(LANES, n - lo)
            piece = w_ref[lo:lo + rows, :]
            if rows < LANES:
                piece = jnp.concatenate([piece, jnp.zeros((LANES - rows, tc), F32)], axis=0)
            o_ref[:, lo:lo + LANES] = piece.T.astype(BF16)

    return _hosted_call(
        body, comm, name=name, grid=(r // tc,), in_specs=[pl.BlockSpec((None, n, tc), lambda i, sp: (0, 0, i))],
        out_specs=[pl.BlockSpec((None, tc, n_pad), lambda i, sp: (sp[1] // per, ((sp[1] % per) * r) // tc + i, 0))],
        out_shape=[SDS((g, p, n_pad), BF16)], scratch_shapes=[], operands=(wt_t,), parallel=1, prefetch=sp)


def _adam_math(w, g, m, v):
    m2 = ADAM_B1 * m + (1.0 - ADAM_B1) * g
    v2 = ADAM_B2 * v + (1.0 - ADAM_B2) * (g * g)
    c1 = 1.0 / (1.0 - ADAM_B1 ** ADAM_STEP)
    c2 = 1.0 / (1.0 - ADAM_B2 ** ADAM_STEP)
    return -ADAM_LR * ((m2 * c1) / (jnp.sqrt(v2 * c2) + ADAM_EPS) + ADAM_WD * w), m2, v2


def _adamw_t(wt_t, g, m_t, v_t, name):
    _, n, r = wt_t.shape
    ng = g.shape[1]
    tc = LANES

    def body(w_ref, g_ref, m_ref, v_ref, go_ref, d_ref, mo_ref, vo_ref, gt_ref):
        for lo in range(0, ng, LANES):
            gt_ref[lo:lo + LANES, :] = g_ref[:, lo:lo + LANES].T
        g_ = gt_ref[0:n, :]
        delta, m2, v2 = _adam_math(w_ref[...], g_, m_ref[...], v_ref[...])
        go_ref[...] = g_
        d_ref[...] = delta
        mo_ref[...] = m2
        vo_ref[...] = v2

    blk = pl.BlockSpec((None, n, tc), lambda i: (0, 0, i))
    return pl.pallas_call(
        body, name=name, grid=(r // tc,), in_specs=[blk, pl.BlockSpec((tc, ng), lambda i: (i, 0)), blk, blk],
        out_specs=[blk] * 4, out_shape=[SDS(wt_t.shape, F32)] * 4, scratch_shapes=[pltpu.VMEM((ng, tc), F32)],
        compiler_params=_cparams("parallel"),
    )(wt_t, g, m_t, v_t)


def _adamw(wt, g, m, v, name):
    lead = (None,) * (wt.ndim - 2)
    zero = (0,) * (wt.ndim - 2)
    r, n = wt.shape[-2:]
    ng = g.shape[1]
    nct = 2 if ng == n and n % (2 * LANES) == 0 else 1
    tc, tg = n // nct, ng // nct
    tr = _tile(r, max(8, (3 << 17) // tg // 8 * 8), 8)

    def body(w_ref, g_ref, m_ref, v_ref, go_ref, d_ref, mo_ref, vo_ref):
        g_ = g_ref[:, 0:tc]
        delta, m2, v2 = _adam_math(w_ref[...], g_, m_ref[...], v_ref[...])
        go_ref[...] = g_
        d_ref[...] = delta
        mo_ref[...] = m2
        vo_ref[...] = v2

    blk = pl.BlockSpec(lead + (tr, tc), lambda i, j: zero + (i, j))
    return pl.pallas_call(
        body, name=name, grid=(r // tr, nct), in_specs=[blk, pl.BlockSpec((tr, tg), lambda i, j: (i, j)), blk, blk],
        out_specs=[blk] * 4, out_shape=[SDS(wt.shape, F32)] * 4, compiler_params=_cparams("parallel", "parallel"),
    )(wt, g, m, v)


def _place():
    x, y, c = lax.axis_index("x"), lax.axis_index("y"), lax.axis_index("c")
    return x, y, c, 2 * x + y


def _chip_dev(k, c):
    return (k // 2, k % 2, c)


def _half(ref, j, h, rows, per):
    return ref.at[j // per, pl.ds((j % per) * rows + h * (rows // 2), rows // 2)]


def _gather_stage(bufs, metas, rows_of, ici_parts, fwd_parts, zero_pad):
    nw = len(bufs)
    ici_on = [i for i in range(nw) if ici_parts[i] is not None]
    fwd_on = [i for i in range(nw) if fwd_parts[i] is not None]
    pad_jobs = [(i, gi) for i in ici_on if ici_parts[i][0] == 0 and metas[i][1] > metas[i][2] * rows_of[i]
                for gi in range(metas[i][0])]

    def part_of(ref, i, j, h, part):
        per = metas[i][2]
        p, np_ = part
        pr = rows_of[i] // 2 // np_
        return ref.at[j // per, pl.ds((j % per) * rows_of[i] + h * (rows_of[i] // 2) + p * pr, pr)]

    def descriptors(ins, outs, sems):
        src, zp, dst = ins[:nw], ins[nw], outs
        pads, send, recv, fsend, frecv = sems
        x, y, c, me = _place()

        def pad(n):
            i, gi = pad_jobs[n]
            extra = metas[i][1] - metas[i][2] * rows_of[i]
            return pltpu.make_async_copy(zp.at[pl.ds(0, extra)], dst[i].at[gi, pl.ds(metas[i][2] * rows_of[i], extra)], pads.at[n])

        def ici(i, r, frm):
            return pltpu.make_async_remote_copy(
                src_ref=part_of(src[i], i, me, c, ici_parts[i]), dst_ref=part_of(dst[i], i, frm, c, ici_parts[i]),
                send_sem=send.at[i, r - 1], recv_sem=recv.at[i, r - 1], device_id=_chip_dev((me + r) % N_CHIPS, c),
                device_id_type=MESH)

        def d2d(i, r, frm, h):
            blk = part_of(dst[i], i, frm, h, fwd_parts[i])
            return pltpu.make_async_remote_copy(src_ref=blk, dst_ref=blk, send_sem=fsend.at[i, r - 1],
                                                recv_sem=frecv.at[i, r - 1], device_id=(x, y, 1 - c), device_id_type=MESH)

        return pad, ici, d2d, c, me

    def start(ins, outs, sems):
        pad, ici, d2d, c, me = descriptors(ins, outs, sems)
        for n in range(len(pad_jobs)):
            pad(n).start()
        for i in fwd_on:
            for r in range(1, N_CHIPS):
                d2d(i, r, (me - r) % N_CHIPS, c).start()
        for i in ici_on:
            for r in range(1, N_CHIPS):
                ici(i, r, me).start()

    def finish(ins, outs, sems):
        pad, ici, d2d, c, me = descriptors(ins, outs, sems)
        for i in fwd_on:
            for r in range(1, N_CHIPS):
                d2d(i, r, (me - r) % N_CHIPS, 1 - c).wait_recv()
                d2d(i, r, (me - r) % N_CHIPS, c).wait_send()
        for i in ici_on:
            for r in range(1, N_CHIPS):
                ici(i, r, (me - r) % N_CHIPS).wait_recv()
                ici(i, r, me).wait_send()
        for n in range(len(pad_jobs)):
            pad(n).wait()

    return _Stage(ins=list(bufs) + [zero_pad], out_shapes=[SDS(b.shape, b.dtype) for b in bufs],
                  aliases={i: i for i in range(nw)},
                  sems=[pltpu.SemaphoreType.DMA((max(len(pad_jobs), 1),))] + [pltpu.SemaphoreType.DMA((nw, N_CHIPS - 1))] * 4,
                  start=start, finish=finish)


def _gather_small(packed, name):
    r, n = packed.shape

    def body(src, dst, send, recv):
        x, y, c, me = _place()
        dst[me] = src[...]
        cps = []
        for d in range(1, N_CHIPS):
            cp = pltpu.make_async_remote_copy(src_ref=src, dst_ref=dst.at[me], send_sem=send.at[d - 1], recv_sem=recv.at[d - 1],
                                              device_id=_chip_dev((me + d) % N_CHIPS, c), device_id_type=MESH)
            cp.start()
            cps.append(cp)
        for d in range(1, N_CHIPS):
            pltpu.make_async_remote_copy(src_ref=src, dst_ref=dst.at[(me - d) % N_CHIPS], send_sem=send.at[d - 1],
                                         recv_sem=recv.at[d - 1], device_id=_chip_dev((me + d) % N_CHIPS, c),
                                         device_id_type=MESH).wait_recv()
        for cp in cps:
            cp.wait_send()

    return pl.pallas_call(
        body, name=name, in_specs=[VMEM_SPEC], out_specs=VMEM_SPEC, out_shape=SDS((N_CHIPS, r, n), F32),
        scratch_shapes=[pltpu.SemaphoreType.DMA((N_CHIPS - 1,))] * 2,
    )(packed)


def _all_reduce_small(packed, name):
    r, n = packed.shape

    def body(src, out, slots, send, recv):
        x, y, c, me = _place()
        idx = 2 * me + c
        slots[idx] = src[...]
        cps = []

        def peer(d):
            p = (idx + d) % N_DEV
            return (p // 4, (p // 2) % 2, p % 2)

        for d in range(1, N_DEV):
            cp = pltpu.make_async_remote_copy(src_ref=src, dst_ref=slots.at[idx], send_sem=send.at[d - 1], recv_sem=recv.at[d - 1],
                                              device_id=peer(d), device_id_type=MESH)
            cp.start()
            cps.append(cp)
        for d in range(1, N_DEV):
            pltpu.make_async_remote_copy(src_ref=src, dst_ref=slots.at[(idx - d) % N_DEV], send_sem=send.at[d - 1],
                                         recv_sem=recv.at[d - 1], device_id=peer(d), device_id_type=MESH).wait_recv()
        for cp in cps:
            cp.wait_send()
        acc = slots[0]
        for k in range(1, N_DEV):
            acc = acc + slots[k]
        out[...] = acc

    return pl.pallas_call(
        body, name=name, in_specs=[VMEM_SPEC], out_specs=VMEM_SPEC, out_shape=SDS((r, n), F32),
        scratch_shapes=[pltpu.VMEM((N_DEV, r, n), F32)] + [pltpu.SemaphoreType.DMA((N_DEV - 1,))] * 2,
    )(packed)


def _simple_stage(ins, out_shapes, aliases, n_copies, copies):
    def start(ins_, outs, sems):
        for cp in copies(ins_, outs, *sems):
            cp.start()

    def finish(ins_, outs, sems):
        for cp in copies(ins_, outs, *sems):
            cp.wait()

    return _Stage(ins=list(ins), out_shapes=list(out_shapes), aliases=aliases,
                  sems=[pltpu.SemaphoreType.DMA((n_copies,))] * 2, start=start, finish=finish)


def _rs_pair_exchange(grads, metas, rows_of):
    nw = len(grads)

    def copies(src, dst, send, recv):
        x, y, c, me = _place()
        return [pltpu.make_async_remote_copy(
            src_ref=_half(src[i], j, 1 - c, rows_of[i], metas[i][2]), dst_ref=dst[i].at[j], send_sem=send.at[i * N_CHIPS + j],
            recv_sem=recv.at[i * N_CHIPS + j], device_id=(x, y, 1 - c), device_id_type=MESH)
            for i in range(nw) for j in range(N_CHIPS)]

    out_shapes = [SDS((N_CHIPS, rows_of[i] // 2, g.shape[2]), g.dtype) for i, g in enumerate(grads)]
    return _simple_stage(grads, out_shapes, {}, nw * N_CHIPS, copies)


def _rs_pair_add(g, got, meta, rows, sp, name):
    per = meta[2]
    n = g.shape[2]
    hr = rows // 2
    tr = _tile(hr, max(16, (3 << 19) // n // 16 * 16), 16)

    def body(sp_ref, g_ref, got_ref, snd_ref, own_ref):
        j = pl.program_id(1)
        s = g_ref[...].astype(F32) + got_ref[...].astype(F32)
        snd_ref[...] = s.astype(BF16)

        @pl.when(j == sp_ref[1])
        def _():
            own_ref[...] = s

    grid_spec = pltpu.PrefetchScalarGridSpec(
        num_scalar_prefetch=1, grid=(hr // tr, N_CHIPS),
        in_specs=[pl.BlockSpec((None, tr, n), lambda i, j, sp: (j // per, ((j % per) * rows + sp[0] * hr) // tr + i, 0)),
                  pl.BlockSpec((None, tr, n), lambda i, j, sp: (j, i, 0))],
        out_specs=[pl.BlockSpec((None, tr, n), lambda i, j, sp: (j, i, 0)), pl.BlockSpec((tr, n), lambda i, j, sp: (i, 0))])
    return pl.pallas_call(
        body, name=name, grid_spec=grid_spec, out_shape=[SDS((N_CHIPS, hr, n), BF16), SDS((hr, n), F32)],
        compiler_params=_cparams("parallel", "arbitrary"),
    )(sp, g, got)


def _rs_chip_exchange(sends, part=(0, 1), prev=None):
    nw = len(sends)
    p, np_ = part

    def copies(src, dst, send, recv):
        x, y, c, me = _place()
        cps = []
        for i in range(nw):
            pr = sends[i].shape[1] // np_
            for r in range(1, N_CHIPS):
                cps.append(pltpu.make_async_remote_copy(
                    src_ref=src[i].at[(me + r) % N_CHIPS, pl.ds(p * pr, pr)], dst_ref=dst[i].at[r - 1, pl.ds(p * pr, pr)],
                    send_sem=send.at[i * (N_CHIPS - 1) + r - 1], recv_sem=recv.at[i * (N_CHIPS - 1) + r - 1],
                    device_id=_chip_dev((me + r) % N_CHIPS, c), device_id_type=MESH))
        return cps

    out_shapes = [SDS((N_CHIPS - 1,) + s.shape[1:], BF16) for s in sends]
    if prev is None:
        return _simple_stage(sends, out_shapes, {}, nw * (N_CHIPS - 1), copies)
    return _simple_stage(list(sends) + list(prev), out_shapes, {nw + i: i for i in range(nw)}, nw * (N_CHIPS - 1), copies)


def _rs_chip_add(own, got, sp, name):
    hr, n = own.shape
    tr = _tile(hr, max(16, (3 << 19) // n // 16 * 16), 16)

    def body(sp_ref, own_ref, got_ref, o_ref):
        acc = own_ref[...]
        for r in range(N_CHIPS - 1):
            acc = acc + got_ref[r].astype(F32)
        o_ref[...] = acc

    grid_spec = pltpu.PrefetchScalarGridSpec(
        num_scalar_prefetch=1, grid=(hr // tr,),
        in_specs=[pl.BlockSpec((tr, n), lambda i, sp: (i, 0)), pl.BlockSpec((N_CHIPS - 1, tr, n), lambda i, sp: (0, i, 0))],
        out_specs=pl.BlockSpec((tr, n), lambda i, sp: (sp[0] * (hr // tr) + i, 0)))
    return pl.pallas_call(body, name=name, grid_spec=grid_spec, out_shape=SDS((2 * hr, n), F32),
                          compiler_params=_cparams("parallel"))(sp, own, got)


def _rs_pair_share(blocks):
    nw = len(blocks)

    def copies(src, dst, send, recv):
        x, y, c, me = _place()
        cps = []
        for i in range(nw):
            hr = src[i].shape[0] // 2
            cps.append(pltpu.make_async_remote_copy(
                src_ref=src[i].at[pl.ds(c * hr, hr)], dst_ref=dst[i].at[pl.ds(c * hr, hr)], send_sem=send.at[i],
                recv_sem=recv.at[i], device_id=(x, y, 1 - c), device_id_type=MESH))
        return cps

    return _simple_stage(blocks, [SDS(b.shape, b.dtype) for b in blocks], {i: i for i in range(nw)}, nw, copies)


def kernel(x, p, ln_g, ln_b, ffn1_w_in, ffn1_w_out, mix_w_in, conv_w, hg_lower_bound, hg_norm_w, branch_w_conv, branch_w_hgrn, mix_w_out, ffn2_w_in, ffn2_w_out, ple_w_gate, ple_w_proj, loss_target, m_ln_g, m_ln_b, m_ffn1_w_in, m_ffn1_w_out, m_mix_w_in, m_conv_w, m_hg_lower_bound, m_hg_norm_w, m_branch_w_conv, m_branch_w_hgrn, m_mix_w_out, m_ffn2_w_in, m_ffn2_w_out, m_ple_w_gate, m_ple_w_proj, v_ln_g, v_ln_b, v_ffn1_w_in, v_ffn1_w_out, v_mix_w_in, v_conv_w, v_hg_lower_bound, v_hg_norm_w, v_branch_w_conv, v_branch_w_hgrn, v_mix_w_out, v_ffn2_w_in, v_ffn2_w_out, v_ple_w_gate, v_ple_w_proj):
    assert ln_g.shape[0] == DEPTH and x.shape[0] == 1 and p.shape[:2] == (1, 1)
    t, d = x.shape[1], x.shape[2]
    w = d // 2
    x0 = x.reshape(t, d)
    x0b = _to_bf16(x0, "x_bf16")
    pe = p.reshape(t, p.shape[-1])
    target = loss_target.reshape(t, d)
    cx, cy, cc = lax.axis_index("x"), lax.axis_index("y"), lax.axis_index("c")
    chip = 2 * cx + cy
    sp = jnp.stack([cc, chip]).astype(jnp.int32)

    big = dict(ffn1_w_in=ffn1_w_in, ffn1_w_out=ffn1_w_out, mix_w_in=mix_w_in, branch_w_conv=branch_w_conv,
               branch_w_hgrn=branch_w_hgrn, mix_w_out=mix_w_out, ffn2_w_in=ffn2_w_in, ffn2_w_out=ffn2_w_out,
               ple_w_gate=ple_w_gate, ple_w_proj=ple_w_proj)
    moments = dict(ffn1_w_in=(m_ffn1_w_in, v_ffn1_w_in), ffn1_w_out=(m_ffn1_w_out, v_ffn1_w_out), mix_w_in=(m_mix_w_in, v_mix_w_in),
                   branch_w_conv=(m_branch_w_conv, v_branch_w_conv), branch_w_hgrn=(m_branch_w_hgrn, v_branch_w_hgrn),
                   mix_w_out=(m_mix_w_out, v_mix_w_out), ffn2_w_in=(m_ffn2_w_in, v_ffn2_w_in), ffn2_w_out=(m_ffn2_w_out, v_ffn2_w_out),
                   ple_w_gate=(m_ple_w_gate, v_ple_w_gate), ple_w_proj=(m_ple_w_proj, v_ple_w_proj))
    names = list(big)

    n_loc = ffn1_w_in.shape[-1]
    n_pad = -(-n_loc // LANES) * LANES
    assert mix_w_in.shape[-1] % LANES == 0 and ffn1_w_out.shape[1] * 2 == n_loc
    pad_cols = dict(ffn1_w_in=n_pad, ffn2_w_in=n_pad)
    meta = {k: (N_CHIPS, big[k].shape[1], 1) for k in names}
    meta["ffn1_w_out"] = meta["ffn2_w_out"] = (2, n_pad, 2)
    rows = {k: big[k].shape[1] for k in names}
    swap = lambda a: jnp.transpose(a, (0, 2, 1))
    wbuf = {}
    zero_pad = jnp.zeros((max(n_pad - n_loc, 16), d), BF16)

    def cast(k, comm=()):
        if k in pad_cols:
            return _cast_pad_t(swap(big[k]), pad_cols[k], meta[k], sp, "cast_" + k, comm=comm)
        return _cast_pad(big[k], big[k].shape[2], meta[k], sp, "cast_" + k, comm=comm)

    def gather(ici=(), fwd=()):
        ks = list(dict.fromkeys([k for k, _, _ in ici] + [k for k, _, _ in fwd]))
        ip = {k: (p_, n_) for k, p_, n_ in ici}
        fp = {k: (p_, n_) for k, p_, n_ in fwd}
        return _gather_stage([wbuf[k] for k in ks], [meta[k] for k in ks], [rows[k] for k in ks], [ip.get(k) for k in ks],
                             [fp.get(k) for k in ks], zero_pad), ks

    def gathered(ks, outs):
        wbuf.update(zip(ks, outs))

    def w3(k):
        return wbuf[k]

    def w2(k):
        return wbuf[k].reshape(-1, wbuf[k].shape[2])

    dq, wq = d // N_CHIPS, w // N_CHIPS
    small = jnp.concatenate([ln_g[0], ln_b[0], jnp.pad(conv_w[0], ((0, 5), (0, dq - wq)))], axis=0)
    small = _gather_small(small, "gather_small")
    lng = small[:, 0:4, :].transpose(1, 0, 2).reshape(4, 1, d)
    lnb = small[:, 4:8, :].transpose(1, 0, 2).reshape(4, 1, d)
    cw = small[:, 8:11, :wq].transpose(1, 0, 2).reshape(3, w)
    hg = hg_lower_bound
    nw_ = hg_norm_w

    one = lambda *ks_: [(k, 0, 1) for k in ks_]
    wbuf["ffn1_w_in"] = cast("ffn1_w_in")
    carriers = ["ple_w_proj", "ffn1_w_out", "mix_w_in", None] + [k for k in names if k not in ("ffn1_w_in", "ple_w_proj", "ffn1_w_out", "mix_w_in")]
    assert len(carriers) > FIRST_GATHER_PARTS
    for step, k in enumerate(carriers):
        ici = [("ffn1_w_in", step, FIRST_GATHER_PARTS)] if step < FIRST_GATHER_PARTS else []
        fwd = [("ffn1_w_in", step - 1, FIRST_GATHER_PARTS)] if 1 <= step <= FIRST_GATHER_PARTS else []
        ici += one("ple_w_proj") if step == 1 else []
        fwd += one("ple_w_proj") if step == 2 else []
        if ici or fwd:
            st, ks = gather(ici, fwd)
            if k is None:
                pp, got = _mm(pe, w3("ple_w_proj"), name="ple_proj", b_blocked=True, tn=512, comm=[st])
            else:
                wbuf[k], got = cast(k, comm=[st])
            gathered(ks, got)
        else:
            wbuf[k] = cast(k)
    st, ks = gather(ici=one("ffn1_w_out") + [("mix_w_in", 0, 2)])
    z1, got = _mm(x0b, w3("ffn1_w_in"), name="ffn1_in", b_blocked=True, out_dtype=BF16, tm=1024, comm=[st])
    gathered(ks, got)
    st, ks = gather(fwd=one("ffn1_w_out") + [("mix_w_in", 0, 2)])
    h1, got = _swiglu_fwd(z1, "ffn1_act", comm=[st])
    gathered(ks, got)
    st, ks = gather(ici=[("mix_w_in", 1, 2)])
    y1, got = _mm(h1, w2("ffn1_w_out"), name="ffn1_out", tm=1024, tn=1024, tk=2816, comm=[st])
    gathered(ks, got)
    st, ks = gather(fwd=[("mix_w_in", 1, 2)])
    (r1, x1b), got = _ln_fwd(x0, y1, lng[0], lnb[0], None, None, 0.5, "ln0", comm=[st])
    gathered(ks, got)
    mixo_w = one("branch_w_conv", "branch_w_hgrn", "mix_w_out")
    st, ks = gather(ici=mixo_w + [("ffn2_w_in", 0, 2)])
    z, got = _mm(x1b, w3("mix_w_in"), name="mix_in", b_blocked=True, tm=1024, comm=[st])
    gathered(ks, got)
    ya = _conv_fwd(z, cw, w, "conv_fwd")
    st, ks = gather(ici=[("ffn2_w_in", 1, 2)], fwd=mixo_w + [("ffn2_w_in", 0, 2)])
    (yb, o_h, states), got = _hgrn_fwd(z, hg, nw_, w, "hgrn_fwd", comm=[st])
    gathered(ks, got)
    ma = _mm(ya, w3("branch_w_conv"), name="branch_conv", b_blocked=True, tn=512)
    mb = _mm(yb, w3("branch_w_hgrn"), name="branch_hgrn", b_blocked=True, tn=512)
    merged = _merge_fwd(z, ma, mb, w, "merge_fwd")
    st, ks = gather(fwd=[("ffn2_w_in", 1, 2)])
    y2, got = _mm(merged, w2("mix_w_out"), name="mix_out", tn=1024, comm=[st])
    gathered(ks, got)
    r2, x2b = _ln_fwd(r1, y2, lng[1], lnb[1], lng[0], lnb[0], 1.0, "ln1")
    late = one("ffn2_w_out", "ple_w_gate")
    st, ks = gather(ici=late)
    z3, got = _mm(x2b, w3("ffn2_w_in"), name="ffn2_in", b_blocked=True, out_dtype=BF16, tm=1024, comm=[st])
    gathered(ks, got)
    st, ks = gather(fwd=late)
    h3, got = _swiglu_fwd(z3, "ffn2_act", comm=[st])
    gathered(ks, got)
    y3 = _mm(h3, w2("ffn2_w_out"), name="ffn2_out", tm=1024, tn=1024, tk=2816)
    r3, x3b = _ln_fwd(r2, y3, lng[2], lnb[2], lng[1], lnb[1], 0.5, "ln2")
    gp = _mm(x3b, w2("ple_w_gate"), name="ple_gate", tn=1024)
    dr4, dgp, dpp, dg3, db3, sq = _tail(r3, lng[2], lnb[2], gp, pp, lng[3], lnb[3], target, "tail")

    grads, sends, owns, blocks, outs = {}, {}, {}, {}, {}

    def pair_exchange(*ks):
        return _rs_pair_exchange([grads[k] for k in ks], [meta[k] for k in ks], [rows[k] for k in ks])

    def pair_add(ks, got):
        for k, g_ in zip(ks, got):
            sends[k], owns[k] = _rs_pair_add(grads[k], g_, meta[k], rows[k], sp, "rs_pair_add_" + k)

    def chip_exchange(*ks):
        return _rs_chip_exchange([sends[k] for k in ks])

    def chip_add(ks, got):
        for k, g_ in zip(ks, got):
            blocks[k] = _rs_chip_add(owns[k], g_, sp, "rs_chip_add_" + k)

    def pair_share(*ks):
        return _rs_pair_share([blocks[k] for k in ks])

    def update(ks, full):
        for k, g_ in zip(ks, full):
            m_, v_ = moments[k]
            if k in pad_cols:
                outs[k] = [swap(a) for a in _adamw_t(swap(big[k]), g_, swap(m_), swap(v_), "adamw_" + k)]
            else:
                outs[k] = _adamw(big[k], g_, m_, v_, "adamw_" + k)

    ple = ("ple_w_gate", "ple_w_proj")
    mixo = ("mix_w_out", "branch_w_conv", "branch_w_hgrn")
    dx3m = _mm(dgp, w2("ple_w_gate"), name="d_ple_gate_x", tb=True, tn=1024, tk=2048)
    grads["ple_w_gate"] = _mm(x3b, dgp, name="d_ple_gate_w", ta=True, out_dtype=BF16, tm=1024, tk=2048, tn=1024).reshape(N_CHIPS, -1, d)
    grads["ple_w_proj"] = _mm(pe, dpp, name="d_ple_proj_w", ta=True, out_dtype=BF16, out_blocked=N_CHIPS, tk=2048, tn=512)
    dr3, dy3b, dg2, db2 = _ln_bwd(dr4, dx3m, r3, lng[2], 0.5, "ln2_bwd")
    late_w = ple + ("ffn2_w_out",)
    dh3 = _mm(dy3b, w2("ffn2_w_out"), name="d_ffn2_out_x", tb=True, out_dtype=BF16, tn=1408, tk=2048)
    grads["ffn2_w_out"] = _mm(h3, dy3b, name="d_ffn2_out_w", ta=True, out_dtype=BF16, tm=1408, tk=2048, tn=1024).reshape(2, n_pad, d)
    dz3 = _swiglu_bwd(dh3, z3, "ffn2_act_bwd")
    dx2m, got = _mm(dz3, w3("ffn2_w_in"), name="d_ffn2_in_x", tb=True, b_blocked=True, tm=1024, tn=1024, tk=2816,
                    comm=[pair_exchange(*late_w)])
    pair_add(late_w, got)
    grads["ffn2_w_in"], got = _mm(x2b, dz3, name="d_ffn2_in_w", ta=True, out_dtype=BF16, out_blocked=N_CHIPS, tk=4096, comm=[chip_exchange(*late_w)])
    chip_add(late_w, got)
    dr2, dy2b, dg1, db1 = _ln_bwd(dr3, dx2m, r2, lng[1], 1.0, "ln1_bwd")
    dmer, got = _mm(dy2b, w2("mix_w_out"), name="d_mix_out_x", tb=True, tn=1024, tk=2048, comm=[pair_exchange("ffn2_w_in")])
    pair_add(["ffn2_w_in"], got)
    g_, full = _mm(merged, dy2b, name="d_mix_out_w", ta=True, out_dtype=BF16, tm=1024, tk=2048, tn=1024, comm=[pair_share(*late_w)])
    grads["mix_w_out"] = g_.reshape(N_CHIPS, -1, d)
    update(late_w, full)
    dma, dmb, dgc, dgh = _merge_bwd(dmer, z, ma, mb, w, "merge_bwd")
    dya = _mm(dma, w3("branch_w_conv"), name="d_branch_conv_x", tb=True, b_blocked=True, tn=1024, tk=512)
    dyb = _mm(dmb, w3("branch_w_hgrn"), name="d_branch_hgrn_x", tb=True, b_blocked=True, tn=1024, tk=512)
    grads["branch_w_conv"] = _mm(ya, dma, name="d_branch_conv_w", ta=True, out_dtype=BF16, out_blocked=N_CHIPS, tm=1024, tk=2048, tn=512)
    grads["branch_w_hgrn"] = _mm(yb, dmb, name="d_branch_hgrn_w", ta=True, out_dtype=BF16, out_blocked=N_CHIPS, tm=1024, tk=2048, tn=512)
    dbg, dcg, dhc, dcw = _conv_bwd(dya, z, cw, w, "conv_bwd")
    (dq_, df_, di_, dgr_, dhg, dnw), got2, got = _hgrn_bwd(dyb, z, o_h, states, hg, nw_, w, "hgrn_bwd",
                                                            comm=[chip_exchange("ffn2_w_in"), pair_exchange(*mixo)])
    chip_add(["ffn2_w_in"], got2)
    pair_add(mixo, got)
    dz = _concat_cols([dbg, dcg, dhc, dq_, df_, di_, dgr_, dgc, dgh], "dz_concat")
    dx1m, full, got = _mm(dz, w3("mix_w_in"), name="d_mix_in_x", tb=True, b_blocked=True, tm=1024, tn=1024, tk=2816,
                          comm=[pair_share("ffn2_w_in"), chip_exchange(*mixo)])
    update(["ffn2_w_in"], full)
    chip_add(mixo, got)
    grads["mix_w_in"], full = _mm(x1b, dz, name="d_mix_in_w", ta=True, out_dtype=BF16, out_blocked=N_CHIPS, tk=4096, comm=[pair_share(*mixo)])
    update(mixo, full)
    dr1, dy1b, dg0, db0 = _ln_bwd(dr2, dx1m, r1, lng[0], 0.5, "ln0_bwd")
    dh1, got = _mm(dy1b, w2("ffn1_w_out"), name="d_ffn1_out_x", tb=True, out_dtype=BF16, tn=1408, tk=2048,
                   comm=[pair_exchange("mix_w_in")])
    pair_add(["mix_w_in"], got)
    mix_sends = [sends["mix_w_in"]]
    g_, got_a = _mm(h1, dy1b, name="d_ffn1_out_w", ta=True, out_dtype=BF16, tm=1408, tk=2048, tn=1024, comm=[_rs_chip_exchange(mix_sends, (0, 2))])
    grads["ffn1_w_out"] = g_.reshape(2, n_pad, d)
    dz1 = _swiglu_bwd(dh1, z1, "ffn1_act_bwd")
    g_other, got2, got = _mm(x0b, dz1, name="d_ffn1_in_w_other", ta=True, out_dtype=BF16, out_blocked=N_CHIPS, tk=4096, half=(sp, True),
                             comm=[_rs_chip_exchange(mix_sends, (1, 2), got_a), pair_exchange("ffn1_w_out")])
    chip_add(["mix_w_in"], got2)
    pair_add(["ffn1_w_out"], got)
    grads["ffn1_w_in"], full, got2, got = _mm(
        x0b, dz1, name="d_ffn1_in_w_own", ta=True, out_dtype=BF16, out_blocked=N_CHIPS, tk=4096, half=(sp, False),
        comm=[pair_share("mix_w_in"), chip_exchange("ffn1_w_out"),
              _rs_pair_exchange([g_other], [meta["ffn1_w_in"]], [rows["ffn1_w_in"]])])
    update(["mix_w_in"], full)
    chip_add(["ffn1_w_out"], got2)
    pair_add(["ffn1_w_in"], got)
    dx0, got2, full = _mm(dz1, w3("ffn1_w_in"), name="d_ffn1_in_x", tb=True, b_blocked=True, tm=1024, tn=1024, tk=2816,
                          add=(dr1, ALPHA), comm=[chip_exchange("ffn1_w_in"), pair_share("ffn1_w_out")])
    chip_add(["ffn1_w_in"], got2)
    update(["ffn1_w_out"], full)
    grad_x = dx0.reshape(x.shape)
    update(["ffn1_w_in"], _run_stages([pair_share("ffn1_w_in")], "rs_tail_pair")[0])

    pack = jnp.concatenate([
        dg0, dg1, dg2, dg3, db0, db1, db2, db3,
        jnp.pad(dcw, ((0, 0), (0, d - w))), jnp.pad(dhg, ((0, 0), (0, d - w))),
        jnp.pad(jnp.sum(dnw.reshape(-1, HEAD), axis=0, keepdims=True), ((0, 0), (0, d - HEAD))), sq], axis=0)
    pack = _all_reduce_small(jnp.pad(pack, ((0, 1), (0, 0))), "reduce_small")
    loss = (0.5 / d) * jnp.sum(pack[14])
    g_ln_g = lax.dynamic_slice_in_dim(pack[0:4], chip * dq, dq, axis=1)
    g_ln_b = lax.dynamic_slice_in_dim(pack[4:8], chip * dq, dq, axis=1)
    g_conv = lax.dynamic_slice_in_dim(pack[8:11, :w], chip * wq, wq, axis=1)
    g_hg = pack[11:13, :w]
    g_nw = pack[13:14, :HEAD]

    small_w = dict(ln_g=(ln_g, g_ln_g, m_ln_g, v_ln_g), ln_b=(ln_b, g_ln_b, m_ln_b, v_ln_b),
                   conv_w=(conv_w, g_conv, m_conv_w, v_conv_w), hg_lower_bound=(hg_lower_bound, g_hg, m_hg_lower_bound, v_hg_lower_bound),
                   hg_norm_w=(hg_norm_w, g_nw, m_hg_norm_w, v_hg_norm_w))
    for k, (w_, g_, m_, v_) in small_w.items():
        outs[k] = _adamw(w_, g_.reshape(-1, w_.shape[-1]), m_, v_, "adamw_" + k)

    order = ["ln_g", "ln_b", "ffn1_w_in", "ffn1_w_out", "mix_w_in", "conv_w", "hg_lower_bound", "hg_norm_w", "branch_w_conv",
             "branch_w_hgrn", "mix_w_out", "ffn2_w_in", "ffn2_w_out", "ple_w_gate", "ple_w_proj"]
    return (loss, grad_x, *[outs[k][0] for k in order], *[outs[k][1] for k in order], *[outs[k][2] for k in order],
            *[outs[k][3] for k in order])
```

```python
import collections
import functools

import jax
import jax.numpy as jnp
from jax import lax
from jax.experimental import pallas as pl
from jax.experimental.pallas import tpu as pltpu

F32 = jnp.float32
BF16 = jnp.bfloat16
MESH = pl.DeviceIdType.MESH
ANY = pl.BlockSpec(memory_space=pl.ANY)
VMEM_SPEC = pl.BlockSpec(memory_space=pltpu.VMEM)
SDS = jax.ShapeDtypeStruct

DEPTH = 1
ALPHA = (2.0 * DEPTH) ** 0.25
LN_EPS = 1e-5
RMS_EPS = 1e-6
CHUNK = 32
HEAD = 128
ADAM_LR, ADAM_B1, ADAM_B2, ADAM_EPS, ADAM_WD, ADAM_STEP = 0.001, 0.9, 0.999, 1e-08, 0.01, 10

LANES = 128
N_CHIPS = 4
N_DEV = 8
FIRST_GATHER_PARTS = 8
VMEM_LIMIT = 52 * 1024 * 1024
MM_PIECE = 512


def _cparams(*sem):
    if sem:
        return pltpu.CompilerParams(dimension_semantics=sem, vmem_limit_bytes=VMEM_LIMIT)
    return pltpu.CompilerParams(vmem_limit_bytes=VMEM_LIMIT)


def _tile(n, target, mult):
    best = None
    for t in range(mult, min(n, target) + 1, mult):
        if n % t == 0:
            best = t
    return best if best is not None else n


def _sigmoid(x):
    return 1.0 / (1.0 + jnp.exp(-x))


_Stage = collections.namedtuple("_Stage", "ins out_shapes aliases sems start finish")


def _hosted_call(compute, stages, *, name, grid, in_specs, out_specs, out_shape, scratch_shapes, operands, parallel,
                 prefetch=None):
    n_cmp, n_out, n_scr = len(in_specs), len(out_specs), len(scratch_shapes)
    n_in = n_cmp
    n_pre = int(prefetch is not None)
    c_in = [len(s.ins) for s in stages]
    c_out = [len(s.out_shapes) for s in stages]
    c_sem = [len(s.sems) for s in stages]
    aliases = {}
    for si, s in enumerate(stages):
        for a_in, a_out in s.aliases.items():
            aliases[n_pre + n_in + sum(c_in[:si]) + a_in] = n_out + sum(c_out[:si]) + a_out

    def body(*refs):
        refs = refs[n_pre:]
        ins = refs[:n_cmp]
        cins = refs[n_in:n_in + sum(c_in)]
        outs = refs[n_in + sum(c_in):n_in + sum(c_in) + n_out]
        couts = refs[n_in + sum(c_in) + n_out:n_in + sum(c_in) + n_out + sum(c_out)]
        scr = refs[n_in + sum(c_in) + n_out + sum(c_out):][:n_scr]
        sems = refs[n_in + sum(c_in) + n_out + sum(c_out) + n_scr:]

        def stage_refs(si):
            return (cins[sum(c_in[:si]):sum(c_in[:si + 1])], couts[sum(c_out[:si]):sum(c_out[:si + 1])],
                    sems[sum(c_sem[:si]):sum(c_sem[:si + 1])])

        if stages:
            first = functools.reduce(jnp.logical_and, [pl.program_id(ax) == 0 for ax in range(len(grid))])
            last = functools.reduce(jnp.logical_and, [pl.program_id(ax) == grid[ax] - 1 for ax in range(len(grid))])

            @pl.when(first)
            def _():
                for si, s in enumerate(stages):
                    s.start(*stage_refs(si))

        compute(*ins, *outs, *scr)
        if stages:
            @pl.when(last)
            def _():
                for si, s in enumerate(stages):
                    s.finish(*stage_refs(si))

    sem = ("arbitrary",) * len(grid) if stages else ("parallel",) * parallel + ("arbitrary",) * (len(grid) - parallel)
    all_in = list(in_specs) + [ANY] * (n_in - n_cmp + sum(c_in))
    all_out = list(out_specs) + [ANY] * sum(c_out)
    all_scr = list(scratch_shapes) + [q for s in stages for q in s.sems]
    all_shape = list(out_shape) + [o for s in stages for o in s.out_shapes]
    args = list(operands) + [a for s in stages for a in s.ins]
    if prefetch is None:
        res = pl.pallas_call(body, name=name, grid=grid, in_specs=all_in, out_specs=all_out, out_shape=all_shape,
                             input_output_aliases=aliases, scratch_shapes=all_scr, compiler_params=_cparams(*sem))(*args)
    else:
        grid_spec = pltpu.PrefetchScalarGridSpec(num_scalar_prefetch=1, grid=grid, in_specs=all_in, out_specs=all_out,
                                                 scratch_shapes=all_scr)
        res = pl.pallas_call(body, name=name, grid_spec=grid_spec, out_shape=all_shape, input_output_aliases=aliases,
                             compiler_params=_cparams(*sem))(prefetch, *args)
    main = res[0] if n_out == 1 else list(res[:n_out])
    if not stages:
        return main
    rest = res[n_out:]
    return (main, *[list(rest[sum(c_out[:si]):sum(c_out[:si + 1])]) for si in range(len(stages))])


def _run_stages(stages, name):
    def body(*refs):
        n_i = sum(len(s.ins) for s in stages)
        n_o = sum(len(s.out_shapes) for s in stages)
        cins, couts, sems = refs[:n_i], refs[n_i:n_i + n_o], refs[n_i + n_o:]
        pos = [0, 0, 0]
        parts = []
        for s in stages:
            parts.append((cins[pos[0]:pos[0] + len(s.ins)], couts[pos[1]:pos[1] + len(s.out_shapes)], sems[pos[2]:pos[2] + len(s.sems)]))
            pos = [pos[0] + len(s.ins), pos[1] + len(s.out_shapes), pos[2] + len(s.sems)]
        for s, p_ in zip(stages, parts):
            s.start(*p_)
        for s, p_ in zip(stages, parts):
            s.finish(*p_)

    aliases, ni, no = {}, 0, 0
    for s in stages:
        for a_in, a_out in s.aliases.items():
            aliases[ni + a_in] = no + a_out
        ni, no = ni + len(s.ins), no + len(s.out_shapes)
    res = pl.pallas_call(
        body, name=name, in_specs=[ANY] * ni, out_specs=[ANY] * no, out_shape=[o for s in stages for o in s.out_shapes],
        input_output_aliases=aliases, scratch_shapes=[q for s in stages for q in s.sems],
    )(*[a for s in stages for a in s.ins])
    out, pos = [], 0
    for s in stages:
        out.append(list(res[pos:pos + len(s.out_shapes)]))
        pos += len(s.out_shapes)
    return out


def _mm(a, b, *, name, ta=False, tb=False, b_blocked=False, out_blocked=0, out_dtype=F32,
        tm=512, tn=1408, tk=2048, comm=(), half=None, add=None):
    a2 = a if isinstance(a, tuple) else None
    b2 = b if isinstance(b, tuple) else None
    if a2:
        assert not ta and a2[0].shape == a2[1].shape
        m, kd = a2[0].shape[0], 2 * a2[0].shape[1]
    elif ta:
        kd, m = a.shape
    else:
        m, kd = a.shape
    if b2:
        assert not tb and not b_blocked and b2[0].shape == b2[1].shape
        b = SDS((b2[0].shape[0], 2 * b2[0].shape[1]), b2[0].dtype)
    if b_blocked and not tb:
        g, kb, nb = b.shape
        assert kb == kd
        n = g * nb
        tn = _tile(nb, tn, LANES)
        tk = _tile(kd, tk, LANES)
        per_n = nb // tn
        b_spec = pl.BlockSpec((None, tk, tn), lambda i, j, k, *s: (j // per_n, k, j % per_n))
    elif b_blocked and tb:
        g, n, kb = b.shape
        assert g * kb == kd
        tn = _tile(n, tn, LANES)
        tk = _tile(kb, tk, LANES)
        per_k = kb // tk
        b_spec = pl.BlockSpec((None, tn, tk), lambda i, j, k, *s: (k // per_k, j, k % per_k))
    elif tb:
        n, kb = b.shape
        assert kb == kd
        tn = _tile(n, tn, LANES)
        tk = _tile(kd, tk, LANES)
        b_spec = pl.BlockSpec((tn, tk), lambda i, j, k, *s: (j, k))
    else:
        kb, n = b.shape
        assert kb == kd
        tn = _tile(n // out_blocked if out_blocked else n, tn, LANES)
        per_o = (n // out_blocked) // tn if out_blocked else None
        tk = _tile(kd, tk, LANES)
        b_spec = pl.BlockSpec((tk, tn), lambda i, j, k, *s: (k, j))
    m_run = m // 2 if half else m
    tm = _tile(m_run, tm, LANES if ta else 8)

    def row(i, s):
        if not half:
            return i
        h = 1 - s[0][0] if half[1] else s[0][0]
        return h * (m_run // tm) + i

    nk = kd // tk
    nkh, njh = nk // 2, (n // tn) // 2
    if a2:
        assert nk % 2 == 0
        a_specs = [pl.BlockSpec((tm, tk), lambda i, j, k, *s: (row(i, s), jnp.minimum(k, nkh - 1))),
                   pl.BlockSpec((tm, tk), lambda i, j, k, *s: (row(i, s), jnp.maximum(k - nkh, 0)))]
    elif ta:
        a_specs = [pl.BlockSpec((tk, tm), lambda i, j, k, *s: (k, row(i, s)))]
    else:
        a_specs = [pl.BlockSpec((tm, tk), lambda i, j, k, *s: (row(i, s), k))]
    if b2:
        assert (n // tn) % 2 == 0
        b_specs = [pl.BlockSpec((tk, tn), lambda i, j, k, *s: (k, jnp.minimum(j, njh - 1))),
                   pl.BlockSpec((tk, tn), lambda i, j, k, *s: (k, jnp.maximum(j - njh, 0)))]
    else:
        b_specs = [b_spec]
    if out_blocked:
        assert not b_blocked and not tb
        o_spec = pl.BlockSpec((None, tm, tn), lambda i, j, k, *s: (j // per_o, row(i, s), j % per_o))
        o_shape = SDS((out_blocked, m, n // out_blocked), out_dtype)
    else:
        o_spec = pl.BlockSpec((tm, tn), lambda i, j, k, *s: (row(i, s), j))
        o_shape = SDS((m, n), out_dtype)
    dn = (((0 if ta else 1,), (1 if tb else 0,)), ((), ()))
    grid = (m_run // tm, n // tn, nk)

    pieces = [(lo, min(MM_PIECE, tn - lo)) for lo in range(0, tn, MM_PIECE)]

    def compute(*refs):
        a_refs, b_refs = refs[:len(a_specs)], refs[len(a_specs):len(a_specs) + len(b_specs)]
        rest = refs[len(a_specs) + len(b_specs):]
        add_ref = rest[0] if add else None
        o_ref, acc_ref = rest[-2:]
        j, k = pl.program_id(1), pl.program_id(2)

        def result(acc, cols):
            if add:
                acc = acc + add[1] * add_ref[:, cols]
            return acc.astype(o_ref.dtype)

        if nk > 1:
            @pl.when(k == 0)
            def _():
                acc_ref[...] = jnp.zeros_like(acc_ref)

        def run(a_ref, b_ref):
            a_tile = a_ref[...].astype(BF16)
            for lo, wd in pieces:
                cols = slice(lo, lo + wd)
                b_tile = b_ref[cols, :] if tb else b_ref[:, cols]
                part = lax.dot_general(a_tile, b_tile.astype(BF16), dn, preferred_element_type=F32)
                if nk == 1:
                    o_ref[:, cols] = result(part, cols)
                else:
                    acc_ref[:, cols] += part

        if a2:
            pl.when(k < nkh)(lambda: run(a_refs[0], b_refs[0]))
            pl.when(k >= nkh)(lambda: run(a_refs[1], b_refs[0]))
        elif b2:
            pl.when(j < njh)(lambda: run(a_refs[0], b_refs[0]))
            pl.when(j >= njh)(lambda: run(a_refs[0], b_refs[1]))
        else:
            run(a_refs[0], b_refs[0])

        if nk > 1:
            @pl.when(k == nk - 1)
            def _():
                o_ref[...] = result(acc_ref[...], slice(None))

    operands = [*(a2 or (a,)), *(b2 or (b,))] + ([add[0]] if add else [])
    return _hosted_call(compute, comm, name=name, grid=grid, in_specs=a_specs + b_specs + ([o_spec] if add else []),
                        out_specs=[o_spec], out_shape=[o_shape], scratch_shapes=[pltpu.VMEM((tm, tn), F32)], operands=operands,
                        parallel=2, prefetch=half[0] if half else None)


def _swiglu_fwd(z, name, comm=()):
    t, n = z.shape
    n2 = n // 2
    tr = _tile(t, 128, 16)

    def body(a_ref, u_ref, o_ref):
        a = a_ref[...].astype(F32)
        o_ref[...] = (a * _sigmoid(a) * u_ref[...].astype(F32)).astype(o_ref.dtype)

    return _hosted_call(
        body, comm, name=name, grid=(t // tr,),
        in_specs=[pl.BlockSpec((tr, n2), lambda i: (i, 0)), pl.BlockSpec((tr, n2), lambda i: (i, 1))],
        out_specs=[pl.BlockSpec((tr, n2), lambda i: (i, 0))], out_shape=[SDS((t, n2), BF16)], scratch_shapes=[],
        operands=(z, z), parallel=1)


def _swiglu_bwd(dy, w_out, z, name, comm=()):
    t, d = dy.shape
    n2 = w_out.shape[0]
    tm = _tile(t, 512, 16)
    tn = _tile(n2, 1408, LANES)
    nj = n2 // tn
    pieces = [(lo, min(MM_PIECE, tn - lo)) for lo in range(0, tn, MM_PIECE)]

    def body(dy_ref, w_ref, a_ref, u_ref, da_ref, du_ref):
        dy_tile = dy_ref[...]
        for lo, wd in pieces:
            cols = slice(lo, lo + wd)
            dh = lax.dot_general(dy_tile, w_ref[cols, :], (((1,), (1,)), ((), ())), preferred_element_type=F32)
            a = a_ref[:, cols].astype(F32)
            s = _sigmoid(a)
            da_ref[:, cols] = (dh * u_ref[:, cols].astype(F32) * (s * (1.0 + a * (1.0 - s)))).astype(BF16)
            du_ref[:, cols] = (dh * a * s).astype(BF16)

    tile = pl.BlockSpec((tm, tn), lambda i, j: (i, j))
    return _hosted_call(
        body, comm, name=name, grid=(t // tm, nj),
        in_specs=[pl.BlockSpec((tm, d), lambda i, j: (i, 0)), pl.BlockSpec((tn, d), lambda i, j: (j, 0)), tile,
                  pl.BlockSpec((tm, tn), lambda i, j: (i, nj + j))],
        out_specs=[tile, tile], out_shape=[SDS((t, n2), BF16)] * 2, scratch_shapes=[], operands=(dy, w_out, z, z), parallel=2)


def _ln_stats(r):
    mu = jnp.mean(r, axis=-1, keepdims=True)
    xc = r - mu
    var = jnp.mean(xc * xc, axis=-1, keepdims=True)
    return xc * lax.rsqrt(var + LN_EPS)


def _ln_fwd(xp, y, g, b, gp, bp, scale, name, comm=()):
    t, d = xp.shape
    tr = _tile(t, 256, 16)

    def body(xp_ref, y_ref, g_ref, b_ref, *rest):
        r_ref, xb_ref = rest[-2:]
        x_prev = xp_ref[...]
        if gp is not None:
            x_prev = _ln_stats(x_prev) * rest[0][...] + rest[1][...]
        r = ALPHA * x_prev + scale * y_ref[...]
        r_ref[...] = r
        xb_ref[...] = (_ln_stats(r) * g_ref[...] + b_ref[...]).astype(BF16)

    row = pl.BlockSpec((tr, d), lambda i: (i, 0))
    vec = pl.BlockSpec((1, d), lambda i: (0, 0))
    prev = [] if gp is None else [gp, bp]
    return _hosted_call(
        body, comm, name=name, grid=(t // tr,), in_specs=[row, row, vec, vec] + [vec] * len(prev), out_specs=[row, row],
        out_shape=[SDS((t, d), F32), SDS((t, d), BF16)], scratch_shapes=[], operands=(xp, y, g, b, *prev), parallel=1)


def _ln_bwd(dra, dxm, r, g, scale, name):
    t, d = r.shape
    tr = _tile(t, 256, 16)

    def body(dra_ref, dxm_ref, r_ref, g_ref, dr_ref, dyb_ref, dg_ref, db_ref):
        i = pl.program_id(0)
        dx = ALPHA * dra_ref[...] + dxm_ref[...]
        rr = r_ref[...]
        mu = jnp.mean(rr, axis=-1, keepdims=True)
        xc = rr - mu
        rstd = lax.rsqrt(jnp.mean(xc * xc, axis=-1, keepdims=True) + LN_EPS)
        xh = xc * rstd
        dxh = dx * g_ref[...]
        dr = rstd * (dxh - jnp.mean(dxh, axis=-1, keepdims=True) - xh * jnp.mean(dxh * xh, axis=-1, keepdims=True))
        dr_ref[...] = dr
        dyb_ref[...] = (scale * dr).astype(BF16)
        dg = jnp.sum(dx * xh, axis=0, keepdims=True)
        db = jnp.sum(dx, axis=0, keepdims=True)

        @pl.when(i == 0)
        def _():
            dg_ref[...] = dg
            db_ref[...] = db

        @pl.when(i > 0)
        def _():
            dg_ref[...] += dg
            db_ref[...] += db

    row = pl.BlockSpec((tr, d), lambda i: (i, 0))
    vec = pl.BlockSpec((1, d), lambda i: (0, 0))
    return pl.pallas_call(
        body, name=name, grid=(t // tr,), in_specs=[row, row, row, vec], out_specs=[row, row, vec, vec],
        out_shape=[SDS((t, d), F32), SDS((t, d), BF16), SDS((1, d), F32), SDS((1, d), F32)],
        compiler_params=_cparams("arbitrary"),
    )(dra, dxm, r, g)


def _tail(r3, g3, b3, gp, pp, g, b, target, name):
    t, d = r3.shape
    tr = _tile(t, 256, 16)

    def body(r3_ref, g3_ref, b3_ref, gp_ref, pp_ref, g_ref, b_ref, tg_ref, dr_ref, dgp_ref, dpp_ref, dg_ref, db_ref, sq_ref):
        i = pl.program_id(0)
        gate = _sigmoid(gp_ref[...])
        pp_ = pp_ref[...]
        r = ALPHA * (_ln_stats(r3_ref[...]) * g3_ref[...] + b3_ref[...]) + gate * pp_
        mu = jnp.mean(r, axis=-1, keepdims=True)
        xc = r - mu
        rstd = lax.rsqrt(jnp.mean(xc * xc, axis=-1, keepdims=True) + LN_EPS)
        xh = xc * rstd
        err = xh * g_ref[...] + b_ref[...] - tg_ref[...]
        dx = err * (1.0 / d)
        dxh = dx * g_ref[...]
        dr = rstd * (dxh - jnp.mean(dxh, axis=-1, keepdims=True) - xh * jnp.mean(dxh * xh, axis=-1, keepdims=True))
        dr_ref[...] = dr
        dgp_ref[...] = (dr * pp_ * gate * (1.0 - gate)).astype(BF16)
        dpp_ref[...] = (dr * gate).astype(BF16)
        dg = jnp.sum(dx * xh, axis=0, keepdims=True)
        db = jnp.sum(dx, axis=0, keepdims=True)
        sq = jnp.sum(err * err, axis=0, keepdims=True)

        @pl.when(i == 0)
        def _():
            dg_ref[...] = dg
            db_ref[...] = db
            sq_ref[...] = sq

        @pl.when(i > 0)
        def _():
            dg_ref[...] += dg
            db_ref[...] += db
            sq_ref[...] += sq

    row = pl.BlockSpec((tr, d), lambda i: (i, 0))
    vec = pl.BlockSpec((1, d), lambda i: (0, 0))
    return pl.pallas_call(
        body, name=name, grid=(t // tr,), in_specs=[row, vec, vec, row, row, vec, vec, row],
        out_specs=[row, row, row, vec, vec, vec],
        out_shape=[SDS((t, d), F32), SDS((t, d), BF16), SDS((t, d), BF16), SDS((1, d), F32), SDS((1, d), F32),
                   SDS((1, d), F32)],
        compiler_params=_cparams("arbitrary"),
    )(r3, g3, b3, gp, pp, g, b, target)


def _to_bf16(x, name):
    t, d = x.shape
    tr = _tile(t, 512, 16)
    row = pl.BlockSpec((tr, d), lambda i: (i, 0))

    def body(x_ref, o_ref):
        o_ref[...] = x_ref[...].astype(BF16)

    return pl.pallas_call(body, name=name, grid=(t // tr,), in_specs=[row], out_specs=row, out_shape=SDS((t, d), BF16),
                          compiler_params=_cparams("parallel"))(x)


def _concat_cols(parts, name):
    t = parts[0].shape[0]
    widths = [p_.shape[1] for p_ in parts]
    tr = _tile(t, 256, 16)

    def body(*refs):
        o_ref = refs[-1]
        at = 0
        for ref, wd in zip(refs[:-1], widths):
            o_ref[:, at:at + wd] = ref[...]
            at += wd

    return pl.pallas_call(
        body, name=name, grid=(t // tr,), in_specs=[pl.BlockSpec((tr, wd), lambda i: (i, 0)) for wd in widths],
        out_specs=pl.BlockSpec((tr, sum(widths)), lambda i: (i, 0)), out_shape=SDS((t, sum(widths)), parts[0].dtype),
        compiler_params=_cparams("parallel"),
    )(*parts)


def _merge_fwd(z, ma, mb, w, name):
    t = z.shape[0]
    tr = _tile(t, 256, 16)

    def body(gc_ref, gh_ref, ma_ref, mb_ref, o_ref):
        o_ref[...] = (_sigmoid(gc_ref[...]) * ma_ref[...] + _sigmoid(gh_ref[...]) * mb_ref[...]).astype(BF16)

    half = pl.BlockSpec((tr, w), lambda i, j: (i, j))
    return pl.pallas_call(
        body, name=name, grid=(t // tr, 2),
        in_specs=[pl.BlockSpec((tr, w), lambda i, j: (i, 7 + j)), pl.BlockSpec((tr, w), lambda i, j: (i, 9 + j)), half, half],
        out_specs=half, out_shape=SDS((t, 2 * w), BF16), compiler_params=_cparams("parallel", "parallel"),
    )(z, z, ma, mb)


def _merge_bwd(dmer, z, ma, mb, w, name):
    t = z.shape[0]
    tr = _tile(t, 256, 16)

    def body(d_ref, gc_ref, gh_ref, ma_ref, mb_ref, dma_ref, dmb_ref, dgc_ref, dgh_ref):
        dm = d_ref[...]
        sc = _sigmoid(gc_ref[...])
        sh = _sigmoid(gh_ref[...])
        dma_ref[...] = (dm * sc).astype(BF16)
        dmb_ref[...] = (dm * sh).astype(BF16)
        dgc_ref[...] = (dm * ma_ref[...] * sc * (1.0 - sc)).astype(BF16)
        dgh_ref[...] = (dm * mb_ref[...] * sh * (1.0 - sh)).astype(BF16)

    half = pl.BlockSpec((tr, w), lambda i, j: (i, j))
    return pl.pallas_call(
        body, name=name, grid=(t // tr, 2),
        in_specs=[half, pl.BlockSpec((tr, w), lambda i, j: (i, 7 + j)), pl.BlockSpec((tr, w), lambda i, j: (i, 9 + j)), half, half],
        out_specs=[half] * 4, out_shape=[SDS((t, 2 * w), BF16)] * 4, compiler_params=_cparams("parallel", "parallel"),
    )(dmer, z, z, ma, mb)


def _shift_down(x, s, row):
    return jnp.where(row >= s, pltpu.roll(x, s, axis=0), 0.0)


def _shift_up(x, s, row, t):
    return jnp.where(row < t - s, pltpu.roll(x, t - s, axis=0), 0.0)


def _conv_fwd(z, cw, w, name):
    t = z.shape[0]
    tc = LANES
    nb = w // tc

    def body(b_ref, c_ref, h_ref, w_ref, o_ref):
        u = c_ref[...] * h_ref[...]
        row = lax.broadcasted_iota(jnp.int32, u.shape, 0)
        cw_ = w_ref[...]
        conv = cw_[2:3, :] * u + cw_[1:2, :] * _shift_down(u, 1, row) + cw_[0:1, :] * _shift_down(u, 2, row)
        o_ref[...] = (b_ref[...] * conv).astype(BF16)

    col = lambda off: pl.BlockSpec((t, tc), lambda j: (0, off * nb + j))
    return pl.pallas_call(
        body, name=name, grid=(nb,), in_specs=[col(0), col(1), col(2), pl.BlockSpec((3, tc), lambda j: (0, j))],
        out_specs=pl.BlockSpec((t, tc), lambda j: (0, j)), out_shape=SDS((t, w), BF16), compiler_params=_cparams("parallel"),
    )(z, z, z, cw)


def _conv_bwd(dy, z, cw, w, name):
    t = z.shape[0]
    tc = LANES
    nb = w // tc

    def body(dy_ref, b_ref, c_ref, h_ref, w_ref, db_ref, dc_ref, dh_ref, dw_ref):
        c_, h_ = c_ref[...], h_ref[...]
        u = c_ * h_
        row = lax.broadcasted_iota(jnp.int32, u.shape, 0)
        cw_ = w_ref[...]
        u1 = _shift_down(u, 1, row)
        u2 = _shift_down(u, 2, row)
        dy_ = dy_ref[...]
        db_ref[...] = (dy_ * (cw_[2:3, :] * u + cw_[1:2, :] * u1 + cw_[0:1, :] * u2)).astype(BF16)
        dconv = dy_ * b_ref[...]
        du = cw_[2:3, :] * dconv + cw_[1:2, :] * _shift_up(dconv, 1, row, t) + cw_[0:1, :] * _shift_up(dconv, 2, row, t)
        dc_ref[...] = (du * h_).astype(BF16)
        dh_ref[...] = (du * c_).astype(BF16)
        dw_ref[0:1, :] = jnp.sum(dconv * u2, axis=0, keepdims=True)
        dw_ref[1:2, :] = jnp.sum(dconv * u1, axis=0, keepdims=True)
        dw_ref[2:3, :] = jnp.sum(dconv * u, axis=0, keepdims=True)

    col = lambda off: pl.BlockSpec((t, tc), lambda j: (0, off * nb + j))
    own = pl.BlockSpec((t, tc), lambda j: (0, j))
    wsp = pl.BlockSpec((3, tc), lambda j: (0, j))
    return pl.pallas_call(
        body, name=name, grid=(nb,), in_specs=[own, col(0), col(1), col(2), wsp], out_specs=[own, own, own, wsp],
        out_shape=[SDS((t, w), BF16)] * 3 + [SDS((3, w), F32)], compiler_params=_cparams("parallel"),
    )(dy, z, z, z, cw)


def _lower_bound(hg):
    mx = jnp.max(hg, axis=0, keepdims=True)
    e = jnp.exp(hg - mx)
    inv = 1.0 / jnp.sum(e, axis=0, keepdims=True)
    return e[0:1, :] * inv, e[1:2, :] * inv


def _chunk_cumsum(x, row):
    s = 1
    while s < CHUNK:
        x = x + jnp.where(row % CHUNK >= s, pltpu.roll(x, s, axis=0), 0.0)
        s *= 2
    return x


def _dot_nt(a, b):
    return lax.dot_general(a.astype(BF16), b.astype(BF16), (((1,), (1,)), ((), ())), preferred_element_type=F32)


def _dot_tn(a, b):
    return lax.dot_general(a.astype(BF16), b.astype(BF16), (((0,), (0,)), ((), ())), preferred_element_type=F32)


def _dot_nn(a, b):
    return jnp.dot(a.astype(BF16), b.astype(BF16), preferred_element_type=F32)


def _tril(x):
    r = lax.broadcasted_iota(jnp.int32, x.shape, 0)
    c = lax.broadcasted_iota(jnp.int32, x.shape, 1)
    return jnp.where(r >= c, x, 0.0)


HGRN_GROUP = 4
HGRN_ROWS = 512
HGRN_UNROLL = 2


def _unrolled_loop(n, step, init):
    assert n % HGRN_UNROLL == 0

    def trip(i, carry):
        for u in range(HGRN_UNROLL):
            carry = step(i * HGRN_UNROLL + u, carry)
        return carry

    return lax.fori_loop(0, n // HGRN_UNROLL, trip, init)


def _hgrn_chunk_inputs(q_ref, f_ref, cum_ref, lb, rows, ln):
    qr = q_ref[rows, ln]
    q = qr * _sigmoid(qr)
    f = lb + (1.0 - lb) * _sigmoid(f_ref[rows, ln])
    return q, 1.0 - f, cum_ref[rows, ln]


def _hgrn_fwd(z, hg, nw, w, name, comm=()):
    t = z.shape[0]
    nh = w // HEAD
    gh = _tile(nh, HGRN_GROUP, 1)
    gw = gh * HEAD
    ngrp = nh // gh
    tb = _tile(t, HGRN_ROWS, CHUNK)
    ncb = tb // CHUNK

    def body(q_ref, f_ref, i_ref, g_ref, hg_ref, nw_ref, y_ref, o_ref, st_ref, cum_ref, *s_refs):
        lb_all, _ = _lower_bound(hg_ref[...])
        row = lax.broadcasted_iota(jnp.int32, (tb, gw), 0)
        cum_ref[...] = _chunk_cumsum(jnp.log(lb_all + (1.0 - lb_all) * _sigmoid(f_ref[...])), row)

        @pl.when(pl.program_id(1) == 0)
        def _():
            for s_ref in s_refs:
                s_ref[...] = jnp.zeros_like(s_ref)

        def step(c, carry):
            rows = pl.ds(pl.multiple_of(c * CHUNK, CHUNK), CHUNK)
            for g in range(gh):
                ln = slice(g * HEAD, (g + 1) * HEAD)
                lb = lb_all[:, ln]
                q, k, cum = _hgrn_chunk_inputs(q_ref, f_ref, cum_ref, lb, rows, ln)
                v = i_ref[rows, ln]
                last = cum[CHUNK - 1:CHUNK, :]
                qe = q * jnp.exp(cum)
                st = s_refs[g][...]
                st_ref[g, c] = st.astype(BF16)
                o_ref[rows, ln] = _dot_nt(qe, st) + _dot_nn(_tril(_dot_nt(qe, k * jnp.exp(-cum))), v)
                s_refs[g][...] = st * jnp.exp(last) + _dot_tn(v, k * jnp.exp(last - cum))
            return carry

        _unrolled_loop(ncb, step, 0)
        for g in range(gh):
            ln = slice(g * HEAD, (g + 1) * HEAD)
            o = o_ref[:, ln]
            n = o * lax.rsqrt(jnp.mean(o * o, axis=-1, keepdims=True) + RMS_EPS)
            gr = g_ref[:, ln]
            y_ref[:, ln] = (n * nw_ref[...] * gr * _sigmoid(gr)).astype(BF16)

    col = lambda off: pl.BlockSpec((tb, gw), lambda h, j: (j, off * ngrp + h))
    own = pl.BlockSpec((tb, gw), lambda h, j: (j, h))
    return _hosted_call(
        body, comm, name=name, grid=(ngrp, t // tb),
        in_specs=[col(3), col(4), col(5), col(6), pl.BlockSpec((2, gw), lambda h, j: (0, h)),
                  pl.BlockSpec((1, HEAD), lambda h, j: (0, 0))],
        out_specs=[own, own, pl.BlockSpec((gh, ncb, HEAD, HEAD), lambda h, j: (h, j, 0, 0))],
        out_shape=[SDS((t, w), BF16), SDS((t, w), F32), SDS((nh, t // CHUNK, HEAD, HEAD), BF16)],
        scratch_shapes=[pltpu.VMEM((tb, gw), F32)] + [pltpu.VMEM((HEAD, HEAD), F32)] * gh,
        operands=(z, z, z, z, hg, nw), parallel=1)


def _hgrn_bwd(dy, z, o, states, hg, nw, w, name, comm=()):
    t = z.shape[0]
    nh = w // HEAD
    gh = _tile(nh, HGRN_GROUP, 1)
    gw = gh * HEAD
    ngrp = nh // gh
    tb = _tile(t, HGRN_ROWS, CHUNK)
    ncb = tb // CHUNK
    nt = t // tb

    def body(dy_ref, q_ref, f_ref, i_ref, g_ref, o_ref, st_ref, hg_ref, nw_ref,
             dq_ref, df_ref, di_ref, dg_ref, dhg_ref, dnw_ref, cum_ref, do_ref, *ds_refs):
        lb_all, s1_all = _lower_bound(hg_ref[...])
        row = lax.broadcasted_iota(jnp.int32, (tb, gw), 0)
        crow = lax.broadcasted_iota(jnp.int32, (CHUNK, HEAD), 0)
        cum_ref[...] = _chunk_cumsum(jnp.log(lb_all + (1.0 - lb_all) * _sigmoid(f_ref[...])), row)

        @pl.when(pl.program_id(1) == 0)
        def _():
            for ds_ref in ds_refs:
                ds_ref[...] = jnp.zeros_like(ds_ref)
            dhg_ref[...] = jnp.zeros_like(dhg_ref)
            dnw_ref[...] = jnp.zeros_like(dnw_ref)

        for g in range(gh):
            ln = slice(g * HEAD, (g + 1) * HEAD)
            o_ = o_ref[:, ln]
            rstd = lax.rsqrt(jnp.mean(o_ * o_, axis=-1, keepdims=True) + RMS_EPS)
            n = o_ * rstd
            gr = g_ref[:, ln]
            sg = _sigmoid(gr)
            dy_ = dy_ref[:, ln]
            dg_ref[:, ln] = (dy_ * n * nw_ref[...] * (sg * (1.0 + gr * (1.0 - sg)))).astype(BF16)
            dsil = dy_ * gr * sg
            dnw_ref[:, ln] += jnp.sum(dsil * n, axis=0, keepdims=True)
            dn = dsil * nw_ref[...]
            do_ref[:, ln] = rstd * (dn - n * jnp.mean(dn * n, axis=-1, keepdims=True))

        def step(cc, dlbs):
            c = ncb - 1 - cc
            rows = pl.ds(pl.multiple_of(c * CHUNK, CHUNK), CHUNK)
            new = []
            for g in range(gh):
                ln = slice(g * HEAD, (g + 1) * HEAD)
                lb = lb_all[:, ln]
                qr = q_ref[rows, ln]
                sq = _sigmoid(qr)
                q = qr * sq
                sf = _sigmoid(f_ref[rows, ln])
                f = lb + (1.0 - lb) * sf
                k = 1.0 - f
                cum = cum_ref[rows, ln]
                v = i_ref[rows, ln]
                do = do_ref[rows, ln]
                last = cum[CHUNK - 1:CHUNK, :]
                eg = jnp.exp(cum)
                eng = jnp.exp(-cum)
                elc = jnp.exp(last - cum)
                qe, ke, kl = q * eg, k * eng, k * elc
                ds = ds_refs[g][...]
                a = _tril(_dot_nt(qe, ke))
                da = _tril(_dot_nt(do, v))
                di_ref[rows, ln] = (_dot_tn(a, do) + _dot_nt(kl, ds)).astype(BF16)
                st = st_ref[g, c]
                dkl = _dot_nn(v, ds)
                dq = (_dot_nn(do, st) + _dot_nn(da, ke)) * eg
                dk = _dot_tn(da, qe) * eng + dkl * elc
                el = jnp.exp(last)
                ds_refs[g][...] = ds * el + _dot_tn(do, qe)
                dlast = jnp.sum(kl * dkl, axis=0, keepdims=True) + el * jnp.sum(ds * st.astype(F32), axis=0, keepdims=True)
                x = q * dq - k * dk + jnp.where(crow == CHUNK - 1, dlast, 0.0)
                s = 1
                while s < CHUNK:
                    x = x + _shift_up(x, s, crow, CHUNK)
                    s *= 2
                df = x / f - dk
                dq_ref[rows, ln] = (dq * (sq * (1.0 + qr * (1.0 - sq)))).astype(BF16)
                df_ref[rows, ln] = (df * (1.0 - lb) * sf * (1.0 - sf)).astype(BF16)
                new.append(dlbs[g] + jnp.sum(df * (1.0 - sf), axis=0, keepdims=True))
            return tuple(new)

        dlbs = _unrolled_loop(ncb, step, tuple(jnp.zeros((1, HEAD), F32) for _ in range(gh)))
        for g in range(gh):
            ln = slice(g * HEAD, (g + 1) * HEAD)
            dlb = dlbs[g] * lb_all[:, ln] * s1_all[:, ln]
            dhg_ref[0:1, ln] += dlb
            dhg_ref[1:2, ln] -= dlb

    col = lambda off: pl.BlockSpec((tb, gw), lambda h, j: (nt - 1 - j, off * ngrp + h))
    own = pl.BlockSpec((tb, gw), lambda h, j: (nt - 1 - j, h))
    hsp = pl.BlockSpec((2, gw), lambda h, j: (0, h))
    return _hosted_call(
        body, comm, name=name, grid=(ngrp, nt),
        in_specs=[own, col(3), col(4), col(5), col(6), own,
                  pl.BlockSpec((gh, ncb, HEAD, HEAD), lambda h, j: (h, nt - 1 - j, 0, 0)),
                  hsp, pl.BlockSpec((1, HEAD), lambda h, j: (0, 0))],
        out_specs=[own, own, own, own, hsp, pl.BlockSpec((1, gw), lambda h, j: (0, h))],
        out_shape=[SDS((t, w), BF16)] * 4 + [SDS((2, w), F32), SDS((1, w), F32)],
        scratch_shapes=[pltpu.VMEM((tb, gw), F32)] * 2 + [pltpu.VMEM((HEAD, HEAD), F32)] * gh,
        operands=(dy, z, z, z, z, o, states, hg, nw), parallel=1)


def _cast_pad(wt, n_pad, meta, sp, name, comm=()):
    _, r, n = wt.shape
    g, p, per = meta
    tr = _tile(r, max(16, (3 << 19) // n_pad // 16 * 16), 16)

    def body(w_ref, o_ref):
        if n_pad != n:
            o_ref[...] = jnp.zeros(o_ref.shape, o_ref.dtype)
        o_ref[:, 0:n] = w_ref[...].astype(BF16)

    return _hosted_call(
        body, comm, name=name, grid=(r // tr,), in_specs=[pl.BlockSpec((None, tr, n), lambda i, sp: (0, i, 0))],
        out_specs=[pl.BlockSpec((None, tr, n_pad), lambda i, sp: (sp[1] // per, ((sp[1] % per) * r) // tr + i, 0))],
        out_shape=[SDS((g, p, n_pad), BF16)], scratch_shapes=[], operands=(wt,), parallel=1, prefetch=sp)


def _cast_pad_t(wt_t, n_pad, meta, sp, name, comm=()):
    _, n, r = wt_t.shape
    g, p, per = meta
    tc = _tile(r, 256, LANES)

    def body(w_ref, o_ref):
        for lo in range(0, n_pad, LANES):
            rows = min(LANES, n - lo)
            piece = w_ref[lo:lo + rows, :]
            if rows < LANES:
                piece = jnp.concatenate([piece, jnp.zeros((LANES - rows, tc), F32)], axis=0)
            o_ref[:, lo:lo + LANES] = piece.T.astype(BF16)

    return _hosted_call(
        body, comm, name=name, grid=(r // tc,), in_specs=[pl.BlockSpec((None, n, tc), lambda i, sp: (0, 0, i))],
        out_specs=[pl.BlockSpec((None, tc, n_pad), lambda i, sp: (sp[1] // per, ((sp[1] % per) * r) // tc + i, 0))],
        out_shape=[SDS((g, p, n_pad), BF16)], scratch_shapes=[], operands=(wt_t,), parallel=1, prefetch=sp)


def _adam_math(w, g, m, v):
    m2 = ADAM_B1 * m + (1.0 - ADAM_B1) * g
    v2 = ADAM_B2 * v + (1.0 - ADAM_B2) * (g * g)
    c1 = 1.0 / (1.0 - ADAM_B1 ** ADAM_STEP)
    c2 = 1.0 / (1.0 - ADAM_B2 ** ADAM_STEP)
    return -ADAM_LR * ((m2 * c1) / (jnp.sqrt(v2 * c2) + ADAM_EPS) + ADAM_WD * w), m2, v2


def _adamw_t(wt_t, g, m_t, v_t, name):
    _, n, r = wt_t.shape
    ng = g.shape[1]
    tc = LANES

    def body(w_ref, g_ref, m_ref, v_ref, go_ref, d_ref, mo_ref, vo_ref, gt_ref):
        for lo in range(0, ng, LANES):
            gt_ref[lo:lo + LANES, :] = g_ref[:, lo:lo + LANES].T
        g_ = gt_ref[0:n, :]
        delta, m2, v2 = _adam_math(w_ref[...], g_, m_ref[...], v_ref[...])
        go_ref[...] = g_
        d_ref[...] = delta
        mo_ref[...] = m2
        vo_ref[...] = v2

    blk = pl.BlockSpec((None, n, tc), lambda i: (0, 0, i))
    return pl.pallas_call(
        body, name=name, grid=(r // tc,), in_specs=[blk, pl.BlockSpec((tc, ng), lambda i: (i, 0)), blk, blk],
        out_specs=[blk] * 4, out_shape=[SDS(wt_t.shape, F32)] * 4, scratch_shapes=[pltpu.VMEM((ng, tc), F32)],
        compiler_params=_cparams("parallel"),
    )(wt_t, g, m_t, v_t)


def _adamw(wt, g, m, v, name):
    lead = (None,) * (wt.ndim - 2)
    zero = (0,) * (wt.ndim - 2)
    r, n = wt.shape[-2:]
    ng = g.shape[1]
    nct = 2 if ng == n and n % (2 * LANES) == 0 else 1
    tc, tg = n // nct, ng // nct
    tr = _tile(r, max(8, (3 << 17) // tg // 8 * 8), 8)

    def body(w_ref, g_ref, m_ref, v_ref, go_ref, d_ref, mo_ref, vo_ref):
        g_ = g_ref[:, 0:tc]
        delta, m2, v2 = _adam_math(w_ref[...], g_, m_ref[...], v_ref[...])
        go_ref[...] = g_
        d_ref[...] = delta
        mo_ref[...] = m2
        vo_ref[...] = v2

    blk = pl.BlockSpec(lead + (tr, tc), lambda i, j: zero + (i, j))
    return pl.pallas_call(
        body, name=name, grid=(r // tr, nct), in_specs=[blk, pl.BlockSpec((tr, tg), lambda i, j: (i, j)), blk, blk],
        out_specs=[blk] * 4, out_shape=[SDS(wt.shape, F32)] * 4, compiler_params=_cparams("parallel", "parallel"),
    )(wt, g, m, v)


def _place():
    x, y, c = lax.axis_index("x"), lax.axis_index("y"), lax.axis_index("c")
    return x, y, c, 2 * x + y


def _chip_dev(k, c):
    return (k // 2, k % 2, c)


def _half(ref, j, h, rows, per):
    return ref.at[j // per, pl.ds((j % per) * rows + h * (rows // 2), rows // 2)]


def _gather_stage(bufs, metas, rows_of, ici_parts, fwd_parts, zero_pad):
    nw = len(bufs)
    ici_on = [i for i in range(nw) if ici_parts[i] is not None]
    fwd_on = [i for i in range(nw) if fwd_parts[i] is not None]
    pad_jobs = [(i, gi) for i in ici_on if ici_parts[i][0] == 0 and metas[i][1] > metas[i][2] * rows_of[i]
                for gi in range(metas[i][0])]

    def part_of(ref, i, j, h, part):
        per = metas[i][2]
        p, np_ = part
        pr = rows_of[i] // 2 // np_
        return ref.at[j // per, pl.ds((j % per) * rows_of[i] + h * (rows_of[i] // 2) + p * pr, pr)]

    def descriptors(ins, outs, sems):
        src, zp, dst = ins[:nw], ins[nw], outs
        pads, send, recv, fsend, frecv = sems
        x, y, c, me = _place()

        def pad(n):
            i, gi = pad_jobs[n]
            extra = metas[i][1] - metas[i][2] * rows_of[i]
            return pltpu.make_async_copy(zp.at[pl.ds(0, extra)], dst[i].at[gi, pl.ds(metas[i][2] * rows_of[i], extra)], pads.at[n])

        def ici(i, r, frm):
            return pltpu.make_async_remote_copy(
                src_ref=part_of(src[i], i, me, c, ici_parts[i]), dst_ref=part_of(dst[i], i, frm, c, ici_parts[i]),
                send_sem=send.at[i, r - 1], recv_sem=recv.at[i, r - 1], device_id=_chip_dev((me + r) % N_CHIPS, c),
                device_id_type=MESH)

        def d2d(i, r, frm, h):
            blk = part_of(dst[i], i, frm, h, fwd_parts[i])
            return pltpu.make_async_remote_copy(src_ref=blk, dst_ref=blk, send_sem=fsend.at[i, r - 1],
                                                recv_sem=frecv.at[i, r - 1], device_id=(x, y, 1 - c), device_id_type=MESH)

        return pad, ici, d2d, c, me

    def start(ins, outs, sems):
        pad, ici, d2d, c, me = descriptors(ins, outs, sems)
        for n in range(len(pad_jobs)):
            pad(n).start()
        for i in fwd_on:
            for r in range(1, N_CHIPS):
                d2d(i, r, (me - r) % N_CHIPS, c).start()
        for i in ici_on:
            for r in range(1, N_CHIPS):
                ici(i, r, me).start()

    def finish(ins, outs, sems):
        pad, ici, d2d, c, me = descriptors(ins, outs, sems)
        for i in fwd_on:
            for r in range(1, N_CHIPS):
                d2d(i, r, (me - r) % N_CHIPS, 1 - c).wait_recv()
                d2d(i, r, (me - r) % N_CHIPS, c).wait_send()
        for i in ici_on:
            for r in range(1, N_CHIPS):
                ici(i, r, (me - r) % N_CHIPS).wait_recv()
                ici(i, r, me).wait_send()
        for n in range(len(pad_jobs)):
            pad(n).wait()

    return _Stage(ins=list(bufs) + [zero_pad], out_shapes=[SDS(b.shape, b.dtype) for b in bufs],
                  aliases={i: i for i in range(nw)},
                  sems=[pltpu.SemaphoreType.DMA((max(len(pad_jobs), 1),))] + [pltpu.SemaphoreType.DMA((nw, N_CHIPS - 1))] * 4,
                  start=start, finish=finish)


def _gather_small(packed, name):
    r, n = packed.shape

    def body(src, dst, send, recv):
        x, y, c, me = _place()
        dst[me] = src[...]
        cps = []
        for d in range(1, N_CHIPS):
            cp = pltpu.make_async_remote_copy(src_ref=src, dst_ref=dst.at[me], send_sem=send.at[d - 1], recv_sem=recv.at[d - 1],
                                              device_id=_chip_dev((me + d) % N_CHIPS, c), device_id_type=MESH)
            cp.start()
            cps.append(cp)
        for d in range(1, N_CHIPS):
            pltpu.make_async_remote_copy(src_ref=src, dst_ref=dst.at[(me - d) % N_CHIPS], send_sem=send.at[d - 1],
                                         recv_sem=recv.at[d - 1], device_id=_chip_dev((me + d) % N_CHIPS, c),
                                         device_id_type=MESH).wait_recv()
        for cp in cps:
            cp.wait_send()

    return pl.pallas_call(
        body, name=name, in_specs=[VMEM_SPEC], out_specs=VMEM_SPEC, out_shape=SDS((N_CHIPS, r, n), F32),
        scratch_shapes=[pltpu.SemaphoreType.DMA((N_CHIPS - 1,))] * 2,
    )(packed)


def _all_reduce_small(packed, name):
    r, n = packed.shape

    def body(src, out, slots, send, recv):
        x, y, c, me = _place()
        idx = 2 * me + c
        slots[idx] = src[...]
        cps = []

        def peer(d):
            p = (idx + d) % N_DEV
            return (p // 4, (p // 2) % 2, p % 2)

        for d in range(1, N_DEV):
            cp = pltpu.make_async_remote_copy(src_ref=src, dst_ref=slots.at[idx], send_sem=send.at[d - 1], recv_sem=recv.at[d - 1],
                                              device_id=peer(d), device_id_type=MESH)
            cp.start()
            cps.append(cp)
        for d in range(1, N_DEV):
            pltpu.make_async_remote_copy(src_ref=src, dst_ref=slots.at[(idx - d) % N_DEV], send_sem=send.at[d - 1],
                                         recv_sem=recv.at[d - 1], device_id=peer(d), device_id_type=MESH).wait_recv()
        for cp in cps:
            cp.wait_send()
        acc = slots[0]
        for k in range(1, N_DEV):
            acc = acc + slots[k]
        out[...] = acc

    return pl.pallas_call(
        body, name=name, in_specs=[VMEM_SPEC], out_specs=VMEM_SPEC, out_shape=SDS((r, n), F32),
        scratch_shapes=[pltpu.VMEM((N_DEV, r, n), F32)] + [pltpu.SemaphoreType.DMA((N_DEV - 1,))] * 2,
    )(packed)


def _simple_stage(ins, out_shapes, aliases, n_copies, copies):
    def start(ins_, outs, sems):
        for cp in copies(ins_, outs, *sems):
            cp.start()

    def finish(ins_, outs, sems):
        for cp in copies(ins_, outs, *sems):
            cp.wait()

    return _Stage(ins=list(ins), out_shapes=list(out_shapes), aliases=aliases,
                  sems=[pltpu.SemaphoreType.DMA((n_copies,))] * 2, start=start, finish=finish)


def _rs_pair_exchange(grads, metas, rows_of):
    nw = len(grads)

    def copies(src, dst, send, recv):
        x, y, c, me = _place()
        return [pltpu.make_async_remote_copy(
            src_ref=_half(src[i], j, 1 - c, rows_of[i], metas[i][2]), dst_ref=dst[i].at[j], send_sem=send.at[i * N_CHIPS + j],
            recv_sem=recv.at[i * N_CHIPS + j], device_id=(x, y, 1 - c), device_id_type=MESH)
            for i in range(nw) for j in range(N_CHIPS)]

    out_shapes = [SDS((N_CHIPS, rows_of[i] // 2, g.shape[2]), g.dtype) for i, g in enumerate(grads)]
    return _simple_stage(grads, out_shapes, {}, nw * N_CHIPS, copies)


def _rs_pair_add(g, got, meta, rows, sp, name):
    per = meta[2]
    n = g.shape[2]
    hr = rows // 2
    tr = _tile(hr, max(16, (3 << 19) // n // 16 * 16), 16)

    def body(sp_ref, g_ref, got_ref, snd_ref, own_ref):
        j = pl.program_id(1)
        s = g_ref[...].astype(F32) + got_ref[...].astype(F32)
        snd_ref[...] = s.astype(BF16)

        @pl.when(j == sp_ref[1])
        def _():
            own_ref[...] = s

    grid_spec = pltpu.PrefetchScalarGridSpec(
        num_scalar_prefetch=1, grid=(hr // tr, N_CHIPS),
        in_specs=[pl.BlockSpec((None, tr, n), lambda i, j, sp: (j // per, ((j % per) * rows + sp[0] * hr) // tr + i, 0)),
                  pl.BlockSpec((None, tr, n), lambda i, j, sp: (j, i, 0))],
        out_specs=[pl.BlockSpec((None, tr, n), lambda i, j, sp: (j, i, 0)), pl.BlockSpec((tr, n), lambda i, j, sp: (i, 0))])
    return pl.pallas_call(
        body, name=name, grid_spec=grid_spec, out_shape=[SDS((N_CHIPS, hr, n), BF16), SDS((hr, n), F32)],
        compiler_params=_cparams("parallel", "arbitrary"),
    )(sp, g, got)


def _rs_chip_exchange(sends, part=(0, 1), prev=None):
    nw = len(sends)
    p, np_ = part

    def copies(src, dst, send, recv):
        x, y, c, me = _place()
        cps = []
        for i in range(nw):
            pr = sends[i].shape[1] // np_
            for r in range(1, N_CHIPS):
                cps.append(pltpu.make_async_remote_copy(
                    src_ref=src[i].at[(me + r) % N_CHIPS, pl.ds(p * pr, pr)], dst_ref=dst[i].at[r - 1, pl.ds(p * pr, pr)],
                    send_sem=send.at[i * (N_CHIPS - 1) + r - 1], recv_sem=recv.at[i * (N_CHIPS - 1) + r - 1],
                    device_id=_chip_dev((me + r) % N_CHIPS, c), device_id_type=MESH))
        return cps

    out_shapes = [SDS((N_CHIPS - 1,) + s.shape[1:], BF16) for s in sends]
    if prev is None:
        return _simple_stage(sends, out_shapes, {}, nw * (N_CHIPS - 1), copies)
    return _simple_stage(list(sends) + list(prev), out_shapes, {nw + i: i for i in range(nw)}, nw * (N_CHIPS - 1), copies)


def _rs_chip_add(own, got, sp, name):
    hr, n = own.shape
    tr = _tile(hr, max(16, (3 << 19) // n // 16 * 16), 16)

    def body(sp_ref, own_ref, got_ref, o_ref):
        acc = own_ref[...]
        for r in range(N_CHIPS - 1):
            acc = acc + got_ref[r].astype(F32)
        o_ref[...] = acc

    grid_spec = pltpu.PrefetchScalarGridSpec(
        num_scalar_prefetch=1, grid=(hr // tr,),
        in_specs=[pl.BlockSpec((tr, n), lambda i, sp: (i, 0)), pl.BlockSpec((N_CHIPS - 1, tr, n), lambda i, sp: (0, i, 0))],
        out_specs=pl.BlockSpec((tr, n), lambda i, sp: (sp[0] * (hr // tr) + i, 0)))
    return pl.pallas_call(body, name=name, grid_spec=grid_spec, out_shape=SDS((2 * hr, n), F32),
                          compiler_params=_cparams("parallel"))(sp, own, got)


def _rs_pair_share(blocks):
    nw = len(blocks)

    def copies(src, dst, send, recv):
        x, y, c, me = _place()
        cps = []
        for i in range(nw):
            hr = src[i].shape[0] // 2
            cps.append(pltpu.make_async_remote_copy(
                src_ref=src[i].at[pl.ds(c * hr, hr)], dst_ref=dst[i].at[pl.ds(c * hr, hr)], send_sem=send.at[i],
                recv_sem=recv.at[i], device_id=(x, y, 1 - c), device_id_type=MESH))
        return cps

    return _simple_stage(blocks, [SDS(b.shape, b.dtype) for b in blocks], {i: i for i in range(nw)}, nw, copies)


def kernel(x, p, ln_g, ln_b, ffn1_w_in, ffn1_w_out, mix_w_in, conv_w, hg_lower_bound, hg_norm_w, branch_w_conv, branch_w_hgrn, mix_w_out, ffn2_w_in, ffn2_w_out, ple_w_gate, ple_w_proj, loss_target, m_ln_g, m_ln_b, m_ffn1_w_in, m_ffn1_w_out, m_mix_w_in, m_conv_w, m_hg_lower_bound, m_hg_norm_w, m_branch_w_conv, m_branch_w_hgrn, m_mix_w_out, m_ffn2_w_in, m_ffn2_w_out, m_ple_w_gate, m_ple_w_proj, v_ln_g, v_ln_b, v_ffn1_w_in, v_ffn1_w_out, v_mix_w_in, v_conv_w, v_hg_lower_bound, v_hg_norm_w, v_branch_w_conv, v_branch_w_hgrn, v_mix_w_out, v_ffn2_w_in, v_ffn2_w_out, v_ple_w_gate, v_ple_w_proj):
    assert ln_g.shape[0] == DEPTH and x.shape[0] == 1 and p.shape[:2] == (1, 1)
    t, d = x.shape[1], x.shape[2]
    w = d // 2
    x0 = x.reshape(t, d)
    x0b = _to_bf16(x0, "x_bf16")
    pe = p.reshape(t, p.shape[-1])
    target = loss_target.reshape(t, d)
    cx, cy, cc = lax.axis_index("x"), lax.axis_index("y"), lax.axis_index("c")
    chip = 2 * cx + cy
    sp = jnp.stack([cc, chip]).astype(jnp.int32)

    big = dict(ffn1_w_in=ffn1_w_in, ffn1_w_out=ffn1_w_out, mix_w_in=mix_w_in, branch_w_conv=branch_w_conv,
               branch_w_hgrn=branch_w_hgrn, mix_w_out=mix_w_out, ffn2_w_in=ffn2_w_in, ffn2_w_out=ffn2_w_out,
               ple_w_gate=ple_w_gate, ple_w_proj=ple_w_proj)
    moments = dict(ffn1_w_in=(m_ffn1_w_in, v_ffn1_w_in), ffn1_w_out=(m_ffn1_w_out, v_ffn1_w_out), mix_w_in=(m_mix_w_in, v_mix_w_in),
                   branch_w_conv=(m_branch_w_conv, v_branch_w_conv), branch_w_hgrn=(m_branch_w_hgrn, v_branch_w_hgrn),
                   mix_w_out=(m_mix_w_out, v_mix_w_out), ffn2_w_in=(m_ffn2_w_in, v_ffn2_w_in), ffn2_w_out=(m_ffn2_w_out, v_ffn2_w_out),
                   ple_w_gate=(m_ple_w_gate, v_ple_w_gate), ple_w_proj=(m_ple_w_proj, v_ple_w_proj))
    names = list(big)

    n_loc = ffn1_w_in.shape[-1]
    n_pad = -(-n_loc // LANES) * LANES
    assert mix_w_in.shape[-1] % LANES == 0 and ffn1_w_out.shape[1] * 2 == n_loc
    pad_cols = dict(ffn1_w_in=n_pad, ffn2_w_in=n_pad)
    meta = {k: (N_CHIPS, big[k].shape[1], 1) for k in names}
    meta["ffn1_w_out"] = meta["ffn2_w_out"] = (2, n_pad, 2)
    rows = {k: big[k].shape[1] for k in names}
    swap = lambda a: jnp.transpose(a, (0, 2, 1))
    wbuf = {}
    zero_pad = jnp.zeros((max(n_pad - n_loc, 16), d), BF16)

    def cast(k, comm=()):
        if k in pad_cols:
            return _cast_pad_t(swap(big[k]), pad_cols[k], meta[k], sp, "cast_" + k, comm=comm)
        return _cast_pad(big[k], big[k].shape[2], meta[k], sp, "cast_" + k, comm=comm)

    def gather(ici=(), fwd=()):
        ks = list(dict.fromkeys([k for k, _, _ in ici] + [k for k, _, _ in fwd]))
        ip = {k: (p_, n_) for k, p_, n_ in ici}
        fp = {k: (p_, n_) for k, p_, n_ in fwd}
        return _gather_stage([wbuf[k] for k in ks], [meta[k] for k in ks], [rows[k] for k in ks], [ip.get(k) for k in ks],
                             [fp.get(k) for k in ks], zero_pad), ks

    def gathered(ks, outs):
        wbuf.update(zip(ks, outs))

    def w3(k):
        return wbuf[k]

    def w2(k):
        return wbuf[k].reshape(-1, wbuf[k].shape[2])

    dq, wq = d // N_CHIPS, w // N_CHIPS
    small = jnp.concatenate([ln_g[0], ln_b[0], jnp.pad(conv_w[0], ((0, 5), (0, dq - wq)))], axis=0)
    small = _gather_small(small, "gather_small")
    lng = small[:, 0:4, :].transpose(1, 0, 2).reshape(4, 1, d)
    lnb = small[:, 4:8, :].transpose(1, 0, 2).reshape(4, 1, d)
    cw = small[:, 8:11, :wq].transpose(1, 0, 2).reshape(3, w)
    hg = hg_lower_bound
    nw_ = hg_norm_w

    one = lambda *ks_: [(k, 0, 1) for k in ks_]
    wbuf["ffn1_w_in"] = cast("ffn1_w_in")
    carriers = ["ple_w_proj", "ffn1_w_out", "mix_w_in", None] + [k for k in names if k not in ("ffn1_w_in", "ple_w_proj", "ffn1_w_out", "mix_w_in")]
    assert len(carriers) > FIRST_GATHER_PARTS
    for step, k in enumerate(carriers):
        ici = [("ffn1_w_in", step, FIRST_GATHER_PARTS)] if step < FIRST_GATHER_PARTS else []
        fwd = [("ffn1_w_in", step - 1, FIRST_GATHER_PARTS)] if 1 <= step <= FIRST_GATHER_PARTS else []
        ici += one("ple_w_proj") if step == 1 else []
        fwd += one("ple_w_proj") if step == 2 else []
        if ici or fwd:
            st, ks = gather(ici, fwd)
            if k is None:
                pp, got = _mm(pe, w3("ple_w_proj"), name="ple_proj", b_blocked=True, tn=512, comm=[st])
            else:
                wbuf[k], got = cast(k, comm=[st])
            gathered(ks, got)
        else:
            wbuf[k] = cast(k)
    st, ks = gather(ici=one("ffn1_w_out") + [("mix_w_in", 0, 2)])
    z1, got = _mm(x0b, w3("ffn1_w_in"), name="ffn1_in", b_blocked=True, out_dtype=BF16, tm=1024, comm=[st])
    gathered(ks, got)
    st, ks = gather(fwd=one("ffn1_w_out") + [("mix_w_in", 0, 2)])
    h1, got = _swiglu_fwd(z1, "ffn1_act", comm=[st])
    gathered(ks, got)
    st, ks = gather(ici=[("mix_w_in", 1, 2)])
    y1, got = _mm(h1, w2("ffn1_w_out"), name="ffn1_out", tm=1024, tn=1024, tk=2816, comm=[st])
    gathered(ks, got)
    st, ks = gather(fwd=[("mix_w_in", 1, 2)])
    (r1, x1b), got = _ln_fwd(x0, y1, lng[0], lnb[0], None, None, 0.5, "ln0", comm=[st])
    gathered(ks, got)
    mixo_w = one("branch_w_conv", "branch_w_hgrn", "mix_w_out")
    st, ks = gather(ici=mixo_w + [("ffn2_w_in", 0, 2)])
    z, got = _mm(x1b, w3("mix_w_in"), name="mix_in", b_blocked=True, tm=1024, comm=[st])
    gathered(ks, got)
    ya = _conv_fwd(z, cw, w, "conv_fwd")
    st, ks = gather(ici=[("ffn2_w_in", 1, 2)], fwd=mixo_w + [("ffn2_w_in", 0, 2)])
    (yb, o_h, states), got = _hgrn_fwd(z, hg, nw_, w, "hgrn_fwd", comm=[st])
    gathered(ks, got)
    ma = _mm(ya, w3("branch_w_conv"), name="branch_conv", b_blocked=True, tn=512)
    mb = _mm(yb, w3("branch_w_hgrn"), name="branch_hgrn", b_blocked=True, tn=512)
    merged = _merge_fwd(z, ma, mb, w, "merge_fwd")
    st, ks = gather(fwd=[("ffn2_w_in", 1, 2)])
    y2, got = _mm(merged, w2("mix_w_out"), name="mix_out", tn=1024, comm=[st])
    gathered(ks, got)
    r2, x2b = _ln_fwd(r1, y2, lng[1], lnb[1], lng[0], lnb[0], 1.0, "ln1")
    late = one("ffn2_w_out", "ple_w_gate")
    st, ks = gather(ici=late)
    z3, got = _mm(x2b, w3("ffn2_w_in"), name="ffn2_in", b_blocked=True, out_dtype=BF16, tm=1024, comm=[st])
    gathered(ks, got)
    st, ks = gather(fwd=late)
    h3, got = _swiglu_fwd(z3, "ffn2_act", comm=[st])
    gathered(ks, got)
    y3 = _mm(h3, w2("ffn2_w_out"), name="ffn2_out", tm=1024, tn=1024, tk=2816)
    r3, x3b = _ln_fwd(r2, y3, lng[2], lnb[2], lng[1], lnb[1], 0.5, "ln2")
    gp = _mm(x3b, w2("ple_w_gate"), name="ple_gate", tn=1024)
    dr4, dgp, dpp, dg3, db3, sq = _tail(r3, lng[2], lnb[2], gp, pp, lng[3], lnb[3], target, "tail")

    grads, sends, owns, blocks, outs = {}, {}, {}, {}, {}

    def pair_exchange(*ks):
        return _rs_pair_exchange([grads[k] for k in ks], [meta[k] for k in ks], [rows[k] for k in ks])

    def pair_add(ks, got):
        for k, g_ in zip(ks, got):
            sends[k], owns[k] = _rs_pair_add(grads[k], g_, meta[k], rows[k], sp, "rs_pair_add_" + k)

    def chip_exchange(*ks):
        return _rs_chip_exchange([sends[k] for k in ks])

    def chip_add(ks, got):
        for k, g_ in zip(ks, got):
            blocks[k] = _rs_chip_add(owns[k], g_, sp, "rs_chip_add_" + k)

    def pair_share(*ks):
        return _rs_pair_share([blocks[k] for k in ks])

    def update(ks, full):
        for k, g_ in zip(ks, full):
            m_, v_ = moments[k]
            if k in pad_cols:
                outs[k] = [swap(a) for a in _adamw_t(swap(big[k]), g_, swap(m_), swap(v_), "adamw_" + k)]
            else:
                outs[k] = _adamw(big[k], g_, m_, v_, "adamw_" + k)

    ple = ("ple_w_gate", "ple_w_proj")
    mixo = ("mix_w_out", "branch_w_conv", "branch_w_hgrn")
    dx3m = _mm(dgp, w2("ple_w_gate"), name="d_ple_gate_x", tb=True, tn=1024, tk=2048)
    grads["ple_w_gate"] = _mm(x3b, dgp, name="d_ple_gate_w", ta=True, out_dtype=BF16, tm=1024, tk=2048, tn=1024).reshape(N_CHIPS, -1, d)
    grads["ple_w_proj"] = _mm(pe, dpp, name="d_ple_proj_w", ta=True, out_dtype=BF16, out_blocked=N_CHIPS, tk=2048, tn=512)
    dr3, dy3b, dg2, db2 = _ln_bwd(dr4, dx3m, r3, lng[2], 0.5, "ln2_bwd")
    late_w = ple + ("ffn2_w_out",)
    dz3 = tuple(_swiglu_bwd(dy3b, w2("ffn2_w_out"), z3, "d_ffn2_out_x"))
    grads["ffn2_w_out"] = _mm(h3, dy3b, name="d_ffn2_out_w", ta=True, out_dtype=BF16, tm=1408, tk=2048, tn=1024).reshape(2, n_pad, d)
    dx2m, got = _mm(dz3, w3("ffn2_w_in"), name="d_ffn2_in_x", tb=True, b_blocked=True, tm=512, tn=1024, tk=2816,
                    comm=[pair_exchange(*late_w)])
    pair_add(late_w, got)
    grads["ffn2_w_in"], got = _mm(x2b, dz3, name="d_ffn2_in_w", ta=True, out_dtype=BF16, out_blocked=N_CHIPS, tk=2048, comm=[chip_exchange(*late_w)])
    chip_add(late_w, got)
    dr2, dy2b, dg1, db1 = _ln_bwd(dr3, dx2m, r2, lng[1], 1.0, "ln1_bwd")
    dmer, got = _mm(dy2b, w2("mix_w_out"), name="d_mix_out_x", tb=True, tn=1024, tk=2048, comm=[pair_exchange("ffn2_w_in")])
    pair_add(["ffn2_w_in"], got)
    g_, full = _mm(merged, dy2b, name="d_mix_out_w", ta=True, out_dtype=BF16, tm=1024, tk=2048, tn=1024, comm=[pair_share(*late_w)])
    grads["mix_w_out"] = g_.reshape(N_CHIPS, -1, d)
    update(late_w, full)
    dma, dmb, dgc, dgh = _merge_bwd(dmer, z, ma, mb, w, "merge_bwd")
    dya = _mm(dma, w3("branch_w_conv"), name="d_branch_conv_x", tb=True, b_blocked=True, tn=1024, tk=512)
    dyb = _mm(dmb, w3("branch_w_hgrn"), name="d_branch_hgrn_x", tb=True, b_blocked=True, tn=1024, tk=512)
    grads["branch_w_conv"] = _mm(ya, dma, name="d_branch_conv_w", ta=True, out_dtype=BF16, out_blocked=N_CHIPS, tm=1024, tk=2048, tn=512)
    grads["branch_w_hgrn"] = _mm(yb, dmb, name="d_branch_hgrn_w", ta=True, out_dtype=BF16, out_blocked=N_CHIPS, tm=1024, tk=2048, tn=512)
    dbg, dcg, dhc, dcw = _conv_bwd(dya, z, cw, w, "conv_bwd")
    (dq_, df_, di_, dgr_, dhg, dnw), got2, got = _hgrn_bwd(dyb, z, o_h, states, hg, nw_, w, "hgrn_bwd",
                                                            comm=[chip_exchange("ffn2_w_in"), pair_exchange(*mixo)])
    chip_add(["ffn2_w_in"], got2)
    pair_add(mixo, got)
    dz = _concat_cols([dbg, dcg, dhc, dq_, df_, di_, dgr_, dgc, dgh], "dz_concat")
    dx1m, full, got = _mm(dz, w3("mix_w_in"), name="d_mix_in_x", tb=True, b_blocked=True, tm=1024, tn=1024, tk=2816,
                          comm=[pair_share("ffn2_w_in"), chip_exchange(*mixo)])
    update(["ffn2_w_in"], full)
    chip_add(mixo, got)
    grads["mix_w_in"], full = _mm(x1b, dz, name="d_mix_in_w", ta=True, out_dtype=BF16, out_blocked=N_CHIPS, tk=4096, comm=[pair_share(*mixo)])
    update(mixo, full)
    dr1, dy1b, dg0, db0 = _ln_bwd(dr2, dx1m, r1, lng[0], 0.5, "ln0_bwd")
    dz1, got = _swiglu_bwd(dy1b, w2("ffn1_w_out"), z1, "d_ffn1_out_x", comm=[pair_exchange("mix_w_in")])
    dz1 = tuple(dz1)
    pair_add(["mix_w_in"], got)
    mix_sends = [sends["mix_w_in"]]
    g_, got_a = _mm(h1, dy1b, name="d_ffn1_out_w", ta=True, out_dtype=BF16, tm=1408, tk=2048, tn=1024, comm=[_rs_chip_exchange(mix_sends, (0, 2))])
    grads["ffn1_w_out"] = g_.reshape(2, n_pad, d)
    g_other, got2, got = _mm(x0b, dz1, name="d_ffn1_in_w_other", ta=True, out_dtype=BF16, out_blocked=N_CHIPS, tk=2048, half=(sp, True),
                             comm=[_rs_chip_exchange(mix_sends, (1, 2), got_a), pair_exchange("ffn1_w_out")])
    chip_add(["mix_w_in"], got2)
    pair_add(["ffn1_w_out"], got)
    grads["ffn1_w_in"], full, got2, got = _mm(
        x0b, dz1, name="d_ffn1_in_w_own", ta=True, out_dtype=BF16, out_blocked=N_CHIPS, tk=2048, half=(sp, False),
        comm=[pair_share("mix_w_in"), chip_exchange("ffn1_w_out"),
              _rs_pair_exchange([g_other], [meta["ffn1_w_in"]], [rows["ffn1_w_in"]])])
    update(["mix_w_in"], full)
    chip_add(["ffn1_w_out"], got2)
    pair_add(["ffn1_w_in"], got)
    dx0, got2, full = _mm(dz1, w3("ffn1_w_in"), name="d_ffn1_in_x", tb=True, b_blocked=True, tm=512, tn=1024, tk=2816,
                          add=(dr1, ALPHA), comm=[chip_exchange("ffn1_w_in"), pair_share("ffn1_w_out")])
    chip_add(["ffn1_w_in"], got2)
    update(["ffn1_w_out"], full)
    grad_x = dx0.reshape(x.shape)
    update(["ffn1_w_in"], _run_stages([pair_share("ffn1_w_in")], "rs_tail_pair")[0])

    pack = jnp.concatenate([
        dg0, dg1, dg2, dg3, db0, db1, db2, db3,
        jnp.pad(dcw, ((0, 0), (0, d - w))), jnp.pad(dhg, ((0, 0), (0, d - w))),
        jnp.pad(jnp.sum(dnw.reshape(-1, HEAD), axis=0, keepdims=True), ((0, 0), (0, d - HEAD))), sq], axis=0)
    pack = _all_reduce_small(jnp.pad(pack, ((0, 1), (0, 0))), "reduce_small")
    loss = (0.5 / d) * jnp.sum(pack[14])
    g_ln_g = lax.dynamic_slice_in_dim(pack[0:4], chip * dq, dq, axis=1)
    g_ln_b = lax.dynamic_slice_in_dim(pack[4:8], chip * dq, dq, axis=1)
    g_conv = lax.dynamic_slice_in_dim(pack[8:11, :w], chip * wq, wq, axis=1)
    g_hg = pack[11:13, :w]
    g_nw = pack[13:14, :HEAD]

    small_w = dict(ln_g=(ln_g, g_ln_g, m_ln_g, v_ln_g), ln_b=(ln_b, g_ln_b, m_ln_b, v_ln_b),
                   conv_w=(conv_w, g_conv, m_conv_w, v_conv_w), hg_lower_bound=(hg_lower_bound, g_hg, m_hg_lower_bound, v_hg_lower_bound),
                   hg_norm_w=(hg_norm_w, g_nw, m_hg_norm_w, v_hg_norm_w))
    for k, (w_, g_, m_, v_) in small_w.items():
        outs[k] = _adamw(w_, g_.reshape(-1, w_.shape[-1]), m_, v_, "adamw_" + k)

    order = ["ln_g", "ln_b", "ffn1_w_in", "ffn1_w_out", "mix_w_in", "conv_w", "hg_lower_bound", "hg_norm_w", "branch_w_conv",
             "branch_w_hgrn", "mix_w_out", "ffn2_w_in", "ffn2_w_out", "ple_w_gate", "ple_w_proj"]
    return (loss, grad_x, *[outs[k][0] for k in order], *[outs[k][1] for k in order], *[outs[k][2] for k in order],
            *[outs[k][3] for k in order])
```

```python
import collections
import functools

import jax
import jax.numpy as jnp
from jax import lax
from jax.experimental import pallas as pl
from jax.experimental.pallas import tpu as pltpu

F32 = jnp.float32
BF16 = jnp.bfloat16
MESH = pl.DeviceIdType.MESH
ANY = pl.BlockSpec(memory_space=pl.ANY)
VMEM_SPEC = pl.BlockSpec(memory_space=pltpu.VMEM)
SDS = jax.ShapeDtypeStruct

DEPTH = 1
ALPHA = (2.0 * DEPTH) ** 0.25
LN_EPS = 1e-5
RMS_EPS = 1e-6
CHUNK = 32
HEAD = 128
ADAM_LR, ADAM_B1, ADAM_B2, ADAM_EPS, ADAM_WD, ADAM_STEP = 0.001, 0.9, 0.999, 1e-08, 0.01, 10

LANES = 128
N_CHIPS = 4
N_DEV = 8
FIRST_GATHER_PARTS = 8
VMEM_LIMIT = 52 * 1024 * 1024
MM_PIECE = 512


def _cparams(*sem):
    if sem:
        return pltpu.CompilerParams(dimension_semantics=sem, vmem_limit_bytes=VMEM_LIMIT)
    return pltpu.CompilerParams(vmem_limit_bytes=VMEM_LIMIT)


def _tile(n, target, mult):
    best = None
    for t in range(mult, min(n, target) + 1, mult):
        if n % t == 0:
            best = t
    return best if best is not None else n


def _sigmoid(x):
    return 1.0 / (1.0 + jnp.exp(-x))


_Stage = collections.namedtuple("_Stage", "ins out_shapes aliases sems start finish")


def _hosted_call(compute, stages, *, name, grid, in_specs, out_specs, out_shape, scratch_shapes, operands, parallel,
                 prefetch=None):
    n_cmp, n_out, n_scr = len(in_specs), len(out_specs), len(scratch_shapes)
    n_in = n_cmp
    n_pre = int(prefetch is not None)
    c_in = [len(s.ins) for s in stages]
    c_out = [len(s.out_shapes) for s in stages]
    c_sem = [len(s.sems) for s in stages]
    aliases = {}
    for si, s in enumerate(stages):
        for a_in, a_out in s.aliases.items():
            aliases[n_pre + n_in + sum(c_in[:si]) + a_in] = n_out + sum(c_out[:si]) + a_out

    def body(*refs):
        refs = refs[n_pre:]
        ins = refs[:n_cmp]
        cins = refs[n_in:n_in + sum(c_in)]
        outs = refs[n_in + sum(c_in):n_in + sum(c_in) + n_out]
        couts = refs[n_in + sum(c_in) + n_out:n_in + sum(c_in) + n_out + sum(c_out)]
        scr = refs[n_in + sum(c_in) + n_out + sum(c_out):][:n_scr]
        sems = refs[n_in + sum(c_in) + n_out + sum(c_out) + n_scr:]

        def stage_refs(si):
            return (cins[sum(c_in[:si]):sum(c_in[:si + 1])], couts[sum(c_out[:si]):sum(c_out[:si + 1])],
                    sems[sum(c_sem[:si]):sum(c_sem[:si + 1])])

        if stages:
            first = functools.reduce(jnp.logical_and, [pl.program_id(ax) == 0 for ax in range(len(grid))])
            last = functools.reduce(jnp.logical_and, [pl.program_id(ax) == grid[ax] - 1 for ax in range(len(grid))])

            @pl.when(first)
            def _():
                for si, s in enumerate(stages):
                    s.start(*stage_refs(si))

        compute(*ins, *outs, *scr)
        if stages:
            @pl.when(last)
            def _():
                for si, s in enumerate(stages):
                    s.finish(*stage_refs(si))

    sem = ("arbitrary",) * len(grid) if stages else ("parallel",) * parallel + ("arbitrary",) * (len(grid) - parallel)
    all_in = list(in_specs) + [ANY] * (n_in - n_cmp + sum(c_in))
    all_out = list(out_specs) + [ANY] * sum(c_out)
    all_scr = list(scratch_shapes) + [q for s in stages for q in s.sems]
    all_shape = list(out_shape) + [o for s in stages for o in s.out_shapes]
    args = list(operands) + [a for s in stages for a in s.ins]
    if prefetch is None:
        res = pl.pallas_call(body, name=name, grid=grid, in_specs=all_in, out_specs=all_out, out_shape=all_shape,
                             input_output_aliases=aliases, scratch_shapes=all_scr, compiler_params=_cparams(*sem))(*args)
    else:
        grid_spec = pltpu.PrefetchScalarGridSpec(num_scalar_prefetch=1, grid=grid, in_specs=all_in, out_specs=all_out,
                                                 scratch_shapes=all_scr)
        res = pl.pallas_call(body, name=name, grid_spec=grid_spec, out_shape=all_shape, input_output_aliases=aliases,
                             compiler_params=_cparams(*sem))(prefetch, *args)
    main = res[0] if n_out == 1 else list(res[:n_out])
    if not stages:
        return main
    rest = res[n_out:]
    return (main, *[list(rest[sum(c_out[:si]):sum(c_out[:si + 1])]) for si in range(len(stages))])


def _run_stages(stages, name):
    def body(*refs):
        n_i = sum(len(s.ins) for s in stages)
        n_o = sum(len(s.out_shapes) for s in stages)
        cins, couts, sems = refs[:n_i], refs[n_i:n_i + n_o], refs[n_i + n_o:]
        pos = [0, 0, 0]
        parts = []
        for s in stages:
            parts.append((cins[pos[0]:pos[0] + len(s.ins)], couts[pos[1]:pos[1] + len(s.out_shapes)], sems[pos[2]:pos[2] + len(s.sems)]))
            pos = [pos[0] + len(s.ins), pos[1] + len(s.out_shapes), pos[2] + len(s.sems)]
        for s, p_ in zip(stages, parts):
            s.start(*p_)
        for s, p_ in zip(stages, parts):
            s.finish(*p_)

    aliases, ni, no = {}, 0, 0
    for s in stages:
        for a_in, a_out in s.aliases.items():
            aliases[ni + a_in] = no + a_out
        ni, no = ni + len(s.ins), no + len(s.out_shapes)
    res = pl.pallas_call(
        body, name=name, in_specs=[ANY] * ni, out_specs=[ANY] * no, out_shape=[o for s in stages for o in s.out_shapes],
        input_output_aliases=aliases, scratch_shapes=[q for s in stages for q in s.sems],
    )(*[a for s in stages for a in s.ins])
    out, pos = [], 0
    for s in stages:
        out.append(list(res[pos:pos + len(s.out_shapes)]))
        pos += len(s.out_shapes)
    return out


def _mm(a, b, *, name, ta=False, tb=False, b_blocked=False, out_blocked=0, out_dtype=F32,
        tm=512, tn=1408, tk=2048, comm=(), half=None, add=None):
    if ta:
        kd, m = a.shape
    else:
        m, kd = a.shape
    if b_blocked and not tb:
        g, kb, nb = b.shape
        assert kb == kd
        n = g * nb
        tn = _tile(nb, tn, LANES)
        tk = _tile(kd, tk, LANES)
        per_n = nb // tn
        b_spec = pl.BlockSpec((None, tk, tn), lambda i, j, k, *s: (j // per_n, k, j % per_n))
    elif b_blocked and tb:
        g, n, kb = b.shape
        assert g * kb == kd
        tn = _tile(n, tn, LANES)
        tk = _tile(kb, tk, LANES)
        per_k = kb // tk
        b_spec = pl.BlockSpec((None, tn, tk), lambda i, j, k, *s: (k // per_k, j, k % per_k))
    elif tb:
        n, kb = b.shape
        assert kb == kd
        tn = _tile(n, tn, LANES)
        tk = _tile(kd, tk, LANES)
        b_spec = pl.BlockSpec((tn, tk), lambda i, j, k, *s: (j, k))
    else:
        kb, n = b.shape
        assert kb == kd
        tn = _tile(n // out_blocked if out_blocked else n, tn, LANES)
        per_o = (n // out_blocked) // tn if out_blocked else None
        tk = _tile(kd, tk, LANES)
        b_spec = pl.BlockSpec((tk, tn), lambda i, j, k, *s: (k, j))
    m_run = m // 2 if half else m
    tm = _tile(m_run, tm, LANES if ta else 8)

    def row(i, s):
        if not half:
            return i
        h = 1 - s[0][0] if half[1] else s[0][0]
        return h * (m_run // tm) + i

    if ta:
        a_spec = pl.BlockSpec((tk, tm), lambda i, j, k, *s: (k, row(i, s)))
    else:
        a_spec = pl.BlockSpec((tm, tk), lambda i, j, k, *s: (row(i, s), k))
    if out_blocked:
        assert not b_blocked and not tb
        o_spec = pl.BlockSpec((None, tm, tn), lambda i, j, k, *s: (j // per_o, row(i, s), j % per_o))
        o_shape = SDS((out_blocked, m, n // out_blocked), out_dtype)
    else:
        o_spec = pl.BlockSpec((tm, tn), lambda i, j, k, *s: (row(i, s), j))
        o_shape = SDS((m, n), out_dtype)
    nk = kd // tk
    dn = (((0 if ta else 1,), (1 if tb else 0,)), ((), ()))
    grid = (m_run // tm, n // tn, nk)

    pieces = [(lo, min(MM_PIECE, tn - lo)) for lo in range(0, tn, MM_PIECE)]

    def compute(a_ref, b_ref, *rest):
        add_ref = rest[0] if add else None
        o_ref, acc_ref = rest[-2:]
        a_tile = a_ref[...].astype(BF16)
        k = pl.program_id(2)

        def result(acc, cols):
            if add:
                acc = acc + add[1] * add_ref[:, cols]
            return acc.astype(o_ref.dtype)

        if nk > 1:
            @pl.when(k == 0)
            def _():
                acc_ref[...] = jnp.zeros_like(acc_ref)

        for lo, wd in pieces:
            cols = slice(lo, lo + wd)
            b_tile = b_ref[cols, :] if tb else b_ref[:, cols]
            part = lax.dot_general(a_tile, b_tile.astype(BF16), dn, preferred_element_type=F32)
            if nk == 1:
                o_ref[:, cols] = result(part, cols)
            else:
                acc_ref[:, cols] += part

        if nk > 1:
            @pl.when(k == nk - 1)
            def _():
                o_ref[...] = result(acc_ref[...], slice(None))

    extra = [(add[0], o_spec)] if add else []
    return _hosted_call(compute, comm, name=name, grid=grid, in_specs=[a_spec, b_spec] + [s_ for _, s_ in extra], out_specs=[o_spec],
                        out_shape=[o_shape], scratch_shapes=[pltpu.VMEM((tm, tn), F32)], operands=(a, b, *[a_ for a_, _ in extra]),
                        parallel=2, prefetch=half[0] if half else None)


def _swiglu_fwd(z, name, comm=()):
    t, n = z.shape
    n2 = n // 2
    tr = _tile(t, 128, 16)

    def body(a_ref, u_ref, o_ref):
        a = a_ref[...].astype(F32)
        o_ref[...] = (a * _sigmoid(a) * u_ref[...].astype(F32)).astype(o_ref.dtype)

    return _hosted_call(
        body, comm, name=name, grid=(t // tr,),
        in_specs=[pl.BlockSpec((tr, n2), lambda i: (i, 0)), pl.BlockSpec((tr, n2), lambda i: (i, 1))],
        out_specs=[pl.BlockSpec((tr, n2), lambda i: (i, 0))], out_shape=[SDS((t, n2), BF16)], scratch_shapes=[],
        operands=(z, z), parallel=1)


def _swiglu_bwd(dh, z, name):
    t, n = z.shape
    n2 = n // 2
    tr = _tile(t, 128, 16)

    def body(dh_ref, a_ref, u_ref, o_ref):
        a = a_ref[...].astype(F32)
        dh_ = dh_ref[...].astype(F32)
        s = _sigmoid(a)
        o_ref[:, 0:n2] = (dh_ * u_ref[...].astype(F32) * (s * (1.0 + a * (1.0 - s)))).astype(o_ref.dtype)
        o_ref[:, n2:n] = (dh_ * a * s).astype(o_ref.dtype)

    return pl.pallas_call(
        body, name=name, grid=(t // tr,),
        in_specs=[pl.BlockSpec((tr, n2), lambda i: (i, 0)), pl.BlockSpec((tr, n2), lambda i: (i, 0)),
                  pl.BlockSpec((tr, n2), lambda i: (i, 1))],
        out_specs=pl.BlockSpec((tr, n), lambda i: (i, 0)), out_shape=SDS((t, n), BF16),
        compiler_params=_cparams("parallel"),
    )(dh, z, z)


def _ln_stats(r):
    mu = jnp.mean(r, axis=-1, keepdims=True)
    xc = r - mu
    var = jnp.mean(xc * xc, axis=-1, keepdims=True)
    return xc * lax.rsqrt(var + LN_EPS)


def _ln_fwd(xp, y, g, b, gp, bp, scale, name, comm=()):
    t, d = xp.shape
    tr = _tile(t, 256, 16)

    def body(xp_ref, y_ref, g_ref, b_ref, *rest):
        r_ref, xb_ref = rest[-2:]
        x_prev = xp_ref[...]
        if gp is not None:
            x_prev = _ln_stats(x_prev) * rest[0][...] + rest[1][...]
        r = ALPHA * x_prev + scale * y_ref[...]
        r_ref[...] = r
        xb_ref[...] = (_ln_stats(r) * g_ref[...] + b_ref[...]).astype(BF16)

    row = pl.BlockSpec((tr, d), lambda i: (i, 0))
    vec = pl.BlockSpec((1, d), lambda i: (0, 0))
    prev = [] if gp is None else [gp, bp]
    return _hosted_call(
        body, comm, name=name, grid=(t // tr,), in_specs=[row, row, vec, vec] + [vec] * len(prev), out_specs=[row, row],
        out_shape=[SDS((t, d), F32), SDS((t, d), BF16)], scratch_shapes=[], operands=(xp, y, g, b, *prev), parallel=1)


def _ln_bwd(dra, dxm, r, g, scale, name):
    t, d = r.shape
    tr = _tile(t, 256, 16)

    def body(dra_ref, dxm_ref, r_ref, g_ref, dr_ref, dyb_ref, dg_ref, db_ref):
        i = pl.program_id(0)
        dx = ALPHA * dra_ref[...] + dxm_ref[...]
        rr = r_ref[...]
        mu = jnp.mean(rr, axis=-1, keepdims=True)
        xc = rr - mu
        rstd = lax.rsqrt(jnp.mean(xc * xc, axis=-1, keepdims=True) + LN_EPS)
        xh = xc * rstd
        dxh = dx * g_ref[...]
        dr = rstd * (dxh - jnp.mean(dxh, axis=-1, keepdims=True) - xh * jnp.mean(dxh * xh, axis=-1, keepdims=True))
        dr_ref[...] = dr
        dyb_ref[...] = (scale * dr).astype(BF16)
        dg = jnp.sum(dx * xh, axis=0, keepdims=True)
        db = jnp.sum(dx, axis=0, keepdims=True)

        @pl.when(i == 0)
        def _():
            dg_ref[...] = dg
            db_ref[...] = db

        @pl.when(i > 0)
        def _():
            dg_ref[...] += dg
            db_ref[...] += db

    row = pl.BlockSpec((tr, d), lambda i: (i, 0))
    vec = pl.BlockSpec((1, d), lambda i: (0, 0))
    return pl.pallas_call(
        body, name=name, grid=(t // tr,), in_specs=[row, row, row, vec], out_specs=[row, row, vec, vec],
        out_shape=[SDS((t, d), F32), SDS((t, d), BF16), SDS((1, d), F32), SDS((1, d), F32)],
        compiler_params=_cparams("arbitrary"),
    )(dra, dxm, r, g)


def _tail(r3, g3, b3, gp, pp, g, b, target, name):
    t, d = r3.shape
    tr = _tile(t, 256, 16)

    def body(r3_ref, g3_ref, b3_ref, gp_ref, pp_ref, g_ref, b_ref, tg_ref, dr_ref, dgp_ref, dpp_ref, dg_ref, db_ref, sq_ref):
        i = pl.program_id(0)
        gate = _sigmoid(gp_ref[...])
        pp_ = pp_ref[...]
        r = ALPHA * (_ln_stats(r3_ref[...]) * g3_ref[...] + b3_ref[...]) + gate * pp_
        mu = jnp.mean(r, axis=-1, keepdims=True)
        xc = r - mu
        rstd = lax.rsqrt(jnp.mean(xc * xc, axis=-1, keepdims=True) + LN_EPS)
        xh = xc * rstd
        err = xh * g_ref[...] + b_ref[...] - tg_ref[...]
        dx = err * (1.0 / d)
        dxh = dx * g_ref[...]
        dr = rstd * (dxh - jnp.mean(dxh, axis=-1, keepdims=True) - xh * jnp.mean(dxh * xh, axis=-1, keepdims=True))
        dr_ref[...] = dr
        dgp_ref[...] = (dr * pp_ * gate * (1.0 - gate)).astype(BF16)
        dpp_ref[...] = (dr * gate).astype(BF16)
        dg = jnp.sum(dx * xh, axis=0, keepdims=True)
        db = jnp.sum(dx, axis=0, keepdims=True)
        sq = jnp.sum(err * err, axis=0, keepdims=True)

        @pl.when(i == 0)
        def _():
            dg_ref[...] = dg
            db_ref[...] = db
            sq_ref[...] = sq

        @pl.when(i > 0)
        def _():
            dg_ref[...] += dg
            db_ref[...] += db
            sq_ref[...] += sq

    row = pl.BlockSpec((tr, d), lambda i: (i, 0))
    vec = pl.BlockSpec((1, d), lambda i: (0, 0))
    return pl.pallas_call(
        body, name=name, grid=(t // tr,), in_specs=[row, vec, vec, row, row, vec, vec, row],
        out_specs=[row, row, row, vec, vec, vec],
        out_shape=[SDS((t, d), F32), SDS((t, d), BF16), SDS((t, d), BF16), SDS((1, d), F32), SDS((1, d), F32),
                   SDS((1, d), F32)],
        compiler_params=_cparams("arbitrary"),
    )(r3, g3, b3, gp, pp, g, b, target)


def _to_bf16(x, name):
    t, d = x.shape
    tr = _tile(t, 512, 16)
    row = pl.BlockSpec((tr, d), lambda i: (i, 0))

    def body(x_ref, o_ref):
        o_ref[...] = x_ref[...].astype(BF16)

    return pl.pallas_call(body, name=name, grid=(t // tr,), in_specs=[row], out_specs=row, out_shape=SDS((t, d), BF16),
                          compiler_params=_cparams("parallel"))(x)


def _concat_cols(parts, name):
    t = parts[0].shape[0]
    widths = [p_.shape[1] for p_ in parts]
    tr = _tile(t, 256, 16)

    def body(*refs):
        o_ref = refs[-1]
        at = 0
        for ref, wd in zip(refs[:-1], widths):
            o_ref[:, at:at + wd] = ref[...]
            at += wd

    return pl.pallas_call(
        body, name=name, grid=(t // tr,), in_specs=[pl.BlockSpec((tr, wd), lambda i: (i, 0)) for wd in widths],
        out_specs=pl.BlockSpec((tr, sum(widths)), lambda i: (i, 0)), out_shape=SDS((t, sum(widths)), parts[0].dtype),
        compiler_params=_cparams("parallel"),
    )(*parts)


def _merge_fwd(z, ma, mb, w, name):
    t = z.shape[0]
    tr = _tile(t, 256, 16)

    def body(gc_ref, gh_ref, ma_ref, mb_ref, o_ref):
        o_ref[...] = (_sigmoid(gc_ref[...]) * ma_ref[...] + _sigmoid(gh_ref[...]) * mb_ref[...]).astype(BF16)

    half = pl.BlockSpec((tr, w), lambda i, j: (i, j))
    return pl.pallas_call(
        body, name=name, grid=(t // tr, 2),
        in_specs=[pl.BlockSpec((tr, w), lambda i, j: (i, 7 + j)), pl.BlockSpec((tr, w), lambda i, j: (i, 9 + j)), half, half],
        out_specs=half, out_shape=SDS((t, 2 * w), BF16), compiler_params=_cparams("parallel", "parallel"),
    )(z, z, ma, mb)


def _merge_bwd(dmer, z, ma, mb, w, name):
    t = z.shape[0]
    tr = _tile(t, 256, 16)

    def body(d_ref, gc_ref, gh_ref, ma_ref, mb_ref, dma_ref, dmb_ref, dgc_ref, dgh_ref):
        dm = d_ref[...]
        sc = _sigmoid(gc_ref[...])
        sh = _sigmoid(gh_ref[...])
        dma_ref[...] = (dm * sc).astype(BF16)
        dmb_ref[...] = (dm * sh).astype(BF16)
        dgc_ref[...] = (dm * ma_ref[...] * sc * (1.0 - sc)).astype(BF16)
        dgh_ref[...] = (dm * mb_ref[...] * sh * (1.0 - sh)).astype(BF16)

    half = pl.BlockSpec((tr, w), lambda i, j: (i, j))
    return pl.pallas_call(
        body, name=name, grid=(t // tr, 2),
        in_specs=[half, pl.BlockSpec((tr, w), lambda i, j: (i, 7 + j)), pl.BlockSpec((tr, w), lambda i, j: (i, 9 + j)), half, half],
        out_specs=[half] * 4, out_shape=[SDS((t, 2 * w), BF16)] * 4, compiler_params=_cparams("parallel", "parallel"),
    )(dmer, z, z, ma, mb)


def _shift_down(x, s, row):
    return jnp.where(row >= s, pltpu.roll(x, s, axis=0), 0.0)


def _shift_up(x, s, row, t):
    return jnp.where(row < t - s, pltpu.roll(x, t - s, axis=0), 0.0)


def _conv_fwd(z, cw, w, name):
    t = z.shape[0]
    tc = LANES
    nb = w // tc

    def body(b_ref, c_ref, h_ref, w_ref, o_ref):
        u = c_ref[...] * h_ref[...]
        row = lax.broadcasted_iota(jnp.int32, u.shape, 0)
        cw_ = w_ref[...]
        conv = cw_[2:3, :] * u + cw_[1:2, :] * _shift_down(u, 1, row) + cw_[0:1, :] * _shift_down(u, 2, row)
        o_ref[...] = (b_ref[...] * conv).astype(BF16)

    col = lambda off: pl.BlockSpec((t, tc), lambda j: (0, off * nb + j))
    return pl.pallas_call(
        body, name=name, grid=(nb,), in_specs=[col(0), col(1), col(2), pl.BlockSpec((3, tc), lambda j: (0, j))],
        out_specs=pl.BlockSpec((t, tc), lambda j: (0, j)), out_shape=SDS((t, w), BF16), compiler_params=_cparams("parallel"),
    )(z, z, z, cw)


def _conv_bwd(dy, z, cw, w, name):
    t = z.shape[0]
    tc = LANES
    nb = w // tc

    def body(dy_ref, b_ref, c_ref, h_ref, w_ref, db_ref, dc_ref, dh_ref, dw_ref):
        c_, h_ = c_ref[...], h_ref[...]
        u = c_ * h_
        row = lax.broadcasted_iota(jnp.int32, u.shape, 0)
        cw_ = w_ref[...]
        u1 = _shift_down(u, 1, row)
        u2 = _shift_down(u, 2, row)
        dy_ = dy_ref[...]
        db_ref[...] = (dy_ * (cw_[2:3, :] * u + cw_[1:2, :] * u1 + cw_[0:1, :] * u2)).astype(BF16)
        dconv = dy_ * b_ref[...]
        du = cw_[2:3, :] * dconv + cw_[1:2, :] * _shift_up(dconv, 1, row, t) + cw_[0:1, :] * _shift_up(dconv, 2, row, t)
        dc_ref[...] = (du * h_).astype(BF16)
        dh_ref[...] = (du * c_).astype(BF16)
        dw_ref[0:1, :] = jnp.sum(dconv * u2, axis=0, keepdims=True)
        dw_ref[1:2, :] = jnp.sum(dconv * u1, axis=0, keepdims=True)
        dw_ref[2:3, :] = jnp.sum(dconv * u, axis=0, keepdims=True)

    col = lambda off: pl.BlockSpec((t, tc), lambda j: (0, off * nb + j))
    own = pl.BlockSpec((t, tc), lambda j: (0, j))
    wsp = pl.BlockSpec((3, tc), lambda j: (0, j))
    return pl.pallas_call(
        body, name=name, grid=(nb,), in_specs=[own, col(0), col(1), col(2), wsp], out_specs=[own, own, own, wsp],
        out_shape=[SDS((t, w), BF16)] * 3 + [SDS((3, w), F32)], compiler_params=_cparams("parallel"),
    )(dy, z, z, z, cw)


def _lower_bound(hg):
    mx = jnp.max(hg, axis=0, keepdims=True)
    e = jnp.exp(hg - mx)
    inv = 1.0 / jnp.sum(e, axis=0, keepdims=True)
    return e[0:1, :] * inv, e[1:2, :] * inv


def _chunk_cumsum(x, row):
    s = 1
    while s < CHUNK:
        x = x + jnp.where(row % CHUNK >= s, pltpu.roll(x, s, axis=0), 0.0)
        s *= 2
    return x


def _dot_nt(a, b):
    return lax.dot_general(a.astype(BF16), b.astype(BF16), (((1,), (1,)), ((), ())), preferred_element_type=F32)


def _dot_tn(a, b):
    return lax.dot_general(a.astype(BF16), b.astype(BF16), (((0,), (0,)), ((), ())), preferred_element_type=F32)


def _dot_nn(a, b):
    return jnp.dot(a.astype(BF16), b.astype(BF16), preferred_element_type=F32)


def _tril(x):
    r = lax.broadcasted_iota(jnp.int32, x.shape, 0)
    c = lax.broadcasted_iota(jnp.int32, x.shape, 1)
    return jnp.where(r >= c, x, 0.0)


HGRN_GROUP = 4
HGRN_ROWS = 512
HGRN_UNROLL = 2


def _unrolled_loop(n, step, init):
    assert n % HGRN_UNROLL == 0

    def trip(i, carry):
        for u in range(HGRN_UNROLL):
            carry = step(i * HGRN_UNROLL + u, carry)
        return carry

    return lax.fori_loop(0, n // HGRN_UNROLL, trip, init)


def _hgrn_chunk_inputs(q_ref, f_ref, cum_ref, lb, rows, ln):
    qr = q_ref[rows, ln]
    q = qr * _sigmoid(qr)
    f = lb + (1.0 - lb) * _sigmoid(f_ref[rows, ln])
    return q, 1.0 - f, cum_ref[rows, ln]


def _hgrn_fwd(z, hg, nw, w, name, comm=()):
    t = z.shape[0]
    nh = w // HEAD
    gh = _tile(nh, HGRN_GROUP, 1)
    gw = gh * HEAD
    ngrp = nh // gh
    tb = _tile(t, HGRN_ROWS, CHUNK)
    ncb = tb // CHUNK

    def body(q_ref, f_ref, i_ref, g_ref, hg_ref, nw_ref, y_ref, o_ref, st_ref, cum_ref, *s_refs):
        lb_all, _ = _lower_bound(hg_ref[...])
        row = lax.broadcasted_iota(jnp.int32, (tb, gw), 0)
        cum_ref[...] = _chunk_cumsum(jnp.log(lb_all + (1.0 - lb_all) * _sigmoid(f_ref[...])), row)

        @pl.when(pl.program_id(1) == 0)
        def _():
            for s_ref in s_refs:
                s_ref[...] = jnp.zeros_like(s_ref)

        def step(c, carry):
            rows = pl.ds(pl.multiple_of(c * CHUNK, CHUNK), CHUNK)
            for g in range(gh):
                ln = slice(g * HEAD, (g + 1) * HEAD)
                lb = lb_all[:, ln]
                q, k, cum = _hgrn_chunk_inputs(q_ref, f_ref, cum_ref, lb, rows, ln)
                v = i_ref[rows, ln]
                last = cum[CHUNK - 1:CHUNK, :]
                mid = cum[CHUNK // 2 - 1:CHUNK // 2, :]
                st = s_refs[g][...]
                st_ref[g, c] = st.astype(BF16)
                scores = _tril(_dot_nt(q * jnp.exp(cum - mid), k * jnp.exp(mid - cum)))
                o_ref[rows, ln] = _dot_nt(q * jnp.exp(cum), st) + _dot_nn(scores, v)
                s_refs[g][...] = st * jnp.exp(last) + _dot_tn(v, k * jnp.exp(last - cum))
            return carry

        _unrolled_loop(ncb, step, 0)
        for g in range(gh):
            ln = slice(g * HEAD, (g + 1) * HEAD)
            o = o_ref[:, ln]
            n = o * lax.rsqrt(jnp.mean(o * o, axis=-1, keepdims=True) + RMS_EPS)
            gr = g_ref[:, ln]
            y_ref[:, ln] = (n * nw_ref[...] * gr * _sigmoid(gr)).astype(BF16)

    col = lambda off: pl.BlockSpec((tb, gw), lambda h, j: (j, off * ngrp + h))
    own = pl.BlockSpec((tb, gw), lambda h, j: (j, h))
    return _hosted_call(
        body, comm, name=name, grid=(ngrp, t // tb),
        in_specs=[col(3), col(4), col(5), col(6), pl.BlockSpec((2, gw), lambda h, j: (0, h)),
                  pl.BlockSpec((1, HEAD), lambda h, j: (0, 0))],
        out_specs=[own, own, pl.BlockSpec((gh, ncb, HEAD, HEAD), lambda h, j: (h, j, 0, 0))],
        out_shape=[SDS((t, w), BF16), SDS((t, w), F32), SDS((nh, t // CHUNK, HEAD, HEAD), BF16)],
        scratch_shapes=[pltpu.VMEM((tb, gw), F32)] + [pltpu.VMEM((HEAD, HEAD), F32)] * gh,
        operands=(z, z, z, z, hg, nw), parallel=1)


def _hgrn_bwd(dy, z, o, states, hg, nw, w, name, comm=()):
    t = z.shape[0]
    nh = w // HEAD
    gh = _tile(nh, HGRN_GROUP, 1)
    gw = gh * HEAD
    ngrp = nh // gh
    tb = _tile(t, HGRN_ROWS, CHUNK)
    ncb = tb // CHUNK
    nt = t // tb

    def body(dy_ref, q_ref, f_ref, i_ref, g_ref, o_ref, st_ref, hg_ref, nw_ref,
             dq_ref, df_ref, di_ref, dg_ref, dhg_ref, dnw_ref, cum_ref, do_ref, *ds_refs):
        lb_all, s1_all = _lower_bound(hg_ref[...])
        row = lax.broadcasted_iota(jnp.int32, (tb, gw), 0)
        crow = lax.broadcasted_iota(jnp.int32, (CHUNK, HEAD), 0)
        cum_ref[...] = _chunk_cumsum(jnp.log(lb_all + (1.0 - lb_all) * _sigmoid(f_ref[...])), row)

        @pl.when(pl.program_id(1) == 0)
        def _():
            for ds_ref in ds_refs:
                ds_ref[...] = jnp.zeros_like(ds_ref)
            dhg_ref[...] = jnp.zeros_like(dhg_ref)
            dnw_ref[...] = jnp.zeros_like(dnw_ref)

        for g in range(gh):
            ln = slice(g * HEAD, (g + 1) * HEAD)
            o_ = o_ref[:, ln]
            rstd = lax.rsqrt(jnp.mean(o_ * o_, axis=-1, keepdims=True) + RMS_EPS)
            n = o_ * rstd
            gr = g_ref[:, ln]
            sg = _sigmoid(gr)
            dy_ = dy_ref[:, ln]
            dg_ref[:, ln] = (dy_ * n * nw_ref[...] * (sg * (1.0 + gr * (1.0 - sg)))).astype(BF16)
            dsil = dy_ * gr * sg
            dnw_ref[:, ln] += jnp.sum(dsil * n, axis=0, keepdims=True)
            dn = dsil * nw_ref[...]
            do_ref[:, ln] = rstd * (dn - n * jnp.mean(dn * n, axis=-1, keepdims=True))

        def step(cc, dlbs):
            c = ncb - 1 - cc
            rows = pl.ds(pl.multiple_of(c * CHUNK, CHUNK), CHUNK)
            new = []
            for g in range(gh):
                ln = slice(g * HEAD, (g + 1) * HEAD)
                lb = lb_all[:, ln]
                qr = q_ref[rows, ln]
                sq = _sigmoid(qr)
                q = qr * sq
                sf = _sigmoid(f_ref[rows, ln])
                f = lb + (1.0 - lb) * sf
                k = 1.0 - f
                cum = cum_ref[rows, ln]
                v = i_ref[rows, ln]
                do = do_ref[rows, ln]
                last = cum[CHUNK - 1:CHUNK, :]
                mid = cum[CHUNK // 2 - 1:CHUNK // 2, :]
                eg = jnp.exp(cum)
                em = jnp.exp(cum - mid)
                enm = jnp.exp(mid - cum)
                elc = jnp.exp(last - cum)
                qm, km, kl = q * em, k * enm, k * elc
                ds = ds_refs[g][...]
                a = _tril(_dot_nt(qm, km))
                da = _tril(_dot_nt(do, v))
                di_ref[rows, ln] = (_dot_tn(a, do) + _dot_nt(kl, ds)).astype(BF16)
                st = st_ref[g, c]
                dkl = _dot_nn(v, ds)
                dq = _dot_nn(do, st) * eg + _dot_nn(da, km) * em
                dk = _dot_tn(da, qm) * enm + dkl * elc
                el = jnp.exp(last)
                ds_refs[g][...] = ds * el + _dot_tn(do, q * eg)
                dlast = jnp.sum(kl * dkl, axis=0, keepdims=True) + el * jnp.sum(ds * st.astype(F32), axis=0, keepdims=True)
                x = q * dq - k * dk + jnp.where(crow == CHUNK - 1, dlast, 0.0)
                s = 1
                while s < CHUNK:
                    x = x + _shift_up(x, s, crow, CHUNK)
                    s *= 2
                df = x / f - dk
                dq_ref[rows, ln] = (dq * (sq * (1.0 + qr * (1.0 - sq)))).astype(BF16)
                df_ref[rows, ln] = (df * (1.0 - lb) * sf * (1.0 - sf)).astype(BF16)
                new.append(dlbs[g] + jnp.sum(df * (1.0 - sf), axis=0, keepdims=True))
            return tuple(new)

        dlbs = _unrolled_loop(ncb, step, tuple(jnp.zeros((1, HEAD), F32) for _ in range(gh)))
        for g in range(gh):
            ln = slice(g * HEAD, (g + 1) * HEAD)
            dlb = dlbs[g] * lb_all[:, ln] * s1_all[:, ln]
            dhg_ref[0:1, ln] += dlb
            dhg_ref[1:2, ln] -= dlb

    col = lambda off: pl.BlockSpec((tb, gw), lambda h, j: (nt - 1 - j, off * ngrp + h))
    own = pl.BlockSpec((tb, gw), lambda h, j: (nt - 1 - j, h))
    hsp = pl.BlockSpec((2, gw), lambda h, j: (0, h))
    return _hosted_call(
        body, comm, name=name, grid=(ngrp, nt),
        in_specs=[own, col(3), col(4), col(5), col(6), own,
                  pl.BlockSpec((gh, ncb, HEAD, HEAD), lambda h, j: (h, nt - 1 - j, 0, 0)),
                  hsp, pl.BlockSpec((1, HEAD), lambda h, j: (0, 0))],
        out_specs=[own, own, own, own, hsp, pl.BlockSpec((1, gw), lambda h, j: (0, h))],
        out_shape=[SDS((t, w), BF16)] * 4 + [SDS((2, w), F32), SDS((1, w), F32)],
        scratch_shapes=[pltpu.VMEM((tb, gw), F32)] * 2 + [pltpu.VMEM((HEAD, HEAD), F32)] * gh,
        operands=(dy, z, z, z, z, o, states, hg, nw), parallel=1)


def _cast_pad(wt, n_pad, meta, sp, name, comm=()):
    _, r, n = wt.shape
    g, p, per = meta
    tr = _tile(r, max(16, (3 << 19) // n_pad // 16 * 16), 16)

    def body(w_ref, o_ref):
        if n_pad != n:
            o_ref[...] = jnp.zeros(o_ref.shape, o_ref.dtype)
        o_ref[:, 0:n] = w_ref[...].astype(BF16)

    return _hosted_call(
        body, comm, name=name, grid=(r // tr,), in_specs=[pl.BlockSpec((None, tr, n), lambda i, sp: (0, i, 0))],
        out_specs=[pl.BlockSpec((None, tr, n_pad), lambda i, sp: (sp[1] // per, ((sp[1] % per) * r) // tr + i, 0))],
        out_shape=[SDS((g, p, n_pad), BF16)], scratch_shapes=[], operands=(wt,), parallel=1, prefetch=sp)


def _cast_pad_t(wt_t, n_pad, meta, sp, name, comm=()):
    _, n, r = wt_t.shape
    g, p, per = meta
    tc = _tile(r, 256, LANES)

    def body(w_ref, o_ref):
        for lo in range(0, n_pad, LANES):
            rows = min(LANES, n - lo)
            piece = w_ref[lo:lo + rows, :]
            if rows < LANES:
                piece = jnp.concatenate([piece, jnp.zeros((LANES - rows, tc), F32)], axis=0)
            o_ref[:, lo:lo + LANES] = piece.T.astype(BF16)

    return _hosted_call(
        body, comm, name=name, grid=(r // tc,), in_specs=[pl.BlockSpec((None, n, tc), lambda i, sp: (0, 0, i))],
        out_specs=[pl.BlockSpec((None, tc, n_pad), lambda i, sp: (sp[1] // per, ((sp[1] % per) * r) // tc + i, 0))],
        out_shape=[SDS((g, p, n_pad), BF16)], scratch_shapes=[], operands=(wt_t,), parallel=1, prefetch=sp)


def _adam_math(w, g, m, v):
    m2 = ADAM_B1 * m + (1.0 - ADAM_B1) * g
    v2 = ADAM_B2 * v + (1.0 - ADAM_B2) * (g * g)
    c1 = 1.0 / (1.0 - ADAM_B1 ** ADAM_STEP)
    c2 = 1.0 / (1.0 - ADAM_B2 ** ADAM_STEP)
    return -ADAM_LR * ((m2 * c1) / (jnp.sqrt(v2 * c2) + ADAM_EPS) + ADAM_WD * w), m2, v2


def _adamw_t(wt_t, g, m_t, v_t, name):
    _, n, r = wt_t.shape
    ng = g.shape[1]
    tc = LANES

    def body(w_ref, g_ref, m_ref, v_ref, go_ref, d_ref, mo_ref, vo_ref, gt_ref):
        for lo in range(0, ng, LANES):
            gt_ref[lo:lo + LANES, :] = g_ref[:, lo:lo + LANES].T
        g_ = gt_ref[0:n, :]
        delta, m2, v2 = _adam_math(w_ref[...], g_, m_ref[...], v_ref[...])
        go_ref[...] = g_
        d_ref[...] = delta
        mo_ref[...] = m2
        vo_ref[...] = v2

    blk = pl.BlockSpec((None, n, tc), lambda i: (0, 0, i))
    return pl.pallas_call(
        body, name=name, grid=(r // tc,), in_specs=[blk, pl.BlockSpec((tc, ng), lambda i: (i, 0)), blk, blk],
        out_specs=[blk] * 4, out_shape=[SDS(wt_t.shape, F32)] * 4, scratch_shapes=[pltpu.VMEM((ng, tc), F32)],
        compiler_params=_cparams("parallel"),
    )(wt_t, g, m_t, v_t)


def _adamw(wt, g, m, v, name):
    lead = (None,) * (wt.ndim - 2)
    zero = (0,) * (wt.ndim - 2)
    r, n = wt.shape[-2:]
    ng = g.shape[1]
    nct = 2 if ng == n and n % (2 * LANES) == 0 else 1
    tc, tg = n // nct, ng // nct
    tr = _tile(r, max(8, (3 << 17) // tg // 8 * 8), 8)

    def body(w_ref, g_ref, m_ref, v_ref, go_ref, d_ref, mo_ref, vo_ref):
        g_ = g_ref[:, 0:tc]
        delta, m2, v2 = _adam_math(w_ref[...], g_, m_ref[...], v_ref[...])
        go_ref[...] = g_
        d_ref[...] = delta
        mo_ref[...] = m2
        vo_ref[...] = v2

    blk = pl.BlockSpec(lead + (tr, tc), lambda i, j: zero + (i, j))
    return pl.pallas_call(
        body, name=name, grid=(r // tr, nct), in_specs=[blk, pl.BlockSpec((tr, tg), lambda i, j: (i, j)), blk, blk],
        out_specs=[blk] * 4, out_shape=[SDS(wt.shape, F32)] * 4, compiler_params=_cparams("parallel", "parallel"),
    )(wt, g, m, v)


def _place():
    x, y, c = lax.axis_index("x"), lax.axis_index("y"), lax.axis_index("c")
    return x, y, c, 2 * x + y


def _chip_dev(k, c):
    return (k // 2, k % 2, c)


def _half(ref, j, h, rows, per):
    return ref.at[j // per, pl.ds((j % per) * rows + h * (rows // 2), rows // 2)]


def _gather_stage(bufs, metas, rows_of, ici_parts, fwd_parts, zero_pad):
    nw = len(bufs)
    ici_on = [i for i in range(nw) if ici_parts[i] is not None]
    fwd_on = [i for i in range(nw) if fwd_parts[i] is not None]
    pad_jobs = [(i, gi) for i in ici_on if ici_parts[i][0] == 0 and metas[i][1] > metas[i][2] * rows_of[i]
                for gi in range(metas[i][0])]

    def part_of(ref, i, j, h, part):
        per = metas[i][2]
        p, np_ = part
        pr = rows_of[i] // 2 // np_
        return ref.at[j // per, pl.ds((j % per) * rows_of[i] + h * (rows_of[i] // 2) + p * pr, pr)]

    def descriptors(ins, outs, sems):
        src, zp, dst = ins[:nw], ins[nw], outs
        pads, send, recv, fsend, frecv = sems
        x, y, c, me = _place()

        def pad(n):
            i, gi = pad_jobs[n]
            extra = metas[i][1] - metas[i][2] * rows_of[i]
            return pltpu.make_async_copy(zp.at[pl.ds(0, extra)], dst[i].at[gi, pl.ds(metas[i][2] * rows_of[i], extra)], pads.at[n])

        def ici(i, r, frm):
            return pltpu.make_async_remote_copy(
                src_ref=part_of(src[i], i, me, c, ici_parts[i]), dst_ref=part_of(dst[i], i, frm, c, ici_parts[i]),
                send_sem=send.at[i, r - 1], recv_sem=recv.at[i, r - 1], device_id=_chip_dev((me + r) % N_CHIPS, c),
                device_id_type=MESH)

        def d2d(i, r, frm, h):
            blk = part_of(dst[i], i, frm, h, fwd_parts[i])
            return pltpu.make_async_remote_copy(src_ref=blk, dst_ref=blk, send_sem=fsend.at[i, r - 1],
                                                recv_sem=frecv.at[i, r - 1], device_id=(x, y, 1 - c), device_id_type=MESH)

        return pad, ici, d2d, c, me

    def start(ins, outs, sems):
        pad, ici, d2d, c, me = descriptors(ins, outs, sems)
        for n in range(len(pad_jobs)):
            pad(n).start()
        for i in fwd_on:
            for r in range(1, N_CHIPS):
                d2d(i, r, (me - r) % N_CHIPS, c).start()
        for i in ici_on:
            for r in range(1, N_CHIPS):
                ici(i, r, me).start()

    def finish(ins, outs, sems):
        pad, ici, d2d, c, me = descriptors(ins, outs, sems)
        for i in fwd_on:
            for r in range(1, N_CHIPS):
                d2d(i, r, (me - r) % N_CHIPS, 1 - c).wait_recv()
                d2d(i, r, (me - r) % N_CHIPS, c).wait_send()
        for i in ici_on:
            for r in range(1, N_CHIPS):
                ici(i, r, (me - r) % N_CHIPS).wait_recv()
                ici(i, r, me).wait_send()
        for n in range(len(pad_jobs)):
            pad(n).wait()

    return _Stage(ins=list(bufs) + [zero_pad], out_shapes=[SDS(b.shape, b.dtype) for b in bufs],
                  aliases={i: i for i in range(nw)},
                  sems=[pltpu.SemaphoreType.DMA((max(len(pad_jobs), 1),))] + [pltpu.SemaphoreType.DMA((nw, N_CHIPS - 1))] * 4,
                  start=start, finish=finish)


def _gather_small(packed, name):
    r, n = packed.shape

    def body(src, dst, send, recv):
        x, y, c, me = _place()
        dst[me] = src[...]
        cps = []
        for d in range(1, N_CHIPS):
            cp = pltpu.make_async_remote_copy(src_ref=src, dst_ref=dst.at[me], send_sem=send.at[d - 1], recv_sem=recv.at[d - 1],
                                              device_id=_chip_dev((me + d) % N_CHIPS, c), device_id_type=MESH)
            cp.start()
            cps.append(cp)
        for d in range(1, N_CHIPS):
            pltpu.make_async_remote_copy(src_ref=src, dst_ref=dst.at[(me - d) % N_CHIPS], send_sem=send.at[d - 1],
                                         recv_sem=recv.at[d - 1], device_id=_chip_dev((me + d) % N_CHIPS, c),
                                         device_id_type=MESH).wait_recv()
        for cp in cps:
            cp.wait_send()

    return pl.pallas_call(
        body, name=name, in_specs=[VMEM_SPEC], out_specs=VMEM_SPEC, out_shape=SDS((N_CHIPS, r, n), F32),
        scratch_shapes=[pltpu.SemaphoreType.DMA((N_CHIPS - 1,))] * 2,
    )(packed)


def _all_reduce_small(packed, name):
    r, n = packed.shape

    def body(src, out, slots, send, recv):
        x, y, c, me = _place()
        idx = 2 * me + c
        slots[idx] = src[...]
        cps = []

        def peer(d):
            p = (idx + d) % N_DEV
            return (p // 4, (p // 2) % 2, p % 2)

        for d in range(1, N_DEV):
            cp = pltpu.make_async_remote_copy(src_ref=src, dst_ref=slots.at[idx], send_sem=send.at[d - 1], recv_sem=recv.at[d - 1],
                                              device_id=peer(d), device_id_type=MESH)
            cp.start()
            cps.append(cp)
        for d in range(1, N_DEV):
            pltpu.make_async_remote_copy(src_ref=src, dst_ref=slots.at[(idx - d) % N_DEV], send_sem=send.at[d - 1],
                                         recv_sem=recv.at[d - 1], device_id=peer(d), device_id_type=MESH).wait_recv()
        for cp in cps:
            cp.wait_send()
        acc = slots[0]
        for k in range(1, N_DEV):
            acc = acc + slots[k]
        out[...] = acc

    return pl.pallas_call(
        body, name=name, in_specs=[VMEM_SPEC], out_specs=VMEM_SPEC, out_shape=SDS((r, n), F32),
        scratch_shapes=[pltpu.VMEM((N_DEV, r, n), F32)] + [pltpu.SemaphoreType.DMA((N_DEV - 1,))] * 2,
    )(packed)


def _simple_stage(ins, out_shapes, aliases, n_copies, copies):
    def start(ins_, outs, sems):
        for cp in copies(ins_, outs, *sems):
            cp.start()

    def finish(ins_, outs, sems):
        for cp in copies(ins_, outs, *sems):
            cp.wait()

    return _Stage(ins=list(ins), out_shapes=list(out_shapes), aliases=aliases,
                  sems=[pltpu.SemaphoreType.DMA((n_copies,))] * 2, start=start, finish=finish)


def _rs_pair_exchange(grads, metas, rows_of):
    nw = len(grads)

    def copies(src, dst, send, recv):
        x, y, c, me = _place()
        return [pltpu.make_async_remote_copy(
            src_ref=_half(src[i], j, 1 - c, rows_of[i], metas[i][2]), dst_ref=dst[i].at[j], send_sem=send.at[i * N_CHIPS + j],
            recv_sem=recv.at[i * N_CHIPS + j], device_id=(x, y, 1 - c), device_id_type=MESH)
            for i in range(nw) for j in range(N_CHIPS)]

    out_shapes = [SDS((N_CHIPS, rows_of[i] // 2, g.shape[2]), g.dtype) for i, g in enumerate(grads)]
    return _simple_stage(grads, out_shapes, {}, nw * N_CHIPS, copies)


def _rs_pair_add(g, got, meta, rows, sp, name):
    per = meta[2]
    n = g.shape[2]
    hr = rows // 2
    tr = _tile(hr, max(16, (3 << 19) // n // 16 * 16), 16)

    def body(sp_ref, g_ref, got_ref, snd_ref, own_ref):
        j = pl.program_id(1)
        s = g_ref[...].astype(F32) + got_ref[...].astype(F32)
        snd_ref[...] = s.astype(BF16)

        @pl.when(j == sp_ref[1])
        def _():
            own_ref[...] = s

    grid_spec = pltpu.PrefetchScalarGridSpec(
        num_scalar_prefetch=1, grid=(hr // tr, N_CHIPS),
        in_specs=[pl.BlockSpec((None, tr, n), lambda i, j, sp: (j // per, ((j % per) * rows + sp[0] * hr) // tr + i, 0)),
                  pl.BlockSpec((None, tr, n), lambda i, j, sp: (j, i, 0))],
        out_specs=[pl.BlockSpec((None, tr, n), lambda i, j, sp: (j, i, 0)), pl.BlockSpec((tr, n), lambda i, j, sp: (i, 0))])
    return pl.pallas_call(
        body, name=name, grid_spec=grid_spec, out_shape=[SDS((N_CHIPS, hr, n), BF16), SDS((hr, n), F32)],
        compiler_params=_cparams("parallel", "arbitrary"),
    )(sp, g, got)


def _rs_chip_exchange(sends, part=(0, 1), prev=None):
    nw = len(sends)
    p, np_ = part

    def copies(src, dst, send, recv):
        x, y, c, me = _place()
        cps = []
        for i in range(nw):
            pr = sends[i].shape[1] // np_
            for r in range(1, N_CHIPS):
                cps.append(pltpu.make_async_remote_copy(
                    src_ref=src[i].at[(me + r) % N_CHIPS, pl.ds(p * pr, pr)], dst_ref=dst[i].at[r - 1, pl.ds(p * pr, pr)],
                    send_sem=send.at[i * (N_CHIPS - 1) + r - 1], recv_sem=recv.at[i * (N_CHIPS - 1) + r - 1],
                    device_id=_chip_dev((me + r) % N_CHIPS, c), device_id_type=MESH))
        return cps

    out_shapes = [SDS((N_CHIPS - 1,) + s.shape[1:], BF16) for s in sends]
    if prev is None:
        return _simple_stage(sends, out_shapes, {}, nw * (N_CHIPS - 1), copies)
    return _simple_stage(list(sends) + list(prev), out_shapes, {nw + i: i for i in range(nw)}, nw * (N_CHIPS - 1), copies)


def _rs_chip_add(own, got, sp, name):
    hr, n = own.shape
    tr = _tile(hr, max(16, (3 << 19) // n // 16 * 16), 16)

    def body(sp_ref, own_ref, got_ref, o_ref):
        acc = own_ref[...]
        for r in range(N_CHIPS - 1):
            acc = acc + got_ref[r].astype(F32)
        o_ref[...] = acc

    grid_spec = pltpu.PrefetchScalarGridSpec(
        num_scalar_prefetch=1, grid=(hr // tr,),
        in_specs=[pl.BlockSpec((tr, n), lambda i, sp: (i, 0)), pl.BlockSpec((N_CHIPS - 1, tr, n), lambda i, sp: (0, i, 0))],
        out_specs=pl.BlockSpec((tr, n), lambda i, sp: (sp[0] * (hr // tr) + i, 0)))
    return pl.pallas_call(body, name=name, grid_spec=grid_spec, out_shape=SDS((2 * hr, n), F32),
                          compiler_params=_cparams("parallel"))(sp, own, got)


def _rs_pair_share(blocks):
    nw = len(blocks)

    def copies(src, dst, send, recv):
        x, y, c, me = _place()
        cps = []
        for i in range(nw):
            hr = src[i].shape[0] // 2
            cps.append(pltpu.make_async_remote_copy(
                src_ref=src[i].at[pl.ds(c * hr, hr)], dst_ref=dst[i].at[pl.ds(c * hr, hr)], send_sem=send.at[i],
                recv_sem=recv.at[i], device_id=(x, y, 1 - c), device_id_type=MESH))
        return cps

    return _simple_stage(blocks, [SDS(b.shape, b.dtype) for b in blocks], {i: i for i in range(nw)}, nw, copies)


def kernel(x, p, ln_g, ln_b, ffn1_w_in, ffn1_w_out, mix_w_in, conv_w, hg_lower_bound, hg_norm_w, branch_w_conv, branch_w_hgrn, mix_w_out, ffn2_w_in, ffn2_w_out, ple_w_gate, ple_w_proj, loss_target, m_ln_g, m_ln_b, m_ffn1_w_in, m_ffn1_w_out, m_mix_w_in, m_conv_w, m_hg_lower_bound, m_hg_norm_w, m_branch_w_conv, m_branch_w_hgrn, m_mix_w_out, m_ffn2_w_in, m_ffn2_w_out, m_ple_w_gate, m_ple_w_proj, v_ln_g, v_ln_b, v_ffn1_w_in, v_ffn1_w_out, v_mix_w_in, v_conv_w, v_hg_lower_bound, v_hg_norm_w, v_branch_w_conv, v_branch_w_hgrn, v_mix_w_out, v_ffn2_w_in, v_ffn2_w_out, v_ple_w_gate, v_ple_w_proj):
    assert ln_g.shape[0] == DEPTH and x.shape[0] == 1 and p.shape[:2] == (1, 1)
    t, d = x.shape[1], x.shape[2]
    w = d // 2
    x0 = x.reshape(t, d)
    x0b = _to_bf16(x0, "x_bf16")
    pe = p.reshape(t, p.shape[-1])
    target = loss_target.reshape(t, d)
    cx, cy, cc = lax.axis_index("x"), lax.axis_index("y"), lax.axis_index("c")
    chip = 2 * cx + cy
    sp = jnp.stack([cc, chip]).astype(jnp.int32)

    big = dict(ffn1_w_in=ffn1_w_in, ffn1_w_out=ffn1_w_out, mix_w_in=mix_w_in, branch_w_conv=branch_w_conv,
               branch_w_hgrn=branch_w_hgrn, mix_w_out=mix_w_out, ffn2_w_in=ffn2_w_in, ffn2_w_out=ffn2_w_out,
               ple_w_gate=ple_w_gate, ple_w_proj=ple_w_proj)
    moments = dict(ffn1_w_in=(m_ffn1_w_in, v_ffn1_w_in), ffn1_w_out=(m_ffn1_w_out, v_ffn1_w_out), mix_w_in=(m_mix_w_in, v_mix_w_in),
                   branch_w_conv=(m_branch_w_conv, v_branch_w_conv), branch_w_hgrn=(m_branch_w_hgrn, v_branch_w_hgrn),
                   mix_w_out=(m_mix_w_out, v_mix_w_out), ffn2_w_in=(m_ffn2_w_in, v_ffn2_w_in), ffn2_w_out=(m_ffn2_w_out, v_ffn2_w_out),
                   ple_w_gate=(m_ple_w_gate, v_ple_w_gate), ple_w_proj=(m_ple_w_proj, v_ple_w_proj))
    names = list(big)

    n_loc = ffn1_w_in.shape[-1]
    n_pad = -(-n_loc // LANES) * LANES
    assert mix_w_in.shape[-1] % LANES == 0 and ffn1_w_out.shape[1] * 2 == n_loc
    pad_cols = dict(ffn1_w_in=n_pad, ffn2_w_in=n_pad)
    meta = {k: (N_CHIPS, big[k].shape[1], 1) for k in names}
    meta["ffn1_w_out"] = meta["ffn2_w_out"] = (2, n_pad, 2)
    rows = {k: big[k].shape[1] for k in names}
    swap = lambda a: jnp.transpose(a, (0, 2, 1))
    wbuf = {}
    zero_pad = jnp.zeros((max(n_pad - n_loc, 16), d), BF16)

    def cast(k, comm=()):
        if k in pad_cols:
            return _cast_pad_t(swap(big[k]), pad_cols[k], meta[k], sp, "cast_" + k, comm=comm)
        return _cast_pad(big[k], big[k].shape[2], meta[k], sp, "cast_" + k, comm=comm)

    def gather(ici=(), fwd=()):
        ks = list(dict.fromkeys([k for k, _, _ in ici] + [k for k, _, _ in fwd]))
        ip = {k: (p_, n_) for k, p_, n_ in ici}
        fp = {k: (p_, n_) for k, p_, n_ in fwd}
        return _gather_stage([wbuf[k] for k in ks], [meta[k] for k in ks], [rows[k] for k in ks], [ip.get(k) for k in ks],
                             [fp.get(k) for k in ks], zero_pad), ks

    def gathered(ks, outs):
        wbuf.update(zip(ks, outs))

    def w3(k):
        return wbuf[k]

    def w2(k):
        return wbuf[k].reshape(-1, wbuf[k].shape[2])

    dq, wq = d // N_CHIPS, w // N_CHIPS
    small = jnp.concatenate([ln_g[0], ln_b[0], jnp.pad(conv_w[0], ((0, 5), (0, dq - wq)))], axis=0)
    small = _gather_small(small, "gather_small")
    lng = small[:, 0:4, :].transpose(1, 0, 2).reshape(4, 1, d)
    lnb = small[:, 4:8, :].transpose(1, 0, 2).reshape(4, 1, d)
    cw = small[:, 8:11, :wq].transpose(1, 0, 2).reshape(3, w)
    hg = hg_lower_bound
    nw_ = hg_norm_w

    one = lambda *ks_: [(k, 0, 1) for k in ks_]
    wbuf["ffn1_w_in"] = cast("ffn1_w_in")
    carriers = ["ple_w_proj", "ffn1_w_out", "mix_w_in", None] + [k for k in names if k not in ("ffn1_w_in", "ple_w_proj", "ffn1_w_out", "mix_w_in")]
    assert len(carriers) > FIRST_GATHER_PARTS
    for step, k in enumerate(carriers):
        ici = [("ffn1_w_in", step, FIRST_GATHER_PARTS)] if step < FIRST_GATHER_PARTS else []
        fwd = [("ffn1_w_in", step - 1, FIRST_GATHER_PARTS)] if 1 <= step <= FIRST_GATHER_PARTS else []
        ici += one("ple_w_proj") if step == 1 else []
        fwd += one("ple_w_proj") if step == 2 else []
        if ici or fwd:
            st, ks = gather(ici, fwd)
            if k is None:
                pp, got = _mm(pe, w3("ple_w_proj"), name="ple_proj", b_blocked=True, tn=512, comm=[st])
            else:
                wbuf[k], got = cast(k, comm=[st])
            gathered(ks, got)
        else:
            wbuf[k] = cast(k)
    st, ks = gather(ici=one("ffn1_w_out") + [("mix_w_in", 0, 2)])
    z1, got = _mm(x0b, w3("ffn1_w_in"), name="ffn1_in", b_blocked=True, out_dtype=BF16, tm=1024, comm=[st])
    gathered(ks, got)
    st, ks = gather(fwd=one("ffn1_w_out") + [("mix_w_in", 0, 2)])
    h1, got = _swiglu_fwd(z1, "ffn1_act", comm=[st])
    gathered(ks, got)
    st, ks = gather(ici=[("mix_w_in", 1, 2)])
    y1, got = _mm(h1, w2("ffn1_w_out"), name="ffn1_out", tm=1024, tn=1024, tk=2816, comm=[st])
    gathered(ks, got)
    st, ks = gather(fwd=[("mix_w_in", 1, 2)])
    (r1, x1b), got = _ln_fwd(x0, y1, lng[0], lnb[0], None, None, 0.5, "ln0", comm=[st])
    gathered(ks, got)
    mixo_w = one("branch_w_conv", "branch_w_hgrn", "mix_w_out")
    st, ks = gather(ici=mixo_w + [("ffn2_w_in", 0, 2)])
    z, got = _mm(x1b, w3("mix_w_in"), name="mix_in", b_blocked=True, tm=1024, comm=[st])
    gathered(ks, got)
    ya = _conv_fwd(z, cw, w, "conv_fwd")
    st, ks = gather(ici=[("ffn2_w_in", 1, 2)], fwd=mixo_w + [("ffn2_w_in", 0, 2)])
    (yb, o_h, states), got = _hgrn_fwd(z, hg, nw_, w, "hgrn_fwd", comm=[st])
    gathered(ks, got)
    ma = _mm(ya, w3("branch_w_conv"), name="branch_conv", b_blocked=True, tn=512)
    mb = _mm(yb, w3("branch_w_hgrn"), name="branch_hgrn", b_blocked=True, tn=512)
    merged = _merge_fwd(z, ma, mb, w, "merge_fwd")
    st, ks = gather(fwd=[("ffn2_w_in", 1, 2)])
    y2, got = _mm(merged, w2("mix_w_out"), name="mix_out", tn=1024, comm=[st])
    gathered(ks, got)
    r2, x2b = _ln_fwd(r1, y2, lng[1], lnb[1], lng[0], lnb[0], 1.0, "ln1")
    late = one("ffn2_w_out", "ple_w_gate")
    st, ks = gather(ici=late)
    z3, got = _mm(x2b, w3("ffn2_w_in"), name="ffn2_in", b_blocked=True, out_dtype=BF16, tm=1024, comm=[st])
    gathered(ks, got)
    st, ks = gather(fwd=late)
    h3, got = _swiglu_fwd(z3, "ffn2_act", comm=[st])
    gathered(ks, got)
    y3 = _mm(h3, w2("ffn2_w_out"), name="ffn2_out", tm=1024, tn=1024, tk=2816)
    r3, x3b = _ln_fwd(r2, y3, lng[2], lnb[2], lng[1], lnb[1], 0.5, "ln2")
    gp = _mm(x3b, w2("ple_w_gate"), name="ple_gate", tn=1024)
    dr4, dgp, dpp, dg3, db3, sq = _tail(r3, lng[2], lnb[2], gp, pp, lng[3], lnb[3], target, "tail")

    grads, sends, owns, blocks, outs = {}, {}, {}, {}, {}

    def pair_exchange(*ks):
        return _rs_pair_exchange([grads[k] for k in ks], [meta[k] for k in ks], [rows[k] for k in ks])

    def pair_add(ks, got):
        for k, g_ in zip(ks, got):
            sends[k], owns[k] = _rs_pair_add(grads[k], g_, meta[k], rows[k], sp, "rs_pair_add_" + k)

    def chip_exchange(*ks):
        return _rs_chip_exchange([sends[k] for k in ks])

    def chip_add(ks, got):
        for k, g_ in zip(ks, got):
            blocks[k] = _rs_chip_add(owns[k], g_, sp, "rs_chip_add_" + k)

    def pair_share(*ks):
        return _rs_pair_share([blocks[k] for k in ks])

    def update(ks, full):
        for k, g_ in zip(ks, full):
            m_, v_ = moments[k]
            if k in pad_cols:
                outs[k] = [swap(a) for a in _adamw_t(swap(big[k]), g_, swap(m_), swap(v_), "adamw_" + k)]
            else:
                outs[k] = _adamw(big[k], g_, m_, v_, "adamw_" + k)

    ple = ("ple_w_gate", "ple_w_proj")
    mixo = ("mix_w_out", "branch_w_conv", "branch_w_hgrn")
    dx3m = _mm(dgp, w2("ple_w_gate"), name="d_ple_gate_x", tb=True, tn=1024, tk=2048)
    grads["ple_w_gate"] = _mm(x3b, dgp, name="d_ple_gate_w", ta=True, out_dtype=BF16, tm=1024, tk=2048, tn=1024).reshape(N_CHIPS, -1, d)
    grads["ple_w_proj"] = _mm(pe, dpp, name="d_ple_proj_w", ta=True, out_dtype=BF16, out_blocked=N_CHIPS, tk=2048, tn=512)
    dr3, dy3b, dg2, db2 = _ln_bwd(dr4, dx3m, r3, lng[2], 0.5, "ln2_bwd")
    late_w = ple + ("ffn2_w_out",)
    dh3 = _mm(dy3b, w2("ffn2_w_out"), name="d_ffn2_out_x", tb=True, out_dtype=BF16, tn=1408, tk=2048)
    grads["ffn2_w_out"] = _mm(h3, dy3b, name="d_ffn2_out_w", ta=True, out_dtype=BF16, tm=1408, tk=2048, tn=1024).reshape(2, n_pad, d)
    dz3 = _swiglu_bwd(dh3, z3, "ffn2_act_bwd")
    dx2m, got = _mm(dz3, w3("ffn2_w_in"), name="d_ffn2_in_x", tb=True, b_blocked=True, tm=1024, tn=1024, tk=2816,
                    comm=[pair_exchange(*late_w)])
    pair_add(late_w, got)
    grads["ffn2_w_in"], got = _mm(x2b, dz3, name="d_ffn2_in_w", ta=True, out_dtype=BF16, out_blocked=N_CHIPS, tk=4096, comm=[chip_exchange(*late_w)])
    chip_add(late_w, got)
    dr2, dy2b, dg1, db1 = _ln_bwd(dr3, dx2m, r2, lng[1], 1.0, "ln1_bwd")
    dmer, got = _mm(dy2b, w2("mix_w_out"), name="d_mix_out_x", tb=True, tn=1024, tk=2048, comm=[pair_exchange("ffn2_w_in")])
    pair_add(["ffn2_w_in"], got)
    g_, full = _mm(merged, dy2b, name="d_mix_out_w", ta=True, out_dtype=BF16, tm=1024, tk=2048, tn=1024, comm=[pair_share(*late_w)])
    grads["mix_w_out"] = g_.reshape(N_CHIPS, -1, d)
    update(late_w, full)
    dma, dmb, dgc, dgh = _merge_bwd(dmer, z, ma, mb, w, "merge_bwd")
    dya = _mm(dma, w3("branch_w_conv"), name="d_branch_conv_x", tb=True, b_blocked=True, tn=1024, tk=512)
    dyb = _mm(dmb, w3("branch_w_hgrn"), name="d_branch_hgrn_x", tb=True, b_blocked=True, tn=1024, tk=512)
    grads["branch_w_conv"] = _mm(ya, dma, name="d_branch_conv_w", ta=True, out_dtype=BF16, out_blocked=N_CHIPS, tm=1024, tk=2048, tn=512)
    grads["branch_w_hgrn"] = _mm(yb, dmb, name="d_branch_hgrn_w", ta=True, out_dtype=BF16, out_blocked=N_CHIPS, tm=1024, tk=2048, tn=512)
    dbg, dcg, dhc, dcw = _conv_bwd(dya, z, cw, w, "conv_bwd")
    (dq_, df_, di_, dgr_, dhg, dnw), got2, got = _hgrn_bwd(dyb, z, o_h, states, hg, nw_, w, "hgrn_bwd",
                                                            comm=[chip_exchange("ffn2_w_in"), pair_exchange(*mixo)])
    chip_add(["ffn2_w_in"], got2)
    pair_add(mixo, got)
    dz = _concat_cols([dbg, dcg, dhc, dq_, df_, di_, dgr_, dgc, dgh], "dz_concat")
    dx1m, full, got = _mm(dz, w3("mix_w_in"), name="d_mix_in_x", tb=True, b_blocked=True, tm=1024, tn=1024, tk=2816,
                          comm=[pair_share("ffn2_w_in"), chip_exchange(*mixo)])
    update(["ffn2_w_in"], full)
    chip_add(mixo, got)
    grads["mix_w_in"], full = _mm(x1b, dz, name="d_mix_in_w", ta=True, out_dtype=BF16, out_blocked=N_CHIPS, tk=4096, comm=[pair_share(*mixo)])
    update(mixo, full)
    dr1, dy1b, dg0, db0 = _ln_bwd(dr2, dx1m, r1, lng[0], 0.5, "ln0_bwd")
    dh1, got = _mm(dy1b, w2("ffn1_w_out"), name="d_ffn1_out_x", tb=True, out_dtype=BF16, tn=1408, tk=2048,
                   comm=[pair_exchange("mix_w_in")])
    pair_add(["mix_w_in"], got)
    mix_sends = [sends["mix_w_in"]]
    g_, got_a = _mm(h1, dy1b, name="d_ffn1_out_w", ta=True, out_dtype=BF16, tm=1408, tk=2048, tn=1024, comm=[_rs_chip_exchange(mix_sends, (0, 2))])
    grads["ffn1_w_out"] = g_.reshape(2, n_pad, d)
    dz1 = _swiglu_bwd(dh1, z1, "ffn1_act_bwd")
    g_other, got2, got = _mm(x0b, dz1, name="d_ffn1_in_w_other", ta=True, out_dtype=BF16, out_blocked=N_CHIPS, tk=4096, half=(sp, True),
                             comm=[_rs_chip_exchange(mix_sends, (1, 2), got_a), pair_exchange("ffn1_w_out")])
    chip_add(["mix_w_in"], got2)
    pair_add(["ffn1_w_out"], got)
    grads["ffn1_w_in"], full, got2, got = _mm(
        x0b, dz1, name="d_ffn1_in_w_own", ta=True, out_dtype=BF16, out_blocked=N_CHIPS, tk=4096, half=(sp, False),
        comm=[pair_share("mix_w_in"), chip_exchange("ffn1_w_out"),
              _rs_pair_exchange([g_other], [meta["ffn1_w_in"]], [rows["ffn1_w_in"]])])
    update(["mix_w_in"], full)
    chip_add(["ffn1_w_out"], got2)
    pair_add(["ffn1_w_in"], got)
    dx0, got2, full = _mm(dz1, w3("ffn1_w_in"), name="d_ffn1_in_x", tb=True, b_blocked=True, tm=1024, tn=1024, tk=2816,
                          add=(dr1, ALPHA), comm=[chip_exchange("ffn1_w_in"), pair_share("ffn1_w_out")])
    chip_add(["ffn1_w_in"], got2)
    update(["ffn1_w_out"], full)
    grad_x = dx0.reshape(x.shape)
    update(["ffn1_w_in"], _run_stages([pair_share("ffn1_w_in")], "rs_tail_pair")[0])

    pack = jnp.concatenate([
        dg0, dg1, dg2, dg3, db0, db1, db2, db3,
        jnp.pad(dcw, ((0, 0), (0, d - w))), jnp.pad(dhg, ((0, 0), (0, d - w))),
        jnp.pad(jnp.sum(dnw.reshape(-1, HEAD), axis=0, keepdims=True), ((0, 0), (0, d - HEAD))), sq], axis=0)
    pack = _all_reduce_small(jnp.pad(pack, ((0, 1), (0, 0))), "reduce_small")
    loss = (0.5 / d) * jnp.sum(pack[14])
    g_ln_g = lax.dynamic_slice_in_dim(pack[0:4], chip * dq, dq, axis=1)
    g_ln_b = lax.dynamic_slice_in_dim(pack[4:8], chip * dq, dq, axis=1)
    g_conv = lax.dynamic_slice_in_dim(pack[8:11, :w], chip * wq, wq, axis=1)
    g_hg = pack[11:13, :w]
    g_nw = pack[13:14, :HEAD]

    small_w = dict(ln_g=(ln_g, g_ln_g, m_ln_g, v_ln_g), ln_b=(ln_b, g_ln_b, m_ln_b, v_ln_b),
                   conv_w=(conv_w, g_conv, m_conv_w, v_conv_w), hg_lower_bound=(hg_lower_bound, g_hg, m_hg_lower_bound, v_hg_lower_bound),
                   hg_norm_w=(hg_norm_w, g_nw, m_hg_norm_w, v_hg_norm_w))
    for k, (w_, g_, m_, v_) in small_w.items():
        outs[k] = _adamw(w_, g_.reshape(-1, w_.shape[-1]), m_, v_, "adamw_" + k)

    order = ["ln_g", "ln_b", "ffn1_w_in", "ffn1_w_out", "mix_w_in", "conv_w", "hg_lower_bound", "hg_norm_w", "branch_w_conv",
             "branch_w_hgrn", "mix_w_out", "ffn2_w_in", "ffn2_w_out", "ple_w_gate", "ple_w_proj"]
    return (loss, grad_x, *[outs[k][0] for k in order], *[outs[k][1] for k in order], *[outs[k][2] for k in order],
            *[outs[k][3] for k in order])
```

```python
import collections
import functools

import jax
import jax.numpy as jnp
from jax import lax
from jax.experimental import pallas as pl
from jax.experimental.pallas import tpu as pltpu

F32 = jnp.float32
BF16 = jnp.bfloat16
MESH = pl.DeviceIdType.MESH
ANY = pl.BlockSpec(memory_space=pl.ANY)
VMEM_SPEC = pl.BlockSpec(memory_space=pltpu.VMEM)
SDS = jax.ShapeDtypeStruct

DEPTH = 1
ALPHA = (2.0 * DEPTH) ** 0.25
LN_EPS = 1e-5
RMS_EPS = 1e-6
CHUNK = 32
HEAD = 128
ADAM_LR, ADAM_B1, ADAM_B2, ADAM_EPS, ADAM_WD, ADAM_STEP = 0.001, 0.9, 0.999, 1e-08, 0.01, 10

LANES = 128
N_CHIPS = 4
N_DEV = 8
FIRST_GATHER_PARTS = 8
VMEM_LIMIT = 52 * 1024 * 1024
MM_PIECE = 512


def _cparams(*sem):
    if sem:
        return pltpu.CompilerParams(dimension_semantics=sem, vmem_limit_bytes=VMEM_LIMIT)
    return pltpu.CompilerParams(vmem_limit_bytes=VMEM_LIMIT)


def _tile(n, target, mult):
    best = None
    for t in range(mult, min(n, target) + 1, mult):
        if n % t == 0:
            best = t
    return best if best is not None else n


def _sigmoid(x):
    return 1.0 / (1.0 + jnp.exp(-x))


_Stage = collections.namedtuple("_Stage", "ins out_shapes aliases sems start finish")


def _hosted_call(compute, stages, *, name, grid, in_specs, out_specs, out_shape, scratch_shapes, operands, parallel,
                 prefetch=None):
    n_cmp, n_out, n_scr = len(in_specs), len(out_specs), len(scratch_shapes)
    n_in = n_cmp
    n_pre = int(prefetch is not None)
    c_in = [len(s.ins) for s in stages]
    c_out = [len(s.out_shapes) for s in stages]
    c_sem = [len(s.sems) for s in stages]
    aliases = {}
    for si, s in enumerate(stages):
        for a_in, a_out in s.aliases.items():
            aliases[n_pre + n_in + sum(c_in[:si]) + a_in] = n_out + sum(c_out[:si]) + a_out

    def body(*refs):
        refs = refs[n_pre:]
        ins = refs[:n_cmp]
        cins = refs[n_in:n_in + sum(c_in)]
        outs = refs[n_in + sum(c_in):n_in + sum(c_in) + n_out]
        couts = refs[n_in + sum(c_in) + n_out:n_in + sum(c_in) + n_out + sum(c_out)]
        scr = refs[n_in + sum(c_in) + n_out + sum(c_out):][:n_scr]
        sems = refs[n_in + sum(c_in) + n_out + sum(c_out) + n_scr:]

        def stage_refs(si):
            return (cins[sum(c_in[:si]):sum(c_in[:si + 1])], couts[sum(c_out[:si]):sum(c_out[:si + 1])],
                    sems[sum(c_sem[:si]):sum(c_sem[:si + 1])])

        if stages:
            first = functools.reduce(jnp.logical_and, [pl.program_id(ax) == 0 for ax in range(len(grid))])
            last = functools.reduce(jnp.logical_and, [pl.program_id(ax) == grid[ax] - 1 for ax in range(len(grid))])

            @pl.when(first)
            def _():
                for si, s in enumerate(stages):
                    s.start(*stage_refs(si))

        compute(*ins, *outs, *scr)
        if stages:
            @pl.when(last)
            def _():
                for si, s in enumerate(stages):
                    s.finish(*stage_refs(si))

    sem = ("arbitrary",) * len(grid) if stages else ("parallel",) * parallel + ("arbitrary",) * (len(grid) - parallel)
    all_in = list(in_specs) + [ANY] * (n_in - n_cmp + sum(c_in))
    all_out = list(out_specs) + [ANY] * sum(c_out)
    all_scr = list(scratch_shapes) + [q for s in stages for q in s.sems]
    all_shape = list(out_shape) + [o for s in stages for o in s.out_shapes]
    args = list(operands) + [a for s in stages for a in s.ins]
    if prefetch is None:
        res = pl.pallas_call(body, name=name, grid=grid, in_specs=all_in, out_specs=all_out, out_shape=all_shape,
                             input_output_aliases=aliases, scratch_shapes=all_scr, compiler_params=_cparams(*sem))(*args)
    else:
        grid_spec = pltpu.PrefetchScalarGridSpec(num_scalar_prefetch=1, grid=grid, in_specs=all_in, out_specs=all_out,
                                                 scratch_shapes=all_scr)
        res = pl.pallas_call(body, name=name, grid_spec=grid_spec, out_shape=all_shape, input_output_aliases=aliases,
                             compiler_params=_cparams(*sem))(prefetch, *args)
    main = res[0] if n_out == 1 else list(res[:n_out])
    if not stages:
        return main
    rest = res[n_out:]
    return (main, *[list(rest[sum(c_out[:si]):sum(c_out[:si + 1])]) for si in range(len(stages))])


def _run_stages(stages, name):
    def body(*refs):
        n_i = sum(len(s.ins) for s in stages)
        n_o = sum(len(s.out_shapes) for s in stages)
        cins, couts, sems = refs[:n_i], refs[n_i:n_i + n_o], refs[n_i + n_o:]
        pos = [0, 0, 0]
        parts = []
        for s in stages:
            parts.append((cins[pos[0]:pos[0] + len(s.ins)], couts[pos[1]:pos[1] + len(s.out_shapes)], sems[pos[2]:pos[2] + len(s.sems)]))
            pos = [pos[0] + len(s.ins), pos[1] + len(s.out_shapes), pos[2] + len(s.sems)]
        for s, p_ in zip(stages, parts):
            s.start(*p_)
        for s, p_ in zip(stages, parts):
            s.finish(*p_)

    aliases, ni, no = {}, 0, 0
    for s in stages:
        for a_in, a_out in s.aliases.items():
            aliases[ni + a_in] = no + a_out
        ni, no = ni + len(s.ins), no + len(s.out_shapes)
    res = pl.pallas_call(
        body, name=name, in_specs=[ANY] * ni, out_specs=[ANY] * no, out_shape=[o for s in stages for o in s.out_shapes],
        input_output_aliases=aliases, scratch_shapes=[q for s in stages for q in s.sems],
    )(*[a for s in stages for a in s.ins])
    out, pos = [], 0
    for s in stages:
        out.append(list(res[pos:pos + len(s.out_shapes)]))
        pos += len(s.out_shapes)
    return out


def _mm(a, b, *, name, ta=False, tb=False, b_blocked=False, out_blocked=0, out_dtype=F32,
        tm=512, tn=1408, tk=2048, comm=(), half=None, add=None):
    if ta:
        kd, m = a.shape
    else:
        m, kd = a.shape
    if b_blocked and not tb:
        g, kb, nb = b.shape
        assert kb == kd
        n = g * nb
        tn = _tile(nb, tn, LANES)
        tk = _tile(kd, tk, LANES)
        per_n = nb // tn
        b_spec = pl.BlockSpec((None, tk, tn), lambda i, j, k, *s: (j // per_n, k, j % per_n))
    elif b_blocked and tb:
        g, n, kb = b.shape
        assert g * kb == kd
        tn = _tile(n, tn, LANES)
        tk = _tile(kb, tk, LANES)
        per_k = kb // tk
        b_spec = pl.BlockSpec((None, tn, tk), lambda i, j, k, *s: (k // per_k, j, k % per_k))
    elif tb:
        n, kb = b.shape
        assert kb == kd
        tn = _tile(n, tn, LANES)
        tk = _tile(kd, tk, LANES)
        b_spec = pl.BlockSpec((tn, tk), lambda i, j, k, *s: (j, k))
    else:
        kb, n = b.shape
        assert kb == kd
        tn = _tile(n // out_blocked if out_blocked else n, tn, LANES)
        per_o = (n // out_blocked) // tn if out_blocked else None
        tk = _tile(kd, tk, LANES)
        b_spec = pl.BlockSpec((tk, tn), lambda i, j, k, *s: (k, j))
    m_run = m // 2 if half else m
    tm = _tile(m_run, tm, LANES if ta else 8)

    def row(i, s):
        if not half:
            return i
        h = 1 - s[0][0] if half[1] else s[0][0]
        return h * (m_run // tm) + i

    if ta:
        a_spec = pl.BlockSpec((tk, tm), lambda i, j, k, *s: (k, row(i, s)))
    else:
        a_spec = pl.BlockSpec((tm, tk), lambda i, j, k, *s: (row(i, s), k))
    if out_blocked:
        assert not b_blocked and not tb
        o_spec = pl.BlockSpec((None, tm, tn), lambda i, j, k, *s: (j // per_o, row(i, s), j % per_o))
        o_shape = SDS((out_blocked, m, n // out_blocked), out_dtype)
    else:
        o_spec = pl.BlockSpec((tm, tn), lambda i, j, k, *s: (row(i, s), j))
        o_shape = SDS((m, n), out_dtype)
    nk = kd // tk
    dn = (((0 if ta else 1,), (1 if tb else 0,)), ((), ()))
    grid = (m_run // tm, n // tn, nk)

    pieces = [(lo, min(MM_PIECE, tn - lo)) for lo in range(0, tn, MM_PIECE)]

    def compute(a_ref, b_ref, *rest):
        add_ref = rest[0] if add else None
        o_ref, acc_ref = rest[-2:]
        a_tile = a_ref[...].astype(BF16)
        k = pl.program_id(2)

        def result(acc, cols):
            if add:
                acc = acc + add[1] * add_ref[:, cols]
            return acc.astype(o_ref.dtype)

        if nk > 1:
            @pl.when(k == 0)
            def _():
                acc_ref[...] = jnp.zeros_like(acc_ref)

        for lo, wd in pieces:
            cols = slice(lo, lo + wd)
            b_tile = b_ref[cols, :] if tb else b_ref[:, cols]
            part = lax.dot_general(a_tile, b_tile.astype(BF16), dn, preferred_element_type=F32)
            if nk == 1:
                o_ref[:, cols] = result(part, cols)
            else:
                acc_ref[:, cols] += part

        if nk > 1:
            @pl.when(k == nk - 1)
            def _():
                o_ref[...] = result(acc_ref[...], slice(None))

    extra = [(add[0], o_spec)] if add else []
    return _hosted_call(compute, comm, name=name, grid=grid, in_specs=[a_spec, b_spec] + [s_ for _, s_ in extra], out_specs=[o_spec],
                        out_shape=[o_shape], scratch_shapes=[pltpu.VMEM((tm, tn), F32)], operands=(a, b, *[a_ for a_, _ in extra]),
                        parallel=2, prefetch=half[0] if half else None)


def _swiglu_fwd(z, name, comm=()):
    t, n = z.shape
    n2 = n // 2
    tr = _tile(t, 128, 16)

    def body(a_ref, u_ref, o_ref):
        a = a_ref[...].astype(F32)
        o_ref[...] = (a * _sigmoid(a) * u_ref[...].astype(F32)).astype(o_ref.dtype)

    return _hosted_call(
        body, comm, name=name, grid=(t // tr,),
        in_specs=[pl.BlockSpec((tr, n2), lambda i: (i, 0)), pl.BlockSpec((tr, n2), lambda i: (i, 1))],
        out_specs=[pl.BlockSpec((tr, n2), lambda i: (i, 0))], out_shape=[SDS((t, n2), BF16)], scratch_shapes=[],
        operands=(z, z), parallel=1)


def _swiglu_bwd(dh, z, name):
    t, n = z.shape
    n2 = n // 2
    tr = _tile(t, 128, 16)

    def body(dh_ref, a_ref, u_ref, o_ref):
        a = a_ref[...].astype(F32)
        dh_ = dh_ref[...].astype(F32)
        s = _sigmoid(a)
        o_ref[:, 0:n2] = (dh_ * u_ref[...].astype(F32) * (s * (1.0 + a * (1.0 - s)))).astype(o_ref.dtype)
        o_ref[:, n2:n] = (dh_ * a * s).astype(o_ref.dtype)

    return pl.pallas_call(
        body, name=name, grid=(t // tr,),
        in_specs=[pl.BlockSpec((tr, n2), lambda i: (i, 0)), pl.BlockSpec((tr, n2), lambda i: (i, 0)),
                  pl.BlockSpec((tr, n2), lambda i: (i, 1))],
        out_specs=pl.BlockSpec((tr, n), lambda i: (i, 0)), out_shape=SDS((t, n), BF16),
        compiler_params=_cparams("parallel"),
    )(dh, z, z)


def _ln_stats(r):
    mu = jnp.mean(r, axis=-1, keepdims=True)
    xc = r - mu
    var = jnp.mean(xc * xc, axis=-1, keepdims=True)
    return xc * lax.rsqrt(var + LN_EPS)


def _ln_fwd(xp, y, g, b, gp, bp, scale, name, comm=()):
    t, d = xp.shape
    tr = _tile(t, 256, 16)

    def body(xp_ref, y_ref, g_ref, b_ref, *rest):
        r_ref, xb_ref = rest[-2:]
        x_prev = xp_ref[...]
        if gp is not None:
            x_prev = _ln_stats(x_prev) * rest[0][...] + rest[1][...]
        r = ALPHA * x_prev + scale * y_ref[...]
        r_ref[...] = r
        xb_ref[...] = (_ln_stats(r) * g_ref[...] + b_ref[...]).astype(BF16)

    row = pl.BlockSpec((tr, d), lambda i: (i, 0))
    vec = pl.BlockSpec((1, d), lambda i: (0, 0))
    prev = [] if gp is None else [gp, bp]
    return _hosted_call(
        body, comm, name=name, grid=(t // tr,), in_specs=[row, row, vec, vec] + [vec] * len(prev), out_specs=[row, row],
        out_shape=[SDS((t, d), F32), SDS((t, d), BF16)], scratch_shapes=[], operands=(xp, y, g, b, *prev), parallel=1)


def _ln_bwd(dra, dxm, r, g, scale, name):
    t, d = r.shape
    tr = _tile(t, 256, 16)

    def body(dra_ref, dxm_ref, r_ref, g_ref, dr_ref, dyb_ref, dg_ref, db_ref):
        i = pl.program_id(0)
        dx = ALPHA * dra_ref[...] + dxm_ref[...]
        rr = r_ref[...]
        mu = jnp.mean(rr, axis=-1, keepdims=True)
        xc = rr - mu
        rstd = lax.rsqrt(jnp.mean(xc * xc, axis=-1, keepdims=True) + LN_EPS)
        xh = xc * rstd
        dxh = dx * g_ref[...]
        dr = rstd * (dxh - jnp.mean(dxh, axis=-1, keepdims=True) - xh * jnp.mean(dxh * xh, axis=-1, keepdims=True))
        dr_ref[...] = dr
        dyb_ref[...] = (scale * dr).astype(BF16)
        dg = jnp.sum(dx * xh, axis=0, keepdims=True)
        db = jnp.sum(dx, axis=0, keepdims=True)

        @pl.when(i == 0)
        def _():
            dg_ref[...] = dg
            db_ref[...] = db

        @pl.when(i > 0)
        def _():
            dg_ref[...] += dg
            db_ref[...] += db

    row = pl.BlockSpec((tr, d), lambda i: (i, 0))
    vec = pl.BlockSpec((1, d), lambda i: (0, 0))
    return pl.pallas_call(
        body, name=name, grid=(t // tr,), in_specs=[row, row, row, vec], out_specs=[row, row, vec, vec],
        out_shape=[SDS((t, d), F32), SDS((t, d), BF16), SDS((1, d), F32), SDS((1, d), F32)],
        compiler_params=_cparams("arbitrary"),
    )(dra, dxm, r, g)


def _tail(r3, g3, b3, gp, pp, g, b, target, name):
    t, d = r3.shape
    tr = _tile(t, 256, 16)

    def body(r3_ref, g3_ref, b3_ref, gp_ref, pp_ref, g_ref, b_ref, tg_ref, dr_ref, dgp_ref, dpp_ref, dg_ref, db_ref, sq_ref):
        i = pl.program_id(0)
        gate = _sigmoid(gp_ref[...].astype(F32))
        pp_ = pp_ref[...].astype(F32)
        r = ALPHA * (_ln_stats(r3_ref[...]) * g3_ref[...] + b3_ref[...]) + gate * pp_
        mu = jnp.mean(r, axis=-1, keepdims=True)
        xc = r - mu
        rstd = lax.rsqrt(jnp.mean(xc * xc, axis=-1, keepdims=True) + LN_EPS)
        xh = xc * rstd
        err = xh * g_ref[...] + b_ref[...] - tg_ref[...]
        dx = err * (1.0 / d)
        dxh = dx * g_ref[...]
        dr = rstd * (dxh - jnp.mean(dxh, axis=-1, keepdims=True) - xh * jnp.mean(dxh * xh, axis=-1, keepdims=True))
        dr_ref[...] = dr
        dgp_ref[...] = (dr * pp_ * gate * (1.0 - gate)).astype(BF16)
        dpp_ref[...] = (dr * gate).astype(BF16)
        dg = jnp.sum(dx * xh, axis=0, keepdims=True)
        db = jnp.sum(dx, axis=0, keepdims=True)
        sq = jnp.sum(err * err, axis=0, keepdims=True)

        @pl.when(i == 0)
        def _():
            dg_ref[...] = dg
            db_ref[...] = db
            sq_ref[...] = sq

        @pl.when(i > 0)
        def _():
            dg_ref[...] += dg
            db_ref[...] += db
            sq_ref[...] += sq

    row = pl.BlockSpec((tr, d), lambda i: (i, 0))
    vec = pl.BlockSpec((1, d), lambda i: (0, 0))
    return pl.pallas_call(
        body, name=name, grid=(t // tr,), in_specs=[row, vec, vec, row, row, vec, vec, row],
        out_specs=[row, row, row, vec, vec, vec],
        out_shape=[SDS((t, d), F32), SDS((t, d), BF16), SDS((t, d), BF16), SDS((1, d), F32), SDS((1, d), F32),
                   SDS((1, d), F32)],
        compiler_params=_cparams("arbitrary"),
    )(r3, g3, b3, gp, pp, g, b, target)


def _to_bf16(x, name):
    t, d = x.shape
    tr = _tile(t, 512, 16)
    row = pl.BlockSpec((tr, d), lambda i: (i, 0))

    def body(x_ref, o_ref):
        o_ref[...] = x_ref[...].astype(BF16)

    return pl.pallas_call(body, name=name, grid=(t // tr,), in_specs=[row], out_specs=row, out_shape=SDS((t, d), BF16),
                          compiler_params=_cparams("parallel"))(x)


def _concat_cols(parts, name):
    t = parts[0].shape[0]
    widths = [p_.shape[1] for p_ in parts]
    tr = _tile(t, 256, 16)

    def body(*refs):
        o_ref = refs[-1]
        at = 0
        for ref, wd in zip(refs[:-1], widths):
            o_ref[:, at:at + wd] = ref[...]
            at += wd

    return pl.pallas_call(
        body, name=name, grid=(t // tr,), in_specs=[pl.BlockSpec((tr, wd), lambda i: (i, 0)) for wd in widths],
        out_specs=pl.BlockSpec((tr, sum(widths)), lambda i: (i, 0)), out_shape=SDS((t, sum(widths)), parts[0].dtype),
        compiler_params=_cparams("parallel"),
    )(*parts)


def _merge_fwd(z, ma, mb, w, name):
    t = z.shape[0]
    tr = _tile(t, 256, 16)

    def body(gc_ref, gh_ref, ma_ref, mb_ref, o_ref):
        o_ref[...] = (_sigmoid(gc_ref[...]) * ma_ref[...].astype(F32) + _sigmoid(gh_ref[...]) * mb_ref[...].astype(F32)).astype(BF16)

    half = pl.BlockSpec((tr, w), lambda i, j: (i, j))
    return pl.pallas_call(
        body, name=name, grid=(t // tr, 2),
        in_specs=[pl.BlockSpec((tr, w), lambda i, j: (i, 7 + j)), pl.BlockSpec((tr, w), lambda i, j: (i, 9 + j)), half, half],
        out_specs=half, out_shape=SDS((t, 2 * w), BF16), compiler_params=_cparams("parallel", "parallel"),
    )(z, z, ma, mb)


def _merge_bwd(dmer, z, ma, mb, w, name):
    t = z.shape[0]
    tr = _tile(t, 256, 16)

    def body(d_ref, gc_ref, gh_ref, ma_ref, mb_ref, dma_ref, dmb_ref, dgc_ref, dgh_ref):
        dm = d_ref[...]
        sc = _sigmoid(gc_ref[...])
        sh = _sigmoid(gh_ref[...])
        dma_ref[...] = (dm * sc).astype(BF16)
        dmb_ref[...] = (dm * sh).astype(BF16)
        dgc_ref[...] = (dm * ma_ref[...].astype(F32) * sc * (1.0 - sc)).astype(BF16)
        dgh_ref[...] = (dm * mb_ref[...].astype(F32) * sh * (1.0 - sh)).astype(BF16)

    half = pl.BlockSpec((tr, w), lambda i, j: (i, j))
    return pl.pallas_call(
        body, name=name, grid=(t // tr, 2),
        in_specs=[half, pl.BlockSpec((tr, w), lambda i, j: (i, 7 + j)), pl.BlockSpec((tr, w), lambda i, j: (i, 9 + j)), half, half],
        out_specs=[half] * 4, out_shape=[SDS((t, 2 * w), BF16)] * 4, compiler_params=_cparams("parallel", "parallel"),
    )(dmer, z, z, ma, mb)


def _shift_down(x, s, row):
    return jnp.where(row >= s, pltpu.roll(x, s, axis=0), 0.0)


def _shift_up(x, s, row, t):
    return jnp.where(row < t - s, pltpu.roll(x, t - s, axis=0), 0.0)


def _conv_fwd(z, cw, w, name):
    t = z.shape[0]
    tc = LANES
    nb = w // tc

    def body(b_ref, c_ref, h_ref, w_ref, o_ref):
        u = c_ref[...] * h_ref[...]
        row = lax.broadcasted_iota(jnp.int32, u.shape, 0)
        cw_ = w_ref[...]
        conv = cw_[2:3, :] * u + cw_[1:2, :] * _shift_down(u, 1, row) + cw_[0:1, :] * _shift_down(u, 2, row)
        o_ref[...] = (b_ref[...] * conv).astype(BF16)

    col = lambda off: pl.BlockSpec((t, tc), lambda j: (0, off * nb + j))
    return pl.pallas_call(
        body, name=name, grid=(nb,), in_specs=[col(0), col(1), col(2), pl.BlockSpec((3, tc), lambda j: (0, j))],
        out_specs=pl.BlockSpec((t, tc), lambda j: (0, j)), out_shape=SDS((t, w), BF16), compiler_params=_cparams("parallel"),
    )(z, z, z, cw)


def _conv_bwd(dy, z, cw, w, name):
    t = z.shape[0]
    tc = LANES
    nb = w // tc

    def body(dy_ref, b_ref, c_ref, h_ref, w_ref, db_ref, dc_ref, dh_ref, dw_ref):
        c_, h_ = c_ref[...], h_ref[...]
        u = c_ * h_
        row = lax.broadcasted_iota(jnp.int32, u.shape, 0)
        cw_ = w_ref[...]
        u1 = _shift_down(u, 1, row)
        u2 = _shift_down(u, 2, row)
        dy_ = dy_ref[...]
        db_ref[...] = (dy_ * (cw_[2:3, :] * u + cw_[1:2, :] * u1 + cw_[0:1, :] * u2)).astype(BF16)
        dconv = dy_ * b_ref[...]
        du = cw_[2:3, :] * dconv + cw_[1:2, :] * _shift_up(dconv, 1, row, t) + cw_[0:1, :] * _shift_up(dconv, 2, row, t)
        dc_ref[...] = (du * h_).astype(BF16)
        dh_ref[...] = (du * c_).astype(BF16)
        dw_ref[0:1, :] = jnp.sum(dconv * u2, axis=0, keepdims=True)
        dw_ref[1:2, :] = jnp.sum(dconv * u1, axis=0, keepdims=True)
        dw_ref[2:3, :] = jnp.sum(dconv * u, axis=0, keepdims=True)

    col = lambda off: pl.BlockSpec((t, tc), lambda j: (0, off * nb + j))
    own = pl.BlockSpec((t, tc), lambda j: (0, j))
    wsp = pl.BlockSpec((3, tc), lambda j: (0, j))
    return pl.pallas_call(
        body, name=name, grid=(nb,), in_specs=[own, col(0), col(1), col(2), wsp], out_specs=[own, own, own, wsp],
        out_shape=[SDS((t, w), BF16)] * 3 + [SDS((3, w), F32)], compiler_params=_cparams("parallel"),
    )(dy, z, z, z, cw)


def _lower_bound(hg):
    mx = jnp.max(hg, axis=0, keepdims=True)
    e = jnp.exp(hg - mx)
    inv = 1.0 / jnp.sum(e, axis=0, keepdims=True)
    return e[0:1, :] * inv, e[1:2, :] * inv


def _chunk_cumsum(x, row):
    s = 1
    while s < CHUNK:
        x = x + jnp.where(row % CHUNK >= s, pltpu.roll(x, s, axis=0), 0.0)
        s *= 2
    return x


def _dot_nt(a, b):
    return lax.dot_general(a.astype(BF16), b.astype(BF16), (((1,), (1,)), ((), ())), preferred_element_type=F32)


def _dot_tn(a, b):
    return lax.dot_general(a.astype(BF16), b.astype(BF16), (((0,), (0,)), ((), ())), preferred_element_type=F32)


def _dot_nn(a, b):
    return jnp.dot(a.astype(BF16), b.astype(BF16), preferred_element_type=F32)


def _tril(x):
    r = lax.broadcasted_iota(jnp.int32, x.shape, 0)
    c = lax.broadcasted_iota(jnp.int32, x.shape, 1)
    return jnp.where(r >= c, x, 0.0)


HGRN_GROUP = 4
HGRN_ROWS = 512
HGRN_UNROLL = 2


def _unrolled_loop(n, step, init):
    assert n % HGRN_UNROLL == 0

    def trip(i, carry):
        for u in range(HGRN_UNROLL):
            carry = step(i * HGRN_UNROLL + u, carry)
        return carry

    return lax.fori_loop(0, n // HGRN_UNROLL, trip, init)


def _hgrn_chunk_inputs(q_ref, f_ref, cum_ref, lb, rows, ln):
    qr = q_ref[rows, ln]
    q = qr * _sigmoid(qr)
    f = lb + (1.0 - lb) * _sigmoid(f_ref[rows, ln])
    return q, 1.0 - f, cum_ref[rows, ln]


def _hgrn_fwd(z, hg, nw, w, name, comm=()):
    t = z.shape[0]
    nh = w // HEAD
    gh = _tile(nh, HGRN_GROUP, 1)
    gw = gh * HEAD
    ngrp = nh // gh
    tb = _tile(t, HGRN_ROWS, CHUNK)
    ncb = tb // CHUNK

    def body(q_ref, f_ref, i_ref, g_ref, hg_ref, nw_ref, y_ref, o_ref, st_ref, cum_ref, *s_refs):
        lb_all, _ = _lower_bound(hg_ref[...])
        row = lax.broadcasted_iota(jnp.int32, (tb, gw), 0)
        cum_ref[...] = _chunk_cumsum(jnp.log(lb_all + (1.0 - lb_all) * _sigmoid(f_ref[...])), row)

        @pl.when(pl.program_id(1) == 0)
        def _():
            for s_ref in s_refs:
                s_ref[...] = jnp.zeros_like(s_ref)

        def step(c, carry):
            rows = pl.ds(pl.multiple_of(c * CHUNK, CHUNK), CHUNK)
            for g in range(gh):
                ln = slice(g * HEAD, (g + 1) * HEAD)
                lb = lb_all[:, ln]
                q, k, cum = _hgrn_chunk_inputs(q_ref, f_ref, cum_ref, lb, rows, ln)
                v = i_ref[rows, ln]
                last = cum[CHUNK - 1:CHUNK, :]
                mid = cum[CHUNK // 2 - 1:CHUNK // 2, :]
                st = s_refs[g][...]
                st_ref[g, c] = st.astype(BF16)
                scores = _tril(_dot_nt(q * jnp.exp(cum - mid), k * jnp.exp(mid - cum)))
                o_ref[rows, ln] = _dot_nt(q * jnp.exp(cum), st) + _dot_nn(scores, v)
                s_refs[g][...] = st * jnp.exp(last) + _dot_tn(v, k * jnp.exp(last - cum))
            return carry

        _unrolled_loop(ncb, step, 0)
        for g in range(gh):
            ln = slice(g * HEAD, (g + 1) * HEAD)
            o = o_ref[:, ln]
            n = o * lax.rsqrt(jnp.mean(o * o, axis=-1, keepdims=True) + RMS_EPS)
            gr = g_ref[:, ln]
            y_ref[:, ln] = (n * nw_ref[...] * gr * _sigmoid(gr)).astype(BF16)

    col = lambda off: pl.BlockSpec((tb, gw), lambda h, j: (j, off * ngrp + h))
    own = pl.BlockSpec((tb, gw), lambda h, j: (j, h))
    return _hosted_call(
        body, comm, name=name, grid=(ngrp, t // tb),
        in_specs=[col(3), col(4), col(5), col(6), pl.BlockSpec((2, gw), lambda h, j: (0, h)),
                  pl.BlockSpec((1, HEAD), lambda h, j: (0, 0))],
        out_specs=[own, own, pl.BlockSpec((gh, ncb, HEAD, HEAD), lambda h, j: (h, j, 0, 0))],
        out_shape=[SDS((t, w), BF16), SDS((t, w), F32), SDS((nh, t // CHUNK, HEAD, HEAD), BF16)],
        scratch_shapes=[pltpu.VMEM((tb, gw), F32)] + [pltpu.VMEM((HEAD, HEAD), F32)] * gh,
        operands=(z, z, z, z, hg, nw), parallel=1)


def _hgrn_bwd(dy, z, o, states, hg, nw, w, name, comm=()):
    t = z.shape[0]
    nh = w // HEAD
    gh = _tile(nh, HGRN_GROUP, 1)
    gw = gh * HEAD
    ngrp = nh // gh
    tb = _tile(t, HGRN_ROWS, CHUNK)
    ncb = tb // CHUNK
    nt = t // tb

    def body(dy_ref, q_ref, f_ref, i_ref, g_ref, o_ref, st_ref, hg_ref, nw_ref,
             dq_ref, df_ref, di_ref, dg_ref, dhg_ref, dnw_ref, cum_ref, do_ref, *ds_refs):
        lb_all, s1_all = _lower_bound(hg_ref[...])
        row = lax.broadcasted_iota(jnp.int32, (tb, gw), 0)
        crow = lax.broadcasted_iota(jnp.int32, (CHUNK, HEAD), 0)
        cum_ref[...] = _chunk_cumsum(jnp.log(lb_all + (1.0 - lb_all) * _sigmoid(f_ref[...])), row)

        @pl.when(pl.program_id(1) == 0)
        def _():
            for ds_ref in ds_refs:
                ds_ref[...] = jnp.zeros_like(ds_ref)
            dhg_ref[...] = jnp.zeros_like(dhg_ref)
            dnw_ref[...] = jnp.zeros_like(dnw_ref)

        for g in range(gh):
            ln = slice(g * HEAD, (g + 1) * HEAD)
            o_ = o_ref[:, ln]
            rstd = lax.rsqrt(jnp.mean(o_ * o_, axis=-1, keepdims=True) + RMS_EPS)
            n = o_ * rstd
            gr = g_ref[:, ln]
            sg = _sigmoid(gr)
            dy_ = dy_ref[:, ln]
            dg_ref[:, ln] = (dy_ * n * nw_ref[...] * (sg * (1.0 + gr * (1.0 - sg)))).astype(BF16)
            dsil = dy_ * gr * sg
            dnw_ref[:, ln] += jnp.sum(dsil * n, axis=0, keepdims=True)
            dn = dsil * nw_ref[...]
            do_ref[:, ln] = rstd * (dn - n * jnp.mean(dn * n, axis=-1, keepdims=True))

        def step(cc, dlbs):
            c = ncb - 1 - cc
            rows = pl.ds(pl.multiple_of(c * CHUNK, CHUNK), CHUNK)
            new = []
            for g in range(gh):
                ln = slice(g * HEAD, (g + 1) * HEAD)
                lb = lb_all[:, ln]
                qr = q_ref[rows, ln]
                sq = _sigmoid(qr)
                q = qr * sq
                sf = _sigmoid(f_ref[rows, ln])
                f = lb + (1.0 - lb) * sf
                k = 1.0 - f
                cum = cum_ref[rows, ln]
                v = i_ref[rows, ln]
                do = do_ref[rows, ln]
                last = cum[CHUNK - 1:CHUNK, :]
                mid = cum[CHUNK // 2 - 1:CHUNK // 2, :]
                eg = jnp.exp(cum)
                em = jnp.exp(cum - mid)
                enm = jnp.exp(mid - cum)
                elc = jnp.exp(last - cum)
                qm, km, kl = q * em, k * enm, k * elc
                ds = ds_refs[g][...]
                a = _tril(_dot_nt(qm, km))
                da = _tril(_dot_nt(do, v))
                di_ref[rows, ln] = (_dot_tn(a, do) + _dot_nt(kl, ds)).astype(BF16)
                st = st_ref[g, c]
                dkl = _dot_nn(v, ds)
                dq = _dot_nn(do, st) * eg + _dot_nn(da, km) * em
                dk = _dot_tn(da, qm) * enm + dkl * elc
                el = jnp.exp(last)
                ds_refs[g][...] = ds * el + _dot_tn(do, q * eg)
                dlast = jnp.sum(kl * dkl, axis=0, keepdims=True) + el * jnp.sum(ds * st.astype(F32), axis=0, keepdims=True)
                x = q * dq - k * dk + jnp.where(crow == CHUNK - 1, dlast, 0.0)
                s = 1
                while s < CHUNK:
                    x = x + _shift_up(x, s, crow, CHUNK)
                    s *= 2
                df = x / f - dk
                dq_ref[rows, ln] = (dq * (sq * (1.0 + qr * (1.0 - sq)))).astype(BF16)
                df_ref[rows, ln] = (df * (1.0 - lb) * sf * (1.0 - sf)).astype(BF16)
                new.append(dlbs[g] + jnp.sum(df * (1.0 - sf), axis=0, keepdims=True))
            return tuple(new)

        dlbs = _unrolled_loop(ncb, step, tuple(jnp.zeros((1, HEAD), F32) for _ in range(gh)))
        for g in range(gh):
            ln = slice(g * HEAD, (g + 1) * HEAD)
            dlb = dlbs[g] * lb_all[:, ln] * s1_all[:, ln]
            dhg_ref[0:1, ln] += dlb
            dhg_ref[1:2, ln] -= dlb

    col = lambda off: pl.BlockSpec((tb, gw), lambda h, j: (nt - 1 - j, off * ngrp + h))
    own = pl.BlockSpec((tb, gw), lambda h, j: (nt - 1 - j, h))
    hsp = pl.BlockSpec((2, gw), lambda h, j: (0, h))
    return _hosted_call(
        body, comm, name=name, grid=(ngrp, nt),
        in_specs=[own, col(3), col(4), col(5), col(6), own,
                  pl.BlockSpec((gh, ncb, HEAD, HEAD), lambda h, j: (h, nt - 1 - j, 0, 0)),
                  hsp, pl.BlockSpec((1, HEAD), lambda h, j: (0, 0))],
        out_specs=[own, own, own, own, hsp, pl.BlockSpec((1, gw), lambda h, j: (0, h))],
        out_shape=[SDS((t, w), BF16)] * 4 + [SDS((2, w), F32), SDS((1, w), F32)],
        scratch_shapes=[pltpu.VMEM((tb, gw), F32)] * 2 + [pltpu.VMEM((HEAD, HEAD), F32)] * gh,
        operands=(dy, z, z, z, z, o, states, hg, nw), parallel=1)


def _cast_pad(wt, n_pad, meta, sp, name, comm=()):
    _, r, n = wt.shape
    g, p, per = meta
    tr = _tile(r, max(16, (3 << 19) // n_pad // 16 * 16), 16)

    def body(w_ref, o_ref):
        if n_pad != n:
            o_ref[...] = jnp.zeros(o_ref.shape, o_ref.dtype)
        o_ref[:, 0:n] = w_ref[...].astype(BF16)

    return _hosted_call(
        body, comm, name=name, grid=(r // tr,), in_specs=[pl.BlockSpec((None, tr, n), lambda i, sp: (0, i, 0))],
        out_specs=[pl.BlockSpec((None, tr, n_pad), lambda i, sp: (sp[1] // per, ((sp[1] % per) * r) // tr + i, 0))],
        out_shape=[SDS((g, p, n_pad), BF16)], scratch_shapes=[], operands=(wt,), parallel=1, prefetch=sp)


def _cast_pad_t(wt_t, n_pad, meta, sp, name, comm=()):
    _, n, r = wt_t.shape
    g, p, per = meta
    tc = _tile(r, 256, LANES)

    def body(w_ref, o_ref):
        for lo in range(0, n_pad, LANES):
            rows = min(LANES, n - lo)
            piece = w_ref[lo:lo + rows, :]
            if rows < LANES:
                piece = jnp.concatenate([piece, jnp.zeros((LANES - rows, tc), F32)], axis=0)
            o_ref[:, lo:lo + LANES] = piece.T.astype(BF16)

    return _hosted_call(
        body, comm, name=name, grid=(r // tc,), in_specs=[pl.BlockSpec((None, n, tc), lambda i, sp: (0, 0, i))],
        out_specs=[pl.BlockSpec((None, tc, n_pad), lambda i, sp: (sp[1] // per, ((sp[1] % per) * r) // tc + i, 0))],
        out_shape=[SDS((g, p, n_pad), BF16)], scratch_shapes=[], operands=(wt_t,), parallel=1, prefetch=sp)


def _adam_math(w, g, m, v):
    m2 = ADAM_B1 * m + (1.0 - ADAM_B1) * g
    v2 = ADAM_B2 * v + (1.0 - ADAM_B2) * (g * g)
    c1 = 1.0 / (1.0 - ADAM_B1 ** ADAM_STEP)
    c2 = 1.0 / (1.0 - ADAM_B2 ** ADAM_STEP)
    return -ADAM_LR * ((m2 * c1) / (jnp.sqrt(v2 * c2) + ADAM_EPS) + ADAM_WD * w), m2, v2


def _adamw_t(wt_t, g, m_t, v_t, name):
    _, n, r = wt_t.shape
    ng = g.shape[1]
    tc = LANES

    def body(w_ref, g_ref, m_ref, v_ref, go_ref, d_ref, mo_ref, vo_ref, gt_ref):
        for lo in range(0, ng, LANES):
            gt_ref[lo:lo + LANES, :] = g_ref[:, lo:lo + LANES].T
        g_ = gt_ref[0:n, :]
        delta, m2, v2 = _adam_math(w_ref[...], g_, m_ref[...], v_ref[...])
        go_ref[...] = g_
        d_ref[...] = delta
        mo_ref[...] = m2
        vo_ref[...] = v2

    blk = pl.BlockSpec((None, n, tc), lambda i: (0, 0, i))
    return pl.pallas_call(
        body, name=name, grid=(r // tc,), in_specs=[blk, pl.BlockSpec((tc, ng), lambda i: (i, 0)), blk, blk],
        out_specs=[blk] * 4, out_shape=[SDS(wt_t.shape, F32)] * 4, scratch_shapes=[pltpu.VMEM((ng, tc), F32)],
        compiler_params=_cparams("parallel"),
    )(wt_t, g, m_t, v_t)


def _adamw(wt, g, m, v, name):
    lead = (None,) * (wt.ndim - 2)
    zero = (0,) * (wt.ndim - 2)
    r, n = wt.shape[-2:]
    ng = g.shape[1]
    nct = 2 if ng == n and n % (2 * LANES) == 0 else 1
    tc, tg = n // nct, ng // nct
    tr = _tile(r, max(8, (3 << 17) // tg // 8 * 8), 8)

    def body(w_ref, g_ref, m_ref, v_ref, go_ref, d_ref, mo_ref, vo_ref):
        g_ = g_ref[:, 0:tc]
        delta, m2, v2 = _adam_math(w_ref[...], g_, m_ref[...], v_ref[...])
        go_ref[...] = g_
        d_ref[...] = delta
        mo_ref[...] = m2
        vo_ref[...] = v2

    blk = pl.BlockSpec(lead + (tr, tc), lambda i, j: zero + (i, j))
    return pl.pallas_call(
        body, name=name, grid=(r // tr, nct), in_specs=[blk, pl.BlockSpec((tr, tg), lambda i, j: (i, j)), blk, blk],
        out_specs=[blk] * 4, out_shape=[SDS(wt.shape, F32)] * 4, compiler_params=_cparams("parallel", "parallel"),
    )(wt, g, m, v)


def _place():
    x, y, c = lax.axis_index("x"), lax.axis_index("y"), lax.axis_index("c")
    return x, y, c, 2 * x + y


def _chip_dev(k, c):
    return (k // 2, k % 2, c)


def _half(ref, j, h, rows, per):
    return ref.at[j // per, pl.ds((j % per) * rows + h * (rows // 2), rows // 2)]


def _gather_stage(bufs, metas, rows_of, ici_parts, fwd_parts, zero_pad):
    nw = len(bufs)
    ici_on = [i for i in range(nw) if ici_parts[i] is not None]
    fwd_on = [i for i in range(nw) if fwd_parts[i] is not None]
    pad_jobs = [(i, gi) for i in ici_on if ici_parts[i][0] == 0 and metas[i][1] > metas[i][2] * rows_of[i]
                for gi in range(metas[i][0])]

    def part_of(ref, i, j, h, part):
        per = metas[i][2]
        p, np_ = part
        pr = rows_of[i] // 2 // np_
        return ref.at[j // per, pl.ds((j % per) * rows_of[i] + h * (rows_of[i] // 2) + p * pr, pr)]

    def descriptors(ins, outs, sems):
        src, zp, dst = ins[:nw], ins[nw], outs
        pads, send, recv, fsend, frecv = sems
        x, y, c, me = _place()

        def pad(n):
            i, gi = pad_jobs[n]
            extra = metas[i][1] - metas[i][2] * rows_of[i]
            return pltpu.make_async_copy(zp.at[pl.ds(0, extra)], dst[i].at[gi, pl.ds(metas[i][2] * rows_of[i], extra)], pads.at[n])

        def ici(i, r, frm):
            return pltpu.make_async_remote_copy(
                src_ref=part_of(src[i], i, me, c, ici_parts[i]), dst_ref=part_of(dst[i], i, frm, c, ici_parts[i]),
                send_sem=send.at[i, r - 1], recv_sem=recv.at[i, r - 1], device_id=_chip_dev((me + r) % N_CHIPS, c),
                device_id_type=MESH)

        def d2d(i, r, frm, h):
            blk = part_of(dst[i], i, frm, h, fwd_parts[i])
            return pltpu.make_async_remote_copy(src_ref=blk, dst_ref=blk, send_sem=fsend.at[i, r - 1],
                                                recv_sem=frecv.at[i, r - 1], device_id=(x, y, 1 - c), device_id_type=MESH)

        return pad, ici, d2d, c, me

    def start(ins, outs, sems):
        pad, ici, d2d, c, me = descriptors(ins, outs, sems)
        for n in range(len(pad_jobs)):
            pad(n).start()
        for i in fwd_on:
            for r in range(1, N_CHIPS):
                d2d(i, r, (me - r) % N_CHIPS, c).start()
        for i in ici_on:
            for r in range(1, N_CHIPS):
                ici(i, r, me).start()

    def finish(ins, outs, sems):
        pad, ici, d2d, c, me = descriptors(ins, outs, sems)
        for i in fwd_on:
            for r in range(1, N_CHIPS):
                d2d(i, r, (me - r) % N_CHIPS, 1 - c).wait_recv()
                d2d(i, r, (me - r) % N_CHIPS, c).wait_send()
        for i in ici_on:
            for r in range(1, N_CHIPS):
                ici(i, r, (me - r) % N_CHIPS).wait_recv()
                ici(i, r, me).wait_send()
        for n in range(len(pad_jobs)):
            pad(n).wait()

    return _Stage(ins=list(bufs) + [zero_pad], out_shapes=[SDS(b.shape, b.dtype) for b in bufs],
                  aliases={i: i for i in range(nw)},
                  sems=[pltpu.SemaphoreType.DMA((max(len(pad_jobs), 1),))] + [pltpu.SemaphoreType.DMA((nw, N_CHIPS - 1))] * 4,
                  start=start, finish=finish)


def _gather_small(packed, name):
    r, n = packed.shape

    def body(src, dst, send, recv):
        x, y, c, me = _place()
        dst[me] = src[...]
        cps = []
        for d in range(1, N_CHIPS):
            cp = pltpu.make_async_remote_copy(src_ref=src, dst_ref=dst.at[me], send_sem=send.at[d - 1], recv_sem=recv.at[d - 1],
                                              device_id=_chip_dev((me + d) % N_CHIPS, c), device_id_type=MESH)
            cp.start()
            cps.append(cp)
        for d in range(1, N_CHIPS):
            pltpu.make_async_remote_copy(src_ref=src, dst_ref=dst.at[(me - d) % N_CHIPS], send_sem=send.at[d - 1],
                                         recv_sem=recv.at[d - 1], device_id=_chip_dev((me + d) % N_CHIPS, c),
                                         device_id_type=MESH).wait_recv()
        for cp in cps:
            cp.wait_send()

    return pl.pallas_call(
        body, name=name, in_specs=[VMEM_SPEC], out_specs=VMEM_SPEC, out_shape=SDS((N_CHIPS, r, n), F32),
        scratch_shapes=[pltpu.SemaphoreType.DMA((N_CHIPS - 1,))] * 2,
    )(packed)


def _all_reduce_small(packed, name):
    r, n = packed.shape

    def body(src, out, slots, send, recv):
        x, y, c, me = _place()
        idx = 2 * me + c
        slots[idx] = src[...]
        cps = []

        def peer(d):
            p = (idx + d) % N_DEV
            return (p // 4, (p // 2) % 2, p % 2)

        for d in range(1, N_DEV):
            cp = pltpu.make_async_remote_copy(src_ref=src, dst_ref=slots.at[idx], send_sem=send.at[d - 1], recv_sem=recv.at[d - 1],
                                              device_id=peer(d), device_id_type=MESH)
            cp.start()
            cps.append(cp)
        for d in range(1, N_DEV):
            pltpu.make_async_remote_copy(src_ref=src, dst_ref=slots.at[(idx - d) % N_DEV], send_sem=send.at[d - 1],
                                         recv_sem=recv.at[d - 1], device_id=peer(d), device_id_type=MESH).wait_recv()
        for cp in cps:
            cp.wait_send()
        acc = slots[0]
        for k in range(1, N_DEV):
            acc = acc + slots[k]
        out[...] = acc

    return pl.pallas_call(
        body, name=name, in_specs=[VMEM_SPEC], out_specs=VMEM_SPEC, out_shape=SDS((r, n), F32),
        scratch_shapes=[pltpu.VMEM((N_DEV, r, n), F32)] + [pltpu.SemaphoreType.DMA((N_DEV - 1,))] * 2,
    )(packed)


def _simple_stage(ins, out_shapes, aliases, n_copies, copies):
    def start(ins_, outs, sems):
        for cp in copies(ins_, outs, *sems):
            cp.start()

    def finish(ins_, outs, sems):
        for cp in copies(ins_, outs, *sems):
            cp.wait()

    return _Stage(ins=list(ins), out_shapes=list(out_shapes), aliases=aliases,
                  sems=[pltpu.SemaphoreType.DMA((n_copies,))] * 2, start=start, finish=finish)


def _rs_pair_exchange(grads, metas, rows_of):
    nw = len(grads)

    def copies(src, dst, send, recv):
        x, y, c, me = _place()
        return [pltpu.make_async_remote_copy(
            src_ref=_half(src[i], j, 1 - c, rows_of[i], metas[i][2]), dst_ref=dst[i].at[j], send_sem=send.at[i * N_CHIPS + j],
            recv_sem=recv.at[i * N_CHIPS + j], device_id=(x, y, 1 - c), device_id_type=MESH)
            for i in range(nw) for j in range(N_CHIPS)]

    out_shapes = [SDS((N_CHIPS, rows_of[i] // 2, g.shape[2]), g.dtype) for i, g in enumerate(grads)]
    return _simple_stage(grads, out_shapes, {}, nw * N_CHIPS, copies)


def _rs_pair_add(g, got, meta, rows, sp, name):
    per = meta[2]
    n = g.shape[2]
    hr = rows // 2
    tr = _tile(hr, max(16, (3 << 19) // n // 16 * 16), 16)

    def body(sp_ref, g_ref, got_ref, snd_ref, own_ref):
        j = pl.program_id(1)
        s = g_ref[...].astype(F32) + got_ref[...].astype(F32)
        snd_ref[...] = s.astype(BF16)

        @pl.when(j == sp_ref[1])
        def _():
            own_ref[...] = s

    grid_spec = pltpu.PrefetchScalarGridSpec(
        num_scalar_prefetch=1, grid=(hr // tr, N_CHIPS),
        in_specs=[pl.BlockSpec((None, tr, n), lambda i, j, sp: (j // per, ((j % per) * rows + sp[0] * hr) // tr + i, 0)),
                  pl.BlockSpec((None, tr, n), lambda i, j, sp: (j, i, 0))],
        out_specs=[pl.BlockSpec((None, tr, n), lambda i, j, sp: (j, i, 0)), pl.BlockSpec((tr, n), lambda i, j, sp: (i, 0))])
    return pl.pallas_call(
        body, name=name, grid_spec=grid_spec, out_shape=[SDS((N_CHIPS, hr, n), BF16), SDS((hr, n), F32)],
        compiler_params=_cparams("parallel", "arbitrary"),
    )(sp, g, got)


def _rs_chip_exchange(sends, part=(0, 1), prev=None):
    nw = len(sends)
    p, np_ = part

    def copies(src, dst, send, recv):
        x, y, c, me = _place()
        cps = []
        for i in range(nw):
            pr = sends[i].shape[1] // np_
            for r in range(1, N_CHIPS):
                cps.append(pltpu.make_async_remote_copy(
                    src_ref=src[i].at[(me + r) % N_CHIPS, pl.ds(p * pr, pr)], dst_ref=dst[i].at[r - 1, pl.ds(p * pr, pr)],
                    send_sem=send.at[i * (N_CHIPS - 1) + r - 1], recv_sem=recv.at[i * (N_CHIPS - 1) + r - 1],
                    device_id=_chip_dev((me + r) % N_CHIPS, c), device_id_type=MESH))
        return cps

    out_shapes = [SDS((N_CHIPS - 1,) + s.shape[1:], BF16) for s in sends]
    if prev is None:
        return _simple_stage(sends, out_shapes, {}, nw * (N_CHIPS - 1), copies)
    return _simple_stage(list(sends) + list(prev), out_shapes, {nw + i: i for i in range(nw)}, nw * (N_CHIPS - 1), copies)


def _rs_chip_add(own, got, sp, name):
    hr, n = own.shape
    tr = _tile(hr, max(16, (3 << 19) // n // 16 * 16), 16)

    def body(sp_ref, own_ref, got_ref, o_ref):
        acc = own_ref[...]
        for r in range(N_CHIPS - 1):
            acc = acc + got_ref[r].astype(F32)
        o_ref[...] = acc

    grid_spec = pltpu.PrefetchScalarGridSpec(
        num_scalar_prefetch=1, grid=(hr // tr,),
        in_specs=[pl.BlockSpec((tr, n), lambda i, sp: (i, 0)), pl.BlockSpec((N_CHIPS - 1, tr, n), lambda i, sp: (0, i, 0))],
        out_specs=pl.BlockSpec((tr, n), lambda i, sp: (sp[0] * (hr // tr) + i, 0)))
    return pl.pallas_call(body, name=name, grid_spec=grid_spec, out_shape=SDS((2 * hr, n), F32),
                          compiler_params=_cparams("parallel"))(sp, own, got)


def _rs_pair_share(blocks):
    nw = len(blocks)

    def copies(src, dst, send, recv):
        x, y, c, me = _place()
        cps = []
        for i in range(nw):
            hr = src[i].shape[0] // 2
            cps.append(pltpu.make_async_remote_copy(
                src_ref=src[i].at[pl.ds(c * hr, hr)], dst_ref=dst[i].at[pl.ds(c * hr, hr)], send_sem=send.at[i],
                recv_sem=recv.at[i], device_id=(x, y, 1 - c), device_id_type=MESH))
        return cps

    return _simple_stage(blocks, [SDS(b.shape, b.dtype) for b in blocks], {i: i for i in range(nw)}, nw, copies)


def kernel(x, p, ln_g, ln_b, ffn1_w_in, ffn1_w_out, mix_w_in, conv_w, hg_lower_bound, hg_norm_w, branch_w_conv, branch_w_hgrn, mix_w_out, ffn2_w_in, ffn2_w_out, ple_w_gate, ple_w_proj, loss_target, m_ln_g, m_ln_b, m_ffn1_w_in, m_ffn1_w_out, m_mix_w_in, m_conv_w, m_hg_lower_bound, m_hg_norm_w, m_branch_w_conv, m_branch_w_hgrn, m_mix_w_out, m_ffn2_w_in, m_ffn2_w_out, m_ple_w_gate, m_ple_w_proj, v_ln_g, v_ln_b, v_ffn1_w_in, v_ffn1_w_out, v_mix_w_in, v_conv_w, v_hg_lower_bound, v_hg_norm_w, v_branch_w_conv, v_branch_w_hgrn, v_mix_w_out, v_ffn2_w_in, v_ffn2_w_out, v_ple_w_gate, v_ple_w_proj):
    assert ln_g.shape[0] == DEPTH and x.shape[0] == 1 and p.shape[:2] == (1, 1)
    t, d = x.shape[1], x.shape[2]
    w = d // 2
    x0 = x.reshape(t, d)
    x0b = _to_bf16(x0, "x_bf16")
    pe = p.reshape(t, p.shape[-1])
    target = loss_target.reshape(t, d)
    cx, cy, cc = lax.axis_index("x"), lax.axis_index("y"), lax.axis_index("c")
    chip = 2 * cx + cy
    sp = jnp.stack([cc, chip]).astype(jnp.int32)

    big = dict(ffn1_w_in=ffn1_w_in, ffn1_w_out=ffn1_w_out, mix_w_in=mix_w_in, branch_w_conv=branch_w_conv,
               branch_w_hgrn=branch_w_hgrn, mix_w_out=mix_w_out, ffn2_w_in=ffn2_w_in, ffn2_w_out=ffn2_w_out,
               ple_w_gate=ple_w_gate, ple_w_proj=ple_w_proj)
    moments = dict(ffn1_w_in=(m_ffn1_w_in, v_ffn1_w_in), ffn1_w_out=(m_ffn1_w_out, v_ffn1_w_out), mix_w_in=(m_mix_w_in, v_mix_w_in),
                   branch_w_conv=(m_branch_w_conv, v_branch_w_conv), branch_w_hgrn=(m_branch_w_hgrn, v_branch_w_hgrn),
                   mix_w_out=(m_mix_w_out, v_mix_w_out), ffn2_w_in=(m_ffn2_w_in, v_ffn2_w_in), ffn2_w_out=(m_ffn2_w_out, v_ffn2_w_out),
                   ple_w_gate=(m_ple_w_gate, v_ple_w_gate), ple_w_proj=(m_ple_w_proj, v_ple_w_proj))
    names = list(big)

    n_loc = ffn1_w_in.shape[-1]
    n_pad = -(-n_loc // LANES) * LANES
    assert mix_w_in.shape[-1] % LANES == 0 and ffn1_w_out.shape[1] * 2 == n_loc
    pad_cols = dict(ffn1_w_in=n_pad, ffn2_w_in=n_pad)
    meta = {k: (N_CHIPS, big[k].shape[1], 1) for k in names}
    meta["ffn1_w_out"] = meta["ffn2_w_out"] = (2, n_pad, 2)
    rows = {k: big[k].shape[1] for k in names}
    swap = lambda a: jnp.transpose(a, (0, 2, 1))
    wbuf = {}
    zero_pad = jnp.zeros((max(n_pad - n_loc, 16), d), BF16)

    def cast(k, comm=()):
        if k in pad_cols:
            return _cast_pad_t(swap(big[k]), pad_cols[k], meta[k], sp, "cast_" + k, comm=comm)
        return _cast_pad(big[k], big[k].shape[2], meta[k], sp, "cast_" + k, comm=comm)

    def gather(ici=(), fwd=()):
        ks = list(dict.fromkeys([k for k, _, _ in ici] + [k for k, _, _ in fwd]))
        ip = {k: (p_, n_) for k, p_, n_ in ici}
        fp = {k: (p_, n_) for k, p_, n_ in fwd}
        return _gather_stage([wbuf[k] for k in ks], [meta[k] for k in ks], [rows[k] for k in ks], [ip.get(k) for k in ks],
                             [fp.get(k) for k in ks], zero_pad), ks

    def gathered(ks, outs):
        wbuf.update(zip(ks, outs))

    def w3(k):
        return wbuf[k]

    def w2(k):
        return wbuf[k].reshape(-1, wbuf[k].shape[2])

    dq, wq = d // N_CHIPS, w // N_CHIPS
    small = jnp.concatenate([ln_g[0], ln_b[0], jnp.pad(conv_w[0], ((0, 5), (0, dq - wq)))], axis=0)
    small = _gather_small(small, "gather_small")
    lng = small[:, 0:4, :].transpose(1, 0, 2).reshape(4, 1, d)
    lnb = small[:, 4:8, :].transpose(1, 0, 2).reshape(4, 1, d)
    cw = small[:, 8:11, :wq].transpose(1, 0, 2).reshape(3, w)
    hg = hg_lower_bound
    nw_ = hg_norm_w

    one = lambda *ks_: [(k, 0, 1) for k in ks_]
    wbuf["ffn1_w_in"] = cast("ffn1_w_in")
    carriers = ["ple_w_proj", "ffn1_w_out", "mix_w_in", None] + [k for k in names if k not in ("ffn1_w_in", "ple_w_proj", "ffn1_w_out", "mix_w_in")]
    assert len(carriers) > FIRST_GATHER_PARTS
    for step, k in enumerate(carriers):
        ici = [("ffn1_w_in", step, FIRST_GATHER_PARTS)] if step < FIRST_GATHER_PARTS else []
        fwd = [("ffn1_w_in", step - 1, FIRST_GATHER_PARTS)] if 1 <= step <= FIRST_GATHER_PARTS else []
        ici += one("ple_w_proj") if step == 1 else []
        fwd += one("ple_w_proj") if step == 2 else []
        if ici or fwd:
            st, ks = gather(ici, fwd)
            if k is None:
                pp, got = _mm(pe, w3("ple_w_proj"), name="ple_proj", b_blocked=True, out_dtype=BF16, tn=512, comm=[st])
            else:
                wbuf[k], got = cast(k, comm=[st])
            gathered(ks, got)
        else:
            wbuf[k] = cast(k)
    st, ks = gather(ici=one("ffn1_w_out") + [("mix_w_in", 0, 2)])
    z1, got = _mm(x0b, w3("ffn1_w_in"), name="ffn1_in", b_blocked=True, out_dtype=BF16, tm=1024, comm=[st])
    gathered(ks, got)
    st, ks = gather(fwd=one("ffn1_w_out") + [("mix_w_in", 0, 2)])
    h1, got = _swiglu_fwd(z1, "ffn1_act", comm=[st])
    gathered(ks, got)
    st, ks = gather(ici=[("mix_w_in", 1, 2)])
    y1, got = _mm(h1, w2("ffn1_w_out"), name="ffn1_out", tm=1024, tn=1024, tk=2816, comm=[st])
    gathered(ks, got)
    st, ks = gather(fwd=[("mix_w_in", 1, 2)])
    (r1, x1b), got = _ln_fwd(x0, y1, lng[0], lnb[0], None, None, 0.5, "ln0", comm=[st])
    gathered(ks, got)
    mixo_w = one("branch_w_conv", "branch_w_hgrn", "mix_w_out")
    st, ks = gather(ici=mixo_w + [("ffn2_w_in", 0, 2)])
    z, got = _mm(x1b, w3("mix_w_in"), name="mix_in", b_blocked=True, tm=1024, comm=[st])
    gathered(ks, got)
    ya = _conv_fwd(z, cw, w, "conv_fwd")
    st, ks = gather(ici=[("ffn2_w_in", 1, 2)], fwd=mixo_w + [("ffn2_w_in", 0, 2)])
    (yb, o_h, states), got = _hgrn_fwd(z, hg, nw_, w, "hgrn_fwd", comm=[st])
    gathered(ks, got)
    ma = _mm(ya, w3("branch_w_conv"), name="branch_conv", b_blocked=True, out_dtype=BF16, tn=512)
    mb = _mm(yb, w3("branch_w_hgrn"), name="branch_hgrn", b_blocked=True, out_dtype=BF16, tn=512)
    merged = _merge_fwd(z, ma, mb, w, "merge_fwd")
    st, ks = gather(fwd=[("ffn2_w_in", 1, 2)])
    y2, got = _mm(merged, w2("mix_w_out"), name="mix_out", tn=1024, comm=[st])
    gathered(ks, got)
    r2, x2b = _ln_fwd(r1, y2, lng[1], lnb[1], lng[0], lnb[0], 1.0, "ln1")
    late = one("ffn2_w_out", "ple_w_gate")
    st, ks = gather(ici=late)
    z3, got = _mm(x2b, w3("ffn2_w_in"), name="ffn2_in", b_blocked=True, out_dtype=BF16, tm=1024, comm=[st])
    gathered(ks, got)
    st, ks = gather(fwd=late)
    h3, got = _swiglu_fwd(z3, "ffn2_act", comm=[st])
    gathered(ks, got)
    y3 = _mm(h3, w2("ffn2_w_out"), name="ffn2_out", tm=1024, tn=1024, tk=2816)
    r3, x3b = _ln_fwd(r2, y3, lng[2], lnb[2], lng[1], lnb[1], 0.5, "ln2")
    gp = _mm(x3b, w2("ple_w_gate"), name="ple_gate", out_dtype=BF16, tn=1024)
    dr4, dgp, dpp, dg3, db3, sq = _tail(r3, lng[2], lnb[2], gp, pp, lng[3], lnb[3], target, "tail")

    grads, sends, owns, blocks, outs = {}, {}, {}, {}, {}

    def pair_exchange(*ks):
        return _rs_pair_exchange([grads[k] for k in ks], [meta[k] for k in ks], [rows[k] for k in ks])

    def pair_add(ks, got):
        for k, g_ in zip(ks, got):
            sends[k], owns[k] = _rs_pair_add(grads[k], g_, meta[k], rows[k], sp, "rs_pair_add_" + k)

    def chip_exchange(*ks):
        return _rs_chip_exchange([sends[k] for k in ks])

    def chip_add(ks, got):
        for k, g_ in zip(ks, got):
            blocks[k] = _rs_chip_add(owns[k], g_, sp, "rs_chip_add_" + k)

    def pair_share(*ks):
        return _rs_pair_share([blocks[k] for k in ks])

    def update(ks, full):
        for k, g_ in zip(ks, full):
            m_, v_ = moments[k]
            if k in pad_cols:
                outs[k] = [swap(a) for a in _adamw_t(swap(big[k]), g_, swap(m_), swap(v_), "adamw_" + k)]
            else:
                outs[k] = _adamw(big[k], g_, m_, v_, "adamw_" + k)

    ple = ("ple_w_gate", "ple_w_proj")
    mixo = ("mix_w_out", "branch_w_conv", "branch_w_hgrn")
    dx3m = _mm(dgp, w2("ple_w_gate"), name="d_ple_gate_x", tb=True, tn=1024, tk=2048)
    grads["ple_w_gate"] = _mm(x3b, dgp, name="d_ple_gate_w", ta=True, out_dtype=BF16, tm=1024, tk=2048, tn=1024).reshape(N_CHIPS, -1, d)
    grads["ple_w_proj"] = _mm(pe, dpp, name="d_ple_proj_w", ta=True, out_dtype=BF16, out_blocked=N_CHIPS, tk=2048, tn=512)
    dr3, dy3b, dg2, db2 = _ln_bwd(dr4, dx3m, r3, lng[2], 0.5, "ln2_bwd")
    late_w = ple + ("ffn2_w_out",)
    dh3 = _mm(dy3b, w2("ffn2_w_out"), name="d_ffn2_out_x", tb=True, out_dtype=BF16, tn=1408, tk=2048)
    grads["ffn2_w_out"] = _mm(h3, dy3b, name="d_ffn2_out_w", ta=True, out_dtype=BF16, tm=1408, tk=2048, tn=1024).reshape(2, n_pad, d)
    dz3 = _swiglu_bwd(dh3, z3, "ffn2_act_bwd")
    dx2m, got = _mm(dz3, w3("ffn2_w_in"), name="d_ffn2_in_x", tb=True, b_blocked=True, tm=1024, tn=1024, tk=2816,
                    comm=[pair_exchange(*late_w)])
    pair_add(late_w, got)
    grads["ffn2_w_in"], got = _mm(x2b, dz3, name="d_ffn2_in_w", ta=True, out_dtype=BF16, out_blocked=N_CHIPS, tk=4096, comm=[chip_exchange(*late_w)])
    chip_add(late_w, got)
    dr2, dy2b, dg1, db1 = _ln_bwd(dr3, dx2m, r2, lng[1], 1.0, "ln1_bwd")
    dmer, got = _mm(dy2b, w2("mix_w_out"), name="d_mix_out_x", tb=True, tn=1024, tk=2048, comm=[pair_exchange("ffn2_w_in")])
    pair_add(["ffn2_w_in"], got)
    g_, full = _mm(merged, dy2b, name="d_mix_out_w", ta=True, out_dtype=BF16, tm=1024, tk=2048, tn=1024, comm=[pair_share(*late_w)])
    grads["mix_w_out"] = g_.reshape(N_CHIPS, -1, d)
    update(late_w, full)
    dma, dmb, dgc, dgh = _merge_bwd(dmer, z, ma, mb, w, "merge_bwd")
    dya = _mm(dma, w3("branch_w_conv"), name="d_branch_conv_x", tb=True, b_blocked=True, tn=1024, tk=512)
    dyb = _mm(dmb, w3("branch_w_hgrn"), name="d_branch_hgrn_x", tb=True, b_blocked=True, tn=1024, tk=512)
    grads["branch_w_conv"] = _mm(ya, dma, name="d_branch_conv_w", ta=True, out_dtype=BF16, out_blocked=N_CHIPS, tm=1024, tk=2048, tn=512)
    grads["branch_w_hgrn"] = _mm(yb, dmb, name="d_branch_hgrn_w", ta=True, out_dtype=BF16, out_blocked=N_CHIPS, tm=1024, tk=2048, tn=512)
    dbg, dcg, dhc, dcw = _conv_bwd(dya, z, cw, w, "conv_bwd")
    (dq_, df_, di_, dgr_, dhg, dnw), got2, got = _hgrn_bwd(dyb, z, o_h, states, hg, nw_, w, "hgrn_bwd",
                                                            comm=[chip_exchange("ffn2_w_in"), pair_exchange(*mixo)])
    chip_add(["ffn2_w_in"], got2)
    pair_add(mixo, got)
    dz = _concat_cols([dbg, dcg, dhc, dq_, df_, di_, dgr_, dgc, dgh], "dz_concat")
    dx1m, full, got = _mm(dz, w3("mix_w_in"), name="d_mix_in_x", tb=True, b_blocked=True, tm=1024, tn=1024, tk=2816,
                          comm=[pair_share("ffn2_w_in"), chip_exchange(*mixo)])
    update(["ffn2_w_in"], full)
    chip_add(mixo, got)
    grads["mix_w_in"], full = _mm(x1b, dz, name="d_mix_in_w", ta=True, out_dtype=BF16, out_blocked=N_CHIPS, tk=4096, comm=[pair_share(*mixo)])
    update(mixo, full)
    dr1, dy1b, dg0, db0 = _ln_bwd(dr2, dx1m, r1, lng[0], 0.5, "ln0_bwd")
    dh1, got = _mm(dy1b, w2("ffn1_w_out"), name="d_ffn1_out_x", tb=True, out_dtype=BF16, tn=1408, tk=2048,
                   comm=[pair_exchange("mix_w_in")])
    pair_add(["mix_w_in"], got)
    mix_sends = [sends["mix_w_in"]]
    g_, got_a = _mm(h1, dy1b, name="d_ffn1_out_w", ta=True, out_dtype=BF16, tm=1408, tk=2048, tn=1024, comm=[_rs_chip_exchange(mix_sends, (0, 2))])
    grads["ffn1_w_out"] = g_.reshape(2, n_pad, d)
    dz1 = _swiglu_bwd(dh1, z1, "ffn1_act_bwd")
    g_other, got2, got = _mm(x0b, dz1, name="d_ffn1_in_w_other", ta=True, out_dtype=BF16, out_blocked=N_CHIPS, tk=4096, half=(sp, True),
                             comm=[_rs_chip_exchange(mix_sends, (1, 2), got_a), pair_exchange("ffn1_w_out")])
    chip_add(["mix_w_in"], got2)
    pair_add(["ffn1_w_out"], got)
    grads["ffn1_w_in"], full, got2, got = _mm(
        x0b, dz1, name="d_ffn1_in_w_own", ta=True, out_dtype=BF16, out_blocked=N_CHIPS, tk=4096, half=(sp, False),
        comm=[pair_share("mix_w_in"), chip_exchange("ffn1_w_out"),
              _rs_pair_exchange([g_other], [meta["ffn1_w_in"]], [rows["ffn1_w_in"]])])
    update(["mix_w_in"], full)
    chip_add(["ffn1_w_out"], got2)
    pair_add(["ffn1_w_in"], got)
    dx0, got2, full = _mm(dz1, w3("ffn1_w_in"), name="d_ffn1_in_x", tb=True, b_blocked=True, tm=1024, tn=1024, tk=2816,
                          add=(dr1, ALPHA), comm=[chip_exchange("ffn1_w_in"), pair_share("ffn1_w_out")])
    chip_add(["ffn1_w_in"], got2)
    update(["ffn1_w_out"], full)
    grad_x = dx0.reshape(x.shape)
    update(["ffn1_w_in"], _run_stages([pair_share("ffn1_w_in")], "rs_tail_pair")[0])

    pack = jnp.concatenate([
        dg0, dg1, dg2, dg3, db0, db1, db2, db3,
        jnp.pad(dcw, ((0, 0), (0, d - w))), jnp.pad(dhg, ((0, 0), (0, d - w))),
        jnp.pad(jnp.sum(dnw.reshape(-1, HEAD), axis=0, keepdims=True), ((0, 0), (0, d - HEAD))), sq], axis=0)
    pack = _all_reduce_small(jnp.pad(pack, ((0, 1), (0, 0))), "reduce_small")
    loss = (0.5 / d) * jnp.sum(pack[14])
    g_ln_g = lax.dynamic_slice_in_dim(pack[0:4], chip * dq, dq, axis=1)
    g_ln_b = lax.dynamic_slice_in_dim(pack[4:8], chip * dq, dq, axis=1)
    g_conv = lax.dynamic_slice_in_dim(pack[8:11, :w], chip * wq, wq, axis=1)
    g_hg = pack[11:13, :w]
    g_nw = pack[13:14, :HEAD]

    small_w = dict(ln_g=(ln_g, g_ln_g, m_ln_g, v_ln_g), ln_b=(ln_b, g_ln_b, m_ln_b, v_ln_b),
                   conv_w=(conv_w, g_conv, m_conv_w, v_conv_w), hg_lower_bound=(hg_lower_bound, g_hg, m_hg_lower_bound, v_hg_lower_bound),
                   hg_norm_w=(hg_norm_w, g_nw, m_hg_norm_w, v_hg_norm_w))
    for k, (w_, g_, m_, v_) in small_w.items():
        outs[k] = _adamw(w_, g_.reshape(-1, w_.shape[-1]), m_, v_, "adamw_" + k)

    order = ["ln_g", "ln_b", "ffn1_w_in", "ffn1_w_out", "mix_w_in", "conv_w", "hg_lower_bound", "hg_norm_w", "branch_w_conv",
             "branch_w_hgrn", "mix_w_out", "ffn2_w_in", "ffn2_w_out", "ple_w_gate", "ple_w_proj"]
    return (loss, grad_x, *[outs[k][0] for k in order], *[outs[k][1] for k in order], *[outs[k][2] for k in order],
            *[outs[k][3] for k in order])
```

```python
import collections
import functools

import jax
import jax.numpy as jnp
from jax import lax
from jax.experimental import pallas as pl
from jax.experimental.pallas import tpu as pltpu

F32 = jnp.float32
BF16 = jnp.bfloat16
MESH = pl.DeviceIdType.MESH
ANY = pl.BlockSpec(memory_space=pl.ANY)
VMEM_SPEC = pl.BlockSpec(memory_space=pltpu.VMEM)
SDS = jax.ShapeDtypeStruct

DEPTH = 1
ALPHA = (2.0 * DEPTH) ** 0.25
LN_EPS = 1e-5
RMS_EPS = 1e-6
CHUNK = 32
HEAD = 128
ADAM_LR, ADAM_B1, ADAM_B2, ADAM_EPS, ADAM_WD, ADAM_STEP = 0.001, 0.9, 0.999, 1e-08, 0.01, 10

LANES = 128
N_CHIPS = 4
N_DEV = 8
FIRST_GATHER_PARTS = 8
VMEM_LIMIT = 52 * 1024 * 1024
MM_PIECE = 256


def _cparams(*sem):
    if sem:
        return pltpu.CompilerParams(dimension_semantics=sem, vmem_limit_bytes=VMEM_LIMIT)
    return pltpu.CompilerParams(vmem_limit_bytes=VMEM_LIMIT)


def _tile(n, target, mult):
    best = None
    for t in range(mult, min(n, target) + 1, mult):
        if n % t == 0:
            best = t
    return best if best is not None else n


def _sigmoid(x):
    return 1.0 / (1.0 + jnp.exp(-x))


_Stage = collections.namedtuple("_Stage", "ins out_shapes aliases sems start finish")


def _hosted_call(compute, stages, *, name, grid, in_specs, out_specs, out_shape, scratch_shapes, operands, parallel,
                 prefetch=None):
    n_cmp, n_out, n_scr = len(in_specs), len(out_specs), len(scratch_shapes)
    n_in = n_cmp
    n_pre = int(prefetch is not None)
    c_in = [len(s.ins) for s in stages]
    c_out = [len(s.out_shapes) for s in stages]
    c_sem = [len(s.sems) for s in stages]
    aliases = {}
    for si, s in enumerate(stages):
        for a_in, a_out in s.aliases.items():
            aliases[n_pre + n_in + sum(c_in[:si]) + a_in] = n_out + sum(c_out[:si]) + a_out

    def body(*refs):
        refs = refs[n_pre:]
        ins = refs[:n_cmp]
        cins = refs[n_in:n_in + sum(c_in)]
        outs = refs[n_in + sum(c_in):n_in + sum(c_in) + n_out]
        couts = refs[n_in + sum(c_in) + n_out:n_in + sum(c_in) + n_out + sum(c_out)]
        scr = refs[n_in + sum(c_in) + n_out + sum(c_out):][:n_scr]
        sems = refs[n_in + sum(c_in) + n_out + sum(c_out) + n_scr:]

        def stage_refs(si):
            return (cins[sum(c_in[:si]):sum(c_in[:si + 1])], couts[sum(c_out[:si]):sum(c_out[:si + 1])],
                    sems[sum(c_sem[:si]):sum(c_sem[:si + 1])])

        if stages:
            first = functools.reduce(jnp.logical_and, [pl.program_id(ax) == 0 for ax in range(len(grid))])
            last = functools.reduce(jnp.logical_and, [pl.program_id(ax) == grid[ax] - 1 for ax in range(len(grid))])

            @pl.when(first)
            def _():
                for si, s in enumerate(stages):
                    s.start(*stage_refs(si))

        compute(*ins, *outs, *scr)
        if stages:
            @pl.when(last)
            def _():
                for si, s in enumerate(stages):
                    s.finish(*stage_refs(si))

    sem = ("arbitrary",) * len(grid) if stages else ("parallel",) * parallel + ("arbitrary",) * (len(grid) - parallel)
    all_in = list(in_specs) + [ANY] * (n_in - n_cmp + sum(c_in))
    all_out = list(out_specs) + [ANY] * sum(c_out)
    all_scr = list(scratch_shapes) + [q for s in stages for q in s.sems]
    all_shape = list(out_shape) + [o for s in stages for o in s.out_shapes]
    args = list(operands) + [a for s in stages for a in s.ins]
    if prefetch is None:
        res = pl.pallas_call(body, name=name, grid=grid, in_specs=all_in, out_specs=all_out, out_shape=all_shape,
                             input_output_aliases=aliases, scratch_shapes=all_scr, compiler_params=_cparams(*sem))(*args)
    else:
        grid_spec = pltpu.PrefetchScalarGridSpec(num_scalar_prefetch=1, grid=grid, in_specs=all_in, out_specs=all_out,
                                                 scratch_shapes=all_scr)
        res = pl.pallas_call(body, name=name, grid_spec=grid_spec, out_shape=all_shape, input_output_aliases=aliases,
                             compiler_params=_cparams(*sem))(prefetch, *args)
    main = res[0] if n_out == 1 else list(res[:n_out])
    if not stages:
        return main
    rest = res[n_out:]
    return (main, *[list(rest[sum(c_out[:si]):sum(c_out[:si + 1])]) for si in range(len(stages))])


def _run_stages(stages, name):
    def body(*refs):
        n_i = sum(len(s.ins) for s in stages)
        n_o = sum(len(s.out_shapes) for s in stages)
        cins, couts, sems = refs[:n_i], refs[n_i:n_i + n_o], refs[n_i + n_o:]
        pos = [0, 0, 0]
        parts = []
        for s in stages:
            parts.append((cins[pos[0]:pos[0] + len(s.ins)], couts[pos[1]:pos[1] + len(s.out_shapes)], sems[pos[2]:pos[2] + len(s.sems)]))
            pos = [pos[0] + len(s.ins), pos[1] + len(s.out_shapes), pos[2] + len(s.sems)]
        for s, p_ in zip(stages, parts):
            s.start(*p_)
        for s, p_ in zip(stages, parts):
            s.finish(*p_)

    aliases, ni, no = {}, 0, 0
    for s in stages:
        for a_in, a_out in s.aliases.items():
            aliases[ni + a_in] = no + a_out
        ni, no = ni + len(s.ins), no + len(s.out_shapes)
    res = pl.pallas_call(
        body, name=name, in_specs=[ANY] * ni, out_specs=[ANY] * no, out_shape=[o for s in stages for o in s.out_shapes],
        input_output_aliases=aliases, scratch_shapes=[q for s in stages for q in s.sems],
    )(*[a for s in stages for a in s.ins])
    out, pos = [], 0
    for s in stages:
        out.append(list(res[pos:pos + len(s.out_shapes)]))
        pos += len(s.out_shapes)
    return out


def _mm(a, b, *, name, ta=False, tb=False, b_blocked=False, out_blocked=0, out_dtype=F32,
        tm=512, tn=1408, tk=2048, comm=(), half=None, add=None):
    if ta:
        kd, m = a.shape
    else:
        m, kd = a.shape
    if b_blocked and not tb:
        g, kb, nb = b.shape
        assert kb == kd
        n = g * nb
        tn = _tile(nb, tn, LANES)
        tk = _tile(kd, tk, LANES)
        per_n = nb // tn
        b_spec = pl.BlockSpec((None, tk, tn), lambda i, j, k, *s: (j // per_n, k, j % per_n))
    elif b_blocked and tb:
        g, n, kb = b.shape
        assert g * kb == kd
        tn = _tile(n, tn, LANES)
        tk = _tile(kb, tk, LANES)
        per_k = kb // tk
        b_spec = pl.BlockSpec((None, tn, tk), lambda i, j, k, *s: (k // per_k, j, k % per_k))
    elif tb:
        n, kb = b.shape
        assert kb == kd
        tn = _tile(n, tn, LANES)
        tk = _tile(kd, tk, LANES)
        b_spec = pl.BlockSpec((tn, tk), lambda i, j, k, *s: (j, k))
    else:
        kb, n = b.shape
        assert kb == kd
        tn = _tile(n // out_blocked if out_blocked else n, tn, LANES)
        per_o = (n // out_blocked) // tn if out_blocked else None
        tk = _tile(kd, tk, LANES)
        b_spec = pl.BlockSpec((tk, tn), lambda i, j, k, *s: (k, j))
    m_run = m // 2 if half else m
    tm = _tile(m_run, tm, LANES if ta else 8)

    def row(i, s):
        if not half:
            return i
        h = 1 - s[0][0] if half[1] else s[0][0]
        return h * (m_run // tm) + i

    if ta:
        a_spec = pl.BlockSpec((tk, tm), lambda i, j, k, *s: (k, row(i, s)))
    else:
        a_spec = pl.BlockSpec((tm, tk), lambda i, j, k, *s: (row(i, s), k))
    if out_blocked:
        assert not b_blocked and not tb
        o_spec = pl.BlockSpec((None, tm, tn), lambda i, j, k, *s: (j // per_o, row(i, s), j % per_o))
        o_shape = SDS((out_blocked, m, n // out_blocked), out_dtype)
    else:
        o_spec = pl.BlockSpec((tm, tn), lambda i, j, k, *s: (row(i, s), j))
        o_shape = SDS((m, n), out_dtype)
    nk = kd // tk
    dn = (((0 if ta else 1,), (1 if tb else 0,)), ((), ()))
    grid = (m_run // tm, n // tn, nk)

    pieces = [(lo, min(MM_PIECE, tn - lo)) for lo in range(0, tn, MM_PIECE)]

    def compute(a_ref, b_ref, *rest):
        add_ref = rest[0] if add else None
        o_ref, acc_ref = rest[-2:]
        a_tile = a_ref[...].astype(BF16)
        k = pl.program_id(2)

        def result(acc, cols):
            if add:
                acc = acc + add[1] * add_ref[:, cols]
            return acc.astype(o_ref.dtype)

        if nk > 1:
            @pl.when(k == 0)
            def _():
                acc_ref[...] = jnp.zeros_like(acc_ref)

        for lo, wd in pieces:
            cols = slice(lo, lo + wd)
            b_tile = b_ref[cols, :] if tb else b_ref[:, cols]
            part = lax.dot_general(a_tile, b_tile.astype(BF16), dn, preferred_element_type=F32)
            if nk == 1:
                o_ref[:, cols] = result(part, cols)
            else:
                acc_ref[:, cols] += part

        if nk > 1:
            @pl.when(k == nk - 1)
            def _():
                o_ref[...] = result(acc_ref[...], slice(None))

    extra = [(add[0], o_spec)] if add else []
    return _hosted_call(compute, comm, name=name, grid=grid, in_specs=[a_spec, b_spec] + [s_ for _, s_ in extra], out_specs=[o_spec],
                        out_shape=[o_shape], scratch_shapes=[pltpu.VMEM((tm, tn), F32)], operands=(a, b, *[a_ for a_, _ in extra]),
                        parallel=2, prefetch=half[0] if half else None)


def _swiglu_fwd(z, name, comm=()):
    t, n = z.shape
    n2 = n // 2
    tr = _tile(t, 128, 16)

    def body(a_ref, u_ref, o_ref):
        a = a_ref[...].astype(F32)
        o_ref[...] = (a * _sigmoid(a) * u_ref[...].astype(F32)).astype(o_ref.dtype)

    return _hosted_call(
        body, comm, name=name, grid=(t // tr,),
        in_specs=[pl.BlockSpec((tr, n2), lambda i: (i, 0)), pl.BlockSpec((tr, n2), lambda i: (i, 1))],
        out_specs=[pl.BlockSpec((tr, n2), lambda i: (i, 0))], out_shape=[SDS((t, n2), BF16)], scratch_shapes=[],
        operands=(z, z), parallel=1)


def _swiglu_bwd(dh, z, name):
    t, n = z.shape
    n2 = n // 2
    tr = _tile(t, 128, 16)

    def body(dh_ref, a_ref, u_ref, o_ref):
        a = a_ref[...].astype(F32)
        dh_ = dh_ref[...].astype(F32)
        s = _sigmoid(a)
        o_ref[:, 0:n2] = (dh_ * u_ref[...].astype(F32) * (s * (1.0 + a * (1.0 - s)))).astype(o_ref.dtype)
        o_ref[:, n2:n] = (dh_ * a * s).astype(o_ref.dtype)

    return pl.pallas_call(
        body, name=name, grid=(t // tr,),
        in_specs=[pl.BlockSpec((tr, n2), lambda i: (i, 0)), pl.BlockSpec((tr, n2), lambda i: (i, 0)),
                  pl.BlockSpec((tr, n2), lambda i: (i, 1))],
        out_specs=pl.BlockSpec((tr, n), lambda i: (i, 0)), out_shape=SDS((t, n), BF16),
        compiler_params=_cparams("parallel"),
    )(dh, z, z)


def _ln_stats(r):
    mu = jnp.mean(r, axis=-1, keepdims=True)
    xc = r - mu
    var = jnp.mean(xc * xc, axis=-1, keepdims=True)
    return xc * lax.rsqrt(var + LN_EPS)


def _ln_fwd(xp, y, g, b, gp, bp, scale, name, comm=()):
    t, d = xp.shape
    tr = _tile(t, 256, 16)

    def body(xp_ref, y_ref, g_ref, b_ref, *rest):
        r_ref, xb_ref = rest[-2:]
        x_prev = xp_ref[...]
        if gp is not None:
            x_prev = _ln_stats(x_prev) * rest[0][...] + rest[1][...]
        r = ALPHA * x_prev + scale * y_ref[...]
        r_ref[...] = r
        xb_ref[...] = (_ln_stats(r) * g_ref[...] + b_ref[...]).astype(BF16)

    row = pl.BlockSpec((tr, d), lambda i: (i, 0))
    vec = pl.BlockSpec((1, d), lambda i: (0, 0))
    prev = [] if gp is None else [gp, bp]
    return _hosted_call(
        body, comm, name=name, grid=(t // tr,), in_specs=[row, row, vec, vec] + [vec] * len(prev), out_specs=[row, row],
        out_shape=[SDS((t, d), F32), SDS((t, d), BF16)], scratch_shapes=[], operands=(xp, y, g, b, *prev), parallel=1)


def _ln_bwd(dra, dxm, r, g, scale, name):
    t, d = r.shape
    tr = _tile(t, 256, 16)

    def body(dra_ref, dxm_ref, r_ref, g_ref, dr_ref, dyb_ref, dg_ref, db_ref):
        i = pl.program_id(0)
        dx = ALPHA * dra_ref[...] + dxm_ref[...]
        rr = r_ref[...]
        mu = jnp.mean(rr, axis=-1, keepdims=True)
        xc = rr - mu
        rstd = lax.rsqrt(jnp.mean(xc * xc, axis=-1, keepdims=True) + LN_EPS)
        xh = xc * rstd
        dxh = dx * g_ref[...]
        dr = rstd * (dxh - jnp.mean(dxh, axis=-1, keepdims=True) - xh * jnp.mean(dxh * xh, axis=-1, keepdims=True))
        dr_ref[...] = dr
        dyb_ref[...] = (scale * dr).astype(BF16)
        dg = jnp.sum(dx * xh, axis=0, keepdims=True)
        db = jnp.sum(dx, axis=0, keepdims=True)

        @pl.when(i == 0)
        def _():
            dg_ref[...] = dg
            db_ref[...] = db

        @pl.when(i > 0)
        def _():
            dg_ref[...] += dg
            db_ref[...] += db

    row = pl.BlockSpec((tr, d), lambda i: (i, 0))
    vec = pl.BlockSpec((1, d), lambda i: (0, 0))
    return pl.pallas_call(
        body, name=name, grid=(t // tr,), in_specs=[row, row, row, vec], out_specs=[row, row, vec, vec],
        out_shape=[SDS((t, d), F32), SDS((t, d), BF16), SDS((1, d), F32), SDS((1, d), F32)],
        compiler_params=_cparams("arbitrary"),
    )(dra, dxm, r, g)


def _tail(r3, g3, b3, gp, pp, g, b, target, name):
    t, d = r3.shape
    tr = _tile(t, 256, 16)

    def body(r3_ref, g3_ref, b3_ref, gp_ref, pp_ref, g_ref, b_ref, tg_ref, dr_ref, dgp_ref, dpp_ref, dg_ref, db_ref, sq_ref):
        i = pl.program_id(0)
        gate = _sigmoid(gp_ref[...])
        pp_ = pp_ref[...]
        r = ALPHA * (_ln_stats(r3_ref[...]) * g3_ref[...] + b3_ref[...]) + gate * pp_
        mu = jnp.mean(r, axis=-1, keepdims=True)
        xc = r - mu
        rstd = lax.rsqrt(jnp.mean(xc * xc, axis=-1, keepdims=True) + LN_EPS)
        xh = xc * rstd
        err = xh * g_ref[...] + b_ref[...] - tg_ref[...]
        dx = err * (1.0 / d)
        dxh = dx * g_ref[...]
        dr = rstd * (dxh - jnp.mean(dxh, axis=-1, keepdims=True) - xh * jnp.mean(dxh * xh, axis=-1, keepdims=True))
        dr_ref[...] = dr
        dgp_ref[...] = (dr * pp_ * gate * (1.0 - gate)).astype(BF16)
        dpp_ref[...] = (dr * gate).astype(BF16)
        dg = jnp.sum(dx * xh, axis=0, keepdims=True)
        db = jnp.sum(dx, axis=0, keepdims=True)
        sq = jnp.sum(err * err, axis=0, keepdims=True)

        @pl.when(i == 0)
        def _():
            dg_ref[...] = dg
            db_ref[...] = db
            sq_ref[...] = sq

        @pl.when(i > 0)
        def _():
            dg_ref[...] += dg
            db_ref[...] += db
            sq_ref[...] += sq

    row = pl.BlockSpec((tr, d), lambda i: (i, 0))
    vec = pl.BlockSpec((1, d), lambda i: (0, 0))
    return pl.pallas_call(
        body, name=name, grid=(t // tr,), in_specs=[row, vec, vec, row, row, vec, vec, row],
        out_specs=[row, row, row, vec, vec, vec],
        out_shape=[SDS((t, d), F32), SDS((t, d), BF16), SDS((t, d), BF16), SDS((1, d), F32), SDS((1, d), F32),
                   SDS((1, d), F32)],
        compiler_params=_cparams("arbitrary"),
    )(r3, g3, b3, gp, pp, g, b, target)


def _to_bf16(x, name):
    t, d = x.shape
    tr = _tile(t, 512, 16)
    row = pl.BlockSpec((tr, d), lambda i: (i, 0))

    def body(x_ref, o_ref):
        o_ref[...] = x_ref[...].astype(BF16)

    return pl.pallas_call(body, name=name, grid=(t // tr,), in_specs=[row], out_specs=row, out_shape=SDS((t, d), BF16),
                          compiler_params=_cparams("parallel"))(x)


def _concat_cols(parts, name):
    t = parts[0].shape[0]
    widths = [p_.shape[1] for p_ in parts]
    tr = _tile(t, 256, 16)

    def body(*refs):
        o_ref = refs[-1]
        at = 0
        for ref, wd in zip(refs[:-1], widths):
            o_ref[:, at:at + wd] = ref[...]
            at += wd

    return pl.pallas_call(
        body, name=name, grid=(t // tr,), in_specs=[pl.BlockSpec((tr, wd), lambda i: (i, 0)) for wd in widths],
        out_specs=pl.BlockSpec((tr, sum(widths)), lambda i: (i, 0)), out_shape=SDS((t, sum(widths)), parts[0].dtype),
        compiler_params=_cparams("parallel"),
    )(*parts)


def _merge_fwd(z, ma, mb, w, name):
    t = z.shape[0]
    tr = _tile(t, 256, 16)

    def body(gc_ref, gh_ref, ma_ref, mb_ref, o_ref):
        o_ref[...] = (_sigmoid(gc_ref[...]) * ma_ref[...] + _sigmoid(gh_ref[...]) * mb_ref[...]).astype(BF16)

    half = pl.BlockSpec((tr, w), lambda i, j: (i, j))
    return pl.pallas_call(
        body, name=name, grid=(t // tr, 2),
        in_specs=[pl.BlockSpec((tr, w), lambda i, j: (i, 7 + j)), pl.BlockSpec((tr, w), lambda i, j: (i, 9 + j)), half, half],
        out_specs=half, out_shape=SDS((t, 2 * w), BF16), compiler_params=_cparams("parallel", "parallel"),
    )(z, z, ma, mb)


def _merge_bwd(dmer, z, ma, mb, w, name):
    t = z.shape[0]
    tr = _tile(t, 256, 16)

    def body(d_ref, gc_ref, gh_ref, ma_ref, mb_ref, dma_ref, dmb_ref, dgc_ref, dgh_ref):
        dm = d_ref[...]
        sc = _sigmoid(gc_ref[...])
        sh = _sigmoid(gh_ref[...])
        dma_ref[...] = (dm * sc).astype(BF16)
        dmb_ref[...] = (dm * sh).astype(BF16)
        dgc_ref[...] = (dm * ma_ref[...] * sc * (1.0 - sc)).astype(BF16)
        dgh_ref[...] = (dm * mb_ref[...] * sh * (1.0 - sh)).astype(BF16)

    half = pl.BlockSpec((tr, w), lambda i, j: (i, j))
    return pl.pallas_call(
        body, name=name, grid=(t // tr, 2),
        in_specs=[half, pl.BlockSpec((tr, w), lambda i, j: (i, 7 + j)), pl.BlockSpec((tr, w), lambda i, j: (i, 9 + j)), half, half],
        out_specs=[half] * 4, out_shape=[SDS((t, 2 * w), BF16)] * 4, compiler_params=_cparams("parallel", "parallel"),
    )(dmer, z, z, ma, mb)


def _shift_down(x, s, row):
    return jnp.where(row >= s, pltpu.roll(x, s, axis=0), 0.0)


def _shift_up(x, s, row, t):
    return jnp.where(row < t - s, pltpu.roll(x, t - s, axis=0), 0.0)


def _conv_fwd(z, cw, w, name):
    t = z.shape[0]
    tc = LANES
    nb = w // tc

    def body(b_ref, c_ref, h_ref, w_ref, o_ref):
        u = c_ref[...] * h_ref[...]
        row = lax.broadcasted_iota(jnp.int32, u.shape, 0)
        cw_ = w_ref[...]
        conv = cw_[2:3, :] * u + cw_[1:2, :] * _shift_down(u, 1, row) + cw_[0:1, :] * _shift_down(u, 2, row)
        o_ref[...] = (b_ref[...] * conv).astype(BF16)

    col = lambda off: pl.BlockSpec((t, tc), lambda j: (0, off * nb + j))
    return pl.pallas_call(
        body, name=name, grid=(nb,), in_specs=[col(0), col(1), col(2), pl.BlockSpec((3, tc), lambda j: (0, j))],
        out_specs=pl.BlockSpec((t, tc), lambda j: (0, j)), out_shape=SDS((t, w), BF16), compiler_params=_cparams("parallel"),
    )(z, z, z, cw)


def _conv_bwd(dy, z, cw, w, name):
    t = z.shape[0]
    tc = LANES
    nb = w // tc

    def body(dy_ref, b_ref, c_ref, h_ref, w_ref, db_ref, dc_ref, dh_ref, dw_ref):
        c_, h_ = c_ref[...], h_ref[...]
        u = c_ * h_
        row = lax.broadcasted_iota(jnp.int32, u.shape, 0)
        cw_ = w_ref[...]
        u1 = _shift_down(u, 1, row)
        u2 = _shift_down(u, 2, row)
        dy_ = dy_ref[...]
        db_ref[...] = (dy_ * (cw_[2:3, :] * u + cw_[1:2, :] * u1 + cw_[0:1, :] * u2)).astype(BF16)
        dconv = dy_ * b_ref[...]
        du = cw_[2:3, :] * dconv + cw_[1:2, :] * _shift_up(dconv, 1, row, t) + cw_[0:1, :] * _shift_up(dconv, 2, row, t)
        dc_ref[...] = (du * h_).astype(BF16)
        dh_ref[...] = (du * c_).astype(BF16)
        dw_ref[0:1, :] = jnp.sum(dconv * u2, axis=0, keepdims=True)
        dw_ref[1:2, :] = jnp.sum(dconv * u1, axis=0, keepdims=True)
        dw_ref[2:3, :] = jnp.sum(dconv * u, axis=0, keepdims=True)

    col = lambda off: pl.BlockSpec((t, tc), lambda j: (0, off * nb + j))
    own = pl.BlockSpec((t, tc), lambda j: (0, j))
    wsp = pl.BlockSpec((3, tc), lambda j: (0, j))
    return pl.pallas_call(
        body, name=name, grid=(nb,), in_specs=[own, col(0), col(1), col(2), wsp], out_specs=[own, own, own, wsp],
        out_shape=[SDS((t, w), BF16)] * 3 + [SDS((3, w), F32)], compiler_params=_cparams("parallel"),
    )(dy, z, z, z, cw)


def _lower_bound(hg):
    mx = jnp.max(hg, axis=0, keepdims=True)
    e = jnp.exp(hg - mx)
    inv = 1.0 / jnp.sum(e, axis=0, keepdims=True)
    return e[0:1, :] * inv, e[1:2, :] * inv


def _chunk_cumsum(x, row):
    s = 1
    while s < CHUNK:
        x = x + jnp.where(row % CHUNK >= s, pltpu.roll(x, s, axis=0), 0.0)
        s *= 2
    return x


def _dot_nt(a, b):
    return lax.dot_general(a.astype(BF16), b.astype(BF16), (((1,), (1,)), ((), ())), preferred_element_type=F32)


def _dot_tn(a, b):
    return lax.dot_general(a.astype(BF16), b.astype(BF16), (((0,), (0,)), ((), ())), preferred_element_type=F32)


def _dot_nn(a, b):
    return jnp.dot(a.astype(BF16), b.astype(BF16), preferred_element_type=F32)


def _tril(x):
    r = lax.broadcasted_iota(jnp.int32, x.shape, 0)
    c = lax.broadcasted_iota(jnp.int32, x.shape, 1)
    return jnp.where(r >= c, x, 0.0)


HGRN_GROUP = 4
HGRN_ROWS = 512
HGRN_UNROLL = 2


def _unrolled_loop(n, step, init):
    assert n % HGRN_UNROLL == 0

    def trip(i, carry):
        for u in range(HGRN_UNROLL):
            carry = step(i * HGRN_UNROLL + u, carry)
        return carry

    return lax.fori_loop(0, n // HGRN_UNROLL, trip, init)


def _hgrn_chunk_inputs(q_ref, f_ref, cum_ref, lb, rows, ln):
    qr = q_ref[rows, ln]
    q = qr * _sigmoid(qr)
    f = lb + (1.0 - lb) * _sigmoid(f_ref[rows, ln])
    return q, 1.0 - f, cum_ref[rows, ln]


def _hgrn_fwd(z, hg, nw, w, name, comm=()):
    t = z.shape[0]
    nh = w // HEAD
    gh = _tile(nh, HGRN_GROUP, 1)
    gw = gh * HEAD
    ngrp = nh // gh
    tb = _tile(t, HGRN_ROWS, CHUNK)
    ncb = tb // CHUNK

    def body(q_ref, f_ref, i_ref, g_ref, hg_ref, nw_ref, y_ref, o_ref, st_ref, cum_ref, *s_refs):
        lb_all, _ = _lower_bound(hg_ref[...])
        row = lax.broadcasted_iota(jnp.int32, (tb, gw), 0)
        cum_ref[...] = _chunk_cumsum(jnp.log(lb_all + (1.0 - lb_all) * _sigmoid(f_ref[...])), row)

        @pl.when(pl.program_id(1) == 0)
        def _():
            for s_ref in s_refs:
                s_ref[...] = jnp.zeros_like(s_ref)

        def step(c, carry):
            rows = pl.ds(pl.multiple_of(c * CHUNK, CHUNK), CHUNK)
            for g in range(gh):
                ln = slice(g * HEAD, (g + 1) * HEAD)
                lb = lb_all[:, ln]
                q, k, cum = _hgrn_chunk_inputs(q_ref, f_ref, cum_ref, lb, rows, ln)
                v = i_ref[rows, ln]
                last = cum[CHUNK - 1:CHUNK, :]
                mid = cum[CHUNK // 2 - 1:CHUNK // 2, :]
                st = s_refs[g][...]
                st_ref[g, c] = st.astype(BF16)
                scores = _tril(_dot_nt(q * jnp.exp(cum - mid), k * jnp.exp(mid - cum)))
                o_ref[rows, ln] = _dot_nt(q * jnp.exp(cum), st) + _dot_nn(scores, v)
                s_refs[g][...] = st * jnp.exp(last) + _dot_tn(v, k * jnp.exp(last - cum))
            return carry

        _unrolled_loop(ncb, step, 0)
        for g in range(gh):
            ln = slice(g * HEAD, (g + 1) * HEAD)
            o = o_ref[:, ln]
            n = o * lax.rsqrt(jnp.mean(o * o, axis=-1, keepdims=True) + RMS_EPS)
            gr = g_ref[:, ln]
            y_ref[:, ln] = (n * nw_ref[...] * gr * _sigmoid(gr)).astype(BF16)

    col = lambda off: pl.BlockSpec((tb, gw), lambda h, j: (j, off * ngrp + h))
    own = pl.BlockSpec((tb, gw), lambda h, j: (j, h))
    return _hosted_call(
        body, comm, name=name, grid=(ngrp, t // tb),
        in_specs=[col(3), col(4), col(5), col(6), pl.BlockSpec((2, gw), lambda h, j: (0, h)),
                  pl.BlockSpec((1, HEAD), lambda h, j: (0, 0))],
        out_specs=[own, own, pl.BlockSpec((gh, ncb, HEAD, HEAD), lambda h, j: (h, j, 0, 0))],
        out_shape=[SDS((t, w), BF16), SDS((t, w), F32), SDS((nh, t // CHUNK, HEAD, HEAD), BF16)],
        scratch_shapes=[pltpu.VMEM((tb, gw), F32)] + [pltpu.VMEM((HEAD, HEAD), F32)] * gh,
        operands=(z, z, z, z, hg, nw), parallel=1)


def _hgrn_bwd(dy, z, o, states, hg, nw, w, name, comm=()):
    t = z.shape[0]
    nh = w // HEAD
    gh = _tile(nh, HGRN_GROUP, 1)
    gw = gh * HEAD
    ngrp = nh // gh
    tb = _tile(t, HGRN_ROWS, CHUNK)
    ncb = tb // CHUNK
    nt = t // tb

    def body(dy_ref, q_ref, f_ref, i_ref, g_ref, o_ref, st_ref, hg_ref, nw_ref,
             dq_ref, df_ref, di_ref, dg_ref, dhg_ref, dnw_ref, cum_ref, do_ref, *ds_refs):
        lb_all, s1_all = _lower_bound(hg_ref[...])
        row = lax.broadcasted_iota(jnp.int32, (tb, gw), 0)
        crow = lax.broadcasted_iota(jnp.int32, (CHUNK, HEAD), 0)
        cum_ref[...] = _chunk_cumsum(jnp.log(lb_all + (1.0 - lb_all) * _sigmoid(f_ref[...])), row)

        @pl.when(pl.program_id(1) == 0)
        def _():
            for ds_ref in ds_refs:
                ds_ref[...] = jnp.zeros_like(ds_ref)
            dhg_ref[...] = jnp.zeros_like(dhg_ref)
            dnw_ref[...] = jnp.zeros_like(dnw_ref)

        for g in range(gh):
            ln = slice(g * HEAD, (g + 1) * HEAD)
            o_ = o_ref[:, ln]
            rstd = lax.rsqrt(jnp.mean(o_ * o_, axis=-1, keepdims=True) + RMS_EPS)
            n = o_ * rstd
            gr = g_ref[:, ln]
            sg = _sigmoid(gr)
            dy_ = dy_ref[:, ln]
            dg_ref[:, ln] = (dy_ * n * nw_ref[...] * (sg * (1.0 + gr * (1.0 - sg)))).astype(BF16)
            dsil = dy_ * gr * sg
            dnw_ref[:, ln] += jnp.sum(dsil * n, axis=0, keepdims=True)
            dn = dsil * nw_ref[...]
            do_ref[:, ln] = rstd * (dn - n * jnp.mean(dn * n, axis=-1, keepdims=True))

        def step(cc, dlbs):
            c = ncb - 1 - cc
            rows = pl.ds(pl.multiple_of(c * CHUNK, CHUNK), CHUNK)
            new = []
            for g in range(gh):
                ln = slice(g * HEAD, (g + 1) * HEAD)
                lb = lb_all[:, ln]
                qr = q_ref[rows, ln]
                sq = _sigmoid(qr)
                q = qr * sq
                sf = _sigmoid(f_ref[rows, ln])
                f = lb + (1.0 - lb) * sf
                k = 1.0 - f
                cum = cum_ref[rows, ln]
                v = i_ref[rows, ln]
                do = do_ref[rows, ln]
                last = cum[CHUNK - 1:CHUNK, :]
                mid = cum[CHUNK // 2 - 1:CHUNK // 2, :]
                eg = jnp.exp(cum)
                em = jnp.exp(cum - mid)
                enm = jnp.exp(mid - cum)
                elc = jnp.exp(last - cum)
                qm, km, kl = q * em, k * enm, k * elc
                ds = ds_refs[g][...]
                a = _tril(_dot_nt(qm, km))
                da = _tril(_dot_nt(do, v))
                di_ref[rows, ln] = (_dot_tn(a, do) + _dot_nt(kl, ds)).astype(BF16)
                st = st_ref[g, c]
                dkl = _dot_nn(v, ds)
                dq = _dot_nn(do, st) * eg + _dot_nn(da, km) * em
                dk = _dot_tn(da, qm) * enm + dkl * elc
                el = jnp.exp(last)
                ds_refs[g][...] = ds * el + _dot_tn(do, q * eg)
                dlast = jnp.sum(kl * dkl, axis=0, keepdims=True) + el * jnp.sum(ds * st.astype(F32), axis=0, keepdims=True)
                x = q * dq - k * dk + jnp.where(crow == CHUNK - 1, dlast, 0.0)
                s = 1
                while s < CHUNK:
                    x = x + _shift_up(x, s, crow, CHUNK)
                    s *= 2
                df = x / f - dk
                dq_ref[rows, ln] = (dq * (sq * (1.0 + qr * (1.0 - sq)))).astype(BF16)
                df_ref[rows, ln] = (df * (1.0 - lb) * sf * (1.0 - sf)).astype(BF16)
                new.append(dlbs[g] + jnp.sum(df * (1.0 - sf), axis=0, keepdims=True))
            return tuple(new)

        dlbs = _unrolled_loop(ncb, step, tuple(jnp.zeros((1, HEAD), F32) for _ in range(gh)))
        for g in range(gh):
            ln = slice(g * HEAD, (g + 1) * HEAD)
            dlb = dlbs[g] * lb_all[:, ln] * s1_all[:, ln]
            dhg_ref[0:1, ln] += dlb
            dhg_ref[1:2, ln] -= dlb

    col = lambda off: pl.BlockSpec((tb, gw), lambda h, j: (nt - 1 - j, off * ngrp + h))
    own = pl.BlockSpec((tb, gw), lambda h, j: (nt - 1 - j, h))
    hsp = pl.BlockSpec((2, gw), lambda h, j: (0, h))
    return _hosted_call(
        body, comm, name=name, grid=(ngrp, nt),
        in_specs=[own, col(3), col(4), col(5), col(6), own,
                  pl.BlockSpec((gh, ncb, HEAD, HEAD), lambda h, j: (h, nt - 1 - j, 0, 0)),
                  hsp, pl.BlockSpec((1, HEAD), lambda h, j: (0, 0))],
        out_specs=[own, own, own, own, hsp, pl.BlockSpec((1, gw), lambda h, j: (0, h))],
        out_shape=[SDS((t, w), BF16)] * 4 + [SDS((2, w), F32), SDS((1, w), F32)],
        scratch_shapes=[pltpu.VMEM((tb, gw), F32)] * 2 + [pltpu.VMEM((HEAD, HEAD), F32)] * gh,
        operands=(dy, z, z, z, z, o, states, hg, nw), parallel=1)


def _cast_pad(wt, n_pad, meta, sp, name, comm=()):
    _, r, n = wt.shape
    g, p, per = meta
    tr = _tile(r, max(16, (3 << 19) // n_pad // 16 * 16), 16)

    def body(w_ref, o_ref):
        if n_pad != n:
            o_ref[...] = jnp.zeros(o_ref.shape, o_ref.dtype)
        o_ref[:, 0:n] = w_ref[...].astype(BF16)

    return _hosted_call(
        body, comm, name=name, grid=(r // tr,), in_specs=[pl.BlockSpec((None, tr, n), lambda i, sp: (0, i, 0))],
        out_specs=[pl.BlockSpec((None, tr, n_pad), lambda i, sp: (sp[1] // per, ((sp[1] % per) * r) // tr + i, 0))],
        out_shape=[SDS((g, p, n_pad), BF16)], scratch_shapes=[], operands=(wt,), parallel=1, prefetch=sp)


def _cast_pad_t(wt_t, n_pad, meta, sp, name, comm=()):
    _, n, r = wt_t.shape
    g, p, per = meta
    tc = _tile(r, 256, LANES)

    def body(w_ref, o_ref):
        for lo in range(0, n_pad, LANES):
            rows = min(LANES, n - lo)
            piece = w_ref[lo:lo + rows, :]
            if rows < LANES:
                piece = jnp.concatenate([piece, jnp.zeros((LANES - rows, tc), F32)], axis=0)
            o_ref[:, lo:lo + LANES] = piece.T.astype(BF16)

    return _hosted_call(
        body, comm, name=name, grid=(r // tc,), in_specs=[pl.BlockSpec((None, n, tc), lambda i, sp: (0, 0, i))],
        out_specs=[pl.BlockSpec((None, tc, n_pad), lambda i, sp: (sp[1] // per, ((sp[1] % per) * r) // tc + i, 0))],
        out_shape=[SDS((g, p, n_pad), BF16)], scratch_shapes=[], operands=(wt_t,), parallel=1, prefetch=sp)


def _adam_math(w, g, m, v):
    m2 = ADAM_B1 * m + (1.0 - ADAM_B1) * g
    v2 = ADAM_B2 * v + (1.0 - ADAM_B2) * (g * g)
    c1 = 1.0 / (1.0 - ADAM_B1 ** ADAM_STEP)
    c2 = 1.0 / (1.0 - ADAM_B2 ** ADAM_STEP)
    return -ADAM_LR * ((m2 * c1) / (jnp.sqrt(v2 * c2) + ADAM_EPS) + ADAM_WD * w), m2, v2


def _adamw_t(wt_t, g, m_t, v_t, name):
    _, n, r = wt_t.shape
    ng = g.shape[1]
    tc = LANES

    def body(w_ref, g_ref, m_ref, v_ref, go_ref, d_ref, mo_ref, vo_ref, gt_ref):
        for lo in range(0, ng, LANES):
            gt_ref[lo:lo + LANES, :] = g_ref[:, lo:lo + LANES].T
        g_ = gt_ref[0:n, :]
        delta, m2, v2 = _adam_math(w_ref[...], g_, m_ref[...], v_ref[...])
        go_ref[...] = g_
        d_ref[...] = delta
        mo_ref[...] = m2
        vo_ref[...] = v2

    blk = pl.BlockSpec((None, n, tc), lambda i: (0, 0, i))
    return pl.pallas_call(
        body, name=name, grid=(r // tc,), in_specs=[blk, pl.BlockSpec((tc, ng), lambda i: (i, 0)), blk, blk],
        out_specs=[blk] * 4, out_shape=[SDS(wt_t.shape, F32)] * 4, scratch_shapes=[pltpu.VMEM((ng, tc), F32)],
        compiler_params=_cparams("parallel"),
    )(wt_t, g, m_t, v_t)


def _adamw(wt, g, m, v, name):
    lead = (None,) * (wt.ndim - 2)
    zero = (0,) * (wt.ndim - 2)
    r, n = wt.shape[-2:]
    ng = g.shape[1]
    nct = 2 if ng == n and n % (2 * LANES) == 0 else 1
    tc, tg = n // nct, ng // nct
    tr = _tile(r, max(8, (3 << 17) // tg // 8 * 8), 8)

    def body(w_ref, g_ref, m_ref, v_ref, go_ref, d_ref, mo_ref, vo_ref):
        g_ = g_ref[:, 0:tc]
        delta, m2, v2 = _adam_math(w_ref[...], g_, m_ref[...], v_ref[...])
        go_ref[...] = g_
        d_ref[...] = delta
        mo_ref[...] = m2
        vo_ref[...] = v2

    blk = pl.BlockSpec(lead + (tr, tc), lambda i, j: zero + (i, j))
    return pl.pallas_call(
        body, name=name, grid=(r // tr, nct), in_specs=[blk, pl.BlockSpec((tr, tg), lambda i, j: (i, j)), blk, blk],
        out_specs=[blk] * 4, out_shape=[SDS(wt.shape, F32)] * 4, compiler_params=_cparams("parallel", "parallel"),
    )(wt, g, m, v)


def _place():
    x, y, c = lax.axis_index("x"), lax.axis_index("y"), lax.axis_index("c")
    return x, y, c, 2 * x + y


def _chip_dev(k, c):
    return (k // 2, k % 2, c)


def _half(ref, j, h, rows, per):
    return ref.at[j // per, pl.ds((j % per) * rows + h * (rows // 2), rows // 2)]


def _gather_stage(bufs, metas, rows_of, ici_parts, fwd_parts, zero_pad):
    nw = len(bufs)
    ici_on = [i for i in range(nw) if ici_parts[i] is not None]
    fwd_on = [i for i in range(nw) if fwd_parts[i] is not None]
    pad_jobs = [(i, gi) for i in ici_on if ici_parts[i][0] == 0 and metas[i][1] > metas[i][2] * rows_of[i]
                for gi in range(metas[i][0])]

    def part_of(ref, i, j, h, part):
        per = metas[i][2]
        p, np_ = part
        pr = rows_of[i] // 2 // np_
        return ref.at[j // per, pl.ds((j % per) * rows_of[i] + h * (rows_of[i] // 2) + p * pr, pr)]

    def descriptors(ins, outs, sems):
        src, zp, dst = ins[:nw], ins[nw], outs
        pads, send, recv, fsend, frecv = sems
        x, y, c, me = _place()

        def pad(n):
            i, gi = pad_jobs[n]
            extra = metas[i][1] - metas[i][2] * rows_of[i]
            return pltpu.make_async_copy(zp.at[pl.ds(0, extra)], dst[i].at[gi, pl.ds(metas[i][2] * rows_of[i], extra)], pads.at[n])

        def ici(i, r, frm):
            return pltpu.make_async_remote_copy(
                src_ref=part_of(src[i], i, me, c, ici_parts[i]), dst_ref=part_of(dst[i], i, frm, c, ici_parts[i]),
                send_sem=send.at[i, r - 1], recv_sem=recv.at[i, r - 1], device_id=_chip_dev((me + r) % N_CHIPS, c),
                device_id_type=MESH)

        def d2d(i, r, frm, h):
            blk = part_of(dst[i], i, frm, h, fwd_parts[i])
            return pltpu.make_async_remote_copy(src_ref=blk, dst_ref=blk, send_sem=fsend.at[i, r - 1],
                                                recv_sem=frecv.at[i, r - 1], device_id=(x, y, 1 - c), device_id_type=MESH)

        return pad, ici, d2d, c, me

    def start(ins, outs, sems):
        pad, ici, d2d, c, me = descriptors(ins, outs, sems)
        for n in range(len(pad_jobs)):
            pad(n).start()
        for i in fwd_on:
            for r in range(1, N_CHIPS):
                d2d(i, r, (me - r) % N_CHIPS, c).start()
        for i in ici_on:
            for r in range(1, N_CHIPS):
                ici(i, r, me).start()

    def finish(ins, outs, sems):
        pad, ici, d2d, c, me = descriptors(ins, outs, sems)
        for i in fwd_on:
            for r in range(1, N_CHIPS):
                d2d(i, r, (me - r) % N_CHIPS, 1 - c).wait_recv()
                d2d(i, r, (me - r) % N_CHIPS, c).wait_send()
        for i in ici_on:
            for r in range(1, N_CHIPS):
                ici(i, r, (me - r) % N_CHIPS).wait_recv()
                ici(i, r, me).wait_send()
        for n in range(len(pad_jobs)):
            pad(n).wait()

    return _Stage(ins=list(bufs) + [zero_pad], out_shapes=[SDS(b.shape, b.dtype) for b in bufs],
                  aliases={i: i for i in range(nw)},
                  sems=[pltpu.SemaphoreType.DMA((max(len(pad_jobs), 1),))] + [pltpu.SemaphoreType.DMA((nw, N_CHIPS - 1))] * 4,
                  start=start, finish=finish)


def _gather_small(packed, name):
    r, n = packed.shape

    def body(src, dst, send, recv):
        x, y, c, me = _place()
        dst[me] = src[...]
        cps = []
        for d in range(1, N_CHIPS):
            cp = pltpu.make_async_remote_copy(src_ref=src, dst_ref=dst.at[me], send_sem=send.at[d - 1], recv_sem=recv.at[d - 1],
                                              device_id=_chip_dev((me + d) % N_CHIPS, c), device_id_type=MESH)
            cp.start()
            cps.append(cp)
        for d in range(1, N_CHIPS):
            pltpu.make_async_remote_copy(src_ref=src, dst_ref=dst.at[(me - d) % N_CHIPS], send_sem=send.at[d - 1],
                                         recv_sem=recv.at[d - 1], device_id=_chip_dev((me + d) % N_CHIPS, c),
                                         device_id_type=MESH).wait_recv()
        for cp in cps:
            cp.wait_send()

    return pl.pallas_call(
        body, name=name, in_specs=[VMEM_SPEC], out_specs=VMEM_SPEC, out_shape=SDS((N_CHIPS, r, n), F32),
        scratch_shapes=[pltpu.SemaphoreType.DMA((N_CHIPS - 1,))] * 2,
    )(packed)


def _all_reduce_small(packed, name):
    r, n = packed.shape

    def body(src, out, slots, send, recv):
        x, y, c, me = _place()
        idx = 2 * me + c
        slots[idx] = src[...]
        cps = []

        def peer(d):
            p = (idx + d) % N_DEV
            return (p // 4, (p // 2) % 2, p % 2)

        for d in range(1, N_DEV):
            cp = pltpu.make_async_remote_copy(src_ref=src, dst_ref=slots.at[idx], send_sem=send.at[d - 1], recv_sem=recv.at[d - 1],
                                              device_id=peer(d), device_id_type=MESH)
            cp.start()
            cps.append(cp)
        for d in range(1, N_DEV):
            pltpu.make_async_remote_copy(src_ref=src, dst_ref=slots.at[(idx - d) % N_DEV], send_sem=send.at[d - 1],
                                         recv_sem=recv.at[d - 1], device_id=peer(d), device_id_type=MESH).wait_recv()
        for cp in cps:
            cp.wait_send()
        acc = slots[0]
        for k in range(1, N_DEV):
            acc = acc + slots[k]
        out[...] = acc

    return pl.pallas_call(
        body, name=name, in_specs=[VMEM_SPEC], out_specs=VMEM_SPEC, out_shape=SDS((r, n), F32),
        scratch_shapes=[pltpu.VMEM((N_DEV, r, n), F32)] + [pltpu.SemaphoreType.DMA((N_DEV - 1,))] * 2,
    )(packed)


def _simple_stage(ins, out_shapes, aliases, n_copies, copies):
    def start(ins_, outs, sems):
        for cp in copies(ins_, outs, *sems):
            cp.start()

    def finish(ins_, outs, sems):
        for cp in copies(ins_, outs, *sems):
            cp.wait()

    return _Stage(ins=list(ins), out_shapes=list(out_shapes), aliases=aliases,
                  sems=[pltpu.SemaphoreType.DMA((n_copies,))] * 2, start=start, finish=finish)


def _rs_pair_exchange(grads, metas, rows_of):
    nw = len(grads)

    def copies(src, dst, send, recv):
        x, y, c, me = _place()
        return [pltpu.make_async_remote_copy(
            src_ref=_half(src[i], j, 1 - c, rows_of[i], metas[i][2]), dst_ref=dst[i].at[j], send_sem=send.at[i * N_CHIPS + j],
            recv_sem=recv.at[i * N_CHIPS + j], device_id=(x, y, 1 - c), device_id_type=MESH)
            for i in range(nw) for j in range(N_CHIPS)]

    out_shapes = [SDS((N_CHIPS, rows_of[i] // 2, g.shape[2]), g.dtype) for i, g in enumerate(grads)]
    return _simple_stage(grads, out_shapes, {}, nw * N_CHIPS, copies)


def _rs_pair_add(g, got, meta, rows, sp, name):
    per = meta[2]
    n = g.shape[2]
    hr = rows // 2
    tr = _tile(hr, max(16, (3 << 19) // n // 16 * 16), 16)

    def body(sp_ref, g_ref, got_ref, snd_ref, own_ref):
        j = pl.program_id(1)
        s = g_ref[...].astype(F32) + got_ref[...].astype(F32)
        snd_ref[...] = s.astype(BF16)

        @pl.when(j == sp_ref[1])
        def _():
            own_ref[...] = s

    grid_spec = pltpu.PrefetchScalarGridSpec(
        num_scalar_prefetch=1, grid=(hr // tr, N_CHIPS),
        in_specs=[pl.BlockSpec((None, tr, n), lambda i, j, sp: (j // per, ((j % per) * rows + sp[0] * hr) // tr + i, 0)),
                  pl.BlockSpec((None, tr, n), lambda i, j, sp: (j, i, 0))],
        out_specs=[pl.BlockSpec((None, tr, n), lambda i, j, sp: (j, i, 0)), pl.BlockSpec((tr, n), lambda i, j, sp: (i, 0))])
    return pl.pallas_call(
        body, name=name, grid_spec=grid_spec, out_shape=[SDS((N_CHIPS, hr, n), BF16), SDS((hr, n), F32)],
        compiler_params=_cparams("parallel", "arbitrary"),
    )(sp, g, got)


def _rs_chip_exchange(sends, part=(0, 1), prev=None):
    nw = len(sends)
    p, np_ = part

    def copies(src, dst, send, recv):
        x, y, c, me = _place()
        cps = []
        for i in range(nw):
            pr = sends[i].shape[1] // np_
            for r in range(1, N_CHIPS):
                cps.append(pltpu.make_async_remote_copy(
                    src_ref=src[i].at[(me + r) % N_CHIPS, pl.ds(p * pr, pr)], dst_ref=dst[i].at[r - 1, pl.ds(p * pr, pr)],
                    send_sem=send.at[i * (N_CHIPS - 1) + r - 1], recv_sem=recv.at[i * (N_CHIPS - 1) + r - 1],
                    device_id=_chip_dev((me + r) % N_CHIPS, c), device_id_type=MESH))
        return cps

    out_shapes = [SDS((N_CHIPS - 1,) + s.shape[1:], BF16) for s in sends]
    if prev is None:
        return _simple_stage(sends, out_shapes, {}, nw * (N_CHIPS - 1), copies)
    return _simple_stage(list(sends) + list(prev), out_shapes, {nw + i: i for i in range(nw)}, nw * (N_CHIPS - 1), copies)


def _rs_chip_add(own, got, sp, name):
    hr, n = own.shape
    tr = _tile(hr, max(16, (3 << 19) // n // 16 * 16), 16)

    def body(sp_ref, own_ref, got_ref, o_ref):
        acc = own_ref[...]
        for r in range(N_CHIPS - 1):
            acc = acc + got_ref[r].astype(F32)
        o_ref[...] = acc

    grid_spec = pltpu.PrefetchScalarGridSpec(
        num_scalar_prefetch=1, grid=(hr // tr,),
        in_specs=[pl.BlockSpec((tr, n), lambda i, sp: (i, 0)), pl.BlockSpec((N_CHIPS - 1, tr, n), lambda i, sp: (0, i, 0))],
        out_specs=pl.BlockSpec((tr, n), lambda i, sp: (sp[0] * (hr // tr) + i, 0)))
    return pl.pallas_call(body, name=name, grid_spec=grid_spec, out_shape=SDS((2 * hr, n), F32),
                          compiler_params=_cparams("parallel"))(sp, own, got)


def _rs_pair_share(blocks):
    nw = len(blocks)

    def copies(src, dst, send, recv):
        x, y, c, me = _place()
        cps = []
        for i in range(nw):
            hr = src[i].shape[0] // 2
            cps.append(pltpu.make_async_remote_copy(
                src_ref=src[i].at[pl.ds(c * hr, hr)], dst_ref=dst[i].at[pl.ds(c * hr, hr)], send_sem=send.at[i],
                recv_sem=recv.at[i], device_id=(x, y, 1 - c), device_id_type=MESH))
        return cps

    return _simple_stage(blocks, [SDS(b.shape, b.dtype) for b in blocks], {i: i for i in range(nw)}, nw, copies)


def kernel(x, p, ln_g, ln_b, ffn1_w_in, ffn1_w_out, mix_w_in, conv_w, hg_lower_bound, hg_norm_w, branch_w_conv, branch_w_hgrn, mix_w_out, ffn2_w_in, ffn2_w_out, ple_w_gate, ple_w_proj, loss_target, m_ln_g, m_ln_b, m_ffn1_w_in, m_ffn1_w_out, m_mix_w_in, m_conv_w, m_hg_lower_bound, m_hg_norm_w, m_branch_w_conv, m_branch_w_hgrn, m_mix_w_out, m_ffn2_w_in, m_ffn2_w_out, m_ple_w_gate, m_ple_w_proj, v_ln_g, v_ln_b, v_ffn1_w_in, v_ffn1_w_out, v_mix_w_in, v_conv_w, v_hg_lower_bound, v_hg_norm_w, v_branch_w_conv, v_branch_w_hgrn, v_mix_w_out, v_ffn2_w_in, v_ffn2_w_out, v_ple_w_gate, v_ple_w_proj):
    assert ln_g.shape[0] == DEPTH and x.shape[0] == 1 and p.shape[:2] == (1, 1)
    t, d = x.shape[1], x.shape[2]
    w = d // 2
    x0 = x.reshape(t, d)
    x0b = _to_bf16(x0, "x_bf16")
    pe = p.reshape(t, p.shape[-1])
    target = loss_target.reshape(t, d)
    cx, cy, cc = lax.axis_index("x"), lax.axis_index("y"), lax.axis_index("c")
    chip = 2 * cx + cy
    sp = jnp.stack([cc, chip]).astype(jnp.int32)

    big = dict(ffn1_w_in=ffn1_w_in, ffn1_w_out=ffn1_w_out, mix_w_in=mix_w_in, branch_w_conv=branch_w_conv,
               branch_w_hgrn=branch_w_hgrn, mix_w_out=mix_w_out, ffn2_w_in=ffn2_w_in, ffn2_w_out=ffn2_w_out,
               ple_w_gate=ple_w_gate, ple_w_proj=ple_w_proj)
    moments = dict(ffn1_w_in=(m_ffn1_w_in, v_ffn1_w_in), ffn1_w_out=(m_ffn1_w_out, v_ffn1_w_out), mix_w_in=(m_mix_w_in, v_mix_w_in),
                   branch_w_conv=(m_branch_w_conv, v_branch_w_conv), branch_w_hgrn=(m_branch_w_hgrn, v_branch_w_hgrn),
                   mix_w_out=(m_mix_w_out, v_mix_w_out), ffn2_w_in=(m_ffn2_w_in, v_ffn2_w_in), ffn2_w_out=(m_ffn2_w_out, v_ffn2_w_out),
                   ple_w_gate=(m_ple_w_gate, v_ple_w_gate), ple_w_proj=(m_ple_w_proj, v_ple_w_proj))
    names = list(big)

    n_loc = ffn1_w_in.shape[-1]
    n_pad = -(-n_loc // LANES) * LANES
    assert mix_w_in.shape[-1] % LANES == 0 and ffn1_w_out.shape[1] * 2 == n_loc
    pad_cols = dict(ffn1_w_in=n_pad, ffn2_w_in=n_pad)
    meta = {k: (N_CHIPS, big[k].shape[1], 1) for k in names}
    meta["ffn1_w_out"] = meta["ffn2_w_out"] = (2, n_pad, 2)
    rows = {k: big[k].shape[1] for k in names}
    swap = lambda a: jnp.transpose(a, (0, 2, 1))
    wbuf = {}
    zero_pad = jnp.zeros((max(n_pad - n_loc, 16), d), BF16)

    def cast(k, comm=()):
        if k in pad_cols:
            return _cast_pad_t(swap(big[k]), pad_cols[k], meta[k], sp, "cast_" + k, comm=comm)
        return _cast_pad(big[k], big[k].shape[2], meta[k], sp, "cast_" + k, comm=comm)

    def gather(ici=(), fwd=()):
        ks = list(dict.fromkeys([k for k, _, _ in ici] + [k for k, _, _ in fwd]))
        ip = {k: (p_, n_) for k, p_, n_ in ici}
        fp = {k: (p_, n_) for k, p_, n_ in fwd}
        return _gather_stage([wbuf[k] for k in ks], [meta[k] for k in ks], [rows[k] for k in ks], [ip.get(k) for k in ks],
                             [fp.get(k) for k in ks], zero_pad), ks

    def gathered(ks, outs):
        wbuf.update(zip(ks, outs))

    def w3(k):
        return wbuf[k]

    def w2(k):
        return wbuf[k].reshape(-1, wbuf[k].shape[2])

    dq, wq = d // N_CHIPS, w // N_CHIPS
    small = jnp.concatenate([ln_g[0], ln_b[0], jnp.pad(conv_w[0], ((0, 5), (0, dq - wq)))], axis=0)
    small = _gather_small(small, "gather_small")
    lng = small[:, 0:4, :].transpose(1, 0, 2).reshape(4, 1, d)
    lnb = small[:, 4:8, :].transpose(1, 0, 2).reshape(4, 1, d)
    cw = small[:, 8:11, :wq].transpose(1, 0, 2).reshape(3, w)
    hg = hg_lower_bound
    nw_ = hg_norm_w

    one = lambda *ks_: [(k, 0, 1) for k in ks_]
    wbuf["ffn1_w_in"] = cast("ffn1_w_in")
    carriers = ["ple_w_proj", "ffn1_w_out", "mix_w_in", None] + [k for k in names if k not in ("ffn1_w_in", "ple_w_proj", "ffn1_w_out", "mix_w_in")]
    assert len(carriers) > FIRST_GATHER_PARTS
    for step, k in enumerate(carriers):
        ici = [("ffn1_w_in", step, FIRST_GATHER_PARTS)] if step < FIRST_GATHER_PARTS else []
        fwd = [("ffn1_w_in", step - 1, FIRST_GATHER_PARTS)] if 1 <= step <= FIRST_GATHER_PARTS else []
        ici += one("ple_w_proj") if step == 1 else []
        fwd += one("ple_w_proj") if step == 2 else []
        if ici or fwd:
            st, ks = gather(ici, fwd)
            if k is None:
                pp, got = _mm(pe, w3("ple_w_proj"), name="ple_proj", b_blocked=True, tn=512, comm=[st])
            else:
                wbuf[k], got = cast(k, comm=[st])
            gathered(ks, got)
        else:
            wbuf[k] = cast(k)
    st, ks = gather(ici=one("ffn1_w_out") + [("mix_w_in", 0, 2)])
    z1, got = _mm(x0b, w3("ffn1_w_in"), name="ffn1_in", b_blocked=True, out_dtype=BF16, tm=1024, comm=[st])
    gathered(ks, got)
    st, ks = gather(fwd=one("ffn1_w_out") + [("mix_w_in", 0, 2)])
    h1, got = _swiglu_fwd(z1, "ffn1_act", comm=[st])
    gathered(ks, got)
    st, ks = gather(ici=[("mix_w_in", 1, 2)])
    y1, got = _mm(h1, w2("ffn1_w_out"), name="ffn1_out", tm=1024, tn=1024, tk=2816, comm=[st])
    gathered(ks, got)
    st, ks = gather(fwd=[("mix_w_in", 1, 2)])
    (r1, x1b), got = _ln_fwd(x0, y1, lng[0], lnb[0], None, None, 0.5, "ln0", comm=[st])
    gathered(ks, got)
    mixo_w = one("branch_w_conv", "branch_w_hgrn", "mix_w_out")
    st, ks = gather(ici=mixo_w + [("ffn2_w_in", 0, 2)])
    z, got = _mm(x1b, w3("mix_w_in"), name="mix_in", b_blocked=True, tm=1024, comm=[st])
    gathered(ks, got)
    ya = _conv_fwd(z, cw, w, "conv_fwd")
    st, ks = gather(ici=[("ffn2_w_in", 1, 2)], fwd=mixo_w + [("ffn2_w_in", 0, 2)])
    (yb, o_h, states), got = _hgrn_fwd(z, hg, nw_, w, "hgrn_fwd", comm=[st])
    gathered(ks, got)
    ma = _mm(ya, w3("branch_w_conv"), name="branch_conv", b_blocked=True, tn=512)
    mb = _mm(yb, w3("branch_w_hgrn"), name="branch_hgrn", b_blocked=True, tn=512)
    merged = _merge_fwd(z, ma, mb, w, "merge_fwd")
    st, ks = gather(fwd=[("ffn2_w_in", 1, 2)])
    y2, got = _mm(merged, w2("mix_w_out"), name="mix_out", tn=1024, comm=[st])
    gathered(ks, got)
    r2, x2b = _ln_fwd(r1, y2, lng[1], lnb[1], lng[0], lnb[0], 1.0, "ln1")
    late = one("ffn2_w_out", "ple_w_gate")
    st, ks = gather(ici=late)
    z3, got = _mm(x2b, w3("ffn2_w_in"), name="ffn2_in", b_blocked=True, out_dtype=BF16, tm=1024, comm=[st])
    gathered(ks, got)
    st, ks = gather(fwd=late)
    h3, got = _swiglu_fwd(z3, "ffn2_act", comm=[st])
    gathered(ks, got)
    y3 = _mm(h3, w2("ffn2_w_out"), name="ffn2_out", tm=1024, tn=1024, tk=2816)
    r3, x3b = _ln_fwd(r2, y3, lng[2], lnb[2], lng[1], lnb[1], 0.5, "ln2")
    gp = _mm(x3b, w2("ple_w_gate"), name="ple_gate", tn=1024)
    dr4, dgp, dpp, dg3, db3, sq = _tail(r3, lng[2], lnb[2], gp, pp, lng[3], lnb[3], target, "tail")

    grads, sends, owns, blocks, outs = {}, {}, {}, {}, {}

    def pair_exchange(*ks):
        return _rs_pair_exchange([grads[k] for k in ks], [meta[k] for k in ks], [rows[k] for k in ks])

    def pair_add(ks, got):
        for k, g_ in zip(ks, got):
            sends[k], owns[k] = _rs_pair_add(grads[k], g_, meta[k], rows[k], sp, "rs_pair_add_" + k)

    def chip_exchange(*ks):
        return _rs_chip_exchange([sends[k] for k in ks])

    def chip_add(ks, got):
        for k, g_ in zip(ks, got):
            blocks[k] = _rs_chip_add(owns[k], g_, sp, "rs_chip_add_" + k)

    def pair_share(*ks):
        return _rs_pair_share([blocks[k] for k in ks])

    def update(ks, full):
        for k, g_ in zip(ks, full):
            m_, v_ = moments[k]
            if k in pad_cols:
                outs[k] = [swap(a) for a in _adamw_t(swap(big[k]), g_, swap(m_), swap(v_), "adamw_" + k)]
            else:
                outs[k] = _adamw(big[k], g_, m_, v_, "adamw_" + k)

    ple = ("ple_w_gate", "ple_w_proj")
    mixo = ("mix_w_out", "branch_w_conv", "branch_w_hgrn")
    dx3m = _mm(dgp, w2("ple_w_gate"), name="d_ple_gate_x", tb=True, tn=1024, tk=2048)
    grads["ple_w_gate"] = _mm(x3b, dgp, name="d_ple_gate_w", ta=True, out_dtype=BF16, tm=1024, tk=2048, tn=1024).reshape(N_CHIPS, -1, d)
    grads["ple_w_proj"] = _mm(pe, dpp, name="d_ple_proj_w", ta=True, out_dtype=BF16, out_blocked=N_CHIPS, tk=2048, tn=512)
    dr3, dy3b, dg2, db2 = _ln_bwd(dr4, dx3m, r3, lng[2], 0.5, "ln2_bwd")
    late_w = ple + ("ffn2_w_out",)
    dh3 = _mm(dy3b, w2("ffn2_w_out"), name="d_ffn2_out_x", tb=True, out_dtype=BF16, tn=1408, tk=2048)
    grads["ffn2_w_out"] = _mm(h3, dy3b, name="d_ffn2_out_w", ta=True, out_dtype=BF16, tm=1408, tk=2048, tn=1024).reshape(2, n_pad, d)
    dz3 = _swiglu_bwd(dh3, z3, "ffn2_act_bwd")
    dx2m, got = _mm(dz3, w3("ffn2_w_in"), name="d_ffn2_in_x", tb=True, b_blocked=True, tm=1024, tn=1024, tk=2816,
                    comm=[pair_exchange(*late_w)])
    pair_add(late_w, got)
    grads["ffn2_w_in"], got = _mm(x2b, dz3, name="d_ffn2_in_w", ta=True, out_dtype=BF16, out_blocked=N_CHIPS, tk=4096, comm=[chip_exchange(*late_w)])
    chip_add(late_w, got)
    dr2, dy2b, dg1, db1 = _ln_bwd(dr3, dx2m, r2, lng[1], 1.0, "ln1_bwd")
    dmer, got = _mm(dy2b, w2("mix_w_out"), name="d_mix_out_x", tb=True, tn=1024, tk=2048, comm=[pair_exchange("ffn2_w_in")])
    pair_add(["ffn2_w_in"], got)
    g_, full = _mm(merged, dy2b, name="d_mix_out_w", ta=True, out_dtype=BF16, tm=1024, tk=2048, tn=1024, comm=[pair_share(*late_w)])
    grads["mix_w_out"] = g_.reshape(N_CHIPS, -1, d)
    update(late_w, full)
    dma, dmb, dgc, dgh = _merge_bwd(dmer, z, ma, mb, w, "merge_bwd")
    dya = _mm(dma, w3("branch_w_conv"), name="d_branch_conv_x", tb=True, b_blocked=True, tn=1024, tk=512)
    dyb = _mm(dmb, w3("branch_w_hgrn"), name="d_branch_hgrn_x", tb=True, b_blocked=True, tn=1024, tk=512)
    grads["branch_w_conv"] = _mm(ya, dma, name="d_branch_conv_w", ta=True, out_dtype=BF16, out_blocked=N_CHIPS, tm=1024, tk=2048, tn=512)
    grads["branch_w_hgrn"] = _mm(yb, dmb, name="d_branch_hgrn_w", ta=True, out_dtype=BF16, out_blocked=N_CHIPS, tm=1024, tk=2048, tn=512)
    dbg, dcg, dhc, dcw = _conv_bwd(dya, z, cw, w, "conv_bwd")
    (dq_, df_, di_, dgr_, dhg, dnw), got2, got = _hgrn_bwd(dyb, z, o_h, states, hg, nw_, w, "hgrn_bwd",
                                                            comm=[chip_exchange("ffn2_w_in"), pair_exchange(*mixo)])
    chip_add(["ffn2_w_in"], got2)
    pair_add(mixo, got)
    dz = _concat_cols([dbg, dcg, dhc, dq_, df_, di_, dgr_, dgc, dgh], "dz_concat")
    dx1m, full, got = _mm(dz, w3("mix_w_in"), name="d_mix_in_x", tb=True, b_blocked=True, tm=1024, tn=1024, tk=2816,
                          comm=[pair_share("ffn2_w_in"), chip_exchange(*mixo)])
    update(["ffn2_w_in"], full)
    chip_add(mixo, got)
    grads["mix_w_in"], full = _mm(x1b, dz, name="d_mix_in_w", ta=True, out_dtype=BF16, out_blocked=N_CHIPS, tk=4096, comm=[pair_share(*mixo)])
    update(mixo, full)
    dr1, dy1b, dg0, db0 = _ln_bwd(dr2, dx1m, r1, lng[0], 0.5, "ln0_bwd")
    dh1, got = _mm(dy1b, w2("ffn1_w_out"), name="d_ffn1_out_x", tb=True, out_dtype=BF16, tn=1408, tk=2048,
                   comm=[pair_exchange("mix_w_in")])
    pair_add(["mix_w_in"], got)
    mix_sends = [sends["mix_w_in"]]
    g_, got_a = _mm(h1, dy1b, name="d_ffn1_out_w", ta=True, out_dtype=BF16, tm=1408, tk=2048, tn=1024, comm=[_rs_chip_exchange(mix_sends, (0, 2))])
    grads["ffn1_w_out"] = g_.reshape(2, n_pad, d)
    dz1 = _swiglu_bwd(dh1, z1, "ffn1_act_bwd")
    g_other, got2, got = _mm(x0b, dz1, name="d_ffn1_in_w_other", ta=True, out_dtype=BF16, out_blocked=N_CHIPS, tk=4096, half=(sp, True),
                             comm=[_rs_chip_exchange(mix_sends, (1, 2), got_a), pair_exchange("ffn1_w_out")])
    chip_add(["mix_w_in"], got2)
    pair_add(["ffn1_w_out"], got)
    grads["ffn1_w_in"], full, got2, got = _mm(
        x0b, dz1, name="d_ffn1_in_w_own", ta=True, out_dtype=BF16, out_blocked=N_CHIPS, tk=4096, half=(sp, False),
        comm=[pair_share("mix_w_in"), chip_exchange("ffn1_w_out"),
              _rs_pair_exchange([g_other], [meta["ffn1_w_in"]], [rows["ffn1_w_in"]])])
    update(["mix_w_in"], full)
    chip_add(["ffn1_w_out"], got2)
    pair_add(["ffn1_w_in"], got)
    dx0, got2, full = _mm(dz1, w3("ffn1_w_in"), name="d_ffn1_in_x", tb=True, b_blocked=True, tm=1024, tn=1024, tk=2816,
                          add=(dr1, ALPHA), comm=[chip_exchange("ffn1_w_in"), pair_share("ffn1_w_out")])
    chip_add(["ffn1_w_in"], got2)
    update(["ffn1_w_out"], full)
    grad_x = dx0.reshape(x.shape)
    update(["ffn1_w_in"], _run_stages([pair_share("ffn1_w_in")], "rs_tail_pair")[0])

    pack = jnp.concatenate([
        dg0, dg1, dg2, dg3, db0, db1, db2, db3,
        jnp.pad(dcw, ((0, 0), (0, d - w))), jnp.pad(dhg, ((0, 0), (0, d - w))),
        jnp.pad(jnp.sum(dnw.reshape(-1, HEAD), axis=0, keepdims=True), ((0, 0), (0, d - HEAD))), sq], axis=0)
    pack = _all_reduce_small(jnp.pad(pack, ((0, 1), (0, 0))), "reduce_small")
    loss = (0.5 / d) * jnp.sum(pack[14])
    g_ln_g = lax.dynamic_slice_in_dim(pack[0:4], chip * dq, dq, axis=1)
    g_ln_b = lax.dynamic_slice_in_dim(pack[4:8], chip * dq, dq, axis=1)
    g_conv = lax.dynamic_slice_in_dim(pack[8:11, :w], chip * wq, wq, axis=1)
    g_hg = pack[11:13, :w]
    g_nw = pack[13:14, :HEAD]

    small_w = dict(ln_g=(ln_g, g_ln_g, m_ln_g, v_ln_g), ln_b=(ln_b, g_ln_b, m_ln_b, v_ln_b),
                   conv_w=(conv_w, g_conv, m_conv_w, v_conv_w), hg_lower_bound=(hg_lower_bound, g_hg, m_hg_lower_bound, v_hg_lower_bound),
                   hg_norm_w=(hg_norm_w, g_nw, m_hg_norm_w, v_hg_norm_w))
    for k, (w_, g_, m_, v_) in small_w.items():
        outs[k] = _adamw(w_, g_.reshape(-1, w_.shape[-1]), m_, v_, "adamw_" + k)

    order = ["ln_g", "ln_b", "ffn1_w_in", "ffn1_w_out", "mix_w_in", "conv_w", "hg_lower_bound", "hg_norm_w", "branch_w_conv",
             "branch_w_hgrn", "mix_w_out", "ffn2_w_in", "ffn2_w_out", "ple_w_gate", "ple_w_proj"]
    return (loss, grad_x, *[outs[k][0] for k in order], *[outs[k][1] for k in order], *[outs[k][2] for k in order],
            *[outs[k][3] for k in order])
```

```python
import collections
import functools

import jax
import jax.numpy as jnp
from jax import lax
from jax.experimental import pallas as pl
from jax.experimental.pallas import tpu as pltpu

F32 = jnp.float32
BF16 = jnp.bfloat16
MESH = pl.DeviceIdType.MESH
ANY = pl.BlockSpec(memory_space=pl.ANY)
VMEM_SPEC = pl.BlockSpec(memory_space=pltpu.VMEM)
SDS = jax.ShapeDtypeStruct

DEPTH = 1
ALPHA = (2.0 * DEPTH) ** 0.25
LN_EPS = 1e-5
RMS_EPS = 1e-6
CHUNK = 32
HEAD = 128
ADAM_LR, ADAM_B1, ADAM_B2, ADAM_EPS, ADAM_WD, ADAM_STEP = 0.001, 0.9, 0.999, 1e-08, 0.01, 10

LANES = 128
N_CHIPS = 4
N_DEV = 8
FIRST_GATHER_PARTS = 4
VMEM_LIMIT = 52 * 1024 * 1024
MM_PIECE = 512


def _cparams(*sem):
    if sem:
        return pltpu.CompilerParams(dimension_semantics=sem, vmem_limit_bytes=VMEM_LIMIT)
    return pltpu.CompilerParams(vmem_limit_bytes=VMEM_LIMIT)


def _tile(n, target, mult):
    best = None
    for t in range(mult, min(n, target) + 1, mult):
        if n % t == 0:
            best = t
    return best if best is not None else n


def _sigmoid(x):
    return 1.0 / (1.0 + jnp.exp(-x))


_Stage = collections.namedtuple("_Stage", "ins out_shapes aliases sems start finish")


def _hosted_call(compute, stages, *, name, grid, in_specs, out_specs, out_shape, scratch_shapes, operands, parallel,
                 prefetch=None):
    n_cmp, n_out, n_scr = len(in_specs), len(out_specs), len(scratch_shapes)
    n_in = n_cmp
    n_pre = int(prefetch is not None)
    c_in = [len(s.ins) for s in stages]
    c_out = [len(s.out_shapes) for s in stages]
    c_sem = [len(s.sems) for s in stages]
    aliases = {}
    for si, s in enumerate(stages):
        for a_in, a_out in s.aliases.items():
            aliases[n_pre + n_in + sum(c_in[:si]) + a_in] = n_out + sum(c_out[:si]) + a_out

    def body(*refs):
        refs = refs[n_pre:]
        ins = refs[:n_cmp]
        cins = refs[n_in:n_in + sum(c_in)]
        outs = refs[n_in + sum(c_in):n_in + sum(c_in) + n_out]
        couts = refs[n_in + sum(c_in) + n_out:n_in + sum(c_in) + n_out + sum(c_out)]
        scr = refs[n_in + sum(c_in) + n_out + sum(c_out):][:n_scr]
        sems = refs[n_in + sum(c_in) + n_out + sum(c_out) + n_scr:]

        def stage_refs(si):
            return (cins[sum(c_in[:si]):sum(c_in[:si + 1])], couts[sum(c_out[:si]):sum(c_out[:si + 1])],
                    sems[sum(c_sem[:si]):sum(c_sem[:si + 1])])

        if stages:
            first = functools.reduce(jnp.logical_and, [pl.program_id(ax) == 0 for ax in range(len(grid))])
            last = functools.reduce(jnp.logical_and, [pl.program_id(ax) == grid[ax] - 1 for ax in range(len(grid))])

            @pl.when(first)
            def _():
                for si, s in enumerate(stages):
                    s.start(*stage_refs(si))

        compute(*ins, *outs, *scr)
        if stages:
            @pl.when(last)
            def _():
                for si, s in enumerate(stages):
                    s.finish(*stage_refs(si))

    sem = ("arbitrary",) * len(grid) if stages else ("parallel",) * parallel + ("arbitrary",) * (len(grid) - parallel)
    all_in = list(in_specs) + [ANY] * (n_in - n_cmp + sum(c_in))
    all_out = list(out_specs) + [ANY] * sum(c_out)
    all_scr = list(scratch_shapes) + [q for s in stages for q in s.sems]
    all_shape = list(out_shape) + [o for s in stages for o in s.out_shapes]
    args = list(operands) + [a for s in stages for a in s.ins]
    if prefetch is None:
        res = pl.pallas_call(body, name=name, grid=grid, in_specs=all_in, out_specs=all_out, out_shape=all_shape,
                             input_output_aliases=aliases, scratch_shapes=all_scr, compiler_params=_cparams(*sem))(*args)
    else:
        grid_spec = pltpu.PrefetchScalarGridSpec(num_scalar_prefetch=1, grid=grid, in_specs=all_in, out_specs=all_out,
                                                 scratch_shapes=all_scr)
        res = pl.pallas_call(body, name=name, grid_spec=grid_spec, out_shape=all_shape, input_output_aliases=aliases,
                             compiler_params=_cparams(*sem))(prefetch, *args)
    main = res[0] if n_out == 1 else list(res[:n_out])
    if not stages:
        return main
    rest = res[n_out:]
    return (main, *[list(rest[sum(c_out[:si]):sum(c_out[:si + 1])]) for si in range(len(stages))])


def _run_stages(stages, name):
    def body(*refs):
        n_i = sum(len(s.ins) for s in stages)
        n_o = sum(len(s.out_shapes) for s in stages)
        cins, couts, sems = refs[:n_i], refs[n_i:n_i + n_o], refs[n_i + n_o:]
        pos = [0, 0, 0]
        parts = []
        for s in stages:
            parts.append((cins[pos[0]:pos[0] + len(s.ins)], couts[pos[1]:pos[1] + len(s.out_shapes)], sems[pos[2]:pos[2] + len(s.sems)]))
            pos = [pos[0] + len(s.ins), pos[1] + len(s.out_shapes), pos[2] + len(s.sems)]
        for s, p_ in zip(stages, parts):
            s.start(*p_)
        for s, p_ in zip(stages, parts):
            s.finish(*p_)

    aliases, ni, no = {}, 0, 0
    for s in stages:
        for a_in, a_out in s.aliases.items():
            aliases[ni + a_in] = no + a_out
        ni, no = ni + len(s.ins), no + len(s.out_shapes)
    res = pl.pallas_call(
        body, name=name, in_specs=[ANY] * ni, out_specs=[ANY] * no, out_shape=[o for s in stages for o in s.out_shapes],
        input_output_aliases=aliases, scratch_shapes=[q for s in stages for q in s.sems],
    )(*[a for s in stages for a in s.ins])
    out, pos = [], 0
    for s in stages:
        out.append(list(res[pos:pos + len(s.out_shapes)]))
        pos += len(s.out_shapes)
    return out


def _mm(a, b, *, name, ta=False, tb=False, b_blocked=False, out_blocked=0, out_dtype=F32,
        tm=512, tn=1408, tk=2048, comm=(), half=None, add=None):
    if ta:
        kd, m = a.shape
    else:
        m, kd = a.shape
    if b_blocked and not tb:
        g, kb, nb = b.shape
        assert kb == kd
        n = g * nb
        tn = _tile(nb, tn, LANES)
        tk = _tile(kd, tk, LANES)
        per_n = nb // tn
        b_spec = pl.BlockSpec((None, tk, tn), lambda i, j, k, *s: (j // per_n, k, j % per_n))
    elif b_blocked and tb:
        g, n, kb = b.shape
        assert g * kb == kd
        tn = _tile(n, tn, LANES)
        tk = _tile(kb, tk, LANES)
        per_k = kb // tk
        b_spec = pl.BlockSpec((None, tn, tk), lambda i, j, k, *s: (k // per_k, j, k % per_k))
    elif tb:
        n, kb = b.shape
        assert kb == kd
        tn = _tile(n, tn, LANES)
        tk = _tile(kd, tk, LANES)
        b_spec = pl.BlockSpec((tn, tk), lambda i, j, k, *s: (j, k))
    else:
        kb, n = b.shape
        assert kb == kd
        tn = _tile(n // out_blocked if out_blocked else n, tn, LANES)
        per_o = (n // out_blocked) // tn if out_blocked else None
        tk = _tile(kd, tk, LANES)
        b_spec = pl.BlockSpec((tk, tn), lambda i, j, k, *s: (k, j))
    m_run = m // 2 if half else m
    tm = _tile(m_run, tm, LANES if ta else 8)

    def row(i, s):
        if not half:
            return i
        h = 1 - s[0][0] if half[1] else s[0][0]
        return h * (m_run // tm) + i

    if ta:
        a_spec = pl.BlockSpec((tk, tm), lambda i, j, k, *s: (k, row(i, s)))
    else:
        a_spec = pl.BlockSpec((tm, tk), lambda i, j, k, *s: (row(i, s), k))
    if out_blocked:
        assert not b_blocked and not tb
        o_spec = pl.BlockSpec((None, tm, tn), lambda i, j, k, *s: (j // per_o, row(i, s), j % per_o))
        o_shape = SDS((out_blocked, m, n // out_blocked), out_dtype)
    else:
        o_spec = pl.BlockSpec((tm, tn), lambda i, j, k, *s: (row(i, s), j))
        o_shape = SDS((m, n), out_dtype)
    nk = kd // tk
    dn = (((0 if ta else 1,), (1 if tb else 0,)), ((), ()))
    grid = (m_run // tm, n // tn, nk)

    pieces = [(lo, min(MM_PIECE, tn - lo)) for lo in range(0, tn, MM_PIECE)]

    def compute(a_ref, b_ref, *rest):
        add_ref = rest[0] if add else None
        o_ref, acc_ref = rest[-2:]
        a_tile = a_ref[...].astype(BF16)
        k = pl.program_id(2)

        def result(acc, cols):
            if add:
                acc = acc + add[1] * add_ref[:, cols]
            return acc.astype(o_ref.dtype)

        if nk > 1:
            @pl.when(k == 0)
            def _():
                acc_ref[...] = jnp.zeros_like(acc_ref)

        for lo, wd in pieces:
            cols = slice(lo, lo + wd)
            b_tile = b_ref[cols, :] if tb else b_ref[:, cols]
            part = lax.dot_general(a_tile, b_tile.astype(BF16), dn, preferred_element_type=F32)
            if nk == 1:
                o_ref[:, cols] = result(part, cols)
            else:
                acc_ref[:, cols] += part

        if nk > 1:
            @pl.when(k == nk - 1)
            def _():
                o_ref[...] = result(acc_ref[...], slice(None))

    extra = [(add[0], o_spec)] if add else []
    return _hosted_call(compute, comm, name=name, grid=grid, in_specs=[a_spec, b_spec] + [s_ for _, s_ in extra], out_specs=[o_spec],
                        out_shape=[o_shape], scratch_shapes=[pltpu.VMEM((tm, tn), F32)], operands=(a, b, *[a_ for a_, _ in extra]),
                        parallel=2, prefetch=half[0] if half else None)


def _swiglu_fwd(z, name, comm=()):
    t, n = z.shape
    n2 = n // 2
    tr = _tile(t, 128, 16)

    def body(a_ref, u_ref, o_ref):
        a = a_ref[...].astype(F32)
        o_ref[...] = (a * _sigmoid(a) * u_ref[...].astype(F32)).astype(o_ref.dtype)

    return _hosted_call(
        body, comm, name=name, grid=(t // tr,),
        in_specs=[pl.BlockSpec((tr, n2), lambda i: (i, 0)), pl.BlockSpec((tr, n2), lambda i: (i, 1))],
        out_specs=[pl.BlockSpec((tr, n2), lambda i: (i, 0))], out_shape=[SDS((t, n2), BF16)], scratch_shapes=[],
        operands=(z, z), parallel=1)


def _swiglu_bwd(dh, z, name):
    t, n = z.shape
    n2 = n // 2
    tr = _tile(t, 128, 16)

    def body(dh_ref, a_ref, u_ref, o_ref):
        a = a_ref[...].astype(F32)
        dh_ = dh_ref[...].astype(F32)
        s = _sigmoid(a)
        o_ref[:, 0:n2] = (dh_ * u_ref[...].astype(F32) * (s * (1.0 + a * (1.0 - s)))).astype(o_ref.dtype)
        o_ref[:, n2:n] = (dh_ * a * s).astype(o_ref.dtype)

    return pl.pallas_call(
        body, name=name, grid=(t // tr,),
        in_specs=[pl.BlockSpec((tr, n2), lambda i: (i, 0)), pl.BlockSpec((tr, n2), lambda i: (i, 0)),
                  pl.BlockSpec((tr, n2), lambda i: (i, 1))],
        out_specs=pl.BlockSpec((tr, n), lambda i: (i, 0)), out_shape=SDS((t, n), BF16),
        compiler_params=_cparams("parallel"),
    )(dh, z, z)


def _ln_stats(r):
    mu = jnp.mean(r, axis=-1, keepdims=True)
    xc = r - mu
    var = jnp.mean(xc * xc, axis=-1, keepdims=True)
    return xc * lax.rsqrt(var + LN_EPS)


def _ln_fwd(xp, y, g, b, gp, bp, scale, name, comm=()):
    t, d = xp.shape
    tr = _tile(t, 256, 16)

    def body(xp_ref, y_ref, g_ref, b_ref, *rest):
        r_ref, xb_ref = rest[-2:]
        x_prev = xp_ref[...]
        if gp is not None:
            x_prev = _ln_stats(x_prev) * rest[0][...] + rest[1][...]
        r = ALPHA * x_prev + scale * y_ref[...]
        r_ref[...] = r
        xb_ref[...] = (_ln_stats(r) * g_ref[...] + b_ref[...]).astype(BF16)

    row = pl.BlockSpec((tr, d), lambda i: (i, 0))
    vec = pl.BlockSpec((1, d), lambda i: (0, 0))
    prev = [] if gp is None else [gp, bp]
    return _hosted_call(
        body, comm, name=name, grid=(t // tr,), in_specs=[row, row, vec, vec] + [vec] * len(prev), out_specs=[row, row],
        out_shape=[SDS((t, d), F32), SDS((t, d), BF16)], scratch_shapes=[], operands=(xp, y, g, b, *prev), parallel=1)


def _ln_bwd(dra, dxm, r, g, scale, name):
    t, d = r.shape
    tr = _tile(t, 256, 16)

    def body(dra_ref, dxm_ref, r_ref, g_ref, dr_ref, dyb_ref, dg_ref, db_ref):
        i = pl.program_id(0)
        dx = ALPHA * dra_ref[...] + dxm_ref[...]
        rr = r_ref[...]
        mu = jnp.mean(rr, axis=-1, keepdims=True)
        xc = rr - mu
        rstd = lax.rsqrt(jnp.mean(xc * xc, axis=-1, keepdims=True) + LN_EPS)
        xh = xc * rstd
        dxh = dx * g_ref[...]
        dr = rstd * (dxh - jnp.mean(dxh, axis=-1, keepdims=True) - xh * jnp.mean(dxh * xh, axis=-1, keepdims=True))
        dr_ref[...] = dr
        dyb_ref[...] = (scale * dr).astype(BF16)
        dg = jnp.sum(dx * xh, axis=0, keepdims=True)
        db = jnp.sum(dx, axis=0, keepdims=True)

        @pl.when(i == 0)
        def _():
            dg_ref[...] = dg
            db_ref[...] = db

        @pl.when(i > 0)
        def _():
            dg_ref[...] += dg
            db_ref[...] += db

    row = pl.BlockSpec((tr, d), lambda i: (i, 0))
    vec = pl.BlockSpec((1, d), lambda i: (0, 0))
    return pl.pallas_call(
        body, name=name, grid=(t // tr,), in_specs=[row, row, row, vec], out_specs=[row, row, vec, vec],
        out_shape=[SDS((t, d), F32), SDS((t, d), BF16), SDS((1, d), F32), SDS((1, d), F32)],
        compiler_params=_cparams("arbitrary"),
    )(dra, dxm, r, g)


def _tail(r3, g3, b3, gp, pp, g, b, target, name):
    t, d = r3.shape
    tr = _tile(t, 256, 16)

    def body(r3_ref, g3_ref, b3_ref, gp_ref, pp_ref, g_ref, b_ref, tg_ref, dr_ref, dgp_ref, dpp_ref, dg_ref, db_ref, sq_ref):
        i = pl.program_id(0)
        gate = _sigmoid(gp_ref[...])
        pp_ = pp_ref[...]
        r = ALPHA * (_ln_stats(r3_ref[...]) * g3_ref[...] + b3_ref[...]) + gate * pp_
        mu = jnp.mean(r, axis=-1, keepdims=True)
        xc = r - mu
        rstd = lax.rsqrt(jnp.mean(xc * xc, axis=-1, keepdims=True) + LN_EPS)
        xh = xc * rstd
        err = xh * g_ref[...] + b_ref[...] - tg_ref[...]
        dx = err * (1.0 / d)
        dxh = dx * g_ref[...]
        dr = rstd * (dxh - jnp.mean(dxh, axis=-1, keepdims=True) - xh * jnp.mean(dxh * xh, axis=-1, keepdims=True))
        dr_ref[...] = dr
        dgp_ref[...] = (dr * pp_ * gate * (1.0 - gate)).astype(BF16)
        dpp_ref[...] = (dr * gate).astype(BF16)
        dg = jnp.sum(dx * xh, axis=0, keepdims=True)
        db = jnp.sum(dx, axis=0, keepdims=True)
        sq = jnp.sum(err * err, axis=0, keepdims=True)

        @pl.when(i == 0)
        def _():
            dg_ref[...] = dg
            db_ref[...] = db
            sq_ref[...] = sq

        @pl.when(i > 0)
        def _():
            dg_ref[...] += dg
            db_ref[...] += db
            sq_ref[...] += sq

    row = pl.BlockSpec((tr, d), lambda i: (i, 0))
    vec = pl.BlockSpec((1, d), lambda i: (0, 0))
    return pl.pallas_call(
        body, name=name, grid=(t // tr,), in_specs=[row, vec, vec, row, row, vec, vec, row],
        out_specs=[row, row, row, vec, vec, vec],
        out_shape=[SDS((t, d), F32), SDS((t, d), BF16), SDS((t, d), BF16), SDS((1, d), F32), SDS((1, d), F32),
                   SDS((1, d), F32)],
        compiler_params=_cparams("arbitrary"),
    )(r3, g3, b3, gp, pp, g, b, target)


def _to_bf16(x, name):
    t, d = x.shape
    tr = _tile(t, 512, 16)
    row = pl.BlockSpec((tr, d), lambda i: (i, 0))

    def body(x_ref, o_ref):
        o_ref[...] = x_ref[...].astype(BF16)

    return pl.pallas_call(body, name=name, grid=(t // tr,), in_specs=[row], out_specs=row, out_shape=SDS((t, d), BF16),
                          compiler_params=_cparams("parallel"))(x)


def _concat_cols(parts, name):
    t = parts[0].shape[0]
    widths = [p_.shape[1] for p_ in parts]
    tr = _tile(t, 256, 16)

    def body(*refs):
        o_ref = refs[-1]
        at = 0
        for ref, wd in zip(refs[:-1], widths):
            o_ref[:, at:at + wd] = ref[...]
            at += wd

    return pl.pallas_call(
        body, name=name, grid=(t // tr,), in_specs=[pl.BlockSpec((tr, wd), lambda i: (i, 0)) for wd in widths],
        out_specs=pl.BlockSpec((tr, sum(widths)), lambda i: (i, 0)), out_shape=SDS((t, sum(widths)), parts[0].dtype),
        compiler_params=_cparams("parallel"),
    )(*parts)


def _merge_fwd(z, ma, mb, w, name):
    t = z.shape[0]
    tr = _tile(t, 256, 16)

    def body(gc_ref, gh_ref, ma_ref, mb_ref, o_ref):
        o_ref[...] = (_sigmoid(gc_ref[...]) * ma_ref[...] + _sigmoid(gh_ref[...]) * mb_ref[...]).astype(BF16)

    half = pl.BlockSpec((tr, w), lambda i, j: (i, j))
    return pl.pallas_call(
        body, name=name, grid=(t // tr, 2),
        in_specs=[pl.BlockSpec((tr, w), lambda i, j: (i, 7 + j)), pl.BlockSpec((tr, w), lambda i, j: (i, 9 + j)), half, half],
        out_specs=half, out_shape=SDS((t, 2 * w), BF16), compiler_params=_cparams("parallel", "parallel"),
    )(z, z, ma, mb)


def _merge_bwd(dmer, z, ma, mb, w, name):
    t = z.shape[0]
    tr = _tile(t, 256, 16)

    def body(d_ref, gc_ref, gh_ref, ma_ref, mb_ref, dma_ref, dmb_ref, dgc_ref, dgh_ref):
        dm = d_ref[...]
        sc = _sigmoid(gc_ref[...])
        sh = _sigmoid(gh_ref[...])
        dma_ref[...] = (dm * sc).astype(BF16)
        dmb_ref[...] = (dm * sh).astype(BF16)
        dgc_ref[...] = (dm * ma_ref[...] * sc * (1.0 - sc)).astype(BF16)
        dgh_ref[...] = (dm * mb_ref[...] * sh * (1.0 - sh)).astype(BF16)

    half = pl.BlockSpec((tr, w), lambda i, j: (i, j))
    return pl.pallas_call(
        body, name=name, grid=(t // tr, 2),
        in_specs=[half, pl.BlockSpec((tr, w), lambda i, j: (i, 7 + j)), pl.BlockSpec((tr, w), lambda i, j: (i, 9 + j)), half, half],
        out_specs=[half] * 4, out_shape=[SDS((t, 2 * w), BF16)] * 4, compiler_params=_cparams("parallel", "parallel"),
    )(dmer, z, z, ma, mb)


def _shift_down(x, s, row):
    return jnp.where(row >= s, pltpu.roll(x, s, axis=0), 0.0)


def _shift_up(x, s, row, t):
    return jnp.where(row < t - s, pltpu.roll(x, t - s, axis=0), 0.0)


def _conv_fwd(z, cw, w, name):
    t = z.shape[0]
    tc = LANES
    nb = w // tc

    def body(b_ref, c_ref, h_ref, w_ref, o_ref):
        u = c_ref[...] * h_ref[...]
        row = lax.broadcasted_iota(jnp.int32, u.shape, 0)
        cw_ = w_ref[...]
        conv = cw_[2:3, :] * u + cw_[1:2, :] * _shift_down(u, 1, row) + cw_[0:1, :] * _shift_down(u, 2, row)
        o_ref[...] = (b_ref[...] * conv).astype(BF16)

    col = lambda off: pl.BlockSpec((t, tc), lambda j: (0, off * nb + j))
    return pl.pallas_call(
        body, name=name, grid=(nb,), in_specs=[col(0), col(1), col(2), pl.BlockSpec((3, tc), lambda j: (0, j))],
        out_specs=pl.BlockSpec((t, tc), lambda j: (0, j)), out_shape=SDS((t, w), BF16), compiler_params=_cparams("parallel"),
    )(z, z, z, cw)


def _conv_bwd(dy, z, cw, w, name):
    t = z.shape[0]
    tc = LANES
    nb = w // tc

    def body(dy_ref, b_ref, c_ref, h_ref, w_ref, db_ref, dc_ref, dh_ref, dw_ref):
        c_, h_ = c_ref[...], h_ref[...]
        u = c_ * h_
        row = lax.broadcasted_iota(jnp.int32, u.shape, 0)
        cw_ = w_ref[...]
        u1 = _shift_down(u, 1, row)
        u2 = _shift_down(u, 2, row)
        dy_ = dy_ref[...]
        db_ref[...] = (dy_ * (cw_[2:3, :] * u + cw_[1:2, :] * u1 + cw_[0:1, :] * u2)).astype(BF16)
        dconv = dy_ * b_ref[...]
        du = cw_[2:3, :] * dconv + cw_[1:2, :] * _shift_up(dconv, 1, row, t) + cw_[0:1, :] * _shift_up(dconv, 2, row, t)
        dc_ref[...] = (du * h_).astype(BF16)
        dh_ref[...] = (du * c_).astype(BF16)
        dw_ref[0:1, :] = jnp.sum(dconv * u2, axis=0, keepdims=True)
        dw_ref[1:2, :] = jnp.sum(dconv * u1, axis=0, keepdims=True)
        dw_ref[2:3, :] = jnp.sum(dconv * u, axis=0, keepdims=True)

    col = lambda off: pl.BlockSpec((t, tc), lambda j: (0, off * nb + j))
    own = pl.BlockSpec((t, tc), lambda j: (0, j))
    wsp = pl.BlockSpec((3, tc), lambda j: (0, j))
    return pl.pallas_call(
        body, name=name, grid=(nb,), in_specs=[own, col(0), col(1), col(2), wsp], out_specs=[own, own, own, wsp],
        out_shape=[SDS((t, w), BF16)] * 3 + [SDS((3, w), F32)], compiler_params=_cparams("parallel"),
    )(dy, z, z, z, cw)


def _lower_bound(hg):
    mx = jnp.max(hg, axis=0, keepdims=True)
    e = jnp.exp(hg - mx)
    inv = 1.0 / jnp.sum(e, axis=0, keepdims=True)
    return e[0:1, :] * inv, e[1:2, :] * inv


def _chunk_cumsum(x, row):
    s = 1
    while s < CHUNK:
        x = x + jnp.where(row % CHUNK >= s, pltpu.roll(x, s, axis=0), 0.0)
        s *= 2
    return x


def _dot_nt(a, b):
    return lax.dot_general(a.astype(BF16), b.astype(BF16), (((1,), (1,)), ((), ())), preferred_element_type=F32)


def _dot_tn(a, b):
    return lax.dot_general(a.astype(BF16), b.astype(BF16), (((0,), (0,)), ((), ())), preferred_element_type=F32)


def _dot_nn(a, b):
    return jnp.dot(a.astype(BF16), b.astype(BF16), preferred_element_type=F32)


def _tril(x):
    r = lax.broadcasted_iota(jnp.int32, x.shape, 0)
    c = lax.broadcasted_iota(jnp.int32, x.shape, 1)
    return jnp.where(r >= c, x, 0.0)


HGRN_GROUP = 4
HGRN_ROWS = 512
HGRN_UNROLL = 2


def _unrolled_loop(n, step, init):
    assert n % HGRN_UNROLL == 0

    def trip(i, carry):
        for u in range(HGRN_UNROLL):
            carry = step(i * HGRN_UNROLL + u, carry)
        return carry

    return lax.fori_loop(0, n // HGRN_UNROLL, trip, init)


def _hgrn_chunk_inputs(q_ref, f_ref, cum_ref, lb, rows, ln):
    qr = q_ref[rows, ln]
    q = qr * _sigmoid(qr)
    f = lb + (1.0 - lb) * _sigmoid(f_ref[rows, ln])
    return q, 1.0 - f, cum_ref[rows, ln]


def _hgrn_fwd(z, hg, nw, w, name, comm=()):
    t = z.shape[0]
    nh = w // HEAD
    gh = _tile(nh, HGRN_GROUP, 1)
    gw = gh * HEAD
    ngrp = nh // gh
    tb = _tile(t, HGRN_ROWS, CHUNK)
    ncb = tb // CHUNK

    def body(q_ref, f_ref, i_ref, g_ref, hg_ref, nw_ref, y_ref, o_ref, st_ref, cum_ref, *s_refs):
        lb_all, _ = _lower_bound(hg_ref[...])
        row = lax.broadcasted_iota(jnp.int32, (tb, gw), 0)
        cum_ref[...] = _chunk_cumsum(jnp.log(lb_all + (1.0 - lb_all) * _sigmoid(f_ref[...])), row)

        @pl.when(pl.program_id(1) == 0)
        def _():
            for s_ref in s_refs:
                s_ref[...] = jnp.zeros_like(s_ref)

        def step(c, carry):
            rows = pl.ds(pl.multiple_of(c * CHUNK, CHUNK), CHUNK)
            for g in range(gh):
                ln = slice(g * HEAD, (g + 1) * HEAD)
                lb = lb_all[:, ln]
                q, k, cum = _hgrn_chunk_inputs(q_ref, f_ref, cum_ref, lb, rows, ln)
                v = i_ref[rows, ln]
                last = cum[CHUNK - 1:CHUNK, :]
                mid = cum[CHUNK // 2 - 1:CHUNK // 2, :]
                st = s_refs[g][...]
                st_ref[g, c] = st.astype(BF16)
                scores = _tril(_dot_nt(q * jnp.exp(cum - mid), k * jnp.exp(mid - cum)))
                o_ref[rows, ln] = _dot_nt(q * jnp.exp(cum), st) + _dot_nn(scores, v)
                s_refs[g][...] = st * jnp.exp(last) + _dot_tn(v, k * jnp.exp(last - cum))
            return carry

        _unrolled_loop(ncb, step, 0)
        for g in range(gh):
            ln = slice(g * HEAD, (g + 1) * HEAD)
            o = o_ref[:, ln]
            n = o * lax.rsqrt(jnp.mean(o * o, axis=-1, keepdims=True) + RMS_EPS)
            gr = g_ref[:, ln]
            y_ref[:, ln] = (n * nw_ref[...] * gr * _sigmoid(gr)).astype(BF16)

    col = lambda off: pl.BlockSpec((tb, gw), lambda h, j: (j, off * ngrp + h))
    own = pl.BlockSpec((tb, gw), lambda h, j: (j, h))
    return _hosted_call(
        body, comm, name=name, grid=(ngrp, t // tb),
        in_specs=[col(3), col(4), col(5), col(6), pl.BlockSpec((2, gw), lambda h, j: (0, h)),
                  pl.BlockSpec((1, HEAD), lambda h, j: (0, 0))],
        out_specs=[own, own, pl.BlockSpec((gh, ncb, HEAD, HEAD), lambda h, j: (h, j, 0, 0))],
        out_shape=[SDS((t, w), BF16), SDS((t, w), F32), SDS((nh, t // CHUNK, HEAD, HEAD), BF16)],
        scratch_shapes=[pltpu.VMEM((tb, gw), F32)] + [pltpu.VMEM((HEAD, HEAD), F32)] * gh,
        operands=(z, z, z, z, hg, nw), parallel=1)


def _hgrn_bwd(dy, z, o, states, hg, nw, w, name, comm=()):
    t = z.shape[0]
    nh = w // HEAD
    gh = _tile(nh, HGRN_GROUP, 1)
    gw = gh * HEAD
    ngrp = nh // gh
    tb = _tile(t, HGRN_ROWS, CHUNK)
    ncb = tb // CHUNK
    nt = t // tb

    def body(dy_ref, q_ref, f_ref, i_ref, g_ref, o_ref, st_ref, hg_ref, nw_ref,
             dq_ref, df_ref, di_ref, dg_ref, dhg_ref, dnw_ref, cum_ref, do_ref, *ds_refs):
        lb_all, s1_all = _lower_bound(hg_ref[...])
        row = lax.broadcasted_iota(jnp.int32, (tb, gw), 0)
        crow = lax.broadcasted_iota(jnp.int32, (CHUNK, HEAD), 0)
        cum_ref[...] = _chunk_cumsum(jnp.log(lb_all + (1.0 - lb_all) * _sigmoid(f_ref[...])), row)

        @pl.when(pl.program_id(1) == 0)
        def _():
            for ds_ref in ds_refs:
                ds_ref[...] = jnp.zeros_like(ds_ref)
            dhg_ref[...] = jnp.zeros_like(dhg_ref)
            dnw_ref[...] = jnp.zeros_like(dnw_ref)

        for g in range(gh):
            ln = slice(g * HEAD, (g + 1) * HEAD)
            o_ = o_ref[:, ln]
            rstd = lax.rsqrt(jnp.mean(o_ * o_, axis=-1, keepdims=True) + RMS_EPS)
            n = o_ * rstd
            gr = g_ref[:, ln]
            sg = _sigmoid(gr)
            dy_ = dy_ref[:, ln]
            dg_ref[:, ln] = (dy_ * n * nw_ref[...] * (sg * (1.0 + gr * (1.0 - sg)))).astype(BF16)
            dsil = dy_ * gr * sg
            dnw_ref[:, ln] += jnp.sum(dsil * n, axis=0, keepdims=True)
            dn = dsil * nw_ref[...]
            do_ref[:, ln] = rstd * (dn - n * jnp.mean(dn * n, axis=-1, keepdims=True))

        def step(cc, dlbs):
            c = ncb - 1 - cc
            rows = pl.ds(pl.multiple_of(c * CHUNK, CHUNK), CHUNK)
            new = []
            for g in range(gh):
                ln = slice(g * HEAD, (g + 1) * HEAD)
                lb = lb_all[:, ln]
                qr = q_ref[rows, ln]
                sq = _sigmoid(qr)
                q = qr * sq
                sf = _sigmoid(f_ref[rows, ln])
                f = lb + (1.0 - lb) * sf
                k = 1.0 - f
                cum = cum_ref[rows, ln]
                v = i_ref[rows, ln]
                do = do_ref[rows, ln]
                last = cum[CHUNK - 1:CHUNK, :]
                mid = cum[CHUNK // 2 - 1:CHUNK // 2, :]
                eg = jnp.exp(cum)
                em = jnp.exp(cum - mid)
                enm = jnp.exp(mid - cum)
                elc = jnp.exp(last - cum)
                qm, km, kl = q * em, k * enm, k * elc
                ds = ds_refs[g][...]
                a = _tril(_dot_nt(qm, km))
                da = _tril(_dot_nt(do, v))
                di_ref[rows, ln] = (_dot_tn(a, do) + _dot_nt(kl, ds)).astype(BF16)
                st = st_ref[g, c]
                dkl = _dot_nn(v, ds)
                dq = _dot_nn(do, st) * eg + _dot_nn(da, km) * em
                dk = _dot_tn(da, qm) * enm + dkl * elc
                el = jnp.exp(last)
                ds_refs[g][...] = ds * el + _dot_tn(do, q * eg)
                dlast = jnp.sum(kl * dkl, axis=0, keepdims=True) + el * jnp.sum(ds * st.astype(F32), axis=0, keepdims=True)
                x = q * dq - k * dk + jnp.where(crow == CHUNK - 1, dlast, 0.0)
                s = 1
                while s < CHUNK:
                    x = x + _shift_up(x, s, crow, CHUNK)
                    s *= 2
                df = x / f - dk
                dq_ref[rows, ln] = (dq * (sq * (1.0 + qr * (1.0 - sq)))).astype(BF16)
                df_ref[rows, ln] = (df * (1.0 - lb) * sf * (1.0 - sf)).astype(BF16)
                new.append(dlbs[g] + jnp.sum(df * (1.0 - sf), axis=0, keepdims=True))
            return tuple(new)

        dlbs = _unrolled_loop(ncb, step, tuple(jnp.zeros((1, HEAD), F32) for _ in range(gh)))
        for g in range(gh):
            ln = slice(g * HEAD, (g + 1) * HEAD)
            dlb = dlbs[g] * lb_all[:, ln] * s1_all[:, ln]
            dhg_ref[0:1, ln] += dlb
            dhg_ref[1:2, ln] -= dlb

    col = lambda off: pl.BlockSpec((tb, gw), lambda h, j: (nt - 1 - j, off * ngrp + h))
    own = pl.BlockSpec((tb, gw), lambda h, j: (nt - 1 - j, h))
    hsp = pl.BlockSpec((2, gw), lambda h, j: (0, h))
    return _hosted_call(
        body, comm, name=name, grid=(ngrp, nt),
        in_specs=[own, col(3), col(4), col(5), col(6), own,
                  pl.BlockSpec((gh, ncb, HEAD, HEAD), lambda h, j: (h, nt - 1 - j, 0, 0)),
                  hsp, pl.BlockSpec((1, HEAD), lambda h, j: (0, 0))],
        out_specs=[own, own, own, own, hsp, pl.BlockSpec((1, gw), lambda h, j: (0, h))],
        out_shape=[SDS((t, w), BF16)] * 4 + [SDS((2, w), F32), SDS((1, w), F32)],
        scratch_shapes=[pltpu.VMEM((tb, gw), F32)] * 2 + [pltpu.VMEM((HEAD, HEAD), F32)] * gh,
        operands=(dy, z, z, z, z, o, states, hg, nw), parallel=1)


def _cast_pad(wt, n_pad, meta, sp, name, comm=()):
    _, r, n = wt.shape
    g, p, per = meta
    tr = _tile(r, max(16, (3 << 19) // n_pad // 16 * 16), 16)

    def body(w_ref, o_ref):
        if n_pad != n:
            o_ref[...] = jnp.zeros(o_ref.shape, o_ref.dtype)
        o_ref[:, 0:n] = w_ref[...].astype(BF16)

    return _hosted_call(
        body, comm, name=name, grid=(r // tr,), in_specs=[pl.BlockSpec((None, tr, n), lambda i, sp: (0, i, 0))],
        out_specs=[pl.BlockSpec((None, tr, n_pad), lambda i, sp: (sp[1] // per, ((sp[1] % per) * r) // tr + i, 0))],
        out_shape=[SDS((g, p, n_pad), BF16)], scratch_shapes=[], operands=(wt,), parallel=1, prefetch=sp)


def _cast_pad_t(wt_t, n_pad, meta, sp, name, comm=()):
    _, n, r = wt_t.shape
    g, p, per = meta
    tc = _tile(r, 256, LANES)

    def body(w_ref, o_ref):
        for lo in range(0, n_pad, LANES):
            rows = min(LANES, n - lo)
            piece = w_ref[lo:lo + rows, :]
            if rows < LANES:
                piece = jnp.concatenate([piece, jnp.zeros((LANES - rows, tc), F32)], axis=0)
            o_ref[:, lo:lo + LANES] = piece.T.astype(BF16)

    return _hosted_call(
        body, comm, name=name, grid=(r // tc,), in_specs=[pl.BlockSpec((None, n, tc), lambda i, sp: (0, 0, i))],
        out_specs=[pl.BlockSpec((None, tc, n_pad), lambda i, sp: (sp[1] // per, ((sp[1] % per) * r) // tc + i, 0))],
        out_shape=[SDS((g, p, n_pad), BF16)], scratch_shapes=[], operands=(wt_t,), parallel=1, prefetch=sp)


def _adam_math(w, g, m, v):
    m2 = ADAM_B1 * m + (1.0 - ADAM_B1) * g
    v2 = ADAM_B2 * v + (1.0 - ADAM_B2) * (g * g)
    c1 = 1.0 / (1.0 - ADAM_B1 ** ADAM_STEP)
    c2 = 1.0 / (1.0 - ADAM_B2 ** ADAM_STEP)
    return -ADAM_LR * ((m2 * c1) / (jnp.sqrt(v2 * c2) + ADAM_EPS) + ADAM_WD * w), m2, v2


def _adamw_t(wt_t, g, m_t, v_t, name):
    _, n, r = wt_t.shape
    ng = g.shape[1]
    tc = LANES

    def body(w_ref, g_ref, m_ref, v_ref, go_ref, d_ref, mo_ref, vo_ref, gt_ref):
        for lo in range(0, ng, LANES):
            gt_ref[lo:lo + LANES, :] = g_ref[:, lo:lo + LANES].T
        g_ = gt_ref[0:n, :]
        delta, m2, v2 = _adam_math(w_ref[...], g_, m_ref[...], v_ref[...])
        go_ref[...] = g_
        d_ref[...] = delta
        mo_ref[...] = m2
        vo_ref[...] = v2

    blk = pl.BlockSpec((None, n, tc), lambda i: (0, 0, i))
    return pl.pallas_call(
        body, name=name, grid=(r // tc,), in_specs=[blk, pl.BlockSpec((tc, ng), lambda i: (i, 0)), blk, blk],
        out_specs=[blk] * 4, out_shape=[SDS(wt_t.shape, F32)] * 4, scratch_shapes=[pltpu.VMEM((ng, tc), F32)],
        compiler_params=_cparams("parallel"),
    )(wt_t, g, m_t, v_t)


def _adamw(wt, g, m, v, name):
    lead = (None,) * (wt.ndim - 2)
    zero = (0,) * (wt.ndim - 2)
    r, n = wt.shape[-2:]
    ng = g.shape[1]
    nct = 2 if ng == n and n % (2 * LANES) == 0 else 1
    tc, tg = n // nct, ng // nct
    tr = _tile(r, max(8, (3 << 17) // tg // 8 * 8), 8)

    def body(w_ref, g_ref, m_ref, v_ref, go_ref, d_ref, mo_ref, vo_ref):
        g_ = g_ref[:, 0:tc]
        delta, m2, v2 = _adam_math(w_ref[...], g_, m_ref[...], v_ref[...])
        go_ref[...] = g_
        d_ref[...] = delta
        mo_ref[...] = m2
        vo_ref[...] = v2

    blk = pl.BlockSpec(lead + (tr, tc), lambda i, j: zero + (i, j))
    return pl.pallas_call(
        body, name=name, grid=(r // tr, nct), in_specs=[blk, pl.BlockSpec((tr, tg), lambda i, j: (i, j)), blk, blk],
        out_specs=[blk] * 4, out_shape=[SDS(wt.shape, F32)] * 4, compiler_params=_cparams("parallel", "parallel"),
    )(wt, g, m, v)


def _place():
    x, y, c = lax.axis_index("x"), lax.axis_index("y"), lax.axis_index("c")
    return x, y, c, 2 * x + y


def _chip_dev(k, c):
    return (k // 2, k % 2, c)


def _half(ref, j, h, rows, per):
    return ref.at[j // per, pl.ds((j % per) * rows + h * (rows // 2), rows // 2)]


def _gather_stage(bufs, metas, rows_of, ici_parts, fwd_parts, zero_pad):
    nw = len(bufs)
    ici_on = [i for i in range(nw) if ici_parts[i] is not None]
    fwd_on = [i for i in range(nw) if fwd_parts[i] is not None]
    pad_jobs = [(i, gi) for i in ici_on if ici_parts[i][0] == 0 and metas[i][1] > metas[i][2] * rows_of[i]
                for gi in range(metas[i][0])]

    def part_of(ref, i, j, h, part):
        per = metas[i][2]
        p, np_ = part
        pr = rows_of[i] // 2 // np_
        return ref.at[j // per, pl.ds((j % per) * rows_of[i] + h * (rows_of[i] // 2) + p * pr, pr)]

    def descriptors(ins, outs, sems):
        src, zp, dst = ins[:nw], ins[nw], outs
        pads, send, recv, fsend, frecv = sems
        x, y, c, me = _place()

        def pad(n):
            i, gi = pad_jobs[n]
            extra = metas[i][1] - metas[i][2] * rows_of[i]
            return pltpu.make_async_copy(zp.at[pl.ds(0, extra)], dst[i].at[gi, pl.ds(metas[i][2] * rows_of[i], extra)], pads.at[n])

        def ici(i, r, frm):
            return pltpu.make_async_remote_copy(
                src_ref=part_of(src[i], i, me, c, ici_parts[i]), dst_ref=part_of(dst[i], i, frm, c, ici_parts[i]),
                send_sem=send.at[i, r - 1], recv_sem=recv.at[i, r - 1], device_id=_chip_dev((me + r) % N_CHIPS, c),
                device_id_type=MESH)

        def d2d(i, r, frm, h):
            blk = part_of(dst[i], i, frm, h, fwd_parts[i])
            return pltpu.make_async_remote_copy(src_ref=blk, dst_ref=blk, send_sem=fsend.at[i, r - 1],
                                                recv_sem=frecv.at[i, r - 1], device_id=(x, y, 1 - c), device_id_type=MESH)

        return pad, ici, d2d, c, me

    def start(ins, outs, sems):
        pad, ici, d2d, c, me = descriptors(ins, outs, sems)
        for n in range(len(pad_jobs)):
            pad(n).start()
        for i in fwd_on:
            for r in range(1, N_CHIPS):
                d2d(i, r, (me - r) % N_CHIPS, c).start()
        for i in ici_on:
            for r in range(1, N_CHIPS):
                ici(i, r, me).start()

    def finish(ins, outs, sems):
        pad, ici, d2d, c, me = descriptors(ins, outs, sems)
        for i in fwd_on:
            for r in range(1, N_CHIPS):
                d2d(i, r, (me - r) % N_CHIPS, 1 - c).wait_recv()
                d2d(i, r, (me - r) % N_CHIPS, c).wait_send()
        for i in ici_on:
            for r in range(1, N_CHIPS):
                ici(i, r, (me - r) % N_CHIPS).wait_recv()
                ici(i, r, me).wait_send()
        for n in range(len(pad_jobs)):
            pad(n).wait()

    return _Stage(ins=list(bufs) + [zero_pad], out_shapes=[SDS(b.shape, b.dtype) for b in bufs],
                  aliases={i: i for i in range(nw)},
                  sems=[pltpu.SemaphoreType.DMA((max(len(pad_jobs), 1),))] + [pltpu.SemaphoreType.DMA((nw, N_CHIPS - 1))] * 4,
                  start=start, finish=finish)


def _gather_small(packed, name):
    r, n = packed.shape

    def body(src, dst, send, recv):
        x, y, c, me = _place()
        dst[me] = src[...]
        cps = []
        for d in range(1, N_CHIPS):
            cp = pltpu.make_async_remote_copy(src_ref=src, dst_ref=dst.at[me], send_sem=send.at[d - 1], recv_sem=recv.at[d - 1],
                                              device_id=_chip_dev((me + d) % N_CHIPS, c), device_id_type=MESH)
            cp.start()
            cps.append(cp)
        for d in range(1, N_CHIPS):
            pltpu.make_async_remote_copy(src_ref=src, dst_ref=dst.at[(me - d) % N_CHIPS], send_sem=send.at[d - 1],
                                         recv_sem=recv.at[d - 1], device_id=_chip_dev((me + d) % N_CHIPS, c),
                                         device_id_type=MESH).wait_recv()
        for cp in cps:
            cp.wait_send()

    return pl.pallas_call(
        body, name=name, in_specs=[VMEM_SPEC], out_specs=VMEM_SPEC, out_shape=SDS((N_CHIPS, r, n), F32),
        scratch_shapes=[pltpu.SemaphoreType.DMA((N_CHIPS - 1,))] * 2,
    )(packed)


def _all_reduce_small(packed, name):
    r, n = packed.shape

    def body(src, out, slots, send, recv):
        x, y, c, me = _place()
        idx = 2 * me + c
        slots[idx] = src[...]
        cps = []

        def peer(d):
            p = (idx + d) % N_DEV
            return (p // 4, (p // 2) % 2, p % 2)

        for d in range(1, N_DEV):
            cp = pltpu.make_async_remote_copy(src_ref=src, dst_ref=slots.at[idx], send_sem=send.at[d - 1], recv_sem=recv.at[d - 1],
                                              device_id=peer(d), device_id_type=MESH)
            cp.start()
            cps.append(cp)
        for d in range(1, N_DEV):
            pltpu.make_async_remote_copy(src_ref=src, dst_ref=slots.at[(idx - d) % N_DEV], send_sem=send.at[d - 1],
                                         recv_sem=recv.at[d - 1], device_id=peer(d), device_id_type=MESH).wait_recv()
        for cp in cps:
            cp.wait_send()
        acc = slots[0]
        for k in range(1, N_DEV):
            acc = acc + slots[k]
        out[...] = acc

    return pl.pallas_call(
        body, name=name, in_specs=[VMEM_SPEC], out_specs=VMEM_SPEC, out_shape=SDS((r, n), F32),
        scratch_shapes=[pltpu.VMEM((N_DEV, r, n), F32)] + [pltpu.SemaphoreType.DMA((N_DEV - 1,))] * 2,
    )(packed)


def _simple_stage(ins, out_shapes, aliases, n_copies, copies):
    def start(ins_, outs, sems):
        for cp in copies(ins_, outs, *sems):
            cp.start()

    def finish(ins_, outs, sems):
        for cp in copies(ins_, outs, *sems):
            cp.wait()

    return _Stage(ins=list(ins), out_shapes=list(out_shapes), aliases=aliases,
                  sems=[pltpu.SemaphoreType.DMA((n_copies,))] * 2, start=start, finish=finish)


def _rs_pair_exchange(grads, metas, rows_of):
    nw = len(grads)

    def copies(src, dst, send, recv):
        x, y, c, me = _place()
        return [pltpu.make_async_remote_copy(
            src_ref=_half(src[i], j, 1 - c, rows_of[i], metas[i][2]), dst_ref=dst[i].at[j], send_sem=send.at[i * N_CHIPS + j],
            recv_sem=recv.at[i * N_CHIPS + j], device_id=(x, y, 1 - c), device_id_type=MESH)
            for i in range(nw) for j in range(N_CHIPS)]

    out_shapes = [SDS((N_CHIPS, rows_of[i] // 2, g.shape[2]), g.dtype) for i, g in enumerate(grads)]
    return _simple_stage(grads, out_shapes, {}, nw * N_CHIPS, copies)


def _rs_pair_add(g, got, meta, rows, sp, name):
    per = meta[2]
    n = g.shape[2]
    hr = rows // 2
    tr = _tile(hr, max(16, (3 << 19) // n // 16 * 16), 16)

    def body(sp_ref, g_ref, got_ref, snd_ref, own_ref):
        j = pl.program_id(1)
        s = g_ref[...].astype(F32) + got_ref[...].astype(F32)
        snd_ref[...] = s.astype(BF16)

        @pl.when(j == sp_ref[1])
        def _():
            own_ref[...] = s

    grid_spec = pltpu.PrefetchScalarGridSpec(
        num_scalar_prefetch=1, grid=(hr // tr, N_CHIPS),
        in_specs=[pl.BlockSpec((None, tr, n), lambda i, j, sp: (j // per, ((j % per) * rows + sp[0] * hr) // tr + i, 0)),
                  pl.BlockSpec((None, tr, n), lambda i, j, sp: (j, i, 0))],
        out_specs=[pl.BlockSpec((None, tr, n), lambda i, j, sp: (j, i, 0)), pl.BlockSpec((tr, n), lambda i, j, sp: (i, 0))])
    return pl.pallas_call(
        body, name=name, grid_spec=grid_spec, out_shape=[SDS((N_CHIPS, hr, n), BF16), SDS((hr, n), F32)],
        compiler_params=_cparams("parallel", "arbitrary"),
    )(sp, g, got)


def _rs_chip_exchange(sends, part=(0, 1), prev=None):
    nw = len(sends)
    p, np_ = part

    def copies(src, dst, send, recv):
        x, y, c, me = _place()
        cps = []
        for i in range(nw):
            pr = sends[i].shape[1] // np_
            for r in range(1, N_CHIPS):
                cps.append(pltpu.make_async_remote_copy(
                    src_ref=src[i].at[(me + r) % N_CHIPS, pl.ds(p * pr, pr)], dst_ref=dst[i].at[r - 1, pl.ds(p * pr, pr)],
                    send_sem=send.at[i * (N_CHIPS - 1) + r - 1], recv_sem=recv.at[i * (N_CHIPS - 1) + r - 1],
                    device_id=_chip_dev((me + r) % N_CHIPS, c), device_id_type=MESH))
        return cps

    out_shapes = [SDS((N_CHIPS - 1,) + s.shape[1:], BF16) for s in sends]
    if prev is None:
        return _simple_stage(sends, out_shapes, {}, nw * (N_CHIPS - 1), copies)
    return _simple_stage(list(sends) + list(prev), out_shapes, {nw + i: i for i in range(nw)}, nw * (N_CHIPS - 1), copies)


def _rs_chip_add(own, got, sp, name):
    hr, n = own.shape
    tr = _tile(hr, max(16, (3 << 19) // n // 16 * 16), 16)

    def body(sp_ref, own_ref, got_ref, o_ref):
        acc = own_ref[...]
        for r in range(N_CHIPS - 1):
            acc = acc + got_ref[r].astype(F32)
        o_ref[...] = acc

    grid_spec = pltpu.PrefetchScalarGridSpec(
        num_scalar_prefetch=1, grid=(hr // tr,),
        in_specs=[pl.BlockSpec((tr, n), lambda i, sp: (i, 0)), pl.BlockSpec((N_CHIPS - 1, tr, n), lambda i, sp: (0, i, 0))],
        out_specs=pl.BlockSpec((tr, n), lambda i, sp: (sp[0] * (hr // tr) + i, 0)))
    return pl.pallas_call(body, name=name, grid_spec=grid_spec, out_shape=SDS((2 * hr, n), F32),
                          compiler_params=_cparams("parallel"))(sp, own, got)


def _rs_pair_share(blocks):
    nw = len(blocks)

    def copies(src, dst, send, recv):
        x, y, c, me = _place()
        cps = []
        for i in range(nw):
            hr = src[i].shape[0] // 2
            cps.append(pltpu.make_async_remote_copy(
                src_ref=src[i].at[pl.ds(c * hr, hr)], dst_ref=dst[i].at[pl.ds(c * hr, hr)], send_sem=send.at[i],
                recv_sem=recv.at[i], device_id=(x, y, 1 - c), device_id_type=MESH))
        return cps

    return _simple_stage(blocks, [SDS(b.shape, b.dtype) for b in blocks], {i: i for i in range(nw)}, nw, copies)


def kernel(x, p, ln_g, ln_b, ffn1_w_in, ffn1_w_out, mix_w_in, conv_w, hg_lower_bound, hg_norm_w, branch_w_conv, branch_w_hgrn, mix_w_out, ffn2_w_in, ffn2_w_out, ple_w_gate, ple_w_proj, loss_target, m_ln_g, m_ln_b, m_ffn1_w_in, m_ffn1_w_out, m_mix_w_in, m_conv_w, m_hg_lower_bound, m_hg_norm_w, m_branch_w_conv, m_branch_w_hgrn, m_mix_w_out, m_ffn2_w_in, m_ffn2_w_out, m_ple_w_gate, m_ple_w_proj, v_ln_g, v_ln_b, v_ffn1_w_in, v_ffn1_w_out, v_mix_w_in, v_conv_w, v_hg_lower_bound, v_hg_norm_w, v_branch_w_conv, v_branch_w_hgrn, v_mix_w_out, v_ffn2_w_in, v_ffn2_w_out, v_ple_w_gate, v_ple_w_proj):
    assert ln_g.shape[0] == DEPTH and x.shape[0] == 1 and p.shape[:2] == (1, 1)
    t, d = x.shape[1], x.shape[2]
    w = d // 2
    x0 = x.reshape(t, d)
    x0b = _to_bf16(x0, "x_bf16")
    pe = p.reshape(t, p.shape[-1])
    target = loss_target.reshape(t, d)
    cx, cy, cc = lax.axis_index("x"), lax.axis_index("y"), lax.axis_index("c")
    chip = 2 * cx + cy
    sp = jnp.stack([cc, chip]).astype(jnp.int32)

    big = dict(ffn1_w_in=ffn1_w_in, ffn1_w_out=ffn1_w_out, mix_w_in=mix_w_in, branch_w_conv=branch_w_conv,
               branch_w_hgrn=branch_w_hgrn, mix_w_out=mix_w_out, ffn2_w_in=ffn2_w_in, ffn2_w_out=ffn2_w_out,
               ple_w_gate=ple_w_gate, ple_w_proj=ple_w_proj)
    moments = dict(ffn1_w_in=(m_ffn1_w_in, v_ffn1_w_in), ffn1_w_out=(m_ffn1_w_out, v_ffn1_w_out), mix_w_in=(m_mix_w_in, v_mix_w_in),
                   branch_w_conv=(m_branch_w_conv, v_branch_w_conv), branch_w_hgrn=(m_branch_w_hgrn, v_branch_w_hgrn),
                   mix_w_out=(m_mix_w_out, v_mix_w_out), ffn2_w_in=(m_ffn2_w_in, v_ffn2_w_in), ffn2_w_out=(m_ffn2_w_out, v_ffn2_w_out),
                   ple_w_gate=(m_ple_w_gate, v_ple_w_gate), ple_w_proj=(m_ple_w_proj, v_ple_w_proj))
    names = list(big)

    n_loc = ffn1_w_in.shape[-1]
    n_pad = -(-n_loc // LANES) * LANES
    assert mix_w_in.shape[-1] % LANES == 0 and ffn1_w_out.shape[1] * 2 == n_loc
    pad_cols = dict(ffn1_w_in=n_pad, ffn2_w_in=n_pad)
    meta = {k: (N_CHIPS, big[k].shape[1], 1) for k in names}
    meta["ffn1_w_out"] = meta["ffn2_w_out"] = (2, n_pad, 2)
    rows = {k: big[k].shape[1] for k in names}
    swap = lambda a: jnp.transpose(a, (0, 2, 1))
    wbuf = {}
    zero_pad = jnp.zeros((max(n_pad - n_loc, 16), d), BF16)

    def cast(k, comm=()):
        if k in pad_cols:
            return _cast_pad_t(swap(big[k]), pad_cols[k], meta[k], sp, "cast_" + k, comm=comm)
        return _cast_pad(big[k], big[k].shape[2], meta[k], sp, "cast_" + k, comm=comm)

    def gather(ici=(), fwd=()):
        ks = list(dict.fromkeys([k for k, _, _ in ici] + [k for k, _, _ in fwd]))
        ip = {k: (p_, n_) for k, p_, n_ in ici}
        fp = {k: (p_, n_) for k, p_, n_ in fwd}
        return _gather_stage([wbuf[k] for k in ks], [meta[k] for k in ks], [rows[k] for k in ks], [ip.get(k) for k in ks],
                             [fp.get(k) for k in ks], zero_pad), ks

    def gathered(ks, outs):
        wbuf.update(zip(ks, outs))

    def w3(k):
        return wbuf[k]

    def w2(k):
        return wbuf[k].reshape(-1, wbuf[k].shape[2])

    dq, wq = d // N_CHIPS, w // N_CHIPS
    small = jnp.concatenate([ln_g[0], ln_b[0], jnp.pad(conv_w[0], ((0, 5), (0, dq - wq)))], axis=0)
    small = _gather_small(small, "gather_small")
    lng = small[:, 0:4, :].transpose(1, 0, 2).reshape(4, 1, d)
    lnb = small[:, 4:8, :].transpose(1, 0, 2).reshape(4, 1, d)
    cw = small[:, 8:11, :wq].transpose(1, 0, 2).reshape(3, w)
    hg = hg_lower_bound
    nw_ = hg_norm_w

    one = lambda *ks_: [(k, 0, 1) for k in ks_]
    wbuf["ffn1_w_in"] = cast("ffn1_w_in")
    first = ["ple_w_proj", "ffn1_w_out", "mix_w_in", None, "ffn2_w_in", "ffn2_w_out"]
    carriers = first + [k for k in names if k != "ffn1_w_in" and k not in first]
    assert len(carriers) > FIRST_GATHER_PARTS
    for step, k in enumerate(carriers):
        ici = [("ffn1_w_in", step, FIRST_GATHER_PARTS)] if step < FIRST_GATHER_PARTS else []
        fwd = [("ffn1_w_in", step - 1, FIRST_GATHER_PARTS)] if 1 <= step <= FIRST_GATHER_PARTS else []
        ici += one("ple_w_proj") if step == 1 else []
        fwd += one("ple_w_proj") if step == 2 else []
        if ici or fwd:
            st, ks = gather(ici, fwd)
            if k is None:
                pp, got = _mm(pe, w3("ple_w_proj"), name="ple_proj", b_blocked=True, tn=512, comm=[st])
            else:
                wbuf[k], got = cast(k, comm=[st])
            gathered(ks, got)
        else:
            wbuf[k] = cast(k)
    st, ks = gather(ici=one("ffn1_w_out") + [("mix_w_in", 0, 2)])
    z1, got = _mm(x0b, w3("ffn1_w_in"), name="ffn1_in", b_blocked=True, out_dtype=BF16, tm=1024, comm=[st])
    gathered(ks, got)
    st, ks = gather(fwd=one("ffn1_w_out") + [("mix_w_in", 0, 2)])
    h1, got = _swiglu_fwd(z1, "ffn1_act", comm=[st])
    gathered(ks, got)
    st, ks = gather(ici=[("mix_w_in", 1, 2)])
    y1, got = _mm(h1, w2("ffn1_w_out"), name="ffn1_out", tm=1024, tn=1024, tk=2816, comm=[st])
    gathered(ks, got)
    st, ks = gather(fwd=[("mix_w_in", 1, 2)])
    (r1, x1b), got = _ln_fwd(x0, y1, lng[0], lnb[0], None, None, 0.5, "ln0", comm=[st])
    gathered(ks, got)
    mixo_w = one("branch_w_conv", "branch_w_hgrn", "mix_w_out")
    st, ks = gather(ici=mixo_w + [("ffn2_w_in", 0, 2)])
    z, got = _mm(x1b, w3("mix_w_in"), name="mix_in", b_blocked=True, tm=1024, comm=[st])
    gathered(ks, got)
    ya = _conv_fwd(z, cw, w, "conv_fwd")
    st, ks = gather(ici=[("ffn2_w_in", 1, 2)], fwd=mixo_w + [("ffn2_w_in", 0, 2)])
    (yb, o_h, states), got = _hgrn_fwd(z, hg, nw_, w, "hgrn_fwd", comm=[st])
    gathered(ks, got)
    ma = _mm(ya, w3("branch_w_conv"), name="branch_conv", b_blocked=True, tn=512)
    mb = _mm(yb, w3("branch_w_hgrn"), name="branch_hgrn", b_blocked=True, tn=512)
    merged = _merge_fwd(z, ma, mb, w, "merge_fwd")
    st, ks = gather(fwd=[("ffn2_w_in", 1, 2)])
    y2, got = _mm(merged, w2("mix_w_out"), name="mix_out", tn=1024, comm=[st])
    gathered(ks, got)
    r2, x2b = _ln_fwd(r1, y2, lng[1], lnb[1], lng[0], lnb[0], 1.0, "ln1")
    late = one("ffn2_w_out", "ple_w_gate")
    st, ks = gather(ici=late)
    z3, got = _mm(x2b, w3("ffn2_w_in"), name="ffn2_in", b_blocked=True, out_dtype=BF16, tm=1024, comm=[st])
    gathered(ks, got)
    st, ks = gather(fwd=late)
    h3, got = _swiglu_fwd(z3, "ffn2_act", comm=[st])
    gathered(ks, got)
    y3 = _mm(h3, w2("ffn2_w_out"), name="ffn2_out", tm=1024, tn=1024, tk=2816)
    r3, x3b = _ln_fwd(r2, y3, lng[2], lnb[2], lng[1], lnb[1], 0.5, "ln2")
    gp = _mm(x3b, w2("ple_w_gate"), name="ple_gate", tn=1024)
    dr4, dgp, dpp, dg3, db3, sq = _tail(r3, lng[2], lnb[2], gp, pp, lng[3], lnb[3], target, "tail")

    grads, sends, owns, blocks, outs = {}, {}, {}, {}, {}

    def pair_exchange(*ks):
        return _rs_pair_exchange([grads[k] for k in ks], [meta[k] for k in ks], [rows[k] for k in ks])

    def pair_add(ks, got):
        for k, g_ in zip(ks, got):
            sends[k], owns[k] = _rs_pair_add(grads[k], g_, meta[k], rows[k], sp, "rs_pair_add_" + k)

    def chip_exchange(*ks):
        return _rs_chip_exchange([sends[k] for k in ks])

    def chip_add(ks, got):
        for k, g_ in zip(ks, got):
            blocks[k] = _rs_chip_add(owns[k], g_, sp, "rs_chip_add_" + k)

    def pair_share(*ks):
        return _rs_pair_share([blocks[k] for k in ks])

    def update(ks, full):
        for k, g_ in zip(ks, full):
            m_, v_ = moments[k]
            if k in pad_cols:
                outs[k] = [swap(a) for a in _adamw_t(swap(big[k]), g_, swap(m_), swap(v_), "adamw_" + k)]
            else:
                outs[k] = _adamw(big[k], g_, m_, v_, "adamw_" + k)

    ple = ("ple_w_gate", "ple_w_proj")
    mixo = ("mix_w_out", "branch_w_conv", "branch_w_hgrn")
    dx3m = _mm(dgp, w2("ple_w_gate"), name="d_ple_gate_x", tb=True, tn=1024, tk=2048)
    grads["ple_w_gate"] = _mm(x3b, dgp, name="d_ple_gate_w", ta=True, out_dtype=BF16, tm=1024, tk=2048, tn=1024).reshape(N_CHIPS, -1, d)
    grads["ple_w_proj"] = _mm(pe, dpp, name="d_ple_proj_w", ta=True, out_dtype=BF16, out_blocked=N_CHIPS, tk=2048, tn=512)
    dr3, dy3b, dg2, db2 = _ln_bwd(dr4, dx3m, r3, lng[2], 0.5, "ln2_bwd")
    late_w = ple + ("ffn2_w_out",)
    dh3 = _mm(dy3b, w2("ffn2_w_out"), name="d_ffn2_out_x", tb=True, out_dtype=BF16, tn=1408, tk=2048)
    grads["ffn2_w_out"] = _mm(h3, dy3b, name="d_ffn2_out_w", ta=True, out_dtype=BF16, tm=1408, tk=2048, tn=1024).reshape(2, n_pad, d)
    dz3 = _swiglu_bwd(dh3, z3, "ffn2_act_bwd")
    dx2m, got = _mm(dz3, w3("ffn2_w_in"), name="d_ffn2_in_x", tb=True, b_blocked=True, tm=1024, tn=1024, tk=2816,
                    comm=[pair_exchange(*late_w)])
    pair_add(late_w, got)
    grads["ffn2_w_in"], got = _mm(x2b, dz3, name="d_ffn2_in_w", ta=True, out_dtype=BF16, out_blocked=N_CHIPS, tk=4096, comm=[chip_exchange(*late_w)])
    chip_add(late_w, got)
    dr2, dy2b, dg1, db1 = _ln_bwd(dr3, dx2m, r2, lng[1], 1.0, "ln1_bwd")
    dmer, got = _mm(dy2b, w2("mix_w_out"), name="d_mix_out_x", tb=True, tn=1024, tk=2048, comm=[pair_exchange("ffn2_w_in")])
    pair_add(["ffn2_w_in"], got)
    g_, full = _mm(merged, dy2b, name="d_mix_out_w", ta=True, out_dtype=BF16, tm=1024, tk=2048, tn=1024, comm=[pair_share(*late_w)])
    grads["mix_w_out"] = g_.reshape(N_CHIPS, -1, d)
    update(late_w, full)
    dma, dmb, dgc, dgh = _merge_bwd(dmer, z, ma, mb, w, "merge_bwd")
    dya = _mm(dma, w3("branch_w_conv"), name="d_branch_conv_x", tb=True, b_blocked=True, tn=1024, tk=512)
    dyb = _mm(dmb, w3("branch_w_hgrn"), name="d_branch_hgrn_x", tb=True, b_blocked=True, tn=1024, tk=512)
    grads["branch_w_conv"] = _mm(ya, dma, name="d_branch_conv_w", ta=True, out_dtype=BF16, out_blocked=N_CHIPS, tm=1024, tk=2048, tn=512)
    grads["branch_w_hgrn"] = _mm(yb, dmb, name="d_branch_hgrn_w", ta=True, out_dtype=BF16, out_blocked=N_CHIPS, tm=1024, tk=2048, tn=512)
    dbg, dcg, dhc, dcw = _conv_bwd(dya, z, cw, w, "conv_bwd")
    (dq_, df_, di_, dgr_, dhg, dnw), got2, got = _hgrn_bwd(dyb, z, o_h, states, hg, nw_, w, "hgrn_bwd",
                                                            comm=[chip_exchange("ffn2_w_in"), pair_exchange(*mixo)])
    chip_add(["ffn2_w_in"], got2)
    pair_add(mixo, got)
    dz = _concat_cols([dbg, dcg, dhc, dq_, df_, di_, dgr_, dgc, dgh], "dz_concat")
    dx1m, full, got = _mm(dz, w3("mix_w_in"), name="d_mix_in_x", tb=True, b_blocked=True, tm=1024, tn=1024, tk=2816,
                          comm=[pair_share("ffn2_w_in"), chip_exchange(*mixo)])
    update(["ffn2_w_in"], full)
    chip_add(mixo, got)
    grads["mix_w_in"], full = _mm(x1b, dz, name="d_mix_in_w", ta=True, out_dtype=BF16, out_blocked=N_CHIPS, tk=4096, comm=[pair_share(*mixo)])
    update(mixo, full)
    dr1, dy1b, dg0, db0 = _ln_bwd(dr2, dx1m, r1, lng[0], 0.5, "ln0_bwd")
    dh1, got = _mm(dy1b, w2("ffn1_w_out"), name="d_ffn1_out_x", tb=True, out_dtype=BF16, tn=1408, tk=2048,
                   comm=[pair_exchange("mix_w_in")])
    pair_add(["mix_w_in"], got)
    mix_sends = [sends["mix_w_in"]]
    g_, got_a = _mm(h1, dy1b, name="d_ffn1_out_w", ta=True, out_dtype=BF16, tm=1408, tk=2048, tn=1024, comm=[_rs_chip_exchange(mix_sends, (0, 2))])
    grads["ffn1_w_out"] = g_.reshape(2, n_pad, d)
    dz1 = _swiglu_bwd(dh1, z1, "ffn1_act_bwd")
    g_other, got2, got = _mm(x0b, dz1, name="d_ffn1_in_w_other", ta=True, out_dtype=BF16, out_blocked=N_CHIPS, tk=4096, half=(sp, True),
                             comm=[_rs_chip_exchange(mix_sends, (1, 2), got_a), pair_exchange("ffn1_w_out")])
    chip_add(["mix_w_in"], got2)
    pair_add(["ffn1_w_out"], got)
    grads["ffn1_w_in"], full, got2, got = _mm(
        x0b, dz1, name="d_ffn1_in_w_own", ta=True, out_dtype=BF16, out_blocked=N_CHIPS, tk=4096, half=(sp, False),
        comm=[pair_share("mix_w_in"), chip_exchange("ffn1_w_out"),
              _rs_pair_exchange([g_other], [meta["ffn1_w_in"]], [rows["ffn1_w_in"]])])
    update(["mix_w_in"], full)
    chip_add(["ffn1_w_out"], got2)
    pair_add(["ffn1_w_in"], got)
    dx0, got2, full = _mm(dz1, w3("ffn1_w_in"), name="d_ffn1_in_x", tb=True, b_blocked=True, tm=1024, tn=1024, tk=2816,
                          add=(dr1, ALPHA), comm=[chip_exchange("ffn1_w_in"), pair_share("ffn1_w_out")])
    chip_add(["ffn1_w_in"], got2)
    update(["ffn1_w_out"], full)
    grad_x = dx0.reshape(x.shape)
    update(["ffn1_w_in"], _run_stages([pair_share("ffn1_w_in")], "rs_tail_pair")[0])

    pack = jnp.concatenate([
        dg0, dg1, dg2, dg3, db0, db1, db2, db3,
        jnp.pad(dcw, ((0, 0), (0, d - w))), jnp.pad(dhg, ((0, 0), (0, d - w))),
        jnp.pad(jnp.sum(dnw.reshape(-1, HEAD), axis=0, keepdims=True), ((0, 0), (0, d - HEAD))), sq], axis=0)
    pack = _all_reduce_small(jnp.pad(pack, ((0, 1), (0, 0))), "reduce_small")
    loss = (0.5 / d) * jnp.sum(pack[14])
    g_ln_g = lax.dynamic_slice_in_dim(pack[0:4], chip * dq, dq, axis=1)
    g_ln_b = lax.dynamic_slice_in_dim(pack[4:8], chip * dq, dq, axis=1)
    g_conv = lax.dynamic_slice_in_dim(pack[8:11, :w], chip * wq, wq, axis=1)
    g_hg = pack[11:13, :w]
    g_nw = pack[13:14, :HEAD]

    small_w = dict(ln_g=(ln_g, g_ln_g, m_ln_g, v_ln_g), ln_b=(ln_b, g_ln_b, m_ln_b, v_ln_b),
                   conv_w=(conv_w, g_conv, m_conv_w, v_conv_w), hg_lower_bound=(hg_lower_bound, g_hg, m_hg_lower_bound, v_hg_lower_bound),
                   hg_norm_w=(hg_norm_w, g_nw, m_hg_norm_w, v_hg_norm_w))
    for k, (w_, g_, m_, v_) in small_w.items():
        outs[k] = _adamw(w_, g_.reshape(-1, w_.shape[-1]), m_, v_, "adamw_" + k)

    order = ["ln_g", "ln_b", "ffn1_w_in", "ffn1_w_out", "mix_w_in", "conv_w", "hg_lower_bound", "hg_norm_w", "branch_w_conv",
             "branch_w_hgrn", "mix_w_out", "ffn2_w_in", "ffn2_w_out", "ple_w_gate", "ple_w_proj"]
    return (loss, grad_x, *[outs[k][0] for k in order], *[outs[k][1] for k in order], *[outs[k][2] for k in order],
            *[outs[k][3] for k in order])
```

```python
import collections
import functools

import jax
import jax.numpy as jnp
from jax import lax
from jax.experimental import pallas as pl
from jax.experimental.pallas import tpu as pltpu

F32 = jnp.float32
BF16 = jnp.bfloat16
MESH = pl.DeviceIdType.MESH
ANY = pl.BlockSpec(memory_space=pl.ANY)
VMEM_SPEC = pl.BlockSpec(memory_space=pltpu.VMEM)
SDS = jax.ShapeDtypeStruct

DEPTH = 1
ALPHA = (2.0 * DEPTH) ** 0.25
LN_EPS = 1e-5
RMS_EPS = 1e-6
CHUNK = 32
HEAD = 128
ADAM_LR, ADAM_B1, ADAM_B2, ADAM_EPS, ADAM_WD, ADAM_STEP = 0.001, 0.9, 0.999, 1e-08, 0.01, 10

LANES = 128
N_CHIPS = 4
N_DEV = 8
FIRST_GATHER_PARTS = 4
VMEM_LIMIT = 52 * 1024 * 1024
MM_PIECE = 512


def _cparams(*sem):
    if sem:
        return pltpu.CompilerParams(dimension_semantics=sem, vmem_limit_bytes=VMEM_LIMIT)
    return pltpu.CompilerParams(vmem_limit_bytes=VMEM_LIMIT)


def _tile(n, target, mult):
    best = None
    for t in range(mult, min(n, target) + 1, mult):
        if n % t == 0:
            best = t
    return best if best is not None else n


def _sigmoid(x):
    return 1.0 / (1.0 + jnp.exp(-x))


_Stage = collections.namedtuple("_Stage", "ins out_shapes aliases sems start finish")


def _hosted_call(compute, stages, *, name, grid, in_specs, out_specs, out_shape, scratch_shapes, operands, parallel,
                 prefetch=None):
    n_cmp, n_out, n_scr = len(in_specs), len(out_specs), len(scratch_shapes)
    n_in = n_cmp
    n_pre = int(prefetch is not None)
    c_in = [len(s.ins) for s in stages]
    c_out = [len(s.out_shapes) for s in stages]
    c_sem = [len(s.sems) for s in stages]
    aliases = {}
    for si, s in enumerate(stages):
        for a_in, a_out in s.aliases.items():
            aliases[n_pre + n_in + sum(c_in[:si]) + a_in] = n_out + sum(c_out[:si]) + a_out

    def body(*refs):
        refs = refs[n_pre:]
        ins = refs[:n_cmp]
        cins = refs[n_in:n_in + sum(c_in)]
        outs = refs[n_in + sum(c_in):n_in + sum(c_in) + n_out]
        couts = refs[n_in + sum(c_in) + n_out:n_in + sum(c_in) + n_out + sum(c_out)]
        scr = refs[n_in + sum(c_in) + n_out + sum(c_out):][:n_scr]
        sems = refs[n_in + sum(c_in) + n_out + sum(c_out) + n_scr:]

        def stage_refs(si):
            return (cins[sum(c_in[:si]):sum(c_in[:si + 1])], couts[sum(c_out[:si]):sum(c_out[:si + 1])],
                    sems[sum(c_sem[:si]):sum(c_sem[:si + 1])])

        if stages:
            first = functools.reduce(jnp.logical_and, [pl.program_id(ax) == 0 for ax in range(len(grid))])
            last = functools.reduce(jnp.logical_and, [pl.program_id(ax) == grid[ax] - 1 for ax in range(len(grid))])

            @pl.when(first)
            def _():
                for si, s in enumerate(stages):
                    s.start(*stage_refs(si))

        compute(*ins, *outs, *scr)
        if stages:
            @pl.when(last)
            def _():
                for si, s in enumerate(stages):
                    s.finish(*stage_refs(si))

    sem = ("arbitrary",) * len(grid) if stages else ("parallel",) * parallel + ("arbitrary",) * (len(grid) - parallel)
    all_in = list(in_specs) + [ANY] * (n_in - n_cmp + sum(c_in))
    all_out = list(out_specs) + [ANY] * sum(c_out)
    all_scr = list(scratch_shapes) + [q for s in stages for q in s.sems]
    all_shape = list(out_shape) + [o for s in stages for o in s.out_shapes]
    args = list(operands) + [a for s in stages for a in s.ins]
    if prefetch is None:
        res = pl.pallas_call(body, name=name, grid=grid, in_specs=all_in, out_specs=all_out, out_shape=all_shape,
                             input_output_aliases=aliases, scratch_shapes=all_scr, compiler_params=_cparams(*sem))(*args)
    else:
        grid_spec = pltpu.PrefetchScalarGridSpec(num_scalar_prefetch=1, grid=grid, in_specs=all_in, out_specs=all_out,
                                                 scratch_shapes=all_scr)
        res = pl.pallas_call(body, name=name, grid_spec=grid_spec, out_shape=all_shape, input_output_aliases=aliases,
                             compiler_params=_cparams(*sem))(prefetch, *args)
    main = res[0] if n_out == 1 else list(res[:n_out])
    if not stages:
        return main
    rest = res[n_out:]
    return (main, *[list(rest[sum(c_out[:si]):sum(c_out[:si + 1])]) for si in range(len(stages))])


def _run_stages(stages, name):
    def body(*refs):
        n_i = sum(len(s.ins) for s in stages)
        n_o = sum(len(s.out_shapes) for s in stages)
        cins, couts, sems = refs[:n_i], refs[n_i:n_i + n_o], refs[n_i + n_o:]
        pos = [0, 0, 0]
        parts = []
        for s in stages:
            parts.append((cins[pos[0]:pos[0] + len(s.ins)], couts[pos[1]:pos[1] + len(s.out_shapes)], sems[pos[2]:pos[2] + len(s.sems)]))
            pos = [pos[0] + len(s.ins), pos[1] + len(s.out_shapes), pos[2] + len(s.sems)]
        for s, p_ in zip(stages, parts):
            s.start(*p_)
        for s, p_ in zip(stages, parts):
            s.finish(*p_)

    aliases, ni, no = {}, 0, 0
    for s in stages:
        for a_in, a_out in s.aliases.items():
            aliases[ni + a_in] = no + a_out
        ni, no = ni + len(s.ins), no + len(s.out_shapes)
    res = pl.pallas_call(
        body, name=name, in_specs=[ANY] * ni, out_specs=[ANY] * no, out_shape=[o for s in stages for o in s.out_shapes],
        input_output_aliases=aliases, scratch_shapes=[q for s in stages for q in s.sems],
    )(*[a for s in stages for a in s.ins])
    out, pos = [], 0
    for s in stages:
        out.append(list(res[pos:pos + len(s.out_shapes)]))
        pos += len(s.out_shapes)
    return out


def _mm(a, b, *, name, ta=False, tb=False, b_blocked=False, out_blocked=0, out_dtype=F32,
        tm=512, tn=1408, tk=2048, comm=(), half=None, add=None):
    if ta:
        kd, m = a.shape
    else:
        m, kd = a.shape
    if b_blocked and not tb:
        g, kb, nb = b.shape
        assert kb == kd
        n = g * nb
        tn = _tile(nb, tn, LANES)
        tk = _tile(kd, tk, LANES)
        per_n = nb // tn
        b_spec = pl.BlockSpec((None, tk, tn), lambda i, j, k, *s: (j // per_n, k, j % per_n))
    elif b_blocked and tb:
        g, n, kb = b.shape
        assert g * kb == kd
        tn = _tile(n, tn, LANES)
        tk = _tile(kb, tk, LANES)
        per_k = kb // tk
        b_spec = pl.BlockSpec((None, tn, tk), lambda i, j, k, *s: (k // per_k, j, k % per_k))
    elif tb:
        n, kb = b.shape
        assert kb == kd
        tn = _tile(n, tn, LANES)
        tk = _tile(kd, tk, LANES)
        b_spec = pl.BlockSpec((tn, tk), lambda i, j, k, *s: (j, k))
    else:
        kb, n = b.shape
        assert kb == kd
        tn = _tile(n // out_blocked if out_blocked else n, tn, LANES)
        per_o = (n // out_blocked) // tn if out_blocked else None
        tk = _tile(kd, tk, LANES)
        b_spec = pl.BlockSpec((tk, tn), lambda i, j, k, *s: (k, j))
    m_run = m // 2 if half else m
    tm = _tile(m_run, tm, LANES if ta else 8)

    def row(i, s):
        if not half:
            return i
        h = 1 - s[0][0] if half[1] else s[0][0]
        return h * (m_run // tm) + i

    if ta:
        a_spec = pl.BlockSpec((tk, tm), lambda i, j, k, *s: (k, row(i, s)))
    else:
        a_spec = pl.BlockSpec((tm, tk), lambda i, j, k, *s: (row(i, s), k))
    if out_blocked:
        assert not b_blocked and not tb
        o_spec = pl.BlockSpec((None, tm, tn), lambda i, j, k, *s: (j // per_o, row(i, s), j % per_o))
        o_shape = SDS((out_blocked, m, n // out_blocked), out_dtype)
    else:
        o_spec = pl.BlockSpec((tm, tn), lambda i, j, k, *s: (row(i, s), j))
        o_shape = SDS((m, n), out_dtype)
    nk = kd // tk
    dn = (((0 if ta else 1,), (1 if tb else 0,)), ((), ()))
    grid = (m_run // tm, n // tn, nk)

    pieces = [(lo, min(MM_PIECE, tn - lo)) for lo in range(0, tn, MM_PIECE)]

    def compute(a_ref, b_ref, *rest):
        add_ref = rest[0] if add else None
        o_ref, acc_ref = rest[-2:]
        a_tile = a_ref[...].astype(BF16)
        k = pl.program_id(2)

        def result(acc, cols):
            if add:
                acc = acc + add[1] * add_ref[:, cols]
            return acc.astype(o_ref.dtype)

        if nk > 1:
            @pl.when(k == 0)
            def _():
                acc_ref[...] = jnp.zeros_like(acc_ref)

        for lo, wd in pieces:
            cols = slice(lo, lo + wd)
            b_tile = b_ref[cols, :] if tb else b_ref[:, cols]
            part = lax.dot_general(a_tile, b_tile.astype(BF16), dn, preferred_element_type=F32)
            if nk == 1:
                o_ref[:, cols] = result(part, cols)
            else:
                acc_ref[:, cols] += part

        if nk > 1:
            @pl.when(k == nk - 1)
            def _():
                o_ref[...] = result(acc_ref[...], slice(None))

    extra = [(add[0], o_spec)] if add else []
    return _hosted_call(compute, comm, name=name, grid=grid, in_specs=[a_spec, b_spec] + [s_ for _, s_ in extra], out_specs=[o_spec],
                        out_shape=[o_shape], scratch_shapes=[pltpu.VMEM((tm, tn), F32)], operands=(a, b, *[a_ for a_, _ in extra]),
                        parallel=2, prefetch=half[0] if half else None)


def _swiglu_fwd(z, name, comm=()):
    t, n = z.shape
    n2 = n // 2
    tr = _tile(t, 128, 16)

    def body(a_ref, u_ref, o_ref):
        a = a_ref[...].astype(F32)
        o_ref[...] = (a * _sigmoid(a) * u_ref[...].astype(F32)).astype(o_ref.dtype)

    return _hosted_call(
        body, comm, name=name, grid=(t // tr,),
        in_specs=[pl.BlockSpec((tr, n2), lambda i: (i, 0)), pl.BlockSpec((tr, n2), lambda i: (i, 1))],
        out_specs=[pl.BlockSpec((tr, n2), lambda i: (i, 0))], out_shape=[SDS((t, n2), BF16)], scratch_shapes=[],
        operands=(z, z), parallel=1)


def _swiglu_bwd(dh, z, name):
    t, n = z.shape
    n2 = n // 2
    tr = _tile(t, 128, 16)

    def body(dh_ref, a_ref, u_ref, o_ref):
        a = a_ref[...].astype(F32)
        dh_ = dh_ref[...].astype(F32)
        s = _sigmoid(a)
        o_ref[:, 0:n2] = (dh_ * u_ref[...].astype(F32) * (s * (1.0 + a * (1.0 - s)))).astype(o_ref.dtype)
        o_ref[:, n2:n] = (dh_ * a * s).astype(o_ref.dtype)

    return pl.pallas_call(
        body, name=name, grid=(t // tr,),
        in_specs=[pl.BlockSpec((tr, n2), lambda i: (i, 0)), pl.BlockSpec((tr, n2), lambda i: (i, 0)),
                  pl.BlockSpec((tr, n2), lambda i: (i, 1))],
        out_specs=pl.BlockSpec((tr, n), lambda i: (i, 0)), out_shape=SDS((t, n), BF16),
        compiler_params=_cparams("parallel"),
    )(dh, z, z)


def _ln_stats(r):
    mu = jnp.mean(r, axis=-1, keepdims=True)
    xc = r - mu
    var = jnp.mean(xc * xc, axis=-1, keepdims=True)
    return xc * lax.rsqrt(var + LN_EPS)


def _ln_fwd(xp, y, g, b, gp, bp, scale, name, comm=()):
    t, d = xp.shape
    tr = _tile(t, 256, 16)

    def body(xp_ref, y_ref, g_ref, b_ref, *rest):
        r_ref, xb_ref = rest[-2:]
        x_prev = xp_ref[...]
        if gp is not None:
            x_prev = _ln_stats(x_prev) * rest[0][...] + rest[1][...]
        r = ALPHA * x_prev + scale * y_ref[...]
        r_ref[...] = r
        xb_ref[...] = (_ln_stats(r) * g_ref[...] + b_ref[...]).astype(BF16)

    row = pl.BlockSpec((tr, d), lambda i: (i, 0))
    vec = pl.BlockSpec((1, d), lambda i: (0, 0))
    prev = [] if gp is None else [gp, bp]
    return _hosted_call(
        body, comm, name=name, grid=(t // tr,), in_specs=[row, row, vec, vec] + [vec] * len(prev), out_specs=[row, row],
        out_shape=[SDS((t, d), F32), SDS((t, d), BF16)], scratch_shapes=[], operands=(xp, y, g, b, *prev), parallel=1)


def _ln_bwd(dra, dxm, r, g, scale, name):
    t, d = r.shape
    tr = _tile(t, 256, 16)

    def body(dra_ref, dxm_ref, r_ref, g_ref, dr_ref, dyb_ref, dg_ref, db_ref):
        i = pl.program_id(0)
        dx = ALPHA * dra_ref[...] + dxm_ref[...]
        rr = r_ref[...]
        mu = jnp.mean(rr, axis=-1, keepdims=True)
        xc = rr - mu
        rstd = lax.rsqrt(jnp.mean(xc * xc, axis=-1, keepdims=True) + LN_EPS)
        xh = xc * rstd
        dxh = dx * g_ref[...]
        dr = rstd * (dxh - jnp.mean(dxh, axis=-1, keepdims=True) - xh * jnp.mean(dxh * xh, axis=-1, keepdims=True))
        dr_ref[...] = dr
        dyb_ref[...] = (scale * dr).astype(BF16)
        dg = jnp.sum(dx * xh, axis=0, keepdims=True)
        db = jnp.sum(dx, axis=0, keepdims=True)

        @pl.when(i == 0)
        def _():
            dg_ref[...] = dg
            db_ref[...] = db

        @pl.when(i > 0)
        def _():
            dg_ref[...] += dg
            db_ref[...] += db

    row = pl.BlockSpec((tr, d), lambda i: (i, 0))
    vec = pl.BlockSpec((1, d), lambda i: (0, 0))
    return pl.pallas_call(
        body, name=name, grid=(t // tr,), in_specs=[row, row, row, vec], out_specs=[row, row, vec, vec],
        out_shape=[SDS((t, d), F32), SDS((t, d), BF16), SDS((1, d), F32), SDS((1, d), F32)],
        compiler_params=_cparams("arbitrary"),
    )(dra, dxm, r, g)


def _tail(r3, g3, b3, gp, pp, g, b, target, name):
    t, d = r3.shape
    tr = _tile(t, 256, 16)

    def body(r3_ref, g3_ref, b3_ref, gp_ref, pp_ref, g_ref, b_ref, tg_ref, dr_ref, dgp_ref, dpp_ref, dg_ref, db_ref, sq_ref):
        i = pl.program_id(0)
        gate = _sigmoid(gp_ref[...])
        pp_ = pp_ref[...]
        r = ALPHA * (_ln_stats(r3_ref[...]) * g3_ref[...] + b3_ref[...]) + gate * pp_
        mu = jnp.mean(r, axis=-1, keepdims=True)
        xc = r - mu
        rstd = lax.rsqrt(jnp.mean(xc * xc, axis=-1, keepdims=True) + LN_EPS)
        xh = xc * rstd
        err = xh * g_ref[...] + b_ref[...] - tg_ref[...]
        dx = err * (1.0 / d)
        dxh = dx * g_ref[...]
        dr = rstd * (dxh - jnp.mean(dxh, axis=-1, keepdims=True) - xh * jnp.mean(dxh * xh, axis=-1, keepdims=True))
        dr_ref[...] = dr
        dgp_ref[...] = (dr * pp_ * gate * (1.0 - gate)).astype(BF16)
        dpp_ref[...] = (dr * gate).astype(BF16)
        dg = jnp.sum(dx * xh, axis=0, keepdims=True)
        db = jnp.sum(dx, axis=0, keepdims=True)
        sq = jnp.sum(err * err, axis=0, keepdims=True)

        @pl.when(i == 0)
        def _():
            dg_ref[...] = dg
            db_ref[...] = db
            sq_ref[...] = sq

        @pl.when(i > 0)
        def _():
            dg_ref[...] += dg
            db_ref[...] += db
            sq_ref[...] += sq

    row = pl.BlockSpec((tr, d), lambda i: (i, 0))
    vec = pl.BlockSpec((1, d), lambda i: (0, 0))
    return pl.pallas_call(
        body, name=name, grid=(t // tr,), in_specs=[row, vec, vec, row, row, vec, vec, row],
        out_specs=[row, row, row, vec, vec, vec],
        out_shape=[SDS((t, d), F32), SDS((t, d), BF16), SDS((t, d), BF16), SDS((1, d), F32), SDS((1, d), F32),
                   SDS((1, d), F32)],
        compiler_params=_cparams("arbitrary"),
    )(r3, g3, b3, gp, pp, g, b, target)


def _to_bf16(x, name):
    t, d = x.shape
    tr = _tile(t, 512, 16)
    row = pl.BlockSpec((tr, d), lambda i: (i, 0))

    def body(x_ref, o_ref):
        o_ref[...] = x_ref[...].astype(BF16)

    return pl.pallas_call(body, name=name, grid=(t // tr,), in_specs=[row], out_specs=row, out_shape=SDS((t, d), BF16),
                          compiler_params=_cparams("parallel"))(x)


def _concat_cols(parts, name):
    t = parts[0].shape[0]
    widths = [p_.shape[1] for p_ in parts]
    tr = _tile(t, 256, 16)

    def body(*refs):
        o_ref = refs[-1]
        at = 0
        for ref, wd in zip(refs[:-1], widths):
            o_ref[:, at:at + wd] = ref[...]
            at += wd

    return pl.pallas_call(
        body, name=name, grid=(t // tr,), in_specs=[pl.BlockSpec((tr, wd), lambda i: (i, 0)) for wd in widths],
        out_specs=pl.BlockSpec((tr, sum(widths)), lambda i: (i, 0)), out_shape=SDS((t, sum(widths)), parts[0].dtype),
        compiler_params=_cparams("parallel"),
    )(*parts)


def _merge_fwd(z, ma, mb, w, name):
    t = z.shape[0]
    tr = _tile(t, 256, 16)

    def body(gc_ref, gh_ref, ma_ref, mb_ref, o_ref):
        o_ref[...] = (_sigmoid(gc_ref[...]) * ma_ref[...] + _sigmoid(gh_ref[...]) * mb_ref[...]).astype(BF16)

    half = pl.BlockSpec((tr, w), lambda i, j: (i, j))
    return pl.pallas_call(
        body, name=name, grid=(t // tr, 2),
        in_specs=[pl.BlockSpec((tr, w), lambda i, j: (i, 7 + j)), pl.BlockSpec((tr, w), lambda i, j: (i, 9 + j)), half, half],
        out_specs=half, out_shape=SDS((t, 2 * w), BF16), compiler_params=_cparams("parallel", "parallel"),
    )(z, z, ma, mb)


def _merge_bwd(dmer, z, ma, mb, w, name):
    t = z.shape[0]
    tr = _tile(t, 256, 16)

    def body(d_ref, gc_ref, gh_ref, ma_ref, mb_ref, dma_ref, dmb_ref, dgc_ref, dgh_ref):
        dm = d_ref[...]
        sc = _sigmoid(gc_ref[...])
        sh = _sigmoid(gh_ref[...])
        dma_ref[...] = (dm * sc).astype(BF16)
        dmb_ref[...] = (dm * sh).astype(BF16)
        dgc_ref[...] = (dm * ma_ref[...] * sc * (1.0 - sc)).astype(BF16)
        dgh_ref[...] = (dm * mb_ref[...] * sh * (1.0 - sh)).astype(BF16)

    half = pl.BlockSpec((tr, w), lambda i, j: (i, j))
    return pl.pallas_call(
        body, name=name, grid=(t // tr, 2),
        in_specs=[half, pl.BlockSpec((tr, w), lambda i, j: (i, 7 + j)), pl.BlockSpec((tr, w), lambda i, j: (i, 9 + j)), half, half],
        out_specs=[half] * 4, out_shape=[SDS((t, 2 * w), BF16)] * 4, compiler_params=_cparams("parallel", "parallel"),
    )(dmer, z, z, ma, mb)


def _shift_down(x, s, row):
    return jnp.where(row >= s, pltpu.roll(x, s, axis=0), 0.0)


def _shift_up(x, s, row, t):
    return jnp.where(row < t - s, pltpu.roll(x, t - s, axis=0), 0.0)


def _conv_fwd(z, cw, w, name):
    t = z.shape[0]
    tc = LANES
    nb = w // tc

    def body(b_ref, c_ref, h_ref, w_ref, o_ref):
        u = c_ref[...] * h_ref[...]
        row = lax.broadcasted_iota(jnp.int32, u.shape, 0)
        cw_ = w_ref[...]
        conv = cw_[2:3, :] * u + cw_[1:2, :] * _shift_down(u, 1, row) + cw_[0:1, :] * _shift_down(u, 2, row)
        o_ref[...] = (b_ref[...] * conv).astype(BF16)

    col = lambda off: pl.BlockSpec((t, tc), lambda j: (0, off * nb + j))
    return pl.pallas_call(
        body, name=name, grid=(nb,), in_specs=[col(0), col(1), col(2), pl.BlockSpec((3, tc), lambda j: (0, j))],
        out_specs=pl.BlockSpec((t, tc), lambda j: (0, j)), out_shape=SDS((t, w), BF16), compiler_params=_cparams("parallel"),
    )(z, z, z, cw)


def _conv_bwd(dy, z, cw, w, name):
    t = z.shape[0]
    tc = LANES
    nb = w // tc

    def body(dy_ref, b_ref, c_ref, h_ref, w_ref, db_ref, dc_ref, dh_ref, dw_ref):
        c_, h_ = c_ref[...], h_ref[...]
        u = c_ * h_
        row = lax.broadcasted_iota(jnp.int32, u.shape, 0)
        cw_ = w_ref[...]
        u1 = _shift_down(u, 1, row)
        u2 = _shift_down(u, 2, row)
        dy_ = dy_ref[...]
        db_ref[...] = (dy_ * (cw_[2:3, :] * u + cw_[1:2, :] * u1 + cw_[0:1, :] * u2)).astype(BF16)
        dconv = dy_ * b_ref[...]
        du = cw_[2:3, :] * dconv + cw_[1:2, :] * _shift_up(dconv, 1, row, t) + cw_[0:1, :] * _shift_up(dconv, 2, row, t)
        dc_ref[...] = (du * h_).astype(BF16)
        dh_ref[...] = (du * c_).astype(BF16)
        dw_ref[0:1, :] = jnp.sum(dconv * u2, axis=0, keepdims=True)
        dw_ref[1:2, :] = jnp.sum(dconv * u1, axis=0, keepdims=True)
        dw_ref[2:3, :] = jnp.sum(dconv * u, axis=0, keepdims=True)

    col = lambda off: pl.BlockSpec((t, tc), lambda j: (0, off * nb + j))
    own = pl.BlockSpec((t, tc), lambda j: (0, j))
    wsp = pl.BlockSpec((3, tc), lambda j: (0, j))
    return pl.pallas_call(
        body, name=name, grid=(nb,), in_specs=[own, col(0), col(1), col(2), wsp], out_specs=[own, own, own, wsp],
        out_shape=[SDS((t, w), BF16)] * 3 + [SDS((3, w), F32)], compiler_params=_cparams("parallel"),
    )(dy, z, z, z, cw)


def _lower_bound(hg):
    mx = jnp.max(hg, axis=0, keepdims=True)
    e = jnp.exp(hg - mx)
    inv = 1.0 / jnp.sum(e, axis=0, keepdims=True)
    return e[0:1, :] * inv, e[1:2, :] * inv


def _chunk_cumsum(x, row):
    s = 1
    while s < CHUNK:
        x = x + jnp.where(row % CHUNK >= s, pltpu.roll(x, s, axis=0), 0.0)
        s *= 2
    return x


def _dot_nt(a, b):
    return lax.dot_general(a.astype(BF16), b.astype(BF16), (((1,), (1,)), ((), ())), preferred_element_type=F32)


def _dot_tn(a, b):
    return lax.dot_general(a.astype(BF16), b.astype(BF16), (((0,), (0,)), ((), ())), preferred_element_type=F32)


def _dot_nn(a, b):
    return jnp.dot(a.astype(BF16), b.astype(BF16), preferred_element_type=F32)


def _tril(x):
    r = lax.broadcasted_iota(jnp.int32, x.shape, 0)
    c = lax.broadcasted_iota(jnp.int32, x.shape, 1)
    return jnp.where(r >= c, x, 0.0)


HGRN_GROUP = 4
HGRN_ROWS = 512
HGRN_UNROLL = 2


def _unrolled_loop(n, step, init):
    assert n % HGRN_UNROLL == 0

    def trip(i, carry):
        for u in range(HGRN_UNROLL):
            carry = step(i * HGRN_UNROLL + u, carry)
        return carry

    return lax.fori_loop(0, n // HGRN_UNROLL, trip, init)


def _hgrn_chunk_inputs(q_ref, f_ref, cum_ref, lb, rows, ln):
    qr = q_ref[rows, ln]
    q = qr * _sigmoid(qr)
    f = lb + (1.0 - lb) * _sigmoid(f_ref[rows, ln])
    return q, 1.0 - f, cum_ref[rows, ln]


def _hgrn_fwd(z, hg, nw, w, name, comm=()):
    t = z.shape[0]
    nh = w // HEAD
    gh = _tile(nh, HGRN_GROUP, 1)
    gw = gh * HEAD
    ngrp = nh // gh
    tb = _tile(t, HGRN_ROWS, CHUNK)
    ncb = tb // CHUNK

    def body(q_ref, f_ref, i_ref, g_ref, hg_ref, nw_ref, y_ref, o_ref, st_ref, cum_ref, *s_refs):
        lb_all, _ = _lower_bound(hg_ref[...])
        row = lax.broadcasted_iota(jnp.int32, (tb, gw), 0)
        cum_ref[...] = _chunk_cumsum(jnp.log(lb_all + (1.0 - lb_all) * _sigmoid(f_ref[...])), row)

        @pl.when(pl.program_id(1) == 0)
        def _():
            for s_ref in s_refs:
                s_ref[...] = jnp.zeros_like(s_ref)

        def step(c, carry):
            rows = pl.ds(pl.multiple_of(c * CHUNK, CHUNK), CHUNK)
            for g in range(gh):
                ln = slice(g * HEAD, (g + 1) * HEAD)
                lb = lb_all[:, ln]
                q, k, cum = _hgrn_chunk_inputs(q_ref, f_ref, cum_ref, lb, rows, ln)
                v = i_ref[rows, ln]
                last = cum[CHUNK - 1:CHUNK, :]
                mid = cum[CHUNK // 2 - 1:CHUNK // 2, :]
                st = s_refs[g][...]
                st_ref[g, c] = st.astype(BF16)
                scores = _tril(_dot_nt(q * jnp.exp(cum - mid), k * jnp.exp(mid - cum)))
                o_ref[rows, ln] = _dot_nt(q * jnp.exp(cum), st) + _dot_nn(scores, v)
                s_refs[g][...] = st * jnp.exp(last) + _dot_tn(v, k * jnp.exp(last - cum))
            return carry

        _unrolled_loop(ncb, step, 0)
        for g in range(gh):
            ln = slice(g * HEAD, (g + 1) * HEAD)
            o = o_ref[:, ln]
            n = o * lax.rsqrt(jnp.mean(o * o, axis=-1, keepdims=True) + RMS_EPS)
            gr = g_ref[:, ln]
            y_ref[:, ln] = (n * nw_ref[...] * gr * _sigmoid(gr)).astype(BF16)

    col = lambda off: pl.BlockSpec((tb, gw), lambda h, j: (j, off * ngrp + h))
    own = pl.BlockSpec((tb, gw), lambda h, j: (j, h))
    return _hosted_call(
        body, comm, name=name, grid=(ngrp, t // tb),
        in_specs=[col(3), col(4), col(5), col(6), pl.BlockSpec((2, gw), lambda h, j: (0, h)),
                  pl.BlockSpec((1, HEAD), lambda h, j: (0, 0))],
        out_specs=[own, own, pl.BlockSpec((gh, ncb, HEAD, HEAD), lambda h, j: (h, j, 0, 0))],
        out_shape=[SDS((t, w), BF16), SDS((t, w), F32), SDS((nh, t // CHUNK, HEAD, HEAD), BF16)],
        scratch_shapes=[pltpu.VMEM((tb, gw), F32)] + [pltpu.VMEM((HEAD, HEAD), F32)] * gh,
        operands=(z, z, z, z, hg, nw), parallel=1)


def _hgrn_bwd(dy, z, o, states, hg, nw, w, name, comm=()):
    t = z.shape[0]
    nh = w // HEAD
    gh = _tile(nh, HGRN_GROUP, 1)
    gw = gh * HEAD
    ngrp = nh // gh
    tb = _tile(t, HGRN_ROWS, CHUNK)
    ncb = tb // CHUNK
    nt = t // tb

    def body(dy_ref, q_ref, f_ref, i_ref, g_ref, o_ref, st_ref, hg_ref, nw_ref,
             dq_ref, df_ref, di_ref, dg_ref, dhg_ref, dnw_ref, cum_ref, do_ref, *ds_refs):
        lb_all, s1_all = _lower_bound(hg_ref[...])
        row = lax.broadcasted_iota(jnp.int32, (tb, gw), 0)
        crow = lax.broadcasted_iota(jnp.int32, (CHUNK, HEAD), 0)
        cum_ref[...] = _chunk_cumsum(jnp.log(lb_all + (1.0 - lb_all) * _sigmoid(f_ref[...])), row)

        @pl.when(pl.program_id(1) == 0)
        def _():
            for ds_ref in ds_refs:
                ds_ref[...] = jnp.zeros_like(ds_ref)
            dhg_ref[...] = jnp.zeros_like(dhg_ref)
            dnw_ref[...] = jnp.zeros_like(dnw_ref)

        for g in range(gh):
            ln = slice(g * HEAD, (g + 1) * HEAD)
            o_ = o_ref[:, ln]
            rstd = lax.rsqrt(jnp.mean(o_ * o_, axis=-1, keepdims=True) + RMS_EPS)
            n = o_ * rstd
            gr = g_ref[:, ln]
            sg = _sigmoid(gr)
            dy_ = dy_ref[:, ln]
            dg_ref[:, ln] = (dy_ * n * nw_ref[...] * (sg * (1.0 + gr * (1.0 - sg)))).astype(BF16)
            dsil = dy_ * gr * sg
            dnw_ref[:, ln] += jnp.sum(dsil * n, axis=0, keepdims=True)
            dn = dsil * nw_ref[...]
            do_ref[:, ln] = rstd * (dn - n * jnp.mean(dn * n, axis=-1, keepdims=True))

        def step(cc, dlbs):
            c = ncb - 1 - cc
            rows = pl.ds(pl.multiple_of(c * CHUNK, CHUNK), CHUNK)
            new = []
            for g in range(gh):
                ln = slice(g * HEAD, (g + 1) * HEAD)
                lb = lb_all[:, ln]
                qr = q_ref[rows, ln]
                sq = _sigmoid(qr)
                q = qr * sq
                sf = _sigmoid(f_ref[rows, ln])
                f = lb + (1.0 - lb) * sf
                k = 1.0 - f
                cum = cum_ref[rows, ln]
                v = i_ref[rows, ln]
                do = do_ref[rows, ln]
                last = cum[CHUNK - 1:CHUNK, :]
                mid = cum[CHUNK // 2 - 1:CHUNK // 2, :]
                eg = jnp.exp(cum)
                em = jnp.exp(cum - mid)
                enm = jnp.exp(mid - cum)
                elc = jnp.exp(last - cum)
                qm, km, kl = q * em, k * enm, k * elc
                ds = ds_refs[g][...]
                a = _tril(_dot_nt(qm, km))
                da = _tril(_dot_nt(do, v))
                di_ref[rows, ln] = (_dot_tn(a, do) + _dot_nt(kl, ds)).astype(BF16)
                st = st_ref[g, c]
                dkl = _dot_nn(v, ds)
                dq = _dot_nn(do, st) * eg + _dot_nn(da, km) * em
                dk = _dot_tn(da, qm) * enm + dkl * elc
                el = jnp.exp(last)
                ds_refs[g][...] = ds * el + _dot_tn(do, q * eg)
                dlast = jnp.sum(kl * dkl, axis=0, keepdims=True) + el * jnp.sum(ds * st.astype(F32), axis=0, keepdims=True)
                x = q * dq - k * dk + jnp.where(crow == CHUNK - 1, dlast, 0.0)
                s = 1
                while s < CHUNK:
                    x = x + _shift_up(x, s, crow, CHUNK)
                    s *= 2
                df = x / f - dk
                dq_ref[rows, ln] = (dq * (sq * (1.0 + qr * (1.0 - sq)))).astype(BF16)
                df_ref[rows, ln] = (df * (1.0 - lb) * sf * (1.0 - sf)).astype(BF16)
                new.append(dlbs[g] + jnp.sum(df * (1.0 - sf), axis=0, keepdims=True))
            return tuple(new)

        dlbs = _unrolled_loop(ncb, step, tuple(jnp.zeros((1, HEAD), F32) for _ in range(gh)))
        for g in range(gh):
            ln = slice(g * HEAD, (g + 1) * HEAD)
            dlb = dlbs[g] * lb_all[:, ln] * s1_all[:, ln]
            dhg_ref[0:1, ln] += dlb
            dhg_ref[1:2, ln] -= dlb

    col = lambda off: pl.BlockSpec((tb, gw), lambda h, j: (nt - 1 - j, off * ngrp + h))
    own = pl.BlockSpec((tb, gw), lambda h, j: (nt - 1 - j, h))
    hsp = pl.BlockSpec((2, gw), lambda h, j: (0, h))
    return _hosted_call(
        body, comm, name=name, grid=(ngrp, nt),
        in_specs=[own, col(3), col(4), col(5), col(6), own,
                  pl.BlockSpec((gh, ncb, HEAD, HEAD), lambda h, j: (h, nt - 1 - j, 0, 0)),
                  hsp, pl.BlockSpec((1, HEAD), lambda h, j: (0, 0))],
        out_specs=[own, own, own, own, hsp, pl.BlockSpec((1, gw), lambda h, j: (0, h))],
        out_shape=[SDS((t, w), BF16)] * 4 + [SDS((2, w), F32), SDS((1, w), F32)],
        scratch_shapes=[pltpu.VMEM((tb, gw), F32)] * 2 + [pltpu.VMEM((HEAD, HEAD), F32)] * gh,
        operands=(dy, z, z, z, z, o, states, hg, nw), parallel=1)


def _cast_pad(wt, n_pad, meta, sp, name, comm=()):
    _, r, n = wt.shape
    g, p, per = meta
    tr = _tile(r, max(16, (3 << 19) // n_pad // 16 * 16), 16)

    def body(w_ref, o_ref):
        if n_pad != n:
            o_ref[...] = jnp.zeros(o_ref.shape, o_ref.dtype)
        o_ref[:, 0:n] = w_ref[...].astype(BF16)

    return _hosted_call(
        body, comm, name=name, grid=(r // tr,), in_specs=[pl.BlockSpec((None, tr, n), lambda i, sp: (0, i, 0))],
        out_specs=[pl.BlockSpec((None, tr, n_pad), lambda i, sp: (sp[1] // per, ((sp[1] % per) * r) // tr + i, 0))],
        out_shape=[SDS((g, p, n_pad), BF16)], scratch_shapes=[], operands=(wt,), parallel=1, prefetch=sp)


def _cast_pad_t(wt_t, n_pad, meta, sp, name, comm=()):
    _, n, r = wt_t.shape
    g, p, per = meta
    tc = _tile(r, 256, LANES)

    def body(w_ref, o_ref):
        for lo in range(0, n_pad, LANES):
            rows = min(LANES, n - lo)
            piece = w_ref[lo:lo + rows, :]
            if rows < LANES:
                piece = jnp.concatenate([piece, jnp.zeros((LANES - rows, tc), F32)], axis=0)
            o_ref[:, lo:lo + LANES] = piece.T.astype(BF16)

    return _hosted_call(
        body, comm, name=name, grid=(r // tc,), in_specs=[pl.BlockSpec((None, n, tc), lambda i, sp: (0, 0, i))],
        out_specs=[pl.BlockSpec((None, tc, n_pad), lambda i, sp: (sp[1] // per, ((sp[1] % per) * r) // tc + i, 0))],
        out_shape=[SDS((g, p, n_pad), BF16)], scratch_shapes=[], operands=(wt_t,), parallel=1, prefetch=sp)


def _adam_math(w, g, m, v):
    m2 = ADAM_B1 * m + (1.0 - ADAM_B1) * g
    v2 = ADAM_B2 * v + (1.0 - ADAM_B2) * (g * g)
    c1 = 1.0 / (1.0 - ADAM_B1 ** ADAM_STEP)
    c2 = 1.0 / (1.0 - ADAM_B2 ** ADAM_STEP)
    return -ADAM_LR * ((m2 * c1) / (jnp.sqrt(v2 * c2) + ADAM_EPS) + ADAM_WD * w), m2, v2


def _adamw_t(wt_t, g, m_t, v_t, name):
    _, n, r = wt_t.shape
    ng = g.shape[1]
    tc = LANES

    def body(w_ref, g_ref, m_ref, v_ref, go_ref, d_ref, mo_ref, vo_ref, gt_ref):
        for lo in range(0, ng, LANES):
            gt_ref[lo:lo + LANES, :] = g_ref[:, lo:lo + LANES].T
        g_ = gt_ref[0:n, :]
        delta, m2, v2 = _adam_math(w_ref[...], g_, m_ref[...], v_ref[...])
        go_ref[...] = g_
        d_ref[...] = delta
        mo_ref[...] = m2
        vo_ref[...] = v2

    blk = pl.BlockSpec((None, n, tc), lambda i: (0, 0, i))
    return pl.pallas_call(
        body, name=name, grid=(r // tc,), in_specs=[blk, pl.BlockSpec((tc, ng), lambda i: (i, 0)), blk, blk],
        out_specs=[blk] * 4, out_shape=[SDS(wt_t.shape, F32)] * 4, scratch_shapes=[pltpu.VMEM((ng, tc), F32)],
        compiler_params=_cparams("parallel"),
    )(wt_t, g, m_t, v_t)


def _adamw(wt, g, m, v, name):
    lead = (None,) * (wt.ndim - 2)
    zero = (0,) * (wt.ndim - 2)
    r, n = wt.shape[-2:]
    ng = g.shape[1]
    nct = 2 if ng == n and n % (2 * LANES) == 0 else 1
    tc, tg = n // nct, ng // nct
    tr = _tile(r, max(8, (3 << 17) // tg // 8 * 8), 8)

    def body(w_ref, g_ref, m_ref, v_ref, go_ref, d_ref, mo_ref, vo_ref):
        g_ = g_ref[:, 0:tc]
        delta, m2, v2 = _adam_math(w_ref[...], g_, m_ref[...], v_ref[...])
        go_ref[...] = g_
        d_ref[...] = delta
        mo_ref[...] = m2
        vo_ref[...] = v2

    blk = pl.BlockSpec(lead + (tr, tc), lambda i, j: zero + (i, j))
    return pl.pallas_call(
        body, name=name, grid=(r // tr, nct), in_specs=[blk, pl.BlockSpec((tr, tg), lambda i, j: (i, j)), blk, blk],
        out_specs=[blk] * 4, out_shape=[SDS(wt.shape, F32)] * 4, compiler_params=_cparams("parallel", "parallel"),
    )(wt, g, m, v)


def _place():
    x, y, c = lax.axis_index("x"), lax.axis_index("y"), lax.axis_index("c")
    return x, y, c, 2 * x + y


def _chip_dev(k, c):
    return (k // 2, k % 2, c)


def _half(ref, j, h, rows, per):
    return ref.at[j // per, pl.ds((j % per) * rows + h * (rows // 2), rows // 2)]


def _gather_stage(bufs, metas, rows_of, ici_parts, fwd_parts, zero_pad):
    nw = len(bufs)
    ici_on = [i for i in range(nw) if ici_parts[i] is not None]
    fwd_on = [i for i in range(nw) if fwd_parts[i] is not None]
    pad_jobs = [(i, gi) for i in ici_on if ici_parts[i][0] == 0 and metas[i][1] > metas[i][2] * rows_of[i]
                for gi in range(metas[i][0])]

    def part_of(ref, i, j, h, part):
        per = metas[i][2]
        p, np_ = part
        pr = rows_of[i] // 2 // np_
        return ref.at[j // per, pl.ds((j % per) * rows_of[i] + h * (rows_of[i] // 2) + p * pr, pr)]

    def descriptors(ins, outs, sems):
        src, zp, dst = ins[:nw], ins[nw], outs
        pads, send, recv, fsend, frecv = sems
        x, y, c, me = _place()

        def pad(n):
            i, gi = pad_jobs[n]
            extra = metas[i][1] - metas[i][2] * rows_of[i]
            return pltpu.make_async_copy(zp.at[pl.ds(0, extra)], dst[i].at[gi, pl.ds(metas[i][2] * rows_of[i], extra)], pads.at[n])

        def ici(i, r, frm):
            return pltpu.make_async_remote_copy(
                src_ref=part_of(src[i], i, me, c, ici_parts[i]), dst_ref=part_of(dst[i], i, frm, c, ici_parts[i]),
                send_sem=send.at[i, r - 1], recv_sem=recv.at[i, r - 1], device_id=_chip_dev((me + r) % N_CHIPS, c),
                device_id_type=MESH)

        def d2d(i, r, frm, h):
            blk = part_of(dst[i], i, frm, h, fwd_parts[i])
            return pltpu.make_async_remote_copy(src_ref=blk, dst_ref=blk, send_sem=fsend.at[i, r - 1],
                                                recv_sem=frecv.at[i, r - 1], device_id=(x, y, 1 - c), device_id_type=MESH)

        return pad, ici, d2d, c, me

    def start(ins, outs, sems):
        pad, ici, d2d, c, me = descriptors(ins, outs, sems)
        for n in range(len(pad_jobs)):
            pad(n).start()
        for i in fwd_on:
            for r in range(1, N_CHIPS):
                d2d(i, r, (me - r) % N_CHIPS, c).start()
        for i in ici_on:
            for r in range(1, N_CHIPS):
                ici(i, r, me).start()

    def finish(ins, outs, sems):
        pad, ici, d2d, c, me = descriptors(ins, outs, sems)
        for i in fwd_on:
            for r in range(1, N_CHIPS):
                d2d(i, r, (me - r) % N_CHIPS, 1 - c).wait_recv()
                d2d(i, r, (me - r) % N_CHIPS, c).wait_send()
        for i in ici_on:
            for r in range(1, N_CHIPS):
                ici(i, r, (me - r) % N_CHIPS).wait_recv()
                ici(i, r, me).wait_send()
        for n in range(len(pad_jobs)):
            pad(n).wait()

    return _Stage(ins=list(bufs) + [zero_pad], out_shapes=[SDS(b.shape, b.dtype) for b in bufs],
                  aliases={i: i for i in range(nw)},
                  sems=[pltpu.SemaphoreType.DMA((max(len(pad_jobs), 1),))] + [pltpu.SemaphoreType.DMA((nw, N_CHIPS - 1))] * 4,
                  start=start, finish=finish)


def _gather_small(packed, name):
    r, n = packed.shape

    def body(src, dst, send, recv):
        x, y, c, me = _place()
        dst[me] = src[...]
        cps = []
        for d in range(1, N_CHIPS):
            cp = pltpu.make_async_remote_copy(src_ref=src, dst_ref=dst.at[me], send_sem=send.at[d - 1], recv_sem=recv.at[d - 1],
                                              device_id=_chip_dev((me + d) % N_CHIPS, c), device_id_type=MESH)
            cp.start()
            cps.append(cp)
        for d in range(1, N_CHIPS):
            pltpu.make_async_remote_copy(src_ref=src, dst_ref=dst.at[(me - d) % N_CHIPS], send_sem=send.at[d - 1],
                                         recv_sem=recv.at[d - 1], device_id=_chip_dev((me + d) % N_CHIPS, c),
                                         device_id_type=MESH).wait_recv()
        for cp in cps:
            cp.wait_send()

    return pl.pallas_call(
        body, name=name, in_specs=[VMEM_SPEC], out_specs=VMEM_SPEC, out_shape=SDS((N_CHIPS, r, n), F32),
        scratch_shapes=[pltpu.SemaphoreType.DMA((N_CHIPS - 1,))] * 2,
    )(packed)


def _all_reduce_small(packed, name):
    r, n = packed.shape

    def body(src, out, slots, send, recv):
        x, y, c, me = _place()
        idx = 2 * me + c
        slots[idx] = src[...]
        cps = []

        def peer(d):
            p = (idx + d) % N_DEV
            return (p // 4, (p // 2) % 2, p % 2)

        for d in range(1, N_DEV):
            cp = pltpu.make_async_remote_copy(src_ref=src, dst_ref=slots.at[idx], send_sem=send.at[d - 1], recv_sem=recv.at[d - 1],
                                              device_id=peer(d), device_id_type=MESH)
            cp.start()
            cps.append(cp)
        for d in range(1, N_DEV):
            pltpu.make_async_remote_copy(src_ref=src, dst_ref=slots.at[(idx - d) % N_DEV], send_sem=send.at[d - 1],
                                         recv_sem=recv.at[d - 1], device_id=peer(d), device_id_type=MESH).wait_recv()
        for cp in cps:
            cp.wait_send()
        acc = slots[0]
        for k in range(1, N_DEV):
            acc = acc + slots[k]
        out[...] = acc

    return pl.pallas_call(
        body, name=name, in_specs=[VMEM_SPEC], out_specs=VMEM_SPEC, out_shape=SDS((r, n), F32),
        scratch_shapes=[pltpu.VMEM((N_DEV, r, n), F32)] + [pltpu.SemaphoreType.DMA((N_DEV - 1,))] * 2,
    )(packed)


def _simple_stage(ins, out_shapes, aliases, n_copies, copies):
    def start(ins_, outs, sems):
        for cp in copies(ins_, outs, *sems):
            cp.start()

    def finish(ins_, outs, sems):
        for cp in copies(ins_, outs, *sems):
            cp.wait()

    return _Stage(ins=list(ins), out_shapes=list(out_shapes), aliases=aliases,
                  sems=[pltpu.SemaphoreType.DMA((n_copies,))] * 2, start=start, finish=finish)


def _rs_pair_exchange(grads, metas, rows_of):
    nw = len(grads)

    def copies(src, dst, send, recv):
        x, y, c, me = _place()
        return [pltpu.make_async_remote_copy(
            src_ref=_half(src[i], j, 1 - c, rows_of[i], metas[i][2]), dst_ref=dst[i].at[j], send_sem=send.at[i * N_CHIPS + j],
            recv_sem=recv.at[i * N_CHIPS + j], device_id=(x, y, 1 - c), device_id_type=MESH)
            for i in range(nw) for j in range(N_CHIPS)]

    out_shapes = [SDS((N_CHIPS, rows_of[i] // 2, g.shape[2]), g.dtype) for i, g in enumerate(grads)]
    return _simple_stage(grads, out_shapes, {}, nw * N_CHIPS, copies)


def _rs_pair_add(g, got, meta, rows, sp, name):
    per = meta[2]
    n = g.shape[2]
    hr = rows // 2
    tr = _tile(hr, max(16, (3 << 19) // n // 16 * 16), 16)

    def body(sp_ref, g_ref, got_ref, snd_ref, own_ref):
        j = pl.program_id(1)
        s = g_ref[...].astype(F32) + got_ref[...].astype(F32)
        snd_ref[...] = s.astype(BF16)

        @pl.when(j == sp_ref[1])
        def _():
            own_ref[...] = s

    grid_spec = pltpu.PrefetchScalarGridSpec(
        num_scalar_prefetch=1, grid=(hr // tr, N_CHIPS),
        in_specs=[pl.BlockSpec((None, tr, n), lambda i, j, sp: (j // per, ((j % per) * rows + sp[0] * hr) // tr + i, 0)),
                  pl.BlockSpec((None, tr, n), lambda i, j, sp: (j, i, 0))],
        out_specs=[pl.BlockSpec((None, tr, n), lambda i, j, sp: (j, i, 0)), pl.BlockSpec((tr, n), lambda i, j, sp: (i, 0))])
    return pl.pallas_call(
        body, name=name, grid_spec=grid_spec, out_shape=[SDS((N_CHIPS, hr, n), BF16), SDS((hr, n), F32)],
        compiler_params=_cparams("parallel", "arbitrary"),
    )(sp, g, got)


def _rs_chip_exchange(sends, part=(0, 1), prev=None):
    nw = len(sends)
    p, np_ = part

    def copies(src, dst, send, recv):
        x, y, c, me = _place()
        cps = []
        for i in range(nw):
            pr = sends[i].shape[1] // np_
            for r in range(1, N_CHIPS):
                cps.append(pltpu.make_async_remote_copy(
                    src_ref=src[i].at[(me + r) % N_CHIPS, pl.ds(p * pr, pr)], dst_ref=dst[i].at[r - 1, pl.ds(p * pr, pr)],
                    send_sem=send.at[i * (N_CHIPS - 1) + r - 1], recv_sem=recv.at[i * (N_CHIPS - 1) + r - 1],
                    device_id=_chip_dev((me + r) % N_CHIPS, c), device_id_type=MESH))
        return cps

    out_shapes = [SDS((N_CHIPS - 1,) + s.shape[1:], BF16) for s in sends]
    if prev is None:
        return _simple_stage(sends, out_shapes, {}, nw * (N_CHIPS - 1), copies)
    return _simple_stage(list(sends) + list(prev), out_shapes, {nw + i: i for i in range(nw)}, nw * (N_CHIPS - 1), copies)


def _rs_chip_add(own, got, sp, name):
    hr, n = own.shape
    tr = _tile(hr, max(16, (3 << 19) // n // 16 * 16), 16)

    def body(sp_ref, own_ref, got_ref, o_ref):
        acc = own_ref[...]
        for r in range(N_CHIPS - 1):
            acc = acc + got_ref[r].astype(F32)
        o_ref[...] = acc

    grid_spec = pltpu.PrefetchScalarGridSpec(
        num_scalar_prefetch=1, grid=(hr // tr,),
        in_specs=[pl.BlockSpec((tr, n), lambda i, sp: (i, 0)), pl.BlockSpec((N_CHIPS - 1, tr, n), lambda i, sp: (0, i, 0))],
        out_specs=pl.BlockSpec((tr, n), lambda i, sp: (sp[0] * (hr // tr) + i, 0)))
    return pl.pallas_call(body, name=name, grid_spec=grid_spec, out_shape=SDS((2 * hr, n), F32),
                          compiler_params=_cparams("parallel"))(sp, own, got)


def _rs_pair_share(blocks):
    nw = len(blocks)

    def copies(src, dst, send, recv):
        x, y, c, me = _place()
        cps = []
        for i in range(nw):
            hr = src[i].shape[0] // 2
            cps.append(pltpu.make_async_remote_copy(
                src_ref=src[i].at[pl.ds(c * hr, hr)], dst_ref=dst[i].at[pl.ds(c * hr, hr)], send_sem=send.at[i],
                recv_sem=recv.at[i], device_id=(x, y, 1 - c), device_id_type=MESH))
        return cps

    return _simple_stage(blocks, [SDS(b.shape, b.dtype) for b in blocks], {i: i for i in range(nw)}, nw, copies)


def kernel(x, p, ln_g, ln_b, ffn1_w_in, ffn1_w_out, mix_w_in, conv_w, hg_lower_bound, hg_norm_w, branch_w_conv, branch_w_hgrn, mix_w_out, ffn2_w_in, ffn2_w_out, ple_w_gate, ple_w_proj, loss_target, m_ln_g, m_ln_b, m_ffn1_w_in, m_ffn1_w_out, m_mix_w_in, m_conv_w, m_hg_lower_bound, m_hg_norm_w, m_branch_w_conv, m_branch_w_hgrn, m_mix_w_out, m_ffn2_w_in, m_ffn2_w_out, m_ple_w_gate, m_ple_w_proj, v_ln_g, v_ln_b, v_ffn1_w_in, v_ffn1_w_out, v_mix_w_in, v_conv_w, v_hg_lower_bound, v_hg_norm_w, v_branch_w_conv, v_branch_w_hgrn, v_mix_w_out, v_ffn2_w_in, v_ffn2_w_out, v_ple_w_gate, v_ple_w_proj):
    assert ln_g.shape[0] == DEPTH and x.shape[0] == 1 and p.shape[:2] == (1, 1)
    t, d = x.shape[1], x.shape[2]
    w = d // 2
    x0 = x.reshape(t, d)
    x0b = _to_bf16(x0, "x_bf16")
    pe = p.reshape(t, p.shape[-1])
    target = loss_target.reshape(t, d)
    cx, cy, cc = lax.axis_index("x"), lax.axis_index("y"), lax.axis_index("c")
    chip = 2 * cx + cy
    sp = jnp.stack([cc, chip]).astype(jnp.int32)

    big = dict(ffn1_w_in=ffn1_w_in, ffn1_w_out=ffn1_w_out, mix_w_in=mix_w_in, branch_w_conv=branch_w_conv,
               branch_w_hgrn=branch_w_hgrn, mix_w_out=mix_w_out, ffn2_w_in=ffn2_w_in, ffn2_w_out=ffn2_w_out,
               ple_w_gate=ple_w_gate, ple_w_proj=ple_w_proj)
    moments = dict(ffn1_w_in=(m_ffn1_w_in, v_ffn1_w_in), ffn1_w_out=(m_ffn1_w_out, v_ffn1_w_out), mix_w_in=(m_mix_w_in, v_mix_w_in),
                   branch_w_conv=(m_branch_w_conv, v_branch_w_conv), branch_w_hgrn=(m_branch_w_hgrn, v_branch_w_hgrn),
                   mix_w_out=(m_mix_w_out, v_mix_w_out), ffn2_w_in=(m_ffn2_w_in, v_ffn2_w_in), ffn2_w_out=(m_ffn2_w_out, v_ffn2_w_out),
                   ple_w_gate=(m_ple_w_gate, v_ple_w_gate), ple_w_proj=(m_ple_w_proj, v_ple_w_proj))
    names = list(big)

    n_loc = ffn1_w_in.shape[-1]
    n_pad = -(-n_loc // LANES) * LANES
    assert mix_w_in.shape[-1] % LANES == 0 and ffn1_w_out.shape[1] * 2 == n_loc
    pad_cols = dict(ffn1_w_in=n_pad, ffn2_w_in=n_pad)
    meta = {k: (N_CHIPS, big[k].shape[1], 1) for k in names}
    meta["ffn1_w_out"] = meta["ffn2_w_out"] = (2, n_pad, 2)
    rows = {k: big[k].shape[1] for k in names}
    swap = lambda a: jnp.transpose(a, (0, 2, 1))
    wbuf = {}
    zero_pad = jnp.zeros((max(n_pad - n_loc, 16), d), BF16)

    def cast(k, comm=()):
        if k in pad_cols:
            return _cast_pad_t(swap(big[k]), pad_cols[k], meta[k], sp, "cast_" + k, comm=comm)
        return _cast_pad(big[k], big[k].shape[2], meta[k], sp, "cast_" + k, comm=comm)

    def gather(ici=(), fwd=()):
        ks = list(dict.fromkeys([k for k, _, _ in ici] + [k for k, _, _ in fwd]))
        ip = {k: (p_, n_) for k, p_, n_ in ici}
        fp = {k: (p_, n_) for k, p_, n_ in fwd}
        return _gather_stage([wbuf[k] for k in ks], [meta[k] for k in ks], [rows[k] for k in ks], [ip.get(k) for k in ks],
                             [fp.get(k) for k in ks], zero_pad), ks

    def gathered(ks, outs):
        wbuf.update(zip(ks, outs))

    def w3(k):
        return wbuf[k]

    def w2(k):
        return wbuf[k].reshape(-1, wbuf[k].shape[2])

    dq, wq = d // N_CHIPS, w // N_CHIPS
    small = jnp.concatenate([ln_g[0], ln_b[0], jnp.pad(conv_w[0], ((0, 5), (0, dq - wq)))], axis=0)
    small = _gather_small(small, "gather_small")
    lng = small[:, 0:4, :].transpose(1, 0, 2).reshape(4, 1, d)
    lnb = small[:, 4:8, :].transpose(1, 0, 2).reshape(4, 1, d)
    cw = small[:, 8:11, :wq].transpose(1, 0, 2).reshape(3, w)
    hg = hg_lower_bound
    nw_ = hg_norm_w

    one = lambda *ks_: [(k, 0, 1) for k in ks_]
    wbuf["ffn1_w_in"] = cast("ffn1_w_in")
    first = ["ple_w_proj", "ffn1_w_out", "mix_w_in", None, "ffn2_w_in", "ffn2_w_out"]
    carriers = first + [k for k in names if k != "ffn1_w_in" and k not in first]
    assert len(carriers) > FIRST_GATHER_PARTS
    for step, k in enumerate(carriers):
        ici = [("ffn1_w_in", step, FIRST_GATHER_PARTS)] if step < FIRST_GATHER_PARTS else []
        fwd = [("ffn1_w_in", step - 1, FIRST_GATHER_PARTS)] if 1 <= step <= FIRST_GATHER_PARTS else []
        ici += one("ple_w_proj") if step == 1 else []
        fwd += one("ple_w_proj") if step == 2 else []
        if ici or fwd:
            st, ks = gather(ici, fwd)
            if k is None:
                pp, got = _mm(pe, w3("ple_w_proj"), name="ple_proj", b_blocked=True, tn=512, comm=[st])
            else:
                wbuf[k], got = cast(k, comm=[st])
            gathered(ks, got)
        else:
            wbuf[k] = cast(k)
    st, ks = gather(ici=one("ffn1_w_out") + [("mix_w_in", 0, 2)])
    z1, got = _mm(x0b, w3("ffn1_w_in"), name="ffn1_in", b_blocked=True, out_dtype=BF16, tm=1024, comm=[st])
    gathered(ks, got)
    st, ks = gather(ici=[("mix_w_in", 2, 4)], fwd=one("ffn1_w_out") + [("mix_w_in", 0, 2)])
    h1, got = _swiglu_fwd(z1, "ffn1_act", comm=[st])
    gathered(ks, got)
    st, ks = gather(ici=[("mix_w_in", 3, 4)], fwd=[("mix_w_in", 2, 4)])
    y1, got = _mm(h1, w2("ffn1_w_out"), name="ffn1_out", tm=1024, tn=1024, tk=2816, comm=[st])
    gathered(ks, got)
    st, ks = gather(fwd=[("mix_w_in", 3, 4)])
    (r1, x1b), got = _ln_fwd(x0, y1, lng[0], lnb[0], None, None, 0.5, "ln0", comm=[st])
    gathered(ks, got)
    mixo_w = one("branch_w_conv", "branch_w_hgrn", "mix_w_out")
    st, ks = gather(ici=mixo_w + [("ffn2_w_in", 0, 2)])
    z, got = _mm(x1b, w3("mix_w_in"), name="mix_in", b_blocked=True, tm=1024, comm=[st])
    gathered(ks, got)
    ya = _conv_fwd(z, cw, w, "conv_fwd")
    st, ks = gather(ici=[("ffn2_w_in", 1, 2)], fwd=mixo_w + [("ffn2_w_in", 0, 2)])
    (yb, o_h, states), got = _hgrn_fwd(z, hg, nw_, w, "hgrn_fwd", comm=[st])
    gathered(ks, got)
    ma = _mm(ya, w3("branch_w_conv"), name="branch_conv", b_blocked=True, tn=512)
    mb = _mm(yb, w3("branch_w_hgrn"), name="branch_hgrn", b_blocked=True, tn=512)
    merged = _merge_fwd(z, ma, mb, w, "merge_fwd")
    st, ks = gather(fwd=[("ffn2_w_in", 1, 2)])
    y2, got = _mm(merged, w2("mix_w_out"), name="mix_out", tn=1024, comm=[st])
    gathered(ks, got)
    r2, x2b = _ln_fwd(r1, y2, lng[1], lnb[1], lng[0], lnb[0], 1.0, "ln1")
    late = one("ffn2_w_out", "ple_w_gate")
    st, ks = gather(ici=late)
    z3, got = _mm(x2b, w3("ffn2_w_in"), name="ffn2_in", b_blocked=True, out_dtype=BF16, tm=1024, comm=[st])
    gathered(ks, got)
    st, ks = gather(fwd=late)
    h3, got = _swiglu_fwd(z3, "ffn2_act", comm=[st])
    gathered(ks, got)
    y3 = _mm(h3, w2("ffn2_w_out"), name="ffn2_out", tm=1024, tn=1024, tk=2816)
    r3, x3b = _ln_fwd(r2, y3, lng[2], lnb[2], lng[1], lnb[1], 0.5, "ln2")
    gp = _mm(x3b, w2("ple_w_gate"), name="ple_gate", tn=1024)
    dr4, dgp, dpp, dg3, db3, sq = _tail(r3, lng[2], lnb[2], gp, pp, lng[3], lnb[3], target, "tail")

    grads, sends, owns, blocks, outs = {}, {}, {}, {}, {}

    def pair_exchange(*ks):
        return _rs_pair_exchange([grads[k] for k in ks], [meta[k] for k in ks], [rows[k] for k in ks])

    def pair_add(ks, got):
        for k, g_ in zip(ks, got):
            sends[k], owns[k] = _rs_pair_add(grads[k], g_, meta[k], rows[k], sp, "rs_pair_add_" + k)

    def chip_exchange(*ks):
        return _rs_chip_exchange([sends[k] for k in ks])

    def chip_add(ks, got):
        for k, g_ in zip(ks, got):
            blocks[k] = _rs_chip_add(owns[k], g_, sp, "rs_chip_add_" + k)

    def pair_share(*ks):
        return _rs_pair_share([blocks[k] for k in ks])

    def update(ks, full):
        for k, g_ in zip(ks, full):
            m_, v_ = moments[k]
            if k in pad_cols:
                outs[k] = [swap(a) for a in _adamw_t(swap(big[k]), g_, swap(m_), swap(v_), "adamw_" + k)]
            else:
                outs[k] = _adamw(big[k], g_, m_, v_, "adamw_" + k)

    ple = ("ple_w_gate", "ple_w_proj")
    mixo = ("mix_w_out", "branch_w_conv", "branch_w_hgrn")
    dx3m = _mm(dgp, w2("ple_w_gate"), name="d_ple_gate_x", tb=True, tn=1024, tk=2048)
    grads["ple_w_gate"] = _mm(x3b, dgp, name="d_ple_gate_w", ta=True, out_dtype=BF16, tm=1024, tk=2048, tn=1024).reshape(N_CHIPS, -1, d)
    grads["ple_w_proj"] = _mm(pe, dpp, name="d_ple_proj_w", ta=True, out_dtype=BF16, out_blocked=N_CHIPS, tk=2048, tn=512)
    dr3, dy3b, dg2, db2 = _ln_bwd(dr4, dx3m, r3, lng[2], 0.5, "ln2_bwd")
    late_w = ple + ("ffn2_w_out",)
    dh3 = _mm(dy3b, w2("ffn2_w_out"), name="d_ffn2_out_x", tb=True, out_dtype=BF16, tn=1408, tk=2048)
    grads["ffn2_w_out"] = _mm(h3, dy3b, name="d_ffn2_out_w", ta=True, out_dtype=BF16, tm=1408, tk=2048, tn=1024).reshape(2, n_pad, d)
    dz3 = _swiglu_bwd(dh3, z3, "ffn2_act_bwd")
    dx2m, got = _mm(dz3, w3("ffn2_w_in"), name="d_ffn2_in_x", tb=True, b_blocked=True, tm=1024, tn=1024, tk=2816,
                    comm=[pair_exchange(*late_w)])
    pair_add(late_w, got)
    grads["ffn2_w_in"], got = _mm(x2b, dz3, name="d_ffn2_in_w", ta=True, out_dtype=BF16, out_blocked=N_CHIPS, tk=4096, comm=[chip_exchange(*late_w)])
    chip_add(late_w, got)
    dr2, dy2b, dg1, db1 = _ln_bwd(dr3, dx2m, r2, lng[1], 1.0, "ln1_bwd")
    dmer, got = _mm(dy2b, w2("mix_w_out"), name="d_mix_out_x", tb=True, tn=1024, tk=2048, comm=[pair_exchange("ffn2_w_in")])
    pair_add(["ffn2_w_in"], got)
    g_, full = _mm(merged, dy2b, name="d_mix_out_w", ta=True, out_dtype=BF16, tm=1024, tk=2048, tn=1024, comm=[pair_share(*late_w)])
    grads["mix_w_out"] = g_.reshape(N_CHIPS, -1, d)
    update(late_w, full)
    dma, dmb, dgc, dgh = _merge_bwd(dmer, z, ma, mb, w, "merge_bwd")
    dya = _mm(dma, w3("branch_w_conv"), name="d_branch_conv_x", tb=True, b_blocked=True, tn=1024, tk=512)
    dyb = _mm(dmb, w3("branch_w_hgrn"), name="d_branch_hgrn_x", tb=True, b_blocked=True, tn=1024, tk=512)
    grads["branch_w_conv"] = _mm(ya, dma, name="d_branch_conv_w", ta=True, out_dtype=BF16, out_blocked=N_CHIPS, tm=1024, tk=2048, tn=512)
    grads["branch_w_hgrn"] = _mm(yb, dmb, name="d_branch_hgrn_w", ta=True, out_dtype=BF16, out_blocked=N_CHIPS, tm=1024, tk=2048, tn=512)
    dbg, dcg, dhc, dcw = _conv_bwd(dya, z, cw, w, "conv_bwd")
    (dq_, df_, di_, dgr_, dhg, dnw), got2, got = _hgrn_bwd(dyb, z, o_h, states, hg, nw_, w, "hgrn_bwd",
                                                            comm=[chip_exchange("ffn2_w_in"), pair_exchange(*mixo)])
    chip_add(["ffn2_w_in"], got2)
    pair_add(mixo, got)
    dz = _concat_cols([dbg, dcg, dhc, dq_, df_, di_, dgr_, dgc, dgh], "dz_concat")
    dx1m, full, got = _mm(dz, w3("mix_w_in"), name="d_mix_in_x", tb=True, b_blocked=True, tm=1024, tn=1024, tk=2816,
                          comm=[pair_share("ffn2_w_in"), chip_exchange(*mixo)])
    update(["ffn2_w_in"], full)
    chip_add(mixo, got)
    grads["mix_w_in"], full = _mm(x1b, dz, name="d_mix_in_w", ta=True, out_dtype=BF16, out_blocked=N_CHIPS, tk=4096, comm=[pair_share(*mixo)])
    update(mixo, full)
    dr1, dy1b, dg0, db0 = _ln_bwd(dr2, dx1m, r1, lng[0], 0.5, "ln0_bwd")
    dh1, got = _mm(dy1b, w2("ffn1_w_out"), name="d_ffn1_out_x", tb=True, out_dtype=BF16, tn=1408, tk=2048,
                   comm=[pair_exchange("mix_w_in")])
    pair_add(["mix_w_in"], got)
    mix_sends = [sends["mix_w_in"]]
    g_, got_a = _mm(h1, dy1b, name="d_ffn1_out_w", ta=True, out_dtype=BF16, tm=1408, tk=2048, tn=1024, comm=[_rs_chip_exchange(mix_sends, (0, 2))])
    grads["ffn1_w_out"] = g_.reshape(2, n_pad, d)
    dz1 = _swiglu_bwd(dh1, z1, "ffn1_act_bwd")
    g_other, got2, got = _mm(x0b, dz1, name="d_ffn1_in_w_other", ta=True, out_dtype=BF16, out_blocked=N_CHIPS, tk=4096, half=(sp, True),
                             comm=[_rs_chip_exchange(mix_sends, (1, 2), got_a), pair_exchange("ffn1_w_out")])
    chip_add(["mix_w_in"], got2)
    pair_add(["ffn1_w_out"], got)
    grads["ffn1_w_in"], full, got2, got = _mm(
        x0b, dz1, name="d_ffn1_in_w_own", ta=True, out_dtype=BF16, out_blocked=N_CHIPS, tk=4096, half=(sp, False),
        comm=[pair_share("mix_w_in"), chip_exchange("ffn1_w_out"),
              _rs_pair_exchange([g_other], [meta["ffn1_w_in"]], [rows["ffn1_w_in"]])])
    update(["mix_w_in"], full)
    chip_add(["ffn1_w_out"], got2)
    pair_add(["ffn1_w_in"], got)
    dx0, got2, full = _mm(dz1, w3("ffn1_w_in"), name="d_ffn1_in_x", tb=True, b_blocked=True, tm=1024, tn=1024, tk=2816,
                          add=(dr1, ALPHA), comm=[chip_exchange("ffn1_w_in"), pair_share("ffn1_w_out")])
    chip_add(["ffn1_w_in"], got2)
    update(["ffn1_w_out"], full)
    grad_x = dx0.reshape(x.shape)
    update(["ffn1_w_in"], _run_stages([pair_share("ffn1_w_in")], "rs_tail_pair")[0])

    pack = jnp.concatenate([
        dg0, dg1, dg2, dg3, db0, db1, db2, db3,
        jnp.pad(dcw, ((0, 0), (0, d - w))), jnp.pad(dhg, ((0, 0), (0, d - w))),
        jnp.pad(jnp.sum(dnw.reshape(-1, HEAD), axis=0, keepdims=True), ((0, 0), (0, d - HEAD))), sq], axis=0)
    pack = _all_reduce_small(jnp.pad(pack, ((0, 1), (0, 0))), "reduce_small")
    loss = (0.5 / d) * jnp.sum(pack[14])
    g_ln_g = lax.dynamic_slice_in_dim(pack[0:4], chip * dq, dq, axis=1)
    g_ln_b = lax.dynamic_slice_in_dim(pack[4:8], chip * dq, dq, axis=1)
    g_conv = lax.dynamic_slice_in_dim(pack[8:11, :w], chip * wq, wq, axis=1)
    g_hg = pack[11:13, :w]
    g_nw = pack[13:14, :HEAD]

    small_w = dict(ln_g=(ln_g, g_ln_g, m_ln_g, v_ln_g), ln_b=(ln_b, g_ln_b, m_ln_b, v_ln_b),
                   conv_w=(conv_w, g_conv, m_conv_w, v_conv_w), hg_lower_bound=(hg_lower_bound, g_hg, m_hg_lower_bound, v_hg_lower_bound),
                   hg_norm_w=(hg_norm_w, g_nw, m_hg_norm_w, v_hg_norm_w))
    for k, (w_, g_, m_, v_) in small_w.items():
        outs[k] = _adamw(w_, g_.reshape(-1, w_.shape[-1]), m_, v_, "adamw_" + k)

    order = ["ln_g", "ln_b", "ffn1_w_in", "ffn1_w_out", "mix_w_in", "conv_w", "hg_lower_bound", "hg_norm_w", "branch_w_conv",
             "branch_w_hgrn", "mix_w_out", "ffn2_w_in", "ffn2_w_out", "ple_w_gate", "ple_w_proj"]
    return (loss, grad_x, *[outs[k][0] for k in order], *[outs[k][1] for k in order], *[outs[k][2] for k in order],
            *[outs[k][3] for k in order])
```

```python
import collections
import functools

import jax
import jax.numpy as jnp
from jax import lax
from jax.experimental import pallas as pl
from jax.experimental.pallas import tpu as pltpu

F32 = jnp.float32
BF16 = jnp.bfloat16
MESH = pl.DeviceIdType.MESH
ANY = pl.BlockSpec(memory_space=pl.ANY)
VMEM_SPEC = pl.BlockSpec(memory_space=pltpu.VMEM)
SDS = jax.ShapeDtypeStruct

DEPTH = 1
ALPHA = (2.0 * DEPTH) ** 0.25
LN_EPS = 1e-5
RMS_EPS = 1e-6
CHUNK = 32
HEAD = 128
ADAM_LR, ADAM_B1, ADAM_B2, ADAM_EPS, ADAM_WD, ADAM_STEP = 0.001, 0.9, 0.999, 1e-08, 0.01, 10

LANES = 128
N_CHIPS = 4
N_DEV = 8
FIRST_GATHER_PARTS = 4
VMEM_LIMIT = 52 * 1024 * 1024
MM_PIECE = 512


def _cparams(*sem):
    if sem:
        return pltpu.CompilerParams(dimension_semantics=sem, vmem_limit_bytes=VMEM_LIMIT)
    return pltpu.CompilerParams(vmem_limit_bytes=VMEM_LIMIT)


def _tile(n, target, mult):
    best = None
    for t in range(mult, min(n, target) + 1, mult):
        if n % t == 0:
            best = t
    return best if best is not None else n


def _sigmoid(x):
    return 1.0 / (1.0 + jnp.exp(-x))


_Stage = collections.namedtuple("_Stage", "ins out_shapes aliases sems start finish")


def _hosted_call(compute, stages, *, name, grid, in_specs, out_specs, out_shape, scratch_shapes, operands, parallel,
                 prefetch=None):
    n_cmp, n_out, n_scr = len(in_specs), len(out_specs), len(scratch_shapes)
    n_in = n_cmp
    n_pre = int(prefetch is not None)
    c_in = [len(s.ins) for s in stages]
    c_out = [len(s.out_shapes) for s in stages]
    c_sem = [len(s.sems) for s in stages]
    aliases = {}
    for si, s in enumerate(stages):
        for a_in, a_out in s.aliases.items():
            aliases[n_pre + n_in + sum(c_in[:si]) + a_in] = n_out + sum(c_out[:si]) + a_out

    def body(*refs):
        refs = refs[n_pre:]
        ins = refs[:n_cmp]
        cins = refs[n_in:n_in + sum(c_in)]
        outs = refs[n_in + sum(c_in):n_in + sum(c_in) + n_out]
        couts = refs[n_in + sum(c_in) + n_out:n_in + sum(c_in) + n_out + sum(c_out)]
        scr = refs[n_in + sum(c_in) + n_out + sum(c_out):][:n_scr]
        sems = refs[n_in + sum(c_in) + n_out + sum(c_out) + n_scr:]

        def stage_refs(si):
            return (cins[sum(c_in[:si]):sum(c_in[:si + 1])], couts[sum(c_out[:si]):sum(c_out[:si + 1])],
                    sems[sum(c_sem[:si]):sum(c_sem[:si + 1])])

        if stages:
            first = functools.reduce(jnp.logical_and, [pl.program_id(ax) == 0 for ax in range(len(grid))])
            last = functools.reduce(jnp.logical_and, [pl.program_id(ax) == grid[ax] - 1 for ax in range(len(grid))])

        compute(*ins, *outs, *scr)
        if stages:
            @pl.when(first)
            def _():
                for si, s in enumerate(stages):
                    s.start(*stage_refs(si))

            @pl.when(last)
            def _():
                for si, s in enumerate(stages):
                    s.finish(*stage_refs(si))

    sem = ("arbitrary",) * len(grid) if stages else ("parallel",) * parallel + ("arbitrary",) * (len(grid) - parallel)
    all_in = list(in_specs) + [ANY] * (n_in - n_cmp + sum(c_in))
    all_out = list(out_specs) + [ANY] * sum(c_out)
    all_scr = list(scratch_shapes) + [q for s in stages for q in s.sems]
    all_shape = list(out_shape) + [o for s in stages for o in s.out_shapes]
    args = list(operands) + [a for s in stages for a in s.ins]
    if prefetch is None:
        res = pl.pallas_call(body, name=name, grid=grid, in_specs=all_in, out_specs=all_out, out_shape=all_shape,
                             input_output_aliases=aliases, scratch_shapes=all_scr, compiler_params=_cparams(*sem))(*args)
    else:
        grid_spec = pltpu.PrefetchScalarGridSpec(num_scalar_prefetch=1, grid=grid, in_specs=all_in, out_specs=all_out,
                                                 scratch_shapes=all_scr)
        res = pl.pallas_call(body, name=name, grid_spec=grid_spec, out_shape=all_shape, input_output_aliases=aliases,
                             compiler_params=_cparams(*sem))(prefetch, *args)
    main = res[0] if n_out == 1 else list(res[:n_out])
    if not stages:
        return main
    rest = res[n_out:]
    return (main, *[list(rest[sum(c_out[:si]):sum(c_out[:si + 1])]) for si in range(len(stages))])


def _run_stages(stages, name):
    def body(*refs):
        n_i = sum(len(s.ins) for s in stages)
        n_o = sum(len(s.out_shapes) for s in stages)
        cins, couts, sems = refs[:n_i], refs[n_i:n_i + n_o], refs[n_i + n_o:]
        pos = [0, 0, 0]
        parts = []
        for s in stages:
            parts.append((cins[pos[0]:pos[0] + len(s.ins)], couts[pos[1]:pos[1] + len(s.out_shapes)], sems[pos[2]:pos[2] + len(s.sems)]))
            pos = [pos[0] + len(s.ins), pos[1] + len(s.out_shapes), pos[2] + len(s.sems)]
        for s, p_ in zip(stages, parts):
            s.start(*p_)
        for s, p_ in zip(stages, parts):
            s.finish(*p_)

    aliases, ni, no = {}, 0, 0
    for s in stages:
        for a_in, a_out in s.aliases.items():
            aliases[ni + a_in] = no + a_out
        ni, no = ni + len(s.ins), no + len(s.out_shapes)
    res = pl.pallas_call(
        body, name=name, in_specs=[ANY] * ni, out_specs=[ANY] * no, out_shape=[o for s in stages for o in s.out_shapes],
        input_output_aliases=aliases, scratch_shapes=[q for s in stages for q in s.sems],
    )(*[a for s in stages for a in s.ins])
    out, pos = [], 0
    for s in stages:
        out.append(list(res[pos:pos + len(s.out_shapes)]))
        pos += len(s.out_shapes)
    return out


def _mm(a, b, *, name, ta=False, tb=False, b_blocked=False, out_blocked=0, out_dtype=F32,
        tm=512, tn=1408, tk=2048, comm=(), half=None, add=None):
    if ta:
        kd, m = a.shape
    else:
        m, kd = a.shape
    if b_blocked and not tb:
        g, kb, nb = b.shape
        assert kb == kd
        n = g * nb
        tn = _tile(nb, tn, LANES)
        tk = _tile(kd, tk, LANES)
        per_n = nb // tn
        b_spec = pl.BlockSpec((None, tk, tn), lambda i, j, k, *s: (j // per_n, k, j % per_n))
    elif b_blocked and tb:
        g, n, kb = b.shape
        assert g * kb == kd
        tn = _tile(n, tn, LANES)
        tk = _tile(kb, tk, LANES)
        per_k = kb // tk
        b_spec = pl.BlockSpec((None, tn, tk), lambda i, j, k, *s: (k // per_k, j, k % per_k))
    elif tb:
        n, kb = b.shape
        assert kb == kd
        tn = _tile(n, tn, LANES)
        tk = _tile(kd, tk, LANES)
        b_spec = pl.BlockSpec((tn, tk), lambda i, j, k, *s: (j, k))
    else:
        kb, n = b.shape
        assert kb == kd
        tn = _tile(n // out_blocked if out_blocked else n, tn, LANES)
        per_o = (n // out_blocked) // tn if out_blocked else None
        tk = _tile(kd, tk, LANES)
        b_spec = pl.BlockSpec((tk, tn), lambda i, j, k, *s: (k, j))
    m_run = m // 2 if half else m
    tm = _tile(m_run, tm, LANES if ta else 8)

    def row(i, s):
        if not half:
            return i
        h = 1 - s[0][0] if half[1] else s[0][0]
        return h * (m_run // tm) + i

    if ta:
        a_spec = pl.BlockSpec((tk, tm), lambda i, j, k, *s: (k, row(i, s)))
    else:
        a_spec = pl.BlockSpec((tm, tk), lambda i, j, k, *s: (row(i, s), k))
    if out_blocked:
        assert not b_blocked and not tb
        o_spec = pl.BlockSpec((None, tm, tn), lambda i, j, k, *s: (j // per_o, row(i, s), j % per_o))
        o_shape = SDS((out_blocked, m, n // out_blocked), out_dtype)
    else:
        o_spec = pl.BlockSpec((tm, tn), lambda i, j, k, *s: (row(i, s), j))
        o_shape = SDS((m, n), out_dtype)
    nk = kd // tk
    dn = (((0 if ta else 1,), (1 if tb else 0,)), ((), ()))
    grid = (m_run // tm, n // tn, nk)

    pieces = [(lo, min(MM_PIECE, tn - lo)) for lo in range(0, tn, MM_PIECE)]

    def compute(a_ref, b_ref, *rest):
        add_ref = rest[0] if add else None
        o_ref, acc_ref = rest[-2:]
        a_tile = a_ref[...].astype(BF16)
        k = pl.program_id(2)

        def result(acc, cols):
            if add:
                acc = acc + add[1] * add_ref[:, cols]
            return acc.astype(o_ref.dtype)

        if nk > 1:
            @pl.when(k == 0)
            def _():
                acc_ref[...] = jnp.zeros_like(acc_ref)

        for lo, wd in pieces:
            cols = slice(lo, lo + wd)
            b_tile = b_ref[cols, :] if tb else b_ref[:, cols]
            part = lax.dot_general(a_tile, b_tile.astype(BF16), dn, preferred_element_type=F32)
            if nk == 1:
                o_ref[:, cols] = result(part, cols)
            else:
                acc_ref[:, cols] += part

        if nk > 1:
            @pl.when(k == nk - 1)
            def _():
                o_ref[...] = result(acc_ref[...], slice(None))

    extra = [(add[0], o_spec)] if add else []
    return _hosted_call(compute, comm, name=name, grid=grid, in_specs=[a_spec, b_spec] + [s_ for _, s_ in extra], out_specs=[o_spec],
                        out_shape=[o_shape], scratch_shapes=[pltpu.VMEM((tm, tn), F32)], operands=(a, b, *[a_ for a_, _ in extra]),
                        parallel=2, prefetch=half[0] if half else None)


def _swiglu_fwd(z, name, comm=()):
    t, n = z.shape
    n2 = n // 2
    tr = _tile(t, 128, 16)

    def body(a_ref, u_ref, o_ref):
        a = a_ref[...].astype(F32)
        o_ref[...] = (a * _sigmoid(a) * u_ref[...].astype(F32)).astype(o_ref.dtype)

    return _hosted_call(
        body, comm, name=name, grid=(t // tr,),
        in_specs=[pl.BlockSpec((tr, n2), lambda i: (i, 0)), pl.BlockSpec((tr, n2), lambda i: (i, 1))],
        out_specs=[pl.BlockSpec((tr, n2), lambda i: (i, 0))], out_shape=[SDS((t, n2), BF16)], scratch_shapes=[],
        operands=(z, z), parallel=1)


def _swiglu_bwd(dh, z, name):
    t, n = z.shape
    n2 = n // 2
    tr = _tile(t, 128, 16)

    def body(dh_ref, a_ref, u_ref, o_ref):
        a = a_ref[...].astype(F32)
        dh_ = dh_ref[...].astype(F32)
        s = _sigmoid(a)
        o_ref[:, 0:n2] = (dh_ * u_ref[...].astype(F32) * (s * (1.0 + a * (1.0 - s)))).astype(o_ref.dtype)
        o_ref[:, n2:n] = (dh_ * a * s).astype(o_ref.dtype)

    return pl.pallas_call(
        body, name=name, grid=(t // tr,),
        in_specs=[pl.BlockSpec((tr, n2), lambda i: (i, 0)), pl.BlockSpec((tr, n2), lambda i: (i, 0)),
                  pl.BlockSpec((tr, n2), lambda i: (i, 1))],
        out_specs=pl.BlockSpec((tr, n), lambda i: (i, 0)), out_shape=SDS((t, n), BF16),
        compiler_params=_cparams("parallel"),
    )(dh, z, z)


def _ln_stats(r):
    mu = jnp.mean(r, axis=-1, keepdims=True)
    xc = r - mu
    var = jnp.mean(xc * xc, axis=-1, keepdims=True)
    return xc * lax.rsqrt(var + LN_EPS)


def _ln_fwd(xp, y, g, b, gp, bp, scale, name, comm=()):
    t, d = xp.shape
    tr = _tile(t, 256, 16)

    def body(xp_ref, y_ref, g_ref, b_ref, *rest):
        r_ref, xb_ref = rest[-2:]
        x_prev = xp_ref[...]
        if gp is not None:
            x_prev = _ln_stats(x_prev) * rest[0][...] + rest[1][...]
        r = ALPHA * x_prev + scale * y_ref[...]
        r_ref[...] = r
        xb_ref[...] = (_ln_stats(r) * g_ref[...] + b_ref[...]).astype(BF16)

    row = pl.BlockSpec((tr, d), lambda i: (i, 0))
    vec = pl.BlockSpec((1, d), lambda i: (0, 0))
    prev = [] if gp is None else [gp, bp]
    return _hosted_call(
        body, comm, name=name, grid=(t // tr,), in_specs=[row, row, vec, vec] + [vec] * len(prev), out_specs=[row, row],
        out_shape=[SDS((t, d), F32), SDS((t, d), BF16)], scratch_shapes=[], operands=(xp, y, g, b, *prev), parallel=1)


def _ln_bwd(dra, dxm, r, g, scale, name):
    t, d = r.shape
    tr = _tile(t, 256, 16)

    def body(dra_ref, dxm_ref, r_ref, g_ref, dr_ref, dyb_ref, dg_ref, db_ref):
        i = pl.program_id(0)
        dx = ALPHA * dra_ref[...] + dxm_ref[...]
        rr = r_ref[...]
        mu = jnp.mean(rr, axis=-1, keepdims=True)
        xc = rr - mu
        rstd = lax.rsqrt(jnp.mean(xc * xc, axis=-1, keepdims=True) + LN_EPS)
        xh = xc * rstd
        dxh = dx * g_ref[...]
        dr = rstd * (dxh - jnp.mean(dxh, axis=-1, keepdims=True) - xh * jnp.mean(dxh * xh, axis=-1, keepdims=True))
        dr_ref[...] = dr
        dyb_ref[...] = (scale * dr).astype(BF16)
        dg = jnp.sum(dx * xh, axis=0, keepdims=True)
        db = jnp.sum(dx, axis=0, keepdims=True)

        @pl.when(i == 0)
        def _():
            dg_ref[...] = dg
            db_ref[...] = db

        @pl.when(i > 0)
        def _():
            dg_ref[...] += dg
            db_ref[...] += db

    row = pl.BlockSpec((tr, d), lambda i: (i, 0))
    vec = pl.BlockSpec((1, d), lambda i: (0, 0))
    return pl.pallas_call(
        body, name=name, grid=(t // tr,), in_specs=[row, row, row, vec], out_specs=[row, row, vec, vec],
        out_shape=[SDS((t, d), F32), SDS((t, d), BF16), SDS((1, d), F32), SDS((1, d), F32)],
        compiler_params=_cparams("arbitrary"),
    )(dra, dxm, r, g)


def _tail(r3, g3, b3, gp, pp, g, b, target, name):
    t, d = r3.shape
    tr = _tile(t, 256, 16)

    def body(r3_ref, g3_ref, b3_ref, gp_ref, pp_ref, g_ref, b_ref, tg_ref, dr_ref, dgp_ref, dpp_ref, dg_ref, db_ref, sq_ref):
        i = pl.program_id(0)
        gate = _sigmoid(gp_ref[...])
        pp_ = pp_ref[...]
        r = ALPHA * (_ln_stats(r3_ref[...]) * g3_ref[...] + b3_ref[...]) + gate * pp_
        mu = jnp.mean(r, axis=-1, keepdims=True)
        xc = r - mu
        rstd = lax.rsqrt(jnp.mean(xc * xc, axis=-1, keepdims=True) + LN_EPS)
        xh = xc * rstd
        err = xh * g_ref[...] + b_ref[...] - tg_ref[...]
        dx = err * (1.0 / d)
        dxh = dx * g_ref[...]
        dr = rstd * (dxh - jnp.mean(dxh, axis=-1, keepdims=True) - xh * jnp.mean(dxh * xh, axis=-1, keepdims=True))
        dr_ref[...] = dr
        dgp_ref[...] = (dr * pp_ * gate * (1.0 - gate)).astype(BF16)
        dpp_ref[...] = (dr * gate).astype(BF16)
        dg = jnp.sum(dx * xh, axis=0, keepdims=True)
        db = jnp.sum(dx, axis=0, keepdims=True)
        sq = jnp.sum(err * err, axis=0, keepdims=True)

        @pl.when(i == 0)
        def _():
            dg_ref[...] = dg
            db_ref[...] = db
            sq_ref[...] = sq

        @pl.when(i > 0)
        def _():
            dg_ref[...] += dg
            db_ref[...] += db
            sq_ref[...] += sq

    row = pl.BlockSpec((tr, d), lambda i: (i, 0))
    vec = pl.BlockSpec((1, d), lambda i: (0, 0))
    return pl.pallas_call(
        body, name=name, grid=(t // tr,), in_specs=[row, vec, vec, row, row, vec, vec, row],
        out_specs=[row, row, row, vec, vec, vec],
        out_shape=[SDS((t, d), F32), SDS((t, d), BF16), SDS((t, d), BF16), SDS((1, d), F32), SDS((1, d), F32),
                   SDS((1, d), F32)],
        compiler_params=_cparams("arbitrary"),
    )(r3, g3, b3, gp, pp, g, b, target)


def _to_bf16(x, name):
    t, d = x.shape
    tr = _tile(t, 512, 16)
    row = pl.BlockSpec((tr, d), lambda i: (i, 0))

    def body(x_ref, o_ref):
        o_ref[...] = x_ref[...].astype(BF16)

    return pl.pallas_call(body, name=name, grid=(t // tr,), in_specs=[row], out_specs=row, out_shape=SDS((t, d), BF16),
                          compiler_params=_cparams("parallel"))(x)


def _concat_cols(parts, name):
    t = parts[0].shape[0]
    widths = [p_.shape[1] for p_ in parts]
    tr = _tile(t, 256, 16)

    def body(*refs):
        o_ref = refs[-1]
        at = 0
        for ref, wd in zip(refs[:-1], widths):
            o_ref[:, at:at + wd] = ref[...]
            at += wd

    return pl.pallas_call(
        body, name=name, grid=(t // tr,), in_specs=[pl.BlockSpec((tr, wd), lambda i: (i, 0)) for wd in widths],
        out_specs=pl.BlockSpec((tr, sum(widths)), lambda i: (i, 0)), out_shape=SDS((t, sum(widths)), parts[0].dtype),
        compiler_params=_cparams("parallel"),
    )(*parts)


def _merge_fwd(z, ma, mb, w, name):
    t = z.shape[0]
    tr = _tile(t, 256, 16)

    def body(gc_ref, gh_ref, ma_ref, mb_ref, o_ref):
        o_ref[...] = (_sigmoid(gc_ref[...]) * ma_ref[...] + _sigmoid(gh_ref[...]) * mb_ref[...]).astype(BF16)

    half = pl.BlockSpec((tr, w), lambda i, j: (i, j))
    return pl.pallas_call(
        body, name=name, grid=(t // tr, 2),
        in_specs=[pl.BlockSpec((tr, w), lambda i, j: (i, 7 + j)), pl.BlockSpec((tr, w), lambda i, j: (i, 9 + j)), half, half],
        out_specs=half, out_shape=SDS((t, 2 * w), BF16), compiler_params=_cparams("parallel", "parallel"),
    )(z, z, ma, mb)


def _merge_bwd(dmer, z, ma, mb, w, name):
    t = z.shape[0]
    tr = _tile(t, 256, 16)

    def body(d_ref, gc_ref, gh_ref, ma_ref, mb_ref, dma_ref, dmb_ref, dgc_ref, dgh_ref):
        dm = d_ref[...]
        sc = _sigmoid(gc_ref[...])
        sh = _sigmoid(gh_ref[...])
        dma_ref[...] = (dm * sc).astype(BF16)
        dmb_ref[...] = (dm * sh).astype(BF16)
        dgc_ref[...] = (dm * ma_ref[...] * sc * (1.0 - sc)).astype(BF16)
        dgh_ref[...] = (dm * mb_ref[...] * sh * (1.0 - sh)).astype(BF16)

    half = pl.BlockSpec((tr, w), lambda i, j: (i, j))
    return pl.pallas_call(
        body, name=name, grid=(t // tr, 2),
        in_specs=[half, pl.BlockSpec((tr, w), lambda i, j: (i, 7 + j)), pl.BlockSpec((tr, w), lambda i, j: (i, 9 + j)), half, half],
        out_specs=[half] * 4, out_shape=[SDS((t, 2 * w), BF16)] * 4, compiler_params=_cparams("parallel", "parallel"),
    )(dmer, z, z, ma, mb)


def _shift_down(x, s, row):
    return jnp.where(row >= s, pltpu.roll(x, s, axis=0), 0.0)


def _shift_up(x, s, row, t):
    return jnp.where(row < t - s, pltpu.roll(x, t - s, axis=0), 0.0)


def _conv_fwd(z, cw, w, name):
    t = z.shape[0]
    tc = LANES
    nb = w // tc

    def body(b_ref, c_ref, h_ref, w_ref, o_ref):
        u = c_ref[...] * h_ref[...]
        row = lax.broadcasted_iota(jnp.int32, u.shape, 0)
        cw_ = w_ref[...]
        conv = cw_[2:3, :] * u + cw_[1:2, :] * _shift_down(u, 1, row) + cw_[0:1, :] * _shift_down(u, 2, row)
        o_ref[...] = (b_ref[...] * conv).astype(BF16)

    col = lambda off: pl.BlockSpec((t, tc), lambda j: (0, off * nb + j))
    return pl.pallas_call(
        body, name=name, grid=(nb,), in_specs=[col(0), col(1), col(2), pl.BlockSpec((3, tc), lambda j: (0, j))],
        out_specs=pl.BlockSpec((t, tc), lambda j: (0, j)), out_shape=SDS((t, w), BF16), compiler_params=_cparams("parallel"),
    )(z, z, z, cw)


def _conv_bwd(dy, z, cw, w, name):
    t = z.shape[0]
    tc = LANES
    nb = w // tc

    def body(dy_ref, b_ref, c_ref, h_ref, w_ref, db_ref, dc_ref, dh_ref, dw_ref):
        c_, h_ = c_ref[...], h_ref[...]
        u = c_ * h_
        row = lax.broadcasted_iota(jnp.int32, u.shape, 0)
        cw_ = w_ref[...]
        u1 = _shift_down(u, 1, row)
        u2 = _shift_down(u, 2, row)
        dy_ = dy_ref[...]
        db_ref[...] = (dy_ * (cw_[2:3, :] * u + cw_[1:2, :] * u1 + cw_[0:1, :] * u2)).astype(BF16)
        dconv = dy_ * b_ref[...]
        du = cw_[2:3, :] * dconv + cw_[1:2, :] * _shift_up(dconv, 1, row, t) + cw_[0:1, :] * _shift_up(dconv, 2, row, t)
        dc_ref[...] = (du * h_).astype(BF16)
        dh_ref[...] = (du * c_).astype(BF16)
        dw_ref[0:1, :] = jnp.sum(dconv * u2, axis=0, keepdims=True)
        dw_ref[1:2, :] = jnp.sum(dconv * u1, axis=0, keepdims=True)
        dw_ref[2:3, :] = jnp.sum(dconv * u, axis=0, keepdims=True)

    col = lambda off: pl.BlockSpec((t, tc), lambda j: (0, off * nb + j))
    own = pl.BlockSpec((t, tc), lambda j: (0, j))
    wsp = pl.BlockSpec((3, tc), lambda j: (0, j))
    return pl.pallas_call(
        body, name=name, grid=(nb,), in_specs=[own, col(0), col(1), col(2), wsp], out_specs=[own, own, own, wsp],
        out_shape=[SDS((t, w), BF16)] * 3 + [SDS((3, w), F32)], compiler_params=_cparams("parallel"),
    )(dy, z, z, z, cw)


def _lower_bound(hg):
    mx = jnp.max(hg, axis=0, keepdims=True)
    e = jnp.exp(hg - mx)
    inv = 1.0 / jnp.sum(e, axis=0, keepdims=True)
    return e[0:1, :] * inv, e[1:2, :] * inv


def _chunk_cumsum(x, row):
    s = 1
    while s < CHUNK:
        x = x + jnp.where(row % CHUNK >= s, pltpu.roll(x, s, axis=0), 0.0)
        s *= 2
    return x


def _dot_nt(a, b):
    return lax.dot_general(a.astype(BF16), b.astype(BF16), (((1,), (1,)), ((), ())), preferred_element_type=F32)


def _dot_tn(a, b):
    return lax.dot_general(a.astype(BF16), b.astype(BF16), (((0,), (0,)), ((), ())), preferred_element_type=F32)


def _dot_nn(a, b):
    return jnp.dot(a.astype(BF16), b.astype(BF16), preferred_element_type=F32)


def _tril(x):
    r = lax.broadcasted_iota(jnp.int32, x.shape, 0)
    c = lax.broadcasted_iota(jnp.int32, x.shape, 1)
    return jnp.where(r >= c, x, 0.0)


HGRN_GROUP = 4
HGRN_ROWS = 512
HGRN_UNROLL = 2


def _unrolled_loop(n, step, init):
    assert n % HGRN_UNROLL == 0

    def trip(i, carry):
        for u in range(HGRN_UNROLL):
            carry = step(i * HGRN_UNROLL + u, carry)
        return carry

    return lax.fori_loop(0, n // HGRN_UNROLL, trip, init)


def _hgrn_chunk_inputs(q_ref, f_ref, cum_ref, lb, rows, ln):
    qr = q_ref[rows, ln]
    q = qr * _sigmoid(qr)
    f = lb + (1.0 - lb) * _sigmoid(f_ref[rows, ln])
    return q, 1.0 - f, cum_ref[rows, ln]


def _hgrn_fwd(z, hg, nw, w, name, comm=()):
    t = z.shape[0]
    nh = w // HEAD
    gh = _tile(nh, HGRN_GROUP, 1)
    gw = gh * HEAD
    ngrp = nh // gh
    tb = _tile(t, HGRN_ROWS, CHUNK)
    ncb = tb // CHUNK

    def body(q_ref, f_ref, i_ref, g_ref, hg_ref, nw_ref, y_ref, o_ref, st_ref, cum_ref, *s_refs):
        lb_all, _ = _lower_bound(hg_ref[...])
        row = lax.broadcasted_iota(jnp.int32, (tb, gw), 0)
        cum_ref[...] = _chunk_cumsum(jnp.log(lb_all + (1.0 - lb_all) * _sigmoid(f_ref[...])), row)

        @pl.when(pl.program_id(1) == 0)
        def _():
            for s_ref in s_refs:
                s_ref[...] = jnp.zeros_like(s_ref)

        def step(c, carry):
            rows = pl.ds(pl.multiple_of(c * CHUNK, CHUNK), CHUNK)
            for g in range(gh):
                ln = slice(g * HEAD, (g + 1) * HEAD)
                lb = lb_all[:, ln]
                q, k, cum = _hgrn_chunk_inputs(q_ref, f_ref, cum_ref, lb, rows, ln)
                v = i_ref[rows, ln]
                last = cum[CHUNK - 1:CHUNK, :]
                mid = cum[CHUNK // 2 - 1:CHUNK // 2, :]
                st = s_refs[g][...]
                st_ref[g, c] = st.astype(BF16)
                scores = _tril(_dot_nt(q * jnp.exp(cum - mid), k * jnp.exp(mid - cum)))
                o_ref[rows, ln] = _dot_nt(q * jnp.exp(cum), st) + _dot_nn(scores, v)
                s_refs[g][...] = st * jnp.exp(last) + _dot_tn(v, k * jnp.exp(last - cum))
            return carry

        _unrolled_loop(ncb, step, 0)
        for g in range(gh):
            ln = slice(g * HEAD, (g + 1) * HEAD)
            o = o_ref[:, ln]
            n = o * lax.rsqrt(jnp.mean(o * o, axis=-1, keepdims=True) + RMS_EPS)
            gr = g_ref[:, ln]
            y_ref[:, ln] = (n * nw_ref[...] * gr * _sigmoid(gr)).astype(BF16)

    col = lambda off: pl.BlockSpec((tb, gw), lambda h, j: (j, off * ngrp + h))
    own = pl.BlockSpec((tb, gw), lambda h, j: (j, h))
    return _hosted_call(
        body, comm, name=name, grid=(ngrp, t // tb),
        in_specs=[col(3), col(4), col(5), col(6), pl.BlockSpec((2, gw), lambda h, j: (0, h)),
                  pl.BlockSpec((1, HEAD), lambda h, j: (0, 0))],
        out_specs=[own, own, pl.BlockSpec((gh, ncb, HEAD, HEAD), lambda h, j: (h, j, 0, 0))],
        out_shape=[SDS((t, w), BF16), SDS((t, w), F32), SDS((nh, t // CHUNK, HEAD, HEAD), BF16)],
        scratch_shapes=[pltpu.VMEM((tb, gw), F32)] + [pltpu.VMEM((HEAD, HEAD), F32)] * gh,
        operands=(z, z, z, z, hg, nw), parallel=1)


def _hgrn_bwd(dy, z, o, states, hg, nw, w, name, comm=()):
    t = z.shape[0]
    nh = w // HEAD
    gh = _tile(nh, HGRN_GROUP, 1)
    gw = gh * HEAD
    ngrp = nh // gh
    tb = _tile(t, HGRN_ROWS, CHUNK)
    ncb = tb // CHUNK
    nt = t // tb

    def body(dy_ref, q_ref, f_ref, i_ref, g_ref, o_ref, st_ref, hg_ref, nw_ref,
             dq_ref, df_ref, di_ref, dg_ref, dhg_ref, dnw_ref, cum_ref, do_ref, *ds_refs):
        lb_all, s1_all = _lower_bound(hg_ref[...])
        row = lax.broadcasted_iota(jnp.int32, (tb, gw), 0)
        crow = lax.broadcasted_iota(jnp.int32, (CHUNK, HEAD), 0)
        cum_ref[...] = _chunk_cumsum(jnp.log(lb_all + (1.0 - lb_all) * _sigmoid(f_ref[...])), row)

        @pl.when(pl.program_id(1) == 0)
        def _():
            for ds_ref in ds_refs:
                ds_ref[...] = jnp.zeros_like(ds_ref)
            dhg_ref[...] = jnp.zeros_like(dhg_ref)
            dnw_ref[...] = jnp.zeros_like(dnw_ref)

        for g in range(gh):
            ln = slice(g * HEAD, (g + 1) * HEAD)
            o_ = o_ref[:, ln]
            rstd = lax.rsqrt(jnp.mean(o_ * o_, axis=-1, keepdims=True) + RMS_EPS)
            n = o_ * rstd
            gr = g_ref[:, ln]
            sg = _sigmoid(gr)
            dy_ = dy_ref[:, ln]
            dg_ref[:, ln] = (dy_ * n * nw_ref[...] * (sg * (1.0 + gr * (1.0 - sg)))).astype(BF16)
            dsil = dy_ * gr * sg
            dnw_ref[:, ln] += jnp.sum(dsil * n, axis=0, keepdims=True)
            dn = dsil * nw_ref[...]
            do_ref[:, ln] = rstd * (dn - n * jnp.mean(dn * n, axis=-1, keepdims=True))

        def step(cc, dlbs):
            c = ncb - 1 - cc
            rows = pl.ds(pl.multiple_of(c * CHUNK, CHUNK), CHUNK)
            new = []
            for g in range(gh):
                ln = slice(g * HEAD, (g + 1) * HEAD)
                lb = lb_all[:, ln]
                qr = q_ref[rows, ln]
                sq = _sigmoid(qr)
                q = qr * sq
                sf = _sigmoid(f_ref[rows, ln])
                f = lb + (1.0 - lb) * sf
                k = 1.0 - f
                cum = cum_ref[rows, ln]
                v = i_ref[rows, ln]
                do = do_ref[rows, ln]
                last = cum[CHUNK - 1:CHUNK, :]
                mid = cum[CHUNK // 2 - 1:CHUNK // 2, :]
                eg = jnp.exp(cum)
                em = jnp.exp(cum - mid)
                enm = jnp.exp(mid - cum)
                elc = jnp.exp(last - cum)
                qm, km, kl = q * em, k * enm, k * elc
                ds = ds_refs[g][...]
                a = _tril(_dot_nt(qm, km))
                da = _tril(_dot_nt(do, v))
                di_ref[rows, ln] = (_dot_tn(a, do) + _dot_nt(kl, ds)).astype(BF16)
                st = st_ref[g, c]
                dkl = _dot_nn(v, ds)
                dq = _dot_nn(do, st) * eg + _dot_nn(da, km) * em
                dk = _dot_tn(da, qm) * enm + dkl * elc
                el = jnp.exp(last)
                ds_refs[g][...] = ds * el + _dot_tn(do, q * eg)
                dlast = jnp.sum(kl * dkl, axis=0, keepdims=True) + el * jnp.sum(ds * st.astype(F32), axis=0, keepdims=True)
                x = q * dq - k * dk + jnp.where(crow == CHUNK - 1, dlast, 0.0)
                s = 1
                while s < CHUNK:
                    x = x + _shift_up(x, s, crow, CHUNK)
                    s *= 2
                df = x / f - dk
                dq_ref[rows, ln] = (dq * (sq * (1.0 + qr * (1.0 - sq)))).astype(BF16)
                df_ref[rows, ln] = (df * (1.0 - lb) * sf * (1.0 - sf)).astype(BF16)
                new.append(dlbs[g] + jnp.sum(df * (1.0 - sf), axis=0, keepdims=True))
            return tuple(new)

        dlbs = _unrolled_loop(ncb, step, tuple(jnp.zeros((1, HEAD), F32) for _ in range(gh)))
        for g in range(gh):
            ln = slice(g * HEAD, (g + 1) * HEAD)
            dlb = dlbs[g] * lb_all[:, ln] * s1_all[:, ln]
            dhg_ref[0:1, ln] += dlb
            dhg_ref[1:2, ln] -= dlb

    col = lambda off: pl.BlockSpec((tb, gw), lambda h, j: (nt - 1 - j, off * ngrp + h))
    own = pl.BlockSpec((tb, gw), lambda h, j: (nt - 1 - j, h))
    hsp = pl.BlockSpec((2, gw), lambda h, j: (0, h))
    return _hosted_call(
        body, comm, name=name, grid=(ngrp, nt),
        in_specs=[own, col(3), col(4), col(5), col(6), own,
                  pl.BlockSpec((gh, ncb, HEAD, HEAD), lambda h, j: (h, nt - 1 - j, 0, 0)),
                  hsp, pl.BlockSpec((1, HEAD), lambda h, j: (0, 0))],
        out_specs=[own, own, own, own, hsp, pl.BlockSpec((1, gw), lambda h, j: (0, h))],
        out_shape=[SDS((t, w), BF16)] * 4 + [SDS((2, w), F32), SDS((1, w), F32)],
        scratch_shapes=[pltpu.VMEM((tb, gw), F32)] * 2 + [pltpu.VMEM((HEAD, HEAD), F32)] * gh,
        operands=(dy, z, z, z, z, o, states, hg, nw), parallel=1)


def _cast_pad(wt, n_pad, meta, sp, name, comm=()):
    _, r, n = wt.shape
    g, p, per = meta
    tr = _tile(r, max(16, (3 << 19) // n_pad // 16 * 16), 16)

    def body(w_ref, o_ref):
        if n_pad != n:
            o_ref[...] = jnp.zeros(o_ref.shape, o_ref.dtype)
        o_ref[:, 0:n] = w_ref[...].astype(BF16)

    return _hosted_call(
        body, comm, name=name, grid=(r // tr,), in_specs=[pl.BlockSpec((None, tr, n), lambda i, sp: (0, i, 0))],
        out_specs=[pl.BlockSpec((None, tr, n_pad), lambda i, sp: (sp[1] // per, ((sp[1] % per) * r) // tr + i, 0))],
        out_shape=[SDS((g, p, n_pad), BF16)], scratch_shapes=[], operands=(wt,), parallel=1, prefetch=sp)


def _cast_pad_t(wt_t, n_pad, meta, sp, name, comm=()):
    _, n, r = wt_t.shape
    g, p, per = meta
    tc = _tile(r, 256, LANES)

    def body(w_ref, o_ref):
        for lo in range(0, n_pad, LANES):
            rows = min(LANES, n - lo)
            piece = w_ref[lo:lo + rows, :]
            if rows < LANES:
                piece = jnp.concatenate([piece, jnp.zeros((LANES - rows, tc), F32)], axis=0)
            o_ref[:, lo:lo + LANES] = piece.T.astype(BF16)

    return _hosted_call(
        body, comm, name=name, grid=(r // tc,), in_specs=[pl.BlockSpec((None, n, tc), lambda i, sp: (0, 0, i))],
        out_specs=[pl.BlockSpec((None, tc, n_pad), lambda i, sp: (sp[1] // per, ((sp[1] % per) * r) // tc + i, 0))],
        out_shape=[SDS((g, p, n_pad), BF16)], scratch_shapes=[], operands=(wt_t,), parallel=1, prefetch=sp)


def _adam_math(w, g, m, v):
    m2 = ADAM_B1 * m + (1.0 - ADAM_B1) * g
    v2 = ADAM_B2 * v + (1.0 - ADAM_B2) * (g * g)
    c1 = 1.0 / (1.0 - ADAM_B1 ** ADAM_STEP)
    c2 = 1.0 / (1.0 - ADAM_B2 ** ADAM_STEP)
    return -ADAM_LR * ((m2 * c1) / (jnp.sqrt(v2 * c2) + ADAM_EPS) + ADAM_WD * w), m2, v2


def _adamw_t(wt_t, g, m_t, v_t, name):
    _, n, r = wt_t.shape
    ng = g.shape[1]
    tc = LANES

    def body(w_ref, g_ref, m_ref, v_ref, go_ref, d_ref, mo_ref, vo_ref, gt_ref):
        for lo in range(0, ng, LANES):
            gt_ref[lo:lo + LANES, :] = g_ref[:, lo:lo + LANES].T
        g_ = gt_ref[0:n, :]
        delta, m2, v2 = _adam_math(w_ref[...], g_, m_ref[...], v_ref[...])
        go_ref[...] = g_
        d_ref[...] = delta
        mo_ref[...] = m2
        vo_ref[...] = v2

    blk = pl.BlockSpec((None, n, tc), lambda i: (0, 0, i))
    return pl.pallas_call(
        body, name=name, grid=(r // tc,), in_specs=[blk, pl.BlockSpec((tc, ng), lambda i: (i, 0)), blk, blk],
        out_specs=[blk] * 4, out_shape=[SDS(wt_t.shape, F32)] * 4, scratch_shapes=[pltpu.VMEM((ng, tc), F32)],
        compiler_params=_cparams("parallel"),
    )(wt_t, g, m_t, v_t)


def _adamw(wt, g, m, v, name):
    lead = (None,) * (wt.ndim - 2)
    zero = (0,) * (wt.ndim - 2)
    r, n = wt.shape[-2:]
    ng = g.shape[1]
    nct = 2 if ng == n and n % (2 * LANES) == 0 else 1
    tc, tg = n // nct, ng // nct
    tr = _tile(r, max(8, (3 << 17) // tg // 8 * 8), 8)

    def body(w_ref, g_ref, m_ref, v_ref, go_ref, d_ref, mo_ref, vo_ref):
        g_ = g_ref[:, 0:tc]
        delta, m2, v2 = _adam_math(w_ref[...], g_, m_ref[...], v_ref[...])
        go_ref[...] = g_
        d_ref[...] = delta
        mo_ref[...] = m2
        vo_ref[...] = v2

    blk = pl.BlockSpec(lead + (tr, tc), lambda i, j: zero + (i, j))
    return pl.pallas_call(
        body, name=name, grid=(r // tr, nct), in_specs=[blk, pl.BlockSpec((tr, tg), lambda i, j: (i, j)), blk, blk],
        out_specs=[blk] * 4, out_shape=[SDS(wt.shape, F32)] * 4, compiler_params=_cparams("parallel", "parallel"),
    )(wt, g, m, v)


def _place():
    x, y, c = lax.axis_index("x"), lax.axis_index("y"), lax.axis_index("c")
    return x, y, c, 2 * x + y


def _chip_dev(k, c):
    return (k // 2, k % 2, c)


def _half(ref, j, h, rows, per):
    return ref.at[j // per, pl.ds((j % per) * rows + h * (rows // 2), rows // 2)]


def _gather_stage(bufs, metas, rows_of, ici_parts, fwd_parts, zero_pad):
    nw = len(bufs)
    ici_on = [i for i in range(nw) if ici_parts[i] is not None]
    fwd_on = [i for i in range(nw) if fwd_parts[i] is not None]
    pad_jobs = [(i, gi) for i in ici_on if ici_parts[i][0] == 0 and metas[i][1] > metas[i][2] * rows_of[i]
                for gi in range(metas[i][0])]

    def part_of(ref, i, j, h, part):
        per = metas[i][2]
        p, np_ = part
        pr = rows_of[i] // 2 // np_
        return ref.at[j // per, pl.ds((j % per) * rows_of[i] + h * (rows_of[i] // 2) + p * pr, pr)]

    def descriptors(ins, outs, sems):
        src, zp, dst = ins[:nw], ins[nw], outs
        pads, send, recv, fsend, frecv = sems
        x, y, c, me = _place()

        def pad(n):
            i, gi = pad_jobs[n]
            extra = metas[i][1] - metas[i][2] * rows_of[i]
            return pltpu.make_async_copy(zp.at[pl.ds(0, extra)], dst[i].at[gi, pl.ds(metas[i][2] * rows_of[i], extra)], pads.at[n])

        def ici(i, r, frm):
            return pltpu.make_async_remote_copy(
                src_ref=part_of(src[i], i, me, c, ici_parts[i]), dst_ref=part_of(dst[i], i, frm, c, ici_parts[i]),
                send_sem=send.at[i, r - 1], recv_sem=recv.at[i, r - 1], device_id=_chip_dev((me + r) % N_CHIPS, c),
                device_id_type=MESH)

        def d2d(i, r, frm, h):
            blk = part_of(dst[i], i, frm, h, fwd_parts[i])
            return pltpu.make_async_remote_copy(src_ref=blk, dst_ref=blk, send_sem=fsend.at[i, r - 1],
                                                recv_sem=frecv.at[i, r - 1], device_id=(x, y, 1 - c), device_id_type=MESH)

        return pad, ici, d2d, c, me

    def start(ins, outs, sems):
        pad, ici, d2d, c, me = descriptors(ins, outs, sems)
        for n in range(len(pad_jobs)):
            pad(n).start()
        for i in fwd_on:
            for r in range(1, N_CHIPS):
                d2d(i, r, (me - r) % N_CHIPS, c).start()
        for i in ici_on:
            for r in range(1, N_CHIPS):
                ici(i, r, me).start()

    def finish(ins, outs, sems):
        pad, ici, d2d, c, me = descriptors(ins, outs, sems)
        for i in fwd_on:
            for r in range(1, N_CHIPS):
                d2d(i, r, (me - r) % N_CHIPS, 1 - c).wait_recv()
                d2d(i, r, (me - r) % N_CHIPS, c).wait_send()
        for i in ici_on:
            for r in range(1, N_CHIPS):
                ici(i, r, (me - r) % N_CHIPS).wait_recv()
                ici(i, r, me).wait_send()
        for n in range(len(pad_jobs)):
            pad(n).wait()

    return _Stage(ins=list(bufs) + [zero_pad], out_shapes=[SDS(b.shape, b.dtype) for b in bufs],
                  aliases={i: i for i in range(nw)},
                  sems=[pltpu.SemaphoreType.DMA((max(len(pad_jobs), 1),))] + [pltpu.SemaphoreType.DMA((nw, N_CHIPS - 1))] * 4,
                  start=start, finish=finish)


def _gather_small(packed, name):
    r, n = packed.shape

    def body(src, dst, send, recv):
        x, y, c, me = _place()
        dst[me] = src[...]
        cps = []
        for d in range(1, N_CHIPS):
            cp = pltpu.make_async_remote_copy(src_ref=src, dst_ref=dst.at[me], send_sem=send.at[d - 1], recv_sem=recv.at[d - 1],
                                              device_id=_chip_dev((me + d) % N_CHIPS, c), device_id_type=MESH)
            cp.start()
            cps.append(cp)
        for d in range(1, N_CHIPS):
            pltpu.make_async_remote_copy(src_ref=src, dst_ref=dst.at[(me - d) % N_CHIPS], send_sem=send.at[d - 1],
                                         recv_sem=recv.at[d - 1], device_id=_chip_dev((me + d) % N_CHIPS, c),
                                         device_id_type=MESH).wait_recv()
        for cp in cps:
            cp.wait_send()

    return pl.pallas_call(
        body, name=name, in_specs=[VMEM_SPEC], out_specs=VMEM_SPEC, out_shape=SDS((N_CHIPS, r, n), F32),
        scratch_shapes=[pltpu.SemaphoreType.DMA((N_CHIPS - 1,))] * 2,
    )(packed)


def _all_reduce_small(packed, name):
    r, n = packed.shape

    def body(src, out, slots, send, recv):
        x, y, c, me = _place()
        idx = 2 * me + c
        slots[idx] = src[...]
        cps = []

        def peer(d):
            p = (idx + d) % N_DEV
            return (p // 4, (p // 2) % 2, p % 2)

        for d in range(1, N_DEV):
            cp = pltpu.make_async_remote_copy(src_ref=src, dst_ref=slots.at[idx], send_sem=send.at[d - 1], recv_sem=recv.at[d - 1],
                                              device_id=peer(d), device_id_type=MESH)
            cp.start()
            cps.append(cp)
        for d in range(1, N_DEV):
            pltpu.make_async_remote_copy(src_ref=src, dst_ref=slots.at[(idx - d) % N_DEV], send_sem=send.at[d - 1],
                                         recv_sem=recv.at[d - 1], device_id=peer(d), device_id_type=MESH).wait_recv()
        for cp in cps:
            cp.wait_send()
        acc = slots[0]
        for k in range(1, N_DEV):
            acc = acc + slots[k]
        out[...] = acc

    return pl.pallas_call(
        body, name=name, in_specs=[VMEM_SPEC], out_specs=VMEM_SPEC, out_shape=SDS((r, n), F32),
        scratch_shapes=[pltpu.VMEM((N_DEV, r, n), F32)] + [pltpu.SemaphoreType.DMA((N_DEV - 1,))] * 2,
    )(packed)


def _simple_stage(ins, out_shapes, aliases, n_copies, copies):
    def start(ins_, outs, sems):
        for cp in copies(ins_, outs, *sems):
            cp.start()

    def finish(ins_, outs, sems):
        for cp in copies(ins_, outs, *sems):
            cp.wait()

    return _Stage(ins=list(ins), out_shapes=list(out_shapes), aliases=aliases,
                  sems=[pltpu.SemaphoreType.DMA((n_copies,))] * 2, start=start, finish=finish)


def _rs_pair_exchange(grads, metas, rows_of):
    nw = len(grads)

    def copies(src, dst, send, recv):
        x, y, c, me = _place()
        return [pltpu.make_async_remote_copy(
            src_ref=_half(src[i], j, 1 - c, rows_of[i], metas[i][2]), dst_ref=dst[i].at[j], send_sem=send.at[i * N_CHIPS + j],
            recv_sem=recv.at[i * N_CHIPS + j], device_id=(x, y, 1 - c), device_id_type=MESH)
            for i in range(nw) for j in range(N_CHIPS)]

    out_shapes = [SDS((N_CHIPS, rows_of[i] // 2, g.shape[2]), g.dtype) for i, g in enumerate(grads)]
    return _simple_stage(grads, out_shapes, {}, nw * N_CHIPS, copies)


def _rs_pair_add(g, got, meta, rows, sp, name):
    per = meta[2]
    n = g.shape[2]
    hr = rows // 2
    tr = _tile(hr, max(16, (3 << 19) // n // 16 * 16), 16)

    def body(sp_ref, g_ref, got_ref, snd_ref, own_ref):
        j = pl.program_id(1)
        s = g_ref[...].astype(F32) + got_ref[...].astype(F32)
        snd_ref[...] = s.astype(BF16)

        @pl.when(j == sp_ref[1])
        def _():
            own_ref[...] = s

    grid_spec = pltpu.PrefetchScalarGridSpec(
        num_scalar_prefetch=1, grid=(hr // tr, N_CHIPS),
        in_specs=[pl.BlockSpec((None, tr, n), lambda i, j, sp: (j // per, ((j % per) * rows + sp[0] * hr) // tr + i, 0)),
                  pl.BlockSpec((None, tr, n), lambda i, j, sp: (j, i, 0))],
        out_specs=[pl.BlockSpec((None, tr, n), lambda i, j, sp: (j, i, 0)), pl.BlockSpec((tr, n), lambda i, j, sp: (i, 0))])
    return pl.pallas_call(
        body, name=name, grid_spec=grid_spec, out_shape=[SDS((N_CHIPS, hr, n), BF16), SDS((hr, n), F32)],
        compiler_params=_cparams("parallel", "arbitrary"),
    )(sp, g, got)


def _rs_chip_exchange(sends, part=(0, 1), prev=None):
    nw = len(sends)
    p, np_ = part

    def copies(src, dst, send, recv):
        x, y, c, me = _place()
        cps = []
        for i in range(nw):
            pr = sends[i].shape[1] // np_
            for r in range(1, N_CHIPS):
                cps.append(pltpu.make_async_remote_copy(
                    src_ref=src[i].at[(me + r) % N_CHIPS, pl.ds(p * pr, pr)], dst_ref=dst[i].at[r - 1, pl.ds(p * pr, pr)],
                    send_sem=send.at[i * (N_CHIPS - 1) + r - 1], recv_sem=recv.at[i * (N_CHIPS - 1) + r - 1],
                    device_id=_chip_dev((me + r) % N_CHIPS, c), device_id_type=MESH))
        return cps

    out_shapes = [SDS((N_CHIPS - 1,) + s.shape[1:], BF16) for s in sends]
    if prev is None:
        return _simple_stage(sends, out_shapes, {}, nw * (N_CHIPS - 1), copies)
    return _simple_stage(list(sends) + list(prev), out_shapes, {nw + i: i for i in range(nw)}, nw * (N_CHIPS - 1), copies)


def _rs_chip_add(own, got, sp, name):
    hr, n = own.shape
    tr = _tile(hr, max(16, (3 << 19) // n // 16 * 16), 16)

    def body(sp_ref, own_ref, got_ref, o_ref):
        acc = own_ref[...]
        for r in range(N_CHIPS - 1):
            acc = acc + got_ref[r].astype(F32)
        o_ref[...] = acc

    grid_spec = pltpu.PrefetchScalarGridSpec(
        num_scalar_prefetch=1, grid=(hr // tr,),
        in_specs=[pl.BlockSpec((tr, n), lambda i, sp: (i, 0)), pl.BlockSpec((N_CHIPS - 1, tr, n), lambda i, sp: (0, i, 0))],
        out_specs=pl.BlockSpec((tr, n), lambda i, sp: (sp[0] * (hr // tr) + i, 0)))
    return pl.pallas_call(body, name=name, grid_spec=grid_spec, out_shape=SDS((2 * hr, n), F32),
                          compiler_params=_cparams("parallel"))(sp, own, got)


def _rs_pair_share(blocks):
    nw = len(blocks)

    def copies(src, dst, send, recv):
        x, y, c, me = _place()
        cps = []
        for i in range(nw):
            hr = src[i].shape[0] // 2
            cps.append(pltpu.make_async_remote_copy(
                src_ref=src[i].at[pl.ds(c * hr, hr)], dst_ref=dst[i].at[pl.ds(c * hr, hr)], send_sem=send.at[i],
                recv_sem=recv.at[i], device_id=(x, y, 1 - c), device_id_type=MESH))
        return cps

    return _simple_stage(blocks, [SDS(b.shape, b.dtype) for b in blocks], {i: i for i in range(nw)}, nw, copies)


def kernel(x, p, ln_g, ln_b, ffn1_w_in, ffn1_w_out, mix_w_in, conv_w, hg_lower_bound, hg_norm_w, branch_w_conv, branch_w_hgrn, mix_w_out, ffn2_w_in, ffn2_w_out, ple_w_gate, ple_w_proj, loss_target, m_ln_g, m_ln_b, m_ffn1_w_in, m_ffn1_w_out, m_mix_w_in, m_conv_w, m_hg_lower_bound, m_hg_norm_w, m_branch_w_conv, m_branch_w_hgrn, m_mix_w_out, m_ffn2_w_in, m_ffn2_w_out, m_ple_w_gate, m_ple_w_proj, v_ln_g, v_ln_b, v_ffn1_w_in, v_ffn1_w_out, v_mix_w_in, v_conv_w, v_hg_lower_bound, v_hg_norm_w, v_branch_w_conv, v_branch_w_hgrn, v_mix_w_out, v_ffn2_w_in, v_ffn2_w_out, v_ple_w_gate, v_ple_w_proj):
    assert ln_g.shape[0] == DEPTH and x.shape[0] == 1 and p.shape[:2] == (1, 1)
    t, d = x.shape[1], x.shape[2]
    w = d // 2
    x0 = x.reshape(t, d)
    x0b = _to_bf16(x0, "x_bf16")
    pe = p.reshape(t, p.shape[-1])
    target = loss_target.reshape(t, d)
    cx, cy, cc = lax.axis_index("x"), lax.axis_index("y"), lax.axis_index("c")
    chip = 2 * cx + cy
    sp = jnp.stack([cc, chip]).astype(jnp.int32)

    big = dict(ffn1_w_in=ffn1_w_in, ffn1_w_out=ffn1_w_out, mix_w_in=mix_w_in, branch_w_conv=branch_w_conv,
               branch_w_hgrn=branch_w_hgrn, mix_w_out=mix_w_out, ffn2_w_in=ffn2_w_in, ffn2_w_out=ffn2_w_out,
               ple_w_gate=ple_w_gate, ple_w_proj=ple_w_proj)
    moments = dict(ffn1_w_in=(m_ffn1_w_in, v_ffn1_w_in), ffn1_w_out=(m_ffn1_w_out, v_ffn1_w_out), mix_w_in=(m_mix_w_in, v_mix_w_in),
                   branch_w_conv=(m_branch_w_conv, v_branch_w_conv), branch_w_hgrn=(m_branch_w_hgrn, v_branch_w_hgrn),
                   mix_w_out=(m_mix_w_out, v_mix_w_out), ffn2_w_in=(m_ffn2_w_in, v_ffn2_w_in), ffn2_w_out=(m_ffn2_w_out, v_ffn2_w_out),
                   ple_w_gate=(m_ple_w_gate, v_ple_w_gate), ple_w_proj=(m_ple_w_proj, v_ple_w_proj))
    names = list(big)

    n_loc = ffn1_w_in.shape[-1]
    n_pad = -(-n_loc // LANES) * LANES
    assert mix_w_in.shape[-1] % LANES == 0 and ffn1_w_out.shape[1] * 2 == n_loc
    pad_cols = dict(ffn1_w_in=n_pad, ffn2_w_in=n_pad)
    meta = {k: (N_CHIPS, big[k].shape[1], 1) for k in names}
    meta["ffn1_w_out"] = meta["ffn2_w_out"] = (2, n_pad, 2)
    rows = {k: big[k].shape[1] for k in names}
    swap = lambda a: jnp.transpose(a, (0, 2, 1))
    wbuf = {}
    zero_pad = jnp.zeros((max(n_pad - n_loc, 16), d), BF16)

    def cast(k, comm=()):
        if k in pad_cols:
            return _cast_pad_t(swap(big[k]), pad_cols[k], meta[k], sp, "cast_" + k, comm=comm)
        return _cast_pad(big[k], big[k].shape[2], meta[k], sp, "cast_" + k, comm=comm)

    def gather(ici=(), fwd=()):
        ks = list(dict.fromkeys([k for k, _, _ in ici] + [k for k, _, _ in fwd]))
        ip = {k: (p_, n_) for k, p_, n_ in ici}
        fp = {k: (p_, n_) for k, p_, n_ in fwd}
        return _gather_stage([wbuf[k] for k in ks], [meta[k] for k in ks], [rows[k] for k in ks], [ip.get(k) for k in ks],
                             [fp.get(k) for k in ks], zero_pad), ks

    def gathered(ks, outs):
        wbuf.update(zip(ks, outs))

    def w3(k):
        return wbuf[k]

    def w2(k):
        return wbuf[k].reshape(-1, wbuf[k].shape[2])

    dq, wq = d // N_CHIPS, w // N_CHIPS
    small = jnp.concatenate([ln_g[0], ln_b[0], jnp.pad(conv_w[0], ((0, 5), (0, dq - wq)))], axis=0)
    small = _gather_small(small, "gather_small")
    lng = small[:, 0:4, :].transpose(1, 0, 2).reshape(4, 1, d)
    lnb = small[:, 4:8, :].transpose(1, 0, 2).reshape(4, 1, d)
    cw = small[:, 8:11, :wq].transpose(1, 0, 2).reshape(3, w)
    hg = hg_lower_bound
    nw_ = hg_norm_w

    one = lambda *ks_: [(k, 0, 1) for k in ks_]
    wbuf["ffn1_w_in"] = cast("ffn1_w_in")
    first = ["ple_w_proj", "ffn1_w_out", "mix_w_in", None, "ffn2_w_in", "ffn2_w_out"]
    carriers = first + [k for k in names if k != "ffn1_w_in" and k not in first]
    assert len(carriers) > FIRST_GATHER_PARTS
    for step, k in enumerate(carriers):
        ici = [("ffn1_w_in", step, FIRST_GATHER_PARTS)] if step < FIRST_GATHER_PARTS else []
        fwd = [("ffn1_w_in", step - 1, FIRST_GATHER_PARTS)] if 1 <= step <= FIRST_GATHER_PARTS else []
        ici += one("ple_w_proj") if step == 1 else []
        fwd += one("ple_w_proj") if step == 2 else []
        if ici or fwd:
            st, ks = gather(ici, fwd)
            if k is None:
                pp, got = _mm(pe, w3("ple_w_proj"), name="ple_proj", b_blocked=True, tn=512, comm=[st])
            else:
                wbuf[k], got = cast(k, comm=[st])
            gathered(ks, got)
        else:
            wbuf[k] = cast(k)
    st, ks = gather(ici=one("ffn1_w_out") + [("mix_w_in", 0, 2)])
    z1, got = _mm(x0b, w3("ffn1_w_in"), name="ffn1_in", b_blocked=True, out_dtype=BF16, tm=1024, comm=[st])
    gathered(ks, got)
    st, ks = gather(fwd=one("ffn1_w_out") + [("mix_w_in", 0, 2)])
    h1, got = _swiglu_fwd(z1, "ffn1_act", comm=[st])
    gathered(ks, got)
    st, ks = gather(ici=[("mix_w_in", 1, 2)])
    y1, got = _mm(h1, w2("ffn1_w_out"), name="ffn1_out", tm=1024, tn=1024, tk=2816, comm=[st])
    gathered(ks, got)
    st, ks = gather(fwd=[("mix_w_in", 1, 2)])
    (r1, x1b), got = _ln_fwd(x0, y1, lng[0], lnb[0], None, None, 0.5, "ln0", comm=[st])
    gathered(ks, got)
    mixo_w = one("branch_w_conv", "branch_w_hgrn", "mix_w_out")
    st, ks = gather(ici=mixo_w + [("ffn2_w_in", 0, 2)])
    z, got = _mm(x1b, w3("mix_w_in"), name="mix_in", b_blocked=True, tm=1024, comm=[st])
    gathered(ks, got)
    ya = _conv_fwd(z, cw, w, "conv_fwd")
    st, ks = gather(ici=[("ffn2_w_in", 1, 2)], fwd=mixo_w + [("ffn2_w_in", 0, 2)])
    (yb, o_h, states), got = _hgrn_fwd(z, hg, nw_, w, "hgrn_fwd", comm=[st])
    gathered(ks, got)
    ma = _mm(ya, w3("branch_w_conv"), name="branch_conv", b_blocked=True, tn=512)
    mb = _mm(yb, w3("branch_w_hgrn"), name="branch_hgrn", b_blocked=True, tn=512)
    merged = _merge_fwd(z, ma, mb, w, "merge_fwd")
    st, ks = gather(fwd=[("ffn2_w_in", 1, 2)])
    y2, got = _mm(merged, w2("mix_w_out"), name="mix_out", tn=1024, comm=[st])
    gathered(ks, got)
    r2, x2b = _ln_fwd(r1, y2, lng[1], lnb[1], lng[0], lnb[0], 1.0, "ln1")
    late = one("ffn2_w_out", "ple_w_gate")
    st, ks = gather(ici=late)
    z3, got = _mm(x2b, w3("ffn2_w_in"), name="ffn2_in", b_blocked=True, out_dtype=BF16, tm=1024, comm=[st])
    gathered(ks, got)
    st, ks = gather(fwd=late)
    h3, got = _swiglu_fwd(z3, "ffn2_act", comm=[st])
    gathered(ks, got)
    y3 = _mm(h3, w2("ffn2_w_out"), name="ffn2_out", tm=1024, tn=1024, tk=2816)
    r3, x3b = _ln_fwd(r2, y3, lng[2], lnb[2], lng[1], lnb[1], 0.5, "ln2")
    gp = _mm(x3b, w2("ple_w_gate"), name="ple_gate", tn=1024)
    dr4, dgp, dpp, dg3, db3, sq = _tail(r3, lng[2], lnb[2], gp, pp, lng[3], lnb[3], target, "tail")

    grads, sends, owns, blocks, outs = {}, {}, {}, {}, {}

    def pair_exchange(*ks):
        return _rs_pair_exchange([grads[k] for k in ks], [meta[k] for k in ks], [rows[k] for k in ks])

    def pair_add(ks, got):
        for k, g_ in zip(ks, got):
            sends[k], owns[k] = _rs_pair_add(grads[k], g_, meta[k], rows[k], sp, "rs_pair_add_" + k)

    def chip_exchange(*ks):
        return _rs_chip_exchange([sends[k] for k in ks])

    def chip_add(ks, got):
        for k, g_ in zip(ks, got):
            blocks[k] = _rs_chip_add(owns[k], g_, sp, "rs_chip_add_" + k)

    def pair_share(*ks):
        return _rs_pair_share([blocks[k] for k in ks])

    def update(ks, full):
        for k, g_ in zip(ks, full):
            m_, v_ = moments[k]
            if k in pad_cols:
                outs[k] = [swap(a) for a in _adamw_t(swap(big[k]), g_, swap(m_), swap(v_), "adamw_" + k)]
            else:
                outs[k] = _adamw(big[k], g_, m_, v_, "adamw_" + k)

    ple = ("ple_w_gate", "ple_w_proj")
    mixo = ("mix_w_out", "branch_w_conv", "branch_w_hgrn")
    dx3m = _mm(dgp, w2("ple_w_gate"), name="d_ple_gate_x", tb=True, tn=1024, tk=2048)
    grads["ple_w_gate"] = _mm(x3b, dgp, name="d_ple_gate_w", ta=True, out_dtype=BF16, tm=1024, tk=2048, tn=1024).reshape(N_CHIPS, -1, d)
    grads["ple_w_proj"] = _mm(pe, dpp, name="d_ple_proj_w", ta=True, out_dtype=BF16, out_blocked=N_CHIPS, tk=2048, tn=512)
    dr3, dy3b, dg2, db2 = _ln_bwd(dr4, dx3m, r3, lng[2], 0.5, "ln2_bwd")
    late_w = ple + ("ffn2_w_out",)
    dh3 = _mm(dy3b, w2("ffn2_w_out"), name="d_ffn2_out_x", tb=True, out_dtype=BF16, tn=1408, tk=2048)
    grads["ffn2_w_out"] = _mm(h3, dy3b, name="d_ffn2_out_w", ta=True, out_dtype=BF16, tm=1408, tk=2048, tn=1024).reshape(2, n_pad, d)
    dz3 = _swiglu_bwd(dh3, z3, "ffn2_act_bwd")
    dx2m, got = _mm(dz3, w3("ffn2_w_in"), name="d_ffn2_in_x", tb=True, b_blocked=True, tm=1024, tn=1024, tk=2816,
                    comm=[pair_exchange(*late_w)])
    pair_add(late_w, got)
    grads["ffn2_w_in"], got = _mm(x2b, dz3, name="d_ffn2_in_w", ta=True, out_dtype=BF16, out_blocked=N_CHIPS, tk=4096, comm=[chip_exchange(*late_w)])
    chip_add(late_w, got)
    dr2, dy2b, dg1, db1 = _ln_bwd(dr3, dx2m, r2, lng[1], 1.0, "ln1_bwd")
    dmer, got = _mm(dy2b, w2("mix_w_out"), name="d_mix_out_x", tb=True, tn=1024, tk=2048, comm=[pair_exchange("ffn2_w_in")])
    pair_add(["ffn2_w_in"], got)
    g_, full = _mm(merged, dy2b, name="d_mix_out_w", ta=True, out_dtype=BF16, tm=1024, tk=2048, tn=1024, comm=[pair_share(*late_w)])
    grads["mix_w_out"] = g_.reshape(N_CHIPS, -1, d)
    update(late_w, full)
    dma, dmb, dgc, dgh = _merge_bwd(dmer, z, ma, mb, w, "merge_bwd")
    dya = _mm(dma, w3("branch_w_conv"), name="d_branch_conv_x", tb=True, b_blocked=True, tn=1024, tk=512)
    dyb = _mm(dmb, w3("branch_w_hgrn"), name="d_branch_hgrn_x", tb=True, b_blocked=True, tn=1024, tk=512)
    grads["branch_w_conv"] = _mm(ya, dma, name="d_branch_conv_w", ta=True, out_dtype=BF16, out_blocked=N_CHIPS, tm=1024, tk=2048, tn=512)
    grads["branch_w_hgrn"] = _mm(yb, dmb, name="d_branch_hgrn_w", ta=True, out_dtype=BF16, out_blocked=N_CHIPS, tm=1024, tk=2048, tn=512)
    dbg, dcg, dhc, dcw = _conv_bwd(dya, z, cw, w, "conv_bwd")
    (dq_, df_, di_, dgr_, dhg, dnw), got2, got = _hgrn_bwd(dyb, z, o_h, states, hg, nw_, w, "hgrn_bwd",
                                                            comm=[chip_exchange("ffn2_w_in"), pair_exchange(*mixo)])
    chip_add(["ffn2_w_in"], got2)
    pair_add(mixo, got)
    dz = _concat_cols([dbg, dcg, dhc, dq_, df_, di_, dgr_, dgc, dgh], "dz_concat")
    dx1m, full, got = _mm(dz, w3("mix_w_in"), name="d_mix_in_x", tb=True, b_blocked=True, tm=1024, tn=1024, tk=2816,
                          comm=[pair_share("ffn2_w_in"), chip_exchange(*mixo)])
    update(["ffn2_w_in"], full)
    chip_add(mixo, got)
    grads["mix_w_in"], full = _mm(x1b, dz, name="d_mix_in_w", ta=True, out_dtype=BF16, out_blocked=N_CHIPS, tk=4096, comm=[pair_share(*mixo)])
    update(mixo, full)
    dr1, dy1b, dg0, db0 = _ln_bwd(dr2, dx1m, r1, lng[0], 0.5, "ln0_bwd")
    dh1, got = _mm(dy1b, w2("ffn1_w_out"), name="d_ffn1_out_x", tb=True, out_dtype=BF16, tn=1408, tk=2048,
                   comm=[pair_exchange("mix_w_in")])
    pair_add(["mix_w_in"], got)
    mix_sends = [sends["mix_w_in"]]
    g_, got_a = _mm(h1, dy1b, name="d_ffn1_out_w", ta=True, out_dtype=BF16, tm=1408, tk=2048, tn=1024, comm=[_rs_chip_exchange(mix_sends, (0, 2))])
    grads["ffn1_w_out"] = g_.reshape(2, n_pad, d)
    dz1 = _swiglu_bwd(dh1, z1, "ffn1_act_bwd")
    g_other, got2, got = _mm(x0b, dz1, name="d_ffn1_in_w_other", ta=True, out_dtype=BF16, out_blocked=N_CHIPS, tk=4096, half=(sp, True),
                             comm=[_rs_chip_exchange(mix_sends, (1, 2), got_a), pair_exchange("ffn1_w_out")])
    chip_add(["mix_w_in"], got2)
    pair_add(["ffn1_w_out"], got)
    grads["ffn1_w_in"], full, got2, got = _mm(
        x0b, dz1, name="d_ffn1_in_w_own", ta=True, out_dtype=BF16, out_blocked=N_CHIPS, tk=4096, half=(sp, False),
        comm=[pair_share("mix_w_in"), chip_exchange("ffn1_w_out"),
              _rs_pair_exchange([g_other], [meta["ffn1_w_in"]], [rows["ffn1_w_in"]])])
    update(["mix_w_in"], full)
    chip_add(["ffn1_w_out"], got2)
    pair_add(["ffn1_w_in"], got)
    dx0, got2, full = _mm(dz1, w3("ffn1_w_in"), name="d_ffn1_in_x", tb=True, b_blocked=True, tm=1024, tn=1024, tk=2816,
                          add=(dr1, ALPHA), comm=[chip_exchange("ffn1_w_in"), pair_share("ffn1_w_out")])
    chip_add(["ffn1_w_in"], got2)
    update(["ffn1_w_out"], full)
    grad_x = dx0.reshape(x.shape)
    update(["ffn1_w_in"], _run_stages([pair_share("ffn1_w_in")], "rs_tail_pair")[0])

    pack = jnp.concatenate([
        dg0, dg1, dg2, dg3, db0, db1, db2, db3,
        jnp.pad(dcw, ((0, 0), (0, d - w))), jnp.pad(dhg, ((0, 0), (0, d - w))),
        jnp.pad(jnp.sum(dnw.reshape(-1, HEAD), axis=0, keepdims=True), ((0, 0), (0, d - HEAD))), sq], axis=0)
    pack = _all_reduce_small(jnp.pad(pack, ((0, 1), (0, 0))), "reduce_small")
    loss = (0.5 / d) * jnp.sum(pack[14])
    g_ln_g = lax.dynamic_slice_in_dim(pack[0:4], chip * dq, dq, axis=1)
    g_ln_b = lax.dynamic_slice_in_dim(pack[4:8], chip * dq, dq, axis=1)
    g_conv = lax.dynamic_slice_in_dim(pack[8:11, :w], chip * wq, wq, axis=1)
    g_hg = pack[11:13, :w]
    g_nw = pack[13:14, :HEAD]

    small_w = dict(ln_g=(ln_g, g_ln_g, m_ln_g, v_ln_g), ln_b=(ln_b, g_ln_b, m_ln_b, v_ln_b),
                   conv_w=(conv_w, g_conv, m_conv_w, v_conv_w), hg_lower_bound=(hg_lower_bound, g_hg, m_hg_lower_bound, v_hg_lower_bound),
                   hg_norm_w=(hg_norm_w, g_nw, m_hg_norm_w, v_hg_norm_w))
    for k, (w_, g_, m_, v_) in small_w.items():
        outs[k] = _adamw(w_, g_.reshape(-1, w_.shape[-1]), m_, v_, "adamw_" + k)

    order = ["ln_g", "ln_b", "ffn1_w_in", "ffn1_w_out", "mix_w_in", "conv_w", "hg_lower_bound", "hg_norm_w", "branch_w_conv",
             "branch_w_hgrn", "mix_w_out", "ffn2_w_in", "ffn2_w_out", "ple_w_gate", "ple_w_proj"]
    return (loss, grad_x, *[outs[k][0] for k in order], *[outs[k][1] for k in order], *[outs[k][2] for k in order],
            *[outs[k][3] for k in order])
```
